```python
import jax
import jax.numpy as jnp
from jax import lax
import numpy as np

D_MODEL = 1024
BATCH = 8
SEQ = 8192
DEPTH = 1

HEAD_DIM = 64
N_HEADS = D_MODEL // HEAD_DIM
N_HEADS_SB = N_HEADS // 2
N_HEADS_DIL = N_HEADS - N_HEADS_SB
D_SB = N_HEADS_SB * HEAD_DIM
D_DIL = N_HEADS_DIL * HEAD_DIM
D_IN = 3 * D_SB + 3 * D_DIL
D_FF = 2816
DILATED_PATTERNS = ((128, 1), (512, 4), (2048, 16))
BLOCK = 128
ROPE_THETA = 10000.0
RMS_EPS = 1e-6
HALF_STEP = 0.5

kernel_name = 'hybrid_stickbreak_dilated_macaron'


def rmsnorm(x, gain):
    xf = x.astype(jnp.float32)
    y = xf * lax.rsqrt(jnp.mean(xf * xf, axis=-1, keepdims=True) + RMS_EPS)
    return (y * gain.astype(jnp.float32)).astype(x.dtype)


def swiglu(x, w_gate, w_up, w_down):
    return (jax.nn.silu(x @ w_gate) * (x @ w_up)) @ w_down


def rotary(t, positions):
    half = t.shape[-1] // 2
    inv_freq = ROPE_THETA ** (-jnp.arange(half, dtype=jnp.float32) / half)
    ang = positions.astype(jnp.float32)[:, None] * inv_freq[None, :]
    cos, sin = jnp.cos(ang), jnp.sin(ang)
    tf = t.astype(jnp.float32)
    t1, t2 = tf[..., :half], tf[..., half:]
    return jnp.concatenate([t1 * cos - t2 * sin, t2 * cos + t1 * sin], axis=-1).astype(t.dtype)


def split_heads(t, n_heads):
    b, s, _ = t.shape
    return t.reshape(b, s, n_heads, HEAD_DIM).transpose(0, 2, 1, 3)


def merge_heads(t):
    b, h, s, d = t.shape
    return t.transpose(0, 2, 1, 3).reshape(b, s, h * d)


def stick_breaking_attention(q, k, v):
    b, h, s, d = q.shape
    nb = s // BLOCK
    scale = d ** -0.5
    qb = q.reshape(b, h, nb, BLOCK, d).transpose(2, 0, 1, 3, 4)
    key_pos = jnp.arange(s)

    def one_block(args):
        q_blk, blk = args
        q_pos = blk * BLOCK + jnp.arange(BLOCK)
        z = jnp.einsum('bhqd,bhkd->bhqk', q_blk, k, preferred_element_type=jnp.float32) * scale
        mask = key_pos[None, :] < q_pos[:, None]
        log_beta = jax.nn.log_sigmoid(z)
        log_stay = jnp.where(mask, jax.nn.log_sigmoid(-z), 0.0)
        later = lax.cumsum(log_stay, axis=3, reverse=True) - log_stay
        weights = jnp.where(mask, jnp.exp(log_beta + later), 0.0)
        return jnp.einsum('bhqk,bhkd->bhqd', weights.astype(v.dtype), v)

    out = lax.map(one_block, (qb, jnp.arange(nb)))
    return out.transpose(1, 2, 0, 3, 4).reshape(b, h, s, d)


def _dilated_pattern(q, k, v, window, dilation):
    b, h, s, d = q.shape
    span = window // dilation
    n_comp = s // dilation
    nb = n_comp // BLOCK

    def to_blocks(t):
        t = t.reshape(b, h, n_comp, dilation, d).transpose(0, 1, 3, 2, 4)
        return t.reshape(b, h, dilation, nb, BLOCK, d)

    def with_previous(t):
        prev = jnp.pad(t, ((0, 0), (0, 0), (0, 0), (1, 0), (0, 0), (0, 0)))[:, :, :, :-1]
        return jnp.concatenate([prev, t], axis=4)

    qb = to_blocks(q)
    kb = with_previous(to_blocks(k))
    vb = with_previous(to_blocks(v))
    z = jnp.einsum('bhcnqd,bhcnkd->bhcnqk', qb, kb, preferred_element_type=jnp.float32) * (d ** -0.5)
    q_idx = jnp.arange(BLOCK)[:, None] + BLOCK
    k_idx = jnp.arange(2 * BLOCK)[None, :]
    dist = q_idx - k_idx
    band = (dist >= 0) & (dist <= span)
    has_prev = (jnp.arange(nb) > 0)[:, None, None] | (k_idx >= BLOCK)[None]
    valid = band[None] & has_prev
    z = jnp.where(valid, z, -jnp.inf)
    m = jnp.max(z, axis=-1, keepdims=True)
    p = jnp.exp(z - m)
    denom = jnp.sum(p, axis=-1, keepdims=True)
    o = jnp.einsum('bhcnqk,bhcnkd->bhcnqd', p, vb.astype(jnp.float32)) / denom
    lse = m + jnp.log(denom)

    def from_blocks(t):
        e = t.shape[-1]
        t = t.reshape(b, h, dilation, n_comp, e).transpose(0, 1, 3, 2, 4)
        return t.reshape(b, h, s, e)

    return from_blocks(o), from_blocks(lse)


def dilated_mixture_attention(q, k, v):
    s = q.shape[2]
    outs, lses = [], []
    for window, dilation in DILATED_PATTERNS:
        unit = BLOCK * dilation
        s_pad = -(-s // unit) * unit
        pad = ((0, 0), (0, 0), (0, s_pad - s), (0, 0))
        o, lse = _dilated_pattern(jnp.pad(q, pad), jnp.pad(k, pad), jnp.pad(v, pad), window, dilation)
        outs.append(o[:, :, :s])
        lses.append(lse[:, :, :s])
    alpha = jax.nn.softmax(jnp.stack(lses), axis=0)
    return jnp.sum(alpha * jnp.stack(outs), axis=0).astype(q.dtype)


def token_mixer(h, w_in, sb_out_norm, dil_out_norm, w_out):
    seq = h.shape[1]
    proj = h @ w_in
    cuts = [D_SB, 2 * D_SB, 3 * D_SB, 3 * D_SB + D_DIL, 3 * D_SB + 2 * D_DIL]
    q_sb, k_sb, v_sb, q_dl, k_dl, v_dl = jnp.split(proj, cuts, axis=-1)
    positions = jnp.arange(seq)
    o_sb = stick_breaking_attention(split_heads(q_sb, N_HEADS_SB), split_heads(k_sb, N_HEADS_SB),
                                    split_heads(v_sb, N_HEADS_SB))
    o_dl = dilated_mixture_attention(rotary(split_heads(q_dl, N_HEADS_DIL), positions),
                                     rotary(split_heads(k_dl, N_HEADS_DIL), positions),
                                     split_heads(v_dl, N_HEADS_DIL))
    merged = jnp.concatenate([rmsnorm(merge_heads(o_sb), sb_out_norm),
                              rmsnorm(merge_heads(o_dl), dil_out_norm)], axis=-1)
    return merged @ w_out


def _fwd_setup_inputs(seed: int = 0) -> dict:
    key = jax.random.key(seed)
    ks = jax.random.split(key, 16)

    def normal(k, shape, scale):
        return jax.random.normal(k, shape, jnp.float32) * scale

    def gain(k, shape):
        return 1.0 + 0.02 * jax.random.normal(k, shape, jnp.float32)

    dm, df = D_MODEL ** -0.5, D_FF ** -0.5
    return {
        'x': normal(ks[0], (BATCH, SEQ, D_MODEL), 1.0),
        'ffn1_norm': gain(ks[1], (DEPTH, D_MODEL)),
        'ffn1_w_gate': normal(ks[2], (DEPTH, D_MODEL, D_FF), dm),
        'ffn1_w_up': normal(ks[3], (DEPTH, D_MODEL, D_FF), dm),
        'ffn1_w_down': normal(ks[4], (DEPTH, D_FF, D_MODEL), df),
        'mix_norm': gain(ks[5], (DEPTH, D_MODEL)),
        'w_in': normal(ks[6], (DEPTH, D_MODEL, D_IN), dm),
        'sb_out_norm': gain(ks[7], (DEPTH, D_SB)),
        'dil_out_norm': gain(ks[8], (DEPTH, D_DIL)),
        'w_out': normal(ks[9], (DEPTH, D_MODEL, D_MODEL), dm),
        'ffn2_norm': gain(ks[10], (DEPTH, D_MODEL)),
        'ffn2_w_gate': normal(ks[11], (DEPTH, D_MODEL, D_FF), dm),
        'ffn2_w_up': normal(ks[12], (DEPTH, D_MODEL, D_FF), dm),
        'ffn2_w_down': normal(ks[13], (DEPTH, D_FF, D_MODEL), df),
        'final_norm': gain(ks[14], (D_MODEL,)),
    }


def _fwd_reference(x, ffn1_norm, ffn1_w_gate, ffn1_w_up, ffn1_w_down, mix_norm, w_in, sb_out_norm,
              dil_out_norm, w_out, ffn2_norm, ffn2_w_gate, ffn2_w_up, ffn2_w_down, final_norm):
    for layer in range(DEPTH):
        x = x + HALF_STEP * swiglu(rmsnorm(x, ffn1_norm[layer]), ffn1_w_gate[layer],
                                   ffn1_w_up[layer], ffn1_w_down[layer])
        x = x + token_mixer(rmsnorm(x, mix_norm[layer]), w_in[layer], sb_out_norm[layer],
                            dil_out_norm[layer], w_out[layer])
        x = x + HALF_STEP * swiglu(rmsnorm(x, ffn2_norm[layer]), ffn2_w_gate[layer],
                                   ffn2_w_up[layer], ffn2_w_down[layer])
    return rmsnorm(x, final_norm)


import jax as _jax
import jax.numpy as _jnp

TWIN_FORMAT = 'train_step'
FWD_PARAMS = ['x', 'ffn1_norm', 'ffn1_w_gate', 'ffn1_w_up', 'ffn1_w_down', 'mix_norm', 'w_in', 'sb_out_norm', 'dil_out_norm', 'w_out', 'ffn2_norm', 'ffn2_w_gate', 'ffn2_w_up', 'ffn2_w_down', 'final_norm']
TWIN_WEIGHTS = ['ffn1_norm', 'ffn1_w_gate', 'ffn1_w_up', 'ffn1_w_down', 'mix_norm', 'w_in', 'sb_out_norm', 'dil_out_norm', 'w_out', 'ffn2_norm', 'ffn2_w_gate', 'ffn2_w_up', 'ffn2_w_down', 'final_norm']
TWIN_DIFF_INPUT = 'x'
TWIN_INPUTS = ['x', 'ffn1_norm', 'ffn1_w_gate', 'ffn1_w_up', 'ffn1_w_down', 'mix_norm', 'w_in', 'sb_out_norm', 'dil_out_norm', 'w_out', 'ffn2_norm', 'ffn2_w_gate', 'ffn2_w_up', 'ffn2_w_down', 'final_norm', 'loss_target', 'm_ffn1_norm', 'm_ffn1_w_gate', 'm_ffn1_w_up', 'm_ffn1_w_down', 'm_mix_norm', 'm_w_in', 'm_sb_out_norm', 'm_dil_out_norm', 'm_w_out', 'm_ffn2_norm', 'm_ffn2_w_gate', 'm_ffn2_w_up', 'm_ffn2_w_down', 'm_final_norm', 'v_ffn1_norm', 'v_ffn1_w_gate', 'v_ffn1_w_up', 'v_ffn1_w_down', 'v_mix_norm', 'v_w_in', 'v_sb_out_norm', 'v_dil_out_norm', 'v_w_out', 'v_ffn2_norm', 'v_ffn2_w_gate', 'v_ffn2_w_up', 'v_ffn2_w_down', 'v_final_norm']
TWIN_OUTPUTS = ['loss', 'grad_x', 'grad_ffn1_norm', 'grad_ffn1_w_gate', 'grad_ffn1_w_up', 'grad_ffn1_w_down', 'grad_mix_norm', 'grad_w_in', 'grad_sb_out_norm', 'grad_dil_out_norm', 'grad_w_out', 'grad_ffn2_norm', 'grad_ffn2_w_gate', 'grad_ffn2_w_up', 'grad_ffn2_w_down', 'grad_final_norm', 'delta_ffn1_norm', 'delta_ffn1_w_gate', 'delta_ffn1_w_up', 'delta_ffn1_w_down', 'delta_mix_norm', 'delta_w_in', 'delta_sb_out_norm', 'delta_dil_out_norm', 'delta_w_out', 'delta_ffn2_norm', 'delta_ffn2_w_gate', 'delta_ffn2_w_up', 'delta_ffn2_w_down', 'delta_final_norm', 'new_m_ffn1_norm', 'new_m_ffn1_w_gate', 'new_m_ffn1_w_up', 'new_m_ffn1_w_down', 'new_m_mix_norm', 'new_m_w_in', 'new_m_sb_out_norm', 'new_m_dil_out_norm', 'new_m_w_out', 'new_m_ffn2_norm', 'new_m_ffn2_w_gate', 'new_m_ffn2_w_up', 'new_m_ffn2_w_down', 'new_m_final_norm', 'new_v_ffn1_norm', 'new_v_ffn1_w_gate', 'new_v_ffn1_w_up', 'new_v_ffn1_w_down', 'new_v_mix_norm', 'new_v_w_in', 'new_v_sb_out_norm', 'new_v_dil_out_norm', 'new_v_w_out', 'new_v_ffn2_norm', 'new_v_ffn2_w_gate', 'new_v_ffn2_w_up', 'new_v_ffn2_w_down', 'new_v_final_norm']
TWIN_LEAF_KINDS = {'loss': 'loss', 'grad_x': 'grad_x', 'grad_ffn1_norm': 'grad_w', 'grad_ffn1_w_gate': 'grad_w', 'grad_ffn1_w_up': 'grad_w', 'grad_ffn1_w_down': 'grad_w', 'grad_mix_norm': 'grad_w', 'grad_w_in': 'grad_w', 'grad_sb_out_norm': 'grad_w', 'grad_dil_out_norm': 'grad_w', 'grad_w_out': 'grad_w', 'grad_ffn2_norm': 'grad_w', 'grad_ffn2_w_gate': 'grad_w', 'grad_ffn2_w_up': 'grad_w', 'grad_ffn2_w_down': 'grad_w', 'grad_final_norm': 'grad_w', 'delta_ffn1_norm': 'delta_w', 'delta_ffn1_w_gate': 'delta_w', 'delta_ffn1_w_up': 'delta_w', 'delta_ffn1_w_down': 'delta_w', 'delta_mix_norm': 'delta_w', 'delta_w_in': 'delta_w', 'delta_sb_out_norm': 'delta_w', 'delta_dil_out_norm': 'delta_w', 'delta_w_out': 'delta_w', 'delta_ffn2_norm': 'delta_w', 'delta_ffn2_w_gate': 'delta_w', 'delta_ffn2_w_up': 'delta_w', 'delta_ffn2_w_down': 'delta_w', 'delta_final_norm': 'delta_w', 'new_m_ffn1_norm': 'new_m', 'new_m_ffn1_w_gate': 'new_m', 'new_m_ffn1_w_up': 'new_m', 'new_m_ffn1_w_down': 'new_m', 'new_m_mix_norm': 'new_m', 'new_m_w_in': 'new_m', 'new_m_sb_out_norm': 'new_m', 'new_m_dil_out_norm': 'new_m', 'new_m_w_out': 'new_m', 'new_m_ffn2_norm': 'new_m', 'new_m_ffn2_w_gate': 'new_m', 'new_m_ffn2_w_up': 'new_m', 'new_m_ffn2_w_down': 'new_m', 'new_m_final_norm': 'new_m', 'new_v_ffn1_norm': 'new_v', 'new_v_ffn1_w_gate': 'new_v', 'new_v_ffn1_w_up': 'new_v', 'new_v_ffn1_w_down': 'new_v', 'new_v_mix_norm': 'new_v', 'new_v_w_in': 'new_v', 'new_v_sb_out_norm': 'new_v', 'new_v_dil_out_norm': 'new_v', 'new_v_w_out': 'new_v', 'new_v_ffn2_norm': 'new_v', 'new_v_ffn2_w_gate': 'new_v', 'new_v_ffn2_w_up': 'new_v', 'new_v_ffn2_w_down': 'new_v', 'new_v_final_norm': 'new_v'}


def _forward(args):
    return _fwd_reference(*[args[k] for k in FWD_PARAMS])


def _output_shape():
    def fwd():
        inp = _fwd_setup_inputs(0)
        return _fwd_reference(*[inp[k] for k in FWD_PARAMS])
    out = _jax.eval_shape(fwd)
    return out.shape, out.dtype

N_MICROBATCH = 1
ADAM_LR = 0.001
ADAM_B1 = 0.9
ADAM_B2 = 0.999
ADAM_EPS = 1e-08
ADAM_WD = 0.01
ADAM_STEP = 10
PER_EXAMPLE_BATCH_AXIS = {'x': 0, 'loss_target': 0}
SHARED_INPUTS = []
_WEIGHT_DTYPES = {'ffn1_norm': _jnp.float32, 'ffn1_w_gate': _jnp.float32, 'ffn1_w_up': _jnp.float32, 'ffn1_w_down': _jnp.float32, 'mix_norm': _jnp.float32, 'w_in': _jnp.float32, 'sb_out_norm': _jnp.float32, 'dil_out_norm': _jnp.float32, 'w_out': _jnp.float32, 'ffn2_norm': _jnp.float32, 'ffn2_w_gate': _jnp.float32, 'ffn2_w_up': _jnp.float32, 'ffn2_w_down': _jnp.float32, 'final_norm': _jnp.float32}
MOMENT_SCALE = {'ffn1_norm': 1.413570e-01, 'ffn1_w_gate': 5.720736e-02, 'ffn1_w_up': 5.545985e-02, 'ffn1_w_down': 9.213298e-02, 'mix_norm': 2.844175e-01, 'w_in': 1.531361e-01, 'sb_out_norm': 1.730077e-01, 'dil_out_norm': 1.754550e-01, 'w_out': 1.785885e-01, 'ffn2_norm': 7.776903e-02, 'ffn2_w_gate': 3.157172e-02, 'ffn2_w_up': 3.061786e-02, 'ffn2_w_down': 5.059697e-02, 'final_norm': 6.398119e+01}


def _to_microbatches(a, axis):
    t = _jnp.moveaxis(a, axis, 0)
    t = t.reshape((N_MICROBATCH, t.shape[0] // N_MICROBATCH) + t.shape[1:])
    return _jnp.moveaxis(t, 1, axis + 1)


def setup_inputs(seed: int = 0) -> dict:
    inp = _fwd_setup_inputs(seed)
    key = _jax.random.fold_in(_jax.random.key(seed), 7919)
    shape, _ = _output_shape()
    out = dict(inp)
    out["loss_target"] = _jax.random.normal(_jax.random.fold_in(key, 0), shape, _jnp.float32)
    for i, name in enumerate(TWIN_WEIGHTS):
        w = inp[name].astype(_jnp.float32)
        if MOMENT_SCALE is None:
            s = _jnp.sqrt(_jnp.mean(_jnp.square(w)) + 1e-30)
        else:
            s = MOMENT_SCALE[name]
        km, kv = _jax.random.split(_jax.random.fold_in(key, i + 1))
        out[name] = w
        out["m_" + name] = s * _jax.random.normal(km, w.shape, _jnp.float32)
        out["v_" + name] = (s * s) * _jax.random.uniform(kv, w.shape, _jnp.float32, 0.5, 1.5)
    if N_MICROBATCH > 1:
        for name, axis in PER_EXAMPLE_BATCH_AXIS.items():
            out[name] = _to_microbatches(out[name], axis)
    return {'x': out['x'], 'ffn1_norm': out['ffn1_norm'], 'ffn1_w_gate': out['ffn1_w_gate'], 'ffn1_w_up': out['ffn1_w_up'], 'ffn1_w_down': out['ffn1_w_down'], 'mix_norm': out['mix_norm'], 'w_in': out['w_in'], 'sb_out_norm': out['sb_out_norm'], 'dil_out_norm': out['dil_out_norm'], 'w_out': out['w_out'], 'ffn2_norm': out['ffn2_norm'], 'ffn2_w_gate': out['ffn2_w_gate'], 'ffn2_w_up': out['ffn2_w_up'], 'ffn2_w_down': out['ffn2_w_down'], 'final_norm': out['final_norm'], 'loss_target': out['loss_target'], 'm_ffn1_norm': out['m_ffn1_norm'], 'm_ffn1_w_gate': out['m_ffn1_w_gate'], 'm_ffn1_w_up': out['m_ffn1_w_up'], 'm_ffn1_w_down': out['m_ffn1_w_down'], 'm_mix_norm': out['m_mix_norm'], 'm_w_in': out['m_w_in'], 'm_sb_out_norm': out['m_sb_out_norm'], 'm_dil_out_norm': out['m_dil_out_norm'], 'm_w_out': out['m_w_out'], 'm_ffn2_norm': out['m_ffn2_norm'], 'm_ffn2_w_gate': out['m_ffn2_w_gate'], 'm_ffn2_w_up': out['m_ffn2_w_up'], 'm_ffn2_w_down': out['m_ffn2_w_down'], 'm_final_norm': out['m_final_norm'], 'v_ffn1_norm': out['v_ffn1_norm'], 'v_ffn1_w_gate': out['v_ffn1_w_gate'], 'v_ffn1_w_up': out['v_ffn1_w_up'], 'v_ffn1_w_down': out['v_ffn1_w_down'], 'v_mix_norm': out['v_mix_norm'], 'v_w_in': out['v_w_in'], 'v_sb_out_norm': out['v_sb_out_norm'], 'v_dil_out_norm': out['v_dil_out_norm'], 'v_w_out': out['v_w_out'], 'v_ffn2_norm': out['v_ffn2_norm'], 'v_ffn2_w_gate': out['v_ffn2_w_gate'], 'v_ffn2_w_up': out['v_ffn2_w_up'], 'v_ffn2_w_down': out['v_ffn2_w_down'], 'v_final_norm': out['v_final_norm']}


def _loss(weights, diff, rest, loss_target):
    with _jax.named_scope("forward"):
        args = {**rest, TWIN_DIFF_INPUT: diff, **{k: w.astype(_WEIGHT_DTYPES[k]) for k, w in weights.items()}}
        y = _forward(args)
    with _jax.named_scope("loss_head"):
        err = _jnp.square(y.astype(_jnp.float32) - loss_target)
        return 0.5 * _jnp.sum(_jnp.mean(err, axis=-1)) if err.ndim else 0.5 * err


def _adamw(w, g, m, v):
    m = ADAM_B1 * m + (1.0 - ADAM_B1) * g
    v = ADAM_B2 * v + (1.0 - ADAM_B2) * _jnp.square(g)
    m_hat = m / (1.0 - ADAM_B1 ** ADAM_STEP)
    v_hat = v / (1.0 - ADAM_B2 ** ADAM_STEP)
    delta = -ADAM_LR * (m_hat / (_jnp.sqrt(v_hat) + ADAM_EPS) + ADAM_WD * w)
    return delta, m, v


def reference(x, ffn1_norm, ffn1_w_gate, ffn1_w_up, ffn1_w_down, mix_norm, w_in, sb_out_norm, dil_out_norm, w_out, ffn2_norm, ffn2_w_gate, ffn2_w_up, ffn2_w_down, final_norm, loss_target, m_ffn1_norm, m_ffn1_w_gate, m_ffn1_w_up, m_ffn1_w_down, m_mix_norm, m_w_in, m_sb_out_norm, m_dil_out_norm, m_w_out, m_ffn2_norm, m_ffn2_w_gate, m_ffn2_w_up, m_ffn2_w_down, m_final_norm, v_ffn1_norm, v_ffn1_w_gate, v_ffn1_w_up, v_ffn1_w_down, v_mix_norm, v_w_in, v_sb_out_norm, v_dil_out_norm, v_w_out, v_ffn2_norm, v_ffn2_w_gate, v_ffn2_w_up, v_ffn2_w_down, v_final_norm):
    given = dict(x=x, ffn1_norm=ffn1_norm, ffn1_w_gate=ffn1_w_gate, ffn1_w_up=ffn1_w_up, ffn1_w_down=ffn1_w_down, mix_norm=mix_norm, w_in=w_in, sb_out_norm=sb_out_norm, dil_out_norm=dil_out_norm, w_out=w_out, ffn2_norm=ffn2_norm, ffn2_w_gate=ffn2_w_gate, ffn2_w_up=ffn2_w_up, ffn2_w_down=ffn2_w_down, final_norm=final_norm, loss_target=loss_target, m_ffn1_norm=m_ffn1_norm, m_ffn1_w_gate=m_ffn1_w_gate, m_ffn1_w_up=m_ffn1_w_up, m_ffn1_w_down=m_ffn1_w_down, m_mix_norm=m_mix_norm, m_w_in=m_w_in, m_sb_out_norm=m_sb_out_norm, m_dil_out_norm=m_dil_out_norm, m_w_out=m_w_out, m_ffn2_norm=m_ffn2_norm, m_ffn2_w_gate=m_ffn2_w_gate, m_ffn2_w_up=m_ffn2_w_up, m_ffn2_w_down=m_ffn2_w_down, m_final_norm=m_final_norm, v_ffn1_norm=v_ffn1_norm, v_ffn1_w_gate=v_ffn1_w_gate, v_ffn1_w_up=v_ffn1_w_up, v_ffn1_w_down=v_ffn1_w_down, v_mix_norm=v_mix_norm, v_w_in=v_w_in, v_sb_out_norm=v_sb_out_norm, v_dil_out_norm=v_dil_out_norm, v_w_out=v_w_out, v_ffn2_norm=v_ffn2_norm, v_ffn2_w_gate=v_ffn2_w_gate, v_ffn2_w_up=v_ffn2_w_up, v_ffn2_w_down=v_ffn2_w_down, v_final_norm=v_final_norm)
    weights = {n: given[n] for n in TWIN_WEIGHTS}
    shared = {n: given[n] for n in SHARED_INPUTS}
    per_example = {n: given[n] for n in ['x']}
    grad_fn = _jax.value_and_grad(_loss, argnums=(0, 1))

    def one_microbatch(ex, loss_target):
        ex = dict(ex)
        diff = ex.pop(TWIN_DIFF_INPUT)
        return grad_fn(weights, diff, {**shared, **ex}, loss_target)

    if N_MICROBATCH == 1:
        loss, (grad_w, grad_x) = one_microbatch(per_example, given["loss_target"])
    else:
        def body(carry, xs):
            loss_sum, grad_sum = carry
            l_k, (gw_k, gx_k) = one_microbatch(xs[0], xs[1])
            with _jax.named_scope("update"):
                return (loss_sum + l_k, _jax.tree.map(_jnp.add, grad_sum, gw_k)), gx_k

        init = (_jnp.zeros((), _jnp.float32), _jax.tree.map(_jnp.zeros_like, weights))
        (loss, grad_w), grad_x = _jax.lax.scan(body, init, (per_example, given["loss_target"]))
    with _jax.named_scope("update"):
        delta_w, new_m, new_v = {}, {}, {}
        for n in TWIN_WEIGHTS:
            delta_w[n], new_m[n], new_v[n] = _adamw(weights[n], grad_w[n], given["m_" + n], given["v_" + n])
    return (loss, grad_x, *[grad_w[n] for n in TWIN_WEIGHTS], *[delta_w[n] for n in TWIN_WEIGHTS],
            *[new_m[n] for n in TWIN_WEIGHTS], *[new_v[n] for n in TWIN_WEIGHTS])
```

```python
import functools

import jax
import jax.numpy as jnp
from jax import lax
from jax.experimental import pallas as pl
from jax.experimental.pallas import tpu as pltpu

F32 = jnp.float32
BF16 = jnp.bfloat16

N_DEV = 8
HEAD_DIM = 64
LANES = 128
DILATED_PATTERNS = ((128, 1), (512, 4), (2048, 16))
DIL_BLOCK = 128
DIL_SUPER = 2048
SB_TILE = 256
ROPE_THETA = 10000.0
RMS_EPS = 1e-6
HALF_STEP = 0.5
ADAM_LR = 0.001
ADAM_B1 = 0.9
ADAM_B2 = 0.999
ADAM_EPS = 1e-08
ADAM_WD = 0.01
ADAM_STEP = 10
NEG_BIG = -1e30
VMEM_CAP_MB = 60


def _pick(n, prefs):
    for p in prefs:
        if n % p == 0:
            return p
    return n


def _cparams(sem=None, vmem_mb=48):
    return pltpu.CompilerParams(dimension_semantics=sem, vmem_limit_bytes=min(vmem_mb, VMEM_CAP_MB) * 1024 * 1024)


def _nbytes(shape, dtype):
    n = 1
    for s in shape:
        n *= s
    return n * jnp.dtype(dtype).itemsize


def _mm(a, b, *, name, ta=False, tb=False, outs=(F32,), res=None, alpha=1.0, extras=(), epilogue=None,
        tm=None, tn=None, tk=None):
    if ta:
        K, M = a.shape
    else:
        M, K = a.shape
    if tb:
        N, Kb = b.shape
    else:
        Kb, N = b.shape
    assert K == Kb, (a.shape, b.shape, ta, tb)
    tm = tm or _pick(M, (512, 256, 128))
    tn = tn or _pick(N, (512, 384, 256, 128))
    tk = tk or (K if K <= 3072 else _pick(K, (1024, 512, 256, 128)))
    nk = K // tk
    a_spec = pl.BlockSpec((tk, tm), lambda i, j, k: (k, i)) if ta else pl.BlockSpec((tm, tk), lambda i, j, k: (i, k))
    b_spec = pl.BlockSpec((tn, tk), lambda i, j, k: (j, k)) if tb else pl.BlockSpec((tk, tn), lambda i, j, k: (k, j))
    mn_spec = pl.BlockSpec((tm, tn), lambda i, j, k: (i, j))
    dims = (((0 if ta else 1,), (1 if tb else 0,)), ((), ()))
    n_extra = len(extras) + (1 if res is not None else 0)
    n_out = len(outs)

    def body(*refs):
        a_ref, b_ref = refs[0], refs[1]
        in_refs = refs[2:2 + n_extra]
        out_refs = refs[2 + n_extra:2 + n_extra + n_out]
        acc_ref = refs[2 + n_extra + n_out]
        k = pl.program_id(2)

        @pl.when(k == 0)
        def _():
            acc_ref[...] = jnp.zeros_like(acc_ref)

        acc_ref[...] += lax.dot_general(a_ref[...].astype(BF16), b_ref[...].astype(BF16), dims,
                                        preferred_element_type=F32)

        @pl.when(k == nk - 1)
        def _():
            acc = acc_ref[...]
            blocks = [r[...] for r in in_refs]
            if res is not None:
                r_blk, blocks = blocks[0], blocks[1:]
            else:
                r_blk = None
            if epilogue is None:
                val = acc * alpha
                if r_blk is not None:
                    val = val + r_blk
                vals = (val,)
            else:
                vals = epilogue(acc, r_blk, *blocks)
            for o_ref, v in zip(out_refs, vals):
                o_ref[...] = v.astype(o_ref.dtype)

    operands = [a, b] + ([res] if res is not None else []) + list(extras)
    in_specs = [a_spec, b_spec] + [mn_spec] * n_extra
    est = 2 * (_nbytes((tm, tk), a.dtype) + _nbytes((tk, tn), b.dtype))
    est += 2 * sum(_nbytes((tm, tn), o.dtype) for o in operands[2:])
    est += 2 * sum(_nbytes((tm, tn), d) for d in outs) + _nbytes((tm, tn), F32)
    result = pl.pallas_call(
        body, name=name, grid=(M // tm, N // tn, nk),
        out_shape=[jax.ShapeDtypeStruct((M, N), d) for d in outs],
        in_specs=in_specs, out_specs=[mn_spec] * n_out,
        scratch_shapes=[pltpu.VMEM((tm, tn), F32)],
        compiler_params=_cparams(("parallel", "parallel", "arbitrary"), vmem_mb=max(32, 2 * est // (1024 * 1024))),
    )(*operands)
    return result[0] if n_out == 1 else result


def _rms_hat(x):
    r = lax.rsqrt(jnp.mean(x * x, axis=-1, keepdims=True) + RMS_EPS)
    return x * r, r


def _rms_fwd(xs, gains, *, name):
    S = xs[0].shape[0]
    widths = [x.shape[1] for x in xs]
    tm = _pick(S, (512, 256, 128))
    n = len(xs)

    def body(*refs):
        o_ref = refs[2 * n]
        off = 0
        for i in range(n):
            xh, _ = _rms_hat(refs[i][...])
            o_ref[:, off:off + widths[i]] = (xh * refs[n + i][...]).astype(o_ref.dtype)
            off += widths[i]

    return pl.pallas_call(
        body, name=name, grid=(S // tm,),
        out_shape=jax.ShapeDtypeStruct((S, sum(widths)), BF16),
        in_specs=[pl.BlockSpec((tm, w), lambda i: (i, 0)) for w in widths]
        + [pl.BlockSpec((1, w), lambda i: (0, 0)) for w in widths],
        out_specs=pl.BlockSpec((tm, sum(widths)), lambda i: (i, 0)),
        compiler_params=_cparams(("parallel",)),
    )(*xs, *gains)


def _rms_bwd(dh, xs, gains, res, *, name):
    S = xs[0].shape[0]
    widths = [x.shape[1] for x in xs]
    tm = _pick(S, (512, 256, 128))
    n = len(xs)
    has_res = res is not None

    def body(*refs):
        dh_ref = refs[0]
        x_refs = refs[1:1 + n]
        g_refs = refs[1 + n:1 + 2 * n]
        r_ref = refs[1 + 2 * n] if has_res else None
        base = 1 + 2 * n + (1 if has_res else 0)
        dx_refs = refs[base:base + n]
        dg_refs = refs[base + n:base + 2 * n]
        first = pl.program_id(0) == 0
        off = 0
        for i in range(n):
            x = x_refs[i][...]
            xh, r = _rms_hat(x)
            d = dh_ref[:, off:off + widths[i]]
            dxh = d * g_refs[i][...]
            dx = r * (dxh - xh * jnp.mean(dxh * xh, axis=-1, keepdims=True))
            if has_res:
                dx = dx + r_ref[...]
            dx_refs[i][...] = dx
            part = jnp.sum(d * xh, axis=0, keepdims=True)

            @pl.when(first)
            def _(i=i, part=part):
                dg_refs[i][...] = part

            @pl.when(jnp.logical_not(first))
            def _(i=i, part=part):
                dg_refs[i][...] += part

            off += widths[i]

    in_specs = [pl.BlockSpec((tm, sum(widths)), lambda i: (i, 0))]
    in_specs += [pl.BlockSpec((tm, w), lambda i: (i, 0)) for w in widths]
    in_specs += [pl.BlockSpec((1, w), lambda i: (0, 0)) for w in widths]
    operands = [dh, *xs, *gains]
    if has_res:
        in_specs.append(pl.BlockSpec((tm, widths[0]), lambda i: (i, 0)))
        operands.append(res)
    out = pl.pallas_call(
        body, name=name, grid=(S // tm,),
        out_shape=[jax.ShapeDtypeStruct((S, w), F32) for w in widths] + [jax.ShapeDtypeStruct((1, w), F32) for w in widths],
        in_specs=in_specs,
        out_specs=[pl.BlockSpec((tm, w), lambda i: (i, 0)) for w in widths]
        + [pl.BlockSpec((1, w), lambda i: (0, 0)) for w in widths],
        compiler_params=_cparams(("arbitrary",)),
    )(*operands)
    return out[:n], out[n:]


def _sigmoid(g):
    return 1.0 / (1.0 + jnp.exp(-g))


def _ffn_fwd(x, gain, wg, wu, wd, *, tag):
    h = _rms_fwd([x], [gain], name=f"{tag}_norm")
    g = _mm(h, wg, outs=(BF16,), name=f"{tag}_gate")

    def act(acc, _, g_blk):
        gf = g_blk.astype(F32)
        return acc, gf * _sigmoid(gf) * acc

    u, a = _mm(h, wu, outs=(BF16, BF16), extras=(g,), epilogue=act, name=f"{tag}_up_act")
    y = _mm(a, wd, res=x, alpha=HALF_STEP, name=f"{tag}_down")
    return y, (h, g, u, a)


def _ffn_bwd(dout, x, gain, wg, wu, wd, saved, *, tag):
    h, g, u, a = saved

    def act_bwd(acc, _, g_blk, u_blk):
        gf, uf = g_blk.astype(F32), u_blk.astype(F32)
        da = acc * HALF_STEP
        sig = _sigmoid(gf)
        silu = gf * sig
        return da * uf * (sig * (1.0 + gf * (1.0 - sig))), da * silu

    dg, du = _mm(dout, wd, tb=True, outs=(BF16, BF16), extras=(g, u), epilogue=act_bwd, name=f"{tag}_bwd_act")
    dwg = _mm(h, dg, ta=True, name=f"{tag}_dwg")
    dwu = _mm(h, du, ta=True, name=f"{tag}_dwu")
    dwd = _mm(a, dout, ta=True, alpha=HALF_STEP, name=f"{tag}_dwd")
    dh = _mm(dg, wg, tb=True, name=f"{tag}_dh_gate")
    dh = _mm(du, wu, tb=True, res=dh, name=f"{tag}_dh_up")
    (dx,), (dgain,) = _rms_bwd(dh, [x], [gain], dout, name=f"{tag}_norm_bwd")
    return dx, dgain, dwg, dwu, dwd


def _rope_tables(S):
    half = HEAD_DIM // 2
    inv_freq = ROPE_THETA ** (-jnp.arange(half, dtype=F32) / half)
    ang = jnp.arange(S, dtype=F32)[:, None] * inv_freq[None, :]
    cos, sin = jnp.cos(ang), jnp.sin(ang)
    reps = LANES // HEAD_DIM
    cos_t = jnp.tile(jnp.concatenate([cos, cos], axis=1), (1, reps))
    sin_t = jnp.tile(jnp.concatenate([-sin, sin], axis=1), (1, reps))
    return cos_t, sin_t


def _rotary(t, cos_t, sin_t, *, col0, width, sign, name):
    S = t.shape[0]
    tm = _pick(S, (512, 256, 128))
    half = HEAD_DIM // 2
    c0 = col0 // LANES

    def body(t_ref, c_ref, s_ref, o_ref):
        v = t_ref[...]
        lane = lax.broadcasted_iota(jnp.int32, v.shape, 1)
        swapped = jnp.where(lane % HEAD_DIM < half, pltpu.roll(v, LANES - half, axis=1), pltpu.roll(v, half, axis=1))
        o_ref[...] = v * c_ref[...] + swapped * (s_ref[...] * sign)

    return pl.pallas_call(
        body, name=name, grid=(S // tm, width // LANES),
        out_shape=jax.ShapeDtypeStruct((S, width), F32),
        in_specs=[pl.BlockSpec((tm, LANES), lambda i, j: (i, c0 + j)),
                  pl.BlockSpec((tm, LANES), lambda i, j: (i, 0)),
                  pl.BlockSpec((tm, LANES), lambda i, j: (i, 0))],
        out_specs=pl.BlockSpec((tm, LANES), lambda i, j: (i, j)),
        compiler_params=_cparams(("parallel", "parallel")),
    )(t, cos_t, sin_t)


def _head_masks(shape):
    lane = lax.broadcasted_iota(jnp.int32, shape, 1)
    return [(lane >= HEAD_DIM * h) & (lane < HEAD_DIM * (h + 1)) for h in range(LANES // HEAD_DIM)]


def _sb_scores(qh, k_j, scale):
    z = lax.dot_general(qh, k_j, (((1,), (1,)), ((), ())), preferred_element_type=F32) * scale
    l1p = jnp.log(1.0 + jnp.exp(-jnp.abs(z)))
    log_beta = jnp.minimum(z, 0.0) - l1p
    log_stay = -jnp.maximum(z, 0.0) - l1p
    return log_beta, log_stay


def _sb_fwd(p_sb, *, name):
    S = p_sb.shape[0]
    W = p_sb.shape[1] // 3
    npair = W // LANES
    T = SB_TILE
    n_tiles = S // T
    assert n_tiles <= HEAD_DIM
    scale = HEAD_DIM ** -0.5

    def body(q_ref, k_ref, v_ref, o_ref, c_ref):
        I = pl.program_id(1)
        lane = lax.broadcasted_iota(jnp.int32, (T, LANES), 1)
        row = lax.broadcasted_iota(jnp.int32, (T, T), 0)
        col = lax.broadcasted_iota(jnp.int32, (T, T), 1)
        causal = col < row
        later_than = (row > col).astype(BF16)
        q = q_ref[...]
        o_tot = jnp.zeros((T, LANES), F32)
        carr = jnp.zeros((T, LANES), F32)
        for h, hm in enumerate(_head_masks((T, LANES))):
            qh = jnp.where(hm, q, jnp.zeros_like(q))

            def tile(J, state, diag, h=h, qh=qh):
                lc, acc, carr = state
                off = pl.multiple_of(J * T, T)
                k_j = k_ref[pl.ds(off, T), :]
                v_j = v_ref[pl.ds(off, T), :]
                log_beta, log_stay = _sb_scores(qh, k_j, scale)
                if diag:
                    log_stay = jnp.where(causal, log_stay, 0.0)
                later = jnp.dot(log_stay.astype(BF16), later_than, preferred_element_type=F32) + lc
                w = jnp.exp(log_beta + later)
                if diag:
                    w = jnp.where(causal, w, 0.0)
                acc = acc + jnp.dot(w.astype(BF16), v_j, preferred_element_type=F32)
                carr = jnp.where(lane == HEAD_DIM * h + J, lc, carr)
                lc = lc + jnp.sum(log_stay, axis=1, keepdims=True)
                return lc, acc, carr

            state = (jnp.zeros((T, 1), F32), jnp.zeros((T, LANES), F32), carr)
            state = tile(I, state, True)
            state = lax.fori_loop(0, I, lambda jj, st: tile(I - 1 - jj, st, False), state)
            _, acc, carr = state
            o_tot = jnp.where(hm, acc, o_tot)
        o_ref[...] = o_tot
        c_ref[...] = carr

    blk = lambda I_off: pl.BlockSpec((T, LANES), lambda p, I: (I, I_off + p))
    full = lambda off: pl.BlockSpec((S, LANES), lambda p, I: (0, off + p))
    return pl.pallas_call(
        body, name=name, grid=(npair, n_tiles),
        out_shape=[jax.ShapeDtypeStruct((S, W), F32), jax.ShapeDtypeStruct((S, W), F32)],
        in_specs=[blk(0), full(npair), full(2 * npair)],
        out_specs=[blk(0), blk(0)],
        compiler_params=_cparams(("parallel", "arbitrary")),
    )(p_sb, p_sb, p_sb)


def _sb_bwd(p_sb, do, carries, *, name):
    S = p_sb.shape[0]
    W = p_sb.shape[1] // 3
    npair = W // LANES
    T = SB_TILE
    n_tiles = S // T
    scale = HEAD_DIM ** -0.5

    def body(q_ref, k_ref, v_ref, do_ref, c_ref, dq_ref, dk_ref, dv_ref):
        I = pl.program_id(1)

        @pl.when(I == 0)
        def _():
            dk_ref[...] = jnp.zeros_like(dk_ref)
            dv_ref[...] = jnp.zeros_like(dv_ref)

        lane = lax.broadcasted_iota(jnp.int32, (T, LANES), 1)
        row = lax.broadcasted_iota(jnp.int32, (T, T), 0)
        col = lax.broadcasted_iota(jnp.int32, (T, T), 1)
        causal = col < row
        later_than = (row > col).astype(BF16)
        earlier_than = (row < col).astype(BF16)
        q = q_ref[...]
        do = do_ref[...].astype(BF16)
        carr = c_ref[...]
        dq_tot = jnp.zeros((T, LANES), F32)
        for h, hm in enumerate(_head_masks((T, LANES))):
            qh = jnp.where(hm, q, jnp.zeros_like(q))
            doh = jnp.where(hm, do, jnp.zeros_like(do))

            def tile(J, state, diag, h=h, qh=qh, doh=doh):
                ec, dq_acc = state
                off = pl.multiple_of(J * T, T)
                k_j = k_ref[pl.ds(off, T), :]
                v_j = v_ref[pl.ds(off, T), :]
                log_beta, log_stay = _sb_scores(qh, k_j, scale)
                if diag:
                    log_stay = jnp.where(causal, log_stay, 0.0)
                lc = jnp.sum(jnp.where(lane == HEAD_DIM * h + J, carr, 0.0), axis=1, keepdims=True)
                later = jnp.dot(log_stay.astype(BF16), later_than, preferred_element_type=F32) + lc
                w = jnp.exp(log_beta + later)
                if diag:
                    w = jnp.where(causal, w, 0.0)
                dw = lax.dot_general(doh, v_j, (((1,), (1,)), ((), ())), preferred_element_type=F32)
                e = w * dw
                e_before = jnp.dot(e.astype(BF16), earlier_than, preferred_element_type=F32) + ec
                beta = jnp.exp(log_beta)
                dz = e * (1.0 - beta) - e_before * beta
                if diag:
                    dz = jnp.where(causal, dz, 0.0)
                dzb = (dz * scale).astype(BF16)
                dq_acc = dq_acc + jnp.dot(dzb, k_j, preferred_element_type=F32)
                dk_ref[pl.ds(off, T), :] += lax.dot_general(dzb, qh, (((0,), (0,)), ((), ())),
                                                            preferred_element_type=F32)
                dv_ref[pl.ds(off, T), :] += lax.dot_general(w.astype(BF16), doh, (((0,), (0,)), ((), ())),
                                                            preferred_element_type=F32)
                ec = ec + jnp.sum(e, axis=1, keepdims=True)
                return ec, dq_acc

            state = (jnp.zeros((T, 1), F32), jnp.zeros((T, LANES), F32))
            state = lax.fori_loop(0, I, lambda J, st: tile(J, st, False), state)
            _, dq_acc = tile(I, state, True)
            dq_tot = jnp.where(hm, dq_acc, dq_tot)
        dq_ref[...] = dq_tot

    blk = lambda src_off: pl.BlockSpec((T, LANES), lambda p, I: (I, src_off + p))
    full = lambda off: pl.BlockSpec((S, LANES), lambda p, I: (0, off + p))
    dq, dk, dv = pl.pallas_call(
        body, name=name, grid=(npair, n_tiles),
        out_shape=[jax.ShapeDtypeStruct((S, W), F32)] * 3,
        in_specs=[blk(0), full(npair), full(2 * npair), blk(0), blk(0)],
        out_specs=[blk(0), full(0), full(0)],
        compiler_params=_cparams(("parallel", "arbitrary")),
    )(p_sb, p_sb, p_sb, do, carries)
    return dq, dk, dv


def _dil_blocks(b, body_fn):
    for pi, (window, dil) in enumerate(DILATED_PATTERNS):
        assert window // dil == DIL_BLOCK
        nblk = DIL_SUPER // (DIL_BLOCK * dil)

        def per_class(c, _, pi=pi, dil=dil, nblk=nblk):
            def per_block(n, _):
                body_fn(pi, dil, c, n, b * nblk + n)
                return 0
            return lax.fori_loop(0, nblk, per_block, 0)

        lax.fori_loop(0, dil, per_class, 0)


def _dil_rows(start, size, dil):
    if dil == 1:
        return pl.ds(pl.multiple_of(start, DIL_BLOCK), size)
    return pl.ds(start, size, stride=dil)


def _dil_valid(gn, ws):
    qi = lax.broadcasted_iota(jnp.int32, (DIL_BLOCK, 2 * DIL_BLOCK), 0)
    kk = lax.broadcasted_iota(jnp.int32, (DIL_BLOCK, 2 * DIL_BLOCK), 1)
    dist = (gn - ws) * DIL_BLOCK + qi - kk
    return (dist >= 0) & (dist <= DIL_BLOCK)


def _dl_fwd(q, k, v, *, name):
    S, W = q.shape
    npair = W // LANES
    nsuper = S // DIL_SUPER
    assert S % DIL_SUPER == 0 and S // max(d for _, d in DILATED_PATTERNS) >= 2 * DIL_BLOCK
    scale = HEAD_DIM ** -0.5
    npat = len(DILATED_PATTERNS)

    def body(q_ref, k_ref, v_ref, o_ref, l_ref, *pattern_refs):
        op_refs, lp_refs = pattern_refs[:npat], pattern_refs[npat:]
        b = pl.program_id(1)
        masks = _head_masks((DIL_BLOCK, LANES))

        def block(pi, dil, c, n, gn):
            ws = jnp.maximum(gn - 1, 0)
            qrows = n * (DIL_BLOCK * dil) + c
            krows = ws * (DIL_BLOCK * dil) + c
            q_idx = _dil_rows(qrows, DIL_BLOCK, dil)
            k_idx = _dil_rows(krows, 2 * DIL_BLOCK, dil)
            qb = q_ref[q_idx, :]
            kb = k_ref[k_idx, :].astype(BF16)
            vb = v_ref[k_idx, :].astype(BF16)
            valid = _dil_valid(gn, ws)
            o_blk = jnp.zeros((DIL_BLOCK, LANES), F32)
            l_blk = jnp.zeros((DIL_BLOCK, LANES), F32)
            for hm in masks:
                qh = jnp.where(hm, qb, 0.0).astype(BF16)
                z = lax.dot_general(qh, kb, (((1,), (1,)), ((), ())), preferred_element_type=F32) * scale
                z = jnp.where(valid, z, NEG_BIG)
                m = jnp.max(z, axis=1, keepdims=True)
                p = jnp.exp(z - m)
                den = jnp.sum(p, axis=1, keepdims=True)
                acc = jnp.dot(p.astype(BF16), vb, preferred_element_type=F32)
                o_blk = jnp.where(hm, acc / den, o_blk)
                l_blk = jnp.where(hm, m + jnp.log(den), l_blk)
            op_refs[pi][q_idx, :] = o_blk
            lp_refs[pi][q_idx, :] = l_blk

        _dil_blocks(b, block)
        lses = [r[...] for r in lp_refs]
        top = functools.reduce(jnp.maximum, lses)
        ws_ = [jnp.exp(l - top) for l in lses]
        den = functools.reduce(jnp.add, ws_)
        num = functools.reduce(jnp.add, [w * r[...] for r, w in zip(op_refs, ws_)])
        o_ref[...] = num / den
        l_ref[...] = top + jnp.log(den)

    blk = pl.BlockSpec((DIL_SUPER, LANES), lambda p, b: (b, p))
    full = pl.BlockSpec((S, LANES), lambda p, b: (0, p))
    return pl.pallas_call(
        body, name=name, grid=(npair, nsuper),
        out_shape=[jax.ShapeDtypeStruct((S, W), F32)] * 2,
        in_specs=[blk, full, full], out_specs=[blk, blk],
        scratch_shapes=[pltpu.VMEM((DIL_SUPER, LANES), F32)] * (2 * npat),
        compiler_params=_cparams(("parallel", "arbitrary")),
    )(q, k, v)


def _dl_bwd(q, k, v, o, lse, do, *, name):
    S, W = q.shape
    npair = W // LANES
    nsuper = S // DIL_SUPER
    scale = HEAD_DIM ** -0.5

    def body(q_ref, k_ref, v_ref, o_ref, l_ref, do_ref, dq_ref, dk_ref, dv_ref, delta_ref):
        b = pl.program_id(1)

        @pl.when(b == 0)
        def _():
            dk_ref[...] = jnp.zeros_like(dk_ref)
            dv_ref[...] = jnp.zeros_like(dv_ref)

        dq_ref[...] = jnp.zeros_like(dq_ref)
        prod = do_ref[...] * o_ref[...]
        delta = jnp.zeros_like(prod)
        for hm in _head_masks(prod.shape):
            delta = jnp.where(hm, jnp.sum(jnp.where(hm, prod, 0.0), axis=1, keepdims=True), delta)
        delta_ref[...] = delta
        masks = _head_masks((DIL_BLOCK, LANES))
        kmasks = _head_masks((2 * DIL_BLOCK, LANES))

        def block(pi, dil, c, n, gn):
            ws = jnp.maximum(gn - 1, 0)
            qrows = n * (DIL_BLOCK * dil) + c
            krows = ws * (DIL_BLOCK * dil) + c
            q_idx = _dil_rows(qrows, DIL_BLOCK, dil)
            k_idx = _dil_rows(krows, 2 * DIL_BLOCK, dil)
            qb = q_ref[q_idx, :]
            dob = do_ref[q_idx, :]
            lb = l_ref[q_idx, :]
            db = delta_ref[q_idx, :]
            kb = k_ref[k_idx, :].astype(BF16)
            vb = v_ref[k_idx, :].astype(BF16)
            valid = _dil_valid(gn, ws)
            dq_blk = jnp.zeros((DIL_BLOCK, LANES), F32)
            dk_blk = jnp.zeros((2 * DIL_BLOCK, LANES), F32)
            dv_blk = jnp.zeros((2 * DIL_BLOCK, LANES), F32)
            for h, (hm, km) in enumerate(zip(masks, kmasks)):
                qh = jnp.where(hm, qb, 0.0).astype(BF16)
                doh = jnp.where(hm, dob, 0.0).astype(BF16)
                lse_h = lb[:, HEAD_DIM * h:HEAD_DIM * h + 1]
                delta_h = db[:, HEAD_DIM * h:HEAD_DIM * h + 1]
                z = lax.dot_general(qh, kb, (((1,), (1,)), ((), ())), preferred_element_type=F32) * scale
                p = jnp.where(valid, jnp.exp(jnp.where(valid, z, NEG_BIG) - lse_h), 0.0)
                dp = lax.dot_general(doh, vb, (((1,), (1,)), ((), ())), preferred_element_type=F32)
                dzb = (p * (dp - delta_h) * scale).astype(BF16)
                dq_blk = jnp.where(hm, jnp.dot(dzb, kb, preferred_element_type=F32), dq_blk)
                dk_blk = dk_blk + lax.dot_general(dzb, qh, (((0,), (0,)), ((), ())), preferred_element_type=F32)
                dv_blk = dv_blk + lax.dot_general(p.astype(BF16), doh, (((0,), (0,)), ((), ())),
                                                  preferred_element_type=F32)
            dq_ref[q_idx, :] = dq_ref[q_idx, :] + dq_blk
            dk_ref[k_idx, :] = dk_ref[k_idx, :] + dk_blk
            dv_ref[k_idx, :] = dv_ref[k_idx, :] + dv_blk

        _dil_blocks(b, block)

    blk = pl.BlockSpec((DIL_SUPER, LANES), lambda p, b: (b, p))
    full = pl.BlockSpec((S, LANES), lambda p, b: (0, p))
    return pl.pallas_call(
        body, name=name, grid=(npair, nsuper),
        out_shape=[jax.ShapeDtypeStruct((S, W), F32)] * 3,
        in_specs=[blk, full, full, blk, blk, blk], out_specs=[blk, full, full],
        scratch_shapes=[pltpu.VMEM((DIL_SUPER, LANES), F32)],
        compiler_params=_cparams(("parallel", "arbitrary")),
    )(q, k, v, o, lse, do)


def _loss_head(x, gain, target, *, name):
    S, D = x.shape
    tm = _pick(S, (512, 256, 128))

    def body(x_ref, g_ref, t_ref, dx_ref, dg_ref, loss_ref):
        first = pl.program_id(0) == 0
        xh, r = _rms_hat(x_ref[...])
        g = g_ref[...]
        err = xh * g - t_ref[...]
        dy = err * (1.0 / D)
        dxh = dy * g
        dx_ref[...] = r * (dxh - xh * jnp.mean(dxh * xh, axis=-1, keepdims=True))
        dg_part = jnp.sum(dy * xh, axis=0, keepdims=True)
        loss_part = jnp.zeros((1, LANES), F32) + 0.5 * jnp.sum(jnp.mean(err * err, axis=-1, keepdims=True),
                                                               axis=0, keepdims=True)

        @pl.when(first)
        def _():
            dg_ref[...] = dg_part
            loss_ref[...] = loss_part

        @pl.when(jnp.logical_not(first))
        def _():
            dg_ref[...] += dg_part
            loss_ref[...] += loss_part

    row = pl.BlockSpec((tm, D), lambda i: (i, 0))
    vec = pl.BlockSpec((1, D), lambda i: (0, 0))
    return pl.pallas_call(
        body, name=name, grid=(S // tm,),
        out_shape=[jax.ShapeDtypeStruct((S, D), F32), jax.ShapeDtypeStruct((1, D), F32),
                   jax.ShapeDtypeStruct((1, LANES), F32)],
        in_specs=[row, vec, row], out_specs=[row, vec, pl.BlockSpec((1, LANES), lambda i: (0, 0))],
        compiler_params=_cparams(("arbitrary",)),
    )(x, gain, target)


def _local_step(x, target, gains, weights):
    S, D = x.shape
    d_sb = gains["sb_out_norm"].shape[1]
    d_dl = gains["dil_out_norm"].shape[1]
    w_in = weights["w_in"]
    w_in_sb, w_in_dl = w_in[:, :3 * d_sb], w_in[:, 3 * d_sb:]
    w_out = weights["w_out"]
    cos_t, sin_t = _rope_tables(S)

    x1, saved1 = _ffn_fwd(x, gains["ffn1_norm"], weights["ffn1_w_gate"], weights["ffn1_w_up"],
                          weights["ffn1_w_down"], tag="ffn1")
    h2 = _rms_fwd([x1], [gains["mix_norm"]], name="mix_norm")
    p_sb = _mm(h2, w_in_sb, outs=(BF16,), name="proj_sb")
    p_dl = _mm(h2, w_in_dl, name="proj_dl")
    q_dl = _rotary(p_dl, cos_t, sin_t, col0=0, width=d_dl, sign=1.0, name="rope_q")
    k_dl = _rotary(p_dl, cos_t, sin_t, col0=d_dl, width=d_dl, sign=1.0, name="rope_k")
    v_dl = p_dl[:, 2 * d_dl:]
    o_sb, carries = _sb_fwd(p_sb, name="sb_fwd")
    o_dl, lse_dl = _dl_fwd(q_dl, k_dl, v_dl, name="dl_fwd")
    merged = _rms_fwd([o_sb, o_dl], [gains["sb_out_norm"], gains["dil_out_norm"]], name="out_norm")
    x2 = _mm(merged, w_out, res=x1, name="out_proj")
    x3, saved2 = _ffn_fwd(x2, gains["ffn2_norm"], weights["ffn2_w_gate"], weights["ffn2_w_up"],
                          weights["ffn2_w_down"], tag="ffn2")
    dx3, d_final, loss_row = _loss_head(x3, gains["final_norm"], target, name="loss_head")

    dx2, d_ffn2_norm, dwg2, dwu2, dwd2 = _ffn_bwd(dx3, x2, gains["ffn2_norm"], weights["ffn2_w_gate"],
                                                  weights["ffn2_w_up"], weights["ffn2_w_down"], saved2, tag="ffn2")
    d_w_out = _mm(merged, dx2, ta=True, name="d_w_out")
    d_merged = _mm(dx2, w_out, tb=True, name="d_merged")
    (do_sb, do_dl), (d_sb_norm, d_dl_norm) = _rms_bwd(
        d_merged, [o_sb, o_dl], [gains["sb_out_norm"], gains["dil_out_norm"]], None, name="out_norm_bwd")
    dq_sb, dk_sb, dv_sb = _sb_bwd(p_sb, do_sb, carries, name="sb_bwd")
    dq_dl, dk_dl, dv_dl = _dl_bwd(q_dl, k_dl, v_dl, o_dl, lse_dl, do_dl, name="dl_bwd")
    dq_dl = _rotary(dq_dl, cos_t, sin_t, col0=0, width=d_dl, sign=-1.0, name="rope_dq")
    dk_dl = _rotary(dk_dl, cos_t, sin_t, col0=0, width=d_dl, sign=-1.0, name="rope_dk")
    pieces = [dq_sb, dk_sb, dv_sb, dq_dl, dk_dl, dv_dl]
    d_w_in = jnp.concatenate([_mm(h2, p, ta=True, name=f"d_w_in_{i}") for i, p in enumerate(pieces)], axis=1)
    dh2 = None
    col = 0
    for i, p in enumerate(pieces):
        dh2 = _mm(p, w_in[:, col:col + p.shape[1]], tb=True, res=dh2, name=f"dh_mix_{i}")
        col += p.shape[1]
    (dx1,), (d_mix_norm,) = _rms_bwd(dh2, [x1], [gains["mix_norm"]], dx2, name="mix_norm_bwd")
    dx, d_ffn1_norm, dwg1, dwu1, dwd1 = _ffn_bwd(dx1, x, gains["ffn1_norm"], weights["ffn1_w_gate"],
                                                 weights["ffn1_w_up"], weights["ffn1_w_down"], saved1, tag="ffn1")
    gain_grads = dict(ffn1_norm=d_ffn1_norm, mix_norm=d_mix_norm, sb_out_norm=d_sb_norm, dil_out_norm=d_dl_norm,
                      ffn2_norm=d_ffn2_norm, final_norm=d_final)
    weight_grads = dict(ffn1_w_gate=dwg1, ffn1_w_up=dwu1, ffn1_w_down=dwd1, w_in=d_w_in, w_out=d_w_out,
                        ffn2_w_gate=dwg2, ffn2_w_up=dwu2, ffn2_w_down=dwd2)
    return loss_row, dx, gain_grads, weight_grads


def _mesh_position():
    return lax.axis_index("x"), lax.axis_index("y"), lax.axis_index("c")


def _flip(coord, bit):
    return 1 - coord if bit else coord


RELATIONS = [(rx, ry, rc) for rx in (0, 1) for ry in (0, 1) for rc in (0, 1)][1:]


def _all_gather(shard, *, name):
    R, C = shard.shape

    def body(x_ref, out_ref, send_sems, recv_sems, local_sem):
        x, y, c = _mesh_position()
        me, sibling = (x, y, c), (x, y, 1 - c)
        chips = [(1 - x, y), (x, 1 - y), (1 - x, 1 - y)]

        def slot(px, py, pc):
            return out_ref.at[4 * px + 2 * py + pc]

        def copy(k, block, to, src=None):
            return pltpu.make_async_remote_copy(
                src_ref=slot(*block) if src is None else src, dst_ref=slot(*block),
                send_sem=send_sems.at[k], recv_sem=recv_sems.at[k],
                device_id=to, device_id_type=pl.DeviceIdType.MESH)

        mine = pltpu.make_async_copy(x_ref, slot(*me), local_sem)
        mine.start()
        first = [copy(0, me, sibling, src=x_ref)]
        first += [copy(1 + j, me, (*chip, c), src=x_ref) for j, chip in enumerate(chips)]
        for cp in first:
            cp.start()
        passed = [copy(4 + j, (*chip, c), sibling) for j, chip in enumerate(chips)]
        for j, chip in enumerate(chips):
            copy(1 + j, (*chip, c), me).wait_recv()
            passed[j].start()
        copy(0, sibling, me).wait_recv()
        for j, chip in enumerate(chips):
            copy(4 + j, (*chip, 1 - c), me).wait_recv()
        for cp in first + passed:
            cp.wait_send()
        mine.wait()

    return pl.pallas_call(
        body, name=name,
        out_shape=jax.ShapeDtypeStruct((N_DEV, R, C), shard.dtype),
        in_specs=[pl.BlockSpec(memory_space=pl.ANY)],
        out_specs=pl.BlockSpec(memory_space=pl.ANY),
        scratch_shapes=[pltpu.SemaphoreType.DMA((7,)), pltpu.SemaphoreType.DMA((7,)), pltpu.SemaphoreType.DMA],
    )(shard)


def _exchange_chunks(packs, *, name):
    n = len(packs)

    def body(*refs):
        in_refs, out_refs = refs[:n], refs[n:2 * n]
        send_sems, recv_sems, local_sems = refs[2 * n:]
        x, y, c = _mesh_position()
        me = 4 * x + 2 * y + c
        copies = []
        for t in range(n):
            local = pltpu.make_async_copy(in_refs[t].at[me], out_refs[t].at[me], local_sems.at[t])
            local.start()
            copies.append(local)
        for r, (rx, ry, rc) in enumerate(RELATIONS):
            px, py, pc = _flip(x, rx), _flip(y, ry), _flip(c, rc)
            peer = 4 * px + 2 * py + pc
            for t in range(n):
                cp = pltpu.make_async_remote_copy(
                    src_ref=in_refs[t].at[peer], dst_ref=out_refs[t].at[me],
                    send_sem=send_sems.at[t, r], recv_sem=recv_sems.at[t, r],
                    device_id=(px, py, pc), device_id_type=pl.DeviceIdType.MESH)
                cp.start()
                copies.append(cp)
        for cp in copies:
            cp.wait()

    return pl.pallas_call(
        body, name=name,
        out_shape=[jax.ShapeDtypeStruct(p.shape, p.dtype) for p in packs],
        in_specs=[pl.BlockSpec(memory_space=pl.ANY)] * n,
        out_specs=[pl.BlockSpec(memory_space=pl.ANY)] * n,
        scratch_shapes=[pltpu.SemaphoreType.DMA((n, 7)), pltpu.SemaphoreType.DMA((n, 7)),
                        pltpu.SemaphoreType.DMA((n,))],
    )(*packs)


def _all_reduce_rows(v, *, name):
    R, C = v.shape

    def body(v_ref, out_ref, buf, send_sems, recv_sems):
        x, y, c = _mesh_position()
        me = 4 * x + 2 * y + c
        buf[me] = v_ref[...]
        copies = []
        for r, (rx, ry, rc) in enumerate(RELATIONS):
            cp = pltpu.make_async_remote_copy(
                src_ref=v_ref, dst_ref=buf.at[me], send_sem=send_sems.at[r], recv_sem=recv_sems.at[r],
                device_id=(_flip(x, rx), _flip(y, ry), _flip(c, rc)), device_id_type=pl.DeviceIdType.MESH)
            cp.start()
            copies.append(cp)
        for cp in copies:
            cp.wait()
        total = buf[0]
        for s in range(1, N_DEV):
            total = total + buf[s]
        out_ref[...] = total

    return pl.pallas_call(
        body, name=name,
        out_shape=jax.ShapeDtypeStruct((R, C), F32),
        in_specs=[pl.BlockSpec(memory_space=pltpu.VMEM)],
        out_specs=pl.BlockSpec(memory_space=pltpu.VMEM),
        scratch_shapes=[pltpu.VMEM((N_DEV, R, C), F32), pltpu.SemaphoreType.DMA((7,)), pltpu.SemaphoreType.DMA((7,))],
    )(v)


def _sum_slots(recv, *, name):
    _, R, C = recv.shape
    tr = _pick(R, (256, 208, 128, 64, 32, 16))

    def body(r_ref, o_ref):
        total = r_ref[0].astype(F32)
        for s in range(1, N_DEV):
            total = total + r_ref[s].astype(F32)
        o_ref[...] = total

    return pl.pallas_call(
        body, name=name, grid=(R // tr,),
        out_shape=jax.ShapeDtypeStruct((R, C), F32),
        in_specs=[pl.BlockSpec((N_DEV, tr, C), lambda i: (0, i, 0))],
        out_specs=pl.BlockSpec((tr, C), lambda i: (i, 0)),
        compiler_params=_cparams(("parallel",)),
    )(recv)


def _adamw(w, g, m, v, *, name):
    R, C = w.shape
    tr = _pick(R, (256, 128, 64, 32, 16, 8))

    def body(w_ref, g_ref, m_ref, v_ref, d_ref, nm_ref, nv_ref):
        g = g_ref[...]
        m_new = ADAM_B1 * m_ref[...] + (1.0 - ADAM_B1) * g
        v_new = ADAM_B2 * v_ref[...] + (1.0 - ADAM_B2) * (g * g)
        m_hat = m_new / (1.0 - ADAM_B1 ** ADAM_STEP)
        v_hat = v_new / (1.0 - ADAM_B2 ** ADAM_STEP)
        d_ref[...] = -ADAM_LR * (m_hat / (jnp.sqrt(v_hat) + ADAM_EPS) + ADAM_WD * w_ref[...])
        nm_ref[...] = m_new
        nv_ref[...] = v_new

    spec = pl.BlockSpec((tr, C), lambda i: (i, 0))
    return pl.pallas_call(
        body, name=name, grid=(R // tr,),
        out_shape=[jax.ShapeDtypeStruct((R, C), F32)] * 3,
        in_specs=[spec] * 4, out_specs=[spec] * 3,
        compiler_params=_cparams(("parallel",)),
    )(w, g, m, v)


WEIGHT_NAMES = ["ffn1_norm", "ffn1_w_gate", "ffn1_w_up", "ffn1_w_down", "mix_norm", "w_in", "sb_out_norm",
                "dil_out_norm", "w_out", "ffn2_norm", "ffn2_w_gate", "ffn2_w_up", "ffn2_w_down", "final_norm"]
GAIN_NAMES = ["ffn1_norm", "mix_norm", "sb_out_norm", "dil_out_norm", "ffn2_norm", "final_norm"]
COL_SHARDED = ["ffn1_w_gate", "ffn1_w_up", "ffn2_w_gate", "ffn2_w_up", "w_in"]
ROW_SHARDED = ["ffn1_w_down", "ffn2_w_down", "w_out"]


def _step(x, target, params, moments_m, moments_v):
    col_pack = jnp.concatenate([params[n] for n in COL_SHARDED], axis=1).astype(BF16)
    row_pack = jnp.concatenate([params[n] for n in ROW_SHARDED], axis=0).astype(BF16)
    col_all = _all_gather(col_pack, name="gather_col")
    row_all = _all_gather(row_pack, name="gather_row")
    weights = {}
    off = 0
    for n in COL_SHARDED:
        w = params[n].shape[1]
        piece = col_all[:, :, off:off + w]
        weights[n] = jnp.transpose(piece, (1, 0, 2)).reshape(piece.shape[1], N_DEV * w)
        off += w
    off = 0
    for n in ROW_SHARDED:
        r = params[n].shape[0]
        weights[n] = row_all[:, off:off + r, :].reshape(N_DEV * r, row_all.shape[2])
        off += r

    gains = {n: params[n] for n in GAIN_NAMES}
    loss_row, grad_x, gain_grads, weight_grads = _local_step(x, target, gains, weights)

    col_chunks, row_chunks = [], []
    for n in COL_SHARDED:
        g = weight_grads[n]
        w = params[n].shape[1]
        col_chunks.append(jnp.transpose(g.reshape(g.shape[0], N_DEV, w), (1, 0, 2)))
    for n in ROW_SHARDED:
        g = weight_grads[n]
        r = params[n].shape[0]
        row_chunks.append(g.reshape(N_DEV, r, g.shape[1]))
    col_send = jnp.concatenate(col_chunks, axis=2).astype(BF16)
    row_send = jnp.concatenate(row_chunks, axis=1).astype(BF16)
    col_recv, row_recv = _exchange_chunks([col_send, row_send], name="exchange_grads")
    col_grad = _sum_slots(col_recv, name="sum_col_grads")
    row_grad = _sum_slots(row_recv, name="sum_row_grads")
    grads = {}
    off = 0
    for n in COL_SHARDED:
        w = params[n].shape[1]
        grads[n] = col_grad[:, off:off + w]
        off += w
    off = 0
    for n in ROW_SHARDED:
        r = params[n].shape[0]
        grads[n] = row_grad[off:off + r, :]
        off += r

    rows = [gain_grads[n].reshape(-1, LANES) for n in GAIN_NAMES] + [loss_row]
    small = jnp.concatenate(rows, axis=0)
    pad = (-small.shape[0]) % 8
    small = jnp.pad(small, ((0, pad), (0, 0)))
    small = _all_reduce_rows(small, name="reduce_gains_loss")
    off = 0
    for n in GAIN_NAMES:
        r = gain_grads[n].shape[1] // LANES
        grads[n] = small[off:off + r].reshape(1, -1)
        off += r
    loss = small[off, 0]

    delta, new_m, new_v = {}, {}, {}
    for n in WEIGHT_NAMES:
        delta[n], new_m[n], new_v[n] = _adamw(params[n], grads[n], moments_m[n], moments_v[n], name=f"adamw_{n}")
    return loss, grad_x, grads, delta, new_m, new_v


def kernel(x, ffn1_norm, ffn1_w_gate, ffn1_w_up, ffn1_w_down, mix_norm, w_in, sb_out_norm, dil_out_norm, w_out, ffn2_norm, ffn2_w_gate, ffn2_w_up, ffn2_w_down, final_norm, loss_target, m_ffn1_norm, m_ffn1_w_gate, m_ffn1_w_up, m_ffn1_w_down, m_mix_norm, m_w_in, m_sb_out_norm, m_dil_out_norm, m_w_out, m_ffn2_norm, m_ffn2_w_gate, m_ffn2_w_up, m_ffn2_w_down, m_final_norm, v_ffn1_norm, v_ffn1_w_gate, v_ffn1_w_up, v_ffn1_w_down, v_mix_norm, v_w_in, v_sb_out_norm, v_dil_out_norm, v_w_out, v_ffn2_norm, v_ffn2_w_gate, v_ffn2_w_up, v_ffn2_w_down, v_final_norm):
    given = dict(locals())
    shapes = {n: given[n].shape for n in WEIGHT_NAMES}

    def as2d(a):
        return a.reshape(1, -1) if a.ndim == 1 else a.reshape(a.shape[-2], a.shape[-1])

    params = {n: as2d(given[n]) for n in WEIGHT_NAMES}
    moments_m = {n: as2d(given["m_" + n]) for n in WEIGHT_NAMES}
    moments_v = {n: as2d(given["v_" + n]) for n in WEIGHT_NAMES}
    loss, grad_x, grads, delta, new_m, new_v = _step(x[0], loss_target[0], params, moments_m, moments_v)
    back = lambda d: [d[n].reshape(shapes[n]) for n in WEIGHT_NAMES]
    return (loss, grad_x[None], *back(grads), *back(delta), *back(new_m), *back(new_v))
```

```python
import functools

import jax
import jax.numpy as jnp
from jax import lax
from jax.experimental import pallas as pl
from jax.experimental.pallas import tpu as pltpu

F32 = jnp.float32
BF16 = jnp.bfloat16

N_DEV = 8
HEAD_DIM = 64
LANES = 128
DILATED_PATTERNS = ((128, 1), (512, 4), (2048, 16))
DIL_BLOCK = 128
DIL_SUPER = 2048
SB_TILE = 256
ROPE_THETA = 10000.0
RMS_EPS = 1e-6
HALF_STEP = 0.5
ADAM_LR = 0.001
ADAM_B1 = 0.9
ADAM_B2 = 0.999
ADAM_EPS = 1e-08
ADAM_WD = 0.01
ADAM_STEP = 10
NEG_BIG = -1e30
VMEM_CAP_MB = 60


def _pick(n, prefs):
    for p in prefs:
        if n % p == 0:
            return p
    return n


def _cparams(sem=None, vmem_mb=48):
    return pltpu.CompilerParams(dimension_semantics=sem, vmem_limit_bytes=min(vmem_mb, VMEM_CAP_MB) * 1024 * 1024)


def _nbytes(shape, dtype):
    n = 1
    for s in shape:
        n *= s
    return n * jnp.dtype(dtype).itemsize


def _mm(a, b, *, name, ta=False, tb=False, outs=(F32,), res=None, alpha=1.0, extras=(), epilogue=None,
        tm=None, tn=None, tk=None):
    if ta:
        K, M = a.shape
    else:
        M, K = a.shape
    if tb:
        N, Kb = b.shape
    else:
        Kb, N = b.shape
    assert K == Kb, (a.shape, b.shape, ta, tb)
    tm = tm or _pick(M, (512, 256, 128))
    tn = tn or _pick(N, (512, 384, 256, 128))
    tk = tk or (K if K <= 3072 else _pick(K, (1024, 512, 256, 128)))
    nk = K // tk
    a_spec = pl.BlockSpec((tk, tm), lambda i, j, k: (k, i)) if ta else pl.BlockSpec((tm, tk), lambda i, j, k: (i, k))
    b_spec = pl.BlockSpec((tn, tk), lambda i, j, k: (j, k)) if tb else pl.BlockSpec((tk, tn), lambda i, j, k: (k, j))
    mn_spec = pl.BlockSpec((tm, tn), lambda i, j, k: (i, j))
    dims = (((0 if ta else 1,), (1 if tb else 0,)), ((), ()))
    n_extra = len(extras) + (1 if res is not None else 0)
    n_out = len(outs)

    def body(*refs):
        a_ref, b_ref = refs[0], refs[1]
        in_refs = refs[2:2 + n_extra]
        out_refs = refs[2 + n_extra:2 + n_extra + n_out]
        acc_ref = refs[2 + n_extra + n_out]
        k = pl.program_id(2)

        @pl.when(k == 0)
        def _():
            acc_ref[...] = jnp.zeros_like(acc_ref)

        acc_ref[...] += lax.dot_general(a_ref[...].astype(BF16), b_ref[...].astype(BF16), dims,
                                        preferred_element_type=F32)

        @pl.when(k == nk - 1)
        def _():
            acc = acc_ref[...]
            blocks = [r[...] for r in in_refs]
            if res is not None:
                r_blk, blocks = blocks[0], blocks[1:]
            else:
                r_blk = None
            if epilogue is None:
                val = acc * alpha
                if r_blk is not None:
                    val = val + r_blk
                vals = (val,)
            else:
                vals = epilogue(acc, r_blk, *blocks)
            for o_ref, v in zip(out_refs, vals):
                o_ref[...] = v.astype(o_ref.dtype)

    operands = [a, b] + ([res] if res is not None else []) + list(extras)
    in_specs = [a_spec, b_spec] + [mn_spec] * n_extra
    est = 2 * (_nbytes((tm, tk), a.dtype) + _nbytes((tk, tn), b.dtype))
    est += 2 * sum(_nbytes((tm, tn), o.dtype) for o in operands[2:])
    est += 2 * sum(_nbytes((tm, tn), d) for d in outs) + _nbytes((tm, tn), F32)
    result = pl.pallas_call(
        body, name=name, grid=(M // tm, N // tn, nk),
        out_shape=[jax.ShapeDtypeStruct((M, N), d) for d in outs],
        in_specs=in_specs, out_specs=[mn_spec] * n_out,
        scratch_shapes=[pltpu.VMEM((tm, tn), F32)],
        compiler_params=_cparams(("parallel", "parallel", "arbitrary"), vmem_mb=max(32, 2 * est // (1024 * 1024))),
    )(*operands)
    return result[0] if n_out == 1 else result


def _rms_hat(x):
    r = lax.rsqrt(jnp.mean(x * x, axis=-1, keepdims=True) + RMS_EPS)
    return x * r, r


def _rms_fwd(xs, gains, *, name):
    S = xs[0].shape[0]
    widths = [x.shape[1] for x in xs]
    tm = _pick(S, (512, 256, 128))
    n = len(xs)

    def body(*refs):
        o_ref = refs[2 * n]
        off = 0
        for i in range(n):
            xh, _ = _rms_hat(refs[i][...])
            o_ref[:, off:off + widths[i]] = (xh * refs[n + i][...]).astype(o_ref.dtype)
            off += widths[i]

    return pl.pallas_call(
        body, name=name, grid=(S // tm,),
        out_shape=jax.ShapeDtypeStruct((S, sum(widths)), BF16),
        in_specs=[pl.BlockSpec((tm, w), lambda i: (i, 0)) for w in widths]
        + [pl.BlockSpec((1, w), lambda i: (0, 0)) for w in widths],
        out_specs=pl.BlockSpec((tm, sum(widths)), lambda i: (i, 0)),
        compiler_params=_cparams(("parallel",)),
    )(*xs, *gains)


def _rms_bwd(dh, xs, gains, res, *, name):
    S = xs[0].shape[0]
    widths = [x.shape[1] for x in xs]
    tm = _pick(S, (512, 256, 128))
    n = len(xs)
    has_res = res is not None

    def body(*refs):
        dh_ref = refs[0]
        x_refs = refs[1:1 + n]
        g_refs = refs[1 + n:1 + 2 * n]
        r_ref = refs[1 + 2 * n] if has_res else None
        base = 1 + 2 * n + (1 if has_res else 0)
        dx_refs = refs[base:base + n]
        dg_refs = refs[base + n:base + 2 * n]
        first = pl.program_id(0) == 0
        off = 0
        for i in range(n):
            x = x_refs[i][...]
            xh, r = _rms_hat(x)
            d = dh_ref[:, off:off + widths[i]]
            dxh = d * g_refs[i][...]
            dx = r * (dxh - xh * jnp.mean(dxh * xh, axis=-1, keepdims=True))
            if has_res:
                dx = dx + r_ref[...]
            dx_refs[i][...] = dx
            part = jnp.sum(d * xh, axis=0, keepdims=True)

            @pl.when(first)
            def _(i=i, part=part):
                dg_refs[i][...] = part

            @pl.when(jnp.logical_not(first))
            def _(i=i, part=part):
                dg_refs[i][...] += part

            off += widths[i]

    in_specs = [pl.BlockSpec((tm, sum(widths)), lambda i: (i, 0))]
    in_specs += [pl.BlockSpec((tm, w), lambda i: (i, 0)) for w in widths]
    in_specs += [pl.BlockSpec((1, w), lambda i: (0, 0)) for w in widths]
    operands = [dh, *xs, *gains]
    if has_res:
        in_specs.append(pl.BlockSpec((tm, widths[0]), lambda i: (i, 0)))
        operands.append(res)
    out = pl.pallas_call(
        body, name=name, grid=(S // tm,),
        out_shape=[jax.ShapeDtypeStruct((S, w), F32) for w in widths] + [jax.ShapeDtypeStruct((1, w), F32) for w in widths],
        in_specs=in_specs,
        out_specs=[pl.BlockSpec((tm, w), lambda i: (i, 0)) for w in widths]
        + [pl.BlockSpec((1, w), lambda i: (0, 0)) for w in widths],
        compiler_params=_cparams(("arbitrary",)),
    )(*operands)
    return out[:n], out[n:]


def _sigmoid(g):
    return 1.0 / (1.0 + jnp.exp(-g))


def _ffn_fwd(x, gain, wg, wu, wd, *, tag):
    h = _rms_fwd([x], [gain], name=f"{tag}_norm")
    g = _mm(h, wg, outs=(BF16,), name=f"{tag}_gate")

    def act(acc, _, g_blk):
        gf = g_blk.astype(F32)
        return acc, gf * _sigmoid(gf) * acc

    u, a = _mm(h, wu, outs=(BF16, BF16), extras=(g,), epilogue=act, name=f"{tag}_up_act")
    y = _mm(a, wd, res=x, alpha=HALF_STEP, name=f"{tag}_down")
    return y, (h, g, u, a)


def _ffn_bwd(dout, x, gain, wg, wu, wd, saved, *, tag):
    h, g, u, a = saved

    def act_bwd(acc, _, g_blk, u_blk):
        gf, uf = g_blk.astype(F32), u_blk.astype(F32)
        da = acc * HALF_STEP
        sig = _sigmoid(gf)
        silu = gf * sig
        return da * uf * (sig * (1.0 + gf * (1.0 - sig))), da * silu

    dg, du = _mm(dout, wd, tb=True, outs=(BF16, BF16), extras=(g, u), epilogue=act_bwd, name=f"{tag}_bwd_act")
    dwg = _mm(h, dg, ta=True, name=f"{tag}_dwg")
    dwu = _mm(h, du, ta=True, name=f"{tag}_dwu")
    dwd = _mm(a, dout, ta=True, alpha=HALF_STEP, name=f"{tag}_dwd")
    dh = _mm(dg, wg, tb=True, name=f"{tag}_dh_gate")
    dh = _mm(du, wu, tb=True, res=dh, name=f"{tag}_dh_up")
    (dx,), (dgain,) = _rms_bwd(dh, [x], [gain], dout, name=f"{tag}_norm_bwd")
    return dx, dgain, dwg, dwu, dwd


def _rope_tables(S):
    half = HEAD_DIM // 2
    inv_freq = ROPE_THETA ** (-jnp.arange(half, dtype=F32) / half)
    ang = jnp.arange(S, dtype=F32)[:, None] * inv_freq[None, :]
    cos, sin = jnp.cos(ang), jnp.sin(ang)
    reps = LANES // HEAD_DIM
    cos_t = jnp.tile(jnp.concatenate([cos, cos], axis=1), (1, reps))
    sin_t = jnp.tile(jnp.concatenate([-sin, sin], axis=1), (1, reps))
    return cos_t, sin_t


def _rotary(t, cos_t, sin_t, *, col0, width, sign, name):
    S = t.shape[0]
    tm = _pick(S, (512, 256, 128))
    half = HEAD_DIM // 2
    c0 = col0 // LANES

    def body(t_ref, c_ref, s_ref, o_ref):
        v = t_ref[...]
        lane = lax.broadcasted_iota(jnp.int32, v.shape, 1)
        swapped = jnp.where(lane % HEAD_DIM < half, pltpu.roll(v, LANES - half, axis=1), pltpu.roll(v, half, axis=1))
        o_ref[...] = v * c_ref[...] + swapped * (s_ref[...] * sign)

    return pl.pallas_call(
        body, name=name, grid=(S // tm, width // LANES),
        out_shape=jax.ShapeDtypeStruct((S, width), F32),
        in_specs=[pl.BlockSpec((tm, LANES), lambda i, j: (i, c0 + j)),
                  pl.BlockSpec((tm, LANES), lambda i, j: (i, 0)),
                  pl.BlockSpec((tm, LANES), lambda i, j: (i, 0))],
        out_specs=pl.BlockSpec((tm, LANES), lambda i, j: (i, j)),
        compiler_params=_cparams(("parallel", "parallel")),
    )(t, cos_t, sin_t)


def _head_masks(shape):
    lane = lax.broadcasted_iota(jnp.int32, shape, 1)
    return [(lane >= HEAD_DIM * h) & (lane < HEAD_DIM * (h + 1)) for h in range(LANES // HEAD_DIM)]


def _sb_scores(q2, k_j):
    z = lax.dot_general(q2, k_j, (((1,), (1,)), ((), ())), preferred_element_type=F32)
    l1p = jnp.log(1.0 + jnp.exp(-jnp.abs(z)))
    log_beta = jnp.minimum(z, 0.0) - l1p
    return log_beta, log_beta - z


def _sb_stack_heads(t, scale=None):
    parts = [jnp.where(hm, t, jnp.zeros_like(t)) for hm in _head_masks(t.shape)]
    t2 = jnp.concatenate(parts, axis=0)
    if scale is not None:
        t2 = (t2.astype(F32) * scale).astype(t2.dtype)
    return t2


def _sb_unstack_heads(t2):
    T = t2.shape[0] // 2
    masks = _head_masks((T, LANES))
    return jnp.where(masks[0], t2[:T], t2[T:])


def _sb_causal(T):
    row = lax.broadcasted_iota(jnp.int32, (2 * T, T), 0)
    col = lax.broadcasted_iota(jnp.int32, (2 * T, T), 1)
    return col < jnp.where(row >= T, row - T, row)


def _sb_triangle(T, later):
    row = lax.broadcasted_iota(jnp.int32, (T, T), 0)
    col = lax.broadcasted_iota(jnp.int32, (T, T), 1)
    return ((row > col) if later else (row < col)).astype(BF16)


def _sb_fwd(p_sb, *, name):
    S = p_sb.shape[0]
    W = p_sb.shape[1] // 3
    npair = W // LANES
    T = SB_TILE
    n_tiles = S // T
    assert n_tiles <= HEAD_DIM
    scale = HEAD_DIM ** -0.5

    def body(q_ref, k_ref, v_ref, o_ref, c_ref):
        I = pl.program_id(1)
        lane = lax.broadcasted_iota(jnp.int32, (T, LANES), 1)
        causal = _sb_causal(T)
        later_than = _sb_triangle(T, True)
        q2 = _sb_stack_heads(q_ref[...], scale)

        def scores(J, diag):
            off = pl.multiple_of(J * T, T)
            log_beta, log_stay = _sb_scores(q2, k_ref[pl.ds(off, T), :])
            if diag:
                log_stay = jnp.where(causal, log_stay, 0.0)
            local = jnp.dot(log_stay.astype(BF16), later_than, preferred_element_type=F32)
            return log_beta, local, jnp.sum(log_stay, axis=1, keepdims=True), v_ref[pl.ds(off, T), :]

        def weigh(J, sc, lc, acc, carr, diag):
            log_beta, local, _, v_j = sc
            w = jnp.exp(log_beta + (local + lc))
            if diag:
                w = jnp.where(causal, w, 0.0)
            acc = acc + jnp.dot(w.astype(BF16), v_j, preferred_element_type=F32)
            carr = jnp.where(lane == J, lc[:T], carr)
            carr = jnp.where(lane == HEAD_DIM + J, lc[T:], carr)
            return acc, carr

        def one(J, state, diag):
            lc, acc, carr = state
            sc = scores(J, diag)
            acc, carr = weigh(J, sc, lc, acc, carr, diag)
            return lc + sc[2], acc, carr

        def two(J, state):
            lc, acc, carr = state
            sc_a, sc_b = scores(J, False), scores(J - 1, False)
            acc, carr = weigh(J, sc_a, lc, acc, carr, False)
            lc = lc + sc_a[2]
            acc, carr = weigh(J - 1, sc_b, lc, acc, carr, False)
            return lc + sc_b[2], acc, carr

        state = (jnp.zeros((2 * T, 1), F32), jnp.zeros((2 * T, LANES), F32), jnp.zeros((T, LANES), F32))
        state = one(I, state, True)
        state = lax.fori_loop(0, I // 2, lambda jj, st: two(I - 1 - 2 * jj, st), state)
        state = lax.cond(I % 2 == 1, lambda st: one(0, st, False), lambda st: st, state)
        _, acc, carr = state
        o_ref[...] = _sb_unstack_heads(acc)
        c_ref[...] = carr

    blk = lambda I_off: pl.BlockSpec((T, LANES), lambda p, I: (I, I_off + p))
    full = lambda off: pl.BlockSpec((S, LANES), lambda p, I: (0, off + p))
    return pl.pallas_call(
        body, name=name, grid=(npair, n_tiles),
        out_shape=[jax.ShapeDtypeStruct((S, W), F32), jax.ShapeDtypeStruct((S, W), F32)],
        in_specs=[blk(0), full(npair), full(2 * npair)],
        out_specs=[blk(0), blk(0)],
        compiler_params=_cparams(("parallel", "arbitrary")),
    )(p_sb, p_sb, p_sb)


def _sb_bwd(p_sb, do, carries, *, name):
    S = p_sb.shape[0]
    W = p_sb.shape[1] // 3
    npair = W // LANES
    T = SB_TILE
    n_tiles = S // T
    scale = HEAD_DIM ** -0.5

    def body(q_ref, k_ref, v_ref, do_ref, c_ref, dq_ref, dk_ref, dv_ref):
        I = pl.program_id(1)

        @pl.when(I == 0)
        def _():
            dk_ref[...] = jnp.zeros_like(dk_ref)
            dv_ref[...] = jnp.zeros_like(dv_ref)

        lane = lax.broadcasted_iota(jnp.int32, (T, LANES), 1)
        causal = _sb_causal(T)
        later_than = _sb_triangle(T, True)
        earlier_than = _sb_triangle(T, False)
        q2 = _sb_stack_heads(q_ref[...], scale)
        do2 = _sb_stack_heads(do_ref[...].astype(BF16))
        carr = c_ref[...]
        tn_dims = (((0,), (0,)), ((), ()))

        def chain(J, diag):
            off = pl.multiple_of(J * T, T)
            k_j = k_ref[pl.ds(off, T), :]
            v_j = v_ref[pl.ds(off, T), :]
            log_beta, log_stay = _sb_scores(q2, k_j)
            if diag:
                log_stay = jnp.where(causal, log_stay, 0.0)
            lc = jnp.concatenate(
                [jnp.sum(jnp.where(lane == HEAD_DIM * h + J, carr, 0.0), axis=1, keepdims=True) for h in range(2)],
                axis=0)
            later = jnp.dot(log_stay.astype(BF16), later_than, preferred_element_type=F32) + lc
            w = jnp.exp(log_beta + later)
            if diag:
                w = jnp.where(causal, w, 0.0)
            dw = lax.dot_general(do2, v_j, (((1,), (1,)), ((), ())), preferred_element_type=F32)
            e = w * dw
            local = jnp.dot(e.astype(BF16), earlier_than, preferred_element_type=F32)
            return off, k_j, w, e, local, jnp.exp(log_beta), jnp.sum(e, axis=1, keepdims=True)

        def finish(ch, ec, dq_acc, diag):
            off, k_j, w, e, local, beta, _ = ch
            e_before = local + ec
            dz = e - beta * (e + e_before)
            if diag:
                dz = jnp.where(causal, dz, 0.0)
            dzb = dz.astype(BF16)
            dq_acc = dq_acc + jnp.dot(dzb, k_j, preferred_element_type=F32)
            dk_ref[pl.ds(off, T), :] += lax.dot_general(dzb, q2, tn_dims, preferred_element_type=F32)
            dv_ref[pl.ds(off, T), :] += lax.dot_general(w.astype(BF16), do2, tn_dims, preferred_element_type=F32)
            return dq_acc

        def one(J, state, diag):
            ec, dq_acc = state
            ch = chain(J, diag)
            return ec + ch[6], finish(ch, ec, dq_acc, diag)

        def two(J, state):
            ec, dq_acc = state
            ch_a, ch_b = chain(J, False), chain(J + 1, False)
            dq_acc = finish(ch_a, ec, dq_acc, False)
            ec = ec + ch_a[6]
            dq_acc = finish(ch_b, ec, dq_acc, False)
            return ec + ch_b[6], dq_acc

        state = (jnp.zeros((2 * T, 1), F32), jnp.zeros((2 * T, LANES), F32))
        state = lax.fori_loop(0, I // 2, lambda jj, st: two(2 * jj, st), state)
        state = lax.cond(I % 2 == 1, lambda st: one(I - 1, st, False), lambda st: st, state)
        _, dq_acc = one(I, state, True)
        dq_ref[...] = _sb_unstack_heads(dq_acc) * scale

    blk = lambda src_off: pl.BlockSpec((T, LANES), lambda p, I: (I, src_off + p))
    full = lambda off: pl.BlockSpec((S, LANES), lambda p, I: (0, off + p))
    dq, dk, dv = pl.pallas_call(
        body, name=name, grid=(npair, n_tiles),
        out_shape=[jax.ShapeDtypeStruct((S, W), F32)] * 3,
        in_specs=[blk(0), full(npair), full(2 * npair), blk(0), blk(0)],
        out_specs=[blk(0), full(0), full(0)],
        compiler_params=_cparams(("parallel", "arbitrary")),
    )(p_sb, p_sb, p_sb, do, carries)
    return dq, dk, dv


def _dil_blocks(b, body_fn):
    for pi, (window, dil) in enumerate(DILATED_PATTERNS):
        assert window // dil == DIL_BLOCK
        nblk = DIL_SUPER // (DIL_BLOCK * dil)

        def per_class(c, _, pi=pi, dil=dil, nblk=nblk):
            def per_block(n, _):
                body_fn(pi, dil, c, n, b * nblk + n)
                return 0
            return lax.fori_loop(0, nblk, per_block, 0)

        lax.fori_loop(0, dil, per_class, 0)


def _dil_rows(start, size, dil):
    if dil == 1:
        return pl.ds(pl.multiple_of(start, DIL_BLOCK), size)
    return pl.ds(start, size, stride=dil)


def _dil_valid(gn, ws):
    qi = lax.broadcasted_iota(jnp.int32, (DIL_BLOCK, 2 * DIL_BLOCK), 0)
    kk = lax.broadcasted_iota(jnp.int32, (DIL_BLOCK, 2 * DIL_BLOCK), 1)
    dist = (gn - ws) * DIL_BLOCK + qi - kk
    return (dist >= 0) & (dist <= DIL_BLOCK)


def _dl_fwd(q, k, v, *, name):
    S, W = q.shape
    npair = W // LANES
    nsuper = S // DIL_SUPER
    assert S % DIL_SUPER == 0 and S // max(d for _, d in DILATED_PATTERNS) >= 2 * DIL_BLOCK
    scale = HEAD_DIM ** -0.5
    npat = len(DILATED_PATTERNS)

    def body(q_ref, k_ref, v_ref, o_ref, l_ref, *pattern_refs):
        op_refs, lp_refs = pattern_refs[:npat], pattern_refs[npat:]
        b = pl.program_id(1)
        masks = _head_masks((DIL_BLOCK, LANES))

        def block(pi, dil, c, n, gn):
            ws = jnp.maximum(gn - 1, 0)
            qrows = n * (DIL_BLOCK * dil) + c
            krows = ws * (DIL_BLOCK * dil) + c
            q_idx = _dil_rows(qrows, DIL_BLOCK, dil)
            k_idx = _dil_rows(krows, 2 * DIL_BLOCK, dil)
            qb = q_ref[q_idx, :]
            kb = k_ref[k_idx, :].astype(BF16)
            vb = v_ref[k_idx, :].astype(BF16)
            valid = _dil_valid(gn, ws)
            o_blk = jnp.zeros((DIL_BLOCK, LANES), F32)
            l_blk = jnp.zeros((DIL_BLOCK, LANES), F32)
            for hm in masks:
                qh = jnp.where(hm, qb, 0.0).astype(BF16)
                z = lax.dot_general(qh, kb, (((1,), (1,)), ((), ())), preferred_element_type=F32) * scale
                z = jnp.where(valid, z, NEG_BIG)
                m = jnp.max(z, axis=1, keepdims=True)
                p = jnp.exp(z - m)
                den = jnp.sum(p, axis=1, keepdims=True)
                acc = jnp.dot(p.astype(BF16), vb, preferred_element_type=F32)
                o_blk = jnp.where(hm, acc / den, o_blk)
                l_blk = jnp.where(hm, m + jnp.log(den), l_blk)
            op_refs[pi][q_idx, :] = o_blk
            lp_refs[pi][q_idx, :] = l_blk

        _dil_blocks(b, block)
        lses = [r[...] for r in lp_refs]
        top = functools.reduce(jnp.maximum, lses)
        ws_ = [jnp.exp(l - top) for l in lses]
        den = functools.reduce(jnp.add, ws_)
        num = functools.reduce(jnp.add, [w * r[...] for r, w in zip(op_refs, ws_)])
        o_ref[...] = num / den
        l_ref[...] = top + jnp.log(den)

    blk = pl.BlockSpec((DIL_SUPER, LANES), lambda p, b: (b, p))
    full = pl.BlockSpec((S, LANES), lambda p, b: (0, p))
    return pl.pallas_call(
        body, name=name, grid=(npair, nsuper),
        out_shape=[jax.ShapeDtypeStruct((S, W), F32)] * 2,
        in_specs=[blk, full, full], out_specs=[blk, blk],
        scratch_shapes=[pltpu.VMEM((DIL_SUPER, LANES), F32)] * (2 * npat),
        compiler_params=_cparams(("parallel", "arbitrary")),
    )(q, k, v)


def _dl_bwd(q, k, v, o, lse, do, *, name):
    S, W = q.shape
    npair = W // LANES
    nsuper = S // DIL_SUPER
    scale = HEAD_DIM ** -0.5

    def body(q_ref, k_ref, v_ref, o_ref, l_ref, do_ref, dq_ref, dk_ref, dv_ref, delta_ref):
        b = pl.program_id(1)

        @pl.when(b == 0)
        def _():
            dk_ref[...] = jnp.zeros_like(dk_ref)
            dv_ref[...] = jnp.zeros_like(dv_ref)

        dq_ref[...] = jnp.zeros_like(dq_ref)
        prod = do_ref[...] * o_ref[...]
        delta = jnp.zeros_like(prod)
        for hm in _head_masks(prod.shape):
            delta = jnp.where(hm, jnp.sum(jnp.where(hm, prod, 0.0), axis=1, keepdims=True), delta)
        delta_ref[...] = delta
        masks = _head_masks((DIL_BLOCK, LANES))
        kmasks = _head_masks((2 * DIL_BLOCK, LANES))

        def block(pi, dil, c, n, gn):
            ws = jnp.maximum(gn - 1, 0)
            qrows = n * (DIL_BLOCK * dil) + c
            krows = ws * (DIL_BLOCK * dil) + c
            q_idx = _dil_rows(qrows, DIL_BLOCK, dil)
            k_idx = _dil_rows(krows, 2 * DIL_BLOCK, dil)
            qb = q_ref[q_idx, :]
            dob = do_ref[q_idx, :]
            lb = l_ref[q_idx, :]
            db = delta_ref[q_idx, :]
            kb = k_ref[k_idx, :].astype(BF16)
            vb = v_ref[k_idx, :].astype(BF16)
            valid = _dil_valid(gn, ws)
            dq_blk = jnp.zeros((DIL_BLOCK, LANES), F32)
            dk_blk = jnp.zeros((2 * DIL_BLOCK, LANES), F32)
            dv_blk = jnp.zeros((2 * DIL_BLOCK, LANES), F32)
            for h, (hm, km) in enumerate(zip(masks, kmasks)):
                qh = jnp.where(hm, qb, 0.0).astype(BF16)
                doh = jnp.where(hm, dob, 0.0).astype(BF16)
                lse_h = lb[:, HEAD_DIM * h:HEAD_DIM * h + 1]
                delta_h = db[:, HEAD_DIM * h:HEAD_DIM * h + 1]
                z = lax.dot_general(qh, kb, (((1,), (1,)), ((), ())), preferred_element_type=F32) * scale
                p = jnp.where(valid, jnp.exp(jnp.where(valid, z, NEG_BIG) - lse_h), 0.0)
                dp = lax.dot_general(doh, vb, (((1,), (1,)), ((), ())), preferred_element_type=F32)
                dzb = (p * (dp - delta_h) * scale).astype(BF16)
                dq_blk = jnp.where(hm, jnp.dot(dzb, kb, preferred_element_type=F32), dq_blk)
                dk_blk = dk_blk + lax.dot_general(dzb, qh, (((0,), (0,)), ((), ())), preferred_element_type=F32)
                dv_blk = dv_blk + lax.dot_general(p.astype(BF16), doh, (((0,), (0,)), ((), ())),
                                                  preferred_element_type=F32)
            dq_ref[q_idx, :] = dq_ref[q_idx, :] + dq_blk
            dk_ref[k_idx, :] = dk_ref[k_idx, :] + dk_blk
            dv_ref[k_idx, :] = dv_ref[k_idx, :] + dv_blk

        _dil_blocks(b, block)

    blk = pl.BlockSpec((DIL_SUPER, LANES), lambda p, b: (b, p))
    full = pl.BlockSpec((S, LANES), lambda p, b: (0, p))
    return pl.pallas_call(
        body, name=name, grid=(npair, nsuper),
        out_shape=[jax.ShapeDtypeStruct((S, W), F32)] * 3,
        in_specs=[blk, full, full, blk, blk, blk], out_specs=[blk, full, full],
        scratch_shapes=[pltpu.VMEM((DIL_SUPER, LANES), F32)],
        compiler_params=_cparams(("parallel", "arbitrary")),
    )(q, k, v, o, lse, do)


def _loss_head(x, gain, target, *, name):
    S, D = x.shape
    tm = _pick(S, (512, 256, 128))

    def body(x_ref, g_ref, t_ref, dx_ref, dg_ref, loss_ref):
        first = pl.program_id(0) == 0
        xh, r = _rms_hat(x_ref[...])
        g = g_ref[...]
        err = xh * g - t_ref[...]
        dy = err * (1.0 / D)
        dxh = dy * g
        dx_ref[...] = r * (dxh - xh * jnp.mean(dxh * xh, axis=-1, keepdims=True))
        dg_part = jnp.sum(dy * xh, axis=0, keepdims=True)
        loss_part = jnp.zeros((1, LANES), F32) + 0.5 * jnp.sum(jnp.mean(err * err, axis=-1, keepdims=True),
                                                               axis=0, keepdims=True)

        @pl.when(first)
        def _():
            dg_ref[...] = dg_part
            loss_ref[...] = loss_part

        @pl.when(jnp.logical_not(first))
        def _():
            dg_ref[...] += dg_part
            loss_ref[...] += loss_part

    row = pl.BlockSpec((tm, D), lambda i: (i, 0))
    vec = pl.BlockSpec((1, D), lambda i: (0, 0))
    return pl.pallas_call(
        body, name=name, grid=(S // tm,),
        out_shape=[jax.ShapeDtypeStruct((S, D), F32), jax.ShapeDtypeStruct((1, D), F32),
                   jax.ShapeDtypeStruct((1, LANES), F32)],
        in_specs=[row, vec, row], out_specs=[row, vec, pl.BlockSpec((1, LANES), lambda i: (0, 0))],
        compiler_params=_cparams(("arbitrary",)),
    )(x, gain, target)


def _local_step(x, target, gains, weights):
    S, D = x.shape
    d_sb = gains["sb_out_norm"].shape[1]
    d_dl = gains["dil_out_norm"].shape[1]
    w_in = weights["w_in"]
    w_in_sb, w_in_dl = w_in[:, :3 * d_sb], w_in[:, 3 * d_sb:]
    w_out = weights["w_out"]
    cos_t, sin_t = _rope_tables(S)

    x1, saved1 = _ffn_fwd(x, gains["ffn1_norm"], weights["ffn1_w_gate"], weights["ffn1_w_up"],
                          weights["ffn1_w_down"], tag="ffn1")
    h2 = _rms_fwd([x1], [gains["mix_norm"]], name="mix_norm")
    p_sb = _mm(h2, w_in_sb, outs=(BF16,), name="proj_sb")
    p_dl = _mm(h2, w_in_dl, name="proj_dl")
    q_dl = _rotary(p_dl, cos_t, sin_t, col0=0, width=d_dl, sign=1.0, name="rope_q")
    k_dl = _rotary(p_dl, cos_t, sin_t, col0=d_dl, width=d_dl, sign=1.0, name="rope_k")
    v_dl = p_dl[:, 2 * d_dl:]
    o_sb, carries = _sb_fwd(p_sb, name="sb_fwd")
    o_dl, lse_dl = _dl_fwd(q_dl, k_dl, v_dl, name="dl_fwd")
    merged = _rms_fwd([o_sb, o_dl], [gains["sb_out_norm"], gains["dil_out_norm"]], name="out_norm")
    x2 = _mm(merged, w_out, res=x1, name="out_proj")
    x3, saved2 = _ffn_fwd(x2, gains["ffn2_norm"], weights["ffn2_w_gate"], weights["ffn2_w_up"],
                          weights["ffn2_w_down"], tag="ffn2")
    dx3, d_final, loss_row = _loss_head(x3, gains["final_norm"], target, name="loss_head")

    dx2, d_ffn2_norm, dwg2, dwu2, dwd2 = _ffn_bwd(dx3, x2, gains["ffn2_norm"], weights["ffn2_w_gate"],
                                                  weights["ffn2_w_up"], weights["ffn2_w_down"], saved2, tag="ffn2")
    d_w_out = _mm(merged, dx2, ta=True, name="d_w_out")
    d_merged = _mm(dx2, w_out, tb=True, name="d_merged")
    (do_sb, do_dl), (d_sb_norm, d_dl_norm) = _rms_bwd(
        d_merged, [o_sb, o_dl], [gains["sb_out_norm"], gains["dil_out_norm"]], None, name="out_norm_bwd")
    dq_sb, dk_sb, dv_sb = _sb_bwd(p_sb, do_sb, carries, name="sb_bwd")
    dq_dl, dk_dl, dv_dl = _dl_bwd(q_dl, k_dl, v_dl, o_dl, lse_dl, do_dl, name="dl_bwd")
    dq_dl = _rotary(dq_dl, cos_t, sin_t, col0=0, width=d_dl, sign=-1.0, name="rope_dq")
    dk_dl = _rotary(dk_dl, cos_t, sin_t, col0=0, width=d_dl, sign=-1.0, name="rope_dk")
    pieces = [dq_sb, dk_sb, dv_sb, dq_dl, dk_dl, dv_dl]
    d_w_in = jnp.concatenate([_mm(h2, p, ta=True, name=f"d_w_in_{i}") for i, p in enumerate(pieces)], axis=1)
    dh2 = None
    col = 0
    for i, p in enumerate(pieces):
        dh2 = _mm(p, w_in[:, col:col + p.shape[1]], tb=True, res=dh2, name=f"dh_mix_{i}")
        col += p.shape[1]
    (dx1,), (d_mix_norm,) = _rms_bwd(dh2, [x1], [gains["mix_norm"]], dx2, name="mix_norm_bwd")
    dx, d_ffn1_norm, dwg1, dwu1, dwd1 = _ffn_bwd(dx1, x, gains["ffn1_norm"], weights["ffn1_w_gate"],
                                                 weights["ffn1_w_up"], weights["ffn1_w_down"], saved1, tag="ffn1")
    gain_grads = dict(ffn1_norm=d_ffn1_norm, mix_norm=d_mix_norm, sb_out_norm=d_sb_norm, dil_out_norm=d_dl_norm,
                      ffn2_norm=d_ffn2_norm, final_norm=d_final)
    weight_grads = dict(ffn1_w_gate=dwg1, ffn1_w_up=dwu1, ffn1_w_down=dwd1, w_in=d_w_in, w_out=d_w_out,
                        ffn2_w_gate=dwg2, ffn2_w_up=dwu2, ffn2_w_down=dwd2)
    return loss_row, dx, gain_grads, weight_grads


def _mesh_position():
    return lax.axis_index("x"), lax.axis_index("y"), lax.axis_index("c")


def _flip(coord, bit):
    return 1 - coord if bit else coord


RELATIONS = [(rx, ry, rc) for rx in (0, 1) for ry in (0, 1) for rc in (0, 1)][1:]


def _all_gather(shard, *, name):
    R, C = shard.shape

    def body(x_ref, out_ref, send_sems, recv_sems, local_sem):
        x, y, c = _mesh_position()
        me, sibling = (x, y, c), (x, y, 1 - c)
        chips = [(1 - x, y), (x, 1 - y), (1 - x, 1 - y)]

        def slot(px, py, pc):
            return out_ref.at[4 * px + 2 * py + pc]

        def copy(k, block, to, src=None):
            return pltpu.make_async_remote_copy(
                src_ref=slot(*block) if src is None else src, dst_ref=slot(*block),
                send_sem=send_sems.at[k], recv_sem=recv_sems.at[k],
                device_id=to, device_id_type=pl.DeviceIdType.MESH)

        mine = pltpu.make_async_copy(x_ref, slot(*me), local_sem)
        mine.start()
        first = [copy(0, me, sibling, src=x_ref)]
        first += [copy(1 + j, me, (*chip, c), src=x_ref) for j, chip in enumerate(chips)]
        for cp in first:
            cp.start()
        passed = [copy(4 + j, (*chip, c), sibling) for j, chip in enumerate(chips)]
        for j, chip in enumerate(chips):
            copy(1 + j, (*chip, c), me).wait_recv()
            passed[j].start()
        copy(0, sibling, me).wait_recv()
        for j, chip in enumerate(chips):
            copy(4 + j, (*chip, 1 - c), me).wait_recv()
        for cp in first + passed:
            cp.wait_send()
        mine.wait()

    return pl.pallas_call(
        body, name=name,
        out_shape=jax.ShapeDtypeStruct((N_DEV, R, C), shard.dtype),
        in_specs=[pl.BlockSpec(memory_space=pl.ANY)],
        out_specs=pl.BlockSpec(memory_space=pl.ANY),
        scratch_shapes=[pltpu.SemaphoreType.DMA((7,)), pltpu.SemaphoreType.DMA((7,)), pltpu.SemaphoreType.DMA],
    )(shard)


def _exchange_chunks(packs, *, name):
    n = len(packs)

    def body(*refs):
        in_refs, out_refs = refs[:n], refs[n:2 * n]
        send_sems, recv_sems, local_sems = refs[2 * n:]
        x, y, c = _mesh_position()
        me = 4 * x + 2 * y + c
        copies = []
        for t in range(n):
            local = pltpu.make_async_copy(in_refs[t].at[me], out_refs[t].at[me], local_sems.at[t])
            local.start()
            copies.append(local)
        for r, (rx, ry, rc) in enumerate(RELATIONS):
            px, py, pc = _flip(x, rx), _flip(y, ry), _flip(c, rc)
            peer = 4 * px + 2 * py + pc
            for t in range(n):
                cp = pltpu.make_async_remote_copy(
                    src_ref=in_refs[t].at[peer], dst_ref=out_refs[t].at[me],
                    send_sem=send_sems.at[t, r], recv_sem=recv_sems.at[t, r],
                    device_id=(px, py, pc), device_id_type=pl.DeviceIdType.MESH)
                cp.start()
                copies.append(cp)
        for cp in copies:
            cp.wait()

    return pl.pallas_call(
        body, name=name,
        out_shape=[jax.ShapeDtypeStruct(p.shape, p.dtype) for p in packs],
        in_specs=[pl.BlockSpec(memory_space=pl.ANY)] * n,
        out_specs=[pl.BlockSpec(memory_space=pl.ANY)] * n,
        scratch_shapes=[pltpu.SemaphoreType.DMA((n, 7)), pltpu.SemaphoreType.DMA((n, 7)),
                        pltpu.SemaphoreType.DMA((n,))],
    )(*packs)


def _all_reduce_rows(v, *, name):
    R, C = v.shape

    def body(v_ref, out_ref, buf, send_sems, recv_sems):
        x, y, c = _mesh_position()
        me = 4 * x + 2 * y + c
        buf[me] = v_ref[...]
        copies = []
        for r, (rx, ry, rc) in enumerate(RELATIONS):
            cp = pltpu.make_async_remote_copy(
                src_ref=v_ref, dst_ref=buf.at[me], send_sem=send_sems.at[r], recv_sem=recv_sems.at[r],
                device_id=(_flip(x, rx), _flip(y, ry), _flip(c, rc)), device_id_type=pl.DeviceIdType.MESH)
            cp.start()
            copies.append(cp)
        for cp in copies:
            cp.wait()
        total = buf[0]
        for s in range(1, N_DEV):
            total = total + buf[s]
        out_ref[...] = total

    return pl.pallas_call(
        body, name=name,
        out_shape=jax.ShapeDtypeStruct((R, C), F32),
        in_specs=[pl.BlockSpec(memory_space=pltpu.VMEM)],
        out_specs=pl.BlockSpec(memory_space=pltpu.VMEM),
        scratch_shapes=[pltpu.VMEM((N_DEV, R, C), F32), pltpu.SemaphoreType.DMA((7,)), pltpu.SemaphoreType.DMA((7,))],
    )(v)


def _sum_slots(recv, *, name):
    _, R, C = recv.shape
    tr = _pick(R, (256, 208, 128, 64, 32, 16))

    def body(r_ref, o_ref):
        total = r_ref[0].astype(F32)
        for s in range(1, N_DEV):
            total = total + r_ref[s].astype(F32)
        o_ref[...] = total

    return pl.pallas_call(
        body, name=name, grid=(R // tr,),
        out_shape=jax.ShapeDtypeStruct((R, C), F32),
        in_specs=[pl.BlockSpec((N_DEV, tr, C), lambda i: (0, i, 0))],
        out_specs=pl.BlockSpec((tr, C), lambda i: (i, 0)),
        compiler_params=_cparams(("parallel",)),
    )(recv)


def _adamw(w, g, m, v, *, name):
    R, C = w.shape
    tr = _pick(R, (256, 128, 64, 32, 16, 8))

    def body(w_ref, g_ref, m_ref, v_ref, d_ref, nm_ref, nv_ref):
        g = g_ref[...]
        m_new = ADAM_B1 * m_ref[...] + (1.0 - ADAM_B1) * g
        v_new = ADAM_B2 * v_ref[...] + (1.0 - ADAM_B2) * (g * g)
        m_hat = m_new / (1.0 - ADAM_B1 ** ADAM_STEP)
        v_hat = v_new / (1.0 - ADAM_B2 ** ADAM_STEP)
        d_ref[...] = -ADAM_LR * (m_hat / (jnp.sqrt(v_hat) + ADAM_EPS) + ADAM_WD * w_ref[...])
        nm_ref[...] = m_new
        nv_ref[...] = v_new

    spec = pl.BlockSpec((tr, C), lambda i: (i, 0))
    return pl.pallas_call(
        body, name=name, grid=(R // tr,),
        out_shape=[jax.ShapeDtypeStruct((R, C), F32)] * 3,
        in_specs=[spec] * 4, out_specs=[spec] * 3,
        compiler_params=_cparams(("parallel",)),
    )(w, g, m, v)


WEIGHT_NAMES = ["ffn1_norm", "ffn1_w_gate", "ffn1_w_up", "ffn1_w_down", "mix_norm", "w_in", "sb_out_norm",
                "dil_out_norm", "w_out", "ffn2_norm", "ffn2_w_gate", "ffn2_w_up", "ffn2_w_down", "final_norm"]
GAIN_NAMES = ["ffn1_norm", "mix_norm", "sb_out_norm", "dil_out_norm", "ffn2_norm", "final_norm"]
COL_SHARDED = ["ffn1_w_gate", "ffn1_w_up", "ffn2_w_gate", "ffn2_w_up", "w_in"]
ROW_SHARDED = ["ffn1_w_down", "ffn2_w_down", "w_out"]


def _step(x, target, params, moments_m, moments_v):
    col_pack = jnp.concatenate([params[n] for n in COL_SHARDED], axis=1).astype(BF16)
    row_pack = jnp.concatenate([params[n] for n in ROW_SHARDED], axis=0).astype(BF16)
    col_all = _all_gather(col_pack, name="gather_col")
    row_all = _all_gather(row_pack, name="gather_row")
    weights = {}
    off = 0
    for n in COL_SHARDED:
        w = params[n].shape[1]
        piece = col_all[:, :, off:off + w]
        weights[n] = jnp.transpose(piece, (1, 0, 2)).reshape(piece.shape[1], N_DEV * w)
        off += w
    off = 0
    for n in ROW_SHARDED:
        r = params[n].shape[0]
        weights[n] = row_all[:, off:off + r, :].reshape(N_DEV * r, row_all.shape[2])
        off += r

    gains = {n: params[n] for n in GAIN_NAMES}
    loss_row, grad_x, gain_grads, weight_grads = _local_step(x, target, gains, weights)

    col_chunks, row_chunks = [], []
    for n in COL_SHARDED:
        g = weight_grads[n]
        w = params[n].shape[1]
        col_chunks.append(jnp.transpose(g.reshape(g.shape[0], N_DEV, w), (1, 0, 2)))
    for n in ROW_SHARDED:
        g = weight_grads[n]
        r = params[n].shape[0]
        row_chunks.append(g.reshape(N_DEV, r, g.shape[1]))
    col_send = jnp.concatenate(col_chunks, axis=2).astype(BF16)
    row_send = jnp.concatenate(row_chunks, axis=1).astype(BF16)
    col_recv, row_recv = _exchange_chunks([col_send, row_send], name="exchange_grads")
    col_grad = _sum_slots(col_recv, name="sum_col_grads")
    row_grad = _sum_slots(row_recv, name="sum_row_grads")
    grads = {}
    off = 0
    for n in COL_SHARDED:
        w = params[n].shape[1]
        grads[n] = col_grad[:, off:off + w]
        off += w
    off = 0
    for n in ROW_SHARDED:
        r = params[n].shape[0]
        grads[n] = row_grad[off:off + r, :]
        off += r

    rows = [gain_grads[n].reshape(-1, LANES) for n in GAIN_NAMES] + [loss_row]
    small = jnp.concatenate(rows, axis=0)
    pad = (-small.shape[0]) % 8
    small = jnp.pad(small, ((0, pad), (0, 0)))
    small = _all_reduce_rows(small, name="reduce_gains_loss")
    off = 0
    for n in GAIN_NAMES:
        r = gain_grads[n].shape[1] // LANES
        grads[n] = small[off:off + r].reshape(1, -1)
        off += r
    loss = small[off, 0]

    delta, new_m, new_v = {}, {}, {}
    for n in WEIGHT_NAMES:
        delta[n], new_m[n], new_v[n] = _adamw(params[n], grads[n], moments_m[n], moments_v[n], name=f"adamw_{n}")
    return loss, grad_x, grads, delta, new_m, new_v


def kernel(x, ffn1_norm, ffn1_w_gate, ffn1_w_up, ffn1_w_down, mix_norm, w_in, sb_out_norm, dil_out_norm, w_out, ffn2_norm, ffn2_w_gate, ffn2_w_up, ffn2_w_down, final_norm, loss_target, m_ffn1_norm, m_ffn1_w_gate, m_ffn1_w_up, m_ffn1_w_down, m_mix_norm, m_w_in, m_sb_out_norm, m_dil_out_norm, m_w_out, m_ffn2_norm, m_ffn2_w_gate, m_ffn2_w_up, m_ffn2_w_down, m_final_norm, v_ffn1_norm, v_ffn1_w_gate, v_ffn1_w_up, v_ffn1_w_down, v_mix_norm, v_w_in, v_sb_out_norm, v_dil_out_norm, v_w_out, v_ffn2_norm, v_ffn2_w_gate, v_ffn2_w_up, v_ffn2_w_down, v_final_norm):
    given = dict(locals())
    shapes = {n: given[n].shape for n in WEIGHT_NAMES}

    def as2d(a):
        return a.reshape(1, -1) if a.ndim == 1 else a.reshape(a.shape[-2], a.shape[-1])

    params = {n: as2d(given[n]) for n in WEIGHT_NAMES}
    moments_m = {n: as2d(given["m_" + n]) for n in WEIGHT_NAMES}
    moments_v = {n: as2d(given["v_" + n]) for n in WEIGHT_NAMES}
    loss, grad_x, grads, delta, new_m, new_v = _step(x[0], loss_target[0], params, moments_m, moments_v)
    back = lambda d: [d[n].reshape(shapes[n]) for n in WEIGHT_NAMES]
    return (loss, grad_x[None], *back(grads), *back(delta), *back(new_m), *back(new_v))
```

```python
import functools

import jax
import jax.numpy as jnp
from jax import lax
from jax.experimental import pallas as pl
from jax.experimental.pallas import tpu as pltpu

F32 = jnp.float32
BF16 = jnp.bfloat16

N_DEV = 8
HEAD_DIM = 64
LANES = 128
DILATED_PATTERNS = ((128, 1), (512, 4), (2048, 16))
DIL_BLOCK = 128
DIL_SUPER = 2048
DIL_UNROLL = 4
SB_TILE = 256
ROPE_THETA = 10000.0
RMS_EPS = 1e-6
HALF_STEP = 0.5
ADAM_LR = 0.001
ADAM_B1 = 0.9
ADAM_B2 = 0.999
ADAM_EPS = 1e-08
ADAM_WD = 0.01
ADAM_STEP = 10
NEG_BIG = -1e30
VMEM_CAP_MB = 60


def _pick(n, prefs):
    for p in prefs:
        if n % p == 0:
            return p
    return n


MM_MAX_TILE = 1536


def _largest_tile(n, cap):
    if n <= cap:
        return n
    for t in range(cap - cap % LANES, 0, -LANES):
        if n % t == 0:
            return t
    return n


def _cparams(sem=None, vmem_mb=48):
    return pltpu.CompilerParams(dimension_semantics=sem, vmem_limit_bytes=min(vmem_mb, VMEM_CAP_MB) * 1024 * 1024)


def _nbytes(shape, dtype):
    n = 1
    for s in shape:
        n *= s
    return n * jnp.dtype(dtype).itemsize


def _mm(a, b, *, name, ta=False, tb=False, outs=(F32,), res=None, alpha=1.0, extras=(), epilogue=None,
        tm=None, tn=None, tk=None):
    if ta:
        K, M = a.shape
    else:
        M, K = a.shape
    if tb:
        N, Kb = b.shape
    else:
        Kb, N = b.shape
    assert K == Kb, (a.shape, b.shape, ta, tb)
    tm = tm or (_largest_tile(M, MM_MAX_TILE) if ta else _pick(M, (512, 256, 128)))
    tn = tn or _largest_tile(N, MM_MAX_TILE)
    tk = tk or (K if K <= 3072 else _pick(K, (1024, 512, 256, 128)))
    nk = K // tk
    a_spec = pl.BlockSpec((tk, tm), lambda i, j, k: (k, i)) if ta else pl.BlockSpec((tm, tk), lambda i, j, k: (i, k))
    b_spec = pl.BlockSpec((tn, tk), lambda i, j, k: (j, k)) if tb else pl.BlockSpec((tk, tn), lambda i, j, k: (k, j))
    mn_spec = pl.BlockSpec((tm, tn), lambda i, j, k: (i, j))
    dims = (((0 if ta else 1,), (1 if tb else 0,)), ((), ()))
    n_extra = len(extras) + (1 if res is not None else 0)
    n_out = len(outs)

    def body(*refs):
        a_ref, b_ref = refs[0], refs[1]
        in_refs = refs[2:2 + n_extra]
        out_refs = refs[2 + n_extra:2 + n_extra + n_out]
        prod = lax.dot_general(a_ref[...].astype(BF16), b_ref[...].astype(BF16), dims, preferred_element_type=F32)

        def finish(acc):
            blocks = [r[...] for r in in_refs]
            if res is not None:
                r_blk, blocks = blocks[0], blocks[1:]
            else:
                r_blk = None
            if epilogue is None:
                val = acc * alpha
                if r_blk is not None:
                    val = val + r_blk
                vals = (val,)
            else:
                vals = epilogue(acc, r_blk, *blocks)
            for o_ref, v in zip(out_refs, vals):
                o_ref[...] = v.astype(o_ref.dtype)

        if nk == 1:
            finish(prod)
        else:
            acc_ref = refs[2 + n_extra + n_out]
            k = pl.program_id(2)

            @pl.when(k == 0)
            def _():
                acc_ref[...] = prod

            @pl.when(k > 0)
            def _():
                acc_ref[...] += prod

            @pl.when(k == nk - 1)
            def _():
                finish(acc_ref[...])

    operands = [a, b] + ([res] if res is not None else []) + list(extras)
    in_specs = [a_spec, b_spec] + [mn_spec] * n_extra
    est = 2 * (_nbytes((tm, tk), a.dtype) + _nbytes((tk, tn), b.dtype))
    est += 2 * sum(_nbytes((tm, tn), o.dtype) for o in operands[2:])
    est += 2 * sum(_nbytes((tm, tn), d) for d in outs) + _nbytes((tm, tn), F32)
    result = pl.pallas_call(
        body, name=name, grid=(M // tm, N // tn, nk),
        out_shape=[jax.ShapeDtypeStruct((M, N), d) for d in outs],
        in_specs=in_specs, out_specs=[mn_spec] * n_out,
        scratch_shapes=[pltpu.VMEM((tm, tn), F32)] if nk > 1 else [],
        compiler_params=_cparams(("parallel", "parallel", "arbitrary"), vmem_mb=max(32, 2 * est // (1024 * 1024))),
    )(*operands)
    return result[0] if n_out == 1 else result


def _rms_hat(x):
    r = lax.rsqrt(jnp.mean(x * x, axis=-1, keepdims=True) + RMS_EPS)
    return x * r, r


def _rms_fwd(xs, gains, *, name):
    S = xs[0].shape[0]
    widths = [x.shape[1] for x in xs]
    tm = _pick(S, (512, 256, 128))
    n = len(xs)

    def body(*refs):
        o_ref = refs[2 * n]
        off = 0
        for i in range(n):
            xh, _ = _rms_hat(refs[i][...])
            o_ref[:, off:off + widths[i]] = (xh * refs[n + i][...]).astype(o_ref.dtype)
            off += widths[i]

    return pl.pallas_call(
        body, name=name, grid=(S // tm,),
        out_shape=jax.ShapeDtypeStruct((S, sum(widths)), BF16),
        in_specs=[pl.BlockSpec((tm, w), lambda i: (i, 0)) for w in widths]
        + [pl.BlockSpec((1, w), lambda i: (0, 0)) for w in widths],
        out_specs=pl.BlockSpec((tm, sum(widths)), lambda i: (i, 0)),
        compiler_params=_cparams(("parallel",)),
    )(*xs, *gains)


def _rms_bwd(dh, xs, gains, res, *, name):
    S = xs[0].shape[0]
    widths = [x.shape[1] for x in xs]
    tm = _pick(S, (512, 256, 128))
    n = len(xs)
    has_res = res is not None

    def body(*refs):
        dh_ref = refs[0]
        x_refs = refs[1:1 + n]
        g_refs = refs[1 + n:1 + 2 * n]
        r_ref = refs[1 + 2 * n] if has_res else None
        base = 1 + 2 * n + (1 if has_res else 0)
        dx_refs = refs[base:base + n]
        dg_refs = refs[base + n:base + 2 * n]
        first = pl.program_id(0) == 0
        off = 0
        for i in range(n):
            x = x_refs[i][...]
            xh, r = _rms_hat(x)
            d = dh_ref[:, off:off + widths[i]]
            dxh = d * g_refs[i][...]
            dx = r * (dxh - xh * jnp.mean(dxh * xh, axis=-1, keepdims=True))
            if has_res:
                dx = dx + r_ref[...]
            dx_refs[i][...] = dx
            part = jnp.sum(d * xh, axis=0, keepdims=True)

            @pl.when(first)
            def _(i=i, part=part):
                dg_refs[i][...] = part

            @pl.when(jnp.logical_not(first))
            def _(i=i, part=part):
                dg_refs[i][...] += part

            off += widths[i]

    in_specs = [pl.BlockSpec((tm, sum(widths)), lambda i: (i, 0))]
    in_specs += [pl.BlockSpec((tm, w), lambda i: (i, 0)) for w in widths]
    in_specs += [pl.BlockSpec((1, w), lambda i: (0, 0)) for w in widths]
    operands = [dh, *xs, *gains]
    if has_res:
        in_specs.append(pl.BlockSpec((tm, widths[0]), lambda i: (i, 0)))
        operands.append(res)
    out = pl.pallas_call(
        body, name=name, grid=(S // tm,),
        out_shape=[jax.ShapeDtypeStruct((S, w), F32) for w in widths] + [jax.ShapeDtypeStruct((1, w), F32) for w in widths],
        in_specs=in_specs,
        out_specs=[pl.BlockSpec((tm, w), lambda i: (i, 0)) for w in widths]
        + [pl.BlockSpec((1, w), lambda i: (0, 0)) for w in widths],
        compiler_params=_cparams(("arbitrary",)),
    )(*operands)
    return out[:n], out[n:]


def _sigmoid(g):
    return 1.0 / (1.0 + jnp.exp(-g))


def _ffn_fwd(x, gain, wg, wu, wd, *, tag):
    h = _rms_fwd([x], [gain], name=f"{tag}_norm")
    g = _mm(h, wg, outs=(BF16,), name=f"{tag}_gate")

    def act(acc, _, g_blk):
        gf = g_blk.astype(F32)
        return acc, gf * _sigmoid(gf) * acc

    u, a = _mm(h, wu, outs=(BF16, BF16), extras=(g,), epilogue=act, name=f"{tag}_up_act")
    y = _mm(a, wd, res=x, alpha=HALF_STEP, name=f"{tag}_down")
    return y, (h, g, u, a)


def _ffn_bwd(dout, x, gain, wg, wu, wd, saved, *, tag):
    h, g, u, a = saved

    def act_bwd(acc, _, g_blk, u_blk):
        gf, uf = g_blk.astype(F32), u_blk.astype(F32)
        da = acc * HALF_STEP
        sig = _sigmoid(gf)
        silu = gf * sig
        return da * uf * (sig * (1.0 + gf * (1.0 - sig))), da * silu

    dg, du = _mm(dout, wd, tb=True, outs=(BF16, BF16), extras=(g, u), epilogue=act_bwd, name=f"{tag}_bwd_act")
    dwg = _mm(h, dg, ta=True, name=f"{tag}_dwg")
    dwu = _mm(h, du, ta=True, name=f"{tag}_dwu")
    dwd = _mm(a, dout, ta=True, alpha=HALF_STEP, name=f"{tag}_dwd")
    dh = _mm(dg, wg, tb=True, name=f"{tag}_dh_gate")
    dh = _mm(du, wu, tb=True, res=dh, name=f"{tag}_dh_up")
    (dx,), (dgain,) = _rms_bwd(dh, [x], [gain], dout, name=f"{tag}_norm_bwd")
    return dx, dgain, dwg, dwu, dwd


def _rope_tables(S):
    half = HEAD_DIM // 2
    inv_freq = ROPE_THETA ** (-jnp.arange(half, dtype=F32) / half)
    ang = jnp.arange(S, dtype=F32)[:, None] * inv_freq[None, :]
    cos, sin = jnp.cos(ang), jnp.sin(ang)
    reps = LANES // HEAD_DIM
    cos_t = jnp.tile(jnp.concatenate([cos, cos], axis=1), (1, reps))
    sin_t = jnp.tile(jnp.concatenate([-sin, sin], axis=1), (1, reps))
    return cos_t, sin_t


def _rotary(t, cos_t, sin_t, *, col0, width, sign, name):
    S = t.shape[0]
    tm = _pick(S, (512, 256, 128))
    half = HEAD_DIM // 2
    c0 = col0 // LANES

    def body(t_ref, c_ref, s_ref, o_ref):
        v = t_ref[...]
        lane = lax.broadcasted_iota(jnp.int32, v.shape, 1)
        swapped = jnp.where(lane % HEAD_DIM < half, pltpu.roll(v, LANES - half, axis=1), pltpu.roll(v, half, axis=1))
        o_ref[...] = v * c_ref[...] + swapped * (s_ref[...] * sign)

    return pl.pallas_call(
        body, name=name, grid=(S // tm, width // LANES),
        out_shape=jax.ShapeDtypeStruct((S, width), F32),
        in_specs=[pl.BlockSpec((tm, LANES), lambda i, j: (i, c0 + j)),
                  pl.BlockSpec((tm, LANES), lambda i, j: (i, 0)),
                  pl.BlockSpec((tm, LANES), lambda i, j: (i, 0))],
        out_specs=pl.BlockSpec((tm, LANES), lambda i, j: (i, j)),
        compiler_params=_cparams(("parallel", "parallel")),
    )(t, cos_t, sin_t)


def _head_masks(shape):
    lane = lax.broadcasted_iota(jnp.int32, shape, 1)
    return [(lane >= HEAD_DIM * h) & (lane < HEAD_DIM * (h + 1)) for h in range(LANES // HEAD_DIM)]


def _sb_scores(q2, k_j):
    z = lax.dot_general(q2, k_j, (((1,), (1,)), ((), ())), preferred_element_type=F32)
    l1p = jnp.log(1.0 + jnp.exp(-jnp.abs(z)))
    log_beta = jnp.minimum(z, 0.0) - l1p
    return log_beta, log_beta - z


def _sb_stack_heads(t, scale=None):
    parts = [jnp.where(hm, t, jnp.zeros_like(t)) for hm in _head_masks(t.shape)]
    t2 = jnp.concatenate(parts, axis=0)
    if scale is not None:
        t2 = (t2.astype(F32) * scale).astype(t2.dtype)
    return t2


def _sb_unstack_heads(t2):
    T = t2.shape[0] // 2
    masks = _head_masks((T, LANES))
    return jnp.where(masks[0], t2[:T], t2[T:])


def _sb_causal(T):
    row = lax.broadcasted_iota(jnp.int32, (2 * T, T), 0)
    col = lax.broadcasted_iota(jnp.int32, (2 * T, T), 1)
    return col < jnp.where(row >= T, row - T, row)


def _sb_triangle(T, later):
    row = lax.broadcasted_iota(jnp.int32, (T, T), 0)
    col = lax.broadcasted_iota(jnp.int32, (T, T), 1)
    return ((row > col) if later else (row < col)).astype(BF16)


def _sb_fwd(p_sb, *, name):
    S = p_sb.shape[0]
    W = p_sb.shape[1] // 3
    npair = W // LANES
    T = SB_TILE
    n_tiles = S // T
    assert n_tiles <= HEAD_DIM
    scale = HEAD_DIM ** -0.5

    def body(q_ref, k_ref, v_ref, o_ref, c_ref):
        I = pl.program_id(1)
        lane = lax.broadcasted_iota(jnp.int32, (T, LANES), 1)
        causal = _sb_causal(T)
        later_than = _sb_triangle(T, True)
        q2 = _sb_stack_heads(q_ref[...], scale)

        def scores(J, diag):
            off = pl.multiple_of(J * T, T)
            log_beta, log_stay = _sb_scores(q2, k_ref[pl.ds(off, T), :])
            if diag:
                log_stay = jnp.where(causal, log_stay, 0.0)
            local = jnp.dot(log_stay.astype(BF16), later_than, preferred_element_type=F32)
            return log_beta, local, jnp.sum(log_stay, axis=1, keepdims=True), v_ref[pl.ds(off, T), :]

        def weigh(J, sc, lc, acc, carr, diag):
            log_beta, local, _, v_j = sc
            w = jnp.exp(log_beta + (local + lc))
            if diag:
                w = jnp.where(causal, w, 0.0)
            acc = acc + jnp.dot(w.astype(BF16), v_j, preferred_element_type=F32)
            carr = jnp.where(lane == J, lc[:T], carr)
            carr = jnp.where(lane == HEAD_DIM + J, lc[T:], carr)
            return acc, carr

        def one(J, state, diag):
            lc, acc, carr = state
            sc = scores(J, diag)
            acc, carr = weigh(J, sc, lc, acc, carr, diag)
            return lc + sc[2], acc, carr

        def two(J, state):
            lc, acc, carr = state
            sc_a, sc_b = scores(J, False), scores(J - 1, False)
            acc, carr = weigh(J, sc_a, lc, acc, carr, False)
            lc = lc + sc_a[2]
            acc, carr = weigh(J - 1, sc_b, lc, acc, carr, False)
            return lc + sc_b[2], acc, carr

        state = (jnp.zeros((2 * T, 1), F32), jnp.zeros((2 * T, LANES), F32), jnp.zeros((T, LANES), F32))
        state = one(I, state, True)
        state = lax.fori_loop(0, I // 2, lambda jj, st: two(I - 1 - 2 * jj, st), state)
        state = lax.cond(I % 2 == 1, lambda st: one(0, st, False), lambda st: st, state)
        _, acc, carr = state
        o_ref[...] = _sb_unstack_heads(acc)
        c_ref[...] = carr

    blk = lambda I_off: pl.BlockSpec((T, LANES), lambda p, I: (I, I_off + p))
    full = lambda off: pl.BlockSpec((S, LANES), lambda p, I: (0, off + p))
    return pl.pallas_call(
        body, name=name, grid=(npair, n_tiles),
        out_shape=[jax.ShapeDtypeStruct((S, W), F32), jax.ShapeDtypeStruct((S, W), F32)],
        in_specs=[blk(0), full(npair), full(2 * npair)],
        out_specs=[blk(0), blk(0)],
        compiler_params=_cparams(("parallel", "arbitrary")),
    )(p_sb, p_sb, p_sb)


def _sb_bwd(p_sb, do, carries, *, name):
    S = p_sb.shape[0]
    W = p_sb.shape[1] // 3
    npair = W // LANES
    T = SB_TILE
    n_tiles = S // T
    scale = HEAD_DIM ** -0.5

    def body(q_ref, k_ref, v_ref, do_ref, c_ref, dq_ref, dk_ref, dv_ref):
        I = pl.program_id(1)

        @pl.when(I == 0)
        def _():
            dk_ref[...] = jnp.zeros_like(dk_ref)
            dv_ref[...] = jnp.zeros_like(dv_ref)

        lane = lax.broadcasted_iota(jnp.int32, (T, LANES), 1)
        causal = _sb_causal(T)
        later_than = _sb_triangle(T, True)
        earlier_than = _sb_triangle(T, False)
        q2 = _sb_stack_heads(q_ref[...], scale)
        do2 = _sb_stack_heads(do_ref[...].astype(BF16))
        carr = c_ref[...]
        tn_dims = (((0,), (0,)), ((), ()))

        def chain(J, diag):
            off = pl.multiple_of(J * T, T)
            k_j = k_ref[pl.ds(off, T), :]
            v_j = v_ref[pl.ds(off, T), :]
            log_beta, log_stay = _sb_scores(q2, k_j)
            if diag:
                log_stay = jnp.where(causal, log_stay, 0.0)
            lc = jnp.concatenate(
                [jnp.sum(jnp.where(lane == HEAD_DIM * h + J, carr, 0.0), axis=1, keepdims=True) for h in range(2)],
                axis=0)
            later = jnp.dot(log_stay.astype(BF16), later_than, preferred_element_type=F32) + lc
            w = jnp.exp(log_beta + later)
            if diag:
                w = jnp.where(causal, w, 0.0)
            dw = lax.dot_general(do2, v_j, (((1,), (1,)), ((), ())), preferred_element_type=F32)
            e = w * dw
            local = jnp.dot(e.astype(BF16), earlier_than, preferred_element_type=F32)
            return off, k_j, w, e, local, jnp.exp(log_beta), jnp.sum(e, axis=1, keepdims=True)

        def finish(ch, ec, dq_acc, diag):
            off, k_j, w, e, local, beta, _ = ch
            e_before = local + ec
            dz = e - beta * (e + e_before)
            if diag:
                dz = jnp.where(causal, dz, 0.0)
            dzb = dz.astype(BF16)
            dq_acc = dq_acc + jnp.dot(dzb, k_j, preferred_element_type=F32)
            dk_ref[pl.ds(off, T), :] += lax.dot_general(dzb, q2, tn_dims, preferred_element_type=F32)
            dv_ref[pl.ds(off, T), :] += lax.dot_general(w.astype(BF16), do2, tn_dims, preferred_element_type=F32)
            return dq_acc

        def one(J, state, diag):
            ec, dq_acc = state
            ch = chain(J, diag)
            return ec + ch[6], finish(ch, ec, dq_acc, diag)

        def two(J, state):
            ec, dq_acc = state
            ch_a, ch_b = chain(J, False), chain(J + 1, False)
            dq_acc = finish(ch_a, ec, dq_acc, False)
            ec = ec + ch_a[6]
            dq_acc = finish(ch_b, ec, dq_acc, False)
            return ec + ch_b[6], dq_acc

        state = (jnp.zeros((2 * T, 1), F32), jnp.zeros((2 * T, LANES), F32))
        state = lax.fori_loop(0, I // 2, lambda jj, st: two(2 * jj, st), state)
        state = lax.cond(I % 2 == 1, lambda st: one(I - 1, st, False), lambda st: st, state)
        _, dq_acc = one(I, state, True)
        dq_ref[...] = _sb_unstack_heads(dq_acc) * scale

    blk = lambda src_off: pl.BlockSpec((T, LANES), lambda p, I: (I, src_off + p))
    full = lambda off: pl.BlockSpec((S, LANES), lambda p, I: (0, off + p))
    dq, dk, dv = pl.pallas_call(
        body, name=name, grid=(npair, n_tiles),
        out_shape=[jax.ShapeDtypeStruct((S, W), F32)] * 3,
        in_specs=[blk(0), full(npair), full(2 * npair), blk(0), blk(0)],
        out_specs=[blk(0), full(0), full(0)],
        compiler_params=_cparams(("parallel", "arbitrary")),
    )(p_sb, p_sb, p_sb, do, carries)
    return dq, dk, dv


def _dil_blocks(b, body_fn):
    for pi, (window, dil) in enumerate(DILATED_PATTERNS):
        assert window // dil == DIL_BLOCK
        nblk = DIL_SUPER // (DIL_BLOCK * dil)
        assert (dil * nblk) % DIL_UNROLL == 0

        def group(g, _, pi=pi, dil=dil, nblk=nblk):
            for u in range(DIL_UNROLL):
                t = g * DIL_UNROLL + u
                n = t % nblk
                body_fn(pi, dil, t // nblk, n, b * nblk + n)
            return 0

        lax.fori_loop(0, dil * nblk // DIL_UNROLL, group, 0)


def _dil_rows(start, size, dil):
    if dil == 1:
        return pl.ds(pl.multiple_of(start, DIL_BLOCK), size)
    return pl.ds(start, size, stride=dil)


def _dil_valid(gn, ws):
    row = lax.broadcasted_iota(jnp.int32, (2 * DIL_BLOCK, 2 * DIL_BLOCK), 0)
    kk = lax.broadcasted_iota(jnp.int32, (2 * DIL_BLOCK, 2 * DIL_BLOCK), 1)
    qi = jnp.where(row >= DIL_BLOCK, row - DIL_BLOCK, row)
    dist = (gn - ws) * DIL_BLOCK + qi - kk
    return (dist >= 0) & (dist <= DIL_BLOCK)


def _dl_fwd(q, k, v, *, name):
    S, W = q.shape
    npair = W // LANES
    nsuper = S // DIL_SUPER
    assert S % DIL_SUPER == 0 and S // max(d for _, d in DILATED_PATTERNS) >= 2 * DIL_BLOCK
    scale = HEAD_DIM ** -0.5
    npat = len(DILATED_PATTERNS)

    def body(q_ref, k_ref, v_ref, o_ref, l_ref, *pattern_refs):
        op_refs, lp_refs = pattern_refs[:npat], pattern_refs[npat:]
        b = pl.program_id(1)
        masks = _head_masks((DIL_BLOCK, LANES))

        def block(pi, dil, c, n, gn):
            ws = jnp.maximum(gn - 1, 0)
            qrows = n * (DIL_BLOCK * dil) + c
            krows = ws * (DIL_BLOCK * dil) + c
            q_idx = _dil_rows(qrows, DIL_BLOCK, dil)
            k_idx = _dil_rows(krows, 2 * DIL_BLOCK, dil)
            qb = q_ref[q_idx, :]
            kb = k_ref[k_idx, :].astype(BF16)
            vb = v_ref[k_idx, :].astype(BF16)
            valid = _dil_valid(gn, ws)
            q2 = _sb_stack_heads(qb.astype(BF16), scale)
            z = lax.dot_general(q2, kb, (((1,), (1,)), ((), ())), preferred_element_type=F32)
            z = jnp.where(valid, z, NEG_BIG)
            m = jnp.max(z, axis=1, keepdims=True)
            p = jnp.exp(z - m)
            den = jnp.sum(p, axis=1, keepdims=True)
            acc = jnp.dot(p.astype(BF16), vb, preferred_element_type=F32)
            lse = m + jnp.log(den)
            op_refs[pi][q_idx, :] = _sb_unstack_heads(acc / den)
            lp_refs[pi][q_idx, :] = jnp.where(masks[0], lse[:DIL_BLOCK], lse[DIL_BLOCK:])

        _dil_blocks(b, block)
        lses = [r[...] for r in lp_refs]
        top = functools.reduce(jnp.maximum, lses)
        ws_ = [jnp.exp(l - top) for l in lses]
        den = functools.reduce(jnp.add, ws_)
        num = functools.reduce(jnp.add, [w * r[...] for r, w in zip(op_refs, ws_)])
        o_ref[...] = num / den
        l_ref[...] = top + jnp.log(den)

    blk = pl.BlockSpec((DIL_SUPER, LANES), lambda p, b: (b, p))
    full = pl.BlockSpec((S, LANES), lambda p, b: (0, p))
    return pl.pallas_call(
        body, name=name, grid=(npair, nsuper),
        out_shape=[jax.ShapeDtypeStruct((S, W), F32)] * 2,
        in_specs=[blk, full, full], out_specs=[blk, blk],
        scratch_shapes=[pltpu.VMEM((DIL_SUPER, LANES), F32)] * (2 * npat),
        compiler_params=_cparams(("parallel", "arbitrary")),
    )(q, k, v)


def _dl_bwd(q, k, v, o, lse, do, *, name):
    S, W = q.shape
    npair = W // LANES
    nsuper = S // DIL_SUPER
    scale = HEAD_DIM ** -0.5

    def body(q_ref, k_ref, v_ref, o_ref, l_ref, do_ref, dq_ref, dk_ref, dv_ref, delta_ref):
        b = pl.program_id(1)

        @pl.when(b == 0)
        def _():
            dk_ref[...] = jnp.zeros_like(dk_ref)
            dv_ref[...] = jnp.zeros_like(dv_ref)

        dq_ref[...] = jnp.zeros_like(dq_ref)
        prod = do_ref[...] * o_ref[...]
        delta = jnp.zeros_like(prod)
        for hm in _head_masks(prod.shape):
            delta = jnp.where(hm, jnp.sum(jnp.where(hm, prod, 0.0), axis=1, keepdims=True), delta)
        delta_ref[...] = delta

        def block(pi, dil, c, n, gn):
            ws = jnp.maximum(gn - 1, 0)
            qrows = n * (DIL_BLOCK * dil) + c
            krows = ws * (DIL_BLOCK * dil) + c
            q_idx = _dil_rows(qrows, DIL_BLOCK, dil)
            k_idx = _dil_rows(krows, 2 * DIL_BLOCK, dil)
            qb = q_ref[q_idx, :]
            dob = do_ref[q_idx, :]
            lb = l_ref[q_idx, :]
            db = delta_ref[q_idx, :]
            kb = k_ref[k_idx, :].astype(BF16)
            vb = v_ref[k_idx, :].astype(BF16)
            valid = _dil_valid(gn, ws)
            q2 = _sb_stack_heads(qb.astype(BF16), scale)
            do2 = _sb_stack_heads(dob.astype(BF16))
            lse2 = jnp.concatenate([lb[:, HEAD_DIM * h:HEAD_DIM * h + 1] for h in range(2)], axis=0)
            delta2 = jnp.concatenate([db[:, HEAD_DIM * h:HEAD_DIM * h + 1] for h in range(2)], axis=0)
            z = lax.dot_general(q2, kb, (((1,), (1,)), ((), ())), preferred_element_type=F32)
            p = jnp.where(valid, jnp.exp(jnp.where(valid, z, NEG_BIG) - lse2), 0.0)
            dp = lax.dot_general(do2, vb, (((1,), (1,)), ((), ())), preferred_element_type=F32)
            dzb = (p * (dp - delta2)).astype(BF16)
            tn_dims = (((0,), (0,)), ((), ()))
            dq_blk = _sb_unstack_heads(jnp.dot(dzb, kb, preferred_element_type=F32)) * scale
            dk_blk = lax.dot_general(dzb, q2, tn_dims, preferred_element_type=F32)
            dv_blk = lax.dot_general(p.astype(BF16), do2, tn_dims, preferred_element_type=F32)
            dq_ref[q_idx, :] = dq_ref[q_idx, :] + dq_blk
            dk_ref[k_idx, :] = dk_ref[k_idx, :] + dk_blk
            dv_ref[k_idx, :] = dv_ref[k_idx, :] + dv_blk

        _dil_blocks(b, block)

    blk = pl.BlockSpec((DIL_SUPER, LANES), lambda p, b: (b, p))
    full = pl.BlockSpec((S, LANES), lambda p, b: (0, p))
    return pl.pallas_call(
        body, name=name, grid=(npair, nsuper),
        out_shape=[jax.ShapeDtypeStruct((S, W), F32)] * 3,
        in_specs=[blk, full, full, blk, blk, blk], out_specs=[blk, full, full],
        scratch_shapes=[pltpu.VMEM((DIL_SUPER, LANES), F32)],
        compiler_params=_cparams(("parallel", "arbitrary")),
    )(q, k, v, o, lse, do)


def _loss_head(x, gain, target, *, name):
    S, D = x.shape
    tm = _pick(S, (512, 256, 128))

    def body(x_ref, g_ref, t_ref, dx_ref, dg_ref, loss_ref):
        first = pl.program_id(0) == 0
        xh, r = _rms_hat(x_ref[...])
        g = g_ref[...]
        err = xh * g - t_ref[...]
        dy = err * (1.0 / D)
        dxh = dy * g
        dx_ref[...] = r * (dxh - xh * jnp.mean(dxh * xh, axis=-1, keepdims=True))
        dg_part = jnp.sum(dy * xh, axis=0, keepdims=True)
        loss_part = jnp.zeros((1, LANES), F32) + 0.5 * jnp.sum(jnp.mean(err * err, axis=-1, keepdims=True),
                                                               axis=0, keepdims=True)

        @pl.when(first)
        def _():
            dg_ref[...] = dg_part
            loss_ref[...] = loss_part

        @pl.when(jnp.logical_not(first))
        def _():
            dg_ref[...] += dg_part
            loss_ref[...] += loss_part

    row = pl.BlockSpec((tm, D), lambda i: (i, 0))
    vec = pl.BlockSpec((1, D), lambda i: (0, 0))
    return pl.pallas_call(
        body, name=name, grid=(S // tm,),
        out_shape=[jax.ShapeDtypeStruct((S, D), F32), jax.ShapeDtypeStruct((1, D), F32),
                   jax.ShapeDtypeStruct((1, LANES), F32)],
        in_specs=[row, vec, row], out_specs=[row, vec, pl.BlockSpec((1, LANES), lambda i: (0, 0))],
        compiler_params=_cparams(("arbitrary",)),
    )(x, gain, target)


def _local_step(x, target, gains, weights):
    S, D = x.shape
    d_sb = gains["sb_out_norm"].shape[1]
    d_dl = gains["dil_out_norm"].shape[1]
    w_in = weights["w_in"]
    w_in_sb, w_in_dl = w_in[:, :3 * d_sb], w_in[:, 3 * d_sb:]
    w_out = weights["w_out"]
    cos_t, sin_t = _rope_tables(S)

    x1, saved1 = _ffn_fwd(x, gains["ffn1_norm"], weights["ffn1_w_gate"], weights["ffn1_w_up"],
                          weights["ffn1_w_down"], tag="ffn1")
    h2 = _rms_fwd([x1], [gains["mix_norm"]], name="mix_norm")
    p_sb = _mm(h2, w_in_sb, outs=(BF16,), name="proj_sb")
    p_dl = _mm(h2, w_in_dl, name="proj_dl")
    q_dl = _rotary(p_dl, cos_t, sin_t, col0=0, width=d_dl, sign=1.0, name="rope_q")
    k_dl = _rotary(p_dl, cos_t, sin_t, col0=d_dl, width=d_dl, sign=1.0, name="rope_k")
    v_dl = p_dl[:, 2 * d_dl:]
    o_sb, carries = _sb_fwd(p_sb, name="sb_fwd")
    o_dl, lse_dl = _dl_fwd(q_dl, k_dl, v_dl, name="dl_fwd")
    merged = _rms_fwd([o_sb, o_dl], [gains["sb_out_norm"], gains["dil_out_norm"]], name="out_norm")
    x2 = _mm(merged, w_out, res=x1, name="out_proj")
    x3, saved2 = _ffn_fwd(x2, gains["ffn2_norm"], weights["ffn2_w_gate"], weights["ffn2_w_up"],
                          weights["ffn2_w_down"], tag="ffn2")
    dx3, d_final, loss_row = _loss_head(x3, gains["final_norm"], target, name="loss_head")

    dx2, d_ffn2_norm, dwg2, dwu2, dwd2 = _ffn_bwd(dx3, x2, gains["ffn2_norm"], weights["ffn2_w_gate"],
                                                  weights["ffn2_w_up"], weights["ffn2_w_down"], saved2, tag="ffn2")
    d_w_out = _mm(merged, dx2, ta=True, name="d_w_out")
    d_merged = _mm(dx2, w_out, tb=True, name="d_merged")
    (do_sb, do_dl), (d_sb_norm, d_dl_norm) = _rms_bwd(
        d_merged, [o_sb, o_dl], [gains["sb_out_norm"], gains["dil_out_norm"]], None, name="out_norm_bwd")
    dq_sb, dk_sb, dv_sb = _sb_bwd(p_sb, do_sb, carries, name="sb_bwd")
    dq_dl, dk_dl, dv_dl = _dl_bwd(q_dl, k_dl, v_dl, o_dl, lse_dl, do_dl, name="dl_bwd")
    dq_dl = _rotary(dq_dl, cos_t, sin_t, col0=0, width=d_dl, sign=-1.0, name="rope_dq")
    dk_dl = _rotary(dk_dl, cos_t, sin_t, col0=0, width=d_dl, sign=-1.0, name="rope_dk")
    d_proj = jnp.concatenate([p.astype(BF16) for p in (dq_sb, dk_sb, dv_sb, dq_dl, dk_dl, dv_dl)], axis=1)
    d_w_in = _mm(h2, d_proj, ta=True, name="d_w_in")
    dh2 = _mm(d_proj, w_in, tb=True, name="dh_mix")
    (dx1,), (d_mix_norm,) = _rms_bwd(dh2, [x1], [gains["mix_norm"]], dx2, name="mix_norm_bwd")
    dx, d_ffn1_norm, dwg1, dwu1, dwd1 = _ffn_bwd(dx1, x, gains["ffn1_norm"], weights["ffn1_w_gate"],
                                                 weights["ffn1_w_up"], weights["ffn1_w_down"], saved1, tag="ffn1")
    gain_grads = dict(ffn1_norm=d_ffn1_norm, mix_norm=d_mix_norm, sb_out_norm=d_sb_norm, dil_out_norm=d_dl_norm,
                      ffn2_norm=d_ffn2_norm, final_norm=d_final)
    weight_grads = dict(ffn1_w_gate=dwg1, ffn1_w_up=dwu1, ffn1_w_down=dwd1, w_in=d_w_in, w_out=d_w_out,
                        ffn2_w_gate=dwg2, ffn2_w_up=dwu2, ffn2_w_down=dwd2)
    return loss_row, dx, gain_grads, weight_grads


def _mesh_position():
    return lax.axis_index("x"), lax.axis_index("y"), lax.axis_index("c")


def _flip(coord, bit):
    return 1 - coord if bit else coord


RELATIONS = [(rx, ry, rc) for rx in (0, 1) for ry in (0, 1) for rc in (0, 1)][1:]


def _all_gather(shard, *, name):
    R, C = shard.shape

    def body(x_ref, out_ref, send_sems, recv_sems, local_sem):
        x, y, c = _mesh_position()
        me, sibling = (x, y, c), (x, y, 1 - c)
        chips = [(1 - x, y), (x, 1 - y), (1 - x, 1 - y)]

        def slot(px, py, pc):
            return out_ref.at[4 * px + 2 * py + pc]

        def copy(k, block, to, src=None):
            return pltpu.make_async_remote_copy(
                src_ref=slot(*block) if src is None else src, dst_ref=slot(*block),
                send_sem=send_sems.at[k], recv_sem=recv_sems.at[k],
                device_id=to, device_id_type=pl.DeviceIdType.MESH)

        mine = pltpu.make_async_copy(x_ref, slot(*me), local_sem)
        mine.start()
        first = [copy(0, me, sibling, src=x_ref)]
        first += [copy(1 + j, me, (*chip, c), src=x_ref) for j, chip in enumerate(chips)]
        for cp in first:
            cp.start()
        passed = [copy(4 + j, (*chip, c), sibling) for j, chip in enumerate(chips)]
        for j, chip in enumerate(chips):
            copy(1 + j, (*chip, c), me).wait_recv()
            passed[j].start()
        copy(0, sibling, me).wait_recv()
        for j, chip in enumerate(chips):
            copy(4 + j, (*chip, 1 - c), me).wait_recv()
        for cp in first + passed:
            cp.wait_send()
        mine.wait()

    return pl.pallas_call(
        body, name=name,
        out_shape=jax.ShapeDtypeStruct((N_DEV, R, C), shard.dtype),
        in_specs=[pl.BlockSpec(memory_space=pl.ANY)],
        out_specs=pl.BlockSpec(memory_space=pl.ANY),
        scratch_shapes=[pltpu.SemaphoreType.DMA((7,)), pltpu.SemaphoreType.DMA((7,)), pltpu.SemaphoreType.DMA],
    )(shard)


def _exchange_chunks(packs, *, name):
    n = len(packs)

    def body(*refs):
        in_refs, out_refs = refs[:n], refs[n:2 * n]
        send_sems, recv_sems, local_sems = refs[2 * n:]
        x, y, c = _mesh_position()
        me = 4 * x + 2 * y + c
        copies = []
        for t in range(n):
            local = pltpu.make_async_copy(in_refs[t].at[me], out_refs[t].at[me], local_sems.at[t])
            local.start()
            copies.append(local)
        for r, (rx, ry, rc) in enumerate(RELATIONS):
            px, py, pc = _flip(x, rx), _flip(y, ry), _flip(c, rc)
            peer = 4 * px + 2 * py + pc
            for t in range(n):
                cp = pltpu.make_async_remote_copy(
                    src_ref=in_refs[t].at[peer], dst_ref=out_refs[t].at[me],
                    send_sem=send_sems.at[t, r], recv_sem=recv_sems.at[t, r],
                    device_id=(px, py, pc), device_id_type=pl.DeviceIdType.MESH)
                cp.start()
                copies.append(cp)
        for cp in copies:
            cp.wait()

    return pl.pallas_call(
        body, name=name,
        out_shape=[jax.ShapeDtypeStruct(p.shape, p.dtype) for p in packs],
        in_specs=[pl.BlockSpec(memory_space=pl.ANY)] * n,
        out_specs=[pl.BlockSpec(memory_space=pl.ANY)] * n,
        scratch_shapes=[pltpu.SemaphoreType.DMA((n, 7)), pltpu.SemaphoreType.DMA((n, 7)),
                        pltpu.SemaphoreType.DMA((n,))],
    )(*packs)


def _all_reduce_rows(v, *, name):
    R, C = v.shape

    def body(v_ref, out_ref, buf, send_sems, recv_sems):
        x, y, c = _mesh_position()
        me = 4 * x + 2 * y + c
        buf[me] = v_ref[...]
        copies = []
        for r, (rx, ry, rc) in enumerate(RELATIONS):
            cp = pltpu.make_async_remote_copy(
                src_ref=v_ref, dst_ref=buf.at[me], send_sem=send_sems.at[r], recv_sem=recv_sems.at[r],
                device_id=(_flip(x, rx), _flip(y, ry), _flip(c, rc)), device_id_type=pl.DeviceIdType.MESH)
            cp.start()
            copies.append(cp)
        for cp in copies:
            cp.wait()
        total = buf[0]
        for s in range(1, N_DEV):
            total = total + buf[s]
        out_ref[...] = total

    return pl.pallas_call(
        body, name=name,
        out_shape=jax.ShapeDtypeStruct((R, C), F32),
        in_specs=[pl.BlockSpec(memory_space=pltpu.VMEM)],
        out_specs=pl.BlockSpec(memory_space=pltpu.VMEM),
        scratch_shapes=[pltpu.VMEM((N_DEV, R, C), F32), pltpu.SemaphoreType.DMA((7,)), pltpu.SemaphoreType.DMA((7,))],
    )(v)


def _sum_slots(recv, *, name):
    _, R, C = recv.shape
    tr = _pick(R, (256, 208, 128, 64, 32, 16))

    def body(r_ref, o_ref):
        total = r_ref[0].astype(F32)
        for s in range(1, N_DEV):
            total = total + r_ref[s].astype(F32)
        o_ref[...] = total

    return pl.pallas_call(
        body, name=name, grid=(R // tr,),
        out_shape=jax.ShapeDtypeStruct((R, C), F32),
        in_specs=[pl.BlockSpec((N_DEV, tr, C), lambda i: (0, i, 0))],
        out_specs=pl.BlockSpec((tr, C), lambda i: (i, 0)),
        compiler_params=_cparams(("parallel",)),
    )(recv)


def _adamw(w, g, m, v, *, name):
    R, C = w.shape
    tr = _pick(R, (256, 128, 64, 32, 16, 8))

    def body(w_ref, g_ref, m_ref, v_ref, d_ref, nm_ref, nv_ref):
        g = g_ref[...]
        m_new = ADAM_B1 * m_ref[...] + (1.0 - ADAM_B1) * g
        v_new = ADAM_B2 * v_ref[...] + (1.0 - ADAM_B2) * (g * g)
        m_hat = m_new / (1.0 - ADAM_B1 ** ADAM_STEP)
        v_hat = v_new / (1.0 - ADAM_B2 ** ADAM_STEP)
        d_ref[...] = -ADAM_LR * (m_hat / (jnp.sqrt(v_hat) + ADAM_EPS) + ADAM_WD * w_ref[...])
        nm_ref[...] = m_new
        nv_ref[...] = v_new

    spec = pl.BlockSpec((tr, C), lambda i: (i, 0))
    return pl.pallas_call(
        body, name=name, grid=(R // tr,),
        out_shape=[jax.ShapeDtypeStruct((R, C), F32)] * 3,
        in_specs=[spec] * 4, out_specs=[spec] * 3,
        compiler_params=_cparams(("parallel",)),
    )(w, g, m, v)


WEIGHT_NAMES = ["ffn1_norm", "ffn1_w_gate", "ffn1_w_up", "ffn1_w_down", "mix_norm", "w_in", "sb_out_norm",
                "dil_out_norm", "w_out", "ffn2_norm", "ffn2_w_gate", "ffn2_w_up", "ffn2_w_down", "final_norm"]
GAIN_NAMES = ["ffn1_norm", "mix_norm", "sb_out_norm", "dil_out_norm", "ffn2_norm", "final_norm"]
COL_SHARDED = ["ffn1_w_gate", "ffn1_w_up", "ffn2_w_gate", "ffn2_w_up", "w_in"]
ROW_SHARDED = ["ffn1_w_down", "ffn2_w_down", "w_out"]


def _step(x, target, params, moments_m, moments_v):
    col_pack = jnp.concatenate([params[n] for n in COL_SHARDED], axis=1).astype(BF16)
    row_pack = jnp.concatenate([params[n] for n in ROW_SHARDED], axis=0).astype(BF16)
    col_all = _all_gather(col_pack, name="gather_col")
    row_all = _all_gather(row_pack, name="gather_row")
    weights = {}
    off = 0
    for n in COL_SHARDED:
        w = params[n].shape[1]
        piece = col_all[:, :, off:off + w]
        weights[n] = jnp.transpose(piece, (1, 0, 2)).reshape(piece.shape[1], N_DEV * w)
        off += w
    off = 0
    for n in ROW_SHARDED:
        r = params[n].shape[0]
        weights[n] = row_all[:, off:off + r, :].reshape(N_DEV * r, row_all.shape[2])
        off += r

    gains = {n: params[n] for n in GAIN_NAMES}
    loss_row, grad_x, gain_grads, weight_grads = _local_step(x, target, gains, weights)

    col_chunks, row_chunks = [], []
    for n in COL_SHARDED:
        g = weight_grads[n]
        w = params[n].shape[1]
        col_chunks.append(jnp.transpose(g.reshape(g.shape[0], N_DEV, w), (1, 0, 2)))
    for n in ROW_SHARDED:
        g = weight_grads[n]
        r = params[n].shape[0]
        row_chunks.append(g.reshape(N_DEV, r, g.shape[1]))
    col_send = jnp.concatenate(col_chunks, axis=2).astype(BF16)
    row_send = jnp.concatenate(row_chunks, axis=1).astype(BF16)
    col_recv, row_recv = _exchange_chunks([col_send, row_send], name="exchange_grads")
    col_grad = _sum_slots(col_recv, name="sum_col_grads")
    row_grad = _sum_slots(row_recv, name="sum_row_grads")
    grads = {}
    off = 0
    for n in COL_SHARDED:
        w = params[n].shape[1]
        grads[n] = col_grad[:, off:off + w]
        off += w
    off = 0
    for n in ROW_SHARDED:
        r = params[n].shape[0]
        grads[n] = row_grad[off:off + r, :]
        off += r

    rows = [gain_grads[n].reshape(-1, LANES) for n in GAIN_NAMES] + [loss_row]
    small = jnp.concatenate(rows, axis=0)
    pad = (-small.shape[0]) % 8
    small = jnp.pad(small, ((0, pad), (0, 0)))
    small = _all_reduce_rows(small, name="reduce_gains_loss")
    off = 0
    for n in GAIN_NAMES:
        r = gain_grads[n].shape[1] // LANES
        grads[n] = small[off:off + r].reshape(1, -1)
        off += r
    loss = small[off, 0]

    delta, new_m, new_v = {}, {}, {}
    for n in WEIGHT_NAMES:
        delta[n], new_m[n], new_v[n] = _adamw(params[n], grads[n], moments_m[n], moments_v[n], name=f"adamw_{n}")
    return loss, grad_x, grads, delta, new_m, new_v


def kernel(x, ffn1_norm, ffn1_w_gate, ffn1_w_up, ffn1_w_down, mix_norm, w_in, sb_out_norm, dil_out_norm, w_out, ffn2_norm, ffn2_w_gate, ffn2_w_up, ffn2_w_down, final_norm, loss_target, m_ffn1_norm, m_ffn1_w_gate, m_ffn1_w_up, m_ffn1_w_down, m_mix_norm, m_w_in, m_sb_out_norm, m_dil_out_norm, m_w_out, m_ffn2_norm, m_ffn2_w_gate, m_ffn2_w_up, m_ffn2_w_down, m_final_norm, v_ffn1_norm, v_ffn1_w_gate, v_ffn1_w_up, v_ffn1_w_down, v_mix_norm, v_w_in, v_sb_out_norm, v_dil_out_norm, v_w_out, v_ffn2_norm, v_ffn2_w_gate, v_ffn2_w_up, v_ffn2_w_down, v_final_norm):
    given = dict(locals())
    shapes = {n: given[n].shape for n in WEIGHT_NAMES}

    def as2d(a):
        return a.reshape(1, -1) if a.ndim == 1 else a.reshape(a.shape[-2], a.shape[-1])

    params = {n: as2d(given[n]) for n in WEIGHT_NAMES}
    moments_m = {n: as2d(given["m_" + n]) for n in WEIGHT_NAMES}
    moments_v = {n: as2d(given["v_" + n]) for n in WEIGHT_NAMES}
    loss, grad_x, grads, delta, new_m, new_v = _step(x[0], loss_target[0], params, moments_m, moments_v)
    back = lambda d: [d[n].reshape(shapes[n]) for n in WEIGHT_NAMES]
    return (loss, grad_x[None], *back(grads), *back(delta), *back(new_m), *back(new_v))
```

```python
import functools

import jax
import jax.numpy as jnp
from jax import lax
from jax.experimental import pallas as pl
from jax.experimental.pallas import tpu as pltpu

F32 = jnp.float32
BF16 = jnp.bfloat16

N_DEV = 8
HEAD_DIM = 64
LANES = 128
DILATED_PATTERNS = ((128, 1), (512, 4), (2048, 16))
DIL_BLOCK = 128
DIL_SUPER = 2048
DIL_UNROLL = 4
SB_TILE = 256
SB_UNROLL = 4
SB_DEAD = 90.0
SB_UNSEEN = -1e30
ROPE_THETA = 10000.0
RMS_EPS = 1e-6
HALF_STEP = 0.5
ADAM_LR = 0.001
ADAM_B1 = 0.9
ADAM_B2 = 0.999
ADAM_EPS = 1e-08
ADAM_WD = 0.01
ADAM_STEP = 10
NEG_BIG = -1e30
VMEM_CAP_MB = 60


def _pick(n, prefs):
    for p in prefs:
        if n % p == 0:
            return p
    return n


MM_MAX_TILE = 1536


def _largest_tile(n, cap):
    if n <= cap:
        return n
    for t in range(cap - cap % LANES, 0, -LANES):
        if n % t == 0:
            return t
    return n


def _cparams(sem=None, vmem_mb=48):
    return pltpu.CompilerParams(dimension_semantics=sem, vmem_limit_bytes=min(vmem_mb, VMEM_CAP_MB) * 1024 * 1024)


def _nbytes(shape, dtype):
    n = 1
    for s in shape:
        n *= s
    return n * jnp.dtype(dtype).itemsize


def _mm(a, b, *, name, ta=False, tb=False, outs=(F32,), res=None, alpha=1.0, extras=(), epilogue=None,
        tm=None, tn=None, tk=None):
    if ta:
        K, M = a.shape
    else:
        M, K = a.shape
    if tb:
        N, Kb = b.shape
    else:
        Kb, N = b.shape
    assert K == Kb, (a.shape, b.shape, ta, tb)
    tm = tm or (_largest_tile(M, MM_MAX_TILE) if ta else _pick(M, (512, 256, 128)))
    tn = tn or _largest_tile(N, MM_MAX_TILE)
    tk = tk or (K if K <= 3072 else _pick(K, (1024, 512, 256, 128)))
    nk = K // tk
    a_spec = pl.BlockSpec((tk, tm), lambda i, j, k: (k, i)) if ta else pl.BlockSpec((tm, tk), lambda i, j, k: (i, k))
    b_spec = pl.BlockSpec((tn, tk), lambda i, j, k: (j, k)) if tb else pl.BlockSpec((tk, tn), lambda i, j, k: (k, j))
    mn_spec = pl.BlockSpec((tm, tn), lambda i, j, k: (i, j))
    dims = (((0 if ta else 1,), (1 if tb else 0,)), ((), ()))
    n_extra = len(extras) + (1 if res is not None else 0)
    n_out = len(outs)

    def body(*refs):
        a_ref, b_ref = refs[0], refs[1]
        in_refs = refs[2:2 + n_extra]
        out_refs = refs[2 + n_extra:2 + n_extra + n_out]
        prod = lax.dot_general(a_ref[...].astype(BF16), b_ref[...].astype(BF16), dims, preferred_element_type=F32)

        def finish(acc):
            blocks = [r[...] for r in in_refs]
            if res is not None:
                r_blk, blocks = blocks[0], blocks[1:]
            else:
                r_blk = None
            if epilogue is None:
                val = acc * alpha
                if r_blk is not None:
                    val = val + r_blk
                vals = (val,)
            else:
                vals = epilogue(acc, r_blk, *blocks)
            for o_ref, v in zip(out_refs, vals):
                o_ref[...] = v.astype(o_ref.dtype)

        if nk == 1:
            finish(prod)
        else:
            acc_ref = refs[2 + n_extra + n_out]
            k = pl.program_id(2)

            @pl.when(k == 0)
            def _():
                acc_ref[...] = prod

            @pl.when(k > 0)
            def _():
                acc_ref[...] += prod

            @pl.when(k == nk - 1)
            def _():
                finish(acc_ref[...])

    operands = [a, b] + ([res] if res is not None else []) + list(extras)
    in_specs = [a_spec, b_spec] + [mn_spec] * n_extra
    est = 2 * (_nbytes((tm, tk), a.dtype) + _nbytes((tk, tn), b.dtype))
    est += 2 * sum(_nbytes((tm, tn), o.dtype) for o in operands[2:])
    est += 2 * sum(_nbytes((tm, tn), d) for d in outs) + _nbytes((tm, tn), F32)
    result = pl.pallas_call(
        body, name=name, grid=(M // tm, N // tn, nk),
        out_shape=[jax.ShapeDtypeStruct((M, N), d) for d in outs],
        in_specs=in_specs, out_specs=[mn_spec] * n_out,
        scratch_shapes=[pltpu.VMEM((tm, tn), F32)] if nk > 1 else [],
        compiler_params=_cparams(("parallel", "parallel", "arbitrary"), vmem_mb=max(32, 2 * est // (1024 * 1024))),
    )(*operands)
    return result[0] if n_out == 1 else result


def _rms_hat(x):
    r = lax.rsqrt(jnp.mean(x * x, axis=-1, keepdims=True) + RMS_EPS)
    return x * r, r


def _rms_fwd(xs, gains, *, name):
    S = xs[0].shape[0]
    widths = [x.shape[1] for x in xs]
    tm = _pick(S, (512, 256, 128))
    n = len(xs)

    def body(*refs):
        o_ref = refs[2 * n]
        off = 0
        for i in range(n):
            xh, _ = _rms_hat(refs[i][...])
            o_ref[:, off:off + widths[i]] = (xh * refs[n + i][...]).astype(o_ref.dtype)
            off += widths[i]

    return pl.pallas_call(
        body, name=name, grid=(S // tm,),
        out_shape=jax.ShapeDtypeStruct((S, sum(widths)), BF16),
        in_specs=[pl.BlockSpec((tm, w), lambda i: (i, 0)) for w in widths]
        + [pl.BlockSpec((1, w), lambda i: (0, 0)) for w in widths],
        out_specs=pl.BlockSpec((tm, sum(widths)), lambda i: (i, 0)),
        compiler_params=_cparams(("parallel",)),
    )(*xs, *gains)


def _rms_bwd(dh, xs, gains, res, *, name):
    S = xs[0].shape[0]
    widths = [x.shape[1] for x in xs]
    tm = _pick(S, (512, 256, 128))
    n = len(xs)
    has_res = res is not None

    def body(*refs):
        dh_ref = refs[0]
        x_refs = refs[1:1 + n]
        g_refs = refs[1 + n:1 + 2 * n]
        r_ref = refs[1 + 2 * n] if has_res else None
        base = 1 + 2 * n + (1 if has_res else 0)
        dx_refs = refs[base:base + n]
        dg_refs = refs[base + n:base + 2 * n]
        first = pl.program_id(0) == 0
        off = 0
        for i in range(n):
            x = x_refs[i][...]
            xh, r = _rms_hat(x)
            d = dh_ref[:, off:off + widths[i]]
            dxh = d * g_refs[i][...]
            dx = r * (dxh - xh * jnp.mean(dxh * xh, axis=-1, keepdims=True))
            if has_res:
                dx = dx + r_ref[...]
            dx_refs[i][...] = dx
            part = jnp.sum(d * xh, axis=0, keepdims=True)

            @pl.when(first)
            def _(i=i, part=part):
                dg_refs[i][...] = part

            @pl.when(jnp.logical_not(first))
            def _(i=i, part=part):
                dg_refs[i][...] += part

            off += widths[i]

    in_specs = [pl.BlockSpec((tm, sum(widths)), lambda i: (i, 0))]
    in_specs += [pl.BlockSpec((tm, w), lambda i: (i, 0)) for w in widths]
    in_specs += [pl.BlockSpec((1, w), lambda i: (0, 0)) for w in widths]
    operands = [dh, *xs, *gains]
    if has_res:
        in_specs.append(pl.BlockSpec((tm, widths[0]), lambda i: (i, 0)))
        operands.append(res)
    out = pl.pallas_call(
        body, name=name, grid=(S // tm,),
        out_shape=[jax.ShapeDtypeStruct((S, w), F32) for w in widths] + [jax.ShapeDtypeStruct((1, w), F32) for w in widths],
        in_specs=in_specs,
        out_specs=[pl.BlockSpec((tm, w), lambda i: (i, 0)) for w in widths]
        + [pl.BlockSpec((1, w), lambda i: (0, 0)) for w in widths],
        compiler_params=_cparams(("arbitrary",)),
    )(*operands)
    return out[:n], out[n:]


def _sigmoid(g):
    return 1.0 / (1.0 + jnp.exp(-g))


def _ffn_fwd(x, gain, wg, wu, wd, *, tag):
    h = _rms_fwd([x], [gain], name=f"{tag}_norm")
    g = _mm(h, wg, outs=(BF16,), name=f"{tag}_gate")

    def act(acc, _, g_blk):
        gf = g_blk.astype(F32)
        return acc, gf * _sigmoid(gf) * acc

    u, a = _mm(h, wu, outs=(BF16, BF16), extras=(g,), epilogue=act, name=f"{tag}_up_act")
    y = _mm(a, wd, res=x, alpha=HALF_STEP, name=f"{tag}_down")
    return y, (h, g, u, a)


def _ffn_bwd(dout, x, gain, wg, wu, wd, saved, *, tag):
    h, g, u, a = saved

    def act_bwd(acc, _, g_blk, u_blk):
        gf, uf = g_blk.astype(F32), u_blk.astype(F32)
        da = acc * HALF_STEP
        sig = _sigmoid(gf)
        silu = gf * sig
        return da * uf * (sig * (1.0 + gf * (1.0 - sig))), da * silu

    dg, du = _mm(dout, wd, tb=True, outs=(BF16, BF16), extras=(g, u), epilogue=act_bwd, name=f"{tag}_bwd_act")
    dwg = _mm(h, dg, ta=True, name=f"{tag}_dwg")
    dwu = _mm(h, du, ta=True, name=f"{tag}_dwu")
    dwd = _mm(a, dout, ta=True, alpha=HALF_STEP, name=f"{tag}_dwd")
    dh = _mm(dg, wg, tb=True, name=f"{tag}_dh_gate")
    dh = _mm(du, wu, tb=True, res=dh, name=f"{tag}_dh_up")
    (dx,), (dgain,) = _rms_bwd(dh, [x], [gain], dout, name=f"{tag}_norm_bwd")
    return dx, dgain, dwg, dwu, dwd


def _rope_tables(S):
    half = HEAD_DIM // 2
    inv_freq = ROPE_THETA ** (-jnp.arange(half, dtype=F32) / half)
    ang = jnp.arange(S, dtype=F32)[:, None] * inv_freq[None, :]
    cos, sin = jnp.cos(ang), jnp.sin(ang)
    reps = LANES // HEAD_DIM
    cos_t = jnp.tile(jnp.concatenate([cos, cos], axis=1), (1, reps))
    sin_t = jnp.tile(jnp.concatenate([-sin, sin], axis=1), (1, reps))
    return cos_t, sin_t


def _rotary(t, cos_t, sin_t, *, col0, width, sign, name):
    S = t.shape[0]
    tm = _pick(S, (512, 256, 128))
    half = HEAD_DIM // 2
    c0 = col0 // LANES

    def body(t_ref, c_ref, s_ref, o_ref):
        v = t_ref[...]
        lane = lax.broadcasted_iota(jnp.int32, v.shape, 1)
        swapped = jnp.where(lane % HEAD_DIM < half, pltpu.roll(v, LANES - half, axis=1), pltpu.roll(v, half, axis=1))
        o_ref[...] = v * c_ref[...] + swapped * (s_ref[...] * sign)

    return pl.pallas_call(
        body, name=name, grid=(S // tm, width // LANES),
        out_shape=jax.ShapeDtypeStruct((S, width), F32),
        in_specs=[pl.BlockSpec((tm, LANES), lambda i, j: (i, c0 + j)),
                  pl.BlockSpec((tm, LANES), lambda i, j: (i, 0)),
                  pl.BlockSpec((tm, LANES), lambda i, j: (i, 0))],
        out_specs=pl.BlockSpec((tm, LANES), lambda i, j: (i, j)),
        compiler_params=_cparams(("parallel", "parallel")),
    )(t, cos_t, sin_t)


def _head_masks(shape):
    lane = lax.broadcasted_iota(jnp.int32, shape, 1)
    return [(lane >= HEAD_DIM * h) & (lane < HEAD_DIM * (h + 1)) for h in range(LANES // HEAD_DIM)]


def _sb_scores(q2, k_j):
    z = lax.dot_general(q2, k_j, (((1,), (1,)), ((), ())), preferred_element_type=F32)
    sign_bit = jnp.int32(-2 ** 31)
    minus_abs = lax.bitcast_convert_type(lax.bitcast_convert_type(z, jnp.int32) | sign_bit, F32)
    softplus = jnp.maximum(z, 0.0) + jnp.log(1.0 + jnp.exp(minus_abs))
    return z - softplus, softplus


def _sb_stack_heads(t, scale=None):
    parts = [jnp.where(hm, t, jnp.zeros_like(t)) for hm in _head_masks(t.shape)]
    t2 = jnp.concatenate(parts, axis=0)
    if scale is not None:
        t2 = (t2.astype(F32) * scale).astype(t2.dtype)
    return t2


def _sb_unstack_heads(t2):
    T = t2.shape[0] // 2
    masks = _head_masks((T, LANES))
    return jnp.where(masks[0], t2[:T], t2[T:])


def _sb_causal(T):
    row = lax.broadcasted_iota(jnp.int32, (2 * T, T), 0)
    col = lax.broadcasted_iota(jnp.int32, (2 * T, T), 1)
    return col < jnp.where(row >= T, row - T, row)


def _sb_triangle(T, later):
    row = lax.broadcasted_iota(jnp.int32, (T, T), 0)
    col = lax.broadcasted_iota(jnp.int32, (T, T), 1)
    return ((row > col) if later else (row < col)).astype(BF16)


def _sb_fwd(p_sb, *, name):
    S = p_sb.shape[0]
    W = p_sb.shape[1] // 3
    npair = W // LANES
    T = SB_TILE
    n_tiles = S // T
    assert n_tiles <= HEAD_DIM
    scale = HEAD_DIM ** -0.5

    def body(q_ref, k_ref, v_ref, o_ref, c_ref):
        I = pl.program_id(1)
        lane = lax.broadcasted_iota(jnp.int32, (T, LANES), 1)
        causal = _sb_causal(T)
        later_than = _sb_triangle(T, True)
        q2 = _sb_stack_heads(q_ref[...], scale)

        def scores(J, diag):
            off = pl.multiple_of(J * T, T)
            log_beta, stay = _sb_scores(q2, k_ref[pl.ds(off, T), :])
            if diag:
                stay = jnp.where(causal, stay, 0.0)
            local = jnp.dot(stay.astype(BF16), later_than, preferred_element_type=F32)
            return log_beta, local, jnp.sum(stay, axis=1, keepdims=True), v_ref[pl.ds(off, T), :]

        def weigh(J, sc, gone, acc, carr, diag):
            log_beta, local, _, v_j = sc
            w = jnp.exp((log_beta - gone) - local)
            if diag:
                w = jnp.where(causal, w, 0.0)
            acc = acc + jnp.dot(w.astype(BF16), v_j, preferred_element_type=F32)
            carr = jnp.where(lane == J, -gone[:T], carr)
            carr = jnp.where(lane == HEAD_DIM + J, -gone[T:], carr)
            return acc, carr

        def tiles(J, count, state, diag):
            gone, acc, carr, _ = state
            scs = [scores(J - u, diag) for u in range(count)]
            for u, sc in enumerate(scs):
                acc, carr = weigh(J - u, sc, gone, acc, carr, diag)
                gone = gone + sc[2]
            return gone, acc, carr, jnp.min(gone)

        U = SB_UNROLL
        alive = lambda st: st[3] < SB_DEAD
        state = (jnp.zeros((2 * T, 1), F32), jnp.zeros((2 * T, LANES), F32),
                 jnp.full((T, LANES), SB_UNSEEN, F32), jnp.zeros((), F32))
        state = tiles(I, 1, state, True)
        singles = jnp.where(I > 0, (I - 1) % U + 1, 0)
        _, state = lax.while_loop(lambda c: (c[0] < singles) & alive(c[1]),
                                  lambda c: (c[0] + 1, tiles(I - 1 - c[0], 1, c[1], False)), (jnp.int32(0), state))
        blocks = (I - singles) // U
        _, state = lax.while_loop(lambda c: (c[0] < blocks) & alive(c[1]),
                                  lambda c: (c[0] + 1, tiles(I - 1 - singles - U * c[0], U, c[1], False)),
                                  (jnp.int32(0), state))
        _, acc, carr, _ = state
        o_ref[...] = _sb_unstack_heads(acc)
        c_ref[...] = carr

    blk = lambda I_off: pl.BlockSpec((T, LANES), lambda p, I: (I, I_off + p))
    full = lambda off: pl.BlockSpec((S, LANES), lambda p, I: (0, off + p))
    return pl.pallas_call(
        body, name=name, grid=(npair, n_tiles),
        out_shape=[jax.ShapeDtypeStruct((S, W), F32), jax.ShapeDtypeStruct((S, W), F32)],
        in_specs=[blk(0), full(npair), full(2 * npair)],
        out_specs=[blk(0), blk(0)],
        compiler_params=_cparams(("parallel", "arbitrary")),
    )(p_sb, p_sb, p_sb)


def _sb_bwd(p_sb, do, carries, *, name):
    S = p_sb.shape[0]
    W = p_sb.shape[1] // 3
    npair = W // LANES
    T = SB_TILE
    n_tiles = S // T
    scale = HEAD_DIM ** -0.5

    def body(q_ref, k_ref, v_ref, do_ref, c_ref, dq_ref, dk_ref, dv_ref):
        I = pl.program_id(1)

        @pl.when(I == 0)
        def _():
            dk_ref[...] = jnp.zeros_like(dk_ref)
            dv_ref[...] = jnp.zeros_like(dv_ref)

        lane = lax.broadcasted_iota(jnp.int32, (T, LANES), 1)
        causal = _sb_causal(T)
        later_than = _sb_triangle(T, True)
        earlier_than = _sb_triangle(T, False)
        q2 = _sb_stack_heads(q_ref[...], scale)
        do2 = _sb_stack_heads(do_ref[...].astype(BF16))
        carr = c_ref[...]
        tn_dims = (((0,), (0,)), ((), ()))

        def chain(J, diag):
            off = pl.multiple_of(J * T, T)
            k_j = k_ref[pl.ds(off, T), :]
            v_j = v_ref[pl.ds(off, T), :]
            log_beta, stay = _sb_scores(q2, k_j)
            if diag:
                stay = jnp.where(causal, stay, 0.0)
            lc = jnp.concatenate(
                [jnp.sum(jnp.where(lane == HEAD_DIM * h + J, carr, 0.0), axis=1, keepdims=True) for h in range(2)],
                axis=0)
            w = jnp.exp((log_beta + lc) - jnp.dot(stay.astype(BF16), later_than, preferred_element_type=F32))
            if diag:
                w = jnp.where(causal, w, 0.0)
            dw = lax.dot_general(do2, v_j, (((1,), (1,)), ((), ())), preferred_element_type=F32)
            e = w * dw
            local = jnp.dot(e.astype(BF16), earlier_than, preferred_element_type=F32)
            return off, k_j, w, e, local, jnp.exp(log_beta), jnp.sum(e, axis=1, keepdims=True)

        def finish(ch, ec, dq_acc, diag):
            off, k_j, w, e, local, beta, _ = ch
            e_before = local + ec
            dz = e - beta * (e + e_before)
            if diag:
                dz = jnp.where(causal, dz, 0.0)
            dzb = dz.astype(BF16)
            dq_acc = dq_acc + jnp.dot(dzb, k_j, preferred_element_type=F32)
            dk_ref[pl.ds(off, T), :] += lax.dot_general(dzb, q2, tn_dims, preferred_element_type=F32)
            dv_ref[pl.ds(off, T), :] += lax.dot_general(w.astype(BF16), do2, tn_dims, preferred_element_type=F32)
            return dq_acc

        def tiles(J, count, state, diag):
            ec, dq_acc = state
            chains = [chain(J + u, diag) for u in range(count)]
            for ch in chains:
                dq_acc = finish(ch, ec, dq_acc, diag)
                ec = ec + ch[6]
            return ec, dq_acc

        lane_row = lax.broadcasted_iota(jnp.int32, (1, LANES), 1)
        reached = (jnp.max(carr, axis=0, keepdims=True) > 0.5 * SB_UNSEEN) & (lane_row < HEAD_DIM)
        first = jnp.min(jnp.where(reached, lane_row.astype(F32), float(n_tiles))).astype(jnp.int32)
        U = SB_UNROLL
        count = I - first
        state = (jnp.zeros((2 * T, 1), F32), jnp.zeros((2 * T, LANES), F32))
        state = lax.fori_loop(0, count // U, lambda jj, st: tiles(first + U * jj, U, st, False), state)
        state = lax.fori_loop(0, count % U, lambda r, st: tiles(I - count % U + r, 1, st, False), state)
        _, dq_acc = tiles(I, 1, state, True)
        dq_ref[...] = _sb_unstack_heads(dq_acc) * scale

    blk = lambda src_off: pl.BlockSpec((T, LANES), lambda p, I: (I, src_off + p))
    full = lambda off: pl.BlockSpec((S, LANES), lambda p, I: (0, off + p))
    dq, dk, dv = pl.pallas_call(
        body, name=name, grid=(npair, n_tiles),
        out_shape=[jax.ShapeDtypeStruct((S, W), F32)] * 3,
        in_specs=[blk(0), full(npair), full(2 * npair), blk(0), blk(0)],
        out_specs=[blk(0), full(0), full(0)],
        compiler_params=_cparams(("parallel", "arbitrary")),
    )(p_sb, p_sb, p_sb, do, carries)
    return dq, dk, dv


def _dil_blocks(b, body_fn):
    for pi, (window, dil) in enumerate(DILATED_PATTERNS):
        assert window // dil == DIL_BLOCK
        nblk = DIL_SUPER // (DIL_BLOCK * dil)
        assert (dil * nblk) % DIL_UNROLL == 0

        def group(g, _, pi=pi, dil=dil, nblk=nblk):
            for u in range(DIL_UNROLL):
                t = g * DIL_UNROLL + u
                n = t % nblk
                body_fn(pi, dil, t // nblk, n, b * nblk + n)
            return 0

        lax.fori_loop(0, dil * nblk // DIL_UNROLL, group, 0)


def _dil_rows(start, size, dil):
    if dil == 1:
        return pl.ds(pl.multiple_of(start, DIL_BLOCK), size)
    return pl.ds(start, size, stride=dil)


def _dil_valid(gn, ws):
    row = lax.broadcasted_iota(jnp.int32, (2 * DIL_BLOCK, 2 * DIL_BLOCK), 0)
    kk = lax.broadcasted_iota(jnp.int32, (2 * DIL_BLOCK, 2 * DIL_BLOCK), 1)
    qi = jnp.where(row >= DIL_BLOCK, row - DIL_BLOCK, row)
    dist = (gn - ws) * DIL_BLOCK + qi - kk
    return (dist >= 0) & (dist <= DIL_BLOCK)


def _dl_fwd(q, k, v, *, name):
    S, W = q.shape
    npair = W // LANES
    nsuper = S // DIL_SUPER
    assert S % DIL_SUPER == 0 and S // max(d for _, d in DILATED_PATTERNS) >= 2 * DIL_BLOCK
    scale = HEAD_DIM ** -0.5
    npat = len(DILATED_PATTERNS)

    def body(q_ref, k_ref, v_ref, o_ref, l_ref, *pattern_refs):
        op_refs, lp_refs = pattern_refs[:npat], pattern_refs[npat:]
        b = pl.program_id(1)
        masks = _head_masks((DIL_BLOCK, LANES))

        def block(pi, dil, c, n, gn):
            ws = jnp.maximum(gn - 1, 0)
            qrows = n * (DIL_BLOCK * dil) + c
            krows = ws * (DIL_BLOCK * dil) + c
            q_idx = _dil_rows(qrows, DIL_BLOCK, dil)
            k_idx = _dil_rows(krows, 2 * DIL_BLOCK, dil)
            qb = q_ref[q_idx, :]
            kb = k_ref[k_idx, :].astype(BF16)
            vb = v_ref[k_idx, :].astype(BF16)
            valid = _dil_valid(gn, ws)
            q2 = _sb_stack_heads(qb.astype(BF16), scale)
            z = lax.dot_general(q2, kb, (((1,), (1,)), ((), ())), preferred_element_type=F32)
            z = jnp.where(valid, z, NEG_BIG)
            m = jnp.max(z, axis=1, keepdims=True)
            p = jnp.exp(z - m)
            den = jnp.sum(p, axis=1, keepdims=True)
            acc = jnp.dot(p.astype(BF16), vb, preferred_element_type=F32)
            lse = m + jnp.log(den)
            op_refs[pi][q_idx, :] = _sb_unstack_heads(acc / den)
            lp_refs[pi][q_idx, :] = jnp.where(masks[0], lse[:DIL_BLOCK], lse[DIL_BLOCK:])

        _dil_blocks(b, block)
        lses = [r[...] for r in lp_refs]
        top = functools.reduce(jnp.maximum, lses)
        ws_ = [jnp.exp(l - top) for l in lses]
        den = functools.reduce(jnp.add, ws_)
        num = functools.reduce(jnp.add, [w * r[...] for r, w in zip(op_refs, ws_)])
        o_ref[...] = num / den
        l_ref[...] = top + jnp.log(den)

    blk = pl.BlockSpec((DIL_SUPER, LANES), lambda p, b: (b, p))
    full = pl.BlockSpec((S, LANES), lambda p, b: (0, p))
    return pl.pallas_call(
        body, name=name, grid=(npair, nsuper),
        out_shape=[jax.ShapeDtypeStruct((S, W), F32)] * 2,
        in_specs=[blk, full, full], out_specs=[blk, blk],
        scratch_shapes=[pltpu.VMEM((DIL_SUPER, LANES), F32)] * (2 * npat),
        compiler_params=_cparams(("parallel", "arbitrary")),
    )(q, k, v)


def _dl_bwd(q, k, v, o, lse, do, *, name):
    S, W = q.shape
    npair = W // LANES
    nsuper = S // DIL_SUPER
    scale = HEAD_DIM ** -0.5

    def body(q_ref, k_ref, v_ref, o_ref, l_ref, do_ref, dq_ref, dk_ref, dv_ref, delta_ref):
        b = pl.program_id(1)

        @pl.when(b == 0)
        def _():
            dk_ref[...] = jnp.zeros_like(dk_ref)
            dv_ref[...] = jnp.zeros_like(dv_ref)

        dq_ref[...] = jnp.zeros_like(dq_ref)
        prod = do_ref[...] * o_ref[...]
        delta = jnp.zeros_like(prod)
        for hm in _head_masks(prod.shape):
            delta = jnp.where(hm, jnp.sum(jnp.where(hm, prod, 0.0), axis=1, keepdims=True), delta)
        delta_ref[...] = delta

        def block(pi, dil, c, n, gn):
            ws = jnp.maximum(gn - 1, 0)
            qrows = n * (DIL_BLOCK * dil) + c
            krows = ws * (DIL_BLOCK * dil) + c
            q_idx = _dil_rows(qrows, DIL_BLOCK, dil)
            k_idx = _dil_rows(krows, 2 * DIL_BLOCK, dil)
            qb = q_ref[q_idx, :]
            dob = do_ref[q_idx, :]
            lb = l_ref[q_idx, :]
            db = delta_ref[q_idx, :]
            kb = k_ref[k_idx, :].astype(BF16)
            vb = v_ref[k_idx, :].astype(BF16)
            valid = _dil_valid(gn, ws)
            q2 = _sb_stack_heads(qb.astype(BF16), scale)
            do2 = _sb_stack_heads(dob.astype(BF16))
            lse2 = jnp.concatenate([lb[:, HEAD_DIM * h:HEAD_DIM * h + 1] for h in range(2)], axis=0)
            delta2 = jnp.concatenate([db[:, HEAD_DIM * h:HEAD_DIM * h + 1] for h in range(2)], axis=0)
            z = lax.dot_general(q2, kb, (((1,), (1,)), ((), ())), preferred_element_type=F32)
            p = jnp.where(valid, jnp.exp(jnp.where(valid, z, NEG_BIG) - lse2), 0.0)
            dp = lax.dot_general(do2, vb, (((1,), (1,)), ((), ())), preferred_element_type=F32)
            dzb = (p * (dp - delta2)).astype(BF16)
            tn_dims = (((0,), (0,)), ((), ()))
            dq_blk = _sb_unstack_heads(jnp.dot(dzb, kb, preferred_element_type=F32)) * scale
            dk_blk = lax.dot_general(dzb, q2, tn_dims, preferred_element_type=F32)
            dv_blk = lax.dot_general(p.astype(BF16), do2, tn_dims, preferred_element_type=F32)
            dq_ref[q_idx, :] = dq_ref[q_idx, :] + dq_blk
            dk_ref[k_idx, :] = dk_ref[k_idx, :] + dk_blk
            dv_ref[k_idx, :] = dv_ref[k_idx, :] + dv_blk

        _dil_blocks(b, block)

    blk = pl.BlockSpec((DIL_SUPER, LANES), lambda p, b: (b, p))
    full = pl.BlockSpec((S, LANES), lambda p, b: (0, p))
    return pl.pallas_call(
        body, name=name, grid=(npair, nsuper),
        out_shape=[jax.ShapeDtypeStruct((S, W), F32)] * 3,
        in_specs=[blk, full, full, blk, blk, blk], out_specs=[blk, full, full],
        scratch_shapes=[pltpu.VMEM((DIL_SUPER, LANES), F32)],
        compiler_params=_cparams(("parallel", "arbitrary")),
    )(q, k, v, o, lse, do)


def _loss_head(x, gain, target, *, name):
    S, D = x.shape
    tm = _pick(S, (512, 256, 128))

    def body(x_ref, g_ref, t_ref, dx_ref, dg_ref, loss_ref):
        first = pl.program_id(0) == 0
        xh, r = _rms_hat(x_ref[...])
        g = g_ref[...]
        err = xh * g - t_ref[...]
        dy = err * (1.0 / D)
        dxh = dy * g
        dx_ref[...] = r * (dxh - xh * jnp.mean(dxh * xh, axis=-1, keepdims=True))
        dg_part = jnp.sum(dy * xh, axis=0, keepdims=True)
        loss_part = jnp.zeros((1, LANES), F32) + 0.5 * jnp.sum(jnp.mean(err * err, axis=-1, keepdims=True),
                                                               axis=0, keepdims=True)

        @pl.when(first)
        def _():
            dg_ref[...] = dg_part
            loss_ref[...] = loss_part

        @pl.when(jnp.logical_not(first))
        def _():
            dg_ref[...] += dg_part
            loss_ref[...] += loss_part

    row = pl.BlockSpec((tm, D), lambda i: (i, 0))
    vec = pl.BlockSpec((1, D), lambda i: (0, 0))
    return pl.pallas_call(
        body, name=name, grid=(S // tm,),
        out_shape=[jax.ShapeDtypeStruct((S, D), F32), jax.ShapeDtypeStruct((1, D), F32),
                   jax.ShapeDtypeStruct((1, LANES), F32)],
        in_specs=[row, vec, row], out_specs=[row, vec, pl.BlockSpec((1, LANES), lambda i: (0, 0))],
        compiler_params=_cparams(("arbitrary",)),
    )(x, gain, target)


def _local_step(x, target, gains, weights):
    S, D = x.shape
    d_sb = gains["sb_out_norm"].shape[1]
    d_dl = gains["dil_out_norm"].shape[1]
    w_in = weights["w_in"]
    w_in_sb, w_in_dl = w_in[:, :3 * d_sb], w_in[:, 3 * d_sb:]
    w_out = weights["w_out"]
    cos_t, sin_t = _rope_tables(S)

    x1, saved1 = _ffn_fwd(x, gains["ffn1_norm"], weights["ffn1_w_gate"], weights["ffn1_w_up"],
                          weights["ffn1_w_down"], tag="ffn1")
    h2 = _rms_fwd([x1], [gains["mix_norm"]], name="mix_norm")
    p_sb = _mm(h2, w_in_sb, outs=(BF16,), name="proj_sb")
    p_dl = _mm(h2, w_in_dl, name="proj_dl")
    q_dl = _rotary(p_dl, cos_t, sin_t, col0=0, width=d_dl, sign=1.0, name="rope_q")
    k_dl = _rotary(p_dl, cos_t, sin_t, col0=d_dl, width=d_dl, sign=1.0, name="rope_k")
    v_dl = p_dl[:, 2 * d_dl:]
    o_sb, carries = _sb_fwd(p_sb, name="sb_fwd")
    o_dl, lse_dl = _dl_fwd(q_dl, k_dl, v_dl, name="dl_fwd")
    merged = _rms_fwd([o_sb, o_dl], [gains["sb_out_norm"], gains["dil_out_norm"]], name="out_norm")
    x2 = _mm(merged, w_out, res=x1, name="out_proj")
    x3, saved2 = _ffn_fwd(x2, gains["ffn2_norm"], weights["ffn2_w_gate"], weights["ffn2_w_up"],
                          weights["ffn2_w_down"], tag="ffn2")
    dx3, d_final, loss_row = _loss_head(x3, gains["final_norm"], target, name="loss_head")

    dx2, d_ffn2_norm, dwg2, dwu2, dwd2 = _ffn_bwd(dx3, x2, gains["ffn2_norm"], weights["ffn2_w_gate"],
                                                  weights["ffn2_w_up"], weights["ffn2_w_down"], saved2, tag="ffn2")
    d_w_out = _mm(merged, dx2, ta=True, name="d_w_out")
    d_merged = _mm(dx2, w_out, tb=True, name="d_merged")
    (do_sb, do_dl), (d_sb_norm, d_dl_norm) = _rms_bwd(
        d_merged, [o_sb, o_dl], [gains["sb_out_norm"], gains["dil_out_norm"]], None, name="out_norm_bwd")
    dq_sb, dk_sb, dv_sb = _sb_bwd(p_sb, do_sb, carries, name="sb_bwd")
    dq_dl, dk_dl, dv_dl = _dl_bwd(q_dl, k_dl, v_dl, o_dl, lse_dl, do_dl, name="dl_bwd")
    dq_dl = _rotary(dq_dl, cos_t, sin_t, col0=0, width=d_dl, sign=-1.0, name="rope_dq")
    dk_dl = _rotary(dk_dl, cos_t, sin_t, col0=0, width=d_dl, sign=-1.0, name="rope_dk")
    d_proj = jnp.concatenate([p.astype(BF16) for p in (dq_sb, dk_sb, dv_sb, dq_dl, dk_dl, dv_dl)], axis=1)
    d_w_in = _mm(h2, d_proj, ta=True, name="d_w_in")
    dh2 = _mm(d_proj, w_in, tb=True, name="dh_mix")
    (dx1,), (d_mix_norm,) = _rms_bwd(dh2, [x1], [gains["mix_norm"]], dx2, name="mix_norm_bwd")
    dx, d_ffn1_norm, dwg1, dwu1, dwd1 = _ffn_bwd(dx1, x, gains["ffn1_norm"], weights["ffn1_w_gate"],
                                                 weights["ffn1_w_up"], weights["ffn1_w_down"], saved1, tag="ffn1")
    gain_grads = dict(ffn1_norm=d_ffn1_norm, mix_norm=d_mix_norm, sb_out_norm=d_sb_norm, dil_out_norm=d_dl_norm,
                      ffn2_norm=d_ffn2_norm, final_norm=d_final)
    weight_grads = dict(ffn1_w_gate=dwg1, ffn1_w_up=dwu1, ffn1_w_down=dwd1, w_in=d_w_in, w_out=d_w_out,
                        ffn2_w_gate=dwg2, ffn2_w_up=dwu2, ffn2_w_down=dwd2)
    return loss_row, dx, gain_grads, weight_grads


def _mesh_position():
    return lax.axis_index("x"), lax.axis_index("y"), lax.axis_index("c")


def _flip(coord, bit):
    return 1 - coord if bit else coord


RELATIONS = [(rx, ry, rc) for rx in (0, 1) for ry in (0, 1) for rc in (0, 1)][1:]


def _all_gather(shard, *, name):
    R, C = shard.shape

    def body(x_ref, out_ref, send_sems, recv_sems, local_sem):
        x, y, c = _mesh_position()
        me, sibling = (x, y, c), (x, y, 1 - c)
        chips = [(1 - x, y), (x, 1 - y), (1 - x, 1 - y)]

        def slot(px, py, pc):
            return out_ref.at[4 * px + 2 * py + pc]

        def copy(k, block, to, src=None):
            return pltpu.make_async_remote_copy(
                src_ref=slot(*block) if src is None else src, dst_ref=slot(*block),
                send_sem=send_sems.at[k], recv_sem=recv_sems.at[k],
                device_id=to, device_id_type=pl.DeviceIdType.MESH)

        mine = pltpu.make_async_copy(x_ref, slot(*me), local_sem)
        mine.start()
        first = [copy(0, me, sibling, src=x_ref)]
        first += [copy(1 + j, me, (*chip, c), src=x_ref) for j, chip in enumerate(chips)]
        for cp in first:
            cp.start()
        passed = [copy(4 + j, (*chip, c), sibling) for j, chip in enumerate(chips)]
        for j, chip in enumerate(chips):
            copy(1 + j, (*chip, c), me).wait_recv()
            passed[j].start()
        copy(0, sibling, me).wait_recv()
        for j, chip in enumerate(chips):
            copy(4 + j, (*chip, 1 - c), me).wait_recv()
        for cp in first + passed:
            cp.wait_send()
        mine.wait()

    return pl.pallas_call(
        body, name=name,
        out_shape=jax.ShapeDtypeStruct((N_DEV, R, C), shard.dtype),
        in_specs=[pl.BlockSpec(memory_space=pl.ANY)],
        out_specs=pl.BlockSpec(memory_space=pl.ANY),
        scratch_shapes=[pltpu.SemaphoreType.DMA((7,)), pltpu.SemaphoreType.DMA((7,)), pltpu.SemaphoreType.DMA],
    )(shard)


def _exchange_chunks(packs, *, name):
    n = len(packs)

    def body(*refs):
        in_refs, out_refs = refs[:n], refs[n:2 * n]
        send_sems, recv_sems, local_sems = refs[2 * n:]
        x, y, c = _mesh_position()
        me = 4 * x + 2 * y + c
        copies = []
        for t in range(n):
            local = pltpu.make_async_copy(in_refs[t].at[me], out_refs[t].at[me], local_sems.at[t])
            local.start()
            copies.append(local)
        for r, (rx, ry, rc) in enumerate(RELATIONS):
            px, py, pc = _flip(x, rx), _flip(y, ry), _flip(c, rc)
            peer = 4 * px + 2 * py + pc
            for t in range(n):
                cp = pltpu.make_async_remote_copy(
                    src_ref=in_refs[t].at[peer], dst_ref=out_refs[t].at[me],
                    send_sem=send_sems.at[t, r], recv_sem=recv_sems.at[t, r],
                    device_id=(px, py, pc), device_id_type=pl.DeviceIdType.MESH)
                cp.start()
                copies.append(cp)
        for cp in copies:
            cp.wait()

    return pl.pallas_call(
        body, name=name,
        out_shape=[jax.ShapeDtypeStruct(p.shape, p.dtype) for p in packs],
        in_specs=[pl.BlockSpec(memory_space=pl.ANY)] * n,
        out_specs=[pl.BlockSpec(memory_space=pl.ANY)] * n,
        scratch_shapes=[pltpu.SemaphoreType.DMA((n, 7)), pltpu.SemaphoreType.DMA((n, 7)),
                        pltpu.SemaphoreType.DMA((n,))],
    )(*packs)


def _all_reduce_rows(v, *, name):
    R, C = v.shape

    def body(v_ref, out_ref, buf, send_sems, recv_sems):
        x, y, c = _mesh_position()
        me = 4 * x + 2 * y + c
        buf[me] = v_ref[...]
        copies = []
        for r, (rx, ry, rc) in enumerate(RELATIONS):
            cp = pltpu.make_async_remote_copy(
                src_ref=v_ref, dst_ref=buf.at[me], send_sem=send_sems.at[r], recv_sem=recv_sems.at[r],
                device_id=(_flip(x, rx), _flip(y, ry), _flip(c, rc)), device_id_type=pl.DeviceIdType.MESH)
            cp.start()
            copies.append(cp)
        for cp in copies:
            cp.wait()
        total = buf[0]
        for s in range(1, N_DEV):
            total = total + buf[s]
        out_ref[...] = total

    return pl.pallas_call(
        body, name=name,
        out_shape=jax.ShapeDtypeStruct((R, C), F32),
        in_specs=[pl.BlockSpec(memory_space=pltpu.VMEM)],
        out_specs=pl.BlockSpec(memory_space=pltpu.VMEM),
        scratch_shapes=[pltpu.VMEM((N_DEV, R, C), F32), pltpu.SemaphoreType.DMA((7,)), pltpu.SemaphoreType.DMA((7,))],
    )(v)


def _sum_slots(recv, *, name):
    _, R, C = recv.shape
    tr = _pick(R, (256, 208, 128, 64, 32, 16))

    def body(r_ref, o_ref):
        total = r_ref[0].astype(F32)
        for s in range(1, N_DEV):
            total = total + r_ref[s].astype(F32)
        o_ref[...] = total

    return pl.pallas_call(
        body, name=name, grid=(R // tr,),
        out_shape=jax.ShapeDtypeStruct((R, C), F32),
        in_specs=[pl.BlockSpec((N_DEV, tr, C), lambda i: (0, i, 0))],
        out_specs=pl.BlockSpec((tr, C), lambda i: (i, 0)),
        compiler_params=_cparams(("parallel",)),
    )(recv)


def _adamw(w, g, m, v, *, name):
    R, C = w.shape
    tr = _pick(R, (256, 128, 64, 32, 16, 8))

    def body(w_ref, g_ref, m_ref, v_ref, d_ref, nm_ref, nv_ref):
        g = g_ref[...]
        m_new = ADAM_B1 * m_ref[...] + (1.0 - ADAM_B1) * g
        v_new = ADAM_B2 * v_ref[...] + (1.0 - ADAM_B2) * (g * g)
        m_hat = m_new / (1.0 - ADAM_B1 ** ADAM_STEP)
        v_hat = v_new / (1.0 - ADAM_B2 ** ADAM_STEP)
        d_ref[...] = -ADAM_LR * (m_hat / (jnp.sqrt(v_hat) + ADAM_EPS) + ADAM_WD * w_ref[...])
        nm_ref[...] = m_new
        nv_ref[...] = v_new

    spec = pl.BlockSpec((tr, C), lambda i: (i, 0))
    return pl.pallas_call(
        body, name=name, grid=(R // tr,),
        out_shape=[jax.ShapeDtypeStruct((R, C), F32)] * 3,
        in_specs=[spec] * 4, out_specs=[spec] * 3,
        compiler_params=_cparams(("parallel",)),
    )(w, g, m, v)


WEIGHT_NAMES = ["ffn1_norm", "ffn1_w_gate", "ffn1_w_up", "ffn1_w_down", "mix_norm", "w_in", "sb_out_norm",
                "dil_out_norm", "w_out", "ffn2_norm", "ffn2_w_gate", "ffn2_w_up", "ffn2_w_down", "final_norm"]
GAIN_NAMES = ["ffn1_norm", "mix_norm", "sb_out_norm", "dil_out_norm", "ffn2_norm", "final_norm"]
COL_SHARDED = ["ffn1_w_gate", "ffn1_w_up", "ffn2_w_gate", "ffn2_w_up", "w_in"]
ROW_SHARDED = ["ffn1_w_down", "ffn2_w_down", "w_out"]


def _step(x, target, params, moments_m, moments_v):
    col_pack = jnp.concatenate([params[n] for n in COL_SHARDED], axis=1).astype(BF16)
    row_pack = jnp.concatenate([params[n] for n in ROW_SHARDED], axis=0).astype(BF16)
    col_all = _all_gather(col_pack, name="gather_col")
    row_all = _all_gather(row_pack, name="gather_row")
    weights = {}
    off = 0
    for n in COL_SHARDED:
        w = params[n].shape[1]
        piece = col_all[:, :, off:off + w]
        weights[n] = jnp.transpose(piece, (1, 0, 2)).reshape(piece.shape[1], N_DEV * w)
        off += w
    off = 0
    for n in ROW_SHARDED:
        r = params[n].shape[0]
        weights[n] = row_all[:, off:off + r, :].reshape(N_DEV * r, row_all.shape[2])
        off += r

    gains = {n: params[n] for n in GAIN_NAMES}
    loss_row, grad_x, gain_grads, weight_grads = _local_step(x, target, gains, weights)

    col_chunks, row_chunks = [], []
    for n in COL_SHARDED:
        g = weight_grads[n]
        w = params[n].shape[1]
        col_chunks.append(jnp.transpose(g.reshape(g.shape[0], N_DEV, w), (1, 0, 2)))
    for n in ROW_SHARDED:
        g = weight_grads[n]
        r = params[n].shape[0]
        row_chunks.append(g.reshape(N_DEV, r, g.shape[1]))
    col_send = jnp.concatenate(col_chunks, axis=2).astype(BF16)
    row_send = jnp.concatenate(row_chunks, axis=1).astype(BF16)
    col_recv, row_recv = _exchange_chunks([col_send, row_send], name="exchange_grads")
    col_grad = _sum_slots(col_recv, name="sum_col_grads")
    row_grad = _sum_slots(row_recv, name="sum_row_grads")
    grads = {}
    off = 0
    for n in COL_SHARDED:
        w = params[n].shape[1]
        grads[n] = col_grad[:, off:off + w]
        off += w
    off = 0
    for n in ROW_SHARDED:
        r = params[n].shape[0]
        grads[n] = row_grad[off:off + r, :]
        off += r

    rows = [gain_grads[n].reshape(-1, LANES) for n in GAIN_NAMES] + [loss_row]
    small = jnp.concatenate(rows, axis=0)
    pad = (-small.shape[0]) % 8
    small = jnp.pad(small, ((0, pad), (0, 0)))
    small = _all_reduce_rows(small, name="reduce_gains_loss")
    off = 0
    for n in GAIN_NAMES:
        r = gain_grads[n].shape[1] // LANES
        grads[n] = small[off:off + r].reshape(1, -1)
        off += r
    loss = small[off, 0]

    delta, new_m, new_v = {}, {}, {}
    for n in WEIGHT_NAMES:
        delta[n], new_m[n], new_v[n] = _adamw(params[n], grads[n], moments_m[n], moments_v[n], name=f"adamw_{n}")
    return loss, grad_x, grads, delta, new_m, new_v


def kernel(x, ffn1_norm, ffn1_w_gate, ffn1_w_up, ffn1_w_down, mix_norm, w_in, sb_out_norm, dil_out_norm, w_out, ffn2_norm, ffn2_w_gate, ffn2_w_up, ffn2_w_down, final_norm, loss_target, m_ffn1_norm, m_ffn1_w_gate, m_ffn1_w_up, m_ffn1_w_down, m_mix_norm, m_w_in, m_sb_out_norm, m_dil_out_norm, m_w_out, m_ffn2_norm, m_ffn2_w_gate, m_ffn2_w_up, m_ffn2_w_down, m_final_norm, v_ffn1_norm, v_ffn1_w_gate, v_ffn1_w_up, v_ffn1_w_down, v_mix_norm, v_w_in, v_sb_out_norm, v_dil_out_norm, v_w_out, v_ffn2_norm, v_ffn2_w_gate, v_ffn2_w_up, v_ffn2_w_down, v_final_norm):
    given = dict(locals())
    shapes = {n: given[n].shape for n in WEIGHT_NAMES}

    def as2d(a):
        return a.reshape(1, -1) if a.ndim == 1 else a.reshape(a.shape[-2], a.shape[-1])

    params = {n: as2d(given[n]) for n in WEIGHT_NAMES}
    moments_m = {n: as2d(given["m_" + n]) for n in WEIGHT_NAMES}
    moments_v = {n: as2d(given["v_" + n]) for n in WEIGHT_NAMES}
    loss, grad_x, grads, delta, new_m, new_v = _step(x[0], loss_target[0], params, moments_m, moments_v)
    back = lambda d: [d[n].reshape(shapes[n]) for n in WEIGHT_NAMES]
    return (loss, grad_x[None], *back(grads), *back(delta), *back(new_m), *back(new_v))
```

```python
import functools

import jax
import jax.numpy as jnp
from jax import lax
from jax.experimental import pallas as pl
from jax.experimental.pallas import tpu as pltpu

F32 = jnp.float32
BF16 = jnp.bfloat16

N_DEV = 8
HEAD_DIM = 64
LANES = 128
DILATED_PATTERNS = ((128, 1), (512, 4), (2048, 16))
DIL_BLOCK = 128
DIL_SUPER = 2048
DIL_UNROLL = 4
SB_TILE = 256
SB_UNROLL = 4
SB_DEAD = 90.0
SB_UNSEEN = -1e30
ROPE_THETA = 10000.0
RMS_EPS = 1e-6
HALF_STEP = 0.5
ADAM_LR = 0.001
ADAM_B1 = 0.9
ADAM_B2 = 0.999
ADAM_EPS = 1e-08
ADAM_WD = 0.01
ADAM_STEP = 10
NEG_BIG = -1e30
VMEM_CAP_MB = 60


def _pick(n, prefs):
    for p in prefs:
        if n % p == 0:
            return p
    return n


MM_MAX_TILE = 1536


def _largest_tile(n, cap):
    if n <= cap:
        return n
    for t in range(cap - cap % LANES, 0, -LANES):
        if n % t == 0:
            return t
    return n


def _cparams(sem=None, vmem_mb=48):
    return pltpu.CompilerParams(dimension_semantics=sem, vmem_limit_bytes=min(vmem_mb, VMEM_CAP_MB) * 1024 * 1024)


def _nbytes(shape, dtype):
    n = 1
    for s in shape:
        n *= s
    return n * jnp.dtype(dtype).itemsize


def _mm(a, b, *, name, ta=False, tb=False, outs=(F32,), res=None, alpha=1.0, extras=(), epilogue=None,
        tm=None, tn=None, tk=None, comm=None):
    if ta:
        K, M = a.shape
    else:
        M, K = a.shape
    if tb:
        N, Kb = b.shape
    else:
        Kb, N = b.shape
    assert K == Kb, (a.shape, b.shape, ta, tb)
    tm = tm or (_largest_tile(M, MM_MAX_TILE) if ta else _pick(M, (512, 256, 128)))
    tn = tn or _largest_tile(N, MM_MAX_TILE)
    tk = tk or (K if K <= 3072 else _pick(K, (1024, 512, 256, 128)))
    nk = K // tk
    a_spec = pl.BlockSpec((tk, tm), lambda i, j, k: (k, i)) if ta else pl.BlockSpec((tm, tk), lambda i, j, k: (i, k))
    b_spec = pl.BlockSpec((tn, tk), lambda i, j, k: (j, k)) if tb else pl.BlockSpec((tk, tn), lambda i, j, k: (k, j))
    mn_spec = pl.BlockSpec((tm, tn), lambda i, j, k: (i, j))
    dims = (((0 if ta else 1,), (1 if tb else 0,)), ((), ()))
    n_extra = len(extras) + (1 if res is not None else 0)
    n_out = len(outs)
    grid = (M // tm, N // tn, nk)
    hosted = _Hosted(comm, n_in=2 + n_extra, n_out=n_out, n_scratch=1 if nk > 1 else 0)

    def body(*refs):
        a_ref, b_ref = refs[0], refs[1]
        in_refs = refs[2:2 + n_extra]
        refs = hosted.begin(refs, grid)
        out_refs = refs[2 + n_extra:2 + n_extra + n_out]
        prod = lax.dot_general(a_ref[...].astype(BF16), b_ref[...].astype(BF16), dims, preferred_element_type=F32)

        def finish(acc):
            blocks = [r[...] for r in in_refs]
            if res is not None:
                r_blk, blocks = blocks[0], blocks[1:]
            else:
                r_blk = None
            if epilogue is None:
                val = acc * alpha
                if r_blk is not None:
                    val = val + r_blk
                vals = (val,)
            else:
                vals = epilogue(acc, r_blk, *blocks)
            for o_ref, v in zip(out_refs, vals):
                o_ref[...] = v.astype(o_ref.dtype)

        if nk == 1:
            finish(prod)
        else:
            acc_ref = refs[2 + n_extra + n_out]
            k = pl.program_id(2)

            @pl.when(k == 0)
            def _():
                acc_ref[...] = prod

            @pl.when(k > 0)
            def _():
                acc_ref[...] += prod

            @pl.when(k == nk - 1)
            def _():
                finish(acc_ref[...])

        hosted.end(grid)

    operands = [a, b] + ([res] if res is not None else []) + list(extras)
    in_specs = [a_spec, b_spec] + [mn_spec] * n_extra
    est = 2 * (_nbytes((tm, tk), a.dtype) + _nbytes((tk, tn), b.dtype))
    est += 2 * sum(_nbytes((tm, tn), o.dtype) for o in operands[2:])
    est += 2 * sum(_nbytes((tm, tn), d) for d in outs) + _nbytes((tm, tn), F32)
    result = pl.pallas_call(
        body, name=name, grid=grid,
        out_shape=[jax.ShapeDtypeStruct((M, N), d) for d in outs] + hosted.out_shapes,
        in_specs=in_specs + hosted.in_specs, out_specs=[mn_spec] * n_out + hosted.out_specs,
        scratch_shapes=([pltpu.VMEM((tm, tn), F32)] if nk > 1 else []) + hosted.scratch,
        compiler_params=_cparams(hosted.semantics(("parallel", "parallel", "arbitrary")),
                                 vmem_mb=max(32, 2 * est // (1024 * 1024))),
    )(*operands, *hosted.operands)
    own, got = result[:n_out], list(result[n_out:])
    own = own[0] if n_out == 1 else own
    return own if comm is None else (own, got)


def _rms_hat(x):
    r = lax.rsqrt(jnp.mean(x * x, axis=-1, keepdims=True) + RMS_EPS)
    return x * r, r


def _rms_fwd(xs, gains, *, name):
    S = xs[0].shape[0]
    widths = [x.shape[1] for x in xs]
    tm = _pick(S, (512, 256, 128))
    n = len(xs)

    def body(*refs):
        o_ref = refs[2 * n]
        off = 0
        for i in range(n):
            xh, _ = _rms_hat(refs[i][...])
            o_ref[:, off:off + widths[i]] = (xh * refs[n + i][...]).astype(o_ref.dtype)
            off += widths[i]

    return pl.pallas_call(
        body, name=name, grid=(S // tm,),
        out_shape=jax.ShapeDtypeStruct((S, sum(widths)), BF16),
        in_specs=[pl.BlockSpec((tm, w), lambda i: (i, 0)) for w in widths]
        + [pl.BlockSpec((1, w), lambda i: (0, 0)) for w in widths],
        out_specs=pl.BlockSpec((tm, sum(widths)), lambda i: (i, 0)),
        compiler_params=_cparams(("parallel",)),
    )(*xs, *gains)


def _rms_bwd(dh, xs, gains, res, *, name):
    S = xs[0].shape[0]
    widths = [x.shape[1] for x in xs]
    tm = _pick(S, (512, 256, 128))
    n = len(xs)
    has_res = res is not None

    def body(*refs):
        dh_ref = refs[0]
        x_refs = refs[1:1 + n]
        g_refs = refs[1 + n:1 + 2 * n]
        r_ref = refs[1 + 2 * n] if has_res else None
        base = 1 + 2 * n + (1 if has_res else 0)
        dx_refs = refs[base:base + n]
        dg_refs = refs[base + n:base + 2 * n]
        first = pl.program_id(0) == 0
        off = 0
        for i in range(n):
            x = x_refs[i][...]
            xh, r = _rms_hat(x)
            d = dh_ref[:, off:off + widths[i]]
            dxh = d * g_refs[i][...]
            dx = r * (dxh - xh * jnp.mean(dxh * xh, axis=-1, keepdims=True))
            if has_res:
                dx = dx + r_ref[...]
            dx_refs[i][...] = dx
            part = jnp.sum(d * xh, axis=0, keepdims=True)

            @pl.when(first)
            def _(i=i, part=part):
                dg_refs[i][...] = part

            @pl.when(jnp.logical_not(first))
            def _(i=i, part=part):
                dg_refs[i][...] += part

            off += widths[i]

    in_specs = [pl.BlockSpec((tm, sum(widths)), lambda i: (i, 0))]
    in_specs += [pl.BlockSpec((tm, w), lambda i: (i, 0)) for w in widths]
    in_specs += [pl.BlockSpec((1, w), lambda i: (0, 0)) for w in widths]
    operands = [dh, *xs, *gains]
    if has_res:
        in_specs.append(pl.BlockSpec((tm, widths[0]), lambda i: (i, 0)))
        operands.append(res)
    out = pl.pallas_call(
        body, name=name, grid=(S // tm,),
        out_shape=[jax.ShapeDtypeStruct((S, w), F32) for w in widths] + [jax.ShapeDtypeStruct((1, w), F32) for w in widths],
        in_specs=in_specs,
        out_specs=[pl.BlockSpec((tm, w), lambda i: (i, 0)) for w in widths]
        + [pl.BlockSpec((1, w), lambda i: (0, 0)) for w in widths],
        compiler_params=_cparams(("arbitrary",)),
    )(*operands)
    return out[:n], out[n:]


def _sigmoid(g):
    return 1.0 / (1.0 + jnp.exp(-g))


def _ffn_fwd(x, gain, wg, wu, wd, *, tag, comm=None):
    h = _rms_fwd([x], [gain], name=f"{tag}_norm")
    g = _mm(h, wg, outs=(BF16,), name=f"{tag}_gate", comm=comm)
    g, got = g if comm is not None else (g, None)

    def act(acc, _, g_blk):
        gf = g_blk.astype(F32)
        return acc, gf * _sigmoid(gf) * acc

    u, a = _mm(h, wu, outs=(BF16, BF16), extras=(g,), epilogue=act, name=f"{tag}_up_act")
    y = _mm(a, wd, res=x, alpha=HALF_STEP, name=f"{tag}_down")
    return y, (h, g, u, a), got


def _ffn_bwd(dout, x, gain, wg, wu, wd, saved, *, tag, comm=None):
    h, g, u, a = saved

    def act_bwd(acc, _, g_blk, u_blk):
        gf, uf = g_blk.astype(F32), u_blk.astype(F32)
        da = acc * HALF_STEP
        sig = _sigmoid(gf)
        silu = gf * sig
        return da * uf * (sig * (1.0 + gf * (1.0 - sig))), da * silu

    first = _mm(dout, wd, tb=True, outs=(BF16, BF16), extras=(g, u), epilogue=act_bwd, name=f"{tag}_bwd_act",
                comm=comm)
    (dg, du), got = first if comm is not None else (first, None)
    dwg = _mm(h, dg, ta=True, name=f"{tag}_dwg")
    dwu = _mm(h, du, ta=True, name=f"{tag}_dwu")
    dwd = _mm(a, dout, ta=True, alpha=HALF_STEP, name=f"{tag}_dwd")
    dh = _mm(dg, wg, tb=True, name=f"{tag}_dh_gate")
    dh = _mm(du, wu, tb=True, res=dh, name=f"{tag}_dh_up")
    (dx,), (dgain,) = _rms_bwd(dh, [x], [gain], dout, name=f"{tag}_norm_bwd")
    return dx, dgain, dwg, dwu, dwd, got


def _rope_tables(S):
    half = HEAD_DIM // 2
    inv_freq = ROPE_THETA ** (-jnp.arange(half, dtype=F32) / half)
    ang = jnp.arange(S, dtype=F32)[:, None] * inv_freq[None, :]
    cos, sin = jnp.cos(ang), jnp.sin(ang)
    reps = LANES // HEAD_DIM
    cos_t = jnp.tile(jnp.concatenate([cos, cos], axis=1), (1, reps))
    sin_t = jnp.tile(jnp.concatenate([-sin, sin], axis=1), (1, reps))
    return cos_t, sin_t


def _rotary(t, cos_t, sin_t, *, col0, width, sign, name):
    S = t.shape[0]
    tm = _pick(S, (512, 256, 128))
    half = HEAD_DIM // 2
    c0 = col0 // LANES

    def body(t_ref, c_ref, s_ref, o_ref):
        v = t_ref[...]
        lane = lax.broadcasted_iota(jnp.int32, v.shape, 1)
        swapped = jnp.where(lane % HEAD_DIM < half, pltpu.roll(v, LANES - half, axis=1), pltpu.roll(v, half, axis=1))
        o_ref[...] = v * c_ref[...] + swapped * (s_ref[...] * sign)

    return pl.pallas_call(
        body, name=name, grid=(S // tm, width // LANES),
        out_shape=jax.ShapeDtypeStruct((S, width), F32),
        in_specs=[pl.BlockSpec((tm, LANES), lambda i, j: (i, c0 + j)),
                  pl.BlockSpec((tm, LANES), lambda i, j: (i, 0)),
                  pl.BlockSpec((tm, LANES), lambda i, j: (i, 0))],
        out_specs=pl.BlockSpec((tm, LANES), lambda i, j: (i, j)),
        compiler_params=_cparams(("parallel", "parallel")),
    )(t, cos_t, sin_t)


def _head_masks(shape):
    lane = lax.broadcasted_iota(jnp.int32, shape, 1)
    return [(lane >= HEAD_DIM * h) & (lane < HEAD_DIM * (h + 1)) for h in range(LANES // HEAD_DIM)]


def _sb_scores(q2, k_j):
    z = lax.dot_general(q2, k_j, (((1,), (1,)), ((), ())), preferred_element_type=F32)
    sign_bit = jnp.int32(-2 ** 31)
    minus_abs = lax.bitcast_convert_type(lax.bitcast_convert_type(z, jnp.int32) | sign_bit, F32)
    softplus = jnp.maximum(z, 0.0) + jnp.log(1.0 + jnp.exp(minus_abs))
    return z - softplus, softplus


def _sb_stack_heads(t, scale=None):
    parts = [jnp.where(hm, t, jnp.zeros_like(t)) for hm in _head_masks(t.shape)]
    t2 = jnp.concatenate(parts, axis=0)
    if scale is not None:
        t2 = (t2.astype(F32) * scale).astype(t2.dtype)
    return t2


def _sb_unstack_heads(t2):
    T = t2.shape[0] // 2
    masks = _head_masks((T, LANES))
    return jnp.where(masks[0], t2[:T], t2[T:])


def _sb_causal(T):
    row = lax.broadcasted_iota(jnp.int32, (2 * T, T), 0)
    col = lax.broadcasted_iota(jnp.int32, (2 * T, T), 1)
    return col < jnp.where(row >= T, row - T, row)


def _sb_triangle(T, later):
    row = lax.broadcasted_iota(jnp.int32, (T, T), 0)
    col = lax.broadcasted_iota(jnp.int32, (T, T), 1)
    return ((row > col) if later else (row < col)).astype(BF16)


def _sb_fwd(p_sb, *, name, comm=None):
    S = p_sb.shape[0]
    W = p_sb.shape[1] // 3
    npair = W // LANES
    T = SB_TILE
    n_tiles = S // T
    assert n_tiles <= HEAD_DIM
    scale = HEAD_DIM ** -0.5

    grid = (npair, n_tiles)
    hosted = _Hosted(comm, n_in=3, n_out=2, n_scratch=0)

    def body(*refs):
        q_ref, k_ref, v_ref, o_ref, c_ref = hosted.begin(refs, grid)
        I = pl.program_id(1)
        lane = lax.broadcasted_iota(jnp.int32, (T, LANES), 1)
        causal = _sb_causal(T)
        later_than = _sb_triangle(T, True)
        q2 = _sb_stack_heads(q_ref[...], scale)

        def scores(J, diag):
            off = pl.multiple_of(J * T, T)
            log_beta, stay = _sb_scores(q2, k_ref[pl.ds(off, T), :])
            if diag:
                stay = jnp.where(causal, stay, 0.0)
            local = jnp.dot(stay.astype(BF16), later_than, preferred_element_type=F32)
            return log_beta, local, jnp.sum(stay, axis=1, keepdims=True), v_ref[pl.ds(off, T), :]

        def weigh(J, sc, gone, acc, carr, diag):
            log_beta, local, _, v_j = sc
            w = jnp.exp((log_beta - gone) - local)
            if diag:
                w = jnp.where(causal, w, 0.0)
            acc = acc + jnp.dot(w.astype(BF16), v_j, preferred_element_type=F32)
            carr = jnp.where(lane == J, -gone[:T], carr)
            carr = jnp.where(lane == HEAD_DIM + J, -gone[T:], carr)
            return acc, carr

        def tiles(J, count, state, diag):
            gone, acc, carr, _ = state
            scs = [scores(J - u, diag) for u in range(count)]
            for u, sc in enumerate(scs):
                acc, carr = weigh(J - u, sc, gone, acc, carr, diag)
                gone = gone + sc[2]
            return gone, acc, carr, jnp.min(gone)

        U = SB_UNROLL
        alive = lambda st: st[3] < SB_DEAD
        state = (jnp.zeros((2 * T, 1), F32), jnp.zeros((2 * T, LANES), F32),
                 jnp.full((T, LANES), SB_UNSEEN, F32), jnp.zeros((), F32))
        state = tiles(I, 1, state, True)
        singles = jnp.where(I > 0, (I - 1) % U + 1, 0)
        _, state = lax.while_loop(lambda c: (c[0] < singles) & alive(c[1]),
                                  lambda c: (c[0] + 1, tiles(I - 1 - c[0], 1, c[1], False)), (jnp.int32(0), state))
        blocks = (I - singles) // U
        _, state = lax.while_loop(lambda c: (c[0] < blocks) & alive(c[1]),
                                  lambda c: (c[0] + 1, tiles(I - 1 - singles - U * c[0], U, c[1], False)),
                                  (jnp.int32(0), state))
        _, acc, carr, _ = state
        o_ref[...] = _sb_unstack_heads(acc)
        c_ref[...] = carr
        hosted.end(grid)

    blk = lambda I_off: pl.BlockSpec((T, LANES), lambda p, I: (I, I_off + p))
    full = lambda off: pl.BlockSpec((S, LANES), lambda p, I: (0, off + p))
    o, carries, *got = pl.pallas_call(
        body, name=name, grid=grid,
        out_shape=[jax.ShapeDtypeStruct((S, W), F32), jax.ShapeDtypeStruct((S, W), F32)] + hosted.out_shapes,
        in_specs=[blk(0), full(npair), full(2 * npair)] + hosted.in_specs,
        out_specs=[blk(0), blk(0)] + hosted.out_specs,
        scratch_shapes=hosted.scratch,
        compiler_params=_cparams(hosted.semantics(("parallel", "arbitrary"))),
    )(p_sb, p_sb, p_sb, *hosted.operands)
    return (o, carries) if comm is None else (o, carries, got)


def _sb_bwd(p_sb, do, carries, *, name, comm=None):
    S = p_sb.shape[0]
    W = p_sb.shape[1] // 3
    npair = W // LANES
    T = SB_TILE
    n_tiles = S // T
    scale = HEAD_DIM ** -0.5

    grid = (npair, n_tiles)
    hosted = _Hosted(comm, n_in=5, n_out=3, n_scratch=0)

    def body(*refs):
        q_ref, k_ref, v_ref, do_ref, c_ref, dq_ref, dk_ref, dv_ref = hosted.begin(refs, grid)
        I = pl.program_id(1)

        @pl.when(I == 0)
        def _():
            dk_ref[...] = jnp.zeros_like(dk_ref)
            dv_ref[...] = jnp.zeros_like(dv_ref)

        lane = lax.broadcasted_iota(jnp.int32, (T, LANES), 1)
        causal = _sb_causal(T)
        later_than = _sb_triangle(T, True)
        earlier_than = _sb_triangle(T, False)
        q2 = _sb_stack_heads(q_ref[...], scale)
        do2 = _sb_stack_heads(do_ref[...].astype(BF16))
        carr = c_ref[...]
        tn_dims = (((0,), (0,)), ((), ()))

        def chain(J, diag):
            off = pl.multiple_of(J * T, T)
            k_j = k_ref[pl.ds(off, T), :]
            v_j = v_ref[pl.ds(off, T), :]
            log_beta, stay = _sb_scores(q2, k_j)
            if diag:
                stay = jnp.where(causal, stay, 0.0)
            lc = jnp.concatenate(
                [jnp.sum(jnp.where(lane == HEAD_DIM * h + J, carr, 0.0), axis=1, keepdims=True) for h in range(2)],
                axis=0)
            w = jnp.exp((log_beta + lc) - jnp.dot(stay.astype(BF16), later_than, preferred_element_type=F32))
            if diag:
                w = jnp.where(causal, w, 0.0)
            dw = lax.dot_general(do2, v_j, (((1,), (1,)), ((), ())), preferred_element_type=F32)
            e = w * dw
            local = jnp.dot(e.astype(BF16), earlier_than, preferred_element_type=F32)
            return off, k_j, w, e, local, jnp.exp(log_beta), jnp.sum(e, axis=1, keepdims=True)

        def finish(ch, ec, dq_acc, diag):
            off, k_j, w, e, local, beta, _ = ch
            e_before = local + ec
            dz = e - beta * (e + e_before)
            if diag:
                dz = jnp.where(causal, dz, 0.0)
            dzb = dz.astype(BF16)
            dq_acc = dq_acc + jnp.dot(dzb, k_j, preferred_element_type=F32)
            dk_ref[pl.ds(off, T), :] += lax.dot_general(dzb, q2, tn_dims, preferred_element_type=F32)
            dv_ref[pl.ds(off, T), :] += lax.dot_general(w.astype(BF16), do2, tn_dims, preferred_element_type=F32)
            return dq_acc

        def tiles(J, count, state, diag):
            ec, dq_acc = state
            chains = [chain(J + u, diag) for u in range(count)]
            for ch in chains:
                dq_acc = finish(ch, ec, dq_acc, diag)
                ec = ec + ch[6]
            return ec, dq_acc

        lane_row = lax.broadcasted_iota(jnp.int32, (1, LANES), 1)
        reached = (jnp.max(carr, axis=0, keepdims=True) > 0.5 * SB_UNSEEN) & (lane_row < HEAD_DIM)
        first = jnp.min(jnp.where(reached, lane_row.astype(F32), float(n_tiles))).astype(jnp.int32)
        U = SB_UNROLL
        count = I - first
        state = (jnp.zeros((2 * T, 1), F32), jnp.zeros((2 * T, LANES), F32))
        state = lax.fori_loop(0, count // U, lambda jj, st: tiles(first + U * jj, U, st, False), state)
        state = lax.fori_loop(0, count % U, lambda r, st: tiles(I - count % U + r, 1, st, False), state)
        _, dq_acc = tiles(I, 1, state, True)
        dq_ref[...] = _sb_unstack_heads(dq_acc) * scale
        hosted.end(grid)

    blk = lambda src_off: pl.BlockSpec((T, LANES), lambda p, I: (I, src_off + p))
    full = lambda off: pl.BlockSpec((S, LANES), lambda p, I: (0, off + p))
    dq, dk, dv, *got = pl.pallas_call(
        body, name=name, grid=grid,
        out_shape=[jax.ShapeDtypeStruct((S, W), F32)] * 3 + hosted.out_shapes,
        in_specs=[blk(0), full(npair), full(2 * npair), blk(0), blk(0)] + hosted.in_specs,
        out_specs=[blk(0), full(0), full(0)] + hosted.out_specs,
        scratch_shapes=hosted.scratch,
        compiler_params=_cparams(hosted.semantics(("parallel", "arbitrary"))),
    )(p_sb, p_sb, p_sb, do, carries, *hosted.operands)
    return (dq, dk, dv) if comm is None else (dq, dk, dv, got)


def _dil_blocks(b, body_fn):
    for pi, (window, dil) in enumerate(DILATED_PATTERNS):
        assert window // dil == DIL_BLOCK
        nblk = DIL_SUPER // (DIL_BLOCK * dil)
        assert (dil * nblk) % DIL_UNROLL == 0

        def group(g, _, pi=pi, dil=dil, nblk=nblk):
            for u in range(DIL_UNROLL):
                t = g * DIL_UNROLL + u
                n = t % nblk
                body_fn(pi, dil, t // nblk, n, b * nblk + n)
            return 0

        lax.fori_loop(0, dil * nblk // DIL_UNROLL, group, 0)


def _dil_rows(start, size, dil):
    if dil == 1:
        return pl.ds(pl.multiple_of(start, DIL_BLOCK), size)
    return pl.ds(start, size, stride=dil)


def _dil_valid(gn, ws):
    row = lax.broadcasted_iota(jnp.int32, (2 * DIL_BLOCK, 2 * DIL_BLOCK), 0)
    kk = lax.broadcasted_iota(jnp.int32, (2 * DIL_BLOCK, 2 * DIL_BLOCK), 1)
    qi = jnp.where(row >= DIL_BLOCK, row - DIL_BLOCK, row)
    dist = (gn - ws) * DIL_BLOCK + qi - kk
    return (dist >= 0) & (dist <= DIL_BLOCK)


def _dl_fwd(q, k, v, *, name):
    S, W = q.shape
    npair = W // LANES
    nsuper = S // DIL_SUPER
    assert S % DIL_SUPER == 0 and S // max(d for _, d in DILATED_PATTERNS) >= 2 * DIL_BLOCK
    scale = HEAD_DIM ** -0.5
    npat = len(DILATED_PATTERNS)

    def body(q_ref, k_ref, v_ref, o_ref, l_ref, *pattern_refs):
        op_refs, lp_refs = pattern_refs[:npat], pattern_refs[npat:]
        b = pl.program_id(1)
        masks = _head_masks((DIL_BLOCK, LANES))

        def block(pi, dil, c, n, gn):
            ws = jnp.maximum(gn - 1, 0)
            qrows = n * (DIL_BLOCK * dil) + c
            krows = ws * (DIL_BLOCK * dil) + c
            q_idx = _dil_rows(qrows, DIL_BLOCK, dil)
            k_idx = _dil_rows(krows, 2 * DIL_BLOCK, dil)
            qb = q_ref[q_idx, :]
            kb = k_ref[k_idx, :].astype(BF16)
            vb = v_ref[k_idx, :].astype(BF16)
            valid = _dil_valid(gn, ws)
            q2 = _sb_stack_heads(qb.astype(BF16), scale)
            z = lax.dot_general(q2, kb, (((1,), (1,)), ((), ())), preferred_element_type=F32)
            z = jnp.where(valid, z, NEG_BIG)
            m = jnp.max(z, axis=1, keepdims=True)
            p = jnp.exp(z - m)
            den = jnp.sum(p, axis=1, keepdims=True)
            acc = jnp.dot(p.astype(BF16), vb, preferred_element_type=F32)
            lse = m + jnp.log(den)
            op_refs[pi][q_idx, :] = _sb_unstack_heads(acc / den)
            lp_refs[pi][q_idx, :] = jnp.where(masks[0], lse[:DIL_BLOCK], lse[DIL_BLOCK:])

        _dil_blocks(b, block)
        lses = [r[...] for r in lp_refs]
        top = functools.reduce(jnp.maximum, lses)
        ws_ = [jnp.exp(l - top) for l in lses]
        den = functools.reduce(jnp.add, ws_)
        num = functools.reduce(jnp.add, [w * r[...] for r, w in zip(op_refs, ws_)])
        o_ref[...] = num / den
        l_ref[...] = top + jnp.log(den)

    blk = pl.BlockSpec((DIL_SUPER, LANES), lambda p, b: (b, p))
    full = pl.BlockSpec((S, LANES), lambda p, b: (0, p))
    return pl.pallas_call(
        body, name=name, grid=(npair, nsuper),
        out_shape=[jax.ShapeDtypeStruct((S, W), F32)] * 2,
        in_specs=[blk, full, full], out_specs=[blk, blk],
        scratch_shapes=[pltpu.VMEM((DIL_SUPER, LANES), F32)] * (2 * npat),
        compiler_params=_cparams(("parallel", "arbitrary")),
    )(q, k, v)


def _dl_bwd(q, k, v, o, lse, do, *, name):
    S, W = q.shape
    npair = W // LANES
    nsuper = S // DIL_SUPER
    scale = HEAD_DIM ** -0.5

    def body(q_ref, k_ref, v_ref, o_ref, l_ref, do_ref, dq_ref, dk_ref, dv_ref, delta_ref):
        b = pl.program_id(1)

        @pl.when(b == 0)
        def _():
            dk_ref[...] = jnp.zeros_like(dk_ref)
            dv_ref[...] = jnp.zeros_like(dv_ref)

        dq_ref[...] = jnp.zeros_like(dq_ref)
        prod = do_ref[...] * o_ref[...]
        delta = jnp.zeros_like(prod)
        for hm in _head_masks(prod.shape):
            delta = jnp.where(hm, jnp.sum(jnp.where(hm, prod, 0.0), axis=1, keepdims=True), delta)
        delta_ref[...] = delta

        def block(pi, dil, c, n, gn):
            ws = jnp.maximum(gn - 1, 0)
            qrows = n * (DIL_BLOCK * dil) + c
            krows = ws * (DIL_BLOCK * dil) + c
            q_idx = _dil_rows(qrows, DIL_BLOCK, dil)
            k_idx = _dil_rows(krows, 2 * DIL_BLOCK, dil)
            qb = q_ref[q_idx, :]
            dob = do_ref[q_idx, :]
            lb = l_ref[q_idx, :]
            db = delta_ref[q_idx, :]
            kb = k_ref[k_idx, :].astype(BF16)
            vb = v_ref[k_idx, :].astype(BF16)
            valid = _dil_valid(gn, ws)
            q2 = _sb_stack_heads(qb.astype(BF16), scale)
            do2 = _sb_stack_heads(dob.astype(BF16))
            lse2 = jnp.concatenate([lb[:, HEAD_DIM * h:HEAD_DIM * h + 1] for h in range(2)], axis=0)
            delta2 = jnp.concatenate([db[:, HEAD_DIM * h:HEAD_DIM * h + 1] for h in range(2)], axis=0)
            z = lax.dot_general(q2, kb, (((1,), (1,)), ((), ())), preferred_element_type=F32)
            p = jnp.where(valid, jnp.exp(jnp.where(valid, z, NEG_BIG) - lse2), 0.0)
            dp = lax.dot_general(do2, vb, (((1,), (1,)), ((), ())), preferred_element_type=F32)
            dzb = (p * (dp - delta2)).astype(BF16)
            tn_dims = (((0,), (0,)), ((), ()))
            dq_blk = _sb_unstack_heads(jnp.dot(dzb, kb, preferred_element_type=F32)) * scale
            dk_blk = lax.dot_general(dzb, q2, tn_dims, preferred_element_type=F32)
            dv_blk = lax.dot_general(p.astype(BF16), do2, tn_dims, preferred_element_type=F32)
            dq_ref[q_idx, :] = dq_ref[q_idx, :] + dq_blk
            dk_ref[k_idx, :] = dk_ref[k_idx, :] + dk_blk
            dv_ref[k_idx, :] = dv_ref[k_idx, :] + dv_blk

        _dil_blocks(b, block)

    blk = pl.BlockSpec((DIL_SUPER, LANES), lambda p, b: (b, p))
    full = pl.BlockSpec((S, LANES), lambda p, b: (0, p))
    return pl.pallas_call(
        body, name=name, grid=(npair, nsuper),
        out_shape=[jax.ShapeDtypeStruct((S, W), F32)] * 3,
        in_specs=[blk, full, full, blk, blk, blk], out_specs=[blk, full, full],
        scratch_shapes=[pltpu.VMEM((DIL_SUPER, LANES), F32)],
        compiler_params=_cparams(("parallel", "arbitrary")),
    )(q, k, v, o, lse, do)


def _loss_head(x, gain, target, *, name):
    S, D = x.shape
    tm = _pick(S, (512, 256, 128))

    def body(x_ref, g_ref, t_ref, dx_ref, dg_ref, loss_ref):
        first = pl.program_id(0) == 0
        xh, r = _rms_hat(x_ref[...])
        g = g_ref[...]
        err = xh * g - t_ref[...]
        dy = err * (1.0 / D)
        dxh = dy * g
        dx_ref[...] = r * (dxh - xh * jnp.mean(dxh * xh, axis=-1, keepdims=True))
        dg_part = jnp.sum(dy * xh, axis=0, keepdims=True)
        loss_part = jnp.zeros((1, LANES), F32) + 0.5 * jnp.sum(jnp.mean(err * err, axis=-1, keepdims=True),
                                                               axis=0, keepdims=True)

        @pl.when(first)
        def _():
            dg_ref[...] = dg_part
            loss_ref[...] = loss_part

        @pl.when(jnp.logical_not(first))
        def _():
            dg_ref[...] += dg_part
            loss_ref[...] += loss_part

    row = pl.BlockSpec((tm, D), lambda i: (i, 0))
    vec = pl.BlockSpec((1, D), lambda i: (0, 0))
    return pl.pallas_call(
        body, name=name, grid=(S // tm,),
        out_shape=[jax.ShapeDtypeStruct((S, D), F32), jax.ShapeDtypeStruct((1, D), F32),
                   jax.ShapeDtypeStruct((1, LANES), F32)],
        in_specs=[row, vec, row], out_specs=[row, vec, pl.BlockSpec((1, LANES), lambda i: (0, 0))],
        compiler_params=_cparams(("arbitrary",)),
    )(x, gain, target)


class _NoExchange:
    def gather(self, family):
        return None

    def gathered(self, family, got, weights):
        pass

    def send(self, family, grads):
        return None

    def received(self, family, got):
        pass


def _local_step(x, target, gains, weights, exchanges=None):
    S, D = x.shape
    ex = exchanges or _NoExchange()
    weights = dict(weights)
    d_sb = gains["sb_out_norm"].shape[1]
    d_dl = gains["dil_out_norm"].shape[1]
    cos_t, sin_t = _rope_tables(S)

    x1, saved1, got = _ffn_fwd(x, gains["ffn1_norm"], weights["ffn1_w_gate"], weights["ffn1_w_up"],
                               weights["ffn1_w_down"], tag="ffn1", comm=ex.gather("mixer"))
    ex.gathered("mixer", got, weights)
    w_in = weights["w_in"]
    w_in_sb, w_in_dl = w_in[:, :3 * d_sb], w_in[:, 3 * d_sb:]
    w_out = weights["w_out"]
    h2 = _rms_fwd([x1], [gains["mix_norm"]], name="mix_norm")
    p_sb = _mm(h2, w_in_sb, outs=(BF16,), name="proj_sb")
    p_dl = _mm(h2, w_in_dl, name="proj_dl")
    q_dl = _rotary(p_dl, cos_t, sin_t, col0=0, width=d_dl, sign=1.0, name="rope_q")
    k_dl = _rotary(p_dl, cos_t, sin_t, col0=d_dl, width=d_dl, sign=1.0, name="rope_k")
    v_dl = p_dl[:, 2 * d_dl:]
    plan = ex.gather("ffn2")
    o_sb, carries, *got = _sb_fwd(p_sb, name="sb_fwd", comm=plan)
    ex.gathered("ffn2", got[0] if got else None, weights)
    o_dl, lse_dl = _dl_fwd(q_dl, k_dl, v_dl, name="dl_fwd")
    merged = _rms_fwd([o_sb, o_dl], [gains["sb_out_norm"], gains["dil_out_norm"]], name="out_norm")
    x2 = _mm(merged, w_out, res=x1, name="out_proj")
    x3, saved2, _ = _ffn_fwd(x2, gains["ffn2_norm"], weights["ffn2_w_gate"], weights["ffn2_w_up"],
                             weights["ffn2_w_down"], tag="ffn2")
    dx3, d_final, loss_row = _loss_head(x3, gains["final_norm"], target, name="loss_head")

    dx2, d_ffn2_norm, dwg2, dwu2, dwd2, _ = _ffn_bwd(dx3, x2, gains["ffn2_norm"], weights["ffn2_w_gate"],
                                                     weights["ffn2_w_up"], weights["ffn2_w_down"], saved2, tag="ffn2")
    d_w_out = _mm(merged, dx2, ta=True, name="d_w_out")
    d_merged = _mm(dx2, w_out, tb=True, name="d_merged")
    (do_sb, do_dl), (d_sb_norm, d_dl_norm) = _rms_bwd(
        d_merged, [o_sb, o_dl], [gains["sb_out_norm"], gains["dil_out_norm"]], None, name="out_norm_bwd")
    plan = ex.send("ffn2", dict(ffn2_w_gate=dwg2, ffn2_w_up=dwu2, ffn2_w_down=dwd2))
    dq_sb, dk_sb, dv_sb, *got = _sb_bwd(p_sb, do_sb, carries, name="sb_bwd", comm=plan)
    ex.received("ffn2", got[0] if got else None)
    dq_dl, dk_dl, dv_dl = _dl_bwd(q_dl, k_dl, v_dl, o_dl, lse_dl, do_dl, name="dl_bwd")
    dq_dl = _rotary(dq_dl, cos_t, sin_t, col0=0, width=d_dl, sign=-1.0, name="rope_dq")
    dk_dl = _rotary(dk_dl, cos_t, sin_t, col0=0, width=d_dl, sign=-1.0, name="rope_dk")
    d_proj = jnp.concatenate([p.astype(BF16) for p in (dq_sb, dk_sb, dv_sb, dq_dl, dk_dl, dv_dl)], axis=1)
    d_w_in = _mm(h2, d_proj, ta=True, name="d_w_in")
    dh2 = _mm(d_proj, w_in, tb=True, name="dh_mix")
    (dx1,), (d_mix_norm,) = _rms_bwd(dh2, [x1], [gains["mix_norm"]], dx2, name="mix_norm_bwd")
    dx, d_ffn1_norm, dwg1, dwu1, dwd1, got = _ffn_bwd(
        dx1, x, gains["ffn1_norm"], weights["ffn1_w_gate"], weights["ffn1_w_up"], weights["ffn1_w_down"], saved1,
        tag="ffn1", comm=ex.send("mixer", dict(w_in=d_w_in, w_out=d_w_out)))
    ex.received("mixer", got)
    gain_grads = dict(ffn1_norm=d_ffn1_norm, mix_norm=d_mix_norm, sb_out_norm=d_sb_norm, dil_out_norm=d_dl_norm,
                      ffn2_norm=d_ffn2_norm, final_norm=d_final)
    weight_grads = dict(ffn1_w_gate=dwg1, ffn1_w_up=dwu1, ffn1_w_down=dwd1, w_in=d_w_in, w_out=d_w_out,
                        ffn2_w_gate=dwg2, ffn2_w_up=dwu2, ffn2_w_down=dwd2)
    return loss_row, dx, gain_grads, weight_grads


def _mesh_position():
    return lax.axis_index("x"), lax.axis_index("y"), lax.axis_index("c")


def _flip(coord, bit):
    return 1 - coord if bit else coord


RELATIONS = [(rx, ry, rc) for rx in (0, 1) for ry in (0, 1) for rc in (0, 1)][1:]


class _GatherPlan:
    def __init__(self, shards):
        n = len(shards)
        self.operands = list(shards)
        self.out_shapes = [jax.ShapeDtypeStruct((N_DEV,) + s.shape, s.dtype) for s in shards]
        self.scratch = [pltpu.SemaphoreType.DMA((n, 7)), pltpu.SemaphoreType.DMA((n, 7)),
                        pltpu.SemaphoreType.DMA((n,))]

    def _copies(self, in_refs, out_refs, sems):
        send_sems, recv_sems, local_sems = sems
        x, y, c = _mesh_position()
        me, sibling = (x, y, c), (x, y, 1 - c)
        chips = [(1 - x, y), (x, 1 - y), (1 - x, 1 - y)]
        plans = []
        for t, (x_ref, out_ref) in enumerate(zip(in_refs, out_refs)):
            def slot(px, py, pc, out_ref=out_ref):
                return out_ref.at[4 * px + 2 * py + pc]

            def copy(k, block, to, src=None, t=t, slot=slot):
                return pltpu.make_async_remote_copy(
                    src_ref=slot(*block) if src is None else src, dst_ref=slot(*block),
                    send_sem=send_sems.at[t, k], recv_sem=recv_sems.at[t, k],
                    device_id=to, device_id_type=pl.DeviceIdType.MESH)

            plans.append(dict(
                mine=pltpu.make_async_copy(x_ref, slot(*me), local_sems.at[t]),
                first=[copy(0, me, sibling, src=x_ref)]
                + [copy(1 + j, me, (*chip, c), src=x_ref) for j, chip in enumerate(chips)],
                over_ici=[copy(1 + j, (*chip, c), me) for j, chip in enumerate(chips)],
                passed=[copy(4 + j, (*chip, c), sibling) for j, chip in enumerate(chips)],
                from_sibling=[copy(0, sibling, me)] + [copy(4 + j, (*chip, 1 - c), me) for j, chip in enumerate(chips)]))
        return plans

    def start(self, in_refs, out_refs, sems):
        for p in self._copies(in_refs, out_refs, sems):
            p["mine"].start()
            for cp in p["first"]:
                cp.start()

    def finish(self, in_refs, out_refs, sems):
        plans = self._copies(in_refs, out_refs, sems)
        for p in plans:
            for arrived, onward in zip(p["over_ici"], p["passed"]):
                arrived.wait_recv()
                onward.start()
        for p in plans:
            for cp in p["from_sibling"]:
                cp.wait_recv()
            for cp in p["first"] + p["passed"]:
                cp.wait_send()
            p["mine"].wait()


class _Hosted:
    def __init__(self, plan, n_in, n_out, n_scratch):
        self.plan, self.n_in, self.n_out, self.n_scratch = plan, n_in, n_out, n_scratch
        self.operands = list(plan.operands) if plan else []
        self.out_shapes = list(plan.out_shapes) if plan else []
        self.scratch = list(plan.scratch) if plan else []
        self.in_specs = [pl.BlockSpec(memory_space=pl.ANY)] * len(self.operands)
        self.out_specs = [pl.BlockSpec(memory_space=pl.ANY)] * len(self.out_shapes)

    def semantics(self, sem):
        return sem if self.plan is None else ("arbitrary",) * len(sem)

    def _at(self, grid, last):
        hit = None
        for d, n in enumerate(grid):
            here = pl.program_id(d) == (n - 1 if last else 0)
            hit = here if hit is None else hit & here
        return hit

    def begin(self, refs, grid):
        if self.plan is None:
            return refs
        k_in, k_out = len(self.operands), len(self.out_shapes)
        ins, rest = refs[:self.n_in], refs[self.n_in:]
        c_in, rest = rest[:k_in], rest[k_in:]
        outs, rest = rest[:self.n_out], rest[self.n_out:]
        c_out, rest = rest[:k_out], rest[k_out:]
        scratch, sems = rest[:self.n_scratch], rest[self.n_scratch:]
        self._args = (c_in, c_out, sems)
        pl.when(self._at(grid, False))(lambda: self.plan.start(*self._args))
        return tuple(ins) + tuple(outs) + tuple(scratch)

    def end(self, grid):
        if self.plan is not None:
            pl.when(self._at(grid, True))(lambda: self.plan.finish(*self._args))


def _run_plan(plan, *, name):
    hosted = _Hosted(plan, 0, 0, 0)

    def body(*refs):
        hosted.begin(refs, (1,))
        hosted.end((1,))

    return pl.pallas_call(
        body, name=name, grid=(1,), out_shape=hosted.out_shapes,
        in_specs=hosted.in_specs, out_specs=hosted.out_specs, scratch_shapes=hosted.scratch,
        compiler_params=pltpu.CompilerParams(dimension_semantics=("arbitrary",)),
    )(*hosted.operands)


class _ExchangePlan:
    def __init__(self, packs):
        n = len(packs)
        self.operands = list(packs)
        self.out_shapes = [jax.ShapeDtypeStruct(p.shape, p.dtype) for p in packs]
        self.scratch = [pltpu.SemaphoreType.DMA((n, 7)), pltpu.SemaphoreType.DMA((n, 7)),
                        pltpu.SemaphoreType.DMA((n,))]

    def _copies(self, in_refs, out_refs, sems):
        send_sems, recv_sems, local_sems = sems
        x, y, c = _mesh_position()
        me = 4 * x + 2 * y + c
        copies = [pltpu.make_async_copy(i.at[me], o.at[me], local_sems.at[t])
                  for t, (i, o) in enumerate(zip(in_refs, out_refs))]
        for r, (rx, ry, rc) in enumerate(RELATIONS):
            px, py, pc = _flip(x, rx), _flip(y, ry), _flip(c, rc)
            peer = 4 * px + 2 * py + pc
            copies += [pltpu.make_async_remote_copy(
                src_ref=i.at[peer], dst_ref=o.at[me], send_sem=send_sems.at[t, r], recv_sem=recv_sems.at[t, r],
                device_id=(px, py, pc), device_id_type=pl.DeviceIdType.MESH)
                for t, (i, o) in enumerate(zip(in_refs, out_refs))]
        return copies

    def start(self, in_refs, out_refs, sems):
        for cp in self._copies(in_refs, out_refs, sems):
            cp.start()

    def finish(self, in_refs, out_refs, sems):
        for cp in self._copies(in_refs, out_refs, sems):
            cp.wait()


def _all_reduce_rows(v, *, name):
    R, C = v.shape

    def body(v_ref, out_ref, buf, send_sems, recv_sems):
        x, y, c = _mesh_position()
        me = 4 * x + 2 * y + c
        buf[me] = v_ref[...]
        copies = []
        for r, (rx, ry, rc) in enumerate(RELATIONS):
            cp = pltpu.make_async_remote_copy(
                src_ref=v_ref, dst_ref=buf.at[me], send_sem=send_sems.at[r], recv_sem=recv_sems.at[r],
                device_id=(_flip(x, rx), _flip(y, ry), _flip(c, rc)), device_id_type=pl.DeviceIdType.MESH)
            cp.start()
            copies.append(cp)
        for cp in copies:
            cp.wait()
        total = buf[0]
        for s in range(1, N_DEV):
            total = total + buf[s]
        out_ref[...] = total

    return pl.pallas_call(
        body, name=name,
        out_shape=jax.ShapeDtypeStruct((R, C), F32),
        in_specs=[pl.BlockSpec(memory_space=pltpu.VMEM)],
        out_specs=pl.BlockSpec(memory_space=pltpu.VMEM),
        scratch_shapes=[pltpu.VMEM((N_DEV, R, C), F32), pltpu.SemaphoreType.DMA((7,)), pltpu.SemaphoreType.DMA((7,))],
    )(v)


def _sum_slots(recv, *, name):
    _, R, C = recv.shape
    tr = _pick(R, (256, 208, 128, 64, 32, 16))

    def body(r_ref, o_ref):
        total = r_ref[0].astype(F32)
        for s in range(1, N_DEV):
            total = total + r_ref[s].astype(F32)
        o_ref[...] = total

    return pl.pallas_call(
        body, name=name, grid=(R // tr,),
        out_shape=jax.ShapeDtypeStruct((R, C), F32),
        in_specs=[pl.BlockSpec((N_DEV, tr, C), lambda i: (0, i, 0))],
        out_specs=pl.BlockSpec((tr, C), lambda i: (i, 0)),
        compiler_params=_cparams(("parallel",)),
    )(recv)


def _adamw(w, g, m, v, *, name):
    R, C = w.shape
    tr = _pick(R, (256, 128, 64, 32, 16, 8))

    def body(w_ref, g_ref, m_ref, v_ref, d_ref, nm_ref, nv_ref):
        g = g_ref[...]
        m_new = ADAM_B1 * m_ref[...] + (1.0 - ADAM_B1) * g
        v_new = ADAM_B2 * v_ref[...] + (1.0 - ADAM_B2) * (g * g)
        m_hat = m_new / (1.0 - ADAM_B1 ** ADAM_STEP)
        v_hat = v_new / (1.0 - ADAM_B2 ** ADAM_STEP)
        d_ref[...] = -ADAM_LR * (m_hat / (jnp.sqrt(v_hat) + ADAM_EPS) + ADAM_WD * w_ref[...])
        nm_ref[...] = m_new
        nv_ref[...] = v_new

    spec = pl.BlockSpec((tr, C), lambda i: (i, 0))
    return pl.pallas_call(
        body, name=name, grid=(R // tr,),
        out_shape=[jax.ShapeDtypeStruct((R, C), F32)] * 3,
        in_specs=[spec] * 4, out_specs=[spec] * 3,
        compiler_params=_cparams(("parallel",)),
    )(w, g, m, v)


WEIGHT_NAMES = ["ffn1_norm", "ffn1_w_gate", "ffn1_w_up", "ffn1_w_down", "mix_norm", "w_in", "sb_out_norm",
                "dil_out_norm", "w_out", "ffn2_norm", "ffn2_w_gate", "ffn2_w_up", "ffn2_w_down", "final_norm"]
GAIN_NAMES = ["ffn1_norm", "mix_norm", "sb_out_norm", "dil_out_norm", "ffn2_norm", "final_norm"]
COL_SHARDED = ["ffn1_w_gate", "ffn1_w_up", "ffn2_w_gate", "ffn2_w_up", "w_in"]
ROW_SHARDED = ["ffn1_w_down", "ffn2_w_down", "w_out"]
FAMILIES = {"ffn1": (["ffn1_w_gate", "ffn1_w_up"], ["ffn1_w_down"]),
            "mixer": (["w_in"], ["w_out"]),
            "ffn2": (["ffn2_w_gate", "ffn2_w_up"], ["ffn2_w_down"])}


class _Exchanges:
    def __init__(self, params):
        self.params = params
        self.grads = {}

    def gather(self, family):
        cols, rows = FAMILIES[family]
        col_pack = jnp.concatenate([self.params[n] for n in cols], axis=1).astype(BF16)
        row_pack = jnp.concatenate([self.params[n] for n in rows], axis=0).astype(BF16)
        return _GatherPlan([col_pack, row_pack])

    def gathered(self, family, got, weights):
        cols, rows = FAMILIES[family]
        col_all, row_all = got
        off = 0
        for n in cols:
            w = self.params[n].shape[1]
            piece = col_all[:, :, off:off + w]
            weights[n] = jnp.transpose(piece, (1, 0, 2)).reshape(piece.shape[1], N_DEV * w)
            off += w
        off = 0
        for n in rows:
            r = self.params[n].shape[0]
            weights[n] = row_all[:, off:off + r, :].reshape(N_DEV * r, row_all.shape[2])
            off += r

    def send(self, family, grads):
        cols, rows = FAMILIES[family]
        col_chunks, row_chunks = [], []
        for n in cols:
            g, w = grads[n], self.params[n].shape[1]
            col_chunks.append(jnp.transpose(g.reshape(g.shape[0], N_DEV, w), (1, 0, 2)))
        for n in rows:
            g, r = grads[n], self.params[n].shape[0]
            row_chunks.append(g.reshape(N_DEV, r, g.shape[1]))
        return _ExchangePlan([jnp.concatenate(col_chunks, axis=2).astype(BF16),
                              jnp.concatenate(row_chunks, axis=1).astype(BF16)])

    def received(self, family, got):
        cols, rows = FAMILIES[family]
        col_grad = _sum_slots(got[0], name=f"sum_col_grads_{family}")
        row_grad = _sum_slots(got[1], name=f"sum_row_grads_{family}")
        off = 0
        for n in cols:
            w = self.params[n].shape[1]
            self.grads[n] = col_grad[:, off:off + w]
            off += w
        off = 0
        for n in rows:
            r = self.params[n].shape[0]
            self.grads[n] = row_grad[off:off + r, :]
            off += r


def _step(x, target, params, moments_m, moments_v):
    ex = _Exchanges(params)
    weights = {}
    ex.gathered("ffn1", _run_plan(ex.gather("ffn1"), name="gather_ffn1"), weights)
    gains = {n: params[n] for n in GAIN_NAMES}
    loss_row, grad_x, gain_grads, weight_grads = _local_step(x, target, gains, weights, ex)
    ex.received("ffn1", _run_plan(ex.send("ffn1", weight_grads), name="exchange_ffn1_grads"))
    grads = ex.grads

    rows = [gain_grads[n].reshape(-1, LANES) for n in GAIN_NAMES] + [loss_row]
    small = jnp.concatenate(rows, axis=0)
    pad = (-small.shape[0]) % 8
    small = jnp.pad(small, ((0, pad), (0, 0)))
    small = _all_reduce_rows(small, name="reduce_gains_loss")
    off = 0
    for n in GAIN_NAMES:
        r = gain_grads[n].shape[1] // LANES
        grads[n] = small[off:off + r].reshape(1, -1)
        off += r
    loss = small[off, 0]

    delta, new_m, new_v = {}, {}, {}
    for n in WEIGHT_NAMES:
        delta[n], new_m[n], new_v[n] = _adamw(params[n], grads[n], moments_m[n], moments_v[n], name=f"adamw_{n}")
    return loss, grad_x, grads, delta, new_m, new_v


def kernel(x, ffn1_norm, ffn1_w_gate, ffn1_w_up, ffn1_w_down, mix_norm, w_in, sb_out_norm, dil_out_norm, w_out, ffn2_norm, ffn2_w_gate, ffn2_w_up, ffn2_w_down, final_norm, loss_target, m_ffn1_norm, m_ffn1_w_gate, m_ffn1_w_up, m_ffn1_w_down, m_mix_norm, m_w_in, m_sb_out_norm, m_dil_out_norm, m_w_out, m_ffn2_norm, m_ffn2_w_gate, m_ffn2_w_up, m_ffn2_w_down, m_final_norm, v_ffn1_norm, v_ffn1_w_gate, v_ffn1_w_up, v_ffn1_w_down, v_mix_norm, v_w_in, v_sb_out_norm, v_dil_out_norm, v_w_out, v_ffn2_norm, v_ffn2_w_gate, v_ffn2_w_up, v_ffn2_w_down, v_final_norm):
    given = dict(locals())
    shapes = {n: given[n].shape for n in WEIGHT_NAMES}

    def as2d(a):
        return a.reshape(1, -1) if a.ndim == 1 else a.reshape(a.shape[-2], a.shape[-1])

    params = {n: as2d(given[n]) for n in WEIGHT_NAMES}
    moments_m = {n: as2d(given["m_" + n]) for n in WEIGHT_NAMES}
    moments_v = {n: as2d(given["v_" + n]) for n in WEIGHT_NAMES}
    loss, grad_x, grads, delta, new_m, new_v = _step(x[0], loss_target[0], params, moments_m, moments_v)
    back = lambda d: [d[n].reshape(shapes[n]) for n in WEIGHT_NAMES]
    return (loss, grad_x[None], *back(grads), *back(delta), *back(new_m), *back(new_v))
```

```python
import functools

import jax
import jax.numpy as jnp
from jax import lax
from jax.experimental import pallas as pl
from jax.experimental.pallas import tpu as pltpu

F32 = jnp.float32
BF16 = jnp.bfloat16

N_DEV = 8
HEAD_DIM = 64
LANES = 128
DILATED_PATTERNS = ((128, 1), (512, 4), (2048, 16))
DIL_BLOCK = 128
DIL_SUPER = 2048
DIL_UNROLL = 4
SB_TILE = 256
SB_UNROLL = 4
SB_DEAD = 90.0
SB_UNSEEN = -1e30
ROPE_THETA = 10000.0
RMS_EPS = 1e-6
HALF_STEP = 0.5
ADAM_LR = 0.001
ADAM_B1 = 0.9
ADAM_B2 = 0.999
ADAM_EPS = 1e-08
ADAM_WD = 0.01
ADAM_STEP = 10
NEG_BIG = -1e30
VMEM_CAP_MB = 60


def _pick(n, prefs):
    for p in prefs:
        if n % p == 0:
            return p
    return n


MM_MAX_TILE = 1536


def _largest_tile(n, cap):
    if n <= cap:
        return n
    for t in range(cap - cap % LANES, 0, -LANES):
        if n % t == 0:
            return t
    return n


def _cparams(sem=None, vmem_mb=48):
    return pltpu.CompilerParams(dimension_semantics=sem, vmem_limit_bytes=min(vmem_mb, VMEM_CAP_MB) * 1024 * 1024)


def _nbytes(shape, dtype):
    n = 1
    for s in shape:
        n *= s
    return n * jnp.dtype(dtype).itemsize


def _mm(a, b, *, name, ta=False, tb=False, outs=(F32,), res=None, alpha=1.0, extras=(), epilogue=None,
        tm=None, tn=None, tk=None, comm=None):
    if ta:
        K, M = a.shape
    else:
        M, K = a.shape
    if tb:
        N, Kb = b.shape
    else:
        Kb, N = b.shape
    assert K == Kb, (a.shape, b.shape, ta, tb)
    tm = tm or (_largest_tile(M, MM_MAX_TILE) if ta else _pick(M, (512, 256, 128)))
    tn = tn or _largest_tile(N, MM_MAX_TILE)
    tk = tk or (K if K <= 3072 else _pick(K, (1024, 512, 256, 128)))
    nk = K // tk
    a_spec = pl.BlockSpec((tk, tm), lambda i, j, k: (k, i)) if ta else pl.BlockSpec((tm, tk), lambda i, j, k: (i, k))
    b_spec = pl.BlockSpec((tn, tk), lambda i, j, k: (j, k)) if tb else pl.BlockSpec((tk, tn), lambda i, j, k: (k, j))
    mn_spec = pl.BlockSpec((tm, tn), lambda i, j, k: (i, j))
    dims = (((0 if ta else 1,), (1 if tb else 0,)), ((), ()))
    n_extra = len(extras) + (1 if res is not None else 0)
    n_out = len(outs)
    grid = (M // tm, N // tn, nk)
    hosted = _Hosted(comm, n_in=2 + n_extra, n_out=n_out, n_scratch=1 if nk > 1 else 0)

    def body(*refs):
        a_ref, b_ref = refs[0], refs[1]
        in_refs = refs[2:2 + n_extra]
        refs = hosted.begin(refs, grid)
        out_refs = refs[2 + n_extra:2 + n_extra + n_out]
        prod = lax.dot_general(a_ref[...].astype(BF16), b_ref[...].astype(BF16), dims, preferred_element_type=F32)

        def finish(acc):
            blocks = [r[...] for r in in_refs]
            if res is not None:
                r_blk, blocks = blocks[0], blocks[1:]
            else:
                r_blk = None
            if epilogue is None:
                val = acc * alpha
                if r_blk is not None:
                    val = val + r_blk
                vals = (val,)
            else:
                vals = epilogue(acc, r_blk, *blocks)
            for o_ref, v in zip(out_refs, vals):
                o_ref[...] = v.astype(o_ref.dtype)

        if nk == 1:
            finish(prod)
        else:
            acc_ref = refs[2 + n_extra + n_out]
            k = pl.program_id(2)

            @pl.when(k == 0)
            def _():
                acc_ref[...] = prod

            @pl.when(k > 0)
            def _():
                acc_ref[...] += prod

            @pl.when(k == nk - 1)
            def _():
                finish(acc_ref[...])

        hosted.end(grid)

    operands = [a, b] + ([res] if res is not None else []) + list(extras)
    in_specs = [a_spec, b_spec] + [mn_spec] * n_extra
    est = 2 * (_nbytes((tm, tk), a.dtype) + _nbytes((tk, tn), b.dtype))
    est += 2 * sum(_nbytes((tm, tn), o.dtype) for o in operands[2:])
    est += 2 * sum(_nbytes((tm, tn), d) for d in outs) + _nbytes((tm, tn), F32)
    result = pl.pallas_call(
        body, name=name, grid=grid,
        out_shape=[jax.ShapeDtypeStruct((M, N), d) for d in outs] + hosted.out_shapes,
        in_specs=in_specs + hosted.in_specs, out_specs=[mn_spec] * n_out + hosted.out_specs,
        scratch_shapes=([pltpu.VMEM((tm, tn), F32)] if nk > 1 else []) + hosted.scratch,
        compiler_params=_cparams(hosted.semantics(("parallel", "parallel", "arbitrary")),
                                 vmem_mb=max(32, 2 * est // (1024 * 1024))),
    )(*operands, *hosted.operands)
    own, got = result[:n_out], list(result[n_out:])
    own = own[0] if n_out == 1 else own
    return own if comm is None else (own, got)


def _rms_hat(x):
    r = lax.rsqrt(jnp.mean(x * x, axis=-1, keepdims=True) + RMS_EPS)
    return x * r, r


def _rms_fwd(xs, gains, *, name):
    S = xs[0].shape[0]
    widths = [x.shape[1] for x in xs]
    tm = _pick(S, (512, 256, 128))
    n = len(xs)

    def body(*refs):
        o_ref = refs[2 * n]
        off = 0
        for i in range(n):
            xh, _ = _rms_hat(refs[i][...])
            o_ref[:, off:off + widths[i]] = (xh * refs[n + i][...]).astype(o_ref.dtype)
            off += widths[i]

    return pl.pallas_call(
        body, name=name, grid=(S // tm,),
        out_shape=jax.ShapeDtypeStruct((S, sum(widths)), BF16),
        in_specs=[pl.BlockSpec((tm, w), lambda i: (i, 0)) for w in widths]
        + [pl.BlockSpec((1, w), lambda i: (0, 0)) for w in widths],
        out_specs=pl.BlockSpec((tm, sum(widths)), lambda i: (i, 0)),
        compiler_params=_cparams(("parallel",)),
    )(*xs, *gains)


def _rms_bwd(dh, xs, gains, res, *, name):
    S = xs[0].shape[0]
    widths = [x.shape[1] for x in xs]
    tm = _pick(S, (512, 256, 128))
    n = len(xs)
    has_res = res is not None

    def body(*refs):
        dh_ref = refs[0]
        x_refs = refs[1:1 + n]
        g_refs = refs[1 + n:1 + 2 * n]
        r_ref = refs[1 + 2 * n] if has_res else None
        base = 1 + 2 * n + (1 if has_res else 0)
        dx_refs = refs[base:base + n]
        dg_refs = refs[base + n:base + 2 * n]
        first = pl.program_id(0) == 0
        off = 0
        for i in range(n):
            x = x_refs[i][...]
            xh, r = _rms_hat(x)
            d = dh_ref[:, off:off + widths[i]]
            dxh = d * g_refs[i][...]
            dx = r * (dxh - xh * jnp.mean(dxh * xh, axis=-1, keepdims=True))
            if has_res:
                dx = dx + r_ref[...]
            dx_refs[i][...] = dx
            part = jnp.sum(d * xh, axis=0, keepdims=True)

            @pl.when(first)
            def _(i=i, part=part):
                dg_refs[i][...] = part

            @pl.when(jnp.logical_not(first))
            def _(i=i, part=part):
                dg_refs[i][...] += part

            off += widths[i]

    in_specs = [pl.BlockSpec((tm, sum(widths)), lambda i: (i, 0))]
    in_specs += [pl.BlockSpec((tm, w), lambda i: (i, 0)) for w in widths]
    in_specs += [pl.BlockSpec((1, w), lambda i: (0, 0)) for w in widths]
    operands = [dh, *xs, *gains]
    if has_res:
        in_specs.append(pl.BlockSpec((tm, widths[0]), lambda i: (i, 0)))
        operands.append(res)
    out = pl.pallas_call(
        body, name=name, grid=(S // tm,),
        out_shape=[jax.ShapeDtypeStruct((S, w), F32) for w in widths] + [jax.ShapeDtypeStruct((1, w), F32) for w in widths],
        in_specs=in_specs,
        out_specs=[pl.BlockSpec((tm, w), lambda i: (i, 0)) for w in widths]
        + [pl.BlockSpec((1, w), lambda i: (0, 0)) for w in widths],
        compiler_params=_cparams(("arbitrary",)),
    )(*operands)
    return out[:n], out[n:]


def _sigmoid(g):
    return 1.0 / (1.0 + jnp.exp(-g))


def _ride(result, plan):
    return result if plan is not None else (result, None)


def _ffn_fwd(x, gain, w, *, tag, ex, riders=(None, None, None)):
    h = _rms_fwd([x], [gain], name=f"{tag}_norm")
    plan = ex.gather(riders[0])
    g, got = _ride(_mm(h, w[f"{tag}_w_gate"], outs=(BF16,), name=f"{tag}_gate", comm=plan), plan)
    ex.gathered(riders[0], got, w)

    def act(acc, _, g_blk):
        gf = g_blk.astype(F32)
        return acc, gf * _sigmoid(gf) * acc

    plan = ex.gather(riders[1])
    (u, a), got = _ride(_mm(h, w[f"{tag}_w_up"], outs=(BF16, BF16), extras=(g,), epilogue=act, name=f"{tag}_up_act",
                            comm=plan), plan)
    ex.gathered(riders[1], got, w)
    plan = ex.gather(riders[2])
    y, got = _ride(_mm(a, w[f"{tag}_w_down"], res=x, alpha=HALF_STEP, name=f"{tag}_down", comm=plan), plan)
    ex.gathered(riders[2], got, w)
    return y, (h, g, u, a)


def _ffn_bwd(dout, x, gain, w, saved, *, tag, ex, rider=(None, None), spread=False):
    h, g, u, a = saved
    wg, wu, wd = (w[f"{tag}_w_{n}"] for n in ("gate", "up", "down"))

    def act_bwd(acc, _, g_blk, u_blk):
        gf, uf = g_blk.astype(F32), u_blk.astype(F32)
        da = acc * HALF_STEP
        sig = _sigmoid(gf)
        silu = gf * sig
        return da * uf * (sig * (1.0 + gf * (1.0 - sig))), da * silu

    def carrying(group, grad, call):
        group = group if spread else None
        plan = ex.send(group, {group: grad})
        out, got = _ride(call(plan), plan)
        ex.received(group, got)
        return out

    plan = ex.send(*rider)
    (dg, du), got = _ride(_mm(dout, wd, tb=True, outs=(BF16, BF16), extras=(g, u), epilogue=act_bwd,
                              name=f"{tag}_bwd_act", comm=plan), plan)
    ex.received(rider[0], got)
    dwg = _mm(h, dg, ta=True, name=f"{tag}_dwg")
    dwu = carrying(f"{tag}_w_gate", dwg, lambda plan: _mm(h, du, ta=True, name=f"{tag}_dwu", comm=plan))
    dwd = carrying(f"{tag}_w_up", dwu,
                   lambda plan: _mm(a, dout, ta=True, alpha=HALF_STEP, name=f"{tag}_dwd", comm=plan))
    dh = carrying(f"{tag}_w_down", dwd, lambda plan: _mm(dg, wg, tb=True, name=f"{tag}_dh_gate", comm=plan))
    dh = _mm(du, wu, tb=True, res=dh, name=f"{tag}_dh_up")
    (dx,), (dgain,) = _rms_bwd(dh, [x], [gain], dout, name=f"{tag}_norm_bwd")
    return dx, dgain, dwg, dwu, dwd


def _rope_tables(S):
    half = HEAD_DIM // 2
    inv_freq = ROPE_THETA ** (-jnp.arange(half, dtype=F32) / half)
    ang = jnp.arange(S, dtype=F32)[:, None] * inv_freq[None, :]
    cos, sin = jnp.cos(ang), jnp.sin(ang)
    reps = LANES // HEAD_DIM
    cos_t = jnp.tile(jnp.concatenate([cos, cos], axis=1), (1, reps))
    sin_t = jnp.tile(jnp.concatenate([-sin, sin], axis=1), (1, reps))
    return cos_t, sin_t


def _rotary(t, cos_t, sin_t, *, col0, width, sign, name):
    S = t.shape[0]
    tm = _pick(S, (512, 256, 128))
    half = HEAD_DIM // 2
    c0 = col0 // LANES

    def body(t_ref, c_ref, s_ref, o_ref):
        v = t_ref[...]
        lane = lax.broadcasted_iota(jnp.int32, v.shape, 1)
        swapped = jnp.where(lane % HEAD_DIM < half, pltpu.roll(v, LANES - half, axis=1), pltpu.roll(v, half, axis=1))
        o_ref[...] = v * c_ref[...] + swapped * (s_ref[...] * sign)

    return pl.pallas_call(
        body, name=name, grid=(S // tm, width // LANES),
        out_shape=jax.ShapeDtypeStruct((S, width), F32),
        in_specs=[pl.BlockSpec((tm, LANES), lambda i, j: (i, c0 + j)),
                  pl.BlockSpec((tm, LANES), lambda i, j: (i, 0)),
                  pl.BlockSpec((tm, LANES), lambda i, j: (i, 0))],
        out_specs=pl.BlockSpec((tm, LANES), lambda i, j: (i, j)),
        compiler_params=_cparams(("parallel", "parallel")),
    )(t, cos_t, sin_t)


def _head_masks(shape):
    lane = lax.broadcasted_iota(jnp.int32, shape, 1)
    return [(lane >= HEAD_DIM * h) & (lane < HEAD_DIM * (h + 1)) for h in range(LANES // HEAD_DIM)]


def _sb_scores(q2, k_j):
    z = lax.dot_general(q2, k_j, (((1,), (1,)), ((), ())), preferred_element_type=F32)
    sign_bit = jnp.int32(-2 ** 31)
    minus_abs = lax.bitcast_convert_type(lax.bitcast_convert_type(z, jnp.int32) | sign_bit, F32)
    softplus = jnp.maximum(z, 0.0) + jnp.log(1.0 + jnp.exp(minus_abs))
    return z - softplus, softplus


def _sb_stack_heads(t, scale=None):
    parts = [jnp.where(hm, t, jnp.zeros_like(t)) for hm in _head_masks(t.shape)]
    t2 = jnp.concatenate(parts, axis=0)
    if scale is not None:
        t2 = (t2.astype(F32) * scale).astype(t2.dtype)
    return t2


def _sb_unstack_heads(t2):
    T = t2.shape[0] // 2
    masks = _head_masks((T, LANES))
    return jnp.where(masks[0], t2[:T], t2[T:])


def _sb_causal(T):
    row = lax.broadcasted_iota(jnp.int32, (2 * T, T), 0)
    col = lax.broadcasted_iota(jnp.int32, (2 * T, T), 1)
    return col < jnp.where(row >= T, row - T, row)


def _sb_triangle(T, later):
    row = lax.broadcasted_iota(jnp.int32, (T, T), 0)
    col = lax.broadcasted_iota(jnp.int32, (T, T), 1)
    return ((row > col) if later else (row < col)).astype(BF16)


def _sb_fwd(p_sb, *, name, comm=None):
    S = p_sb.shape[0]
    W = p_sb.shape[1] // 3
    npair = W // LANES
    T = SB_TILE
    n_tiles = S // T
    assert n_tiles <= HEAD_DIM
    scale = HEAD_DIM ** -0.5

    grid = (npair, n_tiles)
    hosted = _Hosted(comm, n_in=3, n_out=2, n_scratch=0)

    def body(*refs):
        q_ref, k_ref, v_ref, o_ref, c_ref = hosted.begin(refs, grid)
        I = pl.program_id(1)
        lane = lax.broadcasted_iota(jnp.int32, (T, LANES), 1)
        causal = _sb_causal(T)
        later_than = _sb_triangle(T, True)
        q2 = _sb_stack_heads(q_ref[...], scale)

        def scores(J, diag):
            off = pl.multiple_of(J * T, T)
            log_beta, stay = _sb_scores(q2, k_ref[pl.ds(off, T), :])
            if diag:
                stay = jnp.where(causal, stay, 0.0)
            local = jnp.dot(stay.astype(BF16), later_than, preferred_element_type=F32)
            return log_beta, local, jnp.sum(stay, axis=1, keepdims=True), v_ref[pl.ds(off, T), :]

        def weigh(J, sc, gone, acc, carr, diag):
            log_beta, local, _, v_j = sc
            w = jnp.exp((log_beta - gone) - local)
            if diag:
                w = jnp.where(causal, w, 0.0)
            acc = acc + jnp.dot(w.astype(BF16), v_j, preferred_element_type=F32)
            carr = jnp.where(lane == J, -gone[:T], carr)
            carr = jnp.where(lane == HEAD_DIM + J, -gone[T:], carr)
            return acc, carr

        def tiles(J, count, state, diag):
            gone, acc, carr, _ = state
            scs = [scores(J - u, diag) for u in range(count)]
            for u, sc in enumerate(scs):
                acc, carr = weigh(J - u, sc, gone, acc, carr, diag)
                gone = gone + sc[2]
            return gone, acc, carr, jnp.min(gone)

        U = SB_UNROLL
        alive = lambda st: st[3] < SB_DEAD
        state = (jnp.zeros((2 * T, 1), F32), jnp.zeros((2 * T, LANES), F32),
                 jnp.full((T, LANES), SB_UNSEEN, F32), jnp.zeros((), F32))
        state = tiles(I, 1, state, True)
        singles = jnp.where(I > 0, (I - 1) % U + 1, 0)
        _, state = lax.while_loop(lambda c: (c[0] < singles) & alive(c[1]),
                                  lambda c: (c[0] + 1, tiles(I - 1 - c[0], 1, c[1], False)), (jnp.int32(0), state))
        blocks = (I - singles) // U
        _, state = lax.while_loop(lambda c: (c[0] < blocks) & alive(c[1]),
                                  lambda c: (c[0] + 1, tiles(I - 1 - singles - U * c[0], U, c[1], False)),
                                  (jnp.int32(0), state))
        _, acc, carr, _ = state
        o_ref[...] = _sb_unstack_heads(acc)
        c_ref[...] = carr
        hosted.end(grid)

    blk = lambda I_off: pl.BlockSpec((T, LANES), lambda p, I: (I, I_off + p))
    full = lambda off: pl.BlockSpec((S, LANES), lambda p, I: (0, off + p))
    o, carries, *got = pl.pallas_call(
        body, name=name, grid=grid,
        out_shape=[jax.ShapeDtypeStruct((S, W), F32), jax.ShapeDtypeStruct((S, W), F32)] + hosted.out_shapes,
        in_specs=[blk(0), full(npair), full(2 * npair)] + hosted.in_specs,
        out_specs=[blk(0), blk(0)] + hosted.out_specs,
        scratch_shapes=hosted.scratch,
        compiler_params=_cparams(hosted.semantics(("parallel", "arbitrary"))),
    )(p_sb, p_sb, p_sb, *hosted.operands)
    return (o, carries) if comm is None else (o, carries, got)


def _sb_bwd(p_sb, do, carries, *, name, comm=None):
    S = p_sb.shape[0]
    W = p_sb.shape[1] // 3
    npair = W // LANES
    T = SB_TILE
    n_tiles = S // T
    scale = HEAD_DIM ** -0.5

    grid = (npair, n_tiles)
    hosted = _Hosted(comm, n_in=5, n_out=3, n_scratch=0)

    def body(*refs):
        q_ref, k_ref, v_ref, do_ref, c_ref, dq_ref, dk_ref, dv_ref = hosted.begin(refs, grid)
        I = pl.program_id(1)

        @pl.when(I == 0)
        def _():
            dk_ref[...] = jnp.zeros_like(dk_ref)
            dv_ref[...] = jnp.zeros_like(dv_ref)

        lane = lax.broadcasted_iota(jnp.int32, (T, LANES), 1)
        causal = _sb_causal(T)
        later_than = _sb_triangle(T, True)
        earlier_than = _sb_triangle(T, False)
        q2 = _sb_stack_heads(q_ref[...], scale)
        do2 = _sb_stack_heads(do_ref[...].astype(BF16))
        carr = c_ref[...]
        tn_dims = (((0,), (0,)), ((), ()))

        def chain(J, diag):
            off = pl.multiple_of(J * T, T)
            k_j = k_ref[pl.ds(off, T), :]
            v_j = v_ref[pl.ds(off, T), :]
            log_beta, stay = _sb_scores(q2, k_j)
            if diag:
                stay = jnp.where(causal, stay, 0.0)
            lc = jnp.concatenate(
                [jnp.sum(jnp.where(lane == HEAD_DIM * h + J, carr, 0.0), axis=1, keepdims=True) for h in range(2)],
                axis=0)
            w = jnp.exp((log_beta + lc) - jnp.dot(stay.astype(BF16), later_than, preferred_element_type=F32))
            if diag:
                w = jnp.where(causal, w, 0.0)
            dw = lax.dot_general(do2, v_j, (((1,), (1,)), ((), ())), preferred_element_type=F32)
            e = w * dw
            local = jnp.dot(e.astype(BF16), earlier_than, preferred_element_type=F32)
            return off, k_j, w, e, local, jnp.exp(log_beta), jnp.sum(e, axis=1, keepdims=True)

        def finish(ch, ec, dq_acc, diag):
            off, k_j, w, e, local, beta, _ = ch
            e_before = local + ec
            dz = e - beta * (e + e_before)
            if diag:
                dz = jnp.where(causal, dz, 0.0)
            dzb = dz.astype(BF16)
            dq_acc = dq_acc + jnp.dot(dzb, k_j, preferred_element_type=F32)
            dk_ref[pl.ds(off, T), :] += lax.dot_general(dzb, q2, tn_dims, preferred_element_type=F32)
            dv_ref[pl.ds(off, T), :] += lax.dot_general(w.astype(BF16), do2, tn_dims, preferred_element_type=F32)
            return dq_acc

        def tiles(J, count, state, diag):
            ec, dq_acc = state
            chains = [chain(J + u, diag) for u in range(count)]
            for ch in chains:
                dq_acc = finish(ch, ec, dq_acc, diag)
                ec = ec + ch[6]
            return ec, dq_acc

        lane_row = lax.broadcasted_iota(jnp.int32, (1, LANES), 1)
        reached = (jnp.max(carr, axis=0, keepdims=True) > 0.5 * SB_UNSEEN) & (lane_row < HEAD_DIM)
        first = jnp.min(jnp.where(reached, lane_row.astype(F32), float(n_tiles))).astype(jnp.int32)
        U = SB_UNROLL
        count = I - first
        state = (jnp.zeros((2 * T, 1), F32), jnp.zeros((2 * T, LANES), F32))
        state = lax.fori_loop(0, count // U, lambda jj, st: tiles(first + U * jj, U, st, False), state)
        state = lax.fori_loop(0, count % U, lambda r, st: tiles(I - count % U + r, 1, st, False), state)
        _, dq_acc = tiles(I, 1, state, True)
        dq_ref[...] = _sb_unstack_heads(dq_acc) * scale
        hosted.end(grid)

    blk = lambda src_off: pl.BlockSpec((T, LANES), lambda p, I: (I, src_off + p))
    full = lambda off: pl.BlockSpec((S, LANES), lambda p, I: (0, off + p))
    dq, dk, dv, *got = pl.pallas_call(
        body, name=name, grid=grid,
        out_shape=[jax.ShapeDtypeStruct((S, W), F32)] * 3 + hosted.out_shapes,
        in_specs=[blk(0), full(npair), full(2 * npair), blk(0), blk(0)] + hosted.in_specs,
        out_specs=[blk(0), full(0), full(0)] + hosted.out_specs,
        scratch_shapes=hosted.scratch,
        compiler_params=_cparams(hosted.semantics(("parallel", "arbitrary"))),
    )(p_sb, p_sb, p_sb, do, carries, *hosted.operands)
    return (dq, dk, dv) if comm is None else (dq, dk, dv, got)


def _dil_blocks(b, body_fn):
    for pi, (window, dil) in enumerate(DILATED_PATTERNS):
        assert window // dil == DIL_BLOCK
        nblk = DIL_SUPER // (DIL_BLOCK * dil)
        assert (dil * nblk) % DIL_UNROLL == 0

        def group(g, _, pi=pi, dil=dil, nblk=nblk):
            for u in range(DIL_UNROLL):
                t = g * DIL_UNROLL + u
                n = t % nblk
                body_fn(pi, dil, t // nblk, n, b * nblk + n)
            return 0

        lax.fori_loop(0, dil * nblk // DIL_UNROLL, group, 0)


def _dil_rows(start, size, dil):
    if dil == 1:
        return pl.ds(pl.multiple_of(start, DIL_BLOCK), size)
    return pl.ds(start, size, stride=dil)


def _dil_valid(gn, ws):
    row = lax.broadcasted_iota(jnp.int32, (2 * DIL_BLOCK, 2 * DIL_BLOCK), 0)
    kk = lax.broadcasted_iota(jnp.int32, (2 * DIL_BLOCK, 2 * DIL_BLOCK), 1)
    qi = jnp.where(row >= DIL_BLOCK, row - DIL_BLOCK, row)
    dist = (gn - ws) * DIL_BLOCK + qi - kk
    return (dist >= 0) & (dist <= DIL_BLOCK)


def _dl_fwd(q, k, v, *, name):
    S, W = q.shape
    npair = W // LANES
    nsuper = S // DIL_SUPER
    assert S % DIL_SUPER == 0 and S // max(d for _, d in DILATED_PATTERNS) >= 2 * DIL_BLOCK
    scale = HEAD_DIM ** -0.5
    npat = len(DILATED_PATTERNS)

    def body(q_ref, k_ref, v_ref, o_ref, l_ref, *pattern_refs):
        op_refs, lp_refs = pattern_refs[:npat], pattern_refs[npat:]
        b = pl.program_id(1)
        masks = _head_masks((DIL_BLOCK, LANES))

        def block(pi, dil, c, n, gn):
            ws = jnp.maximum(gn - 1, 0)
            qrows = n * (DIL_BLOCK * dil) + c
            krows = ws * (DIL_BLOCK * dil) + c
            q_idx = _dil_rows(qrows, DIL_BLOCK, dil)
            k_idx = _dil_rows(krows, 2 * DIL_BLOCK, dil)
            qb = q_ref[q_idx, :]
            kb = k_ref[k_idx, :].astype(BF16)
            vb = v_ref[k_idx, :].astype(BF16)
            valid = _dil_valid(gn, ws)
            q2 = _sb_stack_heads(qb.astype(BF16), scale)
            z = lax.dot_general(q2, kb, (((1,), (1,)), ((), ())), preferred_element_type=F32)
            z = jnp.where(valid, z, NEG_BIG)
            m = jnp.max(z, axis=1, keepdims=True)
            p = jnp.exp(z - m)
            den = jnp.sum(p, axis=1, keepdims=True)
            acc = jnp.dot(p.astype(BF16), vb, preferred_element_type=F32)
            lse = m + jnp.log(den)
            op_refs[pi][q_idx, :] = _sb_unstack_heads(acc / den)
            lp_refs[pi][q_idx, :] = jnp.where(masks[0], lse[:DIL_BLOCK], lse[DIL_BLOCK:])

        _dil_blocks(b, block)
        lses = [r[...] for r in lp_refs]
        top = functools.reduce(jnp.maximum, lses)
        ws_ = [jnp.exp(l - top) for l in lses]
        den = functools.reduce(jnp.add, ws_)
        num = functools.reduce(jnp.add, [w * r[...] for r, w in zip(op_refs, ws_)])
        o_ref[...] = num / den
        l_ref[...] = top + jnp.log(den)

    blk = pl.BlockSpec((DIL_SUPER, LANES), lambda p, b: (b, p))
    full = pl.BlockSpec((S, LANES), lambda p, b: (0, p))
    return pl.pallas_call(
        body, name=name, grid=(npair, nsuper),
        out_shape=[jax.ShapeDtypeStruct((S, W), F32)] * 2,
        in_specs=[blk, full, full], out_specs=[blk, blk],
        scratch_shapes=[pltpu.VMEM((DIL_SUPER, LANES), F32)] * (2 * npat),
        compiler_params=_cparams(("parallel", "arbitrary")),
    )(q, k, v)


def _dl_bwd(q, k, v, o, lse, do, *, name):
    S, W = q.shape
    npair = W // LANES
    nsuper = S // DIL_SUPER
    scale = HEAD_DIM ** -0.5

    def body(q_ref, k_ref, v_ref, o_ref, l_ref, do_ref, dq_ref, dk_ref, dv_ref, delta_ref):
        b = pl.program_id(1)

        @pl.when(b == 0)
        def _():
            dk_ref[...] = jnp.zeros_like(dk_ref)
            dv_ref[...] = jnp.zeros_like(dv_ref)

        dq_ref[...] = jnp.zeros_like(dq_ref)
        prod = do_ref[...] * o_ref[...]
        delta = jnp.zeros_like(prod)
        for hm in _head_masks(prod.shape):
            delta = jnp.where(hm, jnp.sum(jnp.where(hm, prod, 0.0), axis=1, keepdims=True), delta)
        delta_ref[...] = delta

        def block(pi, dil, c, n, gn):
            ws = jnp.maximum(gn - 1, 0)
            qrows = n * (DIL_BLOCK * dil) + c
            krows = ws * (DIL_BLOCK * dil) + c
            q_idx = _dil_rows(qrows, DIL_BLOCK, dil)
            k_idx = _dil_rows(krows, 2 * DIL_BLOCK, dil)
            qb = q_ref[q_idx, :]
            dob = do_ref[q_idx, :]
            lb = l_ref[q_idx, :]
            db = delta_ref[q_idx, :]
            kb = k_ref[k_idx, :].astype(BF16)
            vb = v_ref[k_idx, :].astype(BF16)
            valid = _dil_valid(gn, ws)
            q2 = _sb_stack_heads(qb.astype(BF16), scale)
            do2 = _sb_stack_heads(dob.astype(BF16))
            lse2 = jnp.concatenate([lb[:, HEAD_DIM * h:HEAD_DIM * h + 1] for h in range(2)], axis=0)
            delta2 = jnp.concatenate([db[:, HEAD_DIM * h:HEAD_DIM * h + 1] for h in range(2)], axis=0)
            z = lax.dot_general(q2, kb, (((1,), (1,)), ((), ())), preferred_element_type=F32)
            p = jnp.where(valid, jnp.exp(jnp.where(valid, z, NEG_BIG) - lse2), 0.0)
            dp = lax.dot_general(do2, vb, (((1,), (1,)), ((), ())), preferred_element_type=F32)
            dzb = (p * (dp - delta2)).astype(BF16)
            tn_dims = (((0,), (0,)), ((), ()))
            dq_blk = _sb_unstack_heads(jnp.dot(dzb, kb, preferred_element_type=F32)) * scale
            dk_blk = lax.dot_general(dzb, q2, tn_dims, preferred_element_type=F32)
            dv_blk = lax.dot_general(p.astype(BF16), do2, tn_dims, preferred_element_type=F32)
            dq_ref[q_idx, :] = dq_ref[q_idx, :] + dq_blk
            dk_ref[k_idx, :] = dk_ref[k_idx, :] + dk_blk
            dv_ref[k_idx, :] = dv_ref[k_idx, :] + dv_blk

        _dil_blocks(b, block)

    blk = pl.BlockSpec((DIL_SUPER, LANES), lambda p, b: (b, p))
    full = pl.BlockSpec((S, LANES), lambda p, b: (0, p))
    return pl.pallas_call(
        body, name=name, grid=(npair, nsuper),
        out_shape=[jax.ShapeDtypeStruct((S, W), F32)] * 3,
        in_specs=[blk, full, full, blk, blk, blk], out_specs=[blk, full, full],
        scratch_shapes=[pltpu.VMEM((DIL_SUPER, LANES), F32)],
        compiler_params=_cparams(("parallel", "arbitrary")),
    )(q, k, v, o, lse, do)


def _loss_head(x, gain, target, *, name):
    S, D = x.shape
    tm = _pick(S, (512, 256, 128))

    def body(x_ref, g_ref, t_ref, dx_ref, dg_ref, loss_ref):
        first = pl.program_id(0) == 0
        xh, r = _rms_hat(x_ref[...])
        g = g_ref[...]
        err = xh * g - t_ref[...]
        dy = err * (1.0 / D)
        dxh = dy * g
        dx_ref[...] = r * (dxh - xh * jnp.mean(dxh * xh, axis=-1, keepdims=True))
        dg_part = jnp.sum(dy * xh, axis=0, keepdims=True)
        loss_part = jnp.zeros((1, LANES), F32) + 0.5 * jnp.sum(jnp.mean(err * err, axis=-1, keepdims=True),
                                                               axis=0, keepdims=True)

        @pl.when(first)
        def _():
            dg_ref[...] = dg_part
            loss_ref[...] = loss_part

        @pl.when(jnp.logical_not(first))
        def _():
            dg_ref[...] += dg_part
            loss_ref[...] += loss_part

    row = pl.BlockSpec((tm, D), lambda i: (i, 0))
    vec = pl.BlockSpec((1, D), lambda i: (0, 0))
    return pl.pallas_call(
        body, name=name, grid=(S // tm,),
        out_shape=[jax.ShapeDtypeStruct((S, D), F32), jax.ShapeDtypeStruct((1, D), F32),
                   jax.ShapeDtypeStruct((1, LANES), F32)],
        in_specs=[row, vec, row], out_specs=[row, vec, pl.BlockSpec((1, LANES), lambda i: (0, 0))],
        compiler_params=_cparams(("arbitrary",)),
    )(x, gain, target)


class _NoExchange:
    def gather(self, family):
        return None

    def gathered(self, family, got, weights):
        pass

    def send(self, family, grads):
        return None

    def received(self, family, got):
        pass


def _local_step(x, target, gains, weights, exchanges=None):
    S, D = x.shape
    ex = exchanges or _NoExchange()
    weights = dict(weights)
    d_sb = gains["sb_out_norm"].shape[1]
    d_dl = gains["dil_out_norm"].shape[1]
    cos_t, sin_t = _rope_tables(S)

    riders = ("ffn1_w_up", "ffn1_w_down", "mixer") if exchanges else (None, None, None)
    x1, saved1 = _ffn_fwd(x, gains["ffn1_norm"], weights, tag="ffn1", ex=ex, riders=riders)
    w_in = weights["w_in"]
    w_in_sb, w_in_dl = w_in[:, :3 * d_sb], w_in[:, 3 * d_sb:]
    w_out = weights["w_out"]
    h2 = _rms_fwd([x1], [gains["mix_norm"]], name="mix_norm")
    p_sb = _mm(h2, w_in_sb, outs=(BF16,), name="proj_sb")
    p_dl = _mm(h2, w_in_dl, name="proj_dl")
    q_dl = _rotary(p_dl, cos_t, sin_t, col0=0, width=d_dl, sign=1.0, name="rope_q")
    k_dl = _rotary(p_dl, cos_t, sin_t, col0=d_dl, width=d_dl, sign=1.0, name="rope_k")
    v_dl = p_dl[:, 2 * d_dl:]
    plan = ex.gather("ffn2" if exchanges else None)
    o_sb, carries, *got = _sb_fwd(p_sb, name="sb_fwd", comm=plan)
    ex.gathered("ffn2", got[0] if got else None, weights)
    o_dl, lse_dl = _dl_fwd(q_dl, k_dl, v_dl, name="dl_fwd")
    merged = _rms_fwd([o_sb, o_dl], [gains["sb_out_norm"], gains["dil_out_norm"]], name="out_norm")
    x2 = _mm(merged, w_out, res=x1, name="out_proj")
    x3, saved2 = _ffn_fwd(x2, gains["ffn2_norm"], weights, tag="ffn2", ex=ex)
    dx3, d_final, loss_row = _loss_head(x3, gains["final_norm"], target, name="loss_head")

    dx2, d_ffn2_norm, dwg2, dwu2, dwd2 = _ffn_bwd(dx3, x2, gains["ffn2_norm"], weights, saved2, tag="ffn2", ex=ex)
    d_w_out = _mm(merged, dx2, ta=True, name="d_w_out")
    d_merged = _mm(dx2, w_out, tb=True, name="d_merged")
    (do_sb, do_dl), (d_sb_norm, d_dl_norm) = _rms_bwd(
        d_merged, [o_sb, o_dl], [gains["sb_out_norm"], gains["dil_out_norm"]], None, name="out_norm_bwd")
    plan = ex.send("ffn2", dict(ffn2_w_gate=dwg2, ffn2_w_up=dwu2, ffn2_w_down=dwd2))
    dq_sb, dk_sb, dv_sb, *got = _sb_bwd(p_sb, do_sb, carries, name="sb_bwd", comm=plan)
    ex.received("ffn2", got[0] if got else None)
    dq_dl, dk_dl, dv_dl = _dl_bwd(q_dl, k_dl, v_dl, o_dl, lse_dl, do_dl, name="dl_bwd")
    dq_dl = _rotary(dq_dl, cos_t, sin_t, col0=0, width=d_dl, sign=-1.0, name="rope_dq")
    dk_dl = _rotary(dk_dl, cos_t, sin_t, col0=0, width=d_dl, sign=-1.0, name="rope_dk")
    d_proj = jnp.concatenate([p.astype(BF16) for p in (dq_sb, dk_sb, dv_sb, dq_dl, dk_dl, dv_dl)], axis=1)
    d_w_in = _mm(h2, d_proj, ta=True, name="d_w_in")
    dh2 = _mm(d_proj, w_in, tb=True, name="dh_mix")
    (dx1,), (d_mix_norm,) = _rms_bwd(dh2, [x1], [gains["mix_norm"]], dx2, name="mix_norm_bwd")
    dx, d_ffn1_norm, dwg1, dwu1, dwd1 = _ffn_bwd(
        dx1, x, gains["ffn1_norm"], weights, saved1, tag="ffn1", ex=ex,
        rider=("mixer", dict(w_in=d_w_in, w_out=d_w_out)), spread=True)
    gain_grads = dict(ffn1_norm=d_ffn1_norm, mix_norm=d_mix_norm, sb_out_norm=d_sb_norm, dil_out_norm=d_dl_norm,
                      ffn2_norm=d_ffn2_norm, final_norm=d_final)
    weight_grads = dict(ffn1_w_gate=dwg1, ffn1_w_up=dwu1, ffn1_w_down=dwd1, w_in=d_w_in, w_out=d_w_out,
                        ffn2_w_gate=dwg2, ffn2_w_up=dwu2, ffn2_w_down=dwd2)
    return loss_row, dx, gain_grads, weight_grads


def _mesh_position():
    return lax.axis_index("x"), lax.axis_index("y"), lax.axis_index("c")


def _flip(coord, bit):
    return 1 - coord if bit else coord


RELATIONS = [(rx, ry, rc) for rx in (0, 1) for ry in (0, 1) for rc in (0, 1)][1:]


class _GatherPlan:
    def __init__(self, shards):
        n = len(shards)
        self.operands = list(shards)
        self.out_shapes = [jax.ShapeDtypeStruct((N_DEV,) + s.shape, s.dtype) for s in shards]
        self.scratch = [pltpu.SemaphoreType.DMA((n, 7)), pltpu.SemaphoreType.DMA((n, 7)),
                        pltpu.SemaphoreType.DMA((n,))]

    def _copies(self, in_refs, out_refs, sems):
        send_sems, recv_sems, local_sems = sems
        x, y, c = _mesh_position()
        me, sibling = (x, y, c), (x, y, 1 - c)
        chips = [(1 - x, y), (x, 1 - y), (1 - x, 1 - y)]
        plans = []
        for t, (x_ref, out_ref) in enumerate(zip(in_refs, out_refs)):
            def slot(px, py, pc, out_ref=out_ref):
                return out_ref.at[4 * px + 2 * py + pc]

            def copy(k, block, to, src=None, t=t, slot=slot):
                return pltpu.make_async_remote_copy(
                    src_ref=slot(*block) if src is None else src, dst_ref=slot(*block),
                    send_sem=send_sems.at[t, k], recv_sem=recv_sems.at[t, k],
                    device_id=to, device_id_type=pl.DeviceIdType.MESH)

            plans.append(dict(
                mine=pltpu.make_async_copy(x_ref, slot(*me), local_sems.at[t]),
                first=[copy(0, me, sibling, src=x_ref)]
                + [copy(1 + j, me, (*chip, c), src=x_ref) for j, chip in enumerate(chips)],
                over_ici=[copy(1 + j, (*chip, c), me) for j, chip in enumerate(chips)],
                passed=[copy(4 + j, (*chip, c), sibling) for j, chip in enumerate(chips)],
                from_sibling=[copy(0, sibling, me)] + [copy(4 + j, (*chip, 1 - c), me) for j, chip in enumerate(chips)]))
        return plans

    def start(self, in_refs, out_refs, sems):
        for p in self._copies(in_refs, out_refs, sems):
            p["mine"].start()
            for cp in p["first"]:
                cp.start()

    def finish(self, in_refs, out_refs, sems):
        plans = self._copies(in_refs, out_refs, sems)
        for p in plans:
            for arrived, onward in zip(p["over_ici"], p["passed"]):
                arrived.wait_recv()
                onward.start()
        for p in plans:
            for cp in p["from_sibling"]:
                cp.wait_recv()
            for cp in p["first"] + p["passed"]:
                cp.wait_send()
            p["mine"].wait()


class _Hosted:
    def __init__(self, plan, n_in, n_out, n_scratch):
        self.plan, self.n_in, self.n_out, self.n_scratch = plan, n_in, n_out, n_scratch
        self.operands = list(plan.operands) if plan else []
        self.out_shapes = list(plan.out_shapes) if plan else []
        self.scratch = list(plan.scratch) if plan else []
        self.in_specs = [pl.BlockSpec(memory_space=pl.ANY)] * len(self.operands)
        self.out_specs = [pl.BlockSpec(memory_space=pl.ANY)] * len(self.out_shapes)

    def semantics(self, sem):
        return sem if self.plan is None else ("arbitrary",) * len(sem)

    def _at(self, grid, last):
        hit = None
        for d, n in enumerate(grid):
            here = pl.program_id(d) == (n - 1 if last else 0)
            hit = here if hit is None else hit & here
        return hit

    def begin(self, refs, grid):
        if self.plan is None:
            return refs
        k_in, k_out = len(self.operands), len(self.out_shapes)
        ins, rest = refs[:self.n_in], refs[self.n_in:]
        c_in, rest = rest[:k_in], rest[k_in:]
        outs, rest = rest[:self.n_out], rest[self.n_out:]
        c_out, rest = rest[:k_out], rest[k_out:]
        scratch, sems = rest[:self.n_scratch], rest[self.n_scratch:]
        self._args = (c_in, c_out, sems)
        pl.when(self._at(grid, False))(lambda: self.plan.start(*self._args))
        return tuple(ins) + tuple(outs) + tuple(scratch)

    def end(self, grid):
        if self.plan is not None:
            pl.when(self._at(grid, True))(lambda: self.plan.finish(*self._args))


def _run_plan(plan, *, name):
    hosted = _Hosted(plan, 0, 0, 0)

    def body(*refs):
        hosted.begin(refs, (1,))
        hosted.end((1,))

    return pl.pallas_call(
        body, name=name, grid=(1,), out_shape=hosted.out_shapes,
        in_specs=hosted.in_specs, out_specs=hosted.out_specs, scratch_shapes=hosted.scratch,
        compiler_params=pltpu.CompilerParams(dimension_semantics=("arbitrary",)),
    )(*hosted.operands)


class _ExchangePlan:
    def __init__(self, packs):
        n = len(packs)
        self.operands = list(packs)
        self.out_shapes = [jax.ShapeDtypeStruct(p.shape, p.dtype) for p in packs]
        self.scratch = [pltpu.SemaphoreType.DMA((n, 7)), pltpu.SemaphoreType.DMA((n, 7)),
                        pltpu.SemaphoreType.DMA((n,))]

    def _copies(self, in_refs, out_refs, sems):
        send_sems, recv_sems, local_sems = sems
        x, y, c = _mesh_position()
        me = 4 * x + 2 * y + c
        copies = [pltpu.make_async_copy(i.at[me], o.at[me], local_sems.at[t])
                  for t, (i, o) in enumerate(zip(in_refs, out_refs))]
        for r, (rx, ry, rc) in enumerate(RELATIONS):
            px, py, pc = _flip(x, rx), _flip(y, ry), _flip(c, rc)
            peer = 4 * px + 2 * py + pc
            copies += [pltpu.make_async_remote_copy(
                src_ref=i.at[peer], dst_ref=o.at[me], send_sem=send_sems.at[t, r], recv_sem=recv_sems.at[t, r],
                device_id=(px, py, pc), device_id_type=pl.DeviceIdType.MESH)
                for t, (i, o) in enumerate(zip(in_refs, out_refs))]
        return copies

    def start(self, in_refs, out_refs, sems):
        for cp in self._copies(in_refs, out_refs, sems):
            cp.start()

    def finish(self, in_refs, out_refs, sems):
        for cp in self._copies(in_refs, out_refs, sems):
            cp.wait()


def _all_reduce_rows(v, *, name):
    R, C = v.shape

    def body(v_ref, out_ref, buf, send_sems, recv_sems):
        x, y, c = _mesh_position()
        me = 4 * x + 2 * y + c
        buf[me] = v_ref[...]
        copies = []
        for r, (rx, ry, rc) in enumerate(RELATIONS):
            cp = pltpu.make_async_remote_copy(
                src_ref=v_ref, dst_ref=buf.at[me], send_sem=send_sems.at[r], recv_sem=recv_sems.at[r],
                device_id=(_flip(x, rx), _flip(y, ry), _flip(c, rc)), device_id_type=pl.DeviceIdType.MESH)
            cp.start()
            copies.append(cp)
        for cp in copies:
            cp.wait()
        total = buf[0]
        for s in range(1, N_DEV):
            total = total + buf[s]
        out_ref[...] = total

    return pl.pallas_call(
        body, name=name,
        out_shape=jax.ShapeDtypeStruct((R, C), F32),
        in_specs=[pl.BlockSpec(memory_space=pltpu.VMEM)],
        out_specs=pl.BlockSpec(memory_space=pltpu.VMEM),
        scratch_shapes=[pltpu.VMEM((N_DEV, R, C), F32), pltpu.SemaphoreType.DMA((7,)), pltpu.SemaphoreType.DMA((7,))],
    )(v)


def _sum_slots(recv, *, name):
    _, R, C = recv.shape
    tr = _pick(R, (256, 208, 128, 64, 32, 16))

    def body(r_ref, o_ref):
        total = r_ref[0].astype(F32)
        for s in range(1, N_DEV):
            total = total + r_ref[s].astype(F32)
        o_ref[...] = total

    return pl.pallas_call(
        body, name=name, grid=(R // tr,),
        out_shape=jax.ShapeDtypeStruct((R, C), F32),
        in_specs=[pl.BlockSpec((N_DEV, tr, C), lambda i: (0, i, 0))],
        out_specs=pl.BlockSpec((tr, C), lambda i: (i, 0)),
        compiler_params=_cparams(("parallel",)),
    )(recv)


def _adamw(w, g, m, v, *, name):
    R, C = w.shape
    tr = _pick(R, (256, 128, 64, 32, 16, 8))

    def body(w_ref, g_ref, m_ref, v_ref, d_ref, nm_ref, nv_ref):
        g = g_ref[...]
        m_new = ADAM_B1 * m_ref[...] + (1.0 - ADAM_B1) * g
        v_new = ADAM_B2 * v_ref[...] + (1.0 - ADAM_B2) * (g * g)
        m_hat = m_new / (1.0 - ADAM_B1 ** ADAM_STEP)
        v_hat = v_new / (1.0 - ADAM_B2 ** ADAM_STEP)
        d_ref[...] = -ADAM_LR * (m_hat / (jnp.sqrt(v_hat) + ADAM_EPS) + ADAM_WD * w_ref[...])
        nm_ref[...] = m_new
        nv_ref[...] = v_new

    spec = pl.BlockSpec((tr, C), lambda i: (i, 0))
    return pl.pallas_call(
        body, name=name, grid=(R // tr,),
        out_shape=[jax.ShapeDtypeStruct((R, C), F32)] * 3,
        in_specs=[spec] * 4, out_specs=[spec] * 3,
        compiler_params=_cparams(("parallel",)),
    )(w, g, m, v)


WEIGHT_NAMES = ["ffn1_norm", "ffn1_w_gate", "ffn1_w_up", "ffn1_w_down", "mix_norm", "w_in", "sb_out_norm",
                "dil_out_norm", "w_out", "ffn2_norm", "ffn2_w_gate", "ffn2_w_up", "ffn2_w_down", "final_norm"]
GAIN_NAMES = ["ffn1_norm", "mix_norm", "sb_out_norm", "dil_out_norm", "ffn2_norm", "final_norm"]
COL_SHARDED = ["ffn1_w_gate", "ffn1_w_up", "ffn2_w_gate", "ffn2_w_up", "w_in"]
ROW_SHARDED = ["ffn1_w_down", "ffn2_w_down", "w_out"]
GROUPS = {"mixer": (["w_in"], ["w_out"]),
          "ffn2": (["ffn2_w_gate", "ffn2_w_up"], ["ffn2_w_down"])}
for _ffn in ("ffn1", "ffn2"):
    GROUPS.update({f"{_ffn}_w_gate": ([f"{_ffn}_w_gate"], []), f"{_ffn}_w_up": ([f"{_ffn}_w_up"], []),
                   f"{_ffn}_w_down": ([], [f"{_ffn}_w_down"])})


class _Exchanges:
    def __init__(self, params):
        self.params = params
        self.grads = {}

    def gather(self, group):
        if group is None:
            return None
        cols, rows = GROUPS[group]
        packs = []
        if cols:
            packs.append(jnp.concatenate([self.params[n] for n in cols], axis=1).astype(BF16))
        if rows:
            packs.append(jnp.concatenate([self.params[n] for n in rows], axis=0).astype(BF16))
        return _GatherPlan(packs)

    def gathered(self, group, got, weights):
        if group is None:
            return
        cols, rows = GROUPS[group]
        got = list(got)
        if cols:
            col_all, off = got.pop(0), 0
            for n in cols:
                w = self.params[n].shape[1]
                piece = col_all[:, :, off:off + w]
                weights[n] = jnp.transpose(piece, (1, 0, 2)).reshape(piece.shape[1], N_DEV * w)
                off += w
        if rows:
            row_all, off = got.pop(0), 0
            for n in rows:
                r = self.params[n].shape[0]
                weights[n] = row_all[:, off:off + r, :].reshape(N_DEV * r, row_all.shape[2])
                off += r

    def send(self, group, grads):
        if group is None:
            return None
        cols, rows = GROUPS[group]
        packs = []
        if cols:
            chunks = [jnp.transpose(grads[n].reshape(grads[n].shape[0], N_DEV, self.params[n].shape[1]), (1, 0, 2))
                      for n in cols]
            packs.append(jnp.concatenate(chunks, axis=2).astype(BF16))
        if rows:
            chunks = [grads[n].reshape(N_DEV, self.params[n].shape[0], grads[n].shape[1]) for n in rows]
            packs.append(jnp.concatenate(chunks, axis=1).astype(BF16))
        return _ExchangePlan(packs)

    def received(self, group, got):
        if group is None:
            return
        cols, rows = GROUPS[group]
        got = list(got)
        if cols:
            col_grad, off = _sum_slots(got.pop(0), name=f"sum_col_grads_{group}"), 0
            for n in cols:
                w = self.params[n].shape[1]
                self.grads[n] = col_grad[:, off:off + w]
                off += w
        if rows:
            row_grad, off = _sum_slots(got.pop(0), name=f"sum_row_grads_{group}"), 0
            for n in rows:
                r = self.params[n].shape[0]
                self.grads[n] = row_grad[off:off + r, :]
                off += r


def _step(x, target, params, moments_m, moments_v):
    ex = _Exchanges(params)
    weights = {}
    ex.gathered("ffn1_w_gate", _run_plan(ex.gather("ffn1_w_gate"), name="gather_ffn1_gate"), weights)
    gains = {n: params[n] for n in GAIN_NAMES}
    loss_row, grad_x, gain_grads, _ = _local_step(x, target, gains, weights, ex)
    grads = ex.grads

    rows = [gain_grads[n].reshape(-1, LANES) for n in GAIN_NAMES] + [loss_row]
    small = jnp.concatenate(rows, axis=0)
    pad = (-small.shape[0]) % 8
    small = jnp.pad(small, ((0, pad), (0, 0)))
    small = _all_reduce_rows(small, name="reduce_gains_loss")
    off = 0
    for n in GAIN_NAMES:
        r = gain_grads[n].shape[1] // LANES
        grads[n] = small[off:off + r].reshape(1, -1)
        off += r
    loss = small[off, 0]

    delta, new_m, new_v = {}, {}, {}
    for n in WEIGHT_NAMES:
        delta[n], new_m[n], new_v[n] = _adamw(params[n], grads[n], moments_m[n], moments_v[n], name=f"adamw_{n}")
    return loss, grad_x, grads, delta, new_m, new_v


def kernel(x, ffn1_norm, ffn1_w_gate, ffn1_w_up, ffn1_w_down, mix_norm, w_in, sb_out_norm, dil_out_norm, w_out, ffn2_norm, ffn2_w_gate, ffn2_w_up, ffn2_w_down, final_norm, loss_target, m_ffn1_norm, m_ffn1_w_gate, m_ffn1_w_up, m_ffn1_w_down, m_mix_norm, m_w_in, m_sb_out_norm, m_dil_out_norm, m_w_out, m_ffn2_norm, m_ffn2_w_gate, m_ffn2_w_up, m_ffn2_w_down, m_final_norm, v_ffn1_norm, v_ffn1_w_gate, v_ffn1_w_up, v_ffn1_w_down, v_mix_norm, v_w_in, v_sb_out_norm, v_dil_out_norm, v_w_out, v_ffn2_norm, v_ffn2_w_gate, v_ffn2_w_up, v_ffn2_w_down, v_final_norm):
    given = dict(locals())
    shapes = {n: given[n].shape for n in WEIGHT_NAMES}

    def as2d(a):
        return a.reshape(1, -1) if a.ndim == 1 else a.reshape(a.shape[-2], a.shape[-1])

    params = {n: as2d(given[n]) for n in WEIGHT_NAMES}
    moments_m = {n: as2d(given["m_" + n]) for n in WEIGHT_NAMES}
    moments_v = {n: as2d(given["v_" + n]) for n in WEIGHT_NAMES}
    loss, grad_x, grads, delta, new_m, new_v = _step(x[0], loss_target[0], params, moments_m, moments_v)
    back = lambda d: [d[n].reshape(shapes[n]) for n in WEIGHT_NAMES]
    return (loss, grad_x[None], *back(grads), *back(delta), *back(new_m), *back(new_v))
```

```python
import functools

import jax
import jax.numpy as jnp
from jax import lax
from jax.experimental import pallas as pl
from jax.experimental.pallas import tpu as pltpu

F32 = jnp.float32
BF16 = jnp.bfloat16

N_DEV = 8
HEAD_DIM = 64
LANES = 128
DILATED_PATTERNS = ((128, 1), (512, 4), (2048, 16))
DIL_BLOCK = 128
DIL_SUPER = 2048
DIL_UNROLL = 8
SB_TILE = 256
SB_UNROLL = 4
SB_DEAD = 90.0
SB_UNSEEN = -1e30
ROPE_THETA = 10000.0
RMS_EPS = 1e-6
HALF_STEP = 0.5
ADAM_LR = 0.001
ADAM_B1 = 0.9
ADAM_B2 = 0.999
ADAM_EPS = 1e-08
ADAM_WD = 0.01
ADAM_STEP = 10
NEG_BIG = -1e30
VMEM_CAP_MB = 60


def _pick(n, prefs):
    for p in prefs:
        if n % p == 0:
            return p
    return n


MM_MAX_TILE = 1536


def _largest_tile(n, cap):
    if n <= cap:
        return n
    for t in range(cap - cap % LANES, 0, -LANES):
        if n % t == 0:
            return t
    return n


def _cparams(sem=None, vmem_mb=48):
    return pltpu.CompilerParams(dimension_semantics=sem, vmem_limit_bytes=min(vmem_mb, VMEM_CAP_MB) * 1024 * 1024)


def _nbytes(shape, dtype):
    n = 1
    for s in shape:
        n *= s
    return n * jnp.dtype(dtype).itemsize


def _mm(a, b, *, name, ta=False, tb=False, outs=(F32,), res=None, alpha=1.0, extras=(), epilogue=None,
        tm=None, tn=None, tk=None, comm=None):
    if ta:
        K, M = a.shape
    else:
        M, K = a.shape
    if tb:
        N, Kb = b.shape
    else:
        Kb, N = b.shape
    assert K == Kb, (a.shape, b.shape, ta, tb)
    tm = tm or (_largest_tile(M, MM_MAX_TILE) if ta else _pick(M, (512, 256, 128)))
    tn = tn or _largest_tile(N, MM_MAX_TILE)
    tk = tk or (K if K <= 3072 else _pick(K, (1024, 512, 256, 128)))
    nk = K // tk
    a_spec = pl.BlockSpec((tk, tm), lambda i, j, k: (k, i)) if ta else pl.BlockSpec((tm, tk), lambda i, j, k: (i, k))
    b_spec = pl.BlockSpec((tn, tk), lambda i, j, k: (j, k)) if tb else pl.BlockSpec((tk, tn), lambda i, j, k: (k, j))
    mn_spec = pl.BlockSpec((tm, tn), lambda i, j, k: (i, j))
    dims = (((0 if ta else 1,), (1 if tb else 0,)), ((), ()))
    n_extra = len(extras) + (1 if res is not None else 0)
    n_out = len(outs)
    grid = (M // tm, N // tn, nk)
    hosted = _Hosted(comm, n_in=2 + n_extra, n_out=n_out, n_scratch=1 if nk > 1 else 0)

    def body(*refs):
        a_ref, b_ref = refs[0], refs[1]
        in_refs = refs[2:2 + n_extra]
        refs = hosted.begin(refs, grid)
        out_refs = refs[2 + n_extra:2 + n_extra + n_out]
        prod = lax.dot_general(a_ref[...].astype(BF16), b_ref[...].astype(BF16), dims, preferred_element_type=F32)

        def finish(acc):
            blocks = [r[...] for r in in_refs]
            if res is not None:
                r_blk, blocks = blocks[0], blocks[1:]
            else:
                r_blk = None
            if epilogue is None:
                val = acc * alpha
                if r_blk is not None:
                    val = val + r_blk
                vals = (val,)
            else:
                vals = epilogue(acc, r_blk, *blocks)
            for o_ref, v in zip(out_refs, vals):
                o_ref[...] = v.astype(o_ref.dtype)

        if nk == 1:
            finish(prod)
        else:
            acc_ref = refs[2 + n_extra + n_out]
            k = pl.program_id(2)

            @pl.when(k == 0)
            def _():
                acc_ref[...] = prod

            @pl.when(k > 0)
            def _():
                acc_ref[...] += prod

            @pl.when(k == nk - 1)
            def _():
                finish(acc_ref[...])

        hosted.end(grid)

    operands = [a, b] + ([res] if res is not None else []) + list(extras)
    in_specs = [a_spec, b_spec] + [mn_spec] * n_extra
    est = 2 * (_nbytes((tm, tk), a.dtype) + _nbytes((tk, tn), b.dtype))
    est += 2 * sum(_nbytes((tm, tn), o.dtype) for o in operands[2:])
    est += 2 * sum(_nbytes((tm, tn), d) for d in outs) + _nbytes((tm, tn), F32)
    result = pl.pallas_call(
        body, name=name, grid=grid,
        out_shape=[jax.ShapeDtypeStruct((M, N), d) for d in outs] + hosted.out_shapes,
        in_specs=in_specs + hosted.in_specs, out_specs=[mn_spec] * n_out + hosted.out_specs,
        scratch_shapes=([pltpu.VMEM((tm, tn), F32)] if nk > 1 else []) + hosted.scratch,
        compiler_params=_cparams(hosted.semantics(("parallel", "parallel", "arbitrary")),
                                 vmem_mb=max(32, 2 * est // (1024 * 1024))),
    )(*operands, *hosted.operands)
    own, got = result[:n_out], list(result[n_out:])
    own = own[0] if n_out == 1 else own
    return own if comm is None else (own, got)


def _rms_hat(x):
    r = lax.rsqrt(jnp.mean(x * x, axis=-1, keepdims=True) + RMS_EPS)
    return x * r, r


def _rms_fwd(xs, gains, *, name):
    S = xs[0].shape[0]
    widths = [x.shape[1] for x in xs]
    tm = _pick(S, (512, 256, 128))
    n = len(xs)

    def body(*refs):
        o_ref = refs[2 * n]
        off = 0
        for i in range(n):
            xh, _ = _rms_hat(refs[i][...])
            o_ref[:, off:off + widths[i]] = (xh * refs[n + i][...]).astype(o_ref.dtype)
            off += widths[i]

    return pl.pallas_call(
        body, name=name, grid=(S // tm,),
        out_shape=jax.ShapeDtypeStruct((S, sum(widths)), BF16),
        in_specs=[pl.BlockSpec((tm, w), lambda i: (i, 0)) for w in widths]
        + [pl.BlockSpec((1, w), lambda i: (0, 0)) for w in widths],
        out_specs=pl.BlockSpec((tm, sum(widths)), lambda i: (i, 0)),
        compiler_params=_cparams(("parallel",)),
    )(*xs, *gains)


def _rms_bwd(dh, xs, gains, res, *, name):
    S = xs[0].shape[0]
    widths = [x.shape[1] for x in xs]
    tm = _pick(S, (512, 256, 128))
    n = len(xs)
    has_res = res is not None

    def body(*refs):
        dh_ref = refs[0]
        x_refs = refs[1:1 + n]
        g_refs = refs[1 + n:1 + 2 * n]
        r_ref = refs[1 + 2 * n] if has_res else None
        base = 1 + 2 * n + (1 if has_res else 0)
        dx_refs = refs[base:base + n]
        dg_refs = refs[base + n:base + 2 * n]
        first = pl.program_id(0) == 0
        off = 0
        for i in range(n):
            x = x_refs[i][...]
            xh, r = _rms_hat(x)
            d = dh_ref[:, off:off + widths[i]]
            dxh = d * g_refs[i][...]
            dx = r * (dxh - xh * jnp.mean(dxh * xh, axis=-1, keepdims=True))
            if has_res:
                dx = dx + r_ref[...]
            dx_refs[i][...] = dx
            part = jnp.sum(d * xh, axis=0, keepdims=True)

            @pl.when(first)
            def _(i=i, part=part):
                dg_refs[i][...] = part

            @pl.when(jnp.logical_not(first))
            def _(i=i, part=part):
                dg_refs[i][...] += part

            off += widths[i]

    in_specs = [pl.BlockSpec((tm, sum(widths)), lambda i: (i, 0))]
    in_specs += [pl.BlockSpec((tm, w), lambda i: (i, 0)) for w in widths]
    in_specs += [pl.BlockSpec((1, w), lambda i: (0, 0)) for w in widths]
    operands = [dh, *xs, *gains]
    if has_res:
        in_specs.append(pl.BlockSpec((tm, widths[0]), lambda i: (i, 0)))
        operands.append(res)
    out = pl.pallas_call(
        body, name=name, grid=(S // tm,),
        out_shape=[jax.ShapeDtypeStruct((S, w), F32) for w in widths] + [jax.ShapeDtypeStruct((1, w), F32) for w in widths],
        in_specs=in_specs,
        out_specs=[pl.BlockSpec((tm, w), lambda i: (i, 0)) for w in widths]
        + [pl.BlockSpec((1, w), lambda i: (0, 0)) for w in widths],
        compiler_params=_cparams(("arbitrary",)),
    )(*operands)
    return out[:n], out[n:]


def _sigmoid(g):
    return 1.0 / (1.0 + jnp.exp(-g))


def _ride(result, plan):
    return result if plan is not None else (result, None)


def _ffn_fwd(x, gain, w, *, tag, ex, riders=(None, None, None)):
    h = _rms_fwd([x], [gain], name=f"{tag}_norm")
    plan = ex.gather(riders[0])
    g, got = _ride(_mm(h, w[f"{tag}_w_gate"], outs=(BF16,), name=f"{tag}_gate", comm=plan), plan)
    ex.gathered(riders[0], got, w)

    def act(acc, _, g_blk):
        gf = g_blk.astype(F32)
        return acc, gf * _sigmoid(gf) * acc

    plan = ex.gather(riders[1])
    (u, a), got = _ride(_mm(h, w[f"{tag}_w_up"], outs=(BF16, BF16), extras=(g,), epilogue=act, name=f"{tag}_up_act",
                            comm=plan), plan)
    ex.gathered(riders[1], got, w)
    plan = ex.gather(riders[2])
    y, got = _ride(_mm(a, w[f"{tag}_w_down"], res=x, alpha=HALF_STEP, name=f"{tag}_down", comm=plan), plan)
    ex.gathered(riders[2], got, w)
    return y, (h, g, u, a)


def _ffn_bwd(dout, x, gain, w, saved, *, tag, ex, rider=(None, None), spread=False):
    h, g, u, a = saved
    wg, wu, wd = (w[f"{tag}_w_{n}"] for n in ("gate", "up", "down"))

    def act_bwd(acc, _, g_blk, u_blk):
        gf, uf = g_blk.astype(F32), u_blk.astype(F32)
        da = acc * HALF_STEP
        sig = _sigmoid(gf)
        silu = gf * sig
        return da * uf * (sig * (1.0 + gf * (1.0 - sig))), da * silu

    def carrying(group, grad, call):
        group = group if spread else None
        plan = ex.send(group, {group: grad})
        out, got = _ride(call(plan), plan)
        ex.received(group, got)
        return out

    plan = ex.send(*rider)
    (dg, du), got = _ride(_mm(dout, wd, tb=True, outs=(BF16, BF16), extras=(g, u), epilogue=act_bwd,
                              name=f"{tag}_bwd_act", comm=plan), plan)
    ex.received(rider[0], got)
    dwg = _mm(h, dg, ta=True, name=f"{tag}_dwg")
    dwu = carrying(f"{tag}_w_gate", dwg, lambda plan: _mm(h, du, ta=True, name=f"{tag}_dwu", comm=plan))
    dwd = carrying(f"{tag}_w_up", dwu,
                   lambda plan: _mm(a, dout, ta=True, alpha=HALF_STEP, name=f"{tag}_dwd", comm=plan))
    dh = carrying(f"{tag}_w_down", dwd, lambda plan: _mm(dg, wg, tb=True, name=f"{tag}_dh_gate", comm=plan))
    dh = _mm(du, wu, tb=True, res=dh, name=f"{tag}_dh_up")
    (dx,), (dgain,) = _rms_bwd(dh, [x], [gain], dout, name=f"{tag}_norm_bwd")
    return dx, dgain, dwg, dwu, dwd


def _rope_tables(S):
    half = HEAD_DIM // 2
    inv_freq = ROPE_THETA ** (-jnp.arange(half, dtype=F32) / half)
    ang = jnp.arange(S, dtype=F32)[:, None] * inv_freq[None, :]
    cos, sin = jnp.cos(ang), jnp.sin(ang)
    reps = LANES // HEAD_DIM
    cos_t = jnp.tile(jnp.concatenate([cos, cos], axis=1), (1, reps))
    sin_t = jnp.tile(jnp.concatenate([-sin, sin], axis=1), (1, reps))
    return cos_t, sin_t


def _rotary(t, cos_t, sin_t, *, col0, width, sign, name):
    S = t.shape[0]
    tm = _pick(S, (512, 256, 128))
    half = HEAD_DIM // 2
    c0 = col0 // LANES

    def body(t_ref, c_ref, s_ref, o_ref):
        v = t_ref[...]
        lane = lax.broadcasted_iota(jnp.int32, v.shape, 1)
        swapped = jnp.where(lane % HEAD_DIM < half, pltpu.roll(v, LANES - half, axis=1), pltpu.roll(v, half, axis=1))
        o_ref[...] = v * c_ref[...] + swapped * (s_ref[...] * sign)

    return pl.pallas_call(
        body, name=name, grid=(S // tm, width // LANES),
        out_shape=jax.ShapeDtypeStruct((S, width), F32),
        in_specs=[pl.BlockSpec((tm, LANES), lambda i, j: (i, c0 + j)),
                  pl.BlockSpec((tm, LANES), lambda i, j: (i, 0)),
                  pl.BlockSpec((tm, LANES), lambda i, j: (i, 0))],
        out_specs=pl.BlockSpec((tm, LANES), lambda i, j: (i, j)),
        compiler_params=_cparams(("parallel", "parallel")),
    )(t, cos_t, sin_t)


def _head_masks(shape):
    lane = lax.broadcasted_iota(jnp.int32, shape, 1)
    return [(lane >= HEAD_DIM * h) & (lane < HEAD_DIM * (h + 1)) for h in range(LANES // HEAD_DIM)]


def _sb_scores(q2, k_j):
    z = lax.dot_general(q2, k_j, (((1,), (1,)), ((), ())), preferred_element_type=F32)
    sign_bit = jnp.int32(-2 ** 31)
    minus_abs = lax.bitcast_convert_type(lax.bitcast_convert_type(z, jnp.int32) | sign_bit, F32)
    softplus = jnp.maximum(z, 0.0) + jnp.log(1.0 + jnp.exp(minus_abs))
    return z - softplus, softplus


def _sb_stack_heads(t, scale=None):
    parts = [jnp.where(hm, t, jnp.zeros_like(t)) for hm in _head_masks(t.shape)]
    t2 = jnp.concatenate(parts, axis=0)
    if scale is not None:
        t2 = (t2.astype(F32) * scale).astype(t2.dtype)
    return t2


def _sb_unstack_heads(t2):
    T = t2.shape[0] // 2
    masks = _head_masks((T, LANES))
    return jnp.where(masks[0], t2[:T], t2[T:])


def _sb_causal(T):
    row = lax.broadcasted_iota(jnp.int32, (2 * T, T), 0)
    col = lax.broadcasted_iota(jnp.int32, (2 * T, T), 1)
    return col < jnp.where(row >= T, row - T, row)


def _sb_triangle(T, later):
    row = lax.broadcasted_iota(jnp.int32, (T, T), 0)
    col = lax.broadcasted_iota(jnp.int32, (T, T), 1)
    return ((row > col) if later else (row < col)).astype(BF16)


def _sb_fwd(p_sb, *, name, comm=None):
    S = p_sb.shape[0]
    W = p_sb.shape[1] // 3
    npair = W // LANES
    T = SB_TILE
    n_tiles = S // T
    assert n_tiles <= HEAD_DIM
    scale = HEAD_DIM ** -0.5

    grid = (npair, n_tiles)
    hosted = _Hosted(comm, n_in=3, n_out=2, n_scratch=0)

    def body(*refs):
        q_ref, k_ref, v_ref, o_ref, c_ref = hosted.begin(refs, grid)
        I = pl.program_id(1)
        lane = lax.broadcasted_iota(jnp.int32, (T, LANES), 1)
        causal = _sb_causal(T)
        later_than = _sb_triangle(T, True)
        q2 = _sb_stack_heads(q_ref[...], scale)

        def scores(J, diag):
            off = pl.multiple_of(J * T, T)
            log_beta, stay = _sb_scores(q2, k_ref[pl.ds(off, T), :])
            if diag:
                stay = jnp.where(causal, stay, 0.0)
            local = jnp.dot(stay.astype(BF16), later_than, preferred_element_type=F32)
            return log_beta, local, jnp.sum(stay, axis=1, keepdims=True), v_ref[pl.ds(off, T), :]

        def weigh(J, sc, gone, acc, carr, diag):
            log_beta, local, _, v_j = sc
            w = jnp.exp((log_beta - gone) - local)
            if diag:
                w = jnp.where(causal, w, 0.0)
            acc = acc + jnp.dot(w.astype(BF16), v_j, preferred_element_type=F32)
            carr = jnp.where(lane == J, -gone[:T], carr)
            carr = jnp.where(lane == HEAD_DIM + J, -gone[T:], carr)
            return acc, carr

        def tiles(J, count, state, diag):
            gone, acc, carr, _ = state
            scs = [scores(J - u, diag and u == 0) for u in range(count)]
            for u, sc in enumerate(scs):
                acc, carr = weigh(J - u, sc, gone, acc, carr, diag and u == 0)
                gone = gone + sc[2]
            return gone, acc, carr, jnp.min(gone)

        U = SB_UNROLL
        alive = lambda st: st[3] < SB_DEAD
        state = (jnp.zeros((2 * T, 1), F32), jnp.zeros((2 * T, LANES), F32),
                 jnp.full((T, LANES), SB_UNSEEN, F32), jnp.zeros((), F32))
        state = lax.cond(I > 0, lambda st: tiles(I, 2, st, True), lambda st: tiles(I, 1, st, True), state)
        rest = jnp.maximum(I - 1, 0)
        singles = jnp.where(rest > 0, (rest - 1) % U + 1, 0)
        _, state = lax.while_loop(lambda c: (c[0] < singles) & alive(c[1]),
                                  lambda c: (c[0] + 1, tiles(I - 2 - c[0], 1, c[1], False)), (jnp.int32(0), state))
        blocks = (rest - singles) // U
        _, state = lax.while_loop(lambda c: (c[0] < blocks) & alive(c[1]),
                                  lambda c: (c[0] + 1, tiles(I - 2 - singles - U * c[0], U, c[1], False)),
                                  (jnp.int32(0), state))
        _, acc, carr, _ = state
        o_ref[...] = _sb_unstack_heads(acc)
        c_ref[...] = carr
        hosted.end(grid)

    blk = lambda I_off: pl.BlockSpec((T, LANES), lambda p, I: (I, I_off + p))
    full = lambda off: pl.BlockSpec((S, LANES), lambda p, I: (0, off + p))
    o, carries, *got = pl.pallas_call(
        body, name=name, grid=grid,
        out_shape=[jax.ShapeDtypeStruct((S, W), F32), jax.ShapeDtypeStruct((S, W), F32)] + hosted.out_shapes,
        in_specs=[blk(0), full(npair), full(2 * npair)] + hosted.in_specs,
        out_specs=[blk(0), blk(0)] + hosted.out_specs,
        scratch_shapes=hosted.scratch,
        compiler_params=_cparams(hosted.semantics(("parallel", "arbitrary"))),
    )(p_sb, p_sb, p_sb, *hosted.operands)
    return (o, carries) if comm is None else (o, carries, got)


def _sb_bwd(p_sb, do, carries, *, name, comm=None):
    S = p_sb.shape[0]
    W = p_sb.shape[1] // 3
    npair = W // LANES
    T = SB_TILE
    n_tiles = S // T
    scale = HEAD_DIM ** -0.5

    grid = (npair, n_tiles)
    hosted = _Hosted(comm, n_in=5, n_out=3, n_scratch=0)

    def body(*refs):
        q_ref, k_ref, v_ref, do_ref, c_ref, dq_ref, dk_ref, dv_ref = hosted.begin(refs, grid)
        I = pl.program_id(1)

        @pl.when(I == 0)
        def _():
            dk_ref[...] = jnp.zeros_like(dk_ref)
            dv_ref[...] = jnp.zeros_like(dv_ref)

        lane = lax.broadcasted_iota(jnp.int32, (T, LANES), 1)
        causal = _sb_causal(T)
        later_than = _sb_triangle(T, True)
        earlier_than = _sb_triangle(T, False)
        q2 = _sb_stack_heads(q_ref[...], scale)
        do2 = _sb_stack_heads(do_ref[...].astype(BF16))
        carr = c_ref[...]
        tn_dims = (((0,), (0,)), ((), ()))

        def chain(J, diag):
            off = pl.multiple_of(J * T, T)
            k_j = k_ref[pl.ds(off, T), :]
            v_j = v_ref[pl.ds(off, T), :]
            log_beta, stay = _sb_scores(q2, k_j)
            if diag:
                stay = jnp.where(causal, stay, 0.0)
            lc = jnp.concatenate(
                [jnp.sum(jnp.where(lane == HEAD_DIM * h + J, carr, 0.0), axis=1, keepdims=True) for h in range(2)],
                axis=0)
            w = jnp.exp((log_beta + lc) - jnp.dot(stay.astype(BF16), later_than, preferred_element_type=F32))
            if diag:
                w = jnp.where(causal, w, 0.0)
            dw = lax.dot_general(do2, v_j, (((1,), (1,)), ((), ())), preferred_element_type=F32)
            e = w * dw
            local = jnp.dot(e.astype(BF16), earlier_than, preferred_element_type=F32)
            return off, k_j, w, e, local, jnp.exp(log_beta), jnp.sum(e, axis=1, keepdims=True)

        def finish(ch, ec, dq_acc, diag):
            off, k_j, w, e, local, beta, _ = ch
            e_before = local + ec
            dz = e - beta * (e + e_before)
            if diag:
                dz = jnp.where(causal, dz, 0.0)
            dzb = dz.astype(BF16)
            dq_acc = dq_acc + jnp.dot(dzb, k_j, preferred_element_type=F32)
            dk_ref[pl.ds(off, T), :] += lax.dot_general(dzb, q2, tn_dims, preferred_element_type=F32)
            dv_ref[pl.ds(off, T), :] += lax.dot_general(w.astype(BF16), do2, tn_dims, preferred_element_type=F32)
            return dq_acc

        def tiles(J, count, state, diag):
            ec, dq_acc = state
            chains = [chain(J + u, diag and u == count - 1) for u in range(count)]
            for u, ch in enumerate(chains):
                dq_acc = finish(ch, ec, dq_acc, diag and u == count - 1)
                ec = ec + ch[6]
            return ec, dq_acc

        lane_row = lax.broadcasted_iota(jnp.int32, (1, LANES), 1)
        reached = (jnp.max(carr, axis=0, keepdims=True) > 0.5 * SB_UNSEEN) & (lane_row < HEAD_DIM)
        first = jnp.min(jnp.where(reached, lane_row.astype(F32), float(n_tiles))).astype(jnp.int32)
        U = SB_UNROLL
        count = I - first
        rest = jnp.maximum(count - 1, 0)
        state = (jnp.zeros((2 * T, 1), F32), jnp.zeros((2 * T, LANES), F32))
        state = lax.fori_loop(0, rest // U, lambda jj, st: tiles(first + U * jj, U, st, False), state)
        state = lax.fori_loop(0, rest % U, lambda r, st: tiles(I - 1 - rest % U + r, 1, st, False), state)
        _, dq_acc = lax.cond(count > 0, lambda st: tiles(I - 1, 2, st, True), lambda st: tiles(I, 1, st, True), state)
        dq_ref[...] = _sb_unstack_heads(dq_acc) * scale
        hosted.end(grid)

    blk = lambda src_off: pl.BlockSpec((T, LANES), lambda p, I: (I, src_off + p))
    full = lambda off: pl.BlockSpec((S, LANES), lambda p, I: (0, off + p))
    dq, dk, dv, *got = pl.pallas_call(
        body, name=name, grid=grid,
        out_shape=[jax.ShapeDtypeStruct((S, W), F32)] * 3 + hosted.out_shapes,
        in_specs=[blk(0), full(npair), full(2 * npair), blk(0), blk(0)] + hosted.in_specs,
        out_specs=[blk(0), full(0), full(0)] + hosted.out_specs,
        scratch_shapes=hosted.scratch,
        compiler_params=_cparams(hosted.semantics(("parallel", "arbitrary"))),
    )(p_sb, p_sb, p_sb, do, carries, *hosted.operands)
    return (dq, dk, dv) if comm is None else (dq, dk, dv, got)


def _dil_blocks(b, body_fn):
    for pi, (window, dil) in enumerate(DILATED_PATTERNS):
        assert window // dil == DIL_BLOCK
        nblk = DIL_SUPER // (DIL_BLOCK * dil)
        assert (dil * nblk) % DIL_UNROLL == 0

        def group(g, _, pi=pi, dil=dil, nblk=nblk):
            for u in range(DIL_UNROLL):
                t = g * DIL_UNROLL + u
                n = t % nblk
                body_fn(pi, dil, t // nblk, n, b * nblk + n)
            return 0

        lax.fori_loop(0, dil * nblk // DIL_UNROLL, group, 0)


def _dil_rows(start, size, dil):
    if dil == 1:
        return pl.ds(pl.multiple_of(start, DIL_BLOCK), size)
    return pl.ds(start, size, stride=dil)


def _dil_fill_bias(bias_ref):
    row = lax.broadcasted_iota(jnp.int32, (2 * DIL_BLOCK, 2 * DIL_BLOCK), 0)
    kk = lax.broadcasted_iota(jnp.int32, (2 * DIL_BLOCK, 2 * DIL_BLOCK), 1)
    qi = jnp.where(row >= DIL_BLOCK, row - DIL_BLOCK, row)
    for s in range(2):
        dist = s * DIL_BLOCK + qi - kk
        bias_ref[s] = jnp.where((dist >= 0) & (dist <= DIL_BLOCK), 0.0, NEG_BIG)


def _dl_fwd(q, k, v, *, name):
    S, W = q.shape
    npair = W // LANES
    nsuper = S // DIL_SUPER
    assert S % DIL_SUPER == 0 and S // max(d for _, d in DILATED_PATTERNS) >= 2 * DIL_BLOCK
    scale = HEAD_DIM ** -0.5
    npat = len(DILATED_PATTERNS)

    def body(q_ref, k_ref, v_ref, o_ref, l_ref, bias_ref, *pattern_refs):
        op_refs, lp_refs = pattern_refs[:npat], pattern_refs[npat:]
        b = pl.program_id(1)
        masks = _head_masks((DIL_BLOCK, LANES))
        pl.when(b == 0)(lambda: _dil_fill_bias(bias_ref))

        def block(pi, dil, c, n, gn):
            ws = jnp.maximum(gn - 1, 0)
            qrows = n * (DIL_BLOCK * dil) + c
            krows = ws * (DIL_BLOCK * dil) + c
            q_idx = _dil_rows(qrows, DIL_BLOCK, dil)
            k_idx = _dil_rows(krows, 2 * DIL_BLOCK, dil)
            qb = q_ref[q_idx, :]
            kb = k_ref[k_idx, :].astype(BF16)
            vb = v_ref[k_idx, :].astype(BF16)
            q2 = _sb_stack_heads(qb.astype(BF16), scale)
            z = lax.dot_general(q2, kb, (((1,), (1,)), ((), ())), preferred_element_type=F32) + bias_ref[gn - ws]
            m = jnp.max(z, axis=1, keepdims=True)
            p = jnp.exp(z - m)
            den = jnp.sum(p, axis=1, keepdims=True)
            acc = jnp.dot(p.astype(BF16), vb, preferred_element_type=F32)
            lse = m + jnp.log(den)
            op_refs[pi][q_idx, :] = _sb_unstack_heads(acc / den)
            lp_refs[pi][q_idx, :] = jnp.where(masks[0], lse[:DIL_BLOCK], lse[DIL_BLOCK:])

        _dil_blocks(b, block)
        lses = [r[...] for r in lp_refs]
        top = functools.reduce(jnp.maximum, lses)
        ws_ = [jnp.exp(l - top) for l in lses]
        den = functools.reduce(jnp.add, ws_)
        num = functools.reduce(jnp.add, [w * r[...] for r, w in zip(op_refs, ws_)])
        o_ref[...] = num / den
        l_ref[...] = top + jnp.log(den)

    blk = pl.BlockSpec((DIL_SUPER, LANES), lambda p, b: (b, p))
    full = pl.BlockSpec((S, LANES), lambda p, b: (0, p))
    return pl.pallas_call(
        body, name=name, grid=(npair, nsuper),
        out_shape=[jax.ShapeDtypeStruct((S, W), F32)] * 2,
        in_specs=[blk, full, full], out_specs=[blk, blk],
        scratch_shapes=[pltpu.VMEM((2, 2 * DIL_BLOCK, 2 * DIL_BLOCK), F32)]
        + [pltpu.VMEM((DIL_SUPER, LANES), F32)] * (2 * npat),
        compiler_params=_cparams(("arbitrary", "arbitrary")),
    )(q, k, v)


def _dl_bwd(q, k, v, o, lse, do, *, name):
    S, W = q.shape
    npair = W // LANES
    nsuper = S // DIL_SUPER
    scale = HEAD_DIM ** -0.5

    def body(q_ref, k_ref, v_ref, o_ref, l_ref, do_ref, dq_ref, dk_ref, dv_ref, delta_ref, bias_ref):
        b = pl.program_id(1)

        @pl.when(b == 0)
        def _():
            dk_ref[...] = jnp.zeros_like(dk_ref)
            dv_ref[...] = jnp.zeros_like(dv_ref)
            _dil_fill_bias(bias_ref)

        dq_ref[...] = jnp.zeros_like(dq_ref)
        prod = do_ref[...] * o_ref[...]
        delta = jnp.zeros_like(prod)
        for hm in _head_masks(prod.shape):
            delta = jnp.where(hm, jnp.sum(jnp.where(hm, prod, 0.0), axis=1, keepdims=True), delta)
        delta_ref[...] = delta

        def block(pi, dil, c, n, gn):
            ws = jnp.maximum(gn - 1, 0)
            qrows = n * (DIL_BLOCK * dil) + c
            krows = ws * (DIL_BLOCK * dil) + c
            q_idx = _dil_rows(qrows, DIL_BLOCK, dil)
            k_idx = _dil_rows(krows, 2 * DIL_BLOCK, dil)
            qb = q_ref[q_idx, :]
            dob = do_ref[q_idx, :]
            lb = l_ref[q_idx, :]
            db = delta_ref[q_idx, :]
            kb = k_ref[k_idx, :].astype(BF16)
            vb = v_ref[k_idx, :].astype(BF16)
            q2 = _sb_stack_heads(qb.astype(BF16), scale)
            do2 = _sb_stack_heads(dob.astype(BF16))
            lse2 = jnp.concatenate([lb[:, HEAD_DIM * h:HEAD_DIM * h + 1] for h in range(2)], axis=0)
            delta2 = jnp.concatenate([db[:, HEAD_DIM * h:HEAD_DIM * h + 1] for h in range(2)], axis=0)
            z = lax.dot_general(q2, kb, (((1,), (1,)), ((), ())), preferred_element_type=F32)
            p = jnp.exp((z + bias_ref[gn - ws]) - lse2)
            dp = lax.dot_general(do2, vb, (((1,), (1,)), ((), ())), preferred_element_type=F32)
            dzb = (p * (dp - delta2)).astype(BF16)
            tn_dims = (((0,), (0,)), ((), ()))
            dq_blk = _sb_unstack_heads(jnp.dot(dzb, kb, preferred_element_type=F32)) * scale
            dk_blk = lax.dot_general(dzb, q2, tn_dims, preferred_element_type=F32)
            dv_blk = lax.dot_general(p.astype(BF16), do2, tn_dims, preferred_element_type=F32)
            dq_ref[q_idx, :] = dq_ref[q_idx, :] + dq_blk
            dk_ref[k_idx, :] = dk_ref[k_idx, :] + dk_blk
            dv_ref[k_idx, :] = dv_ref[k_idx, :] + dv_blk

        _dil_blocks(b, block)

    blk = pl.BlockSpec((DIL_SUPER, LANES), lambda p, b: (b, p))
    full = pl.BlockSpec((S, LANES), lambda p, b: (0, p))
    return pl.pallas_call(
        body, name=name, grid=(npair, nsuper),
        out_shape=[jax.ShapeDtypeStruct((S, W), F32)] * 3,
        in_specs=[blk, full, full, blk, blk, blk], out_specs=[blk, full, full],
        scratch_shapes=[pltpu.VMEM((DIL_SUPER, LANES), F32), pltpu.VMEM((2, 2 * DIL_BLOCK, 2 * DIL_BLOCK), F32)],
        compiler_params=_cparams(("arbitrary", "arbitrary")),
    )(q, k, v, o, lse, do)


def _loss_head(x, gain, target, *, name):
    S, D = x.shape
    tm = _pick(S, (512, 256, 128))

    def body(x_ref, g_ref, t_ref, dx_ref, dg_ref, loss_ref):
        first = pl.program_id(0) == 0
        xh, r = _rms_hat(x_ref[...])
        g = g_ref[...]
        err = xh * g - t_ref[...]
        dy = err * (1.0 / D)
        dxh = dy * g
        dx_ref[...] = r * (dxh - xh * jnp.mean(dxh * xh, axis=-1, keepdims=True))
        dg_part = jnp.sum(dy * xh, axis=0, keepdims=True)
        loss_part = jnp.zeros((1, LANES), F32) + 0.5 * jnp.sum(jnp.mean(err * err, axis=-1, keepdims=True),
                                                               axis=0, keepdims=True)

        @pl.when(first)
        def _():
            dg_ref[...] = dg_part
            loss_ref[...] = loss_part

        @pl.when(jnp.logical_not(first))
        def _():
            dg_ref[...] += dg_part
            loss_ref[...] += loss_part

    row = pl.BlockSpec((tm, D), lambda i: (i, 0))
    vec = pl.BlockSpec((1, D), lambda i: (0, 0))
    return pl.pallas_call(
        body, name=name, grid=(S // tm,),
        out_shape=[jax.ShapeDtypeStruct((S, D), F32), jax.ShapeDtypeStruct((1, D), F32),
                   jax.ShapeDtypeStruct((1, LANES), F32)],
        in_specs=[row, vec, row], out_specs=[row, vec, pl.BlockSpec((1, LANES), lambda i: (0, 0))],
        compiler_params=_cparams(("arbitrary",)),
    )(x, gain, target)


class _NoExchange:
    def gather(self, family):
        return None

    def gathered(self, family, got, weights):
        pass

    def send(self, family, grads):
        return None

    def received(self, family, got):
        pass


def _local_step(x, target, gains, weights, exchanges=None):
    S, D = x.shape
    ex = exchanges or _NoExchange()
    weights = dict(weights)
    d_sb = gains["sb_out_norm"].shape[1]
    d_dl = gains["dil_out_norm"].shape[1]
    cos_t, sin_t = _rope_tables(S)

    riders = ("ffn1_w_up", "ffn1_w_down", "mixer") if exchanges else (None, None, None)
    x1, saved1 = _ffn_fwd(x, gains["ffn1_norm"], weights, tag="ffn1", ex=ex, riders=riders)
    w_in = weights["w_in"]
    w_in_sb, w_in_dl = w_in[:, :3 * d_sb], w_in[:, 3 * d_sb:]
    w_out = weights["w_out"]
    h2 = _rms_fwd([x1], [gains["mix_norm"]], name="mix_norm")
    p_sb = _mm(h2, w_in_sb, outs=(BF16,), name="proj_sb")
    p_dl = _mm(h2, w_in_dl, name="proj_dl")
    q_dl = _rotary(p_dl, cos_t, sin_t, col0=0, width=d_dl, sign=1.0, name="rope_q")
    k_dl = _rotary(p_dl, cos_t, sin_t, col0=d_dl, width=d_dl, sign=1.0, name="rope_k")
    v_dl = p_dl[:, 2 * d_dl:]
    plan = ex.gather("ffn2" if exchanges else None)
    o_sb, carries, *got = _sb_fwd(p_sb, name="sb_fwd", comm=plan)
    ex.gathered("ffn2", got[0] if got else None, weights)
    o_dl, lse_dl = _dl_fwd(q_dl, k_dl, v_dl, name="dl_fwd")
    merged = _rms_fwd([o_sb, o_dl], [gains["sb_out_norm"], gains["dil_out_norm"]], name="out_norm")
    x2 = _mm(merged, w_out, res=x1, name="out_proj")
    x3, saved2 = _ffn_fwd(x2, gains["ffn2_norm"], weights, tag="ffn2", ex=ex)
    dx3, d_final, loss_row = _loss_head(x3, gains["final_norm"], target, name="loss_head")

    dx2, d_ffn2_norm, dwg2, dwu2, dwd2 = _ffn_bwd(dx3, x2, gains["ffn2_norm"], weights, saved2, tag="ffn2", ex=ex)
    d_w_out = _mm(merged, dx2, ta=True, name="d_w_out")
    d_merged = _mm(dx2, w_out, tb=True, name="d_merged")
    (do_sb, do_dl), (d_sb_norm, d_dl_norm) = _rms_bwd(
        d_merged, [o_sb, o_dl], [gains["sb_out_norm"], gains["dil_out_norm"]], None, name="out_norm_bwd")
    plan = ex.send("ffn2", dict(ffn2_w_gate=dwg2, ffn2_w_up=dwu2, ffn2_w_down=dwd2))
    dq_sb, dk_sb, dv_sb, *got = _sb_bwd(p_sb, do_sb, carries, name="sb_bwd", comm=plan)
    ex.received("ffn2", got[0] if got else None)
    dq_dl, dk_dl, dv_dl = _dl_bwd(q_dl, k_dl, v_dl, o_dl, lse_dl, do_dl, name="dl_bwd")
    dq_dl = _rotary(dq_dl, cos_t, sin_t, col0=0, width=d_dl, sign=-1.0, name="rope_dq")
    dk_dl = _rotary(dk_dl, cos_t, sin_t, col0=0, width=d_dl, sign=-1.0, name="rope_dk")
    d_proj = jnp.concatenate([p.astype(BF16) for p in (dq_sb, dk_sb, dv_sb, dq_dl, dk_dl, dv_dl)], axis=1)
    d_w_in = _mm(h2, d_proj, ta=True, name="d_w_in")
    dh2 = _mm(d_proj, w_in, tb=True, name="dh_mix")
    (dx1,), (d_mix_norm,) = _rms_bwd(dh2, [x1], [gains["mix_norm"]], dx2, name="mix_norm_bwd")
    dx, d_ffn1_norm, dwg1, dwu1, dwd1 = _ffn_bwd(
        dx1, x, gains["ffn1_norm"], weights, saved1, tag="ffn1", ex=ex,
        rider=("mixer", dict(w_in=d_w_in, w_out=d_w_out)), spread=True)
    gain_grads = dict(ffn1_norm=d_ffn1_norm, mix_norm=d_mix_norm, sb_out_norm=d_sb_norm, dil_out_norm=d_dl_norm,
                      ffn2_norm=d_ffn2_norm, final_norm=d_final)
    weight_grads = dict(ffn1_w_gate=dwg1, ffn1_w_up=dwu1, ffn1_w_down=dwd1, w_in=d_w_in, w_out=d_w_out,
                        ffn2_w_gate=dwg2, ffn2_w_up=dwu2, ffn2_w_down=dwd2)
    return loss_row, dx, gain_grads, weight_grads


def _mesh_position():
    return lax.axis_index("x"), lax.axis_index("y"), lax.axis_index("c")


def _flip(coord, bit):
    return 1 - coord if bit else coord


RELATIONS = [(rx, ry, rc) for rx in (0, 1) for ry in (0, 1) for rc in (0, 1)][1:]


class _GatherPlan:
    def __init__(self, shards):
        n = len(shards)
        self.operands = list(shards)
        self.out_shapes = [jax.ShapeDtypeStruct((N_DEV,) + s.shape, s.dtype) for s in shards]
        self.scratch = [pltpu.SemaphoreType.DMA((n, 7)), pltpu.SemaphoreType.DMA((n, 7)),
                        pltpu.SemaphoreType.DMA((n,))]

    def _copies(self, in_refs, out_refs, sems):
        send_sems, recv_sems, local_sems = sems
        x, y, c = _mesh_position()
        me, sibling = (x, y, c), (x, y, 1 - c)
        chips = [(1 - x, y), (x, 1 - y), (1 - x, 1 - y)]
        plans = []
        for t, (x_ref, out_ref) in enumerate(zip(in_refs, out_refs)):
            def slot(px, py, pc, out_ref=out_ref):
                return out_ref.at[4 * px + 2 * py + pc]

            def copy(k, block, to, src=None, t=t, slot=slot):
                return pltpu.make_async_remote_copy(
                    src_ref=slot(*block) if src is None else src, dst_ref=slot(*block),
                    send_sem=send_sems.at[t, k], recv_sem=recv_sems.at[t, k],
                    device_id=to, device_id_type=pl.DeviceIdType.MESH)

            plans.append(dict(
                mine=pltpu.make_async_copy(x_ref, slot(*me), local_sems.at[t]),
                first=[copy(0, me, sibling, src=x_ref)]
                + [copy(1 + j, me, (*chip, c), src=x_ref) for j, chip in enumerate(chips)],
                over_ici=[copy(1 + j, (*chip, c), me) for j, chip in enumerate(chips)],
                passed=[copy(4 + j, (*chip, c), sibling) for j, chip in enumerate(chips)],
                from_sibling=[copy(0, sibling, me)] + [copy(4 + j, (*chip, 1 - c), me) for j, chip in enumerate(chips)]))
        return plans

    def start(self, in_refs, out_refs, sems):
        for p in self._copies(in_refs, out_refs, sems):
            p["mine"].start()
            for cp in p["first"]:
                cp.start()

    def finish(self, in_refs, out_refs, sems):
        plans = self._copies(in_refs, out_refs, sems)
        for p in plans:
            for arrived, onward in zip(p["over_ici"], p["passed"]):
                arrived.wait_recv()
                onward.start()
        for p in plans:
            for cp in p["from_sibling"]:
                cp.wait_recv()
            for cp in p["first"] + p["passed"]:
                cp.wait_send()
            p["mine"].wait()


class _Hosted:
    def __init__(self, plan, n_in, n_out, n_scratch):
        self.plan, self.n_in, self.n_out, self.n_scratch = plan, n_in, n_out, n_scratch
        self.operands = list(plan.operands) if plan else []
        self.out_shapes = list(plan.out_shapes) if plan else []
        self.scratch = list(plan.scratch) if plan else []
        self.in_specs = [pl.BlockSpec(memory_space=pl.ANY)] * len(self.operands)
        self.out_specs = [pl.BlockSpec(memory_space=pl.ANY)] * len(self.out_shapes)

    def semantics(self, sem):
        return sem if self.plan is None else ("arbitrary",) * len(sem)

    def _at(self, grid, last):
        hit = None
        for d, n in enumerate(grid):
            here = pl.program_id(d) == (n - 1 if last else 0)
            hit = here if hit is None else hit & here
        return hit

    def begin(self, refs, grid):
        if self.plan is None:
            return refs
        k_in, k_out = len(self.operands), len(self.out_shapes)
        ins, rest = refs[:self.n_in], refs[self.n_in:]
        c_in, rest = rest[:k_in], rest[k_in:]
        outs, rest = rest[:self.n_out], rest[self.n_out:]
        c_out, rest = rest[:k_out], rest[k_out:]
        scratch, sems = rest[:self.n_scratch], rest[self.n_scratch:]
        self._args = (c_in, c_out, sems)
        pl.when(self._at(grid, False))(lambda: self.plan.start(*self._args))
        return tuple(ins) + tuple(outs) + tuple(scratch)

    def end(self, grid):
        if self.plan is not None:
            pl.when(self._at(grid, True))(lambda: self.plan.finish(*self._args))


def _run_plan(plan, *, name):
    hosted = _Hosted(plan, 0, 0, 0)

    def body(*refs):
        hosted.begin(refs, (1,))
        hosted.end((1,))

    return pl.pallas_call(
        body, name=name, grid=(1,), out_shape=hosted.out_shapes,
        in_specs=hosted.in_specs, out_specs=hosted.out_specs, scratch_shapes=hosted.scratch,
        compiler_params=pltpu.CompilerParams(dimension_semantics=("arbitrary",)),
    )(*hosted.operands)


class _ExchangePlan:
    def __init__(self, packs):
        n = len(packs)
        self.operands = list(packs)
        self.out_shapes = [jax.ShapeDtypeStruct(p.shape, p.dtype) for p in packs]
        self.scratch = [pltpu.SemaphoreType.DMA((n, 7)), pltpu.SemaphoreType.DMA((n, 7)),
                        pltpu.SemaphoreType.DMA((n,))]

    def _copies(self, in_refs, out_refs, sems):
        send_sems, recv_sems, local_sems = sems
        x, y, c = _mesh_position()
        me = 4 * x + 2 * y + c
        copies = [pltpu.make_async_copy(i.at[me], o.at[me], local_sems.at[t])
                  for t, (i, o) in enumerate(zip(in_refs, out_refs))]
        for r, (rx, ry, rc) in enumerate(RELATIONS):
            px, py, pc = _flip(x, rx), _flip(y, ry), _flip(c, rc)
            peer = 4 * px + 2 * py + pc
            copies += [pltpu.make_async_remote_copy(
                src_ref=i.at[peer], dst_ref=o.at[me], send_sem=send_sems.at[t, r], recv_sem=recv_sems.at[t, r],
                device_id=(px, py, pc), device_id_type=pl.DeviceIdType.MESH)
                for t, (i, o) in enumerate(zip(in_refs, out_refs))]
        return copies

    def start(self, in_refs, out_refs, sems):
        for cp in self._copies(in_refs, out_refs, sems):
            cp.start()

    def finish(self, in_refs, out_refs, sems):
        for cp in self._copies(in_refs, out_refs, sems):
            cp.wait()


def _all_reduce_rows(v, *, name):
    R, C = v.shape

    def body(v_ref, out_ref, buf, send_sems, recv_sems):
        x, y, c = _mesh_position()
        me = 4 * x + 2 * y + c
        buf[me] = v_ref[...]
        copies = []
        for r, (rx, ry, rc) in enumerate(RELATIONS):
            cp = pltpu.make_async_remote_copy(
                src_ref=v_ref, dst_ref=buf.at[me], send_sem=send_sems.at[r], recv_sem=recv_sems.at[r],
                device_id=(_flip(x, rx), _flip(y, ry), _flip(c, rc)), device_id_type=pl.DeviceIdType.MESH)
            cp.start()
            copies.append(cp)
        for cp in copies:
            cp.wait()
        total = buf[0]
        for s in range(1, N_DEV):
            total = total + buf[s]
        out_ref[...] = total

    return pl.pallas_call(
        body, name=name,
        out_shape=jax.ShapeDtypeStruct((R, C), F32),
        in_specs=[pl.BlockSpec(memory_space=pltpu.VMEM)],
        out_specs=pl.BlockSpec(memory_space=pltpu.VMEM),
        scratch_shapes=[pltpu.VMEM((N_DEV, R, C), F32), pltpu.SemaphoreType.DMA((7,)), pltpu.SemaphoreType.DMA((7,))],
    )(v)


def _sum_slots(recv, *, name):
    _, R, C = recv.shape
    tr = _pick(R, (256, 208, 128, 64, 32, 16))

    def body(r_ref, o_ref):
        total = r_ref[0].astype(F32)
        for s in range(1, N_DEV):
            total = total + r_ref[s].astype(F32)
        o_ref[...] = total

    return pl.pallas_call(
        body, name=name, grid=(R // tr,),
        out_shape=jax.ShapeDtypeStruct((R, C), F32),
        in_specs=[pl.BlockSpec((N_DEV, tr, C), lambda i: (0, i, 0))],
        out_specs=pl.BlockSpec((tr, C), lambda i: (i, 0)),
        compiler_params=_cparams(("parallel",)),
    )(recv)


def _adamw(w, g, m, v, *, name):
    R, C = w.shape
    tr = _pick(R, (256, 128, 64, 32, 16, 8))

    def body(w_ref, g_ref, m_ref, v_ref, d_ref, nm_ref, nv_ref):
        g = g_ref[...]
        m_new = ADAM_B1 * m_ref[...] + (1.0 - ADAM_B1) * g
        v_new = ADAM_B2 * v_ref[...] + (1.0 - ADAM_B2) * (g * g)
        m_hat = m_new / (1.0 - ADAM_B1 ** ADAM_STEP)
        v_hat = v_new / (1.0 - ADAM_B2 ** ADAM_STEP)
        d_ref[...] = -ADAM_LR * (m_hat / (jnp.sqrt(v_hat) + ADAM_EPS) + ADAM_WD * w_ref[...])
        nm_ref[...] = m_new
        nv_ref[...] = v_new

    spec = pl.BlockSpec((tr, C), lambda i: (i, 0))
    return pl.pallas_call(
        body, name=name, grid=(R // tr,),
        out_shape=[jax.ShapeDtypeStruct((R, C), F32)] * 3,
        in_specs=[spec] * 4, out_specs=[spec] * 3,
        compiler_params=_cparams(("parallel",)),
    )(w, g, m, v)


WEIGHT_NAMES = ["ffn1_norm", "ffn1_w_gate", "ffn1_w_up", "ffn1_w_down", "mix_norm", "w_in", "sb_out_norm",
                "dil_out_norm", "w_out", "ffn2_norm", "ffn2_w_gate", "ffn2_w_up", "ffn2_w_down", "final_norm"]
GAIN_NAMES = ["ffn1_norm", "mix_norm", "sb_out_norm", "dil_out_norm", "ffn2_norm", "final_norm"]
COL_SHARDED = ["ffn1_w_gate", "ffn1_w_up", "ffn2_w_gate", "ffn2_w_up", "w_in"]
ROW_SHARDED = ["ffn1_w_down", "ffn2_w_down", "w_out"]
GROUPS = {"mixer": (["w_in"], ["w_out"]),
          "ffn2": (["ffn2_w_gate", "ffn2_w_up"], ["ffn2_w_down"])}
for _ffn in ("ffn1", "ffn2"):
    GROUPS.update({f"{_ffn}_w_gate": ([f"{_ffn}_w_gate"], []), f"{_ffn}_w_up": ([f"{_ffn}_w_up"], []),
                   f"{_ffn}_w_down": ([], [f"{_ffn}_w_down"])})


class _Exchanges:
    def __init__(self, params):
        self.params = params
        self.grads = {}

    def gather(self, group):
        if group is None:
            return None
        cols, rows = GROUPS[group]
        packs = []
        if cols:
            packs.append(jnp.concatenate([self.params[n] for n in cols], axis=1).astype(BF16))
        if rows:
            packs.append(jnp.concatenate([self.params[n] for n in rows], axis=0).astype(BF16))
        return _GatherPlan(packs)

    def gathered(self, group, got, weights):
        if group is None:
            return
        cols, rows = GROUPS[group]
        got = list(got)
        if cols:
            col_all, off = got.pop(0), 0
            for n in cols:
                w = self.params[n].shape[1]
                piece = col_all[:, :, off:off + w]
                weights[n] = jnp.transpose(piece, (1, 0, 2)).reshape(piece.shape[1], N_DEV * w)
                off += w
        if rows:
            row_all, off = got.pop(0), 0
            for n in rows:
                r = self.params[n].shape[0]
                weights[n] = row_all[:, off:off + r, :].reshape(N_DEV * r, row_all.shape[2])
                off += r

    def send(self, group, grads):
        if group is None:
            return None
        cols, rows = GROUPS[group]
        packs = []
        if cols:
            chunks = [jnp.transpose(grads[n].reshape(grads[n].shape[0], N_DEV, self.params[n].shape[1]), (1, 0, 2))
                      for n in cols]
            packs.append(jnp.concatenate(chunks, axis=2).astype(BF16))
        if rows:
            chunks = [grads[n].reshape(N_DEV, self.params[n].shape[0], grads[n].shape[1]) for n in rows]
            packs.append(jnp.concatenate(chunks, axis=1).astype(BF16))
        return _ExchangePlan(packs)

    def received(self, group, got):
        if group is None:
            return
        cols, rows = GROUPS[group]
        got = list(got)
        if cols:
            col_grad, off = _sum_slots(got.pop(0), name=f"sum_col_grads_{group}"), 0
            for n in cols:
                w = self.params[n].shape[1]
                self.grads[n] = col_grad[:, off:off + w]
                off += w
        if rows:
            row_grad, off = _sum_slots(got.pop(0), name=f"sum_row_grads_{group}"), 0
            for n in rows:
                r = self.params[n].shape[0]
                self.grads[n] = row_grad[off:off + r, :]
                off += r


def _step(x, target, params, moments_m, moments_v):
    ex = _Exchanges(params)
    weights = {}
    ex.gathered("ffn1_w_gate", _run_plan(ex.gather("ffn1_w_gate"), name="gather_ffn1_gate"), weights)
    gains = {n: params[n] for n in GAIN_NAMES}
    loss_row, grad_x, gain_grads, _ = _local_step(x, target, gains, weights, ex)
    grads = ex.grads

    rows = [gain_grads[n].reshape(-1, LANES) for n in GAIN_NAMES] + [loss_row]
    small = jnp.concatenate(rows, axis=0)
    pad = (-small.shape[0]) % 8
    small = jnp.pad(small, ((0, pad), (0, 0)))
    small = _all_reduce_rows(small, name="reduce_gains_loss")
    off = 0
    for n in GAIN_NAMES:
        r = gain_grads[n].shape[1] // LANES
        grads[n] = small[off:off + r].reshape(1, -1)
        off += r
    loss = small[off, 0]

    delta, new_m, new_v = {}, {}, {}
    for n in WEIGHT_NAMES:
        delta[n], new_m[n], new_v[n] = _adamw(params[n], grads[n], moments_m[n], moments_v[n], name=f"adamw_{n}")
    return loss, grad_x, grads, delta, new_m, new_v


def kernel(x, ffn1_norm, ffn1_w_gate, ffn1_w_up, ffn1_w_down, mix_norm, w_in, sb_out_norm, dil_out_norm, w_out, ffn2_norm, ffn2_w_gate, ffn2_w_up, ffn2_w_down, final_norm, loss_target, m_ffn1_norm, m_ffn1_w_gate, m_ffn1_w_up, m_ffn1_w_down, m_mix_norm, m_w_in, m_sb_out_norm, m_dil_out_norm, m_w_out, m_ffn2_norm, m_ffn2_w_gate, m_ffn2_w_up, m_ffn2_w_down, m_final_norm, v_ffn1_norm, v_ffn1_w_gate, v_ffn1_w_up, v_ffn1_w_down, v_mix_norm, v_w_in, v_sb_out_norm, v_dil_out_norm, v_w_out, v_ffn2_norm, v_ffn2_w_gate, v_ffn2_w_up, v_ffn2_w_down, v_final_norm):
    given = dict(locals())
    shapes = {n: given[n].shape for n in WEIGHT_NAMES}

    def as2d(a):
        return a.reshape(1, -1) if a.ndim == 1 else a.reshape(a.shape[-2], a.shape[-1])

    params = {n: as2d(given[n]) for n in WEIGHT_NAMES}
    moments_m = {n: as2d(given["m_" + n]) for n in WEIGHT_NAMES}
    moments_v = {n: as2d(given["v_" + n]) for n in WEIGHT_NAMES}
    loss, grad_x, grads, delta, new_m, new_v = _step(x[0], loss_target[0], params, moments_m, moments_v)
    back = lambda d: [d[n].reshape(shapes[n]) for n in WEIGHT_NAMES]
    return (loss, grad_x[None], *back(grads), *back(delta), *back(new_m), *back(new_v))
```

```python
import functools

import jax
import jax.numpy as jnp
from jax import lax
from jax.experimental import pallas as pl
from jax.experimental.pallas import tpu as pltpu

F32 = jnp.float32
BF16 = jnp.bfloat16

N_DEV = 8
HEAD_DIM = 64
LANES = 128
DILATED_PATTERNS = ((128, 1), (512, 4), (2048, 16))
DIL_BLOCK = 128
DIL_SUPER = 2048
DIL_UNROLL = 8
SB_TILE = 256
SB_UNROLL = 4
SB_DEAD = 90.0
SB_UNSEEN = -1e30
ROPE_THETA = 10000.0
RMS_EPS = 1e-6
HALF_STEP = 0.5
ADAM_LR = 0.001
ADAM_B1 = 0.9
ADAM_B2 = 0.999
ADAM_EPS = 1e-08
ADAM_WD = 0.01
ADAM_STEP = 10
NEG_BIG = -1e30
VMEM_CAP_MB = 60


def _pick(n, prefs):
    for p in prefs:
        if n % p == 0:
            return p
    return n


MM_MAX_TILE = 1536


def _largest_tile(n, cap):
    if n <= cap:
        return n
    for t in range(cap - cap % LANES, 0, -LANES):
        if n % t == 0:
            return t
    return n


def _cparams(sem=None, vmem_mb=48):
    return pltpu.CompilerParams(dimension_semantics=sem, vmem_limit_bytes=min(vmem_mb, VMEM_CAP_MB) * 1024 * 1024)


def _nbytes(shape, dtype):
    n = 1
    for s in shape:
        n *= s
    return n * jnp.dtype(dtype).itemsize


def _mm(a, b, *, name, ta=False, tb=False, outs=(F32,), res=None, alpha=1.0, extras=(), epilogue=None,
        tm=None, tn=None, tk=None, comm=None, rows=(), lanes=(), row_sums=0):
    if ta:
        K, M = a.shape
    else:
        M, K = a.shape
    if tb:
        N, Kb = b.shape
    else:
        Kb, N = b.shape
    assert K == Kb, (a.shape, b.shape, ta, tb)
    tm = tm or (_largest_tile(M, MM_MAX_TILE) if ta else _pick(M, (512, 256, 128)))
    tn = tn or _largest_tile(N, MM_MAX_TILE)
    tk = tk or (K if K <= 3072 else _pick(K, (1024, 512, 256, 128)))
    nk = K // tk
    a_spec = pl.BlockSpec((tk, tm), lambda i, j, k: (k, i)) if ta else pl.BlockSpec((tm, tk), lambda i, j, k: (i, k))
    b_spec = pl.BlockSpec((tn, tk), lambda i, j, k: (j, k)) if tb else pl.BlockSpec((tk, tn), lambda i, j, k: (k, j))
    mn_spec = pl.BlockSpec((tm, tn), lambda i, j, k: (i, j))
    dims = (((0 if ta else 1,), (1 if tb else 0,)), ((), ()))
    row_spec = pl.BlockSpec((1, tn), lambda i, j, k: (0, j))
    lane_spec = pl.BlockSpec((tm, LANES), lambda i, j, k: (i, 0))
    n_extra = len(extras) + (1 if res is not None else 0) + len(rows) + len(lanes)
    n_mn = len(outs)
    n_out = n_mn + row_sums
    assert row_sums == 0 or tn == N
    grid = (M // tm, N // tn, nk)
    hosted = _Hosted(comm, n_in=2 + n_extra, n_out=n_out, n_scratch=1 if nk > 1 else 0)

    def body(*refs):
        a_ref, b_ref = refs[0], refs[1]
        in_refs = refs[2:2 + n_extra]
        refs = hosted.begin(refs, grid)
        out_refs = refs[2 + n_extra:2 + n_extra + n_out]
        prod = lax.dot_general(a_ref[...].astype(BF16), b_ref[...].astype(BF16), dims, preferred_element_type=F32)

        def finish(acc):
            blocks = [r[...] for r in in_refs]
            if res is not None:
                r_blk, blocks = blocks[0], blocks[1:]
            else:
                r_blk = None
            if epilogue is None:
                val = acc * alpha
                if r_blk is not None:
                    val = val + r_blk
                vals = (val,)
            else:
                vals = epilogue(acc, r_blk, *blocks)
                vals = vals if isinstance(vals, (tuple, list)) else (vals,)
            for o_ref, v in zip(out_refs[:n_mn], vals[:n_mn]):
                o_ref[...] = v.astype(o_ref.dtype)
            first_rows = pl.program_id(0) == 0
            for o_ref, part in zip(out_refs[n_mn:], vals[n_mn:]):
                @pl.when(first_rows)
                def _(o_ref=o_ref, part=part):
                    o_ref[...] = part

                @pl.when(jnp.logical_not(first_rows))
                def _(o_ref=o_ref, part=part):
                    o_ref[...] += part

        if nk == 1:
            finish(prod)
        else:
            acc_ref = refs[2 + n_extra + n_out]
            k = pl.program_id(2)

            @pl.when(k == 0)
            def _():
                acc_ref[...] = prod

            @pl.when(k > 0)
            def _():
                acc_ref[...] += prod

            @pl.when(k == nk - 1)
            def _():
                finish(acc_ref[...])

        hosted.end(grid)

    mn_operands = ([res] if res is not None else []) + list(extras)
    operands = [a, b] + mn_operands + list(rows) + list(lanes)
    in_specs = [a_spec, b_spec] + [mn_spec] * len(mn_operands) + [row_spec] * len(rows) + [lane_spec] * len(lanes)
    est = 2 * (_nbytes((tm, tk), a.dtype) + _nbytes((tk, tn), b.dtype))
    est += 2 * sum(_nbytes((tm, tn), o.dtype) for o in mn_operands)
    est += 2 * sum(_nbytes((tm, tn), d) for d in outs) + 2 * _nbytes((tm, tn), F32)
    semantics = ("parallel", "parallel", "arbitrary") if row_sums == 0 else ("arbitrary",) * 3
    result = pl.pallas_call(
        body, name=name, grid=grid,
        out_shape=[jax.ShapeDtypeStruct((M, N), d) for d in outs]
        + [jax.ShapeDtypeStruct((1, N), F32)] * row_sums + hosted.out_shapes,
        in_specs=in_specs + hosted.in_specs,
        out_specs=[mn_spec] * n_mn + [row_spec] * row_sums + hosted.out_specs,
        scratch_shapes=([pltpu.VMEM((tm, tn), F32)] if nk > 1 else []) + hosted.scratch,
        compiler_params=_cparams(hosted.semantics(semantics), vmem_mb=max(32, 2 * est // (1024 * 1024))),
    )(*operands, *hosted.operands)
    own, got = result[:n_out], list(result[n_out:])
    own = own[0] if n_out == 1 else own
    return own if comm is None else (own, got)


def _rms_hat(x):
    r = lax.rsqrt(jnp.mean(x * x, axis=-1, keepdims=True) + RMS_EPS)
    return x * r, r


def _rms_fwd(xs, gains, *, name):
    S = xs[0].shape[0]
    widths = [x.shape[1] for x in xs]
    tm = _pick(S, (512, 256, 128))
    n = len(xs)

    def body(*refs):
        o_ref = refs[2 * n]
        off = 0
        for i in range(n):
            xh, _ = _rms_hat(refs[i][...])
            o_ref[:, off:off + widths[i]] = (xh * refs[n + i][...]).astype(o_ref.dtype)
            off += widths[i]

    return pl.pallas_call(
        body, name=name, grid=(S // tm,),
        out_shape=jax.ShapeDtypeStruct((S, sum(widths)), BF16),
        in_specs=[pl.BlockSpec((tm, w), lambda i: (i, 0)) for w in widths]
        + [pl.BlockSpec((1, w), lambda i: (0, 0)) for w in widths],
        out_specs=pl.BlockSpec((tm, sum(widths)), lambda i: (i, 0)),
        compiler_params=_cparams(("parallel",)),
    )(*xs, *gains)


def _rms_bwd(dh, xs, gains, res, *, name):
    S = xs[0].shape[0]
    widths = [x.shape[1] for x in xs]
    tm = _pick(S, (512, 256, 128))
    n = len(xs)
    has_res = res is not None

    def body(*refs):
        dh_ref = refs[0]
        x_refs = refs[1:1 + n]
        g_refs = refs[1 + n:1 + 2 * n]
        r_ref = refs[1 + 2 * n] if has_res else None
        base = 1 + 2 * n + (1 if has_res else 0)
        dx_refs = refs[base:base + n]
        dg_refs = refs[base + n:base + 2 * n]
        first = pl.program_id(0) == 0
        off = 0
        for i in range(n):
            x = x_refs[i][...]
            xh, r = _rms_hat(x)
            d = dh_ref[:, off:off + widths[i]]
            dxh = d * g_refs[i][...]
            dx = r * (dxh - xh * jnp.mean(dxh * xh, axis=-1, keepdims=True))
            if has_res:
                dx = dx + r_ref[...]
            dx_refs[i][...] = dx
            part = jnp.sum(d * xh, axis=0, keepdims=True)

            @pl.when(first)
            def _(i=i, part=part):
                dg_refs[i][...] = part

            @pl.when(jnp.logical_not(first))
            def _(i=i, part=part):
                dg_refs[i][...] += part

            off += widths[i]

    in_specs = [pl.BlockSpec((tm, sum(widths)), lambda i: (i, 0))]
    in_specs += [pl.BlockSpec((tm, w), lambda i: (i, 0)) for w in widths]
    in_specs += [pl.BlockSpec((1, w), lambda i: (0, 0)) for w in widths]
    operands = [dh, *xs, *gains]
    if has_res:
        in_specs.append(pl.BlockSpec((tm, widths[0]), lambda i: (i, 0)))
        operands.append(res)
    out = pl.pallas_call(
        body, name=name, grid=(S // tm,),
        out_shape=[jax.ShapeDtypeStruct((S, w), F32) for w in widths] + [jax.ShapeDtypeStruct((1, w), F32) for w in widths],
        in_specs=in_specs,
        out_specs=[pl.BlockSpec((tm, w), lambda i: (i, 0)) for w in widths]
        + [pl.BlockSpec((1, w), lambda i: (0, 0)) for w in widths],
        compiler_params=_cparams(("arbitrary",)),
    )(*operands)
    return out[:n], out[n:]


def _sigmoid(g):
    return 1.0 / (1.0 + jnp.exp(-g))


def _ride(result, plan):
    return result if plan is not None else (result, None)


def _ffn_fwd(x, gain, w, *, tag, ex, riders=(None, None, None), head=None):
    h = _rms_fwd([x], [gain], name=f"{tag}_norm")
    plan = ex.gather(riders[0])
    g, got = _ride(_mm(h, w[f"{tag}_w_gate"], outs=(BF16,), name=f"{tag}_gate", comm=plan), plan)
    ex.gathered(riders[0], got, w)

    def act(acc, _, g_blk):
        gf = g_blk.astype(F32)
        return acc, gf * _sigmoid(gf) * acc

    plan = ex.gather(riders[1])
    (u, a), got = _ride(_mm(h, w[f"{tag}_w_up"], outs=(BF16, BF16), extras=(g,), epilogue=act, name=f"{tag}_up_act",
                            comm=plan), plan)
    ex.gathered(riders[1], got, w)
    plan = ex.gather(riders[2])
    if head is None:
        y, got = _ride(_mm(a, w[f"{tag}_w_down"], res=x, alpha=HALF_STEP, name=f"{tag}_down", comm=plan), plan)
    else:
        final_gain, target = head
        y, got = _ride(_mm(a, w[f"{tag}_w_down"], res=x, extras=(target,), rows=(final_gain,), row_sums=2,
                           epilogue=_loss_head_epilogue, name=f"{tag}_down_loss", comm=plan), plan)
    ex.gathered(riders[2], got, w)
    return y, (h, g, u, a)


def _loss_head_epilogue(acc, x_in, target, gain):
    xh, r = _rms_hat(x_in + HALF_STEP * acc)
    err = xh * gain - target
    dy = err * (1.0 / acc.shape[1])
    dxh = dy * gain
    dx = r * (dxh - xh * jnp.mean(dxh * xh, axis=-1, keepdims=True))
    loss = 0.5 * jnp.sum(jnp.mean(err * err, axis=-1, keepdims=True), axis=0, keepdims=True)
    return dx, jnp.sum(dy * xh, axis=0, keepdims=True), jnp.zeros_like(gain) + loss


def _rms_bwd_epilogue(acc, dh_so_far, x, dres, gain):
    dh = acc if dh_so_far is None else acc + dh_so_far
    xh, r = _rms_hat(x)
    dxh = dh * gain
    dx = r * (dxh - xh * jnp.mean(dxh * xh, axis=-1, keepdims=True)) + dres
    return dx, jnp.sum(dh * xh, axis=0, keepdims=True)


def _ffn_bwd(dout, x, gain, w, saved, *, tag, ex, rider=(None, None), spread=False):
    h, g, u, a = saved
    wg, wu, wd = (w[f"{tag}_w_{n}"] for n in ("gate", "up", "down"))

    def act_bwd(acc, _, g_blk, u_blk):
        gf, uf = g_blk.astype(F32), u_blk.astype(F32)
        da = acc * HALF_STEP
        sig = _sigmoid(gf)
        silu = gf * sig
        return da * uf * (sig * (1.0 + gf * (1.0 - sig))), da * silu

    def carrying(group, grad, call):
        group = group if spread else None
        plan = ex.send(group, {group: grad})
        out, got = _ride(call(plan), plan)
        ex.received(group, got)
        return out

    plan = ex.send(*rider)
    (dg, du), got = _ride(_mm(dout, wd, tb=True, outs=(BF16, BF16), extras=(g, u), epilogue=act_bwd,
                              name=f"{tag}_bwd_act", comm=plan), plan)
    ex.received(rider[0], got)
    dwg = _mm(h, dg, ta=True, name=f"{tag}_dwg")
    dwu = carrying(f"{tag}_w_gate", dwg, lambda plan: _mm(h, du, ta=True, name=f"{tag}_dwu", comm=plan))
    dwd = carrying(f"{tag}_w_up", dwu,
                   lambda plan: _mm(a, dout, ta=True, alpha=HALF_STEP, name=f"{tag}_dwd", comm=plan))
    dh = carrying(f"{tag}_w_down", dwd, lambda plan: _mm(dg, wg, tb=True, name=f"{tag}_dh_gate", comm=plan))
    dx, dgain = _mm(du, wu, tb=True, res=dh, extras=(x, dout), rows=(gain,), row_sums=1, epilogue=_rms_bwd_epilogue,
                    name=f"{tag}_dh_up_norm_bwd")
    return dx, dgain, dwg, dwu, dwd


def _rope_tables(S):
    half = HEAD_DIM // 2
    inv_freq = ROPE_THETA ** (-jnp.arange(half, dtype=F32) / half)
    ang = jnp.arange(S, dtype=F32)[:, None] * inv_freq[None, :]
    cos, sin = jnp.cos(ang), jnp.sin(ang)
    reps = LANES // HEAD_DIM
    cos_t = jnp.tile(jnp.concatenate([cos, cos], axis=1), (1, reps))
    sin_t = jnp.tile(jnp.concatenate([-sin, sin], axis=1), (1, reps))
    return cos_t, sin_t


def _rotate(v, cos, sin, sign):
    half = HEAD_DIM // 2
    groups = []
    for g in range(v.shape[1] // LANES):
        t = v[:, g * LANES:(g + 1) * LANES]
        lane = lax.broadcasted_iota(jnp.int32, t.shape, 1)
        swapped = jnp.where(lane % HEAD_DIM < half, pltpu.roll(t, LANES - half, axis=1), pltpu.roll(t, half, axis=1))
        groups.append(t * cos + swapped * (sin * sign))
    return groups[0] if len(groups) == 1 else jnp.concatenate(groups, axis=1)


def _join_d_proj(pieces, rotated, cos_t, sin_t, *, name):
    S = pieces[0].shape[0]
    widths = [p.shape[1] for p in pieces]
    tm = _pick(S, (256, 128))
    n = len(pieces)

    def body(*refs):
        c_ref, s_ref, o_ref = refs[n], refs[n + 1], refs[n + 2]
        off = 0
        for i in range(n):
            v = refs[i][...]
            if i in rotated:
                v = _rotate(v, c_ref[...], s_ref[...], -1.0)
            o_ref[:, off:off + widths[i]] = v.astype(o_ref.dtype)
            off += widths[i]

    return pl.pallas_call(
        body, name=name, grid=(S // tm,),
        out_shape=jax.ShapeDtypeStruct((S, sum(widths)), BF16),
        in_specs=[pl.BlockSpec((tm, w), lambda i: (i, 0)) for w in widths]
        + [pl.BlockSpec((tm, LANES), lambda i: (i, 0))] * 2,
        out_specs=pl.BlockSpec((tm, sum(widths)), lambda i: (i, 0)),
        compiler_params=_cparams(("parallel",)),
    )(*pieces, cos_t, sin_t)


def _head_masks(shape):
    lane = lax.broadcasted_iota(jnp.int32, shape, 1)
    return [(lane >= HEAD_DIM * h) & (lane < HEAD_DIM * (h + 1)) for h in range(LANES // HEAD_DIM)]


def _sb_scores(q2, k_j):
    z = lax.dot_general(q2, k_j, (((1,), (1,)), ((), ())), preferred_element_type=F32)
    sign_bit = jnp.int32(-2 ** 31)
    minus_abs = lax.bitcast_convert_type(lax.bitcast_convert_type(z, jnp.int32) | sign_bit, F32)
    softplus = jnp.maximum(z, 0.0) + jnp.log(1.0 + jnp.exp(minus_abs))
    return z - softplus, softplus


def _sb_stack_heads(t, scale=None):
    parts = [jnp.where(hm, t, jnp.zeros_like(t)) for hm in _head_masks(t.shape)]
    t2 = jnp.concatenate(parts, axis=0)
    if scale is not None:
        t2 = (t2.astype(F32) * scale).astype(t2.dtype)
    return t2


def _sb_unstack_heads(t2):
    T = t2.shape[0] // 2
    masks = _head_masks((T, LANES))
    return jnp.where(masks[0], t2[:T], t2[T:])


def _sb_causal(T):
    row = lax.broadcasted_iota(jnp.int32, (2 * T, T), 0)
    col = lax.broadcasted_iota(jnp.int32, (2 * T, T), 1)
    return col < jnp.where(row >= T, row - T, row)


def _sb_triangle(T, later):
    row = lax.broadcasted_iota(jnp.int32, (T, T), 0)
    col = lax.broadcasted_iota(jnp.int32, (T, T), 1)
    return ((row > col) if later else (row < col)).astype(BF16)


def _sb_fwd(p_sb, *, name, comm=None):
    S = p_sb.shape[0]
    W = p_sb.shape[1] // 3
    npair = W // LANES
    T = SB_TILE
    n_tiles = S // T
    assert n_tiles <= HEAD_DIM
    scale = HEAD_DIM ** -0.5

    grid = (npair, n_tiles)
    hosted = _Hosted(comm, n_in=3, n_out=2, n_scratch=0)

    def body(*refs):
        q_ref, k_ref, v_ref, o_ref, c_ref = hosted.begin(refs, grid)
        I = pl.program_id(1)
        lane = lax.broadcasted_iota(jnp.int32, (T, LANES), 1)
        causal = _sb_causal(T)
        later_than = _sb_triangle(T, True)
        q2 = _sb_stack_heads(q_ref[...], scale)

        def scores(J, diag):
            off = pl.multiple_of(J * T, T)
            log_beta, stay = _sb_scores(q2, k_ref[pl.ds(off, T), :])
            if diag:
                stay = jnp.where(causal, stay, 0.0)
            local = jnp.dot(stay.astype(BF16), later_than, preferred_element_type=F32)
            return log_beta, local, jnp.sum(stay, axis=1, keepdims=True), v_ref[pl.ds(off, T), :]

        def weigh(J, sc, gone, acc, carr, diag):
            log_beta, local, _, v_j = sc
            w = jnp.exp((log_beta - gone) - local)
            if diag:
                w = jnp.where(causal, w, 0.0)
            acc = acc + jnp.dot(w.astype(BF16), v_j, preferred_element_type=F32)
            carr = jnp.where(lane == J, -gone[:T], carr)
            carr = jnp.where(lane == HEAD_DIM + J, -gone[T:], carr)
            return acc, carr

        def tiles(J, count, state, diag):
            gone, acc, carr, _ = state
            scs = [scores(J - u, diag and u == 0) for u in range(count)]
            for u, sc in enumerate(scs):
                acc, carr = weigh(J - u, sc, gone, acc, carr, diag and u == 0)
                gone = gone + sc[2]
            return gone, acc, carr, jnp.min(gone)

        U = SB_UNROLL
        alive = lambda st: st[3] < SB_DEAD
        state = (jnp.zeros((2 * T, 1), F32), jnp.zeros((2 * T, LANES), F32),
                 jnp.full((T, LANES), SB_UNSEEN, F32), jnp.zeros((), F32))
        state = lax.cond(I > 0, lambda st: tiles(I, 2, st, True), lambda st: tiles(I, 1, st, True), state)
        rest = jnp.maximum(I - 1, 0)
        singles = jnp.where(rest > 0, (rest - 1) % U + 1, 0)
        _, state = lax.while_loop(lambda c: (c[0] < singles) & alive(c[1]),
                                  lambda c: (c[0] + 1, tiles(I - 2 - c[0], 1, c[1], False)), (jnp.int32(0), state))
        blocks = (rest - singles) // U
        _, state = lax.while_loop(lambda c: (c[0] < blocks) & alive(c[1]),
                                  lambda c: (c[0] + 1, tiles(I - 2 - singles - U * c[0], U, c[1], False)),
                                  (jnp.int32(0), state))
        _, acc, carr, _ = state
        o_ref[...] = _sb_unstack_heads(acc)
        c_ref[...] = carr
        hosted.end(grid)

    blk = lambda I_off: pl.BlockSpec((T, LANES), lambda p, I: (I, I_off + p))
    full = lambda off: pl.BlockSpec((S, LANES), lambda p, I: (0, off + p))
    o, carries, *got = pl.pallas_call(
        body, name=name, grid=grid,
        out_shape=[jax.ShapeDtypeStruct((S, W), F32), jax.ShapeDtypeStruct((S, W), F32)] + hosted.out_shapes,
        in_specs=[blk(0), full(npair), full(2 * npair)] + hosted.in_specs,
        out_specs=[blk(0), blk(0)] + hosted.out_specs,
        scratch_shapes=hosted.scratch,
        compiler_params=_cparams(hosted.semantics(("parallel", "arbitrary"))),
    )(p_sb, p_sb, p_sb, *hosted.operands)
    return (o, carries) if comm is None else (o, carries, got)


def _sb_bwd(p_sb, do, carries, *, name, comm=None):
    S = p_sb.shape[0]
    W = p_sb.shape[1] // 3
    npair = W // LANES
    T = SB_TILE
    n_tiles = S // T
    scale = HEAD_DIM ** -0.5

    grid = (npair, n_tiles)
    hosted = _Hosted(comm, n_in=5, n_out=3, n_scratch=0)

    def body(*refs):
        q_ref, k_ref, v_ref, do_ref, c_ref, dq_ref, dk_ref, dv_ref = hosted.begin(refs, grid)
        I = pl.program_id(1)

        @pl.when(I == 0)
        def _():
            dk_ref[...] = jnp.zeros_like(dk_ref)
            dv_ref[...] = jnp.zeros_like(dv_ref)

        lane = lax.broadcasted_iota(jnp.int32, (T, LANES), 1)
        causal = _sb_causal(T)
        later_than = _sb_triangle(T, True)
        earlier_than = _sb_triangle(T, False)
        q2 = _sb_stack_heads(q_ref[...], scale)
        do2 = _sb_stack_heads(do_ref[...].astype(BF16))
        carr = c_ref[...]
        tn_dims = (((0,), (0,)), ((), ()))

        def chain(J, diag):
            off = pl.multiple_of(J * T, T)
            k_j = k_ref[pl.ds(off, T), :]
            v_j = v_ref[pl.ds(off, T), :]
            log_beta, stay = _sb_scores(q2, k_j)
            if diag:
                stay = jnp.where(causal, stay, 0.0)
            lc = jnp.concatenate(
                [jnp.sum(jnp.where(lane == HEAD_DIM * h + J, carr, 0.0), axis=1, keepdims=True) for h in range(2)],
                axis=0)
            w = jnp.exp((log_beta + lc) - jnp.dot(stay.astype(BF16), later_than, preferred_element_type=F32))
            if diag:
                w = jnp.where(causal, w, 0.0)
            dw = lax.dot_general(do2, v_j, (((1,), (1,)), ((), ())), preferred_element_type=F32)
            e = w * dw
            local = jnp.dot(e.astype(BF16), earlier_than, preferred_element_type=F32)
            return off, k_j, w, e, local, jnp.exp(log_beta), jnp.sum(e, axis=1, keepdims=True)

        def finish(ch, ec, dq_acc, diag):
            off, k_j, w, e, local, beta, _ = ch
            e_before = local + ec
            dz = e - beta * (e + e_before)
            if diag:
                dz = jnp.where(causal, dz, 0.0)
            dzb = dz.astype(BF16)
            dq_acc = dq_acc + jnp.dot(dzb, k_j, preferred_element_type=F32)
            dk_ref[pl.ds(off, T), :] += lax.dot_general(dzb, q2, tn_dims, preferred_element_type=F32)
            dv_ref[pl.ds(off, T), :] += lax.dot_general(w.astype(BF16), do2, tn_dims, preferred_element_type=F32)
            return dq_acc

        def tiles(J, count, state, diag):
            ec, dq_acc = state
            chains = [chain(J + u, diag and u == count - 1) for u in range(count)]
            for u, ch in enumerate(chains):
                dq_acc = finish(ch, ec, dq_acc, diag and u == count - 1)
                ec = ec + ch[6]
            return ec, dq_acc

        lane_row = lax.broadcasted_iota(jnp.int32, (1, LANES), 1)
        reached = (jnp.max(carr, axis=0, keepdims=True) > 0.5 * SB_UNSEEN) & (lane_row < HEAD_DIM)
        first = jnp.min(jnp.where(reached, lane_row.astype(F32), float(n_tiles))).astype(jnp.int32)
        U = SB_UNROLL
        count = I - first
        rest = jnp.maximum(count - 1, 0)
        state = (jnp.zeros((2 * T, 1), F32), jnp.zeros((2 * T, LANES), F32))
        state = lax.fori_loop(0, rest // U, lambda jj, st: tiles(first + U * jj, U, st, False), state)
        state = lax.fori_loop(0, rest % U, lambda r, st: tiles(I - 1 - rest % U + r, 1, st, False), state)
        _, dq_acc = lax.cond(count > 0, lambda st: tiles(I - 1, 2, st, True), lambda st: tiles(I, 1, st, True), state)
        dq_ref[...] = _sb_unstack_heads(dq_acc) * scale
        hosted.end(grid)

    blk = lambda src_off: pl.BlockSpec((T, LANES), lambda p, I: (I, src_off + p))
    full = lambda off: pl.BlockSpec((S, LANES), lambda p, I: (0, off + p))
    dq, dk, dv, *got = pl.pallas_call(
        body, name=name, grid=grid,
        out_shape=[jax.ShapeDtypeStruct((S, W), F32)] * 3 + hosted.out_shapes,
        in_specs=[blk(0), full(npair), full(2 * npair), blk(0), blk(0)] + hosted.in_specs,
        out_specs=[blk(0), full(0), full(0)] + hosted.out_specs,
        scratch_shapes=hosted.scratch,
        compiler_params=_cparams(hosted.semantics(("parallel", "arbitrary"))),
    )(p_sb, p_sb, p_sb, do, carries, *hosted.operands)
    return (dq, dk, dv) if comm is None else (dq, dk, dv, got)


def _dil_blocks(b, body_fn):
    for pi, (window, dil) in enumerate(DILATED_PATTERNS):
        assert window // dil == DIL_BLOCK
        nblk = DIL_SUPER // (DIL_BLOCK * dil)
        assert (dil * nblk) % DIL_UNROLL == 0

        def group(g, _, pi=pi, dil=dil, nblk=nblk):
            for u in range(DIL_UNROLL):
                t = g * DIL_UNROLL + u
                n = t % nblk
                body_fn(pi, dil, t // nblk, n, b * nblk + n)
            return 0

        lax.fori_loop(0, dil * nblk // DIL_UNROLL, group, 0)


def _dil_rows(start, size, dil):
    if dil == 1:
        return pl.ds(pl.multiple_of(start, DIL_BLOCK), size)
    return pl.ds(start, size, stride=dil)


def _dil_fill_bias(bias_ref):
    row = lax.broadcasted_iota(jnp.int32, (2 * DIL_BLOCK, 2 * DIL_BLOCK), 0)
    kk = lax.broadcasted_iota(jnp.int32, (2 * DIL_BLOCK, 2 * DIL_BLOCK), 1)
    qi = jnp.where(row >= DIL_BLOCK, row - DIL_BLOCK, row)
    for s in range(2):
        dist = s * DIL_BLOCK + qi - kk
        bias_ref[s] = jnp.where((dist >= 0) & (dist <= DIL_BLOCK), 0.0, NEG_BIG)


def _dl_fwd(p_dl, *, name):
    S, W = p_dl.shape[0], p_dl.shape[1] // 3
    npair = W // LANES
    nsuper = S // DIL_SUPER
    assert S % DIL_SUPER == 0 and S // max(d for _, d in DILATED_PATTERNS) >= 2 * DIL_BLOCK
    scale = HEAD_DIM ** -0.5
    npat = len(DILATED_PATTERNS)

    def body(q_ref, k_ref, v_ref, o_ref, l_ref, bias_ref, *pattern_refs):
        op_refs, lp_refs = pattern_refs[:npat], pattern_refs[npat:]
        b = pl.program_id(1)
        masks = _head_masks((DIL_BLOCK, LANES))
        pl.when(b == 0)(lambda: _dil_fill_bias(bias_ref))

        def block(pi, dil, c, n, gn):
            ws = jnp.maximum(gn - 1, 0)
            qrows = n * (DIL_BLOCK * dil) + c
            krows = ws * (DIL_BLOCK * dil) + c
            q_idx = _dil_rows(qrows, DIL_BLOCK, dil)
            k_idx = _dil_rows(krows, 2 * DIL_BLOCK, dil)
            qb = q_ref[q_idx, :]
            kb = k_ref[k_idx, :].astype(BF16)
            vb = v_ref[k_idx, :].astype(BF16)
            q2 = _sb_stack_heads(qb.astype(BF16), scale)
            z = lax.dot_general(q2, kb, (((1,), (1,)), ((), ())), preferred_element_type=F32) + bias_ref[gn - ws]
            m = jnp.max(z, axis=1, keepdims=True)
            p = jnp.exp(z - m)
            den = jnp.sum(p, axis=1, keepdims=True)
            acc = jnp.dot(p.astype(BF16), vb, preferred_element_type=F32)
            lse = m + jnp.log(den)
            op_refs[pi][q_idx, :] = _sb_unstack_heads(acc / den)
            lp_refs[pi][q_idx, :] = jnp.where(masks[0], lse[:DIL_BLOCK], lse[DIL_BLOCK:])

        _dil_blocks(b, block)
        lses = [r[...] for r in lp_refs]
        top = functools.reduce(jnp.maximum, lses)
        ws_ = [jnp.exp(l - top) for l in lses]
        den = functools.reduce(jnp.add, ws_)
        num = functools.reduce(jnp.add, [w * r[...] for r, w in zip(op_refs, ws_)])
        o_ref[...] = num / den
        l_ref[...] = top + jnp.log(den)

    blk = pl.BlockSpec((DIL_SUPER, LANES), lambda p, b: (b, p))
    full = lambda off: pl.BlockSpec((S, LANES), lambda p, b: (0, off + p))
    return pl.pallas_call(
        body, name=name, grid=(npair, nsuper),
        out_shape=[jax.ShapeDtypeStruct((S, W), F32)] * 2,
        in_specs=[blk, full(npair), full(2 * npair)], out_specs=[blk, blk],
        scratch_shapes=[pltpu.VMEM((2, 2 * DIL_BLOCK, 2 * DIL_BLOCK), F32)]
        + [pltpu.VMEM((DIL_SUPER, LANES), F32)] * (2 * npat),
        compiler_params=_cparams(("arbitrary", "arbitrary")),
    )(p_dl, p_dl, p_dl)


def _dl_bwd(p_dl, o, lse, do, *, name):
    S, W = p_dl.shape[0], p_dl.shape[1] // 3
    npair = W // LANES
    nsuper = S // DIL_SUPER
    scale = HEAD_DIM ** -0.5

    def body(q_ref, k_ref, v_ref, o_ref, l_ref, do_ref, dq_ref, dk_ref, dv_ref, delta_ref, bias_ref):
        b = pl.program_id(1)

        @pl.when(b == 0)
        def _():
            dk_ref[...] = jnp.zeros_like(dk_ref)
            dv_ref[...] = jnp.zeros_like(dv_ref)
            _dil_fill_bias(bias_ref)

        dq_ref[...] = jnp.zeros_like(dq_ref)
        prod = do_ref[...] * o_ref[...]
        delta = jnp.zeros_like(prod)
        for hm in _head_masks(prod.shape):
            delta = jnp.where(hm, jnp.sum(jnp.where(hm, prod, 0.0), axis=1, keepdims=True), delta)
        delta_ref[...] = delta

        def block(pi, dil, c, n, gn):
            ws = jnp.maximum(gn - 1, 0)
            qrows = n * (DIL_BLOCK * dil) + c
            krows = ws * (DIL_BLOCK * dil) + c
            q_idx = _dil_rows(qrows, DIL_BLOCK, dil)
            k_idx = _dil_rows(krows, 2 * DIL_BLOCK, dil)
            qb = q_ref[q_idx, :]
            dob = do_ref[q_idx, :]
            lb = l_ref[q_idx, :]
            db = delta_ref[q_idx, :]
            kb = k_ref[k_idx, :].astype(BF16)
            vb = v_ref[k_idx, :].astype(BF16)
            q2 = _sb_stack_heads(qb.astype(BF16), scale)
            do2 = _sb_stack_heads(dob.astype(BF16))
            lse2 = jnp.concatenate([lb[:, HEAD_DIM * h:HEAD_DIM * h + 1] for h in range(2)], axis=0)
            delta2 = jnp.concatenate([db[:, HEAD_DIM * h:HEAD_DIM * h + 1] for h in range(2)], axis=0)
            z = lax.dot_general(q2, kb, (((1,), (1,)), ((), ())), preferred_element_type=F32)
            p = jnp.exp((z + bias_ref[gn - ws]) - lse2)
            dp = lax.dot_general(do2, vb, (((1,), (1,)), ((), ())), preferred_element_type=F32)
            dzb = (p * (dp - delta2)).astype(BF16)
            tn_dims = (((0,), (0,)), ((), ()))
            dq_blk = _sb_unstack_heads(jnp.dot(dzb, kb, preferred_element_type=F32)) * scale
            dk_blk = lax.dot_general(dzb, q2, tn_dims, preferred_element_type=F32)
            dv_blk = lax.dot_general(p.astype(BF16), do2, tn_dims, preferred_element_type=F32)
            dq_ref[q_idx, :] = dq_ref[q_idx, :] + dq_blk
            dk_ref[k_idx, :] = dk_ref[k_idx, :] + dk_blk
            dv_ref[k_idx, :] = dv_ref[k_idx, :] + dv_blk

        _dil_blocks(b, block)

    blk = pl.BlockSpec((DIL_SUPER, LANES), lambda p, b: (b, p))
    full = lambda off: pl.BlockSpec((S, LANES), lambda p, b: (0, off + p))
    return pl.pallas_call(
        body, name=name, grid=(npair, nsuper),
        out_shape=[jax.ShapeDtypeStruct((S, W), F32)] * 3,
        in_specs=[blk, full(npair), full(2 * npair), blk, blk, blk], out_specs=[blk, full(0), full(0)],
        scratch_shapes=[pltpu.VMEM((DIL_SUPER, LANES), F32), pltpu.VMEM((2, 2 * DIL_BLOCK, 2 * DIL_BLOCK), F32)],
        compiler_params=_cparams(("arbitrary", "arbitrary")),
    )(p_dl, p_dl, p_dl, o, lse, do)


class _NoExchange:
    def gather(self, family):
        return None

    def gathered(self, family, got, weights):
        pass

    def send(self, family, grads):
        return None

    def received(self, family, got):
        pass


def _local_step(x, target, gains, weights, exchanges=None):
    S, D = x.shape
    ex = exchanges or _NoExchange()
    weights = dict(weights)
    d_sb = gains["sb_out_norm"].shape[1]
    d_dl = gains["dil_out_norm"].shape[1]
    cos_t, sin_t = _rope_tables(S)

    riders = ("ffn1_w_up", "ffn1_w_down", "mixer") if exchanges else (None, None, None)
    x1, saved1 = _ffn_fwd(x, gains["ffn1_norm"], weights, tag="ffn1", ex=ex, riders=riders)
    w_in = weights["w_in"]
    w_in_sb, w_in_dl = w_in[:, :3 * d_sb], w_in[:, 3 * d_sb:]
    w_out = weights["w_out"]
    h2 = _rms_fwd([x1], [gains["mix_norm"]], name="mix_norm")
    p_sb = _mm(h2, w_in_sb, outs=(BF16,), name="proj_sb")

    def rope_qk(acc, _, cos, sin):
        return jnp.concatenate([_rotate(acc[:, :2 * d_dl], cos, sin, 1.0), acc[:, 2 * d_dl:]], axis=1)

    p_dl = _mm(h2, w_in_dl, lanes=(cos_t, sin_t), epilogue=rope_qk, name="proj_dl_rope")
    plan = ex.gather("ffn2" if exchanges else None)
    o_sb, carries, *got = _sb_fwd(p_sb, name="sb_fwd", comm=plan)
    ex.gathered("ffn2", got[0] if got else None, weights)
    o_dl, lse_dl = _dl_fwd(p_dl, name="dl_fwd")
    merged = _rms_fwd([o_sb, o_dl], [gains["sb_out_norm"], gains["dil_out_norm"]], name="out_norm")
    x2 = _mm(merged, w_out, res=x1, name="out_proj")
    (dx3, d_final, loss_wide), saved2 = _ffn_fwd(x2, gains["ffn2_norm"], weights, tag="ffn2", ex=ex,
                                                 head=(gains["final_norm"], target))
    loss_row = loss_wide[:, :LANES]

    dx2, d_ffn2_norm, dwg2, dwu2, dwd2 = _ffn_bwd(dx3, x2, gains["ffn2_norm"], weights, saved2, tag="ffn2", ex=ex)
    d_w_out = _mm(merged, dx2, ta=True, name="d_w_out")
    d_merged = _mm(dx2, w_out, tb=True, name="d_merged")
    (do_sb, do_dl), (d_sb_norm, d_dl_norm) = _rms_bwd(
        d_merged, [o_sb, o_dl], [gains["sb_out_norm"], gains["dil_out_norm"]], None, name="out_norm_bwd")
    plan = ex.send("ffn2", dict(ffn2_w_gate=dwg2, ffn2_w_up=dwu2, ffn2_w_down=dwd2))
    dq_sb, dk_sb, dv_sb, *got = _sb_bwd(p_sb, do_sb, carries, name="sb_bwd", comm=plan)
    ex.received("ffn2", got[0] if got else None)
    dq_dl, dk_dl, dv_dl = _dl_bwd(p_dl, o_dl, lse_dl, do_dl, name="dl_bwd")
    d_proj = _join_d_proj([dq_sb, dk_sb, dv_sb, dq_dl, dk_dl, dv_dl], (3, 4), cos_t, sin_t, name="d_proj")
    d_w_in = _mm(h2, d_proj, ta=True, name="d_w_in")
    dx1, d_mix_norm = _mm(d_proj, w_in, tb=True, extras=(x1, dx2), rows=(gains["mix_norm"],), row_sums=1,
                          epilogue=_rms_bwd_epilogue, name="dh_mix_norm_bwd")
    dx, d_ffn1_norm, dwg1, dwu1, dwd1 = _ffn_bwd(
        dx1, x, gains["ffn1_norm"], weights, saved1, tag="ffn1", ex=ex,
        rider=("mixer", dict(w_in=d_w_in, w_out=d_w_out)), spread=True)
    gain_grads = dict(ffn1_norm=d_ffn1_norm, mix_norm=d_mix_norm, sb_out_norm=d_sb_norm, dil_out_norm=d_dl_norm,
                      ffn2_norm=d_ffn2_norm, final_norm=d_final)
    weight_grads = dict(ffn1_w_gate=dwg1, ffn1_w_up=dwu1, ffn1_w_down=dwd1, w_in=d_w_in, w_out=d_w_out,
                        ffn2_w_gate=dwg2, ffn2_w_up=dwu2, ffn2_w_down=dwd2)
    return loss_row, dx, gain_grads, weight_grads


def _mesh_position():
    return lax.axis_index("x"), lax.axis_index("y"), lax.axis_index("c")


def _flip(coord, bit):
    return 1 - coord if bit else coord


RELATIONS = [(rx, ry, rc) for rx in (0, 1) for ry in (0, 1) for rc in (0, 1)][1:]


class _GatherPlan:
    def __init__(self, shards):
        n = len(shards)
        self.operands = list(shards)
        self.out_shapes = [jax.ShapeDtypeStruct((N_DEV,) + s.shape, s.dtype) for s in shards]
        self.scratch = [pltpu.SemaphoreType.DMA((n, 7)), pltpu.SemaphoreType.DMA((n, 7)),
                        pltpu.SemaphoreType.DMA((n,))]

    def _copies(self, in_refs, out_refs, sems):
        send_sems, recv_sems, local_sems = sems
        x, y, c = _mesh_position()
        me, sibling = (x, y, c), (x, y, 1 - c)
        chips = [(1 - x, y), (x, 1 - y), (1 - x, 1 - y)]
        plans = []
        for t, (x_ref, out_ref) in enumerate(zip(in_refs, out_refs)):
            def slot(px, py, pc, out_ref=out_ref):
                return out_ref.at[4 * px + 2 * py + pc]

            def copy(k, block, to, src=None, t=t, slot=slot):
                return pltpu.make_async_remote_copy(
                    src_ref=slot(*block) if src is None else src, dst_ref=slot(*block),
                    send_sem=send_sems.at[t, k], recv_sem=recv_sems.at[t, k],
                    device_id=to, device_id_type=pl.DeviceIdType.MESH)

            plans.append(dict(
                mine=pltpu.make_async_copy(x_ref, slot(*me), local_sems.at[t]),
                first=[copy(0, me, sibling, src=x_ref)]
                + [copy(1 + j, me, (*chip, c), src=x_ref) for j, chip in enumerate(chips)],
                over_ici=[copy(1 + j, (*chip, c), me) for j, chip in enumerate(chips)],
                passed=[copy(4 + j, (*chip, c), sibling) for j, chip in enumerate(chips)],
                from_sibling=[copy(0, sibling, me)] + [copy(4 + j, (*chip, 1 - c), me) for j, chip in enumerate(chips)]))
        return plans

    def start(self, in_refs, out_refs, sems):
        for p in self._copies(in_refs, out_refs, sems):
            p["mine"].start()
            for cp in p["first"]:
                cp.start()

    def finish(self, in_refs, out_refs, sems):
        plans = self._copies(in_refs, out_refs, sems)
        for p in plans:
            for arrived, onward in zip(p["over_ici"], p["passed"]):
                arrived.wait_recv()
                onward.start()
        for p in plans:
            for cp in p["from_sibling"]:
                cp.wait_recv()
            for cp in p["first"] + p["passed"]:
                cp.wait_send()
            p["mine"].wait()


class _Hosted:
    def __init__(self, plan, n_in, n_out, n_scratch):
        self.plan, self.n_in, self.n_out, self.n_scratch = plan, n_in, n_out, n_scratch
        self.operands = list(plan.operands) if plan else []
        self.out_shapes = list(plan.out_shapes) if plan else []
        self.scratch = list(plan.scratch) if plan else []
        self.in_specs = [pl.BlockSpec(memory_space=pl.ANY)] * len(self.operands)
        self.out_specs = [pl.BlockSpec(memory_space=pl.ANY)] * len(self.out_shapes)

    def semantics(self, sem):
        return sem if self.plan is None else ("arbitrary",) * len(sem)

    def _at(self, grid, last):
        hit = None
        for d, n in enumerate(grid):
            here = pl.program_id(d) == (n - 1 if last else 0)
            hit = here if hit is None else hit & here
        return hit

    def begin(self, refs, grid):
        if self.plan is None:
            return refs
        k_in, k_out = len(self.operands), len(self.out_shapes)
        ins, rest = refs[:self.n_in], refs[self.n_in:]
        c_in, rest = rest[:k_in], rest[k_in:]
        outs, rest = rest[:self.n_out], rest[self.n_out:]
        c_out, rest = rest[:k_out], rest[k_out:]
        scratch, sems = rest[:self.n_scratch], rest[self.n_scratch:]
        self._args = (c_in, c_out, sems)
        pl.when(self._at(grid, False))(lambda: self.plan.start(*self._args))
        return tuple(ins) + tuple(outs) + tuple(scratch)

    def end(self, grid):
        if self.plan is not None:
            pl.when(self._at(grid, True))(lambda: self.plan.finish(*self._args))


def _run_plan(plan, *, name):
    hosted = _Hosted(plan, 0, 0, 0)

    def body(*refs):
        hosted.begin(refs, (1,))
        hosted.end((1,))

    return pl.pallas_call(
        body, name=name, grid=(1,), out_shape=hosted.out_shapes,
        in_specs=hosted.in_specs, out_specs=hosted.out_specs, scratch_shapes=hosted.scratch,
        compiler_params=pltpu.CompilerParams(dimension_semantics=("arbitrary",)),
    )(*hosted.operands)


class _ExchangePlan:
    def __init__(self, packs):
        n = len(packs)
        self.operands = list(packs)
        self.out_shapes = [jax.ShapeDtypeStruct(p.shape, p.dtype) for p in packs]
        self.scratch = [pltpu.SemaphoreType.DMA((n, 7)), pltpu.SemaphoreType.DMA((n, 7)),
                        pltpu.SemaphoreType.DMA((n,))]

    def _copies(self, in_refs, out_refs, sems):
        send_sems, recv_sems, local_sems = sems
        x, y, c = _mesh_position()
        me = 4 * x + 2 * y + c
        copies = [pltpu.make_async_copy(i.at[me], o.at[me], local_sems.at[t])
                  for t, (i, o) in enumerate(zip(in_refs, out_refs))]
        for r, (rx, ry, rc) in enumerate(RELATIONS):
            px, py, pc = _flip(x, rx), _flip(y, ry), _flip(c, rc)
            peer = 4 * px + 2 * py + pc
            copies += [pltpu.make_async_remote_copy(
                src_ref=i.at[peer], dst_ref=o.at[me], send_sem=send_sems.at[t, r], recv_sem=recv_sems.at[t, r],
                device_id=(px, py, pc), device_id_type=pl.DeviceIdType.MESH)
                for t, (i, o) in enumerate(zip(in_refs, out_refs))]
        return copies

    def start(self, in_refs, out_refs, sems):
        for cp in self._copies(in_refs, out_refs, sems):
            cp.start()

    def finish(self, in_refs, out_refs, sems):
        for cp in self._copies(in_refs, out_refs, sems):
            cp.wait()


def _all_reduce_rows(v, *, name):
    R, C = v.shape

    def body(v_ref, out_ref, buf, send_sems, recv_sems):
        x, y, c = _mesh_position()
        me = 4 * x + 2 * y + c
        buf[me] = v_ref[...]
        copies = []
        for r, (rx, ry, rc) in enumerate(RELATIONS):
            cp = pltpu.make_async_remote_copy(
                src_ref=v_ref, dst_ref=buf.at[me], send_sem=send_sems.at[r], recv_sem=recv_sems.at[r],
                device_id=(_flip(x, rx), _flip(y, ry), _flip(c, rc)), device_id_type=pl.DeviceIdType.MESH)
            cp.start()
            copies.append(cp)
        for cp in copies:
            cp.wait()
        total = buf[0]
        for s in range(1, N_DEV):
            total = total + buf[s]
        out_ref[...] = total

    return pl.pallas_call(
        body, name=name,
        out_shape=jax.ShapeDtypeStruct((R, C), F32),
        in_specs=[pl.BlockSpec(memory_space=pltpu.VMEM)],
        out_specs=pl.BlockSpec(memory_space=pltpu.VMEM),
        scratch_shapes=[pltpu.VMEM((N_DEV, R, C), F32), pltpu.SemaphoreType.DMA((7,)), pltpu.SemaphoreType.DMA((7,))],
    )(v)


def _sum_slots(recv, *, name):
    _, R, C = recv.shape
    tr = _pick(R, (256, 208, 128, 64, 32, 16))

    def body(r_ref, o_ref):
        total = r_ref[0].astype(F32)
        for s in range(1, N_DEV):
            total = total + r_ref[s].astype(F32)
        o_ref[...] = total

    return pl.pallas_call(
        body, name=name, grid=(R // tr,),
        out_shape=jax.ShapeDtypeStruct((R, C), F32),
        in_specs=[pl.BlockSpec((N_DEV, tr, C), lambda i: (0, i, 0))],
        out_specs=pl.BlockSpec((tr, C), lambda i: (i, 0)),
        compiler_params=_cparams(("parallel",)),
    )(recv)


def _adamw(w, g, m, v, *, name):
    R, C = w.shape
    tr = _pick(R, (256, 128, 64, 32, 16, 8))

    def body(w_ref, g_ref, m_ref, v_ref, d_ref, nm_ref, nv_ref):
        g = g_ref[...]
        m_new = ADAM_B1 * m_ref[...] + (1.0 - ADAM_B1) * g
        v_new = ADAM_B2 * v_ref[...] + (1.0 - ADAM_B2) * (g * g)
        m_hat = m_new / (1.0 - ADAM_B1 ** ADAM_STEP)
        v_hat = v_new / (1.0 - ADAM_B2 ** ADAM_STEP)
        d_ref[...] = -ADAM_LR * (m_hat / (jnp.sqrt(v_hat) + ADAM_EPS) + ADAM_WD * w_ref[...])
        nm_ref[...] = m_new
        nv_ref[...] = v_new

    spec = pl.BlockSpec((tr, C), lambda i: (i, 0))
    return pl.pallas_call(
        body, name=name, grid=(R // tr,),
        out_shape=[jax.ShapeDtypeStruct((R, C), F32)] * 3,
        in_specs=[spec] * 4, out_specs=[spec] * 3,
        compiler_params=_cparams(("parallel",)),
    )(w, g, m, v)


WEIGHT_NAMES = ["ffn1_norm", "ffn1_w_gate", "ffn1_w_up", "ffn1_w_down", "mix_norm", "w_in", "sb_out_norm",
                "dil_out_norm", "w_out", "ffn2_norm", "ffn2_w_gate", "ffn2_w_up", "ffn2_w_down", "final_norm"]
GAIN_NAMES = ["ffn1_norm", "mix_norm", "sb_out_norm", "dil_out_norm", "ffn2_norm", "final_norm"]
COL_SHARDED = ["ffn1_w_gate", "ffn1_w_up", "ffn2_w_gate", "ffn2_w_up", "w_in"]
ROW_SHARDED = ["ffn1_w_down", "ffn2_w_down", "w_out"]
GROUPS = {"mixer": (["w_in"], ["w_out"]),
          "ffn2": (["ffn2_w_gate", "ffn2_w_up"], ["ffn2_w_down"])}
for _ffn in ("ffn1", "ffn2"):
    GROUPS.update({f"{_ffn}_w_gate": ([f"{_ffn}_w_gate"], []), f"{_ffn}_w_up": ([f"{_ffn}_w_up"], []),
                   f"{_ffn}_w_down": ([], [f"{_ffn}_w_down"])})


class _Exchanges:
    def __init__(self, params):
        self.params = params
        self.grads = {}

    def gather(self, group):
        if group is None:
            return None
        cols, rows = GROUPS[group]
        packs = []
        if cols:
            packs.append(jnp.concatenate([self.params[n] for n in cols], axis=1).astype(BF16))
        if rows:
            packs.append(jnp.concatenate([self.params[n] for n in rows], axis=0).astype(BF16))
        return _GatherPlan(packs)

    def gathered(self, group, got, weights):
        if group is None:
            return
        cols, rows = GROUPS[group]
        got = list(got)
        if cols:
            col_all, off = got.pop(0), 0
            for n in cols:
                w = self.params[n].shape[1]
                piece = col_all[:, :, off:off + w]
                weights[n] = jnp.transpose(piece, (1, 0, 2)).reshape(piece.shape[1], N_DEV * w)
                off += w
        if rows:
            row_all, off = got.pop(0), 0
            for n in rows:
                r = self.params[n].shape[0]
                weights[n] = row_all[:, off:off + r, :].reshape(N_DEV * r, row_all.shape[2])
                off += r

    def send(self, group, grads):
        if group is None:
            return None
        cols, rows = GROUPS[group]
        packs = []
        if cols:
            chunks = [jnp.transpose(grads[n].reshape(grads[n].shape[0], N_DEV, self.params[n].shape[1]), (1, 0, 2))
                      for n in cols]
            packs.append(jnp.concatenate(chunks, axis=2).astype(BF16))
        if rows:
            chunks = [grads[n].reshape(N_DEV, self.params[n].shape[0], grads[n].shape[1]) for n in rows]
            packs.append(jnp.concatenate(chunks, axis=1).astype(BF16))
        return _ExchangePlan(packs)

    def received(self, group, got):
        if group is None:
            return
        cols, rows = GROUPS[group]
        got = list(got)
        if cols:
            col_grad, off = _sum_slots(got.pop(0), name=f"sum_col_grads_{group}"), 0
            for n in cols:
                w = self.params[n].shape[1]
                self.grads[n] = col_grad[:, off:off + w]
                off += w
        if rows:
            row_grad, off = _sum_slots(got.pop(0), name=f"sum_row_grads_{group}"), 0
            for n in rows:
                r = self.params[n].shape[0]
                self.grads[n] = row_grad[off:off + r, :]
                off += r


def _step(x, target, params, moments_m, moments_v):
    ex = _Exchanges(params)
    weights = {}
    ex.gathered("ffn1_w_gate", _run_plan(ex.gather("ffn1_w_gate"), name="gather_ffn1_gate"), weights)
    gains = {n: params[n] for n in GAIN_NAMES}
    loss_row, grad_x, gain_grads, _ = _local_step(x, target, gains, weights, ex)
    grads = ex.grads

    rows = [gain_grads[n].reshape(-1, LANES) for n in GAIN_NAMES] + [loss_row]
    small = jnp.concatenate(rows, axis=0)
    pad = (-small.shape[0]) % 8
    small = jnp.pad(small, ((0, pad), (0, 0)))
    small = _all_reduce_rows(small, name="reduce_gains_loss")
    off = 0
    for n in GAIN_NAMES:
        r = gain_grads[n].shape[1] // LANES
        grads[n] = small[off:off + r].reshape(1, -1)
        off += r
    loss = small[off, 0]

    delta, new_m, new_v = {}, {}, {}
    for n in WEIGHT_NAMES:
        delta[n], new_m[n], new_v[n] = _adamw(params[n], grads[n], moments_m[n], moments_v[n], name=f"adamw_{n}")
    return loss, grad_x, grads, delta, new_m, new_v


def kernel(x, ffn1_norm, ffn1_w_gate, ffn1_w_up, ffn1_w_down, mix_norm, w_in, sb_out_norm, dil_out_norm, w_out, ffn2_norm, ffn2_w_gate, ffn2_w_up, ffn2_w_down, final_norm, loss_target, m_ffn1_norm, m_ffn1_w_gate, m_ffn1_w_up, m_ffn1_w_down, m_mix_norm, m_w_in, m_sb_out_norm, m_dil_out_norm, m_w_out, m_ffn2_norm, m_ffn2_w_gate, m_ffn2_w_up, m_ffn2_w_down, m_final_norm, v_ffn1_norm, v_ffn1_w_gate, v_ffn1_w_up, v_ffn1_w_down, v_mix_norm, v_w_in, v_sb_out_norm, v_dil_out_norm, v_w_out, v_ffn2_norm, v_ffn2_w_gate, v_ffn2_w_up, v_ffn2_w_down, v_final_norm):
    given = dict(locals())
    shapes = {n: given[n].shape for n in WEIGHT_NAMES}

    def as2d(a):
        return a.reshape(1, -1) if a.ndim == 1 else a.reshape(a.shape[-2], a.shape[-1])

    params = {n: as2d(given[n]) for n in WEIGHT_NAMES}
    moments_m = {n: as2d(given["m_" + n]) for n in WEIGHT_NAMES}
    moments_v = {n: as2d(given["v_" + n]) for n in WEIGHT_NAMES}
    loss, grad_x, grads, delta, new_m, new_v = _step(x[0], loss_target[0], params, moments_m, moments_v)
    back = lambda d: [d[n].reshape(shapes[n]) for n in WEIGHT_NAMES]
    return (loss, grad_x[None], *back(grads), *back(delta), *back(new_m), *back(new_v))
```

```python
import functools

import jax
import jax.numpy as jnp
from jax import lax
from jax.experimental import pallas as pl
from jax.experimental.pallas import tpu as pltpu

F32 = jnp.float32
BF16 = jnp.bfloat16
GRAD_WIRE = jnp.bfloat16

N_DEV = 8
HEAD_DIM = 64
LANES = 128
DILATED_PATTERNS = ((128, 1), (512, 4), (2048, 16))
DIL_BLOCK = 128
DIL_SUPER = 2048
DIL_UNROLL = 8
SB_TILE = 256
SB_UNROLL = 4
SB_DEAD = 90.0
SB_UNSEEN = -1e30
ROPE_THETA = 10000.0
RMS_EPS = 1e-6
HALF_STEP = 0.5
ADAM_LR = 0.001
ADAM_B1 = 0.9
ADAM_B2 = 0.999
ADAM_EPS = 1e-08
ADAM_WD = 0.01
ADAM_STEP = 10
NEG_BIG = -1e30
VMEM_CAP_MB = 60


def _pick(n, prefs):
    for p in prefs:
        if n % p == 0:
            return p
    return n


MM_MAX_TILE = 1536


def _largest_tile(n, cap):
    if n <= cap:
        return n
    for t in range(cap - cap % LANES, 0, -LANES):
        if n % t == 0:
            return t
    return n


def _cparams(sem=None, vmem_mb=48):
    return pltpu.CompilerParams(dimension_semantics=sem, vmem_limit_bytes=min(vmem_mb, VMEM_CAP_MB) * 1024 * 1024)


def _nbytes(shape, dtype):
    n = 1
    for s in shape:
        n *= s
    return n * jnp.dtype(dtype).itemsize


def _mm(a, b, *, name, ta=False, tb=False, outs=(F32,), res=None, alpha=1.0, extras=(), epilogue=None,
        tm=None, tn=None, tk=None, comm=None, rows=(), lanes=(), row_sums=0):
    if ta:
        K, M = a.shape
    else:
        M, K = a.shape
    if tb:
        N, Kb = b.shape
    else:
        Kb, N = b.shape
    assert K == Kb, (a.shape, b.shape, ta, tb)
    tm = tm or (_largest_tile(M, MM_MAX_TILE) if ta else _pick(M, (512, 256, 128)))
    tn = tn or _largest_tile(N, MM_MAX_TILE)
    tk = tk or (K if K <= 3072 else _pick(K, (2048, 1024, 512, 256, 128)))
    nk = K // tk
    a_spec = pl.BlockSpec((tk, tm), lambda i, j, k: (k, i)) if ta else pl.BlockSpec((tm, tk), lambda i, j, k: (i, k))
    b_spec = pl.BlockSpec((tn, tk), lambda i, j, k: (j, k)) if tb else pl.BlockSpec((tk, tn), lambda i, j, k: (k, j))
    mn_spec = pl.BlockSpec((tm, tn), lambda i, j, k: (i, j))
    dims = (((0 if ta else 1,), (1 if tb else 0,)), ((), ()))
    row_spec = pl.BlockSpec((1, tn), lambda i, j, k: (0, j))
    lane_spec = pl.BlockSpec((tm, LANES), lambda i, j, k: (i, 0))
    n_extra = len(extras) + (1 if res is not None else 0) + len(rows) + len(lanes)
    n_mn = len(outs)
    n_out = n_mn + row_sums
    assert row_sums == 0 or tn == N
    grid = (M // tm, N // tn, nk)
    hosted = _Hosted(comm, n_in=2 + n_extra, n_out=n_out, n_scratch=1 if nk > 1 else 0)

    def body(*refs):
        a_ref, b_ref = refs[0], refs[1]
        in_refs = refs[2:2 + n_extra]
        refs = hosted.begin(refs, grid)
        out_refs = refs[2 + n_extra:2 + n_extra + n_out]
        prod = lax.dot_general(a_ref[...].astype(BF16), b_ref[...].astype(BF16), dims, preferred_element_type=F32)

        def finish(acc):
            blocks = [r[...] for r in in_refs]
            if res is not None:
                r_blk, blocks = blocks[0], blocks[1:]
            else:
                r_blk = None
            if epilogue is None:
                val = acc * alpha
                if r_blk is not None:
                    val = val + r_blk
                vals = (val,)
            else:
                vals = epilogue(acc, r_blk, *blocks)
                vals = vals if isinstance(vals, (tuple, list)) else (vals,)
            for o_ref, v in zip(out_refs[:n_mn], vals[:n_mn]):
                o_ref[...] = v.astype(o_ref.dtype)
            first_rows = pl.program_id(0) == 0
            for o_ref, part in zip(out_refs[n_mn:], vals[n_mn:]):
                @pl.when(first_rows)
                def _(o_ref=o_ref, part=part):
                    o_ref[...] = part

                @pl.when(jnp.logical_not(first_rows))
                def _(o_ref=o_ref, part=part):
                    o_ref[...] += part

        if nk == 1:
            finish(prod)
        else:
            acc_ref = refs[2 + n_extra + n_out]
            k = pl.program_id(2)

            @pl.when(k == 0)
            def _():
                acc_ref[...] = prod

            @pl.when(k > 0)
            def _():
                acc_ref[...] += prod

            @pl.when(k == nk - 1)
            def _():
                finish(acc_ref[...])

        hosted.end(grid)

    mn_operands = ([res] if res is not None else []) + list(extras)
    operands = [a, b] + mn_operands + list(rows) + list(lanes)
    in_specs = [a_spec, b_spec] + [mn_spec] * len(mn_operands) + [row_spec] * len(rows) + [lane_spec] * len(lanes)
    est = 2 * (_nbytes((tm, tk), a.dtype) + _nbytes((tk, tn), b.dtype))
    est += 2 * sum(_nbytes((tm, tn), o.dtype) for o in mn_operands)
    est += 2 * sum(_nbytes((tm, tn), d) for d in outs) + 2 * _nbytes((tm, tn), F32)
    semantics = ("parallel", "parallel", "arbitrary") if row_sums == 0 else ("arbitrary",) * 3
    result = pl.pallas_call(
        body, name=name, grid=grid,
        out_shape=[jax.ShapeDtypeStruct((M, N), d) for d in outs]
        + [jax.ShapeDtypeStruct((1, N), F32)] * row_sums + hosted.out_shapes,
        in_specs=in_specs + hosted.in_specs,
        out_specs=[mn_spec] * n_mn + [row_spec] * row_sums + hosted.out_specs,
        scratch_shapes=([pltpu.VMEM((tm, tn), F32)] if nk > 1 else []) + hosted.scratch,
        compiler_params=_cparams(hosted.semantics(semantics), vmem_mb=max(32, 2 * est // (1024 * 1024))),
    )(*operands, *hosted.operands)
    own, got = result[:n_out], list(result[n_out:])
    own = own[0] if n_out == 1 else own
    return own if comm is None else (own, got)


def _rms_hat(x):
    r = lax.rsqrt(jnp.mean(x * x, axis=-1, keepdims=True) + RMS_EPS)
    return x * r, r


def _rms_fwd(xs, gains, *, name):
    S = xs[0].shape[0]
    widths = [x.shape[1] for x in xs]
    tm = _pick(S, (512, 256, 128))
    n = len(xs)

    def body(*refs):
        o_ref = refs[2 * n]
        off = 0
        for i in range(n):
            xh, _ = _rms_hat(refs[i][...])
            o_ref[:, off:off + widths[i]] = (xh * refs[n + i][...]).astype(o_ref.dtype)
            off += widths[i]

    return pl.pallas_call(
        body, name=name, grid=(S // tm,),
        out_shape=jax.ShapeDtypeStruct((S, sum(widths)), BF16),
        in_specs=[pl.BlockSpec((tm, w), lambda i: (i, 0)) for w in widths]
        + [pl.BlockSpec((1, w), lambda i: (0, 0)) for w in widths],
        out_specs=pl.BlockSpec((tm, sum(widths)), lambda i: (i, 0)),
        compiler_params=_cparams(("parallel",)),
    )(*xs, *gains)


def _rms_bwd(dh, xs, gains, res, *, name):
    S = xs[0].shape[0]
    widths = [x.shape[1] for x in xs]
    tm = _pick(S, (512, 256, 128))
    n = len(xs)
    has_res = res is not None

    def body(*refs):
        dh_ref = refs[0]
        x_refs = refs[1:1 + n]
        g_refs = refs[1 + n:1 + 2 * n]
        r_ref = refs[1 + 2 * n] if has_res else None
        base = 1 + 2 * n + (1 if has_res else 0)
        dx_refs = refs[base:base + n]
        dg_refs = refs[base + n:base + 2 * n]
        first = pl.program_id(0) == 0
        off = 0
        for i in range(n):
            x = x_refs[i][...]
            xh, r = _rms_hat(x)
            d = dh_ref[:, off:off + widths[i]]
            dxh = d * g_refs[i][...]
            dx = r * (dxh - xh * jnp.mean(dxh * xh, axis=-1, keepdims=True))
            if has_res:
                dx = dx + r_ref[...]
            dx_refs[i][...] = dx
            part = jnp.sum(d * xh, axis=0, keepdims=True)

            @pl.when(first)
            def _(i=i, part=part):
                dg_refs[i][...] = part

            @pl.when(jnp.logical_not(first))
            def _(i=i, part=part):
                dg_refs[i][...] += part

            off += widths[i]

    in_specs = [pl.BlockSpec((tm, sum(widths)), lambda i: (i, 0))]
    in_specs += [pl.BlockSpec((tm, w), lambda i: (i, 0)) for w in widths]
    in_specs += [pl.BlockSpec((1, w), lambda i: (0, 0)) for w in widths]
    operands = [dh, *xs, *gains]
    if has_res:
        in_specs.append(pl.BlockSpec((tm, widths[0]), lambda i: (i, 0)))
        operands.append(res)
    out = pl.pallas_call(
        body, name=name, grid=(S // tm,),
        out_shape=[jax.ShapeDtypeStruct((S, w), F32) for w in widths] + [jax.ShapeDtypeStruct((1, w), F32) for w in widths],
        in_specs=in_specs,
        out_specs=[pl.BlockSpec((tm, w), lambda i: (i, 0)) for w in widths]
        + [pl.BlockSpec((1, w), lambda i: (0, 0)) for w in widths],
        compiler_params=_cparams(("arbitrary",)),
    )(*operands)
    return out[:n], out[n:]


def _sigmoid(g):
    return 1.0 / (1.0 + jnp.exp(-g))


def _ride(result, plan):
    return result if plan is not None else (result, None)


def _residual_then_norm(alpha):
    def epilogue(acc, res, gain):
        y = res + alpha * acc
        return y, _rms_hat(y)[0] * gain
    return epilogue


def _ffn_fwd(x, gain, w, *, tag, ex, riders=(None, None, None), head=None, h=None, next_gain=None):
    if h is None:
        h = _rms_fwd([x], [gain], name=f"{tag}_norm")
    plan = ex.gather(riders[0])
    g, got = _ride(_mm(h, w[f"{tag}_w_gate"], outs=(BF16,), name=f"{tag}_gate", comm=plan), plan)
    ex.gathered(riders[0], got, w)

    def act(acc, _, g_blk):
        gf = g_blk.astype(F32)
        return acc, gf * _sigmoid(gf) * acc

    plan = ex.gather(riders[1])
    (u, a), got = _ride(_mm(h, w[f"{tag}_w_up"], outs=(BF16, BF16), extras=(g,), epilogue=act, name=f"{tag}_up_act",
                            comm=plan), plan)
    ex.gathered(riders[1], got, w)
    plan = ex.gather(riders[2])
    if head is None and next_gain is None:
        y, got = _ride(_mm(a, w[f"{tag}_w_down"], res=x, alpha=HALF_STEP, name=f"{tag}_down", comm=plan), plan)
    elif head is None:
        y, got = _ride(_mm(a, w[f"{tag}_w_down"], res=x, rows=(next_gain,), outs=(F32, BF16),
                           epilogue=_residual_then_norm(HALF_STEP), name=f"{tag}_down_norm", comm=plan), plan)
    else:
        final_gain, target = head
        y, got = _ride(_mm(a, w[f"{tag}_w_down"], res=x, extras=(target,), rows=(final_gain,), row_sums=2,
                           epilogue=_loss_head_epilogue, name=f"{tag}_down_loss", comm=plan), plan)
    ex.gathered(riders[2], got, w)
    return y, (h, g, u, a)


def _loss_head_epilogue(acc, x_in, target, gain):
    xh, r = _rms_hat(x_in + HALF_STEP * acc)
    err = xh * gain - target
    dy = err * (1.0 / acc.shape[1])
    dxh = dy * gain
    dx = r * (dxh - xh * jnp.mean(dxh * xh, axis=-1, keepdims=True))
    loss = 0.5 * jnp.sum(jnp.mean(err * err, axis=-1, keepdims=True), axis=0, keepdims=True)
    return dx, jnp.sum(dy * xh, axis=0, keepdims=True), jnp.zeros_like(gain) + loss


def _rms_bwd_epilogue(acc, dh_so_far, x, dres, gain):
    dh = acc if dh_so_far is None else acc + dh_so_far
    xh, r = _rms_hat(x)
    dxh = dh * gain
    dx = r * (dxh - xh * jnp.mean(dxh * xh, axis=-1, keepdims=True)) + dres
    return dx, jnp.sum(dh * xh, axis=0, keepdims=True)


def _ffn_bwd(dout, x, gain, w, saved, *, tag, ex, rider=(None, None), spread=False):
    h, g, u, a = saved
    wg, wu, wd = (w[f"{tag}_w_{n}"] for n in ("gate", "up", "down"))

    def act_bwd(acc, _, g_blk, u_blk):
        gf, uf = g_blk.astype(F32), u_blk.astype(F32)
        da = acc * HALF_STEP
        sig = _sigmoid(gf)
        silu = gf * sig
        return da * uf * (sig * (1.0 + gf * (1.0 - sig))), da * silu

    def carrying(group, grad, call):
        group = group if spread else None
        plan = ex.send(group, {group: grad})
        out, got = _ride(call(plan), plan)
        ex.received(group, got)
        return out

    plan = ex.send(*rider)
    (dg, du), got = _ride(_mm(dout, wd, tb=True, outs=(BF16, BF16), extras=(g, u), epilogue=act_bwd,
                              name=f"{tag}_bwd_act", comm=plan), plan)
    ex.received(rider[0], got)
    dwg = _mm(h, dg, ta=True, outs=(GRAD_WIRE,), name=f"{tag}_dwg")
    dwu = carrying(f"{tag}_w_gate", dwg, lambda plan: _mm(h, du, ta=True, outs=(GRAD_WIRE,), name=f"{tag}_dwu", comm=plan))
    dwd = carrying(f"{tag}_w_up", dwu,
                   lambda plan: _mm(a, dout, ta=True, outs=(GRAD_WIRE,), alpha=HALF_STEP, name=f"{tag}_dwd", comm=plan))
    dh = carrying(f"{tag}_w_down", dwd, lambda plan: _mm(dg, wg, tb=True, name=f"{tag}_dh_gate", comm=plan))
    dx, dgain = _mm(du, wu, tb=True, res=dh, extras=(x, dout), rows=(gain,), row_sums=1, epilogue=_rms_bwd_epilogue,
                    name=f"{tag}_dh_up_norm_bwd")
    return dx, dgain, dwg, dwu, dwd


def _rope_tables(S):
    half = HEAD_DIM // 2
    inv_freq = ROPE_THETA ** (-jnp.arange(half, dtype=F32) / half)
    ang = jnp.arange(S, dtype=F32)[:, None] * inv_freq[None, :]
    cos, sin = jnp.cos(ang), jnp.sin(ang)
    reps = LANES // HEAD_DIM
    cos_t = jnp.tile(jnp.concatenate([cos, cos], axis=1), (1, reps))
    sin_t = jnp.tile(jnp.concatenate([-sin, sin], axis=1), (1, reps))
    return cos_t, sin_t


def _rotate(v, cos, sin, sign):
    half = HEAD_DIM // 2
    groups = []
    for g in range(v.shape[1] // LANES):
        t = v[:, g * LANES:(g + 1) * LANES]
        lane = lax.broadcasted_iota(jnp.int32, t.shape, 1)
        swapped = jnp.where(lane % HEAD_DIM < half, pltpu.roll(t, LANES - half, axis=1), pltpu.roll(t, half, axis=1))
        groups.append(t * cos + swapped * (sin * sign))
    return groups[0] if len(groups) == 1 else jnp.concatenate(groups, axis=1)


def _join_d_proj(pieces, rotated, cos_t, sin_t, *, name):
    S = pieces[0].shape[0]
    widths = [p.shape[1] for p in pieces]
    tm = _pick(S, (256, 128))
    n = len(pieces)

    def body(*refs):
        c_ref, s_ref, o_ref = refs[n], refs[n + 1], refs[n + 2]
        off = 0
        for i in range(n):
            v = refs[i][...]
            if i in rotated:
                v = _rotate(v, c_ref[...], s_ref[...], -1.0)
            o_ref[:, off:off + widths[i]] = v.astype(o_ref.dtype)
            off += widths[i]

    return pl.pallas_call(
        body, name=name, grid=(S // tm,),
        out_shape=jax.ShapeDtypeStruct((S, sum(widths)), BF16),
        in_specs=[pl.BlockSpec((tm, w), lambda i: (i, 0)) for w in widths]
        + [pl.BlockSpec((tm, LANES), lambda i: (i, 0))] * 2,
        out_specs=pl.BlockSpec((tm, sum(widths)), lambda i: (i, 0)),
        compiler_params=_cparams(("parallel",)),
    )(*pieces, cos_t, sin_t)


def _head_masks(shape):
    lane = lax.broadcasted_iota(jnp.int32, shape, 1)
    return [(lane >= HEAD_DIM * h) & (lane < HEAD_DIM * (h + 1)) for h in range(LANES // HEAD_DIM)]


def _sb_scores(q2, k_j):
    z = lax.dot_general(q2, k_j, (((1,), (1,)), ((), ())), preferred_element_type=F32)
    sign_bit = jnp.int32(-2 ** 31)
    minus_abs = lax.bitcast_convert_type(lax.bitcast_convert_type(z, jnp.int32) | sign_bit, F32)
    softplus = jnp.maximum(z, 0.0) + jnp.log(1.0 + jnp.exp(minus_abs))
    return z - softplus, softplus


def _sb_stack_heads(t, scale=None):
    parts = [jnp.where(hm, t, jnp.zeros_like(t)) for hm in _head_masks(t.shape)]
    t2 = jnp.concatenate(parts, axis=0)
    if scale is not None:
        t2 = (t2.astype(F32) * scale).astype(t2.dtype)
    return t2


def _sb_unstack_heads(t2):
    T = t2.shape[0] // 2
    masks = _head_masks((T, LANES))
    return jnp.where(masks[0], t2[:T], t2[T:])


def _sb_causal(T):
    row = lax.broadcasted_iota(jnp.int32, (2 * T, T), 0)
    col = lax.broadcasted_iota(jnp.int32, (2 * T, T), 1)
    return col < jnp.where(row >= T, row - T, row)


def _sb_triangle(T, later):
    row = lax.broadcasted_iota(jnp.int32, (T, T), 0)
    col = lax.broadcasted_iota(jnp.int32, (T, T), 1)
    return ((row > col) if later else (row < col)).astype(BF16)


def _sb_fwd(p_sb, *, name, comm=None):
    S = p_sb.shape[0]
    W = p_sb.shape[1] // 3
    npair = W // LANES
    T = SB_TILE
    n_tiles = S // T
    assert n_tiles <= HEAD_DIM
    scale = HEAD_DIM ** -0.5

    grid = (npair, n_tiles)
    hosted = _Hosted(comm, n_in=5, n_out=2, n_scratch=0)

    def body(*refs):
        q_ref, k_ref, v_ref, causal_ref, later_ref, o_ref, c_ref = hosted.begin(refs, grid)
        I = pl.program_id(1)
        lane = lax.broadcasted_iota(jnp.int32, (T, LANES), 1)
        causal = causal_ref[...]
        later_than = later_ref[...]
        q2 = _sb_stack_heads(q_ref[...], scale)

        def scores(J, diag):
            off = pl.multiple_of(J * T, T)
            log_beta, stay = _sb_scores(q2, k_ref[pl.ds(off, T), :])
            if diag:
                stay = stay * causal
            local = jnp.dot(stay.astype(BF16), later_than, preferred_element_type=F32)
            return log_beta, local, jnp.sum(stay, axis=1, keepdims=True), v_ref[pl.ds(off, T), :]

        def weigh(J, sc, gone, acc, carr, diag):
            log_beta, local, _, v_j = sc
            w = jnp.exp((log_beta - gone) - local)
            if diag:
                w = w * causal
            acc = acc + jnp.dot(w.astype(BF16), v_j, preferred_element_type=F32)
            carr = jnp.where(lane == J, -gone[:T], carr)
            carr = jnp.where(lane == HEAD_DIM + J, -gone[T:], carr)
            return acc, carr

        def tiles(J, count, state, diag):
            gone, acc, carr, _ = state
            scs = [scores(J - u, diag and u == 0) for u in range(count)]
            for u, sc in enumerate(scs):
                acc, carr = weigh(J - u, sc, gone, acc, carr, diag and u == 0)
                gone = gone + sc[2]
            return gone, acc, carr, jnp.min(gone)

        U = SB_UNROLL
        alive = lambda st: st[3] < SB_DEAD
        state = (jnp.zeros((2 * T, 1), F32), jnp.zeros((2 * T, LANES), F32),
                 jnp.full((T, LANES), SB_UNSEEN, F32), jnp.zeros((), F32))
        state = lax.cond(I > 0, lambda st: tiles(I, 2, st, True), lambda st: tiles(I, 1, st, True), state)
        rest = jnp.maximum(I - 1, 0)
        singles = jnp.where(rest > 0, (rest - 1) % U + 1, 0)
        _, state = lax.while_loop(lambda c: (c[0] < singles) & alive(c[1]),
                                  lambda c: (c[0] + 1, tiles(I - 2 - c[0], 1, c[1], False)), (jnp.int32(0), state))
        blocks = (rest - singles) // U
        _, state = lax.while_loop(lambda c: (c[0] < blocks) & alive(c[1]),
                                  lambda c: (c[0] + 1, tiles(I - 2 - singles - U * c[0], U, c[1], False)),
                                  (jnp.int32(0), state))
        _, acc, carr, _ = state
        o_ref[...] = _sb_unstack_heads(acc)
        c_ref[...] = carr
        hosted.end(grid)

    blk = lambda I_off: pl.BlockSpec((T, LANES), lambda p, I: (I, I_off + p))
    full = lambda off: pl.BlockSpec((S, LANES), lambda p, I: (0, off + p))
    const = lambda rows: pl.BlockSpec((rows, T), lambda p, I: (0, 0))
    o, carries, *got = pl.pallas_call(
        body, name=name, grid=grid,
        out_shape=[jax.ShapeDtypeStruct((S, W), F32), jax.ShapeDtypeStruct((S, W), F32)] + hosted.out_shapes,
        in_specs=[blk(0), full(npair), full(2 * npair), const(2 * T), const(T)] + hosted.in_specs,
        out_specs=[blk(0), blk(0)] + hosted.out_specs,
        scratch_shapes=hosted.scratch,
        compiler_params=_cparams(hosted.semantics(("parallel", "arbitrary"))),
    )(p_sb, p_sb, p_sb, _sb_causal(T).astype(F32), _sb_triangle(T, True), *hosted.operands)
    return (o, carries) if comm is None else (o, carries, got)


def _sb_bwd(p_sb, do, carries, *, name, comm=None):
    S = p_sb.shape[0]
    W = p_sb.shape[1] // 3
    npair = W // LANES
    T = SB_TILE
    n_tiles = S // T
    scale = HEAD_DIM ** -0.5

    grid = (npair, n_tiles)
    hosted = _Hosted(comm, n_in=8, n_out=3, n_scratch=0)

    def body(*refs):
        (q_ref, k_ref, v_ref, do_ref, c_ref, causal_ref, later_ref, earlier_ref,
         dq_ref, dk_ref, dv_ref) = hosted.begin(refs, grid)
        I = pl.program_id(1)

        @pl.when(I == 0)
        def _():
            dk_ref[...] = jnp.zeros_like(dk_ref)
            dv_ref[...] = jnp.zeros_like(dv_ref)

        lane = lax.broadcasted_iota(jnp.int32, (T, LANES), 1)
        causal = causal_ref[...]
        later_than = later_ref[...]
        earlier_than = earlier_ref[...]
        q2 = _sb_stack_heads(q_ref[...], scale)
        do2 = _sb_stack_heads(do_ref[...].astype(BF16))
        carr = c_ref[...]
        tn_dims = (((0,), (0,)), ((), ()))

        def chain(J, diag):
            off = pl.multiple_of(J * T, T)
            k_j = k_ref[pl.ds(off, T), :]
            v_j = v_ref[pl.ds(off, T), :]
            log_beta, stay = _sb_scores(q2, k_j)
            if diag:
                stay = stay * causal
            lc = jnp.concatenate(
                [jnp.sum(jnp.where(lane == HEAD_DIM * h + J, carr, 0.0), axis=1, keepdims=True) for h in range(2)],
                axis=0)
            w = jnp.exp((log_beta + lc) - jnp.dot(stay.astype(BF16), later_than, preferred_element_type=F32))
            if diag:
                w = w * causal
            dw = lax.dot_general(do2, v_j, (((1,), (1,)), ((), ())), preferred_element_type=F32)
            e = w * dw
            local = jnp.dot(e.astype(BF16), earlier_than, preferred_element_type=F32)
            return off, k_j, w, e, local, jnp.exp(log_beta), jnp.sum(e, axis=1, keepdims=True)

        def finish(ch, ec, dq_acc, diag):
            off, k_j, w, e, local, beta, _ = ch
            e_before = local + ec
            dz = e - beta * (e + e_before)
            if diag:
                dz = dz * causal
            dzb = dz.astype(BF16)
            dq_acc = dq_acc + jnp.dot(dzb, k_j, preferred_element_type=F32)
            dk_ref[pl.ds(off, T), :] += lax.dot_general(dzb, q2, tn_dims, preferred_element_type=F32)
            dv_ref[pl.ds(off, T), :] += lax.dot_general(w.astype(BF16), do2, tn_dims, preferred_element_type=F32)
            return dq_acc

        def tiles(J, count, state, diag):
            ec, dq_acc = state
            chains = [chain(J + u, diag and u == count - 1) for u in range(count)]
            for u, ch in enumerate(chains):
                dq_acc = finish(ch, ec, dq_acc, diag and u == count - 1)
                ec = ec + ch[6]
            return ec, dq_acc

        lane_row = lax.broadcasted_iota(jnp.int32, (1, LANES), 1)
        reached = (jnp.max(carr, axis=0, keepdims=True) > 0.5 * SB_UNSEEN) & (lane_row < HEAD_DIM)
        first = jnp.min(jnp.where(reached, lane_row.astype(F32), float(n_tiles))).astype(jnp.int32)
        U = SB_UNROLL
        count = I - first
        rest = jnp.maximum(count - 1, 0)
        state = (jnp.zeros((2 * T, 1), F32), jnp.zeros((2 * T, LANES), F32))
        state = lax.fori_loop(0, rest // U, lambda jj, st: tiles(first + U * jj, U, st, False), state)
        state = lax.fori_loop(0, rest % U, lambda r, st: tiles(I - 1 - rest % U + r, 1, st, False), state)
        _, dq_acc = lax.cond(count > 0, lambda st: tiles(I - 1, 2, st, True), lambda st: tiles(I, 1, st, True), state)
        dq_ref[...] = _sb_unstack_heads(dq_acc) * scale
        hosted.end(grid)

    blk = lambda src_off: pl.BlockSpec((T, LANES), lambda p, I: (I, src_off + p))
    full = lambda off: pl.BlockSpec((S, LANES), lambda p, I: (0, off + p))
    const = lambda rows: pl.BlockSpec((rows, T), lambda p, I: (0, 0))
    dq, dk, dv, *got = pl.pallas_call(
        body, name=name, grid=grid,
        out_shape=[jax.ShapeDtypeStruct((S, W), F32)] * 3 + hosted.out_shapes,
        in_specs=[blk(0), full(npair), full(2 * npair), blk(0), blk(0), const(2 * T), const(T), const(T)]
        + hosted.in_specs,
        out_specs=[blk(0), full(0), full(0)] + hosted.out_specs,
        scratch_shapes=hosted.scratch,
        compiler_params=_cparams(hosted.semantics(("parallel", "arbitrary"))),
    )(p_sb, p_sb, p_sb, do, carries, _sb_causal(T).astype(F32), _sb_triangle(T, True), _sb_triangle(T, False),
      *hosted.operands)
    return (dq, dk, dv) if comm is None else (dq, dk, dv, got)


def _dil_blocks(b, body_fn):
    for pi, (window, dil) in enumerate(DILATED_PATTERNS):
        assert window // dil == DIL_BLOCK
        nblk = DIL_SUPER // (DIL_BLOCK * dil)
        assert (dil * nblk) % DIL_UNROLL == 0

        def group(g, _, pi=pi, dil=dil, nblk=nblk):
            for u in range(DIL_UNROLL):
                t = g * DIL_UNROLL + u
                n = t % nblk
                body_fn(pi, dil, t // nblk, n, b * nblk + n)
            return 0

        lax.fori_loop(0, dil * nblk // DIL_UNROLL, group, 0)


def _dil_rows(start, size, dil):
    if dil == 1:
        return pl.ds(pl.multiple_of(start, DIL_BLOCK), size)
    return pl.ds(start, size, stride=dil)


def _dil_fill_bias(bias_ref):
    row = lax.broadcasted_iota(jnp.int32, (2 * DIL_BLOCK, 2 * DIL_BLOCK), 0)
    kk = lax.broadcasted_iota(jnp.int32, (2 * DIL_BLOCK, 2 * DIL_BLOCK), 1)
    qi = jnp.where(row >= DIL_BLOCK, row - DIL_BLOCK, row)
    for s in range(2):
        dist = s * DIL_BLOCK + qi - kk
        bias_ref[s] = jnp.where((dist >= 0) & (dist <= DIL_BLOCK), 0.0, NEG_BIG)


def _dl_fwd(p_dl, *, name):
    S, W = p_dl.shape[0], p_dl.shape[1] // 3
    npair = W // LANES
    nsuper = S // DIL_SUPER
    assert S % DIL_SUPER == 0 and S // max(d for _, d in DILATED_PATTERNS) >= 2 * DIL_BLOCK
    scale = HEAD_DIM ** -0.5
    npat = len(DILATED_PATTERNS)

    def body(q_ref, k_ref, v_ref, o_ref, l_ref, bias_ref, *pattern_refs):
        op_refs, lp_refs = pattern_refs[:npat], pattern_refs[npat:]
        b = pl.program_id(1)
        masks = _head_masks((DIL_BLOCK, LANES))
        pl.when(b == 0)(lambda: _dil_fill_bias(bias_ref))

        def block(pi, dil, c, n, gn):
            ws = jnp.maximum(gn - 1, 0)
            qrows = n * (DIL_BLOCK * dil) + c
            krows = ws * (DIL_BLOCK * dil) + c
            q_idx = _dil_rows(qrows, DIL_BLOCK, dil)
            k_idx = _dil_rows(krows, 2 * DIL_BLOCK, dil)
            qb = q_ref[q_idx, :]
            kb = k_ref[k_idx, :].astype(BF16)
            vb = v_ref[k_idx, :].astype(BF16)
            q2 = _sb_stack_heads(qb.astype(BF16), scale)
            z = lax.dot_general(q2, kb, (((1,), (1,)), ((), ())), preferred_element_type=F32) + bias_ref[gn - ws]
            m = jnp.max(z, axis=1, keepdims=True)
            p = jnp.exp(z - m)
            den = jnp.sum(p, axis=1, keepdims=True)
            acc = jnp.dot(p.astype(BF16), vb, preferred_element_type=F32)
            lse = m + jnp.log(den)
            op_refs[pi][q_idx, :] = _sb_unstack_heads(acc / den)
            lp_refs[pi][q_idx, :] = jnp.where(masks[0], lse[:DIL_BLOCK], lse[DIL_BLOCK:])

        _dil_blocks(b, block)
        lses = [r[...] for r in lp_refs]
        top = functools.reduce(jnp.maximum, lses)
        ws_ = [jnp.exp(l - top) for l in lses]
        den = functools.reduce(jnp.add, ws_)
        num = functools.reduce(jnp.add, [w * r[...] for r, w in zip(op_refs, ws_)])
        o_ref[...] = num / den
        l_ref[...] = top + jnp.log(den)

    blk = pl.BlockSpec((DIL_SUPER, LANES), lambda p, b: (b, p))
    full = lambda off: pl.BlockSpec((S, LANES), lambda p, b: (0, off + p))
    return pl.pallas_call(
        body, name=name, grid=(npair, nsuper),
        out_shape=[jax.ShapeDtypeStruct((S, W), F32)] * 2,
        in_specs=[blk, full(npair), full(2 * npair)], out_specs=[blk, blk],
        scratch_shapes=[pltpu.VMEM((2, 2 * DIL_BLOCK, 2 * DIL_BLOCK), F32)]
        + [pltpu.VMEM((DIL_SUPER, LANES), F32)] * (2 * npat),
        compiler_params=_cparams(("arbitrary", "arbitrary")),
    )(p_dl, p_dl, p_dl)


def _dl_bwd(p_dl, o, lse, do, *, name):
    S, W = p_dl.shape[0], p_dl.shape[1] // 3
    npair = W // LANES
    nsuper = S // DIL_SUPER
    scale = HEAD_DIM ** -0.5

    def body(q_ref, k_ref, v_ref, o_ref, l_ref, do_ref, dq_ref, dk_ref, dv_ref, delta_ref, bias_ref):
        b = pl.program_id(1)

        @pl.when(b == 0)
        def _():
            dk_ref[...] = jnp.zeros_like(dk_ref)
            dv_ref[...] = jnp.zeros_like(dv_ref)
            _dil_fill_bias(bias_ref)

        dq_ref[...] = jnp.zeros_like(dq_ref)
        prod = do_ref[...] * o_ref[...]
        delta = jnp.zeros_like(prod)
        for hm in _head_masks(prod.shape):
            delta = jnp.where(hm, jnp.sum(jnp.where(hm, prod, 0.0), axis=1, keepdims=True), delta)
        delta_ref[...] = delta

        def block(pi, dil, c, n, gn):
            ws = jnp.maximum(gn - 1, 0)
            qrows = n * (DIL_BLOCK * dil) + c
            krows = ws * (DIL_BLOCK * dil) + c
            q_idx = _dil_rows(qrows, DIL_BLOCK, dil)
            k_idx = _dil_rows(krows, 2 * DIL_BLOCK, dil)
            qb = q_ref[q_idx, :]
            dob = do_ref[q_idx, :]
            lb = l_ref[q_idx, :]
            db = delta_ref[q_idx, :]
            kb = k_ref[k_idx, :].astype(BF16)
            vb = v_ref[k_idx, :].astype(BF16)
            q2 = _sb_stack_heads(qb.astype(BF16), scale)
            do2 = _sb_stack_heads(dob.astype(BF16))
            lse2 = jnp.concatenate([lb[:, HEAD_DIM * h:HEAD_DIM * h + 1] for h in range(2)], axis=0)
            delta2 = jnp.concatenate([db[:, HEAD_DIM * h:HEAD_DIM * h + 1] for h in range(2)], axis=0)
            z = lax.dot_general(q2, kb, (((1,), (1,)), ((), ())), preferred_element_type=F32)
            p = jnp.exp((z + bias_ref[gn - ws]) - lse2)
            dp = lax.dot_general(do2, vb, (((1,), (1,)), ((), ())), preferred_element_type=F32)
            dzb = (p * (dp - delta2)).astype(BF16)
            tn_dims = (((0,), (0,)), ((), ()))
            dq_blk = _sb_unstack_heads(jnp.dot(dzb, kb, preferred_element_type=F32)) * scale
            dk_blk = lax.dot_general(dzb, q2, tn_dims, preferred_element_type=F32)
            dv_blk = lax.dot_general(p.astype(BF16), do2, tn_dims, preferred_element_type=F32)
            dq_ref[q_idx, :] = dq_ref[q_idx, :] + dq_blk
            dk_ref[k_idx, :] = dk_ref[k_idx, :] + dk_blk
            dv_ref[k_idx, :] = dv_ref[k_idx, :] + dv_blk

        _dil_blocks(b, block)

    blk = pl.BlockSpec((DIL_SUPER, LANES), lambda p, b: (b, p))
    full = lambda off: pl.BlockSpec((S, LANES), lambda p, b: (0, off + p))
    return pl.pallas_call(
        body, name=name, grid=(npair, nsuper),
        out_shape=[jax.ShapeDtypeStruct((S, W), F32)] * 3,
        in_specs=[blk, full(npair), full(2 * npair), blk, blk, blk], out_specs=[blk, full(0), full(0)],
        scratch_shapes=[pltpu.VMEM((DIL_SUPER, LANES), F32), pltpu.VMEM((2, 2 * DIL_BLOCK, 2 * DIL_BLOCK), F32)],
        compiler_params=_cparams(("arbitrary", "arbitrary")),
    )(p_dl, p_dl, p_dl, o, lse, do)


class _NoExchange:
    def gather(self, family):
        return None

    def gathered(self, family, got, weights):
        pass

    def send(self, family, grads):
        return None

    def received(self, family, got):
        pass


def _local_step(x, target, gains, weights, exchanges=None):
    S, D = x.shape
    ex = exchanges or _NoExchange()
    weights = dict(weights)
    d_sb = gains["sb_out_norm"].shape[1]
    d_dl = gains["dil_out_norm"].shape[1]
    cos_t, sin_t = _rope_tables(S)

    riders = ("ffn1_w_up", "ffn1_w_down", "mixer") if exchanges else (None, None, None)
    (x1, h2), saved1 = _ffn_fwd(x, gains["ffn1_norm"], weights, tag="ffn1", ex=ex, riders=riders,
                                next_gain=gains["mix_norm"])
    w_in = weights["w_in"]
    w_in_sb, w_in_dl = w_in[:, :3 * d_sb], w_in[:, 3 * d_sb:]
    w_out = weights["w_out"]
    p_sb = _mm(h2, w_in_sb, outs=(BF16,), name="proj_sb")

    def rope_qk(acc, _, cos, sin):
        return jnp.concatenate([_rotate(acc[:, :2 * d_dl], cos, sin, 1.0), acc[:, 2 * d_dl:]], axis=1)

    p_dl = _mm(h2, w_in_dl, lanes=(cos_t, sin_t), epilogue=rope_qk, name="proj_dl_rope")
    plan = ex.gather("ffn2" if exchanges else None)
    o_sb, carries, *got = _sb_fwd(p_sb, name="sb_fwd", comm=plan)
    ex.gathered("ffn2", got[0] if got else None, weights)
    o_dl, lse_dl = _dl_fwd(p_dl, name="dl_fwd")
    merged = _rms_fwd([o_sb, o_dl], [gains["sb_out_norm"], gains["dil_out_norm"]], name="out_norm")
    x2, h3 = _mm(merged, w_out, res=x1, rows=(gains["ffn2_norm"],), outs=(F32, BF16),
                 epilogue=_residual_then_norm(1.0), name="out_proj_norm")
    (dx3, d_final, loss_wide), saved2 = _ffn_fwd(x2, gains["ffn2_norm"], weights, tag="ffn2", ex=ex, h=h3,
                                                 head=(gains["final_norm"], target))
    loss_row = loss_wide[:, :LANES]

    dx2, d_ffn2_norm, dwg2, dwu2, dwd2 = _ffn_bwd(dx3, x2, gains["ffn2_norm"], weights, saved2, tag="ffn2", ex=ex)
    d_w_out = _mm(merged, dx2, ta=True, outs=(GRAD_WIRE,), name="d_w_out")
    d_merged = _mm(dx2, w_out, tb=True, name="d_merged")
    (do_sb, do_dl), (d_sb_norm, d_dl_norm) = _rms_bwd(
        d_merged, [o_sb, o_dl], [gains["sb_out_norm"], gains["dil_out_norm"]], None, name="out_norm_bwd")
    plan = ex.send("ffn2", dict(ffn2_w_gate=dwg2, ffn2_w_up=dwu2, ffn2_w_down=dwd2))
    dq_sb, dk_sb, dv_sb, *got = _sb_bwd(p_sb, do_sb, carries, name="sb_bwd", comm=plan)
    ex.received("ffn2", got[0] if got else None)
    dq_dl, dk_dl, dv_dl = _dl_bwd(p_dl, o_dl, lse_dl, do_dl, name="dl_bwd")
    d_proj = _join_d_proj([dq_sb, dk_sb, dv_sb, dq_dl, dk_dl, dv_dl], (3, 4), cos_t, sin_t, name="d_proj")
    d_w_in = _mm(h2, d_proj, ta=True, outs=(GRAD_WIRE,), name="d_w_in")
    dx1, d_mix_norm = _mm(d_proj, w_in, tb=True, extras=(x1, dx2), rows=(gains["mix_norm"],), row_sums=1,
                          epilogue=_rms_bwd_epilogue, name="dh_mix_norm_bwd")
    dx, d_ffn1_norm, dwg1, dwu1, dwd1 = _ffn_bwd(
        dx1, x, gains["ffn1_norm"], weights, saved1, tag="ffn1", ex=ex,
        rider=("mixer", dict(w_in=d_w_in, w_out=d_w_out)), spread=True)
    gain_grads = dict(ffn1_norm=d_ffn1_norm, mix_norm=d_mix_norm, sb_out_norm=d_sb_norm, dil_out_norm=d_dl_norm,
                      ffn2_norm=d_ffn2_norm, final_norm=d_final)
    weight_grads = dict(ffn1_w_gate=dwg1, ffn1_w_up=dwu1, ffn1_w_down=dwd1, w_in=d_w_in, w_out=d_w_out,
                        ffn2_w_gate=dwg2, ffn2_w_up=dwu2, ffn2_w_down=dwd2)
    return loss_row, dx, gain_grads, weight_grads


def _mesh_position():
    return lax.axis_index("x"), lax.axis_index("y"), lax.axis_index("c")


def _flip(coord, bit):
    return 1 - coord if bit else coord


RELATIONS = [(rx, ry, rc) for rx in (0, 1) for ry in (0, 1) for rc in (0, 1)][1:]


class _GatherPlan:
    def __init__(self, shards):
        n = len(shards)
        self.operands = list(shards)
        self.out_shapes = [jax.ShapeDtypeStruct((N_DEV,) + s.shape, s.dtype) for s in shards]
        self.scratch = [pltpu.SemaphoreType.DMA((n, 7)), pltpu.SemaphoreType.DMA((n, 7)),
                        pltpu.SemaphoreType.DMA((n,))]

    def _copies(self, in_refs, out_refs, sems):
        send_sems, recv_sems, local_sems = sems
        x, y, c = _mesh_position()
        me, sibling = (x, y, c), (x, y, 1 - c)
        chips = [(1 - x, y), (x, 1 - y), (1 - x, 1 - y)]
        plans = []
        for t, (x_ref, out_ref) in enumerate(zip(in_refs, out_refs)):
            def slot(px, py, pc, out_ref=out_ref):
                return out_ref.at[4 * px + 2 * py + pc]

            def copy(k, block, to, src=None, t=t, slot=slot):
                return pltpu.make_async_remote_copy(
                    src_ref=slot(*block) if src is None else src, dst_ref=slot(*block),
                    send_sem=send_sems.at[t, k], recv_sem=recv_sems.at[t, k],
                    device_id=to, device_id_type=pl.DeviceIdType.MESH)

            plans.append(dict(
                mine=pltpu.make_async_copy(x_ref, slot(*me), local_sems.at[t]),
                first=[copy(0, me, sibling, src=x_ref)]
                + [copy(1 + j, me, (*chip, c), src=x_ref) for j, chip in enumerate(chips)],
                over_ici=[copy(1 + j, (*chip, c), me) for j, chip in enumerate(chips)],
                passed=[copy(4 + j, (*chip, c), sibling) for j, chip in enumerate(chips)],
                from_sibling=[copy(0, sibling, me)] + [copy(4 + j, (*chip, 1 - c), me) for j, chip in enumerate(chips)]))
        return plans

    def start(self, in_refs, out_refs, sems):
        for p in self._copies(in_refs, out_refs, sems):
            p["mine"].start()
            for cp in p["first"]:
                cp.start()

    def finish(self, in_refs, out_refs, sems):
        plans = self._copies(in_refs, out_refs, sems)
        for p in plans:
            for arrived, onward in zip(p["over_ici"], p["passed"]):
                arrived.wait_recv()
                onward.start()
        for p in plans:
            for cp in p["from_sibling"]:
                cp.wait_recv()
            for cp in p["first"] + p["passed"]:
                cp.wait_send()
            p["mine"].wait()


class _Hosted:
    def __init__(self, plan, n_in, n_out, n_scratch):
        self.plan, self.n_in, self.n_out, self.n_scratch = plan, n_in, n_out, n_scratch
        self.operands = list(plan.operands) if plan else []
        self.out_shapes = list(plan.out_shapes) if plan else []
        self.scratch = list(plan.scratch) if plan else []
        self.in_specs = [pl.BlockSpec(memory_space=pl.ANY)] * len(self.operands)
        self.out_specs = [pl.BlockSpec(memory_space=pl.ANY)] * len(self.out_shapes)

    def semantics(self, sem):
        return sem if self.plan is None else ("arbitrary",) * len(sem)

    def _at(self, grid, last):
        hit = None
        for d, n in enumerate(grid):
            here = pl.program_id(d) == (n - 1 if last else 0)
            hit = here if hit is None else hit & here
        return hit

    def begin(self, refs, grid):
        if self.plan is None:
            return refs
        k_in, k_out = len(self.operands), len(self.out_shapes)
        ins, rest = refs[:self.n_in], refs[self.n_in:]
        c_in, rest = rest[:k_in], rest[k_in:]
        outs, rest = rest[:self.n_out], rest[self.n_out:]
        c_out, rest = rest[:k_out], rest[k_out:]
        scratch, sems = rest[:self.n_scratch], rest[self.n_scratch:]
        self._args = (c_in, c_out, sems)
        pl.when(self._at(grid, False))(lambda: self.plan.start(*self._args))
        return tuple(ins) + tuple(outs) + tuple(scratch)

    def end(self, grid):
        if self.plan is not None:
            pl.when(self._at(grid, True))(lambda: self.plan.finish(*self._args))


def _run_plan(plan, *, name):
    hosted = _Hosted(plan, 0, 0, 0)

    def body(*refs):
        hosted.begin(refs, (1,))
        hosted.end((1,))

    return pl.pallas_call(
        body, name=name, grid=(1,), out_shape=hosted.out_shapes,
        in_specs=hosted.in_specs, out_specs=hosted.out_specs, scratch_shapes=hosted.scratch,
        compiler_params=pltpu.CompilerParams(dimension_semantics=("arbitrary",)),
    )(*hosted.operands)


class _ExchangePlan:
    def __init__(self, packs):
        n = len(packs)
        self.operands = list(packs)
        self.out_shapes = [jax.ShapeDtypeStruct(p.shape, p.dtype) for p in packs]
        self.scratch = [pltpu.SemaphoreType.DMA((n, 7)), pltpu.SemaphoreType.DMA((n, 7)),
                        pltpu.SemaphoreType.DMA((n,))]

    def _copies(self, in_refs, out_refs, sems):
        send_sems, recv_sems, local_sems = sems
        x, y, c = _mesh_position()
        me = 4 * x + 2 * y + c
        copies = [pltpu.make_async_copy(i.at[me], o.at[me], local_sems.at[t])
                  for t, (i, o) in enumerate(zip(in_refs, out_refs))]
        for r, (rx, ry, rc) in enumerate(RELATIONS):
            px, py, pc = _flip(x, rx), _flip(y, ry), _flip(c, rc)
            peer = 4 * px + 2 * py + pc
            copies += [pltpu.make_async_remote_copy(
                src_ref=i.at[peer], dst_ref=o.at[me], send_sem=send_sems.at[t, r], recv_sem=recv_sems.at[t, r],
                device_id=(px, py, pc), device_id_type=pl.DeviceIdType.MESH)
                for t, (i, o) in enumerate(zip(in_refs, out_refs))]
        return copies

    def start(self, in_refs, out_refs, sems):
        for cp in self._copies(in_refs, out_refs, sems):
            cp.start()

    def finish(self, in_refs, out_refs, sems):
        for cp in self._copies(in_refs, out_refs, sems):
            cp.wait()


def _all_reduce_rows(v, *, name):
    R, C = v.shape

    def body(v_ref, out_ref, buf, send_sems, recv_sems):
        x, y, c = _mesh_position()
        me = 4 * x + 2 * y + c
        buf[me] = v_ref[...]
        copies = []
        for r, (rx, ry, rc) in enumerate(RELATIONS):
            cp = pltpu.make_async_remote_copy(
                src_ref=v_ref, dst_ref=buf.at[me], send_sem=send_sems.at[r], recv_sem=recv_sems.at[r],
                device_id=(_flip(x, rx), _flip(y, ry), _flip(c, rc)), device_id_type=pl.DeviceIdType.MESH)
            cp.start()
            copies.append(cp)
        for cp in copies:
            cp.wait()
        total = buf[0]
        for s in range(1, N_DEV):
            total = total + buf[s]
        out_ref[...] = total

    return pl.pallas_call(
        body, name=name,
        out_shape=jax.ShapeDtypeStruct((R, C), F32),
        in_specs=[pl.BlockSpec(memory_space=pltpu.VMEM)],
        out_specs=pl.BlockSpec(memory_space=pltpu.VMEM),
        scratch_shapes=[pltpu.VMEM((N_DEV, R, C), F32), pltpu.SemaphoreType.DMA((7,)), pltpu.SemaphoreType.DMA((7,))],
    )(v)


def _sum_slots(recv, *, name):
    _, R, C = recv.shape
    tr = _pick(R, (256, 208, 128, 64, 32, 16))

    def body(r_ref, o_ref):
        total = r_ref[0].astype(F32)
        for s in range(1, N_DEV):
            total = total + r_ref[s].astype(F32)
        o_ref[...] = total

    return pl.pallas_call(
        body, name=name, grid=(R // tr,),
        out_shape=jax.ShapeDtypeStruct((R, C), F32),
        in_specs=[pl.BlockSpec((N_DEV, tr, C), lambda i: (0, i, 0))],
        out_specs=pl.BlockSpec((tr, C), lambda i: (i, 0)),
        compiler_params=_cparams(("parallel",)),
    )(recv)


def _adamw(w, g, m, v, *, name):
    R, C = w.shape
    tr = _pick(R, (256, 128, 64, 32, 16, 8))

    def body(w_ref, g_ref, m_ref, v_ref, d_ref, nm_ref, nv_ref):
        g = g_ref[...]
        m_new = ADAM_B1 * m_ref[...] + (1.0 - ADAM_B1) * g
        v_new = ADAM_B2 * v_ref[...] + (1.0 - ADAM_B2) * (g * g)
        m_hat = m_new / (1.0 - ADAM_B1 ** ADAM_STEP)
        v_hat = v_new / (1.0 - ADAM_B2 ** ADAM_STEP)
        d_ref[...] = -ADAM_LR * (m_hat / (jnp.sqrt(v_hat) + ADAM_EPS) + ADAM_WD * w_ref[...])
        nm_ref[...] = m_new
        nv_ref[...] = v_new

    spec = pl.BlockSpec((tr, C), lambda i: (i, 0))
    return pl.pallas_call(
        body, name=name, grid=(R // tr,),
        out_shape=[jax.ShapeDtypeStruct((R, C), F32)] * 3,
        in_specs=[spec] * 4, out_specs=[spec] * 3,
        compiler_params=_cparams(("parallel",)),
    )(w, g, m, v)


WEIGHT_NAMES = ["ffn1_norm", "ffn1_w_gate", "ffn1_w_up", "ffn1_w_down", "mix_norm", "w_in", "sb_out_norm",
                "dil_out_norm", "w_out", "ffn2_norm", "ffn2_w_gate", "ffn2_w_up", "ffn2_w_down", "final_norm"]
GAIN_NAMES = ["ffn1_norm", "mix_norm", "sb_out_norm", "dil_out_norm", "ffn2_norm", "final_norm"]
COL_SHARDED = ["ffn1_w_gate", "ffn1_w_up", "ffn2_w_gate", "ffn2_w_up", "w_in"]
ROW_SHARDED = ["ffn1_w_down", "ffn2_w_down", "w_out"]
GROUPS = {"mixer": (["w_in"], ["w_out"]),
          "ffn2": (["ffn2_w_gate", "ffn2_w_up"], ["ffn2_w_down"])}
for _ffn in ("ffn1", "ffn2"):
    GROUPS.update({f"{_ffn}_w_gate": ([f"{_ffn}_w_gate"], []), f"{_ffn}_w_up": ([f"{_ffn}_w_up"], []),
                   f"{_ffn}_w_down": ([], [f"{_ffn}_w_down"])})


class _Exchanges:
    def __init__(self, params):
        self.params = params
        self.grads = {}

    def gather(self, group):
        if group is None:
            return None
        cols, rows = GROUPS[group]
        packs = []
        if cols:
            packs.append(jnp.concatenate([self.params[n] for n in cols], axis=1).astype(BF16))
        if rows:
            packs.append(jnp.concatenate([self.params[n] for n in rows], axis=0).astype(BF16))
        return _GatherPlan(packs)

    def gathered(self, group, got, weights):
        if group is None:
            return
        cols, rows = GROUPS[group]
        got = list(got)
        if cols:
            col_all, off = got.pop(0), 0
            for n in cols:
                w = self.params[n].shape[1]
                piece = col_all[:, :, off:off + w]
                weights[n] = jnp.transpose(piece, (1, 0, 2)).reshape(piece.shape[1], N_DEV * w)
                off += w
        if rows:
            row_all, off = got.pop(0), 0
            for n in rows:
                r = self.params[n].shape[0]
                weights[n] = row_all[:, off:off + r, :].reshape(N_DEV * r, row_all.shape[2])
                off += r

    def send(self, group, grads):
        if group is None:
            return None
        cols, rows = GROUPS[group]
        packs = []
        if cols:
            chunks = [jnp.transpose(grads[n].reshape(grads[n].shape[0], N_DEV, self.params[n].shape[1]), (1, 0, 2))
                      for n in cols]
            packs.append(jnp.concatenate(chunks, axis=2).astype(GRAD_WIRE))
        if rows:
            chunks = [grads[n].reshape(N_DEV, self.params[n].shape[0], grads[n].shape[1]) for n in rows]
            packs.append(jnp.concatenate(chunks, axis=1).astype(GRAD_WIRE))
        return _ExchangePlan(packs)

    def received(self, group, got):
        if group is None:
            return
        cols, rows = GROUPS[group]
        got = list(got)
        if cols:
            col_grad, off = _sum_slots(got.pop(0), name=f"sum_col_grads_{group}"), 0
            for n in cols:
                w = self.params[n].shape[1]
                self.grads[n] = col_grad[:, off:off + w]
                off += w
        if rows:
            row_grad, off = _sum_slots(got.pop(0), name=f"sum_row_grads_{group}"), 0
            for n in rows:
                r = self.params[n].shape[0]
                self.grads[n] = row_grad[off:off + r, :]
                off += r


def _step(x, target, params, moments_m, moments_v):
    ex = _Exchanges(params)
    weights = {}
    ex.gathered("ffn1_w_gate", _run_plan(ex.gather("ffn1_w_gate"), name="gather_ffn1_gate"), weights)
    gains = {n: params[n] for n in GAIN_NAMES}
    loss_row, grad_x, gain_grads, _ = _local_step(x, target, gains, weights, ex)
    grads = ex.grads

    rows = [gain_grads[n].reshape(-1, LANES) for n in GAIN_NAMES] + [loss_row]
    small = jnp.concatenate(rows, axis=0)
    pad = (-small.shape[0]) % 8
    small = jnp.pad(small, ((0, pad), (0, 0)))
    small = _all_reduce_rows(small, name="reduce_gains_loss")
    off = 0
    for n in GAIN_NAMES:
        r = gain_grads[n].shape[1] // LANES
        grads[n] = small[off:off + r].reshape(1, -1)
        off += r
    loss = small[off, 0]

    delta, new_m, new_v = {}, {}, {}
    for n in WEIGHT_NAMES:
        delta[n], new_m[n], new_v[n] = _adamw(params[n], grads[n], moments_m[n], moments_v[n], name=f"adamw_{n}")
    return loss, grad_x, grads, delta, new_m, new_v


def kernel(x, ffn1_norm, ffn1_w_gate, ffn1_w_up, ffn1_w_down, mix_norm, w_in, sb_out_norm, dil_out_norm, w_out, ffn2_norm, ffn2_w_gate, ffn2_w_up, ffn2_w_down, final_norm, loss_target, m_ffn1_norm, m_ffn1_w_gate, m_ffn1_w_up, m_ffn1_w_down, m_mix_norm, m_w_in, m_sb_out_norm, m_dil_out_norm, m_w_out, m_ffn2_norm, m_ffn2_w_gate, m_ffn2_w_up, m_ffn2_w_down, m_final_norm, v_ffn1_norm, v_ffn1_w_gate, v_ffn1_w_up, v_ffn1_w_down, v_mix_norm, v_w_in, v_sb_out_norm, v_dil_out_norm, v_w_out, v_ffn2_norm, v_ffn2_w_gate, v_ffn2_w_up, v_ffn2_w_down, v_final_norm):
    given = dict(locals())
    shapes = {n: given[n].shape for n in WEIGHT_NAMES}

    def as2d(a):
        return a.reshape(1, -1) if a.ndim == 1 else a.reshape(a.shape[-2], a.shape[-1])

    params = {n: as2d(given[n]) for n in WEIGHT_NAMES}
    moments_m = {n: as2d(given["m_" + n]) for n in WEIGHT_NAMES}
    moments_v = {n: as2d(given["v_" + n]) for n in WEIGHT_NAMES}
    loss, grad_x, grads, delta, new_m, new_v = _step(x[0], loss_target[0], params, moments_m, moments_v)
    back = lambda d: [d[n].reshape(shapes[n]) for n in WEIGHT_NAMES]
    return (loss, grad_x[None], *back(grads), *back(delta), *back(new_m), *back(new_v))
```

```python
import functools

import jax
import jax.numpy as jnp
from jax import lax
from jax.experimental import pallas as pl
from jax.experimental.pallas import tpu as pltpu

F32 = jnp.float32
BF16 = jnp.bfloat16
GRAD_WIRE = jnp.bfloat16

N_DEV = 8
HEAD_DIM = 64
LANES = 128
DILATED_PATTERNS = ((128, 1), (512, 4), (2048, 16))
DIL_BLOCK = 128
DIL_SUPER = 2048
DIL_UNROLL = 8
SB_TILE = 256
SB_UNROLL = 4
SB_DEAD = 90.0
SB_UNSEEN = -1e30
ROPE_THETA = 10000.0
RMS_EPS = 1e-6
HALF_STEP = 0.5
ADAM_LR = 0.001
ADAM_B1 = 0.9
ADAM_B2 = 0.999
ADAM_EPS = 1e-08
ADAM_WD = 0.01
ADAM_STEP = 10
NEG_BIG = -1e30
VMEM_CAP_MB = 60


def _pick(n, prefs):
    for p in prefs:
        if n % p == 0:
            return p
    return n


MM_MAX_TILE = 1536


def _largest_tile(n, cap):
    if n <= cap:
        return n
    for t in range(cap - cap % LANES, 0, -LANES):
        if n % t == 0:
            return t
    return n


def _cparams(sem=None, vmem_mb=48):
    return pltpu.CompilerParams(dimension_semantics=sem, vmem_limit_bytes=min(vmem_mb, VMEM_CAP_MB) * 1024 * 1024)


def _nbytes(shape, dtype):
    n = 1
    for s in shape:
        n *= s
    return n * jnp.dtype(dtype).itemsize


def _mm(a, b, *, name, ta=False, tb=False, outs=(F32,), res=None, alpha=1.0, extras=(), epilogue=None,
        tm=None, tn=None, tk=None, comm=None, rows=(), lanes=(), row_sums=0):
    if ta:
        K, M = a.shape
    else:
        M, K = a.shape
    if tb:
        N, Kb = b.shape
    else:
        Kb, N = b.shape
    assert K == Kb, (a.shape, b.shape, ta, tb)
    tm = tm or (_largest_tile(M, MM_MAX_TILE) if ta else _pick(M, (512, 256, 128)))
    tn = tn or _largest_tile(N, MM_MAX_TILE)
    tk = tk or (K if K <= 3072 else _pick(K, (2048, 1024, 512, 256, 128)))
    nk = K // tk
    a_spec = pl.BlockSpec((tk, tm), lambda i, j, k: (k, i)) if ta else pl.BlockSpec((tm, tk), lambda i, j, k: (i, k))
    b_spec = pl.BlockSpec((tn, tk), lambda i, j, k: (j, k)) if tb else pl.BlockSpec((tk, tn), lambda i, j, k: (k, j))
    mn_spec = pl.BlockSpec((tm, tn), lambda i, j, k: (i, j))
    dims = (((0 if ta else 1,), (1 if tb else 0,)), ((), ()))
    row_spec = pl.BlockSpec((1, tn), lambda i, j, k: (0, j))
    lane_spec = pl.BlockSpec((tm, LANES), lambda i, j, k: (i, 0))
    n_extra = len(extras) + (1 if res is not None else 0) + len(rows) + len(lanes)
    n_mn = len(outs)
    n_out = n_mn + row_sums
    assert row_sums == 0 or tn == N
    grid = (M // tm, N // tn, nk)
    hosted = _Hosted(comm, n_in=2 + n_extra, n_out=n_out, n_scratch=1 if nk > 1 else 0)

    def body(*refs):
        a_ref, b_ref = refs[0], refs[1]
        in_refs = refs[2:2 + n_extra]
        refs = hosted.begin(refs, grid)
        out_refs = refs[2 + n_extra:2 + n_extra + n_out]
        prod = lax.dot_general(a_ref[...].astype(BF16), b_ref[...].astype(BF16), dims, preferred_element_type=F32)

        def finish(acc):
            blocks = [r[...] for r in in_refs]
            if res is not None:
                r_blk, blocks = blocks[0], blocks[1:]
            else:
                r_blk = None
            if epilogue is None:
                val = acc * alpha
                if r_blk is not None:
                    val = val + r_blk
                vals = (val,)
            else:
                vals = epilogue(acc, r_blk, *blocks)
                vals = vals if isinstance(vals, (tuple, list)) else (vals,)
            for o_ref, v in zip(out_refs[:n_mn], vals[:n_mn]):
                o_ref[...] = v.astype(o_ref.dtype)
            first_rows = pl.program_id(0) == 0
            for o_ref, part in zip(out_refs[n_mn:], vals[n_mn:]):
                @pl.when(first_rows)
                def _(o_ref=o_ref, part=part):
                    o_ref[...] = part

                @pl.when(jnp.logical_not(first_rows))
                def _(o_ref=o_ref, part=part):
                    o_ref[...] += part

        if nk == 1:
            finish(prod)
        else:
            acc_ref = refs[2 + n_extra + n_out]
            k = pl.program_id(2)

            @pl.when(k == 0)
            def _():
                acc_ref[...] = prod

            @pl.when(k > 0)
            def _():
                acc_ref[...] += prod

            @pl.when(k == nk - 1)
            def _():
                finish(acc_ref[...])

        hosted.end(grid)

    mn_operands = ([res] if res is not None else []) + list(extras)
    operands = [a, b] + mn_operands + list(rows) + list(lanes)
    in_specs = [a_spec, b_spec] + [mn_spec] * len(mn_operands) + [row_spec] * len(rows) + [lane_spec] * len(lanes)
    est = 2 * (_nbytes((tm, tk), a.dtype) + _nbytes((tk, tn), b.dtype))
    est += 2 * sum(_nbytes((tm, tn), o.dtype) for o in mn_operands)
    est += 2 * sum(_nbytes((tm, tn), d) for d in outs) + 2 * _nbytes((tm, tn), F32)
    semantics = ("parallel", "parallel", "arbitrary") if row_sums == 0 else ("arbitrary",) * 3
    result = pl.pallas_call(
        body, name=name, grid=grid,
        out_shape=[jax.ShapeDtypeStruct((M, N), d) for d in outs]
        + [jax.ShapeDtypeStruct((1, N), F32)] * row_sums + hosted.out_shapes,
        in_specs=in_specs + hosted.in_specs,
        out_specs=[mn_spec] * n_mn + [row_spec] * row_sums + hosted.out_specs,
        scratch_shapes=([pltpu.VMEM((tm, tn), F32)] if nk > 1 else []) + hosted.scratch,
        compiler_params=_cparams(hosted.semantics(semantics), vmem_mb=max(32, 2 * est // (1024 * 1024))),
    )(*operands, *hosted.operands)
    own, got = result[:n_out], list(result[n_out:])
    own = own[0] if n_out == 1 else own
    return own if comm is None else (own, got)


def _rms_hat(x):
    r = lax.rsqrt(jnp.mean(x * x, axis=-1, keepdims=True) + RMS_EPS)
    return x * r, r


def _rms_fwd(xs, gains, *, name, comm=None):
    S = xs[0].shape[0]
    widths = [x.shape[1] for x in xs]
    tm = _pick(S, (512, 256, 128))
    n = len(xs)
    grid = (S // tm,)
    hosted = _Hosted(comm, n_in=2 * n, n_out=1, n_scratch=0)

    def body(*refs):
        refs = hosted.begin(refs, grid)
        o_ref = refs[2 * n]
        off = 0
        for i in range(n):
            xh, _ = _rms_hat(refs[i][...])
            o_ref[:, off:off + widths[i]] = (xh * refs[n + i][...]).astype(o_ref.dtype)
            off += widths[i]
        hosted.end(grid)

    out, *got = pl.pallas_call(
        body, name=name, grid=grid,
        out_shape=[jax.ShapeDtypeStruct((S, sum(widths)), BF16)] + hosted.out_shapes,
        in_specs=[pl.BlockSpec((tm, w), lambda i: (i, 0)) for w in widths]
        + [pl.BlockSpec((1, w), lambda i: (0, 0)) for w in widths] + hosted.in_specs,
        out_specs=[pl.BlockSpec((tm, sum(widths)), lambda i: (i, 0))] + hosted.out_specs,
        scratch_shapes=hosted.scratch,
        compiler_params=_cparams(hosted.semantics(("parallel",))),
    )(*xs, *gains, *hosted.operands)
    return out if comm is None else (out, got)


def _rms_bwd(dh, xs, gains, res, *, name):
    S = xs[0].shape[0]
    widths = [x.shape[1] for x in xs]
    tm = _pick(S, (512, 256, 128))
    n = len(xs)
    has_res = res is not None

    def body(*refs):
        dh_ref = refs[0]
        x_refs = refs[1:1 + n]
        g_refs = refs[1 + n:1 + 2 * n]
        r_ref = refs[1 + 2 * n] if has_res else None
        base = 1 + 2 * n + (1 if has_res else 0)
        dx_refs = refs[base:base + n]
        dg_refs = refs[base + n:base + 2 * n]
        first = pl.program_id(0) == 0
        off = 0
        for i in range(n):
            x = x_refs[i][...]
            xh, r = _rms_hat(x)
            d = dh_ref[:, off:off + widths[i]]
            dxh = d * g_refs[i][...]
            dx = r * (dxh - xh * jnp.mean(dxh * xh, axis=-1, keepdims=True))
            if has_res:
                dx = dx + r_ref[...]
            dx_refs[i][...] = dx
            part = jnp.sum(d * xh, axis=0, keepdims=True)

            @pl.when(first)
            def _(i=i, part=part):
                dg_refs[i][...] = part

            @pl.when(jnp.logical_not(first))
            def _(i=i, part=part):
                dg_refs[i][...] += part

            off += widths[i]

    in_specs = [pl.BlockSpec((tm, sum(widths)), lambda i: (i, 0))]
    in_specs += [pl.BlockSpec((tm, w), lambda i: (i, 0)) for w in widths]
    in_specs += [pl.BlockSpec((1, w), lambda i: (0, 0)) for w in widths]
    operands = [dh, *xs, *gains]
    if has_res:
        in_specs.append(pl.BlockSpec((tm, widths[0]), lambda i: (i, 0)))
        operands.append(res)
    out = pl.pallas_call(
        body, name=name, grid=(S // tm,),
        out_shape=[jax.ShapeDtypeStruct((S, w), F32) for w in widths] + [jax.ShapeDtypeStruct((1, w), F32) for w in widths],
        in_specs=in_specs,
        out_specs=[pl.BlockSpec((tm, w), lambda i: (i, 0)) for w in widths]
        + [pl.BlockSpec((1, w), lambda i: (0, 0)) for w in widths],
        compiler_params=_cparams(("arbitrary",)),
    )(*operands)
    return out[:n], out[n:]


def _sigmoid(g):
    return 1.0 / (1.0 + jnp.exp(-g))


def _ride(result, plan):
    return result if plan is not None else (result, None)


def _residual_then_norm(alpha):
    def epilogue(acc, res, gain):
        y = res + alpha * acc
        return y, _rms_hat(y)[0] * gain
    return epilogue


def _ffn_fwd(x, gain, w, *, tag, ex, first_rider=None, riders=(None, None, None), head=None, h=None,
             next_gain=None):
    if h is None:
        plan = ex.gather(first_rider)
        h, got = _ride(_rms_fwd([x], [gain], name=f"{tag}_norm", comm=plan), plan)
        ex.gathered(first_rider, got, w)
    plan = ex.gather(riders[0])
    g, got = _ride(_mm(h, w[f"{tag}_w_gate"], outs=(BF16,), name=f"{tag}_gate", comm=plan), plan)
    ex.gathered(riders[0], got, w)

    def act(acc, _, g_blk):
        gf = g_blk.astype(F32)
        sig = _sigmoid(gf)
        silu = gf * sig
        return silu, acc * (sig + silu * (1.0 - sig)), silu * acc

    plan = ex.gather(riders[1])
    (da_du, da_dg, a), got = _ride(_mm(h, w[f"{tag}_w_up"], outs=(BF16, BF16, BF16), extras=(g,), epilogue=act,
                                       name=f"{tag}_up_act", comm=plan), plan)
    ex.gathered(riders[1], got, w)
    plan = ex.gather(riders[2])
    if head is None and next_gain is None:
        y, got = _ride(_mm(a, w[f"{tag}_w_down"], res=x, alpha=HALF_STEP, name=f"{tag}_down", comm=plan), plan)
    elif head is None:
        y, got = _ride(_mm(a, w[f"{tag}_w_down"], res=x, rows=(next_gain,), outs=(F32, BF16),
                           epilogue=_residual_then_norm(HALF_STEP), name=f"{tag}_down_norm", comm=plan), plan)
    else:
        final_gain, target = head
        y, got = _ride(_mm(a, w[f"{tag}_w_down"], res=x, extras=(target,), rows=(final_gain,), row_sums=2,
                           epilogue=_loss_head_epilogue, name=f"{tag}_down_loss", comm=plan), plan)
    ex.gathered(riders[2], got, w)
    return y, (h, da_dg, da_du, a)


def _loss_head_epilogue(acc, x_in, target, gain):
    xh, r = _rms_hat(x_in + HALF_STEP * acc)
    err = xh * gain - target
    dy = err * (1.0 / acc.shape[1])
    dxh = dy * gain
    dx = r * (dxh - xh * jnp.mean(dxh * xh, axis=-1, keepdims=True))
    loss = 0.5 * jnp.sum(jnp.mean(err * err, axis=-1, keepdims=True), axis=0, keepdims=True)
    return dx, jnp.sum(dy * xh, axis=0, keepdims=True), jnp.zeros_like(gain) + loss


def _rms_bwd_epilogue(acc, dh_so_far, x, dres, gain):
    dh = acc if dh_so_far is None else acc + dh_so_far
    xh, r = _rms_hat(x)
    dxh = dh * gain
    dx = r * (dxh - xh * jnp.mean(dxh * xh, axis=-1, keepdims=True)) + dres
    return dx, jnp.sum(dh * xh, axis=0, keepdims=True)


def _ffn_bwd(dout, x, gain, w, saved, *, tag, ex, rider=(None, None), spread=False):
    h, da_dg, da_du, a = saved
    wg, wu, wd = (w[f"{tag}_w_{n}"] for n in ("gate", "up", "down"))

    def act_bwd(acc, _, dg_blk, du_blk):
        da = acc * HALF_STEP
        return da * dg_blk.astype(F32), da * du_blk.astype(F32)

    def carrying(group, grad, call):
        group = group if spread else None
        plan = ex.send(group, {group: grad})
        out, got = _ride(call(plan), plan)
        ex.received(group, got)
        return out

    plan = ex.send(*rider)
    (dg, du), got = _ride(_mm(dout, wd, tb=True, outs=(BF16, BF16), extras=(da_dg, da_du), epilogue=act_bwd,
                              name=f"{tag}_bwd_act", comm=plan), plan)
    ex.received(rider[0], got)
    dwg = _mm(h, dg, ta=True, outs=(GRAD_WIRE,), name=f"{tag}_dwg")
    dwu = carrying(f"{tag}_w_gate", dwg, lambda plan: _mm(h, du, ta=True, outs=(GRAD_WIRE,), name=f"{tag}_dwu", comm=plan))
    dwd = carrying(f"{tag}_w_up", dwu,
                   lambda plan: _mm(a, dout, ta=True, outs=(GRAD_WIRE,), alpha=HALF_STEP, name=f"{tag}_dwd", comm=plan))
    dh = carrying(f"{tag}_w_down", dwd, lambda plan: _mm(dg, wg, tb=True, name=f"{tag}_dh_gate", comm=plan))
    dx, dgain = _mm(du, wu, tb=True, res=dh, extras=(x, dout), rows=(gain,), row_sums=1, epilogue=_rms_bwd_epilogue,
                    name=f"{tag}_dh_up_norm_bwd")
    return dx, dgain, dwg, dwu, dwd


def _rope_tables(S):
    half = HEAD_DIM // 2
    inv_freq = ROPE_THETA ** (-jnp.arange(half, dtype=F32) / half)
    ang = jnp.arange(S, dtype=F32)[:, None] * inv_freq[None, :]
    cos, sin = jnp.cos(ang), jnp.sin(ang)
    reps = LANES // HEAD_DIM
    cos_t = jnp.tile(jnp.concatenate([cos, cos], axis=1), (1, reps))
    sin_t = jnp.tile(jnp.concatenate([-sin, sin], axis=1), (1, reps))
    return cos_t, sin_t


def _rotate(v, cos, sin, sign):
    half = HEAD_DIM // 2
    groups = []
    for g in range(v.shape[1] // LANES):
        t = v[:, g * LANES:(g + 1) * LANES]
        lane = lax.broadcasted_iota(jnp.int32, t.shape, 1)
        swapped = jnp.where(lane % HEAD_DIM < half, pltpu.roll(t, LANES - half, axis=1), pltpu.roll(t, half, axis=1))
        groups.append(t * cos + swapped * (sin * sign))
    return groups[0] if len(groups) == 1 else jnp.concatenate(groups, axis=1)


def _join_d_proj(pieces, rotated, cos_t, sin_t, *, name):
    S = pieces[0].shape[0]
    widths = [p.shape[1] for p in pieces]
    tm = _pick(S, (256, 128))
    n = len(pieces)

    def body(*refs):
        c_ref, s_ref, o_ref = refs[n], refs[n + 1], refs[n + 2]
        off = 0
        for i in range(n):
            v = refs[i][...]
            if i in rotated:
                v = _rotate(v, c_ref[...], s_ref[...], -1.0)
            o_ref[:, off:off + widths[i]] = v.astype(o_ref.dtype)
            off += widths[i]

    return pl.pallas_call(
        body, name=name, grid=(S // tm,),
        out_shape=jax.ShapeDtypeStruct((S, sum(widths)), BF16),
        in_specs=[pl.BlockSpec((tm, w), lambda i: (i, 0)) for w in widths]
        + [pl.BlockSpec((tm, LANES), lambda i: (i, 0))] * 2,
        out_specs=pl.BlockSpec((tm, sum(widths)), lambda i: (i, 0)),
        compiler_params=_cparams(("parallel",)),
    )(*pieces, cos_t, sin_t)


def _head_masks(shape):
    lane = lax.broadcasted_iota(jnp.int32, shape, 1)
    return [(lane >= HEAD_DIM * h) & (lane < HEAD_DIM * (h + 1)) for h in range(LANES // HEAD_DIM)]


def _sb_scores(q2, k_j):
    z = lax.dot_general(q2, k_j, (((1,), (1,)), ((), ())), preferred_element_type=F32)
    sign_bit = jnp.int32(-2 ** 31)
    minus_abs = lax.bitcast_convert_type(lax.bitcast_convert_type(z, jnp.int32) | sign_bit, F32)
    softplus = jnp.maximum(z, 0.0) + jnp.log(1.0 + jnp.exp(minus_abs))
    return z - softplus, softplus


def _sb_stack_heads(t, scale=None):
    parts = [jnp.where(hm, t, jnp.zeros_like(t)) for hm in _head_masks(t.shape)]
    t2 = jnp.concatenate(parts, axis=0)
    if scale is not None:
        t2 = (t2.astype(F32) * scale).astype(t2.dtype)
    return t2


def _sb_unstack_heads(t2):
    T = t2.shape[0] // 2
    masks = _head_masks((T, LANES))
    return jnp.where(masks[0], t2[:T], t2[T:])


def _sb_causal(T):
    row = lax.broadcasted_iota(jnp.int32, (2 * T, T), 0)
    col = lax.broadcasted_iota(jnp.int32, (2 * T, T), 1)
    return col < jnp.where(row >= T, row - T, row)


def _sb_triangle(T, later):
    row = lax.broadcasted_iota(jnp.int32, (T, T), 0)
    col = lax.broadcasted_iota(jnp.int32, (T, T), 1)
    return ((row > col) if later else (row < col)).astype(BF16)


def _sb_fwd(p_sb, *, name, comm=None):
    S = p_sb.shape[0]
    W = p_sb.shape[1] // 3
    npair = W // LANES
    T = SB_TILE
    n_tiles = S // T
    assert n_tiles <= HEAD_DIM
    scale = HEAD_DIM ** -0.5

    grid = (npair, n_tiles)
    hosted = _Hosted(comm, n_in=5, n_out=2, n_scratch=0)

    def body(*refs):
        q_ref, k_ref, v_ref, causal_ref, later_ref, o_ref, c_ref = hosted.begin(refs, grid)
        I = pl.program_id(1)
        lane = lax.broadcasted_iota(jnp.int32, (T, LANES), 1)
        causal = causal_ref[...]
        later_than = later_ref[...]
        q2 = _sb_stack_heads(q_ref[...], scale)

        def scores(J, diag):
            off = pl.multiple_of(J * T, T)
            log_beta, stay = _sb_scores(q2, k_ref[pl.ds(off, T), :])
            if diag:
                stay = stay * causal
            local = jnp.dot(stay.astype(BF16), later_than, preferred_element_type=F32)
            return log_beta, local, jnp.sum(stay, axis=1, keepdims=True), v_ref[pl.ds(off, T), :]

        def weigh(J, sc, gone, acc, carr, diag):
            log_beta, local, _, v_j = sc
            w = jnp.exp((log_beta - gone) - local)
            if diag:
                w = w * causal
            acc = acc + jnp.dot(w.astype(BF16), v_j, preferred_element_type=F32)
            carr = jnp.where(lane == J, -gone[:T], carr)
            carr = jnp.where(lane == HEAD_DIM + J, -gone[T:], carr)
            return acc, carr

        def tiles(J, count, state, diag):
            gone, acc, carr, _ = state
            scs = [scores(J - u, diag and u == 0) for u in range(count)]
            for u, sc in enumerate(scs):
                acc, carr = weigh(J - u, sc, gone, acc, carr, diag and u == 0)
                gone = gone + sc[2]
            return gone, acc, carr, jnp.min(gone)

        U = SB_UNROLL
        alive = lambda st: st[3] < SB_DEAD
        state = (jnp.zeros((2 * T, 1), F32), jnp.zeros((2 * T, LANES), F32),
                 jnp.full((T, LANES), SB_UNSEEN, F32), jnp.zeros((), F32))
        state = lax.cond(I > 0, lambda st: tiles(I, 2, st, True), lambda st: tiles(I, 1, st, True), state)
        rest = jnp.maximum(I - 1, 0)
        singles = jnp.where(rest > 0, (rest - 1) % U + 1, 0)
        _, state = lax.while_loop(lambda c: (c[0] < singles) & alive(c[1]),
                                  lambda c: (c[0] + 1, tiles(I - 2 - c[0], 1, c[1], False)), (jnp.int32(0), state))
        blocks = (rest - singles) // U
        _, state = lax.while_loop(lambda c: (c[0] < blocks) & alive(c[1]),
                                  lambda c: (c[0] + 1, tiles(I - 2 - singles - U * c[0], U, c[1], False)),
                                  (jnp.int32(0), state))
        _, acc, carr, _ = state
        o_ref[...] = _sb_unstack_heads(acc)
        c_ref[...] = carr
        hosted.end(grid)

    blk = lambda I_off: pl.BlockSpec((T, LANES), lambda p, I: (I, I_off + p))
    full = lambda off: pl.BlockSpec((S, LANES), lambda p, I: (0, off + p))
    const = lambda rows: pl.BlockSpec((rows, T), lambda p, I: (0, 0))
    o, carries, *got = pl.pallas_call(
        body, name=name, grid=grid,
        out_shape=[jax.ShapeDtypeStruct((S, W), F32), jax.ShapeDtypeStruct((S, W), F32)] + hosted.out_shapes,
        in_specs=[blk(0), full(npair), full(2 * npair), const(2 * T), const(T)] + hosted.in_specs,
        out_specs=[blk(0), blk(0)] + hosted.out_specs,
        scratch_shapes=hosted.scratch,
        compiler_params=_cparams(hosted.semantics(("parallel", "arbitrary"))),
    )(p_sb, p_sb, p_sb, _sb_causal(T).astype(F32), _sb_triangle(T, True), *hosted.operands)
    return (o, carries) if comm is None else (o, carries, got)


def _sb_bwd(p_sb, do, carries, *, name, comm=None):
    S = p_sb.shape[0]
    W = p_sb.shape[1] // 3
    npair = W // LANES
    T = SB_TILE
    n_tiles = S // T
    scale = HEAD_DIM ** -0.5

    grid = (npair, n_tiles)
    hosted = _Hosted(comm, n_in=8, n_out=3, n_scratch=0)

    def body(*refs):
        (q_ref, k_ref, v_ref, do_ref, c_ref, causal_ref, later_ref, earlier_ref,
         dq_ref, dk_ref, dv_ref) = hosted.begin(refs, grid)
        I = pl.program_id(1)

        @pl.when(I == 0)
        def _():
            dk_ref[...] = jnp.zeros_like(dk_ref)
            dv_ref[...] = jnp.zeros_like(dv_ref)

        lane = lax.broadcasted_iota(jnp.int32, (T, LANES), 1)
        causal = causal_ref[...]
        later_than = later_ref[...]
        earlier_than = earlier_ref[...]
        q2 = _sb_stack_heads(q_ref[...], scale)
        do2 = _sb_stack_heads(do_ref[...].astype(BF16))
        carr = c_ref[...]
        tn_dims = (((0,), (0,)), ((), ()))

        def chain(J, diag):
            off = pl.multiple_of(J * T, T)
            k_j = k_ref[pl.ds(off, T), :]
            v_j = v_ref[pl.ds(off, T), :]
            log_beta, stay = _sb_scores(q2, k_j)
            if diag:
                stay = stay * causal
            lc = jnp.concatenate(
                [jnp.sum(jnp.where(lane == HEAD_DIM * h + J, carr, 0.0), axis=1, keepdims=True) for h in range(2)],
                axis=0)
            w = jnp.exp((log_beta + lc) - jnp.dot(stay.astype(BF16), later_than, preferred_element_type=F32))
            if diag:
                w = w * causal
            dw = lax.dot_general(do2, v_j, (((1,), (1,)), ((), ())), preferred_element_type=F32)
            e = w * dw
            local = jnp.dot(e.astype(BF16), earlier_than, preferred_element_type=F32)
            return off, k_j, w, e, local, jnp.exp(log_beta), jnp.sum(e, axis=1, keepdims=True)

        def finish(ch, ec, dq_acc, diag):
            off, k_j, w, e, local, beta, _ = ch
            e_before = local + ec
            dz = e - beta * (e + e_before)
            if diag:
                dz = dz * causal
            dzb = dz.astype(BF16)
            dq_acc = dq_acc + jnp.dot(dzb, k_j, preferred_element_type=F32)
            dk_ref[pl.ds(off, T), :] += lax.dot_general(dzb, q2, tn_dims, preferred_element_type=F32)
            dv_ref[pl.ds(off, T), :] += lax.dot_general(w.astype(BF16), do2, tn_dims, preferred_element_type=F32)
            return dq_acc

        def tiles(J, count, state, diag):
            ec, dq_acc = state
            chains = [chain(J + u, diag and u == count - 1) for u in range(count)]
            for u, ch in enumerate(chains):
                dq_acc = finish(ch, ec, dq_acc, diag and u == count - 1)
                ec = ec + ch[6]
            return ec, dq_acc

        lane_row = lax.broadcasted_iota(jnp.int32, (1, LANES), 1)
        reached = (jnp.max(carr, axis=0, keepdims=True) > 0.5 * SB_UNSEEN) & (lane_row < HEAD_DIM)
        first = jnp.min(jnp.where(reached, lane_row.astype(F32), float(n_tiles))).astype(jnp.int32)
        U = SB_UNROLL
        count = I - first
        rest = jnp.maximum(count - 1, 0)
        state = (jnp.zeros((2 * T, 1), F32), jnp.zeros((2 * T, LANES), F32))
        state = lax.fori_loop(0, rest // U, lambda jj, st: tiles(first + U * jj, U, st, False), state)
        state = lax.fori_loop(0, rest % U, lambda r, st: tiles(I - 1 - rest % U + r, 1, st, False), state)
        _, dq_acc = lax.cond(count > 0, lambda st: tiles(I - 1, 2, st, True), lambda st: tiles(I, 1, st, True), state)
        dq_ref[...] = _sb_unstack_heads(dq_acc) * scale
        hosted.end(grid)

    blk = lambda src_off: pl.BlockSpec((T, LANES), lambda p, I: (I, src_off + p))
    full = lambda off: pl.BlockSpec((S, LANES), lambda p, I: (0, off + p))
    const = lambda rows: pl.BlockSpec((rows, T), lambda p, I: (0, 0))
    dq, dk, dv, *got = pl.pallas_call(
        body, name=name, grid=grid,
        out_shape=[jax.ShapeDtypeStruct((S, W), F32)] * 3 + hosted.out_shapes,
        in_specs=[blk(0), full(npair), full(2 * npair), blk(0), blk(0), const(2 * T), const(T), const(T)]
        + hosted.in_specs,
        out_specs=[blk(0), full(0), full(0)] + hosted.out_specs,
        scratch_shapes=hosted.scratch,
        compiler_params=_cparams(hosted.semantics(("parallel", "arbitrary"))),
    )(p_sb, p_sb, p_sb, do, carries, _sb_causal(T).astype(F32), _sb_triangle(T, True), _sb_triangle(T, False),
      *hosted.operands)
    return (dq, dk, dv) if comm is None else (dq, dk, dv, got)


def _dil_blocks(b, body_fn):
    for pi, (window, dil) in enumerate(DILATED_PATTERNS):
        assert window // dil == DIL_BLOCK
        nblk = DIL_SUPER // (DIL_BLOCK * dil)
        assert (dil * nblk) % DIL_UNROLL == 0

        def group(g, _, pi=pi, dil=dil, nblk=nblk):
            for u in range(DIL_UNROLL):
                t = g * DIL_UNROLL + u
                n = t % nblk
                body_fn(pi, dil, t // nblk, n, b * nblk + n)
            return 0

        lax.fori_loop(0, dil * nblk // DIL_UNROLL, group, 0)


def _dil_rows(start, size, dil):
    if dil == 1:
        return pl.ds(pl.multiple_of(start, DIL_BLOCK), size)
    return pl.ds(start, size, stride=dil)


def _dil_fill_bias(bias_ref):
    row = lax.broadcasted_iota(jnp.int32, (2 * DIL_BLOCK, 2 * DIL_BLOCK), 0)
    kk = lax.broadcasted_iota(jnp.int32, (2 * DIL_BLOCK, 2 * DIL_BLOCK), 1)
    qi = jnp.where(row >= DIL_BLOCK, row - DIL_BLOCK, row)
    for s in range(2):
        dist = s * DIL_BLOCK + qi - kk
        bias_ref[s] = jnp.where((dist >= 0) & (dist <= DIL_BLOCK), 0.0, NEG_BIG)


def _dl_fwd(p_dl, *, name):
    S, W = p_dl.shape[0], p_dl.shape[1] // 3
    npair = W // LANES
    nsuper = S // DIL_SUPER
    assert S % DIL_SUPER == 0 and S // max(d for _, d in DILATED_PATTERNS) >= 2 * DIL_BLOCK
    scale = HEAD_DIM ** -0.5
    npat = len(DILATED_PATTERNS)

    def body(q_ref, k_ref, v_ref, o_ref, l_ref, bias_ref, *pattern_refs):
        op_refs, lp_refs = pattern_refs[:npat], pattern_refs[npat:]
        b = pl.program_id(1)
        masks = _head_masks((DIL_BLOCK, LANES))
        pl.when(b == 0)(lambda: _dil_fill_bias(bias_ref))

        def block(pi, dil, c, n, gn):
            ws = jnp.maximum(gn - 1, 0)
            qrows = n * (DIL_BLOCK * dil) + c
            krows = ws * (DIL_BLOCK * dil) + c
            q_idx = _dil_rows(qrows, DIL_BLOCK, dil)
            k_idx = _dil_rows(krows, 2 * DIL_BLOCK, dil)
            qb = q_ref[q_idx, :]
            kb = k_ref[k_idx, :].astype(BF16)
            vb = v_ref[k_idx, :].astype(BF16)
            q2 = _sb_stack_heads(qb.astype(BF16), scale)
            z = lax.dot_general(q2, kb, (((1,), (1,)), ((), ())), preferred_element_type=F32) + bias_ref[gn - ws]
            m = jnp.max(z, axis=1, keepdims=True)
            p = jnp.exp(z - m)
            den = jnp.sum(p, axis=1, keepdims=True)
            acc = jnp.dot(p.astype(BF16), vb, preferred_element_type=F32)
            lse = m + jnp.log(den)
            op_refs[pi][q_idx, :] = _sb_unstack_heads(acc / den)
            lp_refs[pi][q_idx, :] = jnp.where(masks[0], lse[:DIL_BLOCK], lse[DIL_BLOCK:])

        _dil_blocks(b, block)
        lses = [r[...] for r in lp_refs]
        top = functools.reduce(jnp.maximum, lses)
        ws_ = [jnp.exp(l - top) for l in lses]
        den = functools.reduce(jnp.add, ws_)
        num = functools.reduce(jnp.add, [w * r[...] for r, w in zip(op_refs, ws_)])
        o_ref[...] = num / den
        l_ref[...] = top + jnp.log(den)

    blk = pl.BlockSpec((DIL_SUPER, LANES), lambda p, b: (b, p))
    full = lambda off: pl.BlockSpec((S, LANES), lambda p, b: (0, off + p))
    return pl.pallas_call(
        body, name=name, grid=(npair, nsuper),
        out_shape=[jax.ShapeDtypeStruct((S, W), F32)] * 2,
        in_specs=[blk, full(npair), full(2 * npair)], out_specs=[blk, blk],
        scratch_shapes=[pltpu.VMEM((2, 2 * DIL_BLOCK, 2 * DIL_BLOCK), F32)]
        + [pltpu.VMEM((DIL_SUPER, LANES), F32)] * (2 * npat),
        compiler_params=_cparams(("arbitrary", "arbitrary")),
    )(p_dl, p_dl, p_dl)


def _dl_bwd(p_dl, o, lse, do, *, name):
    S, W = p_dl.shape[0], p_dl.shape[1] // 3
    npair = W // LANES
    nsuper = S // DIL_SUPER
    scale = HEAD_DIM ** -0.5

    def body(q_ref, k_ref, v_ref, o_ref, l_ref, do_ref, dq_ref, dk_ref, dv_ref, delta_ref, bias_ref):
        b = pl.program_id(1)

        @pl.when(b == 0)
        def _():
            dk_ref[...] = jnp.zeros_like(dk_ref)
            dv_ref[...] = jnp.zeros_like(dv_ref)
            _dil_fill_bias(bias_ref)

        dq_ref[...] = jnp.zeros_like(dq_ref)
        prod = do_ref[...] * o_ref[...]
        delta = jnp.zeros_like(prod)
        for hm in _head_masks(prod.shape):
            delta = jnp.where(hm, jnp.sum(jnp.where(hm, prod, 0.0), axis=1, keepdims=True), delta)
        delta_ref[...] = delta

        def block(pi, dil, c, n, gn):
            ws = jnp.maximum(gn - 1, 0)
            qrows = n * (DIL_BLOCK * dil) + c
            krows = ws * (DIL_BLOCK * dil) + c
            q_idx = _dil_rows(qrows, DIL_BLOCK, dil)
            k_idx = _dil_rows(krows, 2 * DIL_BLOCK, dil)
            qb = q_ref[q_idx, :]
            dob = do_ref[q_idx, :]
            lb = l_ref[q_idx, :]
            db = delta_ref[q_idx, :]
            kb = k_ref[k_idx, :].astype(BF16)
            vb = v_ref[k_idx, :].astype(BF16)
            q2 = _sb_stack_heads(qb.astype(BF16), scale)
            do2 = _sb_stack_heads(dob.astype(BF16))
            lse2 = jnp.concatenate([lb[:, HEAD_DIM * h:HEAD_DIM * h + 1] for h in range(2)], axis=0)
            delta2 = jnp.concatenate([db[:, HEAD_DIM * h:HEAD_DIM * h + 1] for h in range(2)], axis=0)
            z = lax.dot_general(q2, kb, (((1,), (1,)), ((), ())), preferred_element_type=F32)
            p = jnp.exp((z + bias_ref[gn - ws]) - lse2)
            dp = lax.dot_general(do2, vb, (((1,), (1,)), ((), ())), preferred_element_type=F32)
            dzb = (p * (dp - delta2)).astype(BF16)
            tn_dims = (((0,), (0,)), ((), ()))
            dq_blk = _sb_unstack_heads(jnp.dot(dzb, kb, preferred_element_type=F32)) * scale
            dk_blk = lax.dot_general(dzb, q2, tn_dims, preferred_element_type=F32)
            dv_blk = lax.dot_general(p.astype(BF16), do2, tn_dims, preferred_element_type=F32)
            dq_ref[q_idx, :] = dq_ref[q_idx, :] + dq_blk
            dk_ref[k_idx, :] = dk_ref[k_idx, :] + dk_blk
            dv_ref[k_idx, :] = dv_ref[k_idx, :] + dv_blk

        _dil_blocks(b, block)

    blk = pl.BlockSpec((DIL_SUPER, LANES), lambda p, b: (b, p))
    full = lambda off: pl.BlockSpec((S, LANES), lambda p, b: (0, off + p))
    return pl.pallas_call(
        body, name=name, grid=(npair, nsuper),
        out_shape=[jax.ShapeDtypeStruct((S, W), F32)] * 3,
        in_specs=[blk, full(npair), full(2 * npair), blk, blk, blk], out_specs=[blk, full(0), full(0)],
        scratch_shapes=[pltpu.VMEM((DIL_SUPER, LANES), F32), pltpu.VMEM((2, 2 * DIL_BLOCK, 2 * DIL_BLOCK), F32)],
        compiler_params=_cparams(("arbitrary", "arbitrary")),
    )(p_dl, p_dl, p_dl, o, lse, do)


class _NoExchange:
    def gather(self, family):
        return None

    def gathered(self, family, got, weights):
        pass

    def send(self, family, grads):
        return None

    def received(self, family, got):
        pass


def _local_step(x, target, gains, weights, exchanges=None):
    S, D = x.shape
    ex = exchanges or _NoExchange()
    weights = dict(weights)
    d_sb = gains["sb_out_norm"].shape[1]
    d_dl = gains["dil_out_norm"].shape[1]
    cos_t, sin_t = _rope_tables(S)

    riders = ("ffn1_w_up", "ffn1_w_down", "mixer") if exchanges else (None, None, None)
    (x1, h2), saved1 = _ffn_fwd(x, gains["ffn1_norm"], weights, tag="ffn1", ex=ex, riders=riders,
                                first_rider="ffn1_w_gate" if exchanges else None,
                                next_gain=gains["mix_norm"])
    w_in = weights["w_in"]
    w_in_sb, w_in_dl = w_in[:, :3 * d_sb], w_in[:, 3 * d_sb:]
    w_out = weights["w_out"]
    p_sb = _mm(h2, w_in_sb, outs=(BF16,), name="proj_sb")

    def rope_qk(acc, _, cos, sin):
        return jnp.concatenate([_rotate(acc[:, :2 * d_dl], cos, sin, 1.0), acc[:, 2 * d_dl:]], axis=1)

    p_dl = _mm(h2, w_in_dl, lanes=(cos_t, sin_t), epilogue=rope_qk, name="proj_dl_rope")
    plan = ex.gather("ffn2" if exchanges else None)
    o_sb, carries, *got = _sb_fwd(p_sb, name="sb_fwd", comm=plan)
    ex.gathered("ffn2", got[0] if got else None, weights)
    o_dl, lse_dl = _dl_fwd(p_dl, name="dl_fwd")
    merged = _rms_fwd([o_sb, o_dl], [gains["sb_out_norm"], gains["dil_out_norm"]], name="out_norm")
    x2, h3 = _mm(merged, w_out, res=x1, rows=(gains["ffn2_norm"],), outs=(F32, BF16),
                 epilogue=_residual_then_norm(1.0), name="out_proj_norm")
    (dx3, d_final, loss_wide), saved2 = _ffn_fwd(x2, gains["ffn2_norm"], weights, tag="ffn2", ex=ex, h=h3,
                                                 head=(gains["final_norm"], target))
    loss_row = loss_wide[:, :LANES]

    dx2, d_ffn2_norm, dwg2, dwu2, dwd2 = _ffn_bwd(dx3, x2, gains["ffn2_norm"], weights, saved2, tag="ffn2", ex=ex)
    d_w_out = _mm(merged, dx2, ta=True, outs=(GRAD_WIRE,), name="d_w_out")
    d_merged = _mm(dx2, w_out, tb=True, name="d_merged")
    (do_sb, do_dl), (d_sb_norm, d_dl_norm) = _rms_bwd(
        d_merged, [o_sb, o_dl], [gains["sb_out_norm"], gains["dil_out_norm"]], None, name="out_norm_bwd")
    plan = ex.send("ffn2", dict(ffn2_w_gate=dwg2, ffn2_w_up=dwu2, ffn2_w_down=dwd2))
    dq_sb, dk_sb, dv_sb, *got = _sb_bwd(p_sb, do_sb, carries, name="sb_bwd", comm=plan)
    ex.received("ffn2", got[0] if got else None)
    dq_dl, dk_dl, dv_dl = _dl_bwd(p_dl, o_dl, lse_dl, do_dl, name="dl_bwd")
    d_proj = _join_d_proj([dq_sb, dk_sb, dv_sb, dq_dl, dk_dl, dv_dl], (3, 4), cos_t, sin_t, name="d_proj")
    d_w_in = _mm(h2, d_proj, ta=True, outs=(GRAD_WIRE,), name="d_w_in")
    dx1, d_mix_norm = _mm(d_proj, w_in, tb=True, extras=(x1, dx2), rows=(gains["mix_norm"],), row_sums=1,
                          epilogue=_rms_bwd_epilogue, name="dh_mix_norm_bwd")
    dx, d_ffn1_norm, dwg1, dwu1, dwd1 = _ffn_bwd(
        dx1, x, gains["ffn1_norm"], weights, saved1, tag="ffn1", ex=ex,
        rider=("mixer", dict(w_in=d_w_in, w_out=d_w_out)), spread=True)
    gain_grads = dict(ffn1_norm=d_ffn1_norm, mix_norm=d_mix_norm, sb_out_norm=d_sb_norm, dil_out_norm=d_dl_norm,
                      ffn2_norm=d_ffn2_norm, final_norm=d_final)
    weight_grads = dict(ffn1_w_gate=dwg1, ffn1_w_up=dwu1, ffn1_w_down=dwd1, w_in=d_w_in, w_out=d_w_out,
                        ffn2_w_gate=dwg2, ffn2_w_up=dwu2, ffn2_w_down=dwd2)
    return loss_row, dx, gain_grads, weight_grads


def _mesh_position():
    return lax.axis_index("x"), lax.axis_index("y"), lax.axis_index("c")


def _flip(coord, bit):
    return 1 - coord if bit else coord


RELATIONS = [(rx, ry, rc) for rx in (0, 1) for ry in (0, 1) for rc in (0, 1)][1:]


class _GatherPlan:
    def __init__(self, shards):
        n = len(shards)
        self.operands = list(shards)
        self.out_shapes = [jax.ShapeDtypeStruct((N_DEV,) + s.shape, s.dtype) for s in shards]
        self.scratch = [pltpu.SemaphoreType.DMA((n, 7)), pltpu.SemaphoreType.DMA((n, 7)),
                        pltpu.SemaphoreType.DMA((n,))]

    def _copies(self, in_refs, out_refs, sems):
        send_sems, recv_sems, local_sems = sems
        x, y, c = _mesh_position()
        me, sibling = (x, y, c), (x, y, 1 - c)
        chips = [(1 - x, y), (x, 1 - y), (1 - x, 1 - y)]
        plans = []
        for t, (x_ref, out_ref) in enumerate(zip(in_refs, out_refs)):
            def slot(px, py, pc, out_ref=out_ref):
                return out_ref.at[4 * px + 2 * py + pc]

            def copy(k, block, to, src=None, t=t, slot=slot):
                return pltpu.make_async_remote_copy(
                    src_ref=slot(*block) if src is None else src, dst_ref=slot(*block),
                    send_sem=send_sems.at[t, k], recv_sem=recv_sems.at[t, k],
                    device_id=to, device_id_type=pl.DeviceIdType.MESH)

            plans.append(dict(
                mine=pltpu.make_async_copy(x_ref, slot(*me), local_sems.at[t]),
                first=[copy(0, me, sibling, src=x_ref)]
                + [copy(1 + j, me, (*chip, c), src=x_ref) for j, chip in enumerate(chips)],
                over_ici=[copy(1 + j, (*chip, c), me) for j, chip in enumerate(chips)],
                passed=[copy(4 + j, (*chip, c), sibling) for j, chip in enumerate(chips)],
                from_sibling=[copy(0, sibling, me)] + [copy(4 + j, (*chip, 1 - c), me) for j, chip in enumerate(chips)]))
        return plans

    def start(self, in_refs, out_refs, sems):
        for p in self._copies(in_refs, out_refs, sems):
            p["mine"].start()
            for cp in p["first"]:
                cp.start()

    def finish(self, in_refs, out_refs, sems):
        plans = self._copies(in_refs, out_refs, sems)
        for p in plans:
            for arrived, onward in zip(p["over_ici"], p["passed"]):
                arrived.wait_recv()
                onward.start()
        for p in plans:
            for cp in p["from_sibling"]:
                cp.wait_recv()
            for cp in p["first"] + p["passed"]:
                cp.wait_send()
            p["mine"].wait()


class _Hosted:
    def __init__(self, plan, n_in, n_out, n_scratch):
        self.plan, self.n_in, self.n_out, self.n_scratch = plan, n_in, n_out, n_scratch
        self.operands = list(plan.operands) if plan else []
        self.out_shapes = list(plan.out_shapes) if plan else []
        self.scratch = list(plan.scratch) if plan else []
        self.in_specs = [pl.BlockSpec(memory_space=pl.ANY)] * len(self.operands)
        self.out_specs = [pl.BlockSpec(memory_space=pl.ANY)] * len(self.out_shapes)

    def semantics(self, sem):
        return sem if self.plan is None else ("arbitrary",) * len(sem)

    def _at(self, grid, last):
        hit = None
        for d, n in enumerate(grid):
            here = pl.program_id(d) == (n - 1 if last else 0)
            hit = here if hit is None else hit & here
        return hit

    def begin(self, refs, grid):
        if self.plan is None:
            return refs
        k_in, k_out = len(self.operands), len(self.out_shapes)
        ins, rest = refs[:self.n_in], refs[self.n_in:]
        c_in, rest = rest[:k_in], rest[k_in:]
        outs, rest = rest[:self.n_out], rest[self.n_out:]
        c_out, rest = rest[:k_out], rest[k_out:]
        scratch, sems = rest[:self.n_scratch], rest[self.n_scratch:]
        self._args = (c_in, c_out, sems)
        pl.when(self._at(grid, False))(lambda: self.plan.start(*self._args))
        return tuple(ins) + tuple(outs) + tuple(scratch)

    def end(self, grid):
        if self.plan is not None:
            pl.when(self._at(grid, True))(lambda: self.plan.finish(*self._args))


class _ExchangePlan:
    def __init__(self, packs):
        n = len(packs)
        self.operands = list(packs)
        self.out_shapes = [jax.ShapeDtypeStruct(p.shape, p.dtype) for p in packs]
        self.scratch = [pltpu.SemaphoreType.DMA((n, 7)), pltpu.SemaphoreType.DMA((n, 7)),
                        pltpu.SemaphoreType.DMA((n,))]

    def _copies(self, in_refs, out_refs, sems):
        send_sems, recv_sems, local_sems = sems
        x, y, c = _mesh_position()
        me = 4 * x + 2 * y + c
        copies = [pltpu.make_async_copy(i.at[me], o.at[me], local_sems.at[t])
                  for t, (i, o) in enumerate(zip(in_refs, out_refs))]
        for r, (rx, ry, rc) in enumerate(RELATIONS):
            px, py, pc = _flip(x, rx), _flip(y, ry), _flip(c, rc)
            peer = 4 * px + 2 * py + pc
            copies += [pltpu.make_async_remote_copy(
                src_ref=i.at[peer], dst_ref=o.at[me], send_sem=send_sems.at[t, r], recv_sem=recv_sems.at[t, r],
                device_id=(px, py, pc), device_id_type=pl.DeviceIdType.MESH)
                for t, (i, o) in enumerate(zip(in_refs, out_refs))]
        return copies

    def start(self, in_refs, out_refs, sems):
        for cp in self._copies(in_refs, out_refs, sems):
            cp.start()

    def finish(self, in_refs, out_refs, sems):
        for cp in self._copies(in_refs, out_refs, sems):
            cp.wait()


def _all_reduce_rows(v, *, name):
    R, C = v.shape

    def body(v_ref, out_ref, buf, send_sems, recv_sems):
        x, y, c = _mesh_position()
        me = 4 * x + 2 * y + c
        buf[me] = v_ref[...]
        copies = []
        for r, (rx, ry, rc) in enumerate(RELATIONS):
            cp = pltpu.make_async_remote_copy(
                src_ref=v_ref, dst_ref=buf.at[me], send_sem=send_sems.at[r], recv_sem=recv_sems.at[r],
                device_id=(_flip(x, rx), _flip(y, ry), _flip(c, rc)), device_id_type=pl.DeviceIdType.MESH)
            cp.start()
            copies.append(cp)
        for cp in copies:
            cp.wait()
        total = buf[0]
        for s in range(1, N_DEV):
            total = total + buf[s]
        out_ref[...] = total

    return pl.pallas_call(
        body, name=name,
        out_shape=jax.ShapeDtypeStruct((R, C), F32),
        in_specs=[pl.BlockSpec(memory_space=pltpu.VMEM)],
        out_specs=pl.BlockSpec(memory_space=pltpu.VMEM),
        scratch_shapes=[pltpu.VMEM((N_DEV, R, C), F32), pltpu.SemaphoreType.DMA((7,)), pltpu.SemaphoreType.DMA((7,))],
    )(v)


def _sum_slots(recv, *, name):
    _, R, C = recv.shape
    tr = _pick(R, (256, 208, 128, 64, 32, 16))

    def body(r_ref, o_ref):
        total = r_ref[0].astype(F32)
        for s in range(1, N_DEV):
            total = total + r_ref[s].astype(F32)
        o_ref[...] = total

    return pl.pallas_call(
        body, name=name, grid=(R // tr,),
        out_shape=jax.ShapeDtypeStruct((R, C), F32),
        in_specs=[pl.BlockSpec((N_DEV, tr, C), lambda i: (0, i, 0))],
        out_specs=pl.BlockSpec((tr, C), lambda i: (i, 0)),
        compiler_params=_cparams(("parallel",)),
    )(recv)


def _adamw(w, g, m, v, *, name):
    R, C = w.shape
    tr = _pick(R, (256, 128, 64, 32, 16, 8))

    def body(w_ref, g_ref, m_ref, v_ref, d_ref, nm_ref, nv_ref):
        g = g_ref[...]
        m_new = ADAM_B1 * m_ref[...] + (1.0 - ADAM_B1) * g
        v_new = ADAM_B2 * v_ref[...] + (1.0 - ADAM_B2) * (g * g)
        m_hat = m_new / (1.0 - ADAM_B1 ** ADAM_STEP)
        v_hat = v_new / (1.0 - ADAM_B2 ** ADAM_STEP)
        d_ref[...] = -ADAM_LR * (m_hat / (jnp.sqrt(v_hat) + ADAM_EPS) + ADAM_WD * w_ref[...])
        nm_ref[...] = m_new
        nv_ref[...] = v_new

    spec = pl.BlockSpec((tr, C), lambda i: (i, 0))
    return pl.pallas_call(
        body, name=name, grid=(R // tr,),
        out_shape=[jax.ShapeDtypeStruct((R, C), F32)] * 3,
        in_specs=[spec] * 4, out_specs=[spec] * 3,
        compiler_params=_cparams(("parallel",)),
    )(w, g, m, v)


WEIGHT_NAMES = ["ffn1_norm", "ffn1_w_gate", "ffn1_w_up", "ffn1_w_down", "mix_norm", "w_in", "sb_out_norm",
                "dil_out_norm", "w_out", "ffn2_norm", "ffn2_w_gate", "ffn2_w_up", "ffn2_w_down", "final_norm"]
GAIN_NAMES = ["ffn1_norm", "mix_norm", "sb_out_norm", "dil_out_norm", "ffn2_norm", "final_norm"]
COL_SHARDED = ["ffn1_w_gate", "ffn1_w_up", "ffn2_w_gate", "ffn2_w_up", "w_in"]
ROW_SHARDED = ["ffn1_w_down", "ffn2_w_down", "w_out"]
GROUPS = {"mixer": (["w_in"], ["w_out"]),
          "ffn2": (["ffn2_w_gate", "ffn2_w_up"], ["ffn2_w_down"])}
for _ffn in ("ffn1", "ffn2"):
    GROUPS.update({f"{_ffn}_w_gate": ([f"{_ffn}_w_gate"], []), f"{_ffn}_w_up": ([f"{_ffn}_w_up"], []),
                   f"{_ffn}_w_down": ([], [f"{_ffn}_w_down"])})


class _Exchanges:
    def __init__(self, params):
        self.params = params
        self.grads = {}

    def gather(self, group):
        if group is None:
            return None
        cols, rows = GROUPS[group]
        packs = []
        if cols:
            packs.append(jnp.concatenate([self.params[n] for n in cols], axis=1).astype(BF16))
        if rows:
            packs.append(jnp.concatenate([self.params[n] for n in rows], axis=0).astype(BF16))
        return _GatherPlan(packs)

    def gathered(self, group, got, weights):
        if group is None:
            return
        cols, rows = GROUPS[group]
        got = list(got)
        if cols:
            col_all, off = got.pop(0), 0
            for n in cols:
                w = self.params[n].shape[1]
                piece = col_all[:, :, off:off + w]
                weights[n] = jnp.transpose(piece, (1, 0, 2)).reshape(piece.shape[1], N_DEV * w)
                off += w
        if rows:
            row_all, off = got.pop(0), 0
            for n in rows:
                r = self.params[n].shape[0]
                weights[n] = row_all[:, off:off + r, :].reshape(N_DEV * r, row_all.shape[2])
                off += r

    def send(self, group, grads):
        if group is None:
            return None
        cols, rows = GROUPS[group]
        packs = []
        if cols:
            chunks = [jnp.transpose(grads[n].reshape(grads[n].shape[0], N_DEV, self.params[n].shape[1]), (1, 0, 2))
                      for n in cols]
            packs.append(jnp.concatenate(chunks, axis=2).astype(GRAD_WIRE))
        if rows:
            chunks = [grads[n].reshape(N_DEV, self.params[n].shape[0], grads[n].shape[1]) for n in rows]
            packs.append(jnp.concatenate(chunks, axis=1).astype(GRAD_WIRE))
        return _ExchangePlan(packs)

    def received(self, group, got):
        if group is None:
            return
        cols, rows = GROUPS[group]
        got = list(got)
        if cols:
            col_grad, off = _sum_slots(got.pop(0), name=f"sum_col_grads_{group}"), 0
            for n in cols:
                w = self.params[n].shape[1]
                self.grads[n] = col_grad[:, off:off + w]
                off += w
        if rows:
            row_grad, off = _sum_slots(got.pop(0), name=f"sum_row_grads_{group}"), 0
            for n in rows:
                r = self.params[n].shape[0]
                self.grads[n] = row_grad[off:off + r, :]
                off += r


def _step(x, target, params, moments_m, moments_v):
    ex = _Exchanges(params)
    weights = {}
    gains = {n: params[n] for n in GAIN_NAMES}
    loss_row, grad_x, gain_grads, _ = _local_step(x, target, gains, weights, ex)
    grads = ex.grads

    rows = [gain_grads[n].reshape(-1, LANES) for n in GAIN_NAMES] + [loss_row]
    small = jnp.concatenate(rows, axis=0)
    pad = (-small.shape[0]) % 8
    small = jnp.pad(small, ((0, pad), (0, 0)))
    small = _all_reduce_rows(small, name="reduce_gains_loss")
    off = 0
    for n in GAIN_NAMES:
        r = gain_grads[n].shape[1] // LANES
        grads[n] = small[off:off + r].reshape(1, -1)
        off += r
    loss = small[off, 0]

    delta, new_m, new_v = {}, {}, {}
    for n in WEIGHT_NAMES:
        delta[n], new_m[n], new_v[n] = _adamw(params[n], grads[n], moments_m[n], moments_v[n], name=f"adamw_{n}")
    return loss, grad_x, grads, delta, new_m, new_v


def kernel(x, ffn1_norm, ffn1_w_gate, ffn1_w_up, ffn1_w_down, mix_norm, w_in, sb_out_norm, dil_out_norm, w_out, ffn2_norm, ffn2_w_gate, ffn2_w_up, ffn2_w_down, final_norm, loss_target, m_ffn1_norm, m_ffn1_w_gate, m_ffn1_w_up, m_ffn1_w_down, m_mix_norm, m_w_in, m_sb_out_norm, m_dil_out_norm, m_w_out, m_ffn2_norm, m_ffn2_w_gate, m_ffn2_w_up, m_ffn2_w_down, m_final_norm, v_ffn1_norm, v_ffn1_w_gate, v_ffn1_w_up, v_ffn1_w_down, v_mix_norm, v_w_in, v_sb_out_norm, v_dil_out_norm, v_w_out, v_ffn2_norm, v_ffn2_w_gate, v_ffn2_w_up, v_ffn2_w_down, v_final_norm):
    given = dict(locals())
    shapes = {n: given[n].shape for n in WEIGHT_NAMES}

    def as2d(a):
        return a.reshape(1, -1) if a.ndim == 1 else a.reshape(a.shape[-2], a.shape[-1])

    params = {n: as2d(given[n]) for n in WEIGHT_NAMES}
    moments_m = {n: as2d(given["m_" + n]) for n in WEIGHT_NAMES}
    moments_v = {n: as2d(given["v_" + n]) for n in WEIGHT_NAMES}
    loss, grad_x, grads, delta, new_m, new_v = _step(x[0], loss_target[0], params, moments_m, moments_v)
    back = lambda d: [d[n].reshape(shapes[n]) for n in WEIGHT_NAMES]
    return (loss, grad_x[None], *back(grads), *back(delta), *back(new_m), *back(new_v))
```

```python
import functools

import jax
import jax.numpy as jnp
from jax import lax
from jax.experimental import pallas as pl
from jax.experimental.pallas import tpu as pltpu

F32 = jnp.float32
BF16 = jnp.bfloat16
GRAD_WIRE = jnp.bfloat16

N_DEV = 8
HEAD_DIM = 64
LANES = 128
DILATED_PATTERNS = ((128, 1), (512, 4), (2048, 16))
DIL_BLOCK = 128
DIL_SUPER = 2048
DIL_UNROLL = 8
SB_TILE = 256
SB_UNROLL = 4
SB_DEAD = 90.0
SB_UNSEEN = -1e30
ROPE_THETA = 10000.0
RMS_EPS = 1e-6
HALF_STEP = 0.5
ADAM_LR = 0.001
ADAM_B1 = 0.9
ADAM_B2 = 0.999
ADAM_EPS = 1e-08
ADAM_WD = 0.01
ADAM_STEP = 10
NEG_BIG = -1e30
VMEM_CAP_MB = 60


def _pick(n, prefs):
    for p in prefs:
        if n % p == 0:
            return p
    return n


MM_MAX_TILE = 1536
MM_WHOLE = 3072


def _largest_tile(n, cap):
    if n <= cap:
        return n
    for t in range(cap - cap % LANES, 0, -LANES):
        if n % t == 0:
            return t
    return n


def _cparams(sem=None, vmem_mb=48):
    return pltpu.CompilerParams(dimension_semantics=sem, vmem_limit_bytes=min(vmem_mb, VMEM_CAP_MB) * 1024 * 1024)


def _nbytes(shape, dtype):
    n = 1
    for s in shape:
        n *= s
    return n * jnp.dtype(dtype).itemsize


def _mm(a, b, *, name, ta=False, tb=False, outs=(F32,), res=None, alpha=1.0, extras=(), epilogue=None,
        tm=None, tn=None, tk=None, comm=None, rows=(), lanes=(), row_sums=0):
    if ta:
        K, M = a.shape
    else:
        M, K = a.shape
    if tb:
        N, Kb = b.shape
    else:
        Kb, N = b.shape
    assert K == Kb, (a.shape, b.shape, ta, tb)
    tn = tn or (N if (not ta and K <= MM_WHOLE and N <= MM_WHOLE) else _largest_tile(N, MM_MAX_TILE))
    tm = tm or (_largest_tile(M, MM_MAX_TILE) if ta else _pick(M, (512, 256, 128) if tn <= MM_MAX_TILE else (256, 128)))
    tk = tk or (K if K <= MM_WHOLE else _pick(K, (2048, 1024, 512, 256, 128)))
    nk = K // tk
    a_spec = pl.BlockSpec((tk, tm), lambda i, j, k: (k, i)) if ta else pl.BlockSpec((tm, tk), lambda i, j, k: (i, k))
    b_spec = pl.BlockSpec((tn, tk), lambda i, j, k: (j, k)) if tb else pl.BlockSpec((tk, tn), lambda i, j, k: (k, j))
    mn_spec = pl.BlockSpec((tm, tn), lambda i, j, k: (i, j))
    dims = (((0 if ta else 1,), (1 if tb else 0,)), ((), ()))
    row_spec = pl.BlockSpec((1, tn), lambda i, j, k: (0, j))
    lane_spec = pl.BlockSpec((tm, LANES), lambda i, j, k: (i, 0))
    n_extra = len(extras) + (1 if res is not None else 0) + len(rows) + len(lanes)
    n_mn = len(outs)
    n_out = n_mn + row_sums
    assert row_sums == 0 or tn == N
    grid = (M // tm, N // tn, nk)
    hosted = _Hosted(comm, n_in=2 + n_extra, n_out=n_out, n_scratch=1 if nk > 1 else 0)

    def body(*refs):
        a_ref, b_ref = refs[0], refs[1]
        in_refs = refs[2:2 + n_extra]
        refs = hosted.begin(refs, grid)
        out_refs = refs[2 + n_extra:2 + n_extra + n_out]
        prod = lax.dot_general(a_ref[...].astype(BF16), b_ref[...].astype(BF16), dims, preferred_element_type=F32)

        def finish(acc):
            blocks = [r[...] for r in in_refs]
            if res is not None:
                r_blk, blocks = blocks[0], blocks[1:]
            else:
                r_blk = None
            if epilogue is None:
                val = acc * alpha
                if r_blk is not None:
                    val = val + r_blk
                vals = (val,)
            else:
                vals = epilogue(acc, r_blk, *blocks)
                vals = vals if isinstance(vals, (tuple, list)) else (vals,)
            for o_ref, v in zip(out_refs[:n_mn], vals[:n_mn]):
                o_ref[...] = v.astype(o_ref.dtype)
            first_rows = pl.program_id(0) == 0
            for o_ref, part in zip(out_refs[n_mn:], vals[n_mn:]):
                @pl.when(first_rows)
                def _(o_ref=o_ref, part=part):
                    o_ref[...] = part

                @pl.when(jnp.logical_not(first_rows))
                def _(o_ref=o_ref, part=part):
                    o_ref[...] += part

        if nk == 1:
            finish(prod)
        else:
            acc_ref = refs[2 + n_extra + n_out]
            k = pl.program_id(2)

            @pl.when(k == 0)
            def _():
                acc_ref[...] = prod

            @pl.when(k > 0)
            def _():
                acc_ref[...] += prod

            @pl.when(k == nk - 1)
            def _():
                finish(acc_ref[...])

        hosted.end(grid)

    mn_operands = ([res] if res is not None else []) + list(extras)
    operands = [a, b] + mn_operands + list(rows) + list(lanes)
    in_specs = [a_spec, b_spec] + [mn_spec] * len(mn_operands) + [row_spec] * len(rows) + [lane_spec] * len(lanes)
    est = 2 * (_nbytes((tm, tk), a.dtype) + _nbytes((tk, tn), b.dtype))
    est += 2 * sum(_nbytes((tm, tn), o.dtype) for o in mn_operands)
    est += 2 * sum(_nbytes((tm, tn), d) for d in outs) + 2 * _nbytes((tm, tn), F32)
    semantics = ("parallel", "parallel", "arbitrary") if row_sums == 0 else ("arbitrary",) * 3
    result = pl.pallas_call(
        body, name=name, grid=grid,
        out_shape=[jax.ShapeDtypeStruct((M, N), d) for d in outs]
        + [jax.ShapeDtypeStruct((1, N), F32)] * row_sums + hosted.out_shapes,
        in_specs=in_specs + hosted.in_specs,
        out_specs=[mn_spec] * n_mn + [row_spec] * row_sums + hosted.out_specs,
        scratch_shapes=([pltpu.VMEM((tm, tn), F32)] if nk > 1 else []) + hosted.scratch,
        compiler_params=_cparams(hosted.semantics(semantics), vmem_mb=max(32, 2 * est // (1024 * 1024))),
    )(*operands, *hosted.operands)
    own, got = result[:n_out], list(result[n_out:])
    own = own[0] if n_out == 1 else own
    return own if comm is None else (own, got)


def _rms_hat(x):
    r = lax.rsqrt(jnp.mean(x * x, axis=-1, keepdims=True) + RMS_EPS)
    return x * r, r


def _rms_fwd(xs, gains, *, name, comm=None):
    S = xs[0].shape[0]
    widths = [x.shape[1] for x in xs]
    tm = _pick(S, (512, 256, 128))
    n = len(xs)
    grid = (S // tm,)
    hosted = _Hosted(comm, n_in=2 * n, n_out=1, n_scratch=0)

    def body(*refs):
        refs = hosted.begin(refs, grid)
        o_ref = refs[2 * n]
        off = 0
        for i in range(n):
            xh, _ = _rms_hat(refs[i][...])
            o_ref[:, off:off + widths[i]] = (xh * refs[n + i][...]).astype(o_ref.dtype)
            off += widths[i]
        hosted.end(grid)

    out, *got = pl.pallas_call(
        body, name=name, grid=grid,
        out_shape=[jax.ShapeDtypeStruct((S, sum(widths)), BF16)] + hosted.out_shapes,
        in_specs=[pl.BlockSpec((tm, w), lambda i: (i, 0)) for w in widths]
        + [pl.BlockSpec((1, w), lambda i: (0, 0)) for w in widths] + hosted.in_specs,
        out_specs=[pl.BlockSpec((tm, sum(widths)), lambda i: (i, 0))] + hosted.out_specs,
        scratch_shapes=hosted.scratch,
        compiler_params=_cparams(hosted.semantics(("parallel",))),
    )(*xs, *gains, *hosted.operands)
    return out if comm is None else (out, got)


def _rms_bwd(dh, xs, gains, res, *, name):
    S = xs[0].shape[0]
    widths = [x.shape[1] for x in xs]
    tm = _pick(S, (512, 256, 128))
    n = len(xs)
    has_res = res is not None

    def body(*refs):
        dh_ref = refs[0]
        x_refs = refs[1:1 + n]
        g_refs = refs[1 + n:1 + 2 * n]
        r_ref = refs[1 + 2 * n] if has_res else None
        base = 1 + 2 * n + (1 if has_res else 0)
        dx_refs = refs[base:base + n]
        dg_refs = refs[base + n:base + 2 * n]
        first = pl.program_id(0) == 0
        off = 0
        for i in range(n):
            x = x_refs[i][...]
            xh, r = _rms_hat(x)
            d = dh_ref[:, off:off + widths[i]]
            dxh = d * g_refs[i][...]
            dx = r * (dxh - xh * jnp.mean(dxh * xh, axis=-1, keepdims=True))
            if has_res:
                dx = dx + r_ref[...]
            dx_refs[i][...] = dx
            part = jnp.sum(d * xh, axis=0, keepdims=True)

            @pl.when(first)
            def _(i=i, part=part):
                dg_refs[i][...] = part

            @pl.when(jnp.logical_not(first))
            def _(i=i, part=part):
                dg_refs[i][...] += part

            off += widths[i]

    in_specs = [pl.BlockSpec((tm, sum(widths)), lambda i: (i, 0))]
    in_specs += [pl.BlockSpec((tm, w), lambda i: (i, 0)) for w in widths]
    in_specs += [pl.BlockSpec((1, w), lambda i: (0, 0)) for w in widths]
    operands = [dh, *xs, *gains]
    if has_res:
        in_specs.append(pl.BlockSpec((tm, widths[0]), lambda i: (i, 0)))
        operands.append(res)
    out = pl.pallas_call(
        body, name=name, grid=(S // tm,),
        out_shape=[jax.ShapeDtypeStruct((S, w), F32) for w in widths] + [jax.ShapeDtypeStruct((1, w), F32) for w in widths],
        in_specs=in_specs,
        out_specs=[pl.BlockSpec((tm, w), lambda i: (i, 0)) for w in widths]
        + [pl.BlockSpec((1, w), lambda i: (0, 0)) for w in widths],
        compiler_params=_cparams(("arbitrary",)),
    )(*operands)
    return out[:n], out[n:]


def _sigmoid(g):
    return 1.0 / (1.0 + jnp.exp(-g))


def _ride(result, plan):
    return result if plan is not None else (result, None)


def _residual_then_norm(alpha):
    def epilogue(acc, res, gain):
        y = res + alpha * acc
        return y, _rms_hat(y)[0] * gain
    return epilogue


def _ffn_fwd(x, gain, w, *, tag, ex, first_rider=None, riders=(None, None, None), head=None, h=None,
             next_gain=None):
    if h is None:
        plan = ex.gather(first_rider)
        h, got = _ride(_rms_fwd([x], [gain], name=f"{tag}_norm", comm=plan), plan)
        ex.gathered(first_rider, got, w)
    plan = ex.gather(riders[0])
    g, got = _ride(_mm(h, w[f"{tag}_w_gate"], outs=(BF16,), name=f"{tag}_gate", comm=plan), plan)
    ex.gathered(riders[0], got, w)

    def act(acc, _, g_blk):
        gf = g_blk.astype(F32)
        return acc, gf * _sigmoid(gf) * acc

    plan = ex.gather(riders[1])
    (u, a), got = _ride(_mm(h, w[f"{tag}_w_up"], outs=(BF16, BF16), extras=(g,), epilogue=act, name=f"{tag}_up_act",
                            comm=plan), plan)
    ex.gathered(riders[1], got, w)
    plan = ex.gather(riders[2])
    if head is None and next_gain is None:
        y, got = _ride(_mm(a, w[f"{tag}_w_down"], res=x, alpha=HALF_STEP, name=f"{tag}_down", comm=plan), plan)
    elif head is None:
        y, got = _ride(_mm(a, w[f"{tag}_w_down"], res=x, rows=(next_gain,), outs=(F32, BF16),
                           epilogue=_residual_then_norm(HALF_STEP), name=f"{tag}_down_norm", comm=plan), plan)
    else:
        final_gain, target = head
        y, got = _ride(_mm(a, w[f"{tag}_w_down"], res=x, extras=(target,), rows=(final_gain,), row_sums=2,
                           epilogue=_loss_head_epilogue, name=f"{tag}_down_loss", comm=plan), plan)
    ex.gathered(riders[2], got, w)
    return y, (h, g, u, a)


def _loss_head_epilogue(acc, x_in, target, gain):
    xh, r = _rms_hat(x_in + HALF_STEP * acc)
    err = xh * gain - target
    dy = err * (1.0 / acc.shape[1])
    dxh = dy * gain
    dx = r * (dxh - xh * jnp.mean(dxh * xh, axis=-1, keepdims=True))
    loss = 0.5 * jnp.sum(jnp.mean(err * err, axis=-1, keepdims=True), axis=0, keepdims=True)
    return dx, jnp.sum(dy * xh, axis=0, keepdims=True), jnp.zeros_like(gain) + loss


def _rms_bwd_epilogue(acc, dh_so_far, x, dres, gain):
    dh = acc if dh_so_far is None else acc + dh_so_far
    xh, r = _rms_hat(x)
    dxh = dh * gain
    dx = r * (dxh - xh * jnp.mean(dxh * xh, axis=-1, keepdims=True)) + dres
    return dx, jnp.sum(dh * xh, axis=0, keepdims=True)


def _ffn_bwd(dout, x, gain, w, saved, *, tag, ex, rider=(None, None), spread=False):
    h, g, u, a = saved
    wg, wu, wd = (w[f"{tag}_w_{n}"] for n in ("gate", "up", "down"))

    def act_bwd(acc, _, g_blk, u_blk):
        gf, uf = g_blk.astype(F32), u_blk.astype(F32)
        da = acc * HALF_STEP
        sig = _sigmoid(gf)
        silu = gf * sig
        return da * uf * (sig + silu * (1.0 - sig)), da * silu

    def carrying(group, grad, call):
        group = group if spread else None
        plan = ex.send(group, {group: grad})
        out, got = _ride(call(plan), plan)
        ex.received(group, got)
        return out

    plan = ex.send(*rider)
    (dg, du), got = _ride(_mm(dout, wd, tb=True, outs=(BF16, BF16), extras=(g, u), epilogue=act_bwd,
                              name=f"{tag}_bwd_act", comm=plan), plan)
    ex.received(rider[0], got)
    dwg = _mm(h, dg, ta=True, outs=(GRAD_WIRE,), name=f"{tag}_dwg")
    dwu = carrying(f"{tag}_w_gate", dwg, lambda plan: _mm(h, du, ta=True, outs=(GRAD_WIRE,), name=f"{tag}_dwu", comm=plan))
    dwd = carrying(f"{tag}_w_up", dwu,
                   lambda plan: _mm(a, dout, ta=True, outs=(GRAD_WIRE,), alpha=HALF_STEP, name=f"{tag}_dwd", comm=plan))
    dh = carrying(f"{tag}_w_down", dwd, lambda plan: _mm(dg, wg, tb=True, name=f"{tag}_dh_gate", comm=plan))
    dx, dgain = _mm(du, wu, tb=True, res=dh, extras=(x, dout), rows=(gain,), row_sums=1, epilogue=_rms_bwd_epilogue,
                    name=f"{tag}_dh_up_norm_bwd")
    return dx, dgain, dwg, dwu, dwd


def _rope_tables(S):
    half = HEAD_DIM // 2
    inv_freq = ROPE_THETA ** (-jnp.arange(half, dtype=F32) / half)
    ang = jnp.arange(S, dtype=F32)[:, None] * inv_freq[None, :]
    cos, sin = jnp.cos(ang), jnp.sin(ang)
    reps = LANES // HEAD_DIM
    cos_t = jnp.tile(jnp.concatenate([cos, cos], axis=1), (1, reps))
    sin_t = jnp.tile(jnp.concatenate([-sin, sin], axis=1), (1, reps))
    return cos_t, sin_t


def _rotate(v, cos, sin, sign):
    half = HEAD_DIM // 2
    groups = []
    for g in range(v.shape[1] // LANES):
        t = v[:, g * LANES:(g + 1) * LANES]
        lane = lax.broadcasted_iota(jnp.int32, t.shape, 1)
        swapped = jnp.where(lane % HEAD_DIM < half, pltpu.roll(t, LANES - half, axis=1), pltpu.roll(t, half, axis=1))
        groups.append(t * cos + swapped * (sin * sign))
    return groups[0] if len(groups) == 1 else jnp.concatenate(groups, axis=1)


def _join_d_proj(pieces, rotated, cos_t, sin_t, *, name):
    S = pieces[0].shape[0]
    widths = [p.shape[1] for p in pieces]
    tm = _pick(S, (256, 128))
    n = len(pieces)

    def body(*refs):
        c_ref, s_ref, o_ref = refs[n], refs[n + 1], refs[n + 2]
        off = 0
        for i in range(n):
            v = refs[i][...]
            if i in rotated:
                v = _rotate(v, c_ref[...], s_ref[...], -1.0)
            o_ref[:, off:off + widths[i]] = v.astype(o_ref.dtype)
            off += widths[i]

    return pl.pallas_call(
        body, name=name, grid=(S // tm,),
        out_shape=jax.ShapeDtypeStruct((S, sum(widths)), BF16),
        in_specs=[pl.BlockSpec((tm, w), lambda i: (i, 0)) for w in widths]
        + [pl.BlockSpec((tm, LANES), lambda i: (i, 0))] * 2,
        out_specs=pl.BlockSpec((tm, sum(widths)), lambda i: (i, 0)),
        compiler_params=_cparams(("parallel",)),
    )(*pieces, cos_t, sin_t)


def _head_masks(shape):
    lane = lax.broadcasted_iota(jnp.int32, shape, 1)
    return [(lane >= HEAD_DIM * h) & (lane < HEAD_DIM * (h + 1)) for h in range(LANES // HEAD_DIM)]


def _sb_scores(q2, k_j):
    z = lax.dot_general(q2, k_j, (((1,), (1,)), ((), ())), preferred_element_type=F32)
    sign_bit = jnp.int32(-2 ** 31)
    minus_abs = lax.bitcast_convert_type(lax.bitcast_convert_type(z, jnp.int32) | sign_bit, F32)
    softplus = jnp.maximum(z, 0.0) + jnp.log(1.0 + jnp.exp(minus_abs))
    return z - softplus, softplus


def _sb_stack_heads(t, scale=None):
    parts = [jnp.where(hm, t, jnp.zeros_like(t)) for hm in _head_masks(t.shape)]
    t2 = jnp.concatenate(parts, axis=0)
    if scale is not None:
        t2 = (t2.astype(F32) * scale).astype(t2.dtype)
    return t2


def _sb_unstack_heads(t2):
    T = t2.shape[0] // 2
    masks = _head_masks((T, LANES))
    return jnp.where(masks[0], t2[:T], t2[T:])


def _sb_causal(T):
    row = lax.broadcasted_iota(jnp.int32, (2 * T, T), 0)
    col = lax.broadcasted_iota(jnp.int32, (2 * T, T), 1)
    return col < jnp.where(row >= T, row - T, row)


def _sb_triangle(T, later):
    row = lax.broadcasted_iota(jnp.int32, (T, T), 0)
    col = lax.broadcasted_iota(jnp.int32, (T, T), 1)
    return ((row > col) if later else (row < col)).astype(BF16)


def _sb_fwd(p_sb, *, name, comm=None):
    S = p_sb.shape[0]
    W = p_sb.shape[1] // 3
    npair = W // LANES
    T = SB_TILE
    n_tiles = S // T
    assert n_tiles <= HEAD_DIM
    scale = HEAD_DIM ** -0.5

    grid = (npair, n_tiles)
    hosted = _Hosted(comm, n_in=5, n_out=2, n_scratch=0)

    def body(*refs):
        q_ref, k_ref, v_ref, causal_ref, later_ref, o_ref, c_ref = hosted.begin(refs, grid)
        I = pl.program_id(1)
        lane = lax.broadcasted_iota(jnp.int32, (T, LANES), 1)
        causal = causal_ref[...]
        later_than = later_ref[...]
        q2 = _sb_stack_heads(q_ref[...], scale)

        def scores(J, diag):
            off = pl.multiple_of(J * T, T)
            log_beta, stay = _sb_scores(q2, k_ref[pl.ds(off, T), :])
            if diag:
                stay = stay * causal
            local = jnp.dot(stay.astype(BF16), later_than, preferred_element_type=F32)
            return log_beta, local, jnp.sum(stay, axis=1, keepdims=True), v_ref[pl.ds(off, T), :]

        def weigh(J, sc, gone, acc, carr, diag):
            log_beta, local, _, v_j = sc
            w = jnp.exp((log_beta - gone) - local)
            if diag:
                w = w * causal
            acc = acc + jnp.dot(w.astype(BF16), v_j, preferred_element_type=F32)
            carr = jnp.where(lane == J, -gone[:T], carr)
            carr = jnp.where(lane == HEAD_DIM + J, -gone[T:], carr)
            return acc, carr

        def tiles(J, count, state, diag):
            gone, acc, carr, _ = state
            scs = [scores(J - u, diag and u == 0) for u in range(count)]
            for u, sc in enumerate(scs):
                acc, carr = weigh(J - u, sc, gone, acc, carr, diag and u == 0)
                gone = gone + sc[2]
            return gone, acc, carr, jnp.min(gone)

        U = SB_UNROLL
        alive = lambda st: st[3] < SB_DEAD
        state = (jnp.zeros((2 * T, 1), F32), jnp.zeros((2 * T, LANES), F32),
                 jnp.full((T, LANES), SB_UNSEEN, F32), jnp.zeros((), F32))
        state = lax.cond(I > 0, lambda st: tiles(I, 2, st, True), lambda st: tiles(I, 1, st, True), state)
        rest = jnp.maximum(I - 1, 0)
        singles = jnp.where(rest > 0, (rest - 1) % U + 1, 0)
        _, state = lax.while_loop(lambda c: (c[0] < singles) & alive(c[1]),
                                  lambda c: (c[0] + 1, tiles(I - 2 - c[0], 1, c[1], False)), (jnp.int32(0), state))
        blocks = (rest - singles) // U
        _, state = lax.while_loop(lambda c: (c[0] < blocks) & alive(c[1]),
                                  lambda c: (c[0] + 1, tiles(I - 2 - singles - U * c[0], U, c[1], False)),
                                  (jnp.int32(0), state))
        _, acc, carr, _ = state
        o_ref[...] = _sb_unstack_heads(acc)
        c_ref[...] = carr
        hosted.end(grid)

    blk = lambda I_off: pl.BlockSpec((T, LANES), lambda p, I: (I, I_off + p))
    full = lambda off: pl.BlockSpec((S, LANES), lambda p, I: (0, off + p))
    const = lambda rows: pl.BlockSpec((rows, T), lambda p, I: (0, 0))
    o, carries, *got = pl.pallas_call(
        body, name=name, grid=grid,
        out_shape=[jax.ShapeDtypeStruct((S, W), F32), jax.ShapeDtypeStruct((S, W), F32)] + hosted.out_shapes,
        in_specs=[blk(0), full(npair), full(2 * npair), const(2 * T), const(T)] + hosted.in_specs,
        out_specs=[blk(0), blk(0)] + hosted.out_specs,
        scratch_shapes=hosted.scratch,
        compiler_params=_cparams(hosted.semantics(("parallel", "arbitrary"))),
    )(p_sb, p_sb, p_sb, _sb_causal(T).astype(F32), _sb_triangle(T, True), *hosted.operands)
    return (o, carries) if comm is None else (o, carries, got)


def _sb_bwd(p_sb, do, carries, *, name, comm=None):
    S = p_sb.shape[0]
    W = p_sb.shape[1] // 3
    npair = W // LANES
    T = SB_TILE
    n_tiles = S // T
    scale = HEAD_DIM ** -0.5

    grid = (npair, n_tiles)
    hosted = _Hosted(comm, n_in=8, n_out=3, n_scratch=0)

    def body(*refs):
        (q_ref, k_ref, v_ref, do_ref, c_ref, causal_ref, later_ref, earlier_ref,
         dq_ref, dk_ref, dv_ref) = hosted.begin(refs, grid)
        I = pl.program_id(1)

        @pl.when(I == 0)
        def _():
            dk_ref[...] = jnp.zeros_like(dk_ref)
            dv_ref[...] = jnp.zeros_like(dv_ref)

        lane = lax.broadcasted_iota(jnp.int32, (T, LANES), 1)
        causal = causal_ref[...]
        later_than = later_ref[...]
        earlier_than = earlier_ref[...]
        q2 = _sb_stack_heads(q_ref[...], scale)
        do2 = _sb_stack_heads(do_ref[...].astype(BF16))
        carr = c_ref[...]
        tn_dims = (((0,), (0,)), ((), ()))

        def chain(J, diag):
            off = pl.multiple_of(J * T, T)
            k_j = k_ref[pl.ds(off, T), :]
            v_j = v_ref[pl.ds(off, T), :]
            log_beta, stay = _sb_scores(q2, k_j)
            if diag:
                stay = stay * causal
            lc = jnp.concatenate(
                [jnp.sum(jnp.where(lane == HEAD_DIM * h + J, carr, 0.0), axis=1, keepdims=True) for h in range(2)],
                axis=0)
            w = jnp.exp((log_beta + lc) - jnp.dot(stay.astype(BF16), later_than, preferred_element_type=F32))
            if diag:
                w = w * causal
            dw = lax.dot_general(do2, v_j, (((1,), (1,)), ((), ())), preferred_element_type=F32)
            e = w * dw
            local = jnp.dot(e.astype(BF16), earlier_than, preferred_element_type=F32)
            return off, k_j, w, e, local, jnp.exp(log_beta), jnp.sum(e, axis=1, keepdims=True)

        def finish(ch, ec, dq_acc, diag):
            off, k_j, w, e, local, beta, _ = ch
            e_before = local + ec
            dz = e - beta * (e + e_before)
            if diag:
                dz = dz * causal
            dzb = dz.astype(BF16)
            dq_acc = dq_acc + jnp.dot(dzb, k_j, preferred_element_type=F32)
            dk_ref[pl.ds(off, T), :] += lax.dot_general(dzb, q2, tn_dims, preferred_element_type=F32)
            dv_ref[pl.ds(off, T), :] += lax.dot_general(w.astype(BF16), do2, tn_dims, preferred_element_type=F32)
            return dq_acc

        def tiles(J, count, state, diag):
            ec, dq_acc = state
            chains = [chain(J + u, diag and u == count - 1) for u in range(count)]
            for u, ch in enumerate(chains):
                dq_acc = finish(ch, ec, dq_acc, diag and u == count - 1)
                ec = ec + ch[6]
            return ec, dq_acc

        lane_row = lax.broadcasted_iota(jnp.int32, (1, LANES), 1)
        reached = (jnp.max(carr, axis=0, keepdims=True) > 0.5 * SB_UNSEEN) & (lane_row < HEAD_DIM)
        first = jnp.min(jnp.where(reached, lane_row.astype(F32), float(n_tiles))).astype(jnp.int32)
        U = SB_UNROLL
        count = I - first
        rest = jnp.maximum(count - 1, 0)
        state = (jnp.zeros((2 * T, 1), F32), jnp.zeros((2 * T, LANES), F32))
        state = lax.fori_loop(0, rest // U, lambda jj, st: tiles(first + U * jj, U, st, False), state)
        state = lax.fori_loop(0, rest % U, lambda r, st: tiles(I - 1 - rest % U + r, 1, st, False), state)
        _, dq_acc = lax.cond(count > 0, lambda st: tiles(I - 1, 2, st, True), lambda st: tiles(I, 1, st, True), state)
        dq_ref[...] = _sb_unstack_heads(dq_acc) * scale
        hosted.end(grid)

    blk = lambda src_off: pl.BlockSpec((T, LANES), lambda p, I: (I, src_off + p))
    full = lambda off: pl.BlockSpec((S, LANES), lambda p, I: (0, off + p))
    const = lambda rows: pl.BlockSpec((rows, T), lambda p, I: (0, 0))
    dq, dk, dv, *got = pl.pallas_call(
        body, name=name, grid=grid,
        out_shape=[jax.ShapeDtypeStruct((S, W), F32)] * 3 + hosted.out_shapes,
        in_specs=[blk(0), full(npair), full(2 * npair), blk(0), blk(0), const(2 * T), const(T), const(T)]
        + hosted.in_specs,
        out_specs=[blk(0), full(0), full(0)] + hosted.out_specs,
        scratch_shapes=hosted.scratch,
        compiler_params=_cparams(hosted.semantics(("parallel", "arbitrary"))),
    )(p_sb, p_sb, p_sb, do, carries, _sb_causal(T).astype(F32), _sb_triangle(T, True), _sb_triangle(T, False),
      *hosted.operands)
    return (dq, dk, dv) if comm is None else (dq, dk, dv, got)


def _dil_blocks(b, body_fn):
    for pi, (window, dil) in enumerate(DILATED_PATTERNS):
        assert window // dil == DIL_BLOCK
        nblk = DIL_SUPER // (DIL_BLOCK * dil)
        assert (dil * nblk) % DIL_UNROLL == 0

        def group(g, _, pi=pi, dil=dil, nblk=nblk):
            for u in range(DIL_UNROLL):
                t = g * DIL_UNROLL + u
                n = t % nblk
                body_fn(pi, dil, t // nblk, n, b * nblk + n)
            return 0

        lax.fori_loop(0, dil * nblk // DIL_UNROLL, group, 0)


def _dil_rows(start, size, dil):
    if dil == 1:
        return pl.ds(pl.multiple_of(start, DIL_BLOCK), size)
    return pl.ds(start, size, stride=dil)


def _dil_fill_bias(bias_ref):
    row = lax.broadcasted_iota(jnp.int32, (2 * DIL_BLOCK, 2 * DIL_BLOCK), 0)
    kk = lax.broadcasted_iota(jnp.int32, (2 * DIL_BLOCK, 2 * DIL_BLOCK), 1)
    qi = jnp.where(row >= DIL_BLOCK, row - DIL_BLOCK, row)
    for s in range(2):
        dist = s * DIL_BLOCK + qi - kk
        bias_ref[s] = jnp.where((dist >= 0) & (dist <= DIL_BLOCK), 0.0, NEG_BIG)


def _dl_fwd(p_dl, *, name):
    S, W = p_dl.shape[0], p_dl.shape[1] // 3
    npair = W // LANES
    nsuper = S // DIL_SUPER
    assert S % DIL_SUPER == 0 and S // max(d for _, d in DILATED_PATTERNS) >= 2 * DIL_BLOCK
    scale = HEAD_DIM ** -0.5
    npat = len(DILATED_PATTERNS)

    def body(q_ref, k_ref, v_ref, o_ref, l_ref, bias_ref, *pattern_refs):
        op_refs, lp_refs = pattern_refs[:npat], pattern_refs[npat:]
        b = pl.program_id(1)
        masks = _head_masks((DIL_BLOCK, LANES))
        pl.when(b == 0)(lambda: _dil_fill_bias(bias_ref))

        def block(pi, dil, c, n, gn):
            ws = jnp.maximum(gn - 1, 0)
            qrows = n * (DIL_BLOCK * dil) + c
            krows = ws * (DIL_BLOCK * dil) + c
            q_idx = _dil_rows(qrows, DIL_BLOCK, dil)
            k_idx = _dil_rows(krows, 2 * DIL_BLOCK, dil)
            qb = q_ref[q_idx, :]
            kb = k_ref[k_idx, :].astype(BF16)
            vb = v_ref[k_idx, :].astype(BF16)
            q2 = _sb_stack_heads(qb.astype(BF16), scale)
            z = lax.dot_general(q2, kb, (((1,), (1,)), ((), ())), preferred_element_type=F32) + bias_ref[gn - ws]
            m = jnp.max(z, axis=1, keepdims=True)
            p = jnp.exp(z - m)
            den = jnp.sum(p, axis=1, keepdims=True)
            acc = jnp.dot(p.astype(BF16), vb, preferred_element_type=F32)
            lse = m + jnp.log(den)
            op_refs[pi][q_idx, :] = _sb_unstack_heads(acc / den)
            lp_refs[pi][q_idx, :] = jnp.where(masks[0], lse[:DIL_BLOCK], lse[DIL_BLOCK:])

        _dil_blocks(b, block)
        lses = [r[...] for r in lp_refs]
        top = functools.reduce(jnp.maximum, lses)
        ws_ = [jnp.exp(l - top) for l in lses]
        den = functools.reduce(jnp.add, ws_)
        num = functools.reduce(jnp.add, [w * r[...] for r, w in zip(op_refs, ws_)])
        o_ref[...] = num / den
        l_ref[...] = top + jnp.log(den)

    blk = pl.BlockSpec((DIL_SUPER, LANES), lambda p, b: (b, p))
    full = lambda off: pl.BlockSpec((S, LANES), lambda p, b: (0, off + p))
    return pl.pallas_call(
        body, name=name, grid=(npair, nsuper),
        out_shape=[jax.ShapeDtypeStruct((S, W), F32)] * 2,
        in_specs=[blk, full(npair), full(2 * npair)], out_specs=[blk, blk],
        scratch_shapes=[pltpu.VMEM((2, 2 * DIL_BLOCK, 2 * DIL_BLOCK), F32)]
        + [pltpu.VMEM((DIL_SUPER, LANES), F32)] * (2 * npat),
        compiler_params=_cparams(("arbitrary", "arbitrary")),
    )(p_dl, p_dl, p_dl)


def _dl_bwd(p_dl, o, lse, do, *, name):
    S, W = p_dl.shape[0], p_dl.shape[1] // 3
    npair = W // LANES
    nsuper = S // DIL_SUPER
    scale = HEAD_DIM ** -0.5

    def body(q_ref, k_ref, v_ref, o_ref, l_ref, do_ref, dq_ref, dk_ref, dv_ref, delta_ref, bias_ref):
        b = pl.program_id(1)

        @pl.when(b == 0)
        def _():
            dk_ref[...] = jnp.zeros_like(dk_ref)
            dv_ref[...] = jnp.zeros_like(dv_ref)
            _dil_fill_bias(bias_ref)

        dq_ref[...] = jnp.zeros_like(dq_ref)
        prod = do_ref[...] * o_ref[...]
        delta = jnp.zeros_like(prod)
        for hm in _head_masks(prod.shape):
            delta = jnp.where(hm, jnp.sum(jnp.where(hm, prod, 0.0), axis=1, keepdims=True), delta)
        delta_ref[...] = delta

        def block(pi, dil, c, n, gn):
            ws = jnp.maximum(gn - 1, 0)
            qrows = n * (DIL_BLOCK * dil) + c
            krows = ws * (DIL_BLOCK * dil) + c
            q_idx = _dil_rows(qrows, DIL_BLOCK, dil)
            k_idx = _dil_rows(krows, 2 * DIL_BLOCK, dil)
            qb = q_ref[q_idx, :]
            dob = do_ref[q_idx, :]
            lb = l_ref[q_idx, :]
            db = delta_ref[q_idx, :]
            kb = k_ref[k_idx, :].astype(BF16)
            vb = v_ref[k_idx, :].astype(BF16)
            q2 = _sb_stack_heads(qb.astype(BF16), scale)
            do2 = _sb_stack_heads(dob.astype(BF16))
            lse2 = jnp.concatenate([lb[:, HEAD_DIM * h:HEAD_DIM * h + 1] for h in range(2)], axis=0)
            delta2 = jnp.concatenate([db[:, HEAD_DIM * h:HEAD_DIM * h + 1] for h in range(2)], axis=0)
            z = lax.dot_general(q2, kb, (((1,), (1,)), ((), ())), preferred_element_type=F32)
            p = jnp.exp((z + bias_ref[gn - ws]) - lse2)
            dp = lax.dot_general(do2, vb, (((1,), (1,)), ((), ())), preferred_element_type=F32)
            dzb = (p * (dp - delta2)).astype(BF16)
            tn_dims = (((0,), (0,)), ((), ()))
            dq_blk = _sb_unstack_heads(jnp.dot(dzb, kb, preferred_element_type=F32)) * scale
            dk_blk = lax.dot_general(dzb, q2, tn_dims, preferred_element_type=F32)
            dv_blk = lax.dot_general(p.astype(BF16), do2, tn_dims, preferred_element_type=F32)
            dq_ref[q_idx, :] = dq_ref[q_idx, :] + dq_blk
            dk_ref[k_idx, :] = dk_ref[k_idx, :] + dk_blk
            dv_ref[k_idx, :] = dv_ref[k_idx, :] + dv_blk

        _dil_blocks(b, block)

    blk = pl.BlockSpec((DIL_SUPER, LANES), lambda p, b: (b, p))
    full = lambda off: pl.BlockSpec((S, LANES), lambda p, b: (0, off + p))
    return pl.pallas_call(
        body, name=name, grid=(npair, nsuper),
        out_shape=[jax.ShapeDtypeStruct((S, W), F32)] * 3,
        in_specs=[blk, full(npair), full(2 * npair), blk, blk, blk], out_specs=[blk, full(0), full(0)],
        scratch_shapes=[pltpu.VMEM((DIL_SUPER, LANES), F32), pltpu.VMEM((2, 2 * DIL_BLOCK, 2 * DIL_BLOCK), F32)],
        compiler_params=_cparams(("arbitrary", "arbitrary")),
    )(p_dl, p_dl, p_dl, o, lse, do)


class _NoExchange:
    def gather(self, family):
        return None

    def gathered(self, family, got, weights):
        pass

    def send(self, family, grads):
        return None

    def received(self, family, got):
        pass


def _local_step(x, target, gains, weights, exchanges=None):
    S, D = x.shape
    ex = exchanges or _NoExchange()
    weights = dict(weights)
    d_sb = gains["sb_out_norm"].shape[1]
    d_dl = gains["dil_out_norm"].shape[1]
    cos_t, sin_t = _rope_tables(S)

    riders = ("ffn1_w_up", "ffn1_w_down", "mixer") if exchanges else (None, None, None)
    (x1, h2), saved1 = _ffn_fwd(x, gains["ffn1_norm"], weights, tag="ffn1", ex=ex, riders=riders,
                                first_rider="ffn1_w_gate" if exchanges else None,
                                next_gain=gains["mix_norm"])
    w_in = weights["w_in"]
    w_in_sb, w_in_dl = w_in[:, :3 * d_sb], w_in[:, 3 * d_sb:]
    w_out = weights["w_out"]
    p_sb = _mm(h2, w_in_sb, outs=(BF16,), name="proj_sb")

    def rope_qk(acc, _, cos, sin):
        return jnp.concatenate([_rotate(acc[:, :2 * d_dl], cos, sin, 1.0), acc[:, 2 * d_dl:]], axis=1)

    p_dl = _mm(h2, w_in_dl, lanes=(cos_t, sin_t), epilogue=rope_qk, name="proj_dl_rope")
    plan = ex.gather("ffn2" if exchanges else None)
    o_sb, carries, *got = _sb_fwd(p_sb, name="sb_fwd", comm=plan)
    ex.gathered("ffn2", got[0] if got else None, weights)
    o_dl, lse_dl = _dl_fwd(p_dl, name="dl_fwd")
    merged = _rms_fwd([o_sb, o_dl], [gains["sb_out_norm"], gains["dil_out_norm"]], name="out_norm")
    x2, h3 = _mm(merged, w_out, res=x1, rows=(gains["ffn2_norm"],), outs=(F32, BF16),
                 epilogue=_residual_then_norm(1.0), name="out_proj_norm")
    (dx3, d_final, loss_wide), saved2 = _ffn_fwd(x2, gains["ffn2_norm"], weights, tag="ffn2", ex=ex, h=h3,
                                                 head=(gains["final_norm"], target))
    loss_row = loss_wide[:, :LANES]

    dx2, d_ffn2_norm, dwg2, dwu2, dwd2 = _ffn_bwd(dx3, x2, gains["ffn2_norm"], weights, saved2, tag="ffn2", ex=ex)
    d_w_out = _mm(merged, dx2, ta=True, outs=(GRAD_WIRE,), name="d_w_out")
    d_merged = _mm(dx2, w_out, tb=True, name="d_merged")
    (do_sb, do_dl), (d_sb_norm, d_dl_norm) = _rms_bwd(
        d_merged, [o_sb, o_dl], [gains["sb_out_norm"], gains["dil_out_norm"]], None, name="out_norm_bwd")
    plan = ex.send("ffn2", dict(ffn2_w_gate=dwg2, ffn2_w_up=dwu2, ffn2_w_down=dwd2))
    dq_sb, dk_sb, dv_sb, *got = _sb_bwd(p_sb, do_sb, carries, name="sb_bwd", comm=plan)
    ex.received("ffn2", got[0] if got else None)
    dq_dl, dk_dl, dv_dl = _dl_bwd(p_dl, o_dl, lse_dl, do_dl, name="dl_bwd")
    d_proj = _join_d_proj([dq_sb, dk_sb, dv_sb, dq_dl, dk_dl, dv_dl], (3, 4), cos_t, sin_t, name="d_proj")
    d_w_in = _mm(h2, d_proj, ta=True, outs=(GRAD_WIRE,), name="d_w_in")
    dx1, d_mix_norm = _mm(d_proj, w_in, tb=True, extras=(x1, dx2), rows=(gains["mix_norm"],), row_sums=1,
                          epilogue=_rms_bwd_epilogue, name="dh_mix_norm_bwd")
    dx, d_ffn1_norm, dwg1, dwu1, dwd1 = _ffn_bwd(
        dx1, x, gains["ffn1_norm"], weights, saved1, tag="ffn1", ex=ex,
        rider=("mixer", dict(w_in=d_w_in, w_out=d_w_out)), spread=True)
    gain_grads = dict(ffn1_norm=d_ffn1_norm, mix_norm=d_mix_norm, sb_out_norm=d_sb_norm, dil_out_norm=d_dl_norm,
                      ffn2_norm=d_ffn2_norm, final_norm=d_final)
    weight_grads = dict(ffn1_w_gate=dwg1, ffn1_w_up=dwu1, ffn1_w_down=dwd1, w_in=d_w_in, w_out=d_w_out,
                        ffn2_w_gate=dwg2, ffn2_w_up=dwu2, ffn2_w_down=dwd2)
    return loss_row, dx, gain_grads, weight_grads


def _mesh_position():
    return lax.axis_index("x"), lax.axis_index("y"), lax.axis_index("c")


def _flip(coord, bit):
    return 1 - coord if bit else coord


RELATIONS = [(rx, ry, rc) for rx in (0, 1) for ry in (0, 1) for rc in (0, 1)][1:]


class _GatherPlan:
    def __init__(self, shards):
        n = len(shards)
        self.operands = list(shards)
        self.out_shapes = [jax.ShapeDtypeStruct((N_DEV,) + s.shape, s.dtype) for s in shards]
        self.scratch = [pltpu.SemaphoreType.DMA((n, 7)), pltpu.SemaphoreType.DMA((n, 7)),
                        pltpu.SemaphoreType.DMA((n,))]

    def _copies(self, in_refs, out_refs, sems):
        send_sems, recv_sems, local_sems = sems
        x, y, c = _mesh_position()
        me, sibling = (x, y, c), (x, y, 1 - c)
        chips = [(1 - x, y), (x, 1 - y), (1 - x, 1 - y)]
        plans = []
        for t, (x_ref, out_ref) in enumerate(zip(in_refs, out_refs)):
            def slot(px, py, pc, out_ref=out_ref):
                return out_ref.at[4 * px + 2 * py + pc]

            def copy(k, block, to, src=None, t=t, slot=slot):
                return pltpu.make_async_remote_copy(
                    src_ref=slot(*block) if src is None else src, dst_ref=slot(*block),
                    send_sem=send_sems.at[t, k], recv_sem=recv_sems.at[t, k],
                    device_id=to, device_id_type=pl.DeviceIdType.MESH)

            plans.append(dict(
                mine=pltpu.make_async_copy(x_ref, slot(*me), local_sems.at[t]),
                first=[copy(0, me, sibling, src=x_ref)]
                + [copy(1 + j, me, (*chip, c), src=x_ref) for j, chip in enumerate(chips)],
                over_ici=[copy(1 + j, (*chip, c), me) for j, chip in enumerate(chips)],
                passed=[copy(4 + j, (*chip, c), sibling) for j, chip in enumerate(chips)],
                from_sibling=[copy(0, sibling, me)] + [copy(4 + j, (*chip, 1 - c), me) for j, chip in enumerate(chips)]))
        return plans

    def start(self, in_refs, out_refs, sems):
        for p in self._copies(in_refs, out_refs, sems):
            p["mine"].start()
            for cp in p["first"]:
                cp.start()

    def finish(self, in_refs, out_refs, sems):
        plans = self._copies(in_refs, out_refs, sems)
        for p in plans:
            for arrived, onward in zip(p["over_ici"], p["passed"]):
                arrived.wait_recv()
                onward.start()
        for p in plans:
            for cp in p["from_sibling"]:
                cp.wait_recv()
            for cp in p["first"] + p["passed"]:
                cp.wait_send()
            p["mine"].wait()


class _Hosted:
    def __init__(self, plan, n_in, n_out, n_scratch):
        self.plan, self.n_in, self.n_out, self.n_scratch = plan, n_in, n_out, n_scratch
        self.operands = list(plan.operands) if plan else []
        self.out_shapes = list(plan.out_shapes) if plan else []
        self.scratch = list(plan.scratch) if plan else []
        self.in_specs = [pl.BlockSpec(memory_space=pl.ANY)] * len(self.operands)
        self.out_specs = [pl.BlockSpec(memory_space=pl.ANY)] * len(self.out_shapes)

    def semantics(self, sem):
        return sem if self.plan is None else ("arbitrary",) * len(sem)

    def _at(self, grid, last):
        hit = None
        for d, n in enumerate(grid):
            here = pl.program_id(d) == (n - 1 if last else 0)
            hit = here if hit is None else hit & here
        return hit

    def begin(self, refs, grid):
        if self.plan is None:
            return refs
        k_in, k_out = len(self.operands), len(self.out_shapes)
        ins, rest = refs[:self.n_in], refs[self.n_in:]
        c_in, rest = rest[:k_in], rest[k_in:]
        outs, rest = rest[:self.n_out], rest[self.n_out:]
        c_out, rest = rest[:k_out], rest[k_out:]
        scratch, sems = rest[:self.n_scratch], rest[self.n_scratch:]
        self._args = (c_in, c_out, sems)
        pl.when(self._at(grid, False))(lambda: self.plan.start(*self._args))
        return tuple(ins) + tuple(outs) + tuple(scratch)

    def end(self, grid):
        if self.plan is not None:
            pl.when(self._at(grid, True))(lambda: self.plan.finish(*self._args))


class _ExchangePlan:
    def __init__(self, packs):
        n = len(packs)
        self.operands = list(packs)
        self.out_shapes = [jax.ShapeDtypeStruct(p.shape, p.dtype) for p in packs]
        self.scratch = [pltpu.SemaphoreType.DMA((n, 7)), pltpu.SemaphoreType.DMA((n, 7)),
                        pltpu.SemaphoreType.DMA((n,))]

    def _copies(self, in_refs, out_refs, sems):
        send_sems, recv_sems, local_sems = sems
        x, y, c = _mesh_position()
        me = 4 * x + 2 * y + c
        copies = [pltpu.make_async_copy(i.at[me], o.at[me], local_sems.at[t])
                  for t, (i, o) in enumerate(zip(in_refs, out_refs))]
        for r, (rx, ry, rc) in enumerate(RELATIONS):
            px, py, pc = _flip(x, rx), _flip(y, ry), _flip(c, rc)
            peer = 4 * px + 2 * py + pc
            copies += [pltpu.make_async_remote_copy(
                src_ref=i.at[peer], dst_ref=o.at[me], send_sem=send_sems.at[t, r], recv_sem=recv_sems.at[t, r],
                device_id=(px, py, pc), device_id_type=pl.DeviceIdType.MESH)
                for t, (i, o) in enumerate(zip(in_refs, out_refs))]
        return copies

    def start(self, in_refs, out_refs, sems):
        for cp in self._copies(in_refs, out_refs, sems):
            cp.start()

    def finish(self, in_refs, out_refs, sems):
        for cp in self._copies(in_refs, out_refs, sems):
            cp.wait()


def _all_reduce_rows(v, *, name):
    R, C = v.shape

    def body(v_ref, out_ref, buf, send_sems, recv_sems):
        x, y, c = _mesh_position()
        me = 4 * x + 2 * y + c
        buf[me] = v_ref[...]
        copies = []
        for r, (rx, ry, rc) in enumerate(RELATIONS):
            cp = pltpu.make_async_remote_copy(
                src_ref=v_ref, dst_ref=buf.at[me], send_sem=send_sems.at[r], recv_sem=recv_sems.at[r],
                device_id=(_flip(x, rx), _flip(y, ry), _flip(c, rc)), device_id_type=pl.DeviceIdType.MESH)
            cp.start()
            copies.append(cp)
        for cp in copies:
            cp.wait()
        total = buf[0]
        for s in range(1, N_DEV):
            total = total + buf[s]
        out_ref[...] = total

    return pl.pallas_call(
        body, name=name,
        out_shape=jax.ShapeDtypeStruct((R, C), F32),
        in_specs=[pl.BlockSpec(memory_space=pltpu.VMEM)],
        out_specs=pl.BlockSpec(memory_space=pltpu.VMEM),
        scratch_shapes=[pltpu.VMEM((N_DEV, R, C), F32), pltpu.SemaphoreType.DMA((7,)), pltpu.SemaphoreType.DMA((7,))],
    )(v)


def _sum_slots(recv, *, name):
    _, R, C = recv.shape
    tr = _pick(R, (256, 208, 128, 64, 32, 16))

    def body(r_ref, o_ref):
        total = r_ref[0].astype(F32)
        for s in range(1, N_DEV):
            total = total + r_ref[s].astype(F32)
        o_ref[...] = total

    return pl.pallas_call(
        body, name=name, grid=(R // tr,),
        out_shape=jax.ShapeDtypeStruct((R, C), F32),
        in_specs=[pl.BlockSpec((N_DEV, tr, C), lambda i: (0, i, 0))],
        out_specs=pl.BlockSpec((tr, C), lambda i: (i, 0)),
        compiler_params=_cparams(("parallel",)),
    )(recv)


def _adamw(w, g, m, v, *, name):
    R, C = w.shape
    tr = _pick(R, (256, 128, 64, 32, 16, 8))

    def body(w_ref, g_ref, m_ref, v_ref, d_ref, nm_ref, nv_ref):
        g = g_ref[...]
        m_new = ADAM_B1 * m_ref[...] + (1.0 - ADAM_B1) * g
        v_new = ADAM_B2 * v_ref[...] + (1.0 - ADAM_B2) * (g * g)
        m_hat = m_new / (1.0 - ADAM_B1 ** ADAM_STEP)
        v_hat = v_new / (1.0 - ADAM_B2 ** ADAM_STEP)
        d_ref[...] = -ADAM_LR * (m_hat / (jnp.sqrt(v_hat) + ADAM_EPS) + ADAM_WD * w_ref[...])
        nm_ref[...] = m_new
        nv_ref[...] = v_new

    spec = pl.BlockSpec((tr, C), lambda i: (i, 0))
    return pl.pallas_call(
        body, name=name, grid=(R // tr,),
        out_shape=[jax.ShapeDtypeStruct((R, C), F32)] * 3,
        in_specs=[spec] * 4, out_specs=[spec] * 3,
        compiler_params=_cparams(("parallel",)),
    )(w, g, m, v)


WEIGHT_NAMES = ["ffn1_norm", "ffn1_w_gate", "ffn1_w_up", "ffn1_w_down", "mix_norm", "w_in", "sb_out_norm",
                "dil_out_norm", "w_out", "ffn2_norm", "ffn2_w_gate", "ffn2_w_up", "ffn2_w_down", "final_norm"]
GAIN_NAMES = ["ffn1_norm", "mix_norm", "sb_out_norm", "dil_out_norm", "ffn2_norm", "final_norm"]
COL_SHARDED = ["ffn1_w_gate", "ffn1_w_up", "ffn2_w_gate", "ffn2_w_up", "w_in"]
ROW_SHARDED = ["ffn1_w_down", "ffn2_w_down", "w_out"]
GROUPS = {"mixer": (["w_in"], ["w_out"]),
          "ffn2": (["ffn2_w_gate", "ffn2_w_up"], ["ffn2_w_down"])}
for _ffn in ("ffn1", "ffn2"):
    GROUPS.update({f"{_ffn}_w_gate": ([f"{_ffn}_w_gate"], []), f"{_ffn}_w_up": ([f"{_ffn}_w_up"], []),
                   f"{_ffn}_w_down": ([], [f"{_ffn}_w_down"])})


class _Exchanges:
    def __init__(self, params):
        self.params = params
        self.grads = {}

    def gather(self, group):
        if group is None:
            return None
        cols, rows = GROUPS[group]
        packs = []
        if cols:
            packs.append(jnp.concatenate([self.params[n] for n in cols], axis=1).astype(BF16))
        if rows:
            packs.append(jnp.concatenate([self.params[n] for n in rows], axis=0).astype(BF16))
        return _GatherPlan(packs)

    def gathered(self, group, got, weights):
        if group is None:
            return
        cols, rows = GROUPS[group]
        got = list(got)
        if cols:
            col_all, off = got.pop(0), 0
            for n in cols:
                w = self.params[n].shape[1]
                piece = col_all[:, :, off:off + w]
                weights[n] = jnp.transpose(piece, (1, 0, 2)).reshape(piece.shape[1], N_DEV * w)
                off += w
        if rows:
            row_all, off = got.pop(0), 0
            for n in rows:
                r = self.params[n].shape[0]
                weights[n] = row_all[:, off:off + r, :].reshape(N_DEV * r, row_all.shape[2])
                off += r

    def send(self, group, grads):
        if group is None:
            return None
        cols, rows = GROUPS[group]
        packs = []
        if cols:
            chunks = [jnp.transpose(grads[n].reshape(grads[n].shape[0], N_DEV, self.params[n].shape[1]), (1, 0, 2))
                      for n in cols]
            packs.append(jnp.concatenate(chunks, axis=2).astype(GRAD_WIRE))
        if rows:
            chunks = [grads[n].reshape(N_DEV, self.params[n].shape[0], grads[n].shape[1]) for n in rows]
            packs.append(jnp.concatenate(chunks, axis=1).astype(GRAD_WIRE))
        return _ExchangePlan(packs)

    def received(self, group, got):
        if group is None:
            return
        cols, rows = GROUPS[group]
        got = list(got)
        if cols:
            col_grad, off = _sum_slots(got.pop(0), name=f"sum_col_grads_{group}"), 0
            for n in cols:
                w = self.params[n].shape[1]
                self.grads[n] = col_grad[:, off:off + w]
                off += w
        if rows:
            row_grad, off = _sum_slots(got.pop(0), name=f"sum_row_grads_{group}"), 0
            for n in rows:
                r = self.params[n].shape[0]
                self.grads[n] = row_grad[off:off + r, :]
                off += r


def _step(x, target, params, moments_m, moments_v):
    ex = _Exchanges(params)
    weights = {}
    gains = {n: params[n] for n in GAIN_NAMES}
    loss_row, grad_x, gain_grads, _ = _local_step(x, target, gains, weights, ex)
    grads = ex.grads

    rows = [gain_grads[n].reshape(-1, LANES) for n in GAIN_NAMES] + [loss_row]
    small = jnp.concatenate(rows, axis=0)
    pad = (-small.shape[0]) % 8
    small = jnp.pad(small, ((0, pad), (0, 0)))
    small = _all_reduce_rows(small, name="reduce_gains_loss")
    off = 0
    for n in GAIN_NAMES:
        r = gain_grads[n].shape[1] // LANES
        grads[n] = small[off:off + r].reshape(1, -1)
        off += r
    loss = small[off, 0]

    delta, new_m, new_v = {}, {}, {}
    for n in WEIGHT_NAMES:
        delta[n], new_m[n], new_v[n] = _adamw(params[n], grads[n], moments_m[n], moments_v[n], name=f"adamw_{n}")
    return loss, grad_x, grads, delta, new_m, new_v


def kernel(x, ffn1_norm, ffn1_w_gate, ffn1_w_up, ffn1_w_down, mix_norm, w_in, sb_out_norm, dil_out_norm, w_out, ffn2_norm, ffn2_w_gate, ffn2_w_up, ffn2_w_down, final_norm, loss_target, m_ffn1_norm, m_ffn1_w_gate, m_ffn1_w_up, m_ffn1_w_down, m_mix_norm, m_w_in, m_sb_out_norm, m_dil_out_norm, m_w_out, m_ffn2_norm, m_ffn2_w_gate, m_ffn2_w_up, m_ffn2_w_down, m_final_norm, v_ffn1_norm, v_ffn1_w_gate, v_ffn1_w_up, v_ffn1_w_down, v_mix_norm, v_w_in, v_sb_out_norm, v_dil_out_norm, v_w_out, v_ffn2_norm, v_ffn2_w_gate, v_ffn2_w_up, v_ffn2_w_down, v_final_norm):
    given = dict(locals())
    shapes = {n: given[n].shape for n in WEIGHT_NAMES}

    def as2d(a):
        return a.reshape(1, -1) if a.ndim == 1 else a.reshape(a.shape[-2], a.shape[-1])

    params = {n: as2d(given[n]) for n in WEIGHT_NAMES}
    moments_m = {n: as2d(given["m_" + n]) for n in WEIGHT_NAMES}
    moments_v = {n: as2d(given["v_" + n]) for n in WEIGHT_NAMES}
    loss, grad_x, grads, delta, new_m, new_v = _step(x[0], loss_target[0], params, moments_m, moments_v)
    back = lambda d: [d[n].reshape(shapes[n]) for n in WEIGHT_NAMES]
    return (loss, grad_x[None], *back(grads), *back(delta), *back(new_m), *back(new_v))
```

```python
import functools

import jax
import jax.numpy as jnp
from jax import lax
from jax.experimental import pallas as pl
from jax.experimental.pallas import tpu as pltpu

F32 = jnp.float32
BF16 = jnp.bfloat16
GRAD_WIRE = jnp.bfloat16

N_DEV = 8
HEAD_DIM = 64
LANES = 128
DILATED_PATTERNS = ((128, 1), (512, 4), (2048, 16))
DIL_BLOCK = 128
DIL_SUPER = 2048
DIL_UNROLL = 8
SB_TILE = 256
SB_UNROLL = 4
SB_DEAD = 90.0
SB_UNSEEN = -1e30
ROPE_THETA = 10000.0
RMS_EPS = 1e-6
HALF_STEP = 0.5
ADAM_LR = 0.001
ADAM_B1 = 0.9
ADAM_B2 = 0.999
ADAM_EPS = 1e-08
ADAM_WD = 0.01
ADAM_STEP = 10
NEG_BIG = -1e30
VMEM_CAP_MB = 60


def _pick(n, prefs):
    for p in prefs:
        if n % p == 0:
            return p
    return n


MM_MAX_TILE = 1536
MM_WHOLE = 3072


def _largest_tile(n, cap):
    if n <= cap:
        return n
    for t in range(cap - cap % LANES, 0, -LANES):
        if n % t == 0:
            return t
    return n


def _cparams(sem=None, vmem_mb=48):
    return pltpu.CompilerParams(dimension_semantics=sem, vmem_limit_bytes=min(vmem_mb, VMEM_CAP_MB) * 1024 * 1024)


def _nbytes(shape, dtype):
    n = 1
    for s in shape:
        n *= s
    return n * jnp.dtype(dtype).itemsize


def _mm(a, b, *, name, ta=False, tb=False, outs=(F32,), res=None, alpha=1.0, extras=(), epilogue=None,
        tm=None, tn=None, tk=None, comm=None, rows=(), lanes=(), row_sums=0, b_cols=None):
    if ta:
        K, M = a.shape
    else:
        M, K = a.shape
    if tb:
        N, Kb = b.shape
    else:
        Kb, N = b.shape
    col0 = 0
    if b_cols is not None:
        assert not tb
        col0, N = b_cols
    assert K == Kb, (a.shape, b.shape, ta, tb)
    tn = tn or (N if (not ta and K <= MM_WHOLE and N <= MM_WHOLE) else _largest_tile(N, MM_MAX_TILE))
    tm = tm or (_largest_tile(M, MM_MAX_TILE) if ta else _pick(M, (512, 256, 128) if tn <= MM_MAX_TILE else (256, 128)))
    tk = tk or (K if K <= MM_WHOLE else _pick(K, (2048, 1024, 512, 256, 128)))
    nk = K // tk
    a_spec = pl.BlockSpec((tk, tm), lambda i, j, k: (k, i)) if ta else pl.BlockSpec((tm, tk), lambda i, j, k: (i, k))
    assert col0 % tn == 0
    b_spec = (pl.BlockSpec((tn, tk), lambda i, j, k: (j, k)) if tb
              else pl.BlockSpec((tk, tn), lambda i, j, k: (k, j + col0 // tn)))
    mn_spec = pl.BlockSpec((tm, tn), lambda i, j, k: (i, j))
    dims = (((0 if ta else 1,), (1 if tb else 0,)), ((), ()))
    row_spec = pl.BlockSpec((1, tn), lambda i, j, k: (0, j))
    lane_spec = pl.BlockSpec((tm, LANES), lambda i, j, k: (i, 0))
    n_extra = len(extras) + (1 if res is not None else 0) + len(rows) + len(lanes)
    n_mn = len(outs)
    n_out = n_mn + row_sums
    assert row_sums == 0 or tn == N
    grid = (M // tm, N // tn, nk)
    hosted = _Hosted(comm, n_in=2 + n_extra, n_out=n_out, n_scratch=1 if nk > 1 else 0)

    def body(*refs):
        a_ref, b_ref = refs[0], refs[1]
        in_refs = refs[2:2 + n_extra]
        refs = hosted.begin(refs, grid)
        out_refs = refs[2 + n_extra:2 + n_extra + n_out]
        prod = lax.dot_general(a_ref[...].astype(BF16), b_ref[...].astype(BF16), dims, preferred_element_type=F32)

        def finish(acc):
            blocks = [r[...] for r in in_refs]
            if res is not None:
                r_blk, blocks = blocks[0], blocks[1:]
            else:
                r_blk = None
            if epilogue is None:
                val = acc * alpha
                if r_blk is not None:
                    val = val + r_blk
                vals = (val,)
            else:
                vals = epilogue(acc, r_blk, *blocks)
                vals = vals if isinstance(vals, (tuple, list)) else (vals,)
            for o_ref, v in zip(out_refs[:n_mn], vals[:n_mn]):
                o_ref[...] = v.astype(o_ref.dtype)
            first_rows = pl.program_id(0) == 0
            for o_ref, part in zip(out_refs[n_mn:], vals[n_mn:]):
                @pl.when(first_rows)
                def _(o_ref=o_ref, part=part):
                    o_ref[...] = part

                @pl.when(jnp.logical_not(first_rows))
                def _(o_ref=o_ref, part=part):
                    o_ref[...] += part

        if nk == 1:
            finish(prod)
        else:
            acc_ref = refs[2 + n_extra + n_out]
            k = pl.program_id(2)

            @pl.when(k == 0)
            def _():
                acc_ref[...] = prod

            @pl.when(k > 0)
            def _():
                acc_ref[...] += prod

            @pl.when(k == nk - 1)
            def _():
                finish(acc_ref[...])

        hosted.end(grid)

    mn_operands = ([res] if res is not None else []) + list(extras)
    operands = [a, b] + mn_operands + list(rows) + list(lanes)
    in_specs = [a_spec, b_spec] + [mn_spec] * len(mn_operands) + [row_spec] * len(rows) + [lane_spec] * len(lanes)
    est = 2 * (_nbytes((tm, tk), a.dtype) + _nbytes((tk, tn), b.dtype))
    est += 2 * sum(_nbytes((tm, tn), o.dtype) for o in mn_operands)
    est += 2 * sum(_nbytes((tm, tn), d) for d in outs) + 2 * _nbytes((tm, tn), F32)
    semantics = ("parallel", "parallel", "arbitrary") if row_sums == 0 else ("arbitrary",) * 3
    result = pl.pallas_call(
        body, name=name, grid=grid,
        out_shape=[jax.ShapeDtypeStruct((M, N), d) for d in outs]
        + [jax.ShapeDtypeStruct((1, N), F32)] * row_sums + hosted.out_shapes,
        in_specs=in_specs + hosted.in_specs,
        out_specs=[mn_spec] * n_mn + [row_spec] * row_sums + hosted.out_specs,
        scratch_shapes=([pltpu.VMEM((tm, tn), F32)] if nk > 1 else []) + hosted.scratch,
        compiler_params=_cparams(hosted.semantics(semantics), vmem_mb=max(32, 2 * est // (1024 * 1024))),
    )(*operands, *hosted.operands)
    own, got = result[:n_out], list(result[n_out:])
    own = own[0] if n_out == 1 else own
    return own if comm is None else (own, got)


def _rms_hat(x):
    r = lax.rsqrt(jnp.mean(x * x, axis=-1, keepdims=True) + RMS_EPS)
    return x * r, r


def _rms_fwd(xs, gains, *, name, comm=None):
    S = xs[0].shape[0]
    widths = [x.shape[1] for x in xs]
    tm = _pick(S, (512, 256, 128))
    n = len(xs)
    grid = (S // tm,)
    hosted = _Hosted(comm, n_in=2 * n, n_out=1, n_scratch=0)

    def body(*refs):
        refs = hosted.begin(refs, grid)
        o_ref = refs[2 * n]
        off = 0
        for i in range(n):
            xh, _ = _rms_hat(refs[i][...])
            o_ref[:, off:off + widths[i]] = (xh * refs[n + i][...]).astype(o_ref.dtype)
            off += widths[i]
        hosted.end(grid)

    out, *got = pl.pallas_call(
        body, name=name, grid=grid,
        out_shape=[jax.ShapeDtypeStruct((S, sum(widths)), BF16)] + hosted.out_shapes,
        in_specs=[pl.BlockSpec((tm, w), lambda i: (i, 0)) for w in widths]
        + [pl.BlockSpec((1, w), lambda i: (0, 0)) for w in widths] + hosted.in_specs,
        out_specs=[pl.BlockSpec((tm, sum(widths)), lambda i: (i, 0))] + hosted.out_specs,
        scratch_shapes=hosted.scratch,
        compiler_params=_cparams(hosted.semantics(("parallel",))),
    )(*xs, *gains, *hosted.operands)
    return out if comm is None else (out, got)


def _rms_bwd(dh, xs, gains, res, *, name):
    S = xs[0].shape[0]
    widths = [x.shape[1] for x in xs]
    tm = _pick(S, (512, 256, 128))
    n = len(xs)
    has_res = res is not None

    def body(*refs):
        dh_ref = refs[0]
        x_refs = refs[1:1 + n]
        g_refs = refs[1 + n:1 + 2 * n]
        r_ref = refs[1 + 2 * n] if has_res else None
        base = 1 + 2 * n + (1 if has_res else 0)
        dx_refs = refs[base:base + n]
        dg_refs = refs[base + n:base + 2 * n]
        first = pl.program_id(0) == 0
        off = 0
        for i in range(n):
            x = x_refs[i][...]
            xh, r = _rms_hat(x)
            d = dh_ref[:, off:off + widths[i]]
            dxh = d * g_refs[i][...]
            dx = r * (dxh - xh * jnp.mean(dxh * xh, axis=-1, keepdims=True))
            if has_res:
                dx = dx + r_ref[...]
            dx_refs[i][...] = dx
            part = jnp.sum(d * xh, axis=0, keepdims=True)

            @pl.when(first)
            def _(i=i, part=part):
                dg_refs[i][...] = part

            @pl.when(jnp.logical_not(first))
            def _(i=i, part=part):
                dg_refs[i][...] += part

            off += widths[i]

    in_specs = [pl.BlockSpec((tm, sum(widths)), lambda i: (i, 0))]
    in_specs += [pl.BlockSpec((tm, w), lambda i: (i, 0)) for w in widths]
    in_specs += [pl.BlockSpec((1, w), lambda i: (0, 0)) for w in widths]
    operands = [dh, *xs, *gains]
    if has_res:
        in_specs.append(pl.BlockSpec((tm, widths[0]), lambda i: (i, 0)))
        operands.append(res)
    out = pl.pallas_call(
        body, name=name, grid=(S // tm,),
        out_shape=[jax.ShapeDtypeStruct((S, w), F32) for w in widths] + [jax.ShapeDtypeStruct((1, w), F32) for w in widths],
        in_specs=in_specs,
        out_specs=[pl.BlockSpec((tm, w), lambda i: (i, 0)) for w in widths]
        + [pl.BlockSpec((1, w), lambda i: (0, 0)) for w in widths],
        compiler_params=_cparams(("arbitrary",)),
    )(*operands)
    return out[:n], out[n:]


def _sigmoid(g):
    return 1.0 / (1.0 + jnp.exp(-g))


def _ride(result, plan):
    return result if plan is not None else (result, None)


def _residual_then_norm(alpha):
    def epilogue(acc, res, gain):
        y = res + alpha * acc
        return y, _rms_hat(y)[0] * gain
    return epilogue


def _ffn_fwd(x, gain, w, *, tag, ex, first_rider=None, riders=(None, None, None), head=None, h=None,
             next_gain=None):
    if h is None:
        plan = ex.gather(first_rider)
        h, got = _ride(_rms_fwd([x], [gain], name=f"{tag}_norm", comm=plan), plan)
        ex.gathered(first_rider, got, w)
    plan = ex.gather(riders[0])
    g, got = _ride(_mm(h, w[f"{tag}_w_gate"], outs=(BF16,), name=f"{tag}_gate", comm=plan), plan)
    ex.gathered(riders[0], got, w)

    def act(acc, _, g_blk):
        gf = g_blk.astype(F32)
        return acc, gf * _sigmoid(gf) * acc

    plan = ex.gather(riders[1])
    (u, a), got = _ride(_mm(h, w[f"{tag}_w_up"], outs=(BF16, BF16), extras=(g,), epilogue=act, name=f"{tag}_up_act",
                            comm=plan), plan)
    ex.gathered(riders[1], got, w)
    plan = ex.gather(riders[2])
    if head is None and next_gain is None:
        y, got = _ride(_mm(a, w[f"{tag}_w_down"], res=x, alpha=HALF_STEP, name=f"{tag}_down", comm=plan), plan)
    elif head is None:
        y, got = _ride(_mm(a, w[f"{tag}_w_down"], res=x, rows=(next_gain,), outs=(F32, BF16),
                           epilogue=_residual_then_norm(HALF_STEP), name=f"{tag}_down_norm", comm=plan), plan)
    else:
        final_gain, target = head
        y, got = _ride(_mm(a, w[f"{tag}_w_down"], res=x, extras=(target,), rows=(final_gain,), row_sums=2,
                           epilogue=_loss_head_epilogue, name=f"{tag}_down_loss", comm=plan), plan)
    ex.gathered(riders[2], got, w)
    return y, (h, g, u, a)


def _loss_head_epilogue(acc, x_in, target, gain):
    xh, r = _rms_hat(x_in + HALF_STEP * acc)
    err = xh * gain - target
    dy = err * (1.0 / acc.shape[1])
    dxh = dy * gain
    dx = r * (dxh - xh * jnp.mean(dxh * xh, axis=-1, keepdims=True))
    loss = 0.5 * jnp.sum(jnp.mean(err * err, axis=-1, keepdims=True), axis=0, keepdims=True)
    return dx, jnp.sum(dy * xh, axis=0, keepdims=True), jnp.zeros_like(gain) + loss


def _rms_bwd_epilogue(acc, dh_so_far, x, dres, gain):
    dh = acc if dh_so_far is None else acc + dh_so_far
    xh, r = _rms_hat(x)
    dxh = dh * gain
    dx = r * (dxh - xh * jnp.mean(dxh * xh, axis=-1, keepdims=True)) + dres
    return dx, jnp.sum(dh * xh, axis=0, keepdims=True)


def _ffn_bwd(dout, x, gain, w, saved, *, tag, ex, rider=(None, None), spread=False):
    h, g, u, a = saved
    wg, wu, wd = (w[f"{tag}_w_{n}"] for n in ("gate", "up", "down"))

    def act_bwd(acc, _, g_blk, u_blk):
        gf, uf = g_blk.astype(F32), u_blk.astype(F32)
        da = acc * HALF_STEP
        sig = _sigmoid(gf)
        silu = gf * sig
        return da * uf * (sig + silu * (1.0 - sig)), da * silu

    def carrying(group, grad, call):
        group = group if spread else None
        plan = ex.send(group, {group: grad})
        out, got = _ride(call(plan), plan)
        ex.received(group, got)
        return out

    plan = ex.send(*rider)
    (dg, du), got = _ride(_mm(dout, wd, tb=True, outs=(BF16, BF16), extras=(g, u), epilogue=act_bwd,
                              name=f"{tag}_bwd_act", comm=plan), plan)
    ex.received(rider[0], got)
    dwg = _mm(h, dg, ta=True, outs=(GRAD_WIRE,), name=f"{tag}_dwg")
    dwu = carrying(f"{tag}_w_gate", dwg, lambda plan: _mm(h, du, ta=True, outs=(GRAD_WIRE,), name=f"{tag}_dwu", comm=plan))
    dwd = carrying(f"{tag}_w_up", dwu,
                   lambda plan: _mm(a, dout, ta=True, outs=(GRAD_WIRE,), alpha=HALF_STEP, name=f"{tag}_dwd", comm=plan))
    dh = carrying(f"{tag}_w_down", dwd, lambda plan: _mm(dg, wg, tb=True, name=f"{tag}_dh_gate", comm=plan))
    dx, dgain = _mm(du, wu, tb=True, res=dh, extras=(x, dout), rows=(gain,), row_sums=1, epilogue=_rms_bwd_epilogue,
                    name=f"{tag}_dh_up_norm_bwd")
    return dx, dgain, dwg, dwu, dwd


def _rope_tables(S):
    half = HEAD_DIM // 2
    inv_freq = ROPE_THETA ** (-jnp.arange(half, dtype=F32) / half)
    ang = jnp.arange(S, dtype=F32)[:, None] * inv_freq[None, :]
    cos, sin = jnp.cos(ang), jnp.sin(ang)
    reps = LANES // HEAD_DIM
    cos_t = jnp.tile(jnp.concatenate([cos, cos], axis=1), (1, reps))
    sin_t = jnp.tile(jnp.concatenate([-sin, sin], axis=1), (1, reps))
    return cos_t, sin_t


def _rotate(v, cos, sin, sign):
    half = HEAD_DIM // 2
    groups = []
    for g in range(v.shape[1] // LANES):
        t = v[:, g * LANES:(g + 1) * LANES]
        lane = lax.broadcasted_iota(jnp.int32, t.shape, 1)
        swapped = jnp.where(lane % HEAD_DIM < half, pltpu.roll(t, LANES - half, axis=1), pltpu.roll(t, half, axis=1))
        groups.append(t * cos + swapped * (sin * sign))
    return groups[0] if len(groups) == 1 else jnp.concatenate(groups, axis=1)


def _join_d_proj(pieces, rotated, cos_t, sin_t, *, name):
    S = pieces[0].shape[0]
    widths = [p.shape[1] for p in pieces]
    tm = _pick(S, (256, 128))
    n = len(pieces)

    def body(*refs):
        c_ref, s_ref, o_ref = refs[n], refs[n + 1], refs[n + 2]
        off = 0
        for i in range(n):
            v = refs[i][...]
            if i in rotated:
                v = _rotate(v, c_ref[...], s_ref[...], -1.0)
            o_ref[:, off:off + widths[i]] = v.astype(o_ref.dtype)
            off += widths[i]

    return pl.pallas_call(
        body, name=name, grid=(S // tm,),
        out_shape=jax.ShapeDtypeStruct((S, sum(widths)), BF16),
        in_specs=[pl.BlockSpec((tm, w), lambda i: (i, 0)) for w in widths]
        + [pl.BlockSpec((tm, LANES), lambda i: (i, 0))] * 2,
        out_specs=pl.BlockSpec((tm, sum(widths)), lambda i: (i, 0)),
        compiler_params=_cparams(("parallel",)),
    )(*pieces, cos_t, sin_t)


def _head_masks(shape):
    lane = lax.broadcasted_iota(jnp.int32, shape, 1)
    return [(lane >= HEAD_DIM * h) & (lane < HEAD_DIM * (h + 1)) for h in range(LANES // HEAD_DIM)]


def _sb_scores(q2, k_j):
    z = lax.dot_general(q2, k_j, (((1,), (1,)), ((), ())), preferred_element_type=F32)
    sign_bit = jnp.int32(-2 ** 31)
    minus_abs = lax.bitcast_convert_type(lax.bitcast_convert_type(z, jnp.int32) | sign_bit, F32)
    softplus = jnp.maximum(z, 0.0) + jnp.log(1.0 + jnp.exp(minus_abs))
    return z - softplus, softplus


def _sb_stack_heads(t, scale=None):
    parts = [jnp.where(hm, t, jnp.zeros_like(t)) for hm in _head_masks(t.shape)]
    t2 = jnp.concatenate(parts, axis=0)
    if scale is not None:
        t2 = (t2.astype(F32) * scale).astype(t2.dtype)
    return t2


def _sb_unstack_heads(t2):
    T = t2.shape[0] // 2
    masks = _head_masks((T, LANES))
    return jnp.where(masks[0], t2[:T], t2[T:])


def _sb_causal(T):
    row = lax.broadcasted_iota(jnp.int32, (2 * T, T), 0)
    col = lax.broadcasted_iota(jnp.int32, (2 * T, T), 1)
    return col < jnp.where(row >= T, row - T, row)


def _sb_triangle(T, later):
    row = lax.broadcasted_iota(jnp.int32, (T, T), 0)
    col = lax.broadcasted_iota(jnp.int32, (T, T), 1)
    return ((row > col) if later else (row < col)).astype(BF16)


def _sb_fwd(p_sb, *, name, comm=None):
    S = p_sb.shape[0]
    W = p_sb.shape[1] // 3
    npair = W // LANES
    T = SB_TILE
    n_tiles = S // T
    assert n_tiles <= HEAD_DIM
    scale = HEAD_DIM ** -0.5

    grid = (npair, n_tiles)
    hosted = _Hosted(comm, n_in=5, n_out=2, n_scratch=0)

    def body(*refs):
        q_ref, k_ref, v_ref, causal_ref, later_ref, o_ref, c_ref = hosted.begin(refs, grid)
        I = pl.program_id(1)
        lane = lax.broadcasted_iota(jnp.int32, (T, LANES), 1)
        causal = causal_ref[...]
        later_than = later_ref[...]
        q2 = _sb_stack_heads(q_ref[...], scale)

        def scores(J, diag):
            off = pl.multiple_of(J * T, T)
            log_beta, stay = _sb_scores(q2, k_ref[pl.ds(off, T), :])
            if diag:
                stay = stay * causal
            local = jnp.dot(stay.astype(BF16), later_than, preferred_element_type=F32)
            return log_beta, local, jnp.sum(stay, axis=1, keepdims=True), v_ref[pl.ds(off, T), :]

        def weigh(J, sc, gone, acc, carr, diag):
            log_beta, local, _, v_j = sc
            w = jnp.exp((log_beta - gone) - local)
            if diag:
                w = w * causal
            acc = acc + jnp.dot(w.astype(BF16), v_j, preferred_element_type=F32)
            carr = jnp.where(lane == J, -gone[:T], carr)
            carr = jnp.where(lane == HEAD_DIM + J, -gone[T:], carr)
            return acc, carr

        def tiles(J, count, state, diag):
            gone, acc, carr, _ = state
            scs = [scores(J - u, diag and u == 0) for u in range(count)]
            for u, sc in enumerate(scs):
                acc, carr = weigh(J - u, sc, gone, acc, carr, diag and u == 0)
                gone = gone + sc[2]
            return gone, acc, carr, jnp.min(gone)

        U = SB_UNROLL
        alive = lambda st: st[3] < SB_DEAD
        state = (jnp.zeros((2 * T, 1), F32), jnp.zeros((2 * T, LANES), F32),
                 jnp.full((T, LANES), SB_UNSEEN, F32), jnp.zeros((), F32))
        state = lax.cond(I > 0, lambda st: tiles(I, 2, st, True), lambda st: tiles(I, 1, st, True), state)
        rest = jnp.maximum(I - 1, 0)
        singles = jnp.where(rest > 0, (rest - 1) % U + 1, 0)
        _, state = lax.while_loop(lambda c: (c[0] < singles) & alive(c[1]),
                                  lambda c: (c[0] + 1, tiles(I - 2 - c[0], 1, c[1], False)), (jnp.int32(0), state))
        blocks = (rest - singles) // U
        _, state = lax.while_loop(lambda c: (c[0] < blocks) & alive(c[1]),
                                  lambda c: (c[0] + 1, tiles(I - 2 - singles - U * c[0], U, c[1], False)),
                                  (jnp.int32(0), state))
        _, acc, carr, _ = state
        o_ref[...] = _sb_unstack_heads(acc)
        c_ref[...] = carr
        hosted.end(grid)

    blk = lambda I_off: pl.BlockSpec((T, LANES), lambda p, I: (I, I_off + p))
    full = lambda off: pl.BlockSpec((S, LANES), lambda p, I: (0, off + p))
    const = lambda rows: pl.BlockSpec((rows, T), lambda p, I: (0, 0))
    o, carries, *got = pl.pallas_call(
        body, name=name, grid=grid,
        out_shape=[jax.ShapeDtypeStruct((S, W), F32), jax.ShapeDtypeStruct((S, W), F32)] + hosted.out_shapes,
        in_specs=[blk(0), full(npair), full(2 * npair), const(2 * T), const(T)] + hosted.in_specs,
        out_specs=[blk(0), blk(0)] + hosted.out_specs,
        scratch_shapes=hosted.scratch,
        compiler_params=_cparams(hosted.semantics(("parallel", "arbitrary"))),
    )(p_sb, p_sb, p_sb, _sb_causal(T).astype(F32), _sb_triangle(T, True), *hosted.operands)
    return (o, carries) if comm is None else (o, carries, got)


def _sb_bwd(p_sb, do, carries, *, name, comm=None):
    S = p_sb.shape[0]
    W = p_sb.shape[1] // 3
    npair = W // LANES
    T = SB_TILE
    n_tiles = S // T
    scale = HEAD_DIM ** -0.5

    grid = (npair, n_tiles)
    hosted = _Hosted(comm, n_in=8, n_out=3, n_scratch=0)

    def body(*refs):
        (q_ref, k_ref, v_ref, do_ref, c_ref, causal_ref, later_ref, earlier_ref,
         dq_ref, dk_ref, dv_ref) = hosted.begin(refs, grid)
        I = pl.program_id(1)

        @pl.when(I == 0)
        def _():
            dk_ref[...] = jnp.zeros_like(dk_ref)
            dv_ref[...] = jnp.zeros_like(dv_ref)

        lane = lax.broadcasted_iota(jnp.int32, (T, LANES), 1)
        causal = causal_ref[...]
        later_than = later_ref[...]
        earlier_than = earlier_ref[...]
        q2 = _sb_stack_heads(q_ref[...], scale)
        do2 = _sb_stack_heads(do_ref[...].astype(BF16))
        carr = c_ref[...]
        tn_dims = (((0,), (0,)), ((), ()))

        def chain(J, diag):
            off = pl.multiple_of(J * T, T)
            k_j = k_ref[pl.ds(off, T), :]
            v_j = v_ref[pl.ds(off, T), :]
            log_beta, stay = _sb_scores(q2, k_j)
            if diag:
                stay = stay * causal
            lc = jnp.concatenate(
                [jnp.sum(jnp.where(lane == HEAD_DIM * h + J, carr, 0.0), axis=1, keepdims=True) for h in range(2)],
                axis=0)
            w = jnp.exp((log_beta + lc) - jnp.dot(stay.astype(BF16), later_than, preferred_element_type=F32))
            if diag:
                w = w * causal
            dw = lax.dot_general(do2, v_j, (((1,), (1,)), ((), ())), preferred_element_type=F32)
            e = w * dw
            local = jnp.dot(e.astype(BF16), earlier_than, preferred_element_type=F32)
            return off, k_j, w, e, local, jnp.exp(log_beta), jnp.sum(e, axis=1, keepdims=True)

        def finish(ch, ec, dq_acc, diag):
            off, k_j, w, e, local, beta, _ = ch
            e_before = local + ec
            dz = e - beta * (e + e_before)
            if diag:
                dz = dz * causal
            dzb = dz.astype(BF16)
            dq_acc = dq_acc + jnp.dot(dzb, k_j, preferred_element_type=F32)
            dk_ref[pl.ds(off, T), :] += lax.dot_general(dzb, q2, tn_dims, preferred_element_type=F32)
            dv_ref[pl.ds(off, T), :] += lax.dot_general(w.astype(BF16), do2, tn_dims, preferred_element_type=F32)
            return dq_acc

        def tiles(J, count, state, diag):
            ec, dq_acc = state
            chains = [chain(J + u, diag and u == count - 1) for u in range(count)]
            for u, ch in enumerate(chains):
                dq_acc = finish(ch, ec, dq_acc, diag and u == count - 1)
                ec = ec + ch[6]
            return ec, dq_acc

        lane_row = lax.broadcasted_iota(jnp.int32, (1, LANES), 1)
        reached = (jnp.max(carr, axis=0, keepdims=True) > 0.5 * SB_UNSEEN) & (lane_row < HEAD_DIM)
        first = jnp.min(jnp.where(reached, lane_row.astype(F32), float(n_tiles))).astype(jnp.int32)
        U = SB_UNROLL
        count = I - first
        rest = jnp.maximum(count - 1, 0)
        state = (jnp.zeros((2 * T, 1), F32), jnp.zeros((2 * T, LANES), F32))
        state = lax.fori_loop(0, rest // U, lambda jj, st: tiles(first + U * jj, U, st, False), state)
        state = lax.fori_loop(0, rest % U, lambda r, st: tiles(I - 1 - rest % U + r, 1, st, False), state)
        _, dq_acc = lax.cond(count > 0, lambda st: tiles(I - 1, 2, st, True), lambda st: tiles(I, 1, st, True), state)
        dq_ref[...] = _sb_unstack_heads(dq_acc) * scale
        hosted.end(grid)

    blk = lambda src_off: pl.BlockSpec((T, LANES), lambda p, I: (I, src_off + p))
    full = lambda off: pl.BlockSpec((S, LANES), lambda p, I: (0, off + p))
    const = lambda rows: pl.BlockSpec((rows, T), lambda p, I: (0, 0))
    dq, dk, dv, *got = pl.pallas_call(
        body, name=name, grid=grid,
        out_shape=[jax.ShapeDtypeStruct((S, W), F32)] * 3 + hosted.out_shapes,
        in_specs=[blk(0), full(npair), full(2 * npair), blk(0), blk(0), const(2 * T), const(T), const(T)]
        + hosted.in_specs,
        out_specs=[blk(0), full(0), full(0)] + hosted.out_specs,
        scratch_shapes=hosted.scratch,
        compiler_params=_cparams(hosted.semantics(("parallel", "arbitrary"))),
    )(p_sb, p_sb, p_sb, do, carries, _sb_causal(T).astype(F32), _sb_triangle(T, True), _sb_triangle(T, False),
      *hosted.operands)
    return (dq, dk, dv) if comm is None else (dq, dk, dv, got)


def _dil_blocks(b, body_fn):
    for pi, (window, dil) in enumerate(DILATED_PATTERNS):
        assert window // dil == DIL_BLOCK
        nblk = DIL_SUPER // (DIL_BLOCK * dil)
        assert (dil * nblk) % DIL_UNROLL == 0

        def group(g, _, pi=pi, dil=dil, nblk=nblk):
            for u in range(DIL_UNROLL):
                t = g * DIL_UNROLL + u
                n = t % nblk
                body_fn(pi, dil, t // nblk, n, b * nblk + n)
            return 0

        lax.fori_loop(0, dil * nblk // DIL_UNROLL, group, 0)


def _dil_rows(start, size, dil):
    if dil == 1:
        return pl.ds(pl.multiple_of(start, DIL_BLOCK), size)
    return pl.ds(start, size, stride=dil)


def _dil_fill_bias(bias_ref):
    row = lax.broadcasted_iota(jnp.int32, (2 * DIL_BLOCK, 2 * DIL_BLOCK), 0)
    kk = lax.broadcasted_iota(jnp.int32, (2 * DIL_BLOCK, 2 * DIL_BLOCK), 1)
    qi = jnp.where(row >= DIL_BLOCK, row - DIL_BLOCK, row)
    for s in range(2):
        dist = s * DIL_BLOCK + qi - kk
        bias_ref[s] = jnp.where((dist >= 0) & (dist <= DIL_BLOCK), 0.0, NEG_BIG)


def _dl_fwd(p_dl, *, name):
    S, W = p_dl.shape[0], p_dl.shape[1] // 3
    npair = W // LANES
    nsuper = S // DIL_SUPER
    assert S % DIL_SUPER == 0 and S // max(d for _, d in DILATED_PATTERNS) >= 2 * DIL_BLOCK
    scale = HEAD_DIM ** -0.5
    npat = len(DILATED_PATTERNS)

    def body(q_ref, k_ref, v_ref, o_ref, l_ref, bias_ref, *pattern_refs):
        op_refs, lp_refs = pattern_refs[:npat], pattern_refs[npat:]
        b = pl.program_id(1)
        masks = _head_masks((DIL_BLOCK, LANES))
        pl.when(b == 0)(lambda: _dil_fill_bias(bias_ref))

        def block(pi, dil, c, n, gn):
            ws = jnp.maximum(gn - 1, 0)
            qrows = n * (DIL_BLOCK * dil) + c
            krows = ws * (DIL_BLOCK * dil) + c
            q_idx = _dil_rows(qrows, DIL_BLOCK, dil)
            k_idx = _dil_rows(krows, 2 * DIL_BLOCK, dil)
            qb = q_ref[q_idx, :]
            kb = k_ref[k_idx, :].astype(BF16)
            vb = v_ref[k_idx, :].astype(BF16)
            q2 = _sb_stack_heads(qb.astype(BF16), scale)
            z = lax.dot_general(q2, kb, (((1,), (1,)), ((), ())), preferred_element_type=F32) + bias_ref[gn - ws]
            m = jnp.max(z, axis=1, keepdims=True)
            p = jnp.exp(z - m)
            den = jnp.sum(p, axis=1, keepdims=True)
            acc = jnp.dot(p.astype(BF16), vb, preferred_element_type=F32)
            lse = m + jnp.log(den)
            op_refs[pi][q_idx, :] = _sb_unstack_heads(acc / den)
            lp_refs[pi][q_idx, :] = jnp.where(masks[0], lse[:DIL_BLOCK], lse[DIL_BLOCK:])

        _dil_blocks(b, block)
        lses = [r[...] for r in lp_refs]
        top = functools.reduce(jnp.maximum, lses)
        ws_ = [jnp.exp(l - top) for l in lses]
        den = functools.reduce(jnp.add, ws_)
        num = functools.reduce(jnp.add, [w * r[...] for r, w in zip(op_refs, ws_)])
        o_ref[...] = num / den
        l_ref[...] = top + jnp.log(den)

    blk = pl.BlockSpec((DIL_SUPER, LANES), lambda p, b: (b, p))
    full = lambda off: pl.BlockSpec((S, LANES), lambda p, b: (0, off + p))
    return pl.pallas_call(
        body, name=name, grid=(npair, nsuper),
        out_shape=[jax.ShapeDtypeStruct((S, W), F32)] * 2,
        in_specs=[blk, full(npair), full(2 * npair)], out_specs=[blk, blk],
        scratch_shapes=[pltpu.VMEM((2, 2 * DIL_BLOCK, 2 * DIL_BLOCK), F32)]
        + [pltpu.VMEM((DIL_SUPER, LANES), F32)] * (2 * npat),
        compiler_params=_cparams(("arbitrary", "arbitrary")),
    )(p_dl, p_dl, p_dl)


def _dl_bwd(p_dl, o, lse, do, *, name):
    S, W = p_dl.shape[0], p_dl.shape[1] // 3
    npair = W // LANES
    nsuper = S // DIL_SUPER
    scale = HEAD_DIM ** -0.5

    def body(q_ref, k_ref, v_ref, o_ref, l_ref, do_ref, dq_ref, dk_ref, dv_ref, delta_ref, bias_ref):
        b = pl.program_id(1)

        @pl.when(b == 0)
        def _():
            dk_ref[...] = jnp.zeros_like(dk_ref)
            dv_ref[...] = jnp.zeros_like(dv_ref)
            _dil_fill_bias(bias_ref)

        dq_ref[...] = jnp.zeros_like(dq_ref)
        prod = do_ref[...] * o_ref[...]
        delta = jnp.zeros_like(prod)
        for hm in _head_masks(prod.shape):
            delta = jnp.where(hm, jnp.sum(jnp.where(hm, prod, 0.0), axis=1, keepdims=True), delta)
        delta_ref[...] = delta

        def block(pi, dil, c, n, gn):
            ws = jnp.maximum(gn - 1, 0)
            qrows = n * (DIL_BLOCK * dil) + c
            krows = ws * (DIL_BLOCK * dil) + c
            q_idx = _dil_rows(qrows, DIL_BLOCK, dil)
            k_idx = _dil_rows(krows, 2 * DIL_BLOCK, dil)
            qb = q_ref[q_idx, :]
            dob = do_ref[q_idx, :]
            lb = l_ref[q_idx, :]
            db = delta_ref[q_idx, :]
            kb = k_ref[k_idx, :].astype(BF16)
            vb = v_ref[k_idx, :].astype(BF16)
            q2 = _sb_stack_heads(qb.astype(BF16), scale)
            do2 = _sb_stack_heads(dob.astype(BF16))
            lse2 = jnp.concatenate([lb[:, HEAD_DIM * h:HEAD_DIM * h + 1] for h in range(2)], axis=0)
            delta2 = jnp.concatenate([db[:, HEAD_DIM * h:HEAD_DIM * h + 1] for h in range(2)], axis=0)
            z = lax.dot_general(q2, kb, (((1,), (1,)), ((), ())), preferred_element_type=F32)
            p = jnp.exp((z + bias_ref[gn - ws]) - lse2)
            dp = lax.dot_general(do2, vb, (((1,), (1,)), ((), ())), preferred_element_type=F32)
            dzb = (p * (dp - delta2)).astype(BF16)
            tn_dims = (((0,), (0,)), ((), ()))
            dq_blk = _sb_unstack_heads(jnp.dot(dzb, kb, preferred_element_type=F32)) * scale
            dk_blk = lax.dot_general(dzb, q2, tn_dims, preferred_element_type=F32)
            dv_blk = lax.dot_general(p.astype(BF16), do2, tn_dims, preferred_element_type=F32)
            dq_ref[q_idx, :] = dq_ref[q_idx, :] + dq_blk
            dk_ref[k_idx, :] = dk_ref[k_idx, :] + dk_blk
            dv_ref[k_idx, :] = dv_ref[k_idx, :] + dv_blk

        _dil_blocks(b, block)

    blk = pl.BlockSpec((DIL_SUPER, LANES), lambda p, b: (b, p))
    full = lambda off: pl.BlockSpec((S, LANES), lambda p, b: (0, off + p))
    return pl.pallas_call(
        body, name=name, grid=(npair, nsuper),
        out_shape=[jax.ShapeDtypeStruct((S, W), F32)] * 3,
        in_specs=[blk, full(npair), full(2 * npair), blk, blk, blk], out_specs=[blk, full(0), full(0)],
        scratch_shapes=[pltpu.VMEM((DIL_SUPER, LANES), F32), pltpu.VMEM((2, 2 * DIL_BLOCK, 2 * DIL_BLOCK), F32)],
        compiler_params=_cparams(("arbitrary", "arbitrary")),
    )(p_dl, p_dl, p_dl, o, lse, do)


class _NoExchange:
    def gather(self, family):
        return None

    def gathered(self, family, got, weights):
        pass

    def send(self, family, grads):
        return None

    def received(self, family, got):
        pass


def _local_step(x, target, gains, weights, exchanges=None):
    S, D = x.shape
    ex = exchanges or _NoExchange()
    weights = dict(weights)
    d_sb = gains["sb_out_norm"].shape[1]
    d_dl = gains["dil_out_norm"].shape[1]
    cos_t, sin_t = _rope_tables(S)

    riders = ("ffn1_w_up", "ffn1_w_down", "mixer") if exchanges else (None, None, None)
    (x1, h2), saved1 = _ffn_fwd(x, gains["ffn1_norm"], weights, tag="ffn1", ex=ex, riders=riders,
                                first_rider="ffn1_w_gate" if exchanges else None,
                                next_gain=gains["mix_norm"])
    w_in = weights["w_in"]
    w_out = weights["w_out"]
    p_sb = _mm(h2, w_in, b_cols=(0, 3 * d_sb), outs=(BF16,), name="proj_sb")

    def rope_qk(acc, _, cos, sin):
        return jnp.concatenate([_rotate(acc[:, :2 * d_dl], cos, sin, 1.0), acc[:, 2 * d_dl:]], axis=1)

    p_dl = _mm(h2, w_in, b_cols=(3 * d_sb, 3 * d_dl), lanes=(cos_t, sin_t), epilogue=rope_qk, name="proj_dl_rope")
    plan = ex.gather("ffn2" if exchanges else None)
    o_sb, carries, *got = _sb_fwd(p_sb, name="sb_fwd", comm=plan)
    ex.gathered("ffn2", got[0] if got else None, weights)
    o_dl, lse_dl = _dl_fwd(p_dl, name="dl_fwd")
    merged = _rms_fwd([o_sb, o_dl], [gains["sb_out_norm"], gains["dil_out_norm"]], name="out_norm")
    x2, h3 = _mm(merged, w_out, res=x1, rows=(gains["ffn2_norm"],), outs=(F32, BF16),
                 epilogue=_residual_then_norm(1.0), name="out_proj_norm")
    (dx3, d_final, loss_wide), saved2 = _ffn_fwd(x2, gains["ffn2_norm"], weights, tag="ffn2", ex=ex, h=h3,
                                                 head=(gains["final_norm"], target))
    loss_row = loss_wide[:, :LANES]

    dx2, d_ffn2_norm, dwg2, dwu2, dwd2 = _ffn_bwd(dx3, x2, gains["ffn2_norm"], weights, saved2, tag="ffn2", ex=ex)
    d_w_out = _mm(merged, dx2, ta=True, outs=(GRAD_WIRE,), name="d_w_out")
    d_merged = _mm(dx2, w_out, tb=True, name="d_merged")
    (do_sb, do_dl), (d_sb_norm, d_dl_norm) = _rms_bwd(
        d_merged, [o_sb, o_dl], [gains["sb_out_norm"], gains["dil_out_norm"]], None, name="out_norm_bwd")
    plan = ex.send("ffn2", dict(ffn2_w_gate=dwg2, ffn2_w_up=dwu2, ffn2_w_down=dwd2))
    dq_sb, dk_sb, dv_sb, *got = _sb_bwd(p_sb, do_sb, carries, name="sb_bwd", comm=plan)
    ex.received("ffn2", got[0] if got else None)
    dq_dl, dk_dl, dv_dl = _dl_bwd(p_dl, o_dl, lse_dl, do_dl, name="dl_bwd")
    d_proj = _join_d_proj([dq_sb, dk_sb, dv_sb, dq_dl, dk_dl, dv_dl], (3, 4), cos_t, sin_t, name="d_proj")
    d_w_in = _mm(h2, d_proj, ta=True, outs=(GRAD_WIRE,), name="d_w_in")
    dx1, d_mix_norm = _mm(d_proj, w_in, tb=True, extras=(x1, dx2), rows=(gains["mix_norm"],), row_sums=1,
                          epilogue=_rms_bwd_epilogue, name="dh_mix_norm_bwd")
    dx, d_ffn1_norm, dwg1, dwu1, dwd1 = _ffn_bwd(
        dx1, x, gains["ffn1_norm"], weights, saved1, tag="ffn1", ex=ex,
        rider=("mixer", dict(w_in=d_w_in, w_out=d_w_out)), spread=True)
    gain_grads = dict(ffn1_norm=d_ffn1_norm, mix_norm=d_mix_norm, sb_out_norm=d_sb_norm, dil_out_norm=d_dl_norm,
                      ffn2_norm=d_ffn2_norm, final_norm=d_final)
    weight_grads = dict(ffn1_w_gate=dwg1, ffn1_w_up=dwu1, ffn1_w_down=dwd1, w_in=d_w_in, w_out=d_w_out,
                        ffn2_w_gate=dwg2, ffn2_w_up=dwu2, ffn2_w_down=dwd2)
    return loss_row, dx, gain_grads, weight_grads


def _mesh_position():
    return lax.axis_index("x"), lax.axis_index("y"), lax.axis_index("c")


def _flip(coord, bit):
    return 1 - coord if bit else coord


RELATIONS = [(rx, ry, rc) for rx in (0, 1) for ry in (0, 1) for rc in (0, 1)][1:]


class _GatherPlan:
    def __init__(self, shards):
        n = len(shards)
        self.operands = list(shards)
        self.out_shapes = [jax.ShapeDtypeStruct((N_DEV,) + s.shape, s.dtype) for s in shards]
        self.scratch = [pltpu.SemaphoreType.DMA((n, 7)), pltpu.SemaphoreType.DMA((n, 7)),
                        pltpu.SemaphoreType.DMA((n,))]

    def _copies(self, in_refs, out_refs, sems):
        send_sems, recv_sems, local_sems = sems
        x, y, c = _mesh_position()
        me, sibling = (x, y, c), (x, y, 1 - c)
        chips = [(1 - x, y), (x, 1 - y), (1 - x, 1 - y)]
        plans = []
        for t, (x_ref, out_ref) in enumerate(zip(in_refs, out_refs)):
            def slot(px, py, pc, out_ref=out_ref):
                return out_ref.at[4 * px + 2 * py + pc]

            def copy(k, block, to, src=None, t=t, slot=slot):
                return pltpu.make_async_remote_copy(
                    src_ref=slot(*block) if src is None else src, dst_ref=slot(*block),
                    send_sem=send_sems.at[t, k], recv_sem=recv_sems.at[t, k],
                    device_id=to, device_id_type=pl.DeviceIdType.MESH)

            plans.append(dict(
                mine=pltpu.make_async_copy(x_ref, slot(*me), local_sems.at[t]),
                first=[copy(0, me, sibling, src=x_ref)]
                + [copy(1 + j, me, (*chip, c), src=x_ref) for j, chip in enumerate(chips)],
                over_ici=[copy(1 + j, (*chip, c), me) for j, chip in enumerate(chips)],
                passed=[copy(4 + j, (*chip, c), sibling) for j, chip in enumerate(chips)],
                from_sibling=[copy(0, sibling, me)] + [copy(4 + j, (*chip, 1 - c), me) for j, chip in enumerate(chips)]))
        return plans

    def start(self, in_refs, out_refs, sems):
        for p in self._copies(in_refs, out_refs, sems):
            p["mine"].start()
            for cp in p["first"]:
                cp.start()

    def finish(self, in_refs, out_refs, sems):
        plans = self._copies(in_refs, out_refs, sems)
        for p in plans:
            for arrived, onward in zip(p["over_ici"], p["passed"]):
                arrived.wait_recv()
                onward.start()
        for p in plans:
            for cp in p["from_sibling"]:
                cp.wait_recv()
            for cp in p["first"] + p["passed"]:
                cp.wait_send()
            p["mine"].wait()


class _Hosted:
    def __init__(self, plan, n_in, n_out, n_scratch):
        self.plan, self.n_in, self.n_out, self.n_scratch = plan, n_in, n_out, n_scratch
        self.operands = list(plan.operands) if plan else []
        self.out_shapes = list(plan.out_shapes) if plan else []
        self.scratch = list(plan.scratch) if plan else []
        self.in_specs = [pl.BlockSpec(memory_space=pl.ANY)] * len(self.operands)
        self.out_specs = [pl.BlockSpec(memory_space=pl.ANY)] * len(self.out_shapes)

    def semantics(self, sem):
        return sem if self.plan is None else ("arbitrary",) * len(sem)

    def _at(self, grid, last):
        hit = None
        for d, n in enumerate(grid):
            here = pl.program_id(d) == (n - 1 if last else 0)
            hit = here if hit is None else hit & here
        return hit

    def begin(self, refs, grid):
        if self.plan is None:
            return refs
        k_in, k_out = len(self.operands), len(self.out_shapes)
        ins, rest = refs[:self.n_in], refs[self.n_in:]
        c_in, rest = rest[:k_in], rest[k_in:]
        outs, rest = rest[:self.n_out], rest[self.n_out:]
        c_out, rest = rest[:k_out], rest[k_out:]
        scratch, sems = rest[:self.n_scratch], rest[self.n_scratch:]
        self._args = (c_in, c_out, sems)
        pl.when(self._at(grid, False))(lambda: self.plan.start(*self._args))
        return tuple(ins) + tuple(outs) + tuple(scratch)

    def end(self, grid):
        if self.plan is not None:
            pl.when(self._at(grid, True))(lambda: self.plan.finish(*self._args))


class _ExchangePlan:
    def __init__(self, packs):
        n = len(packs)
        self.operands = list(packs)
        self.out_shapes = [jax.ShapeDtypeStruct(p.shape, p.dtype) for p in packs]
        self.scratch = [pltpu.SemaphoreType.DMA((n, 7)), pltpu.SemaphoreType.DMA((n, 7)),
                        pltpu.SemaphoreType.DMA((n,))]

    def _copies(self, in_refs, out_refs, sems):
        send_sems, recv_sems, local_sems = sems
        x, y, c = _mesh_position()
        me = 4 * x + 2 * y + c
        copies = [pltpu.make_async_copy(i.at[me], o.at[me], local_sems.at[t])
                  for t, (i, o) in enumerate(zip(in_refs, out_refs))]
        for r, (rx, ry, rc) in enumerate(RELATIONS):
            px, py, pc = _flip(x, rx), _flip(y, ry), _flip(c, rc)
            peer = 4 * px + 2 * py + pc
            copies += [pltpu.make_async_remote_copy(
                src_ref=i.at[peer], dst_ref=o.at[me], send_sem=send_sems.at[t, r], recv_sem=recv_sems.at[t, r],
                device_id=(px, py, pc), device_id_type=pl.DeviceIdType.MESH)
                for t, (i, o) in enumerate(zip(in_refs, out_refs))]
        return copies

    def start(self, in_refs, out_refs, sems):
        for cp in self._copies(in_refs, out_refs, sems):
            cp.start()

    def finish(self, in_refs, out_refs, sems):
        for cp in self._copies(in_refs, out_refs, sems):
            cp.wait()


def _all_reduce_rows(v, *, name):
    R, C = v.shape

    def body(v_ref, out_ref, buf, send_sems, recv_sems):
        x, y, c = _mesh_position()
        me = 4 * x + 2 * y + c
        buf[me] = v_ref[...]
        copies = []
        for r, (rx, ry, rc) in enumerate(RELATIONS):
            cp = pltpu.make_async_remote_copy(
                src_ref=v_ref, dst_ref=buf.at[me], send_sem=send_sems.at[r], recv_sem=recv_sems.at[r],
                device_id=(_flip(x, rx), _flip(y, ry), _flip(c, rc)), device_id_type=pl.DeviceIdType.MESH)
            cp.start()
            copies.append(cp)
        for cp in copies:
            cp.wait()
        total = buf[0]
        for s in range(1, N_DEV):
            total = total + buf[s]
        out_ref[...] = total

    return pl.pallas_call(
        body, name=name,
        out_shape=jax.ShapeDtypeStruct((R, C), F32),
        in_specs=[pl.BlockSpec(memory_space=pltpu.VMEM)],
        out_specs=pl.BlockSpec(memory_space=pltpu.VMEM),
        scratch_shapes=[pltpu.VMEM((N_DEV, R, C), F32), pltpu.SemaphoreType.DMA((7,)), pltpu.SemaphoreType.DMA((7,))],
    )(v)


def _sum_slots(recv, *, name):
    _, R, C = recv.shape
    tr = _pick(R, (256, 208, 128, 64, 32, 16))

    def body(r_ref, o_ref):
        total = r_ref[0].astype(F32)
        for s in range(1, N_DEV):
            total = total + r_ref[s].astype(F32)
        o_ref[...] = total

    return pl.pallas_call(
        body, name=name, grid=(R // tr,),
        out_shape=jax.ShapeDtypeStruct((R, C), F32),
        in_specs=[pl.BlockSpec((N_DEV, tr, C), lambda i: (0, i, 0))],
        out_specs=pl.BlockSpec((tr, C), lambda i: (i, 0)),
        compiler_params=_cparams(("parallel",)),
    )(recv)


def _adamw(w, g, m, v, *, name):
    R, C = w.shape
    tr = _pick(R, (256, 128, 64, 32, 16, 8))

    def body(w_ref, g_ref, m_ref, v_ref, d_ref, nm_ref, nv_ref):
        g = g_ref[...]
        m_new = ADAM_B1 * m_ref[...] + (1.0 - ADAM_B1) * g
        v_new = ADAM_B2 * v_ref[...] + (1.0 - ADAM_B2) * (g * g)
        m_hat = m_new / (1.0 - ADAM_B1 ** ADAM_STEP)
        v_hat = v_new / (1.0 - ADAM_B2 ** ADAM_STEP)
        d_ref[...] = -ADAM_LR * (m_hat / (jnp.sqrt(v_hat) + ADAM_EPS) + ADAM_WD * w_ref[...])
        nm_ref[...] = m_new
        nv_ref[...] = v_new

    spec = pl.BlockSpec((tr, C), lambda i: (i, 0))
    return pl.pallas_call(
        body, name=name, grid=(R // tr,),
        out_shape=[jax.ShapeDtypeStruct((R, C), F32)] * 3,
        in_specs=[spec] * 4, out_specs=[spec] * 3,
        compiler_params=_cparams(("parallel",)),
    )(w, g, m, v)


WEIGHT_NAMES = ["ffn1_norm", "ffn1_w_gate", "ffn1_w_up", "ffn1_w_down", "mix_norm", "w_in", "sb_out_norm",
                "dil_out_norm", "w_out", "ffn2_norm", "ffn2_w_gate", "ffn2_w_up", "ffn2_w_down", "final_norm"]
GAIN_NAMES = ["ffn1_norm", "mix_norm", "sb_out_norm", "dil_out_norm", "ffn2_norm", "final_norm"]
COL_SHARDED = ["ffn1_w_gate", "ffn1_w_up", "ffn2_w_gate", "ffn2_w_up", "w_in"]
ROW_SHARDED = ["ffn1_w_down", "ffn2_w_down", "w_out"]
GROUPS = {"mixer": (["w_in"], ["w_out"]),
          "ffn2": (["ffn2_w_gate", "ffn2_w_up"], ["ffn2_w_down"])}
for _ffn in ("ffn1", "ffn2"):
    GROUPS.update({f"{_ffn}_w_gate": ([f"{_ffn}_w_gate"], []), f"{_ffn}_w_up": ([f"{_ffn}_w_up"], []),
                   f"{_ffn}_w_down": ([], [f"{_ffn}_w_down"])})


class _Exchanges:
    def __init__(self, params):
        self.params = params
        self.grads = {}

    def gather(self, group):
        if group is None:
            return None
        cols, rows = GROUPS[group]
        return _GatherPlan([self.params[n].astype(BF16) for n in cols + rows])

    def gathered(self, group, got, weights):
        if group is None:
            return
        cols, rows = GROUPS[group]
        for n, blocks in zip(cols + rows, got):
            if n in cols:
                weights[n] = jnp.transpose(blocks, (1, 0, 2)).reshape(blocks.shape[1], N_DEV * blocks.shape[2])
            else:
                weights[n] = blocks.reshape(N_DEV * blocks.shape[1], blocks.shape[2])

    def send(self, group, grads):
        if group is None:
            return None
        cols, rows = GROUPS[group]
        packs = [jnp.transpose(grads[n].reshape(grads[n].shape[0], N_DEV, self.params[n].shape[1]), (1, 0, 2))
                 for n in cols]
        packs += [grads[n].reshape(N_DEV, self.params[n].shape[0], grads[n].shape[1]) for n in rows]
        return _ExchangePlan([p.astype(GRAD_WIRE) for p in packs])

    def received(self, group, got):
        if group is None:
            return
        cols, rows = GROUPS[group]
        for n, slots in zip(cols + rows, got):
            self.grads[n] = _sum_slots(slots, name=f"sum_grads_{n}")


def _step(x, target, params, moments_m, moments_v):
    ex = _Exchanges(params)
    weights = {}
    gains = {n: params[n] for n in GAIN_NAMES}
    loss_row, grad_x, gain_grads, _ = _local_step(x, target, gains, weights, ex)
    grads = ex.grads

    rows = [gain_grads[n].reshape(-1, LANES) for n in GAIN_NAMES] + [loss_row]
    small = jnp.concatenate(rows, axis=0)
    pad = (-small.shape[0]) % 8
    small = jnp.pad(small, ((0, pad), (0, 0)))
    small = _all_reduce_rows(small, name="reduce_gains_loss")
    off = 0
    for n in GAIN_NAMES:
        r = gain_grads[n].shape[1] // LANES
        grads[n] = small[off:off + r].reshape(1, -1)
        off += r
    loss = small[off, 0]

    delta, new_m, new_v = {}, {}, {}
    for n in WEIGHT_NAMES:
        delta[n], new_m[n], new_v[n] = _adamw(params[n], grads[n], moments_m[n], moments_v[n], name=f"adamw_{n}")
    return loss, grad_x, grads, delta, new_m, new_v


def kernel(x, ffn1_norm, ffn1_w_gate, ffn1_w_up, ffn1_w_down, mix_norm, w_in, sb_out_norm, dil_out_norm, w_out, ffn2_norm, ffn2_w_gate, ffn2_w_up, ffn2_w_down, final_norm, loss_target, m_ffn1_norm, m_ffn1_w_gate, m_ffn1_w_up, m_ffn1_w_down, m_mix_norm, m_w_in, m_sb_out_norm, m_dil_out_norm, m_w_out, m_ffn2_norm, m_ffn2_w_gate, m_ffn2_w_up, m_ffn2_w_down, m_final_norm, v_ffn1_norm, v_ffn1_w_gate, v_ffn1_w_up, v_ffn1_w_down, v_mix_norm, v_w_in, v_sb_out_norm, v_dil_out_norm, v_w_out, v_ffn2_norm, v_ffn2_w_gate, v_ffn2_w_up, v_ffn2_w_down, v_final_norm):
    given = dict(locals())
    shapes = {n: given[n].shape for n in WEIGHT_NAMES}

    def as2d(a):
        return a.reshape(1, -1) if a.ndim == 1 else a.reshape(a.shape[-2], a.shape[-1])

    params = {n: as2d(given[n]) for n in WEIGHT_NAMES}
    moments_m = {n: as2d(given["m_" + n]) for n in WEIGHT_NAMES}
    moments_v = {n: as2d(given["v_" + n]) for n in WEIGHT_NAMES}
    loss, grad_x, grads, delta, new_m, new_v = _step(x[0], loss_target[0], params, moments_m, moments_v)
    back = lambda d: [d[n].reshape(shapes[n]) for n in WEIGHT_NAMES]
    return (loss, grad_x[None], *back(grads), *back(delta), *back(new_m), *back(new_v))
```

```python
import functools

import jax
import jax.numpy as jnp
from jax import lax
from jax.experimental import pallas as pl
from jax.experimental.pallas import tpu as pltpu

F32 = jnp.float32
BF16 = jnp.bfloat16
GRAD_WIRE = jnp.bfloat16

N_DEV = 8
HEAD_DIM = 64
LANES = 128
DILATED_PATTERNS = ((128, 1), (512, 4), (2048, 16))
DIL_BLOCK = 128
DIL_SUPER = 2048
DIL_UNROLL = 16
SB_TILE = 256
SB_LANES = 128
SB_UNROLL = 4
SB_DEAD = 90.0
SB_UNSEEN = -1e30
ROPE_THETA = 10000.0
RMS_EPS = 1e-6
HALF_STEP = 0.5
ADAM_LR = 0.001
ADAM_B1 = 0.9
ADAM_B2 = 0.999
ADAM_EPS = 1e-08
ADAM_WD = 0.01
ADAM_STEP = 10
NEG_BIG = -1e30
VMEM_CAP_MB = 60


def _pick(n, prefs):
    for p in prefs:
        if n % p == 0:
            return p
    return n


MM_MAX_TILE = 1536
MM_WHOLE = 3072


def _largest_tile(n, cap):
    if n <= cap:
        return n
    for t in range(cap - cap % LANES, 0, -LANES):
        if n % t == 0:
            return t
    return n


def _cparams(sem=None, vmem_mb=48):
    return pltpu.CompilerParams(dimension_semantics=sem, vmem_limit_bytes=min(vmem_mb, VMEM_CAP_MB) * 1024 * 1024)


def _nbytes(shape, dtype):
    n = 1
    for s in shape:
        n *= s
    return n * jnp.dtype(dtype).itemsize


def _mm(a, b, *, name, ta=False, tb=False, outs=(F32,), res=None, alpha=1.0, extras=(), epilogue=None,
        tm=None, tn=None, tk=None, comm=None, rows=(), lanes=(), row_sums=0, b_cols=None):
    if ta:
        K, M = a.shape
    else:
        M, K = a.shape
    if tb:
        N, Kb = b.shape
    else:
        Kb, N = b.shape
    col0 = 0
    if b_cols is not None:
        assert not tb
        col0, N = b_cols
    assert K == Kb, (a.shape, b.shape, ta, tb)
    tn = tn or (N if (not ta and K <= MM_WHOLE and N <= MM_WHOLE) else _largest_tile(N, MM_MAX_TILE))
    tm = tm or (_largest_tile(M, MM_MAX_TILE) if ta else _pick(M, (512, 256, 128) if tn <= MM_MAX_TILE else (256, 128)))
    tk = tk or (K if K <= MM_WHOLE else _pick(K, (2048, 1024, 512, 256, 128)))
    nk = K // tk
    a_spec = pl.BlockSpec((tk, tm), lambda i, j, k: (k, i)) if ta else pl.BlockSpec((tm, tk), lambda i, j, k: (i, k))
    assert col0 % tn == 0
    b_spec = (pl.BlockSpec((tn, tk), lambda i, j, k: (j, k)) if tb
              else pl.BlockSpec((tk, tn), lambda i, j, k: (k, j + col0 // tn)))
    mn_spec = pl.BlockSpec((tm, tn), lambda i, j, k: (i, j))
    dims = (((0 if ta else 1,), (1 if tb else 0,)), ((), ()))
    row_spec = pl.BlockSpec((1, tn), lambda i, j, k: (0, j))
    lane_spec = pl.BlockSpec((tm, LANES), lambda i, j, k: (i, 0))
    n_extra = len(extras) + (1 if res is not None else 0) + len(rows) + len(lanes)
    n_mn = len(outs)
    n_out = n_mn + row_sums
    assert row_sums == 0 or tn == N
    grid = (M // tm, N // tn, nk)
    hosted = _Hosted(comm, n_in=2 + n_extra, n_out=n_out, n_scratch=1 if nk > 1 else 0)

    def body(*refs):
        a_ref, b_ref = refs[0], refs[1]
        in_refs = refs[2:2 + n_extra]
        refs = hosted.begin(refs, grid)
        out_refs = refs[2 + n_extra:2 + n_extra + n_out]
        prod = lax.dot_general(a_ref[...].astype(BF16), b_ref[...].astype(BF16), dims, preferred_element_type=F32)

        def finish(acc):
            blocks = [r[...] for r in in_refs]
            if res is not None:
                r_blk, blocks = blocks[0], blocks[1:]
            else:
                r_blk = None
            if epilogue is None:
                val = acc * alpha
                if r_blk is not None:
                    val = val + r_blk
                vals = (val,)
            else:
                vals = epilogue(acc, r_blk, *blocks)
                vals = vals if isinstance(vals, (tuple, list)) else (vals,)
            for o_ref, v in zip(out_refs[:n_mn], vals[:n_mn]):
                o_ref[...] = v.astype(o_ref.dtype)
            first_rows = pl.program_id(0) == 0
            for o_ref, part in zip(out_refs[n_mn:], vals[n_mn:]):
                @pl.when(first_rows)
                def _(o_ref=o_ref, part=part):
                    o_ref[...] = part

                @pl.when(jnp.logical_not(first_rows))
                def _(o_ref=o_ref, part=part):
                    o_ref[...] += part

        if nk == 1:
            finish(prod)
        else:
            acc_ref = refs[2 + n_extra + n_out]
            k = pl.program_id(2)

            @pl.when(k == 0)
            def _():
                acc_ref[...] = prod

            @pl.when(k > 0)
            def _():
                acc_ref[...] += prod

            @pl.when(k == nk - 1)
            def _():
                finish(acc_ref[...])

        hosted.end(grid)

    mn_operands = ([res] if res is not None else []) + list(extras)
    operands = [a, b] + mn_operands + list(rows) + list(lanes)
    in_specs = [a_spec, b_spec] + [mn_spec] * len(mn_operands) + [row_spec] * len(rows) + [lane_spec] * len(lanes)
    est = 2 * (_nbytes((tm, tk), a.dtype) + _nbytes((tk, tn), b.dtype))
    est += 2 * sum(_nbytes((tm, tn), o.dtype) for o in mn_operands)
    est += 2 * sum(_nbytes((tm, tn), d) for d in outs) + 2 * _nbytes((tm, tn), F32)
    semantics = ("parallel", "parallel", "arbitrary") if row_sums == 0 else ("arbitrary",) * 3
    result = pl.pallas_call(
        body, name=name, grid=grid,
        out_shape=[jax.ShapeDtypeStruct((M, N), d) for d in outs]
        + [jax.ShapeDtypeStruct((1, N), F32)] * row_sums + hosted.out_shapes,
        in_specs=in_specs + hosted.in_specs,
        out_specs=[mn_spec] * n_mn + [row_spec] * row_sums + hosted.out_specs,
        scratch_shapes=([pltpu.VMEM((tm, tn), F32)] if nk > 1 else []) + hosted.scratch,
        compiler_params=_cparams(hosted.semantics(semantics), vmem_mb=max(32, 2 * est // (1024 * 1024))),
    )(*operands, *hosted.operands)
    own, got = result[:n_out], list(result[n_out:])
    own = own[0] if n_out == 1 else own
    return own if comm is None else (own, got)


def _rms_hat(x):
    r = lax.rsqrt(jnp.mean(x * x, axis=-1, keepdims=True) + RMS_EPS)
    return x * r, r


def _rms_fwd(xs, gains, *, name, comm=None):
    S = xs[0].shape[0]
    widths = [x.shape[1] for x in xs]
    tm = _pick(S, (512, 256, 128))
    n = len(xs)
    grid = (S // tm,)
    hosted = _Hosted(comm, n_in=2 * n, n_out=1, n_scratch=0)

    def body(*refs):
        refs = hosted.begin(refs, grid)
        o_ref = refs[2 * n]
        off = 0
        for i in range(n):
            xh, _ = _rms_hat(refs[i][...])
            o_ref[:, off:off + widths[i]] = (xh * refs[n + i][...]).astype(o_ref.dtype)
            off += widths[i]
        hosted.end(grid)

    out, *got = pl.pallas_call(
        body, name=name, grid=grid,
        out_shape=[jax.ShapeDtypeStruct((S, sum(widths)), BF16)] + hosted.out_shapes,
        in_specs=[pl.BlockSpec((tm, w), lambda i: (i, 0)) for w in widths]
        + [pl.BlockSpec((1, w), lambda i: (0, 0)) for w in widths] + hosted.in_specs,
        out_specs=[pl.BlockSpec((tm, sum(widths)), lambda i: (i, 0))] + hosted.out_specs,
        scratch_shapes=hosted.scratch,
        compiler_params=_cparams(hosted.semantics(("parallel",))),
    )(*xs, *gains, *hosted.operands)
    return out if comm is None else (out, got)


def _rms_bwd(dh, xs, gains, res, *, name):
    S = xs[0].shape[0]
    widths = [x.shape[1] for x in xs]
    tm = _pick(S, (512, 256, 128))
    n = len(xs)
    has_res = res is not None

    def body(*refs):
        dh_ref = refs[0]
        x_refs = refs[1:1 + n]
        g_refs = refs[1 + n:1 + 2 * n]
        r_ref = refs[1 + 2 * n] if has_res else None
        base = 1 + 2 * n + (1 if has_res else 0)
        dx_refs = refs[base:base + n]
        dg_refs = refs[base + n:base + 2 * n]
        first = pl.program_id(0) == 0
        off = 0
        for i in range(n):
            x = x_refs[i][...]
            xh, r = _rms_hat(x)
            d = dh_ref[:, off:off + widths[i]]
            dxh = d * g_refs[i][...]
            dx = r * (dxh - xh * jnp.mean(dxh * xh, axis=-1, keepdims=True))
            if has_res:
                dx = dx + r_ref[...]
            dx_refs[i][...] = dx
            part = jnp.sum(d * xh, axis=0, keepdims=True)

            @pl.when(first)
            def _(i=i, part=part):
                dg_refs[i][...] = part

            @pl.when(jnp.logical_not(first))
            def _(i=i, part=part):
                dg_refs[i][...] += part

            off += widths[i]

    in_specs = [pl.BlockSpec((tm, sum(widths)), lambda i: (i, 0))]
    in_specs += [pl.BlockSpec((tm, w), lambda i: (i, 0)) for w in widths]
    in_specs += [pl.BlockSpec((1, w), lambda i: (0, 0)) for w in widths]
    operands = [dh, *xs, *gains]
    if has_res:
        in_specs.append(pl.BlockSpec((tm, widths[0]), lambda i: (i, 0)))
        operands.append(res)
    out = pl.pallas_call(
        body, name=name, grid=(S // tm,),
        out_shape=[jax.ShapeDtypeStruct((S, w), F32) for w in widths] + [jax.ShapeDtypeStruct((1, w), F32) for w in widths],
        in_specs=in_specs,
        out_specs=[pl.BlockSpec((tm, w), lambda i: (i, 0)) for w in widths]
        + [pl.BlockSpec((1, w), lambda i: (0, 0)) for w in widths],
        compiler_params=_cparams(("arbitrary",)),
    )(*operands)
    return out[:n], out[n:]


def _sigmoid(g):
    return 1.0 / (1.0 + jnp.exp(-g))


def _ride(result, plan):
    return result if plan is not None else (result, None)


def _residual_then_norm(alpha):
    def epilogue(acc, res, gain):
        y = res + alpha * acc
        return y, _rms_hat(y)[0] * gain
    return epilogue


def _ffn_fwd(x, gain, w, *, tag, ex, first_rider=None, riders=(None, None, None), head=None, h=None,
             next_gain=None):
    if h is None:
        plan = ex.gather(first_rider)
        h, got = _ride(_rms_fwd([x], [gain], name=f"{tag}_norm", comm=plan), plan)
        ex.gathered(first_rider, got, w)
    plan = ex.gather(riders[0])
    g, got = _ride(_mm(h, w[f"{tag}_w_gate"], outs=(BF16,), name=f"{tag}_gate", comm=plan), plan)
    ex.gathered(riders[0], got, w)

    def act(acc, _, g_blk):
        gf = g_blk.astype(F32)
        return acc, gf * _sigmoid(gf) * acc

    plan = ex.gather(riders[1])
    (u, a), got = _ride(_mm(h, w[f"{tag}_w_up"], outs=(BF16, BF16), extras=(g,), epilogue=act, name=f"{tag}_up_act",
                            comm=plan), plan)
    ex.gathered(riders[1], got, w)
    plan = ex.gather(riders[2])
    if head is None and next_gain is None:
        y, got = _ride(_mm(a, w[f"{tag}_w_down"], res=x, alpha=HALF_STEP, name=f"{tag}_down", comm=plan), plan)
    elif head is None:
        y, got = _ride(_mm(a, w[f"{tag}_w_down"], res=x, rows=(next_gain,), outs=(F32, BF16),
                           epilogue=_residual_then_norm(HALF_STEP), name=f"{tag}_down_norm", comm=plan), plan)
    else:
        final_gain, target = head
        y, got = _ride(_mm(a, w[f"{tag}_w_down"], res=x, extras=(target,), rows=(final_gain,), row_sums=2,
                           epilogue=_loss_head_epilogue, name=f"{tag}_down_loss", comm=plan), plan)
    ex.gathered(riders[2], got, w)
    return y, (h, g, u, a)


def _loss_head_epilogue(acc, x_in, target, gain):
    xh, r = _rms_hat(x_in + HALF_STEP * acc)
    err = xh * gain - target
    dy = err * (1.0 / acc.shape[1])
    dxh = dy * gain
    dx = r * (dxh - xh * jnp.mean(dxh * xh, axis=-1, keepdims=True))
    loss = 0.5 * jnp.sum(jnp.mean(err * err, axis=-1, keepdims=True), axis=0, keepdims=True)
    return dx, jnp.sum(dy * xh, axis=0, keepdims=True), jnp.zeros_like(gain) + loss


def _rms_bwd_epilogue(acc, dh_so_far, x, dres, gain):
    dh = acc if dh_so_far is None else acc + dh_so_far
    xh, r = _rms_hat(x)
    dxh = dh * gain
    dx = r * (dxh - xh * jnp.mean(dxh * xh, axis=-1, keepdims=True)) + dres
    return dx, jnp.sum(dh * xh, axis=0, keepdims=True)


def _ffn_bwd(dout, x, gain, w, saved, *, tag, ex, rider=(None, None), spread=False):
    h, g, u, a = saved
    wg, wu, wd = (w[f"{tag}_w_{n}"] for n in ("gate", "up", "down"))

    def act_bwd(acc, _, g_blk, u_blk):
        gf, uf = g_blk.astype(F32), u_blk.astype(F32)
        da = acc * HALF_STEP
        sig = _sigmoid(gf)
        silu = gf * sig
        return da * uf * (sig + silu * (1.0 - sig)), da * silu

    def carrying(group, grad, call):
        group = group if spread else None
        plan = ex.send(group, {group: grad})
        out, got = _ride(call(plan), plan)
        ex.received(group, got)
        return out

    plan = ex.send(*rider)
    (dg, du), got = _ride(_mm(dout, wd, tb=True, outs=(BF16, BF16), extras=(g, u), epilogue=act_bwd,
                              name=f"{tag}_bwd_act", comm=plan), plan)
    ex.received(rider[0], got)
    dwg = _mm(h, dg, ta=True, outs=(GRAD_WIRE,), name=f"{tag}_dwg")
    dwu = carrying(f"{tag}_w_gate", dwg, lambda plan: _mm(h, du, ta=True, outs=(GRAD_WIRE,), name=f"{tag}_dwu", comm=plan))
    dwd = carrying(f"{tag}_w_up", dwu,
                   lambda plan: _mm(a, dout, ta=True, outs=(GRAD_WIRE,), alpha=HALF_STEP, name=f"{tag}_dwd", comm=plan))
    dh = carrying(f"{tag}_w_down", dwd, lambda plan: _mm(dg, wg, tb=True, name=f"{tag}_dh_gate", comm=plan))
    dx, dgain = _mm(du, wu, tb=True, res=dh, extras=(x, dout), rows=(gain,), row_sums=1, epilogue=_rms_bwd_epilogue,
                    name=f"{tag}_dh_up_norm_bwd")
    return dx, dgain, dwg, dwu, dwd


def _rope_tables(S):
    half = HEAD_DIM // 2
    inv_freq = ROPE_THETA ** (-jnp.arange(half, dtype=F32) / half)
    ang = jnp.arange(S, dtype=F32)[:, None] * inv_freq[None, :]
    cos, sin = jnp.cos(ang), jnp.sin(ang)
    reps = LANES // HEAD_DIM
    cos_t = jnp.tile(jnp.concatenate([cos, cos], axis=1), (1, reps))
    sin_t = jnp.tile(jnp.concatenate([-sin, sin], axis=1), (1, reps))
    return cos_t, sin_t


def _rotate(v, cos, sin, sign):
    half = HEAD_DIM // 2
    groups = []
    for g in range(v.shape[1] // LANES):
        t = v[:, g * LANES:(g + 1) * LANES]
        lane = lax.broadcasted_iota(jnp.int32, t.shape, 1)
        swapped = jnp.where(lane % HEAD_DIM < half, pltpu.roll(t, LANES - half, axis=1), pltpu.roll(t, half, axis=1))
        groups.append(t * cos + swapped * (sin * sign))
    return groups[0] if len(groups) == 1 else jnp.concatenate(groups, axis=1)


def _join_d_proj(pieces, rotated, cos_t, sin_t, *, name):
    S = pieces[0].shape[0]
    widths = [p.shape[1] for p in pieces]
    tm = _pick(S, (256, 128))
    n = len(pieces)

    def body(*refs):
        c_ref, s_ref, o_ref = refs[n], refs[n + 1], refs[n + 2]
        off = 0
        for i in range(n):
            v = refs[i][...]
            if i in rotated:
                v = _rotate(v, c_ref[...], s_ref[...], -1.0)
            o_ref[:, off:off + widths[i]] = v.astype(o_ref.dtype)
            off += widths[i]

    return pl.pallas_call(
        body, name=name, grid=(S // tm,),
        out_shape=jax.ShapeDtypeStruct((S, sum(widths)), BF16),
        in_specs=[pl.BlockSpec((tm, w), lambda i: (i, 0)) for w in widths]
        + [pl.BlockSpec((tm, LANES), lambda i: (i, 0))] * 2,
        out_specs=pl.BlockSpec((tm, sum(widths)), lambda i: (i, 0)),
        compiler_params=_cparams(("parallel",)),
    )(*pieces, cos_t, sin_t)


def _head_masks(shape):
    lane = lax.broadcasted_iota(jnp.int32, shape, 1)
    return [(lane >= HEAD_DIM * h) & (lane < HEAD_DIM * (h + 1)) for h in range(shape[1] // HEAD_DIM)]


def _sb_scores(q2, k_j):
    z = lax.dot_general(q2, k_j, (((1,), (1,)), ((), ())), preferred_element_type=F32)
    sign_bit = jnp.int32(-2 ** 31)
    minus_abs = lax.bitcast_convert_type(lax.bitcast_convert_type(z, jnp.int32) | sign_bit, F32)
    softplus = jnp.maximum(z, 0.0) + jnp.log(1.0 + jnp.exp(minus_abs))
    return z - softplus, softplus


def _sb_stack_heads(t, scale=None):
    parts = [jnp.where(hm, t, jnp.zeros_like(t)) for hm in _head_masks(t.shape)]
    t2 = jnp.concatenate(parts, axis=0)
    if scale is not None:
        t2 = (t2.astype(F32) * scale).astype(t2.dtype)
    return t2


def _sb_unstack_heads(t2):
    n = t2.shape[1] // HEAD_DIM
    T = t2.shape[0] // n
    masks = _head_masks((T, t2.shape[1]))
    out = t2[:T]
    for h in range(1, n):
        out = jnp.where(masks[h], t2[h * T:(h + 1) * T], out)
    return out


def _sb_causal(T, n_heads):
    row = lax.broadcasted_iota(jnp.int32, (n_heads * T, T), 0)
    col = lax.broadcasted_iota(jnp.int32, (n_heads * T, T), 1)
    return col < row % T


def _sb_triangle(T, later):
    row = lax.broadcasted_iota(jnp.int32, (T, T), 0)
    col = lax.broadcasted_iota(jnp.int32, (T, T), 1)
    return ((row > col) if later else (row < col)).astype(BF16)


def _sb_fwd(p_sb, *, name, comm=None):
    S = p_sb.shape[0]
    W = p_sb.shape[1] // 3
    LW = min(SB_LANES, W)
    NH = LW // HEAD_DIM
    npair = W // LW
    T = SB_TILE
    n_tiles = S // T
    assert n_tiles <= HEAD_DIM
    scale = HEAD_DIM ** -0.5

    grid = (npair, n_tiles)
    hosted = _Hosted(comm, n_in=5, n_out=2, n_scratch=0)

    def body(*refs):
        q_ref, k_ref, v_ref, causal_ref, later_ref, o_ref, c_ref = hosted.begin(refs, grid)
        I = pl.program_id(1)
        lane = lax.broadcasted_iota(jnp.int32, (T, LW), 1)
        causal = causal_ref[...]
        later_than = later_ref[...]
        q2 = _sb_stack_heads(q_ref[...], scale)

        def scores(J, diag):
            off = pl.multiple_of(J * T, T)
            log_beta, stay = _sb_scores(q2, k_ref[pl.ds(off, T), :])
            if diag:
                stay = stay * causal
            local = jnp.dot(stay.astype(BF16), later_than, preferred_element_type=F32)
            return log_beta, local, jnp.sum(stay, axis=1, keepdims=True), v_ref[pl.ds(off, T), :]

        def weigh(J, sc, gone, acc, carr, diag):
            log_beta, local, _, v_j = sc
            w = jnp.exp((log_beta - gone) - local)
            if diag:
                w = w * causal
            acc = acc + jnp.dot(w.astype(BF16), v_j, preferred_element_type=F32)
            for h in range(NH):
                carr = jnp.where(lane == HEAD_DIM * h + J, -gone[h * T:(h + 1) * T], carr)
            return acc, carr

        def tiles(J, count, state, diag):
            gone, acc, carr, _ = state
            scs = [scores(J - u, diag and u == 0) for u in range(count)]
            for u, sc in enumerate(scs):
                acc, carr = weigh(J - u, sc, gone, acc, carr, diag and u == 0)
                gone = gone + sc[2]
            return gone, acc, carr, jnp.min(gone)

        U = SB_UNROLL
        alive = lambda st: st[3] < SB_DEAD
        state = (jnp.zeros((NH * T, 1), F32), jnp.zeros((NH * T, LW), F32),
                 jnp.full((T, LW), SB_UNSEEN, F32), jnp.zeros((), F32))
        state = lax.cond(I > 0, lambda st: tiles(I, 2, st, True), lambda st: tiles(I, 1, st, True), state)
        rest = jnp.maximum(I - 1, 0)
        singles = jnp.where(rest > 0, (rest - 1) % U + 1, 0)
        _, state = lax.while_loop(lambda c: (c[0] < singles) & alive(c[1]),
                                  lambda c: (c[0] + 1, tiles(I - 2 - c[0], 1, c[1], False)), (jnp.int32(0), state))
        blocks = (rest - singles) // U
        _, state = lax.while_loop(lambda c: (c[0] < blocks) & alive(c[1]),
                                  lambda c: (c[0] + 1, tiles(I - 2 - singles - U * c[0], U, c[1], False)),
                                  (jnp.int32(0), state))
        _, acc, carr, _ = state
        o_ref[...] = _sb_unstack_heads(acc)
        c_ref[...] = carr
        hosted.end(grid)

    blk = lambda I_off: pl.BlockSpec((T, LW), lambda p, I: (I, I_off + p))
    full = lambda off: pl.BlockSpec((S, LW), lambda p, I: (0, off + p))
    const = lambda rows: pl.BlockSpec((rows, T), lambda p, I: (0, 0))
    o, carries, *got = pl.pallas_call(
        body, name=name, grid=grid,
        out_shape=[jax.ShapeDtypeStruct((S, W), F32), jax.ShapeDtypeStruct((S, W), F32)] + hosted.out_shapes,
        in_specs=[blk(0), full(npair), full(2 * npair), const(NH * T), const(T)] + hosted.in_specs,
        out_specs=[blk(0), blk(0)] + hosted.out_specs,
        scratch_shapes=hosted.scratch,
        compiler_params=_cparams(hosted.semantics(("parallel", "arbitrary")), vmem_mb=56),
    )(p_sb, p_sb, p_sb, _sb_causal(T, NH).astype(F32), _sb_triangle(T, True), *hosted.operands)
    return (o, carries) if comm is None else (o, carries, got)


def _sb_bwd(p_sb, do, carries, *, name, comm=None):
    S = p_sb.shape[0]
    W = p_sb.shape[1] // 3
    LW = min(LANES, W)
    NH = LW // HEAD_DIM
    npair = W // LW
    T = SB_TILE
    n_tiles = S // T
    scale = HEAD_DIM ** -0.5

    grid = (npair, n_tiles)
    hosted = _Hosted(comm, n_in=8, n_out=3, n_scratch=0)

    def body(*refs):
        (q_ref, k_ref, v_ref, do_ref, c_ref, causal_ref, later_ref, earlier_ref,
         dq_ref, dk_ref, dv_ref) = hosted.begin(refs, grid)
        I = pl.program_id(1)

        @pl.when(I == 0)
        def _():
            dk_ref[...] = jnp.zeros_like(dk_ref)
            dv_ref[...] = jnp.zeros_like(dv_ref)

        lane = lax.broadcasted_iota(jnp.int32, (T, LW), 1)
        causal = causal_ref[...]
        later_than = later_ref[...]
        earlier_than = earlier_ref[...]
        q2 = _sb_stack_heads(q_ref[...], scale)
        do2 = _sb_stack_heads(do_ref[...].astype(BF16))
        carr = c_ref[...]
        tn_dims = (((0,), (0,)), ((), ()))

        def chain(J, diag):
            off = pl.multiple_of(J * T, T)
            k_j = k_ref[pl.ds(off, T), :]
            v_j = v_ref[pl.ds(off, T), :]
            log_beta, stay = _sb_scores(q2, k_j)
            if diag:
                stay = stay * causal
            lc = jnp.concatenate(
                [jnp.sum(jnp.where(lane == HEAD_DIM * h + J, carr, 0.0), axis=1, keepdims=True) for h in range(NH)],
                axis=0)
            w = jnp.exp((log_beta + lc) - jnp.dot(stay.astype(BF16), later_than, preferred_element_type=F32))
            if diag:
                w = w * causal
            dw = lax.dot_general(do2, v_j, (((1,), (1,)), ((), ())), preferred_element_type=F32)
            e = w * dw
            local = jnp.dot(e.astype(BF16), earlier_than, preferred_element_type=F32)
            return off, k_j, w, e, local, jnp.exp(log_beta), jnp.sum(e, axis=1, keepdims=True)

        def finish(ch, ec, dq_acc, diag):
            off, k_j, w, e, local, beta, _ = ch
            e_before = local + ec
            dz = e - beta * (e + e_before)
            if diag:
                dz = dz * causal
            dzb = dz.astype(BF16)
            dq_acc = dq_acc + jnp.dot(dzb, k_j, preferred_element_type=F32)
            dk_ref[pl.ds(off, T), :] += lax.dot_general(dzb, q2, tn_dims, preferred_element_type=F32)
            dv_ref[pl.ds(off, T), :] += lax.dot_general(w.astype(BF16), do2, tn_dims, preferred_element_type=F32)
            return dq_acc

        def tiles(J, count, state, diag):
            ec, dq_acc = state
            chains = [chain(J + u, diag and u == count - 1) for u in range(count)]
            for u, ch in enumerate(chains):
                dq_acc = finish(ch, ec, dq_acc, diag and u == count - 1)
                ec = ec + ch[6]
            return ec, dq_acc

        lane_row = lax.broadcasted_iota(jnp.int32, (1, LW), 1)
        reached = (jnp.max(carr, axis=0, keepdims=True) > 0.5 * SB_UNSEEN) & (lane_row < HEAD_DIM)
        first = jnp.min(jnp.where(reached, lane_row.astype(F32), float(n_tiles))).astype(jnp.int32)
        U = SB_UNROLL
        count = I - first
        rest = jnp.maximum(count - 1, 0)
        state = (jnp.zeros((NH * T, 1), F32), jnp.zeros((NH * T, LW), F32))
        state = lax.fori_loop(0, rest // U, lambda jj, st: tiles(first + U * jj, U, st, False), state)
        state = lax.fori_loop(0, rest % U, lambda r, st: tiles(I - 1 - rest % U + r, 1, st, False), state)
        _, dq_acc = lax.cond(count > 0, lambda st: tiles(I - 1, 2, st, True), lambda st: tiles(I, 1, st, True), state)
        dq_ref[...] = _sb_unstack_heads(dq_acc) * scale
        hosted.end(grid)

    blk = lambda src_off: pl.BlockSpec((T, LW), lambda p, I: (I, src_off + p))
    full = lambda off: pl.BlockSpec((S, LW), lambda p, I: (0, off + p))
    const = lambda rows: pl.BlockSpec((rows, T), lambda p, I: (0, 0))
    dq, dk, dv, *got = pl.pallas_call(
        body, name=name, grid=grid,
        out_shape=[jax.ShapeDtypeStruct((S, W), F32)] * 3 + hosted.out_shapes,
        in_specs=[blk(0), full(npair), full(2 * npair), blk(0), blk(0), const(NH * T), const(T), const(T)]
        + hosted.in_specs,
        out_specs=[blk(0), full(0), full(0)] + hosted.out_specs,
        scratch_shapes=hosted.scratch,
        compiler_params=_cparams(hosted.semantics(("parallel", "arbitrary")), vmem_mb=56),
    )(p_sb, p_sb, p_sb, do, carries, _sb_causal(T, NH).astype(F32), _sb_triangle(T, True), _sb_triangle(T, False),
      *hosted.operands)
    return (dq, dk, dv) if comm is None else (dq, dk, dv, got)


def _dil_blocks(b, body_fn):
    for pi, (window, dil) in enumerate(DILATED_PATTERNS):
        assert window // dil == DIL_BLOCK
        nblk = DIL_SUPER // (DIL_BLOCK * dil)
        assert (dil * nblk) % DIL_UNROLL == 0

        def group(g, _, pi=pi, dil=dil, nblk=nblk):
            for u in range(DIL_UNROLL):
                t = g * DIL_UNROLL + u
                n = t % nblk
                body_fn(pi, dil, t // nblk, n, b * nblk + n)
            return 0

        lax.fori_loop(0, dil * nblk // DIL_UNROLL, group, 0)


def _dil_rows(start, size, dil):
    if dil == 1:
        return pl.ds(pl.multiple_of(start, DIL_BLOCK), size)
    return pl.ds(start, size, stride=dil)


def _dil_fill_bias(bias_ref):
    row = lax.broadcasted_iota(jnp.int32, (2 * DIL_BLOCK, 2 * DIL_BLOCK), 0)
    kk = lax.broadcasted_iota(jnp.int32, (2 * DIL_BLOCK, 2 * DIL_BLOCK), 1)
    qi = jnp.where(row >= DIL_BLOCK, row - DIL_BLOCK, row)
    for s in range(2):
        dist = s * DIL_BLOCK + qi - kk
        bias_ref[s] = jnp.where((dist >= 0) & (dist <= DIL_BLOCK), 0.0, NEG_BIG)


def _dl_fwd(p_dl, *, name):
    S, W = p_dl.shape[0], p_dl.shape[1] // 3
    npair = W // LANES
    nsuper = S // DIL_SUPER
    assert S % DIL_SUPER == 0 and S // max(d for _, d in DILATED_PATTERNS) >= 2 * DIL_BLOCK
    scale = HEAD_DIM ** -0.5
    npat = len(DILATED_PATTERNS)

    def body(q_ref, k_ref, v_ref, o_ref, l_ref, bias_ref, *pattern_refs):
        op_refs, lp_refs = pattern_refs[:npat], pattern_refs[npat:]
        b = pl.program_id(1)
        masks = _head_masks((DIL_BLOCK, LANES))
        pl.when(b == 0)(lambda: _dil_fill_bias(bias_ref))

        def block(pi, dil, c, n, gn):
            ws = jnp.maximum(gn - 1, 0)
            qrows = n * (DIL_BLOCK * dil) + c
            krows = ws * (DIL_BLOCK * dil) + c
            q_idx = _dil_rows(qrows, DIL_BLOCK, dil)
            k_idx = _dil_rows(krows, 2 * DIL_BLOCK, dil)
            qb = q_ref[q_idx, :]
            kb = k_ref[k_idx, :].astype(BF16)
            vb = v_ref[k_idx, :].astype(BF16)
            q2 = _sb_stack_heads(qb.astype(BF16), scale)
            z = lax.dot_general(q2, kb, (((1,), (1,)), ((), ())), preferred_element_type=F32) + bias_ref[gn - ws]
            m = jnp.max(z, axis=1, keepdims=True)
            p = jnp.exp(z - m)
            den = jnp.sum(p, axis=1, keepdims=True)
            acc = jnp.dot(p.astype(BF16), vb, preferred_element_type=F32)
            lse = m + jnp.log(den)
            op_refs[pi][q_idx, :] = _sb_unstack_heads(acc / den)
            lp_refs[pi][q_idx, :] = jnp.where(masks[0], lse[:DIL_BLOCK], lse[DIL_BLOCK:])

        _dil_blocks(b, block)
        lses = [r[...] for r in lp_refs]
        top = functools.reduce(jnp.maximum, lses)
        ws_ = [jnp.exp(l - top) for l in lses]
        den = functools.reduce(jnp.add, ws_)
        num = functools.reduce(jnp.add, [w * r[...] for r, w in zip(op_refs, ws_)])
        o_ref[...] = num / den
        l_ref[...] = top + jnp.log(den)

    blk = pl.BlockSpec((DIL_SUPER, LANES), lambda p, b: (b, p))
    full = lambda off: pl.BlockSpec((S, LANES), lambda p, b: (0, off + p))
    return pl.pallas_call(
        body, name=name, grid=(npair, nsuper),
        out_shape=[jax.ShapeDtypeStruct((S, W), F32)] * 2,
        in_specs=[blk, full(npair), full(2 * npair)], out_specs=[blk, blk],
        scratch_shapes=[pltpu.VMEM((2, 2 * DIL_BLOCK, 2 * DIL_BLOCK), F32)]
        + [pltpu.VMEM((DIL_SUPER, LANES), F32)] * (2 * npat),
        compiler_params=_cparams(("arbitrary", "arbitrary")),
    )(p_dl, p_dl, p_dl)


def _dl_bwd(p_dl, o, lse, do, *, name):
    S, W = p_dl.shape[0], p_dl.shape[1] // 3
    npair = W // LANES
    nsuper = S // DIL_SUPER
    scale = HEAD_DIM ** -0.5

    def body(q_ref, k_ref, v_ref, o_ref, l_ref, do_ref, dq_ref, dk_ref, dv_ref, delta_ref, bias_ref):
        b = pl.program_id(1)

        @pl.when(b == 0)
        def _():
            dk_ref[...] = jnp.zeros_like(dk_ref)
            dv_ref[...] = jnp.zeros_like(dv_ref)
            _dil_fill_bias(bias_ref)

        dq_ref[...] = jnp.zeros_like(dq_ref)
        prod = do_ref[...] * o_ref[...]
        delta = jnp.zeros_like(prod)
        for hm in _head_masks(prod.shape):
            delta = jnp.where(hm, jnp.sum(jnp.where(hm, prod, 0.0), axis=1, keepdims=True), delta)
        delta_ref[...] = delta

        def block(pi, dil, c, n, gn):
            ws = jnp.maximum(gn - 1, 0)
            qrows = n * (DIL_BLOCK * dil) + c
            krows = ws * (DIL_BLOCK * dil) + c
            q_idx = _dil_rows(qrows, DIL_BLOCK, dil)
            k_idx = _dil_rows(krows, 2 * DIL_BLOCK, dil)
            qb = q_ref[q_idx, :]
            dob = do_ref[q_idx, :]
            lb = l_ref[q_idx, :]
            db = delta_ref[q_idx, :]
            kb = k_ref[k_idx, :].astype(BF16)
            vb = v_ref[k_idx, :].astype(BF16)
            q2 = _sb_stack_heads(qb.astype(BF16), scale)
            do2 = _sb_stack_heads(dob.astype(BF16))
            lse2 = jnp.concatenate([lb[:, HEAD_DIM * h:HEAD_DIM * h + 1] for h in range(2)], axis=0)
            delta2 = jnp.concatenate([db[:, HEAD_DIM * h:HEAD_DIM * h + 1] for h in range(2)], axis=0)
            z = lax.dot_general(q2, kb, (((1,), (1,)), ((), ())), preferred_element_type=F32)
            p = jnp.exp((z + bias_ref[gn - ws]) - lse2)
            dp = lax.dot_general(do2, vb, (((1,), (1,)), ((), ())), preferred_element_type=F32)
            dzb = (p * (dp - delta2)).astype(BF16)
            tn_dims = (((0,), (0,)), ((), ()))
            dq_blk = _sb_unstack_heads(jnp.dot(dzb, kb, preferred_element_type=F32)) * scale
            dk_blk = lax.dot_general(dzb, q2, tn_dims, preferred_element_type=F32)
            dv_blk = lax.dot_general(p.astype(BF16), do2, tn_dims, preferred_element_type=F32)
            dq_ref[q_idx, :] = dq_ref[q_idx, :] + dq_blk
            dk_ref[k_idx, :] = dk_ref[k_idx, :] + dk_blk
            dv_ref[k_idx, :] = dv_ref[k_idx, :] + dv_blk

        _dil_blocks(b, block)

    blk = pl.BlockSpec((DIL_SUPER, LANES), lambda p, b: (b, p))
    full = lambda off: pl.BlockSpec((S, LANES), lambda p, b: (0, off + p))
    return pl.pallas_call(
        body, name=name, grid=(npair, nsuper),
        out_shape=[jax.ShapeDtypeStruct((S, W), F32)] * 3,
        in_specs=[blk, full(npair), full(2 * npair), blk, blk, blk], out_specs=[blk, full(0), full(0)],
        scratch_shapes=[pltpu.VMEM((DIL_SUPER, LANES), F32), pltpu.VMEM((2, 2 * DIL_BLOCK, 2 * DIL_BLOCK), F32)],
        compiler_params=_cparams(("arbitrary", "arbitrary")),
    )(p_dl, p_dl, p_dl, o, lse, do)


class _NoExchange:
    def gather(self, family):
        return None

    def gathered(self, family, got, weights):
        pass

    def send(self, family, grads):
        return None

    def received(self, family, got):
        pass


def _local_step(x, target, gains, weights, exchanges=None):
    S, D = x.shape
    ex = exchanges or _NoExchange()
    weights = dict(weights)
    d_sb = gains["sb_out_norm"].shape[1]
    d_dl = gains["dil_out_norm"].shape[1]
    cos_t, sin_t = _rope_tables(S)

    riders = ("ffn1_w_up", "ffn1_w_down", "mixer") if exchanges else (None, None, None)
    (x1, h2), saved1 = _ffn_fwd(x, gains["ffn1_norm"], weights, tag="ffn1", ex=ex, riders=riders,
                                first_rider="ffn1_w_gate" if exchanges else None,
                                next_gain=gains["mix_norm"])
    w_in = weights["w_in"]
    w_out = weights["w_out"]
    p_sb = _mm(h2, w_in, b_cols=(0, 3 * d_sb), outs=(BF16,), name="proj_sb")

    def rope_qk(acc, _, cos, sin):
        return jnp.concatenate([_rotate(acc[:, :2 * d_dl], cos, sin, 1.0), acc[:, 2 * d_dl:]], axis=1)

    p_dl = _mm(h2, w_in, b_cols=(3 * d_sb, 3 * d_dl), lanes=(cos_t, sin_t), epilogue=rope_qk, name="proj_dl_rope")
    plan = ex.gather("ffn2" if exchanges else None)
    o_sb, carries, *got = _sb_fwd(p_sb, name="sb_fwd", comm=plan)
    ex.gathered("ffn2", got[0] if got else None, weights)
    o_dl, lse_dl = _dl_fwd(p_dl, name="dl_fwd")
    merged = _rms_fwd([o_sb, o_dl], [gains["sb_out_norm"], gains["dil_out_norm"]], name="out_norm")
    x2, h3 = _mm(merged, w_out, res=x1, rows=(gains["ffn2_norm"],), outs=(F32, BF16),
                 epilogue=_residual_then_norm(1.0), name="out_proj_norm")
    (dx3, d_final, loss_wide), saved2 = _ffn_fwd(x2, gains["ffn2_norm"], weights, tag="ffn2", ex=ex, h=h3,
                                                 head=(gains["final_norm"], target))
    loss_row = loss_wide[:, :LANES]

    dx2, d_ffn2_norm, dwg2, dwu2, dwd2 = _ffn_bwd(dx3, x2, gains["ffn2_norm"], weights, saved2, tag="ffn2", ex=ex)
    d_w_out = _mm(merged, dx2, ta=True, outs=(GRAD_WIRE,), name="d_w_out")
    d_merged = _mm(dx2, w_out, tb=True, name="d_merged")
    (do_sb, do_dl), (d_sb_norm, d_dl_norm) = _rms_bwd(
        d_merged, [o_sb, o_dl], [gains["sb_out_norm"], gains["dil_out_norm"]], None, name="out_norm_bwd")
    plan = ex.send("ffn2", dict(ffn2_w_gate=dwg2, ffn2_w_up=dwu2, ffn2_w_down=dwd2))
    dq_sb, dk_sb, dv_sb, *got = _sb_bwd(p_sb, do_sb, carries, name="sb_bwd", comm=plan)
    ex.received("ffn2", got[0] if got else None)
    dq_dl, dk_dl, dv_dl = _dl_bwd(p_dl, o_dl, lse_dl, do_dl, name="dl_bwd")
    d_proj = _join_d_proj([dq_sb, dk_sb, dv_sb, dq_dl, dk_dl, dv_dl], (3, 4), cos_t, sin_t, name="d_proj")
    d_w_in = _mm(h2, d_proj, ta=True, outs=(GRAD_WIRE,), name="d_w_in")
    dx1, d_mix_norm = _mm(d_proj, w_in, tb=True, extras=(x1, dx2), rows=(gains["mix_norm"],), row_sums=1,
                          epilogue=_rms_bwd_epilogue, name="dh_mix_norm_bwd")
    dx, d_ffn1_norm, dwg1, dwu1, dwd1 = _ffn_bwd(
        dx1, x, gains["ffn1_norm"], weights, saved1, tag="ffn1", ex=ex,
        rider=("mixer", dict(w_in=d_w_in, w_out=d_w_out)), spread=True)
    gain_grads = dict(ffn1_norm=d_ffn1_norm, mix_norm=d_mix_norm, sb_out_norm=d_sb_norm, dil_out_norm=d_dl_norm,
                      ffn2_norm=d_ffn2_norm, final_norm=d_final)
    weight_grads = dict(ffn1_w_gate=dwg1, ffn1_w_up=dwu1, ffn1_w_down=dwd1, w_in=d_w_in, w_out=d_w_out,
                        ffn2_w_gate=dwg2, ffn2_w_up=dwu2, ffn2_w_down=dwd2)
    return loss_row, dx, gain_grads, weight_grads


def _mesh_position():
    return lax.axis_index("x"), lax.axis_index("y"), lax.axis_index("c")


def _flip(coord, bit):
    return 1 - coord if bit else coord


RELATIONS = [(rx, ry, rc) for rx in (0, 1) for ry in (0, 1) for rc in (0, 1)][1:]


class _GatherPlan:
    def __init__(self, shards):
        n = len(shards)
        self.operands = list(shards)
        self.out_shapes = [jax.ShapeDtypeStruct((N_DEV,) + s.shape, s.dtype) for s in shards]
        self.scratch = [pltpu.SemaphoreType.DMA((n, 7)), pltpu.SemaphoreType.DMA((n, 7)),
                        pltpu.SemaphoreType.DMA((n,))]

    def _copies(self, in_refs, out_refs, sems):
        send_sems, recv_sems, local_sems = sems
        x, y, c = _mesh_position()
        me, sibling = (x, y, c), (x, y, 1 - c)
        chips = [(1 - x, y), (x, 1 - y), (1 - x, 1 - y)]
        plans = []
        for t, (x_ref, out_ref) in enumerate(zip(in_refs, out_refs)):
            def slot(px, py, pc, out_ref=out_ref):
                return out_ref.at[4 * px + 2 * py + pc]

            def copy(k, block, to, src=None, t=t, slot=slot):
                return pltpu.make_async_remote_copy(
                    src_ref=slot(*block) if src is None else src, dst_ref=slot(*block),
                    send_sem=send_sems.at[t, k], recv_sem=recv_sems.at[t, k],
                    device_id=to, device_id_type=pl.DeviceIdType.MESH)

            plans.append(dict(
                mine=pltpu.make_async_copy(x_ref, slot(*me), local_sems.at[t]),
                first=[copy(0, me, sibling, src=x_ref)]
                + [copy(1 + j, me, (*chip, c), src=x_ref) for j, chip in enumerate(chips)],
                over_ici=[copy(1 + j, (*chip, c), me) for j, chip in enumerate(chips)],
                passed=[copy(4 + j, (*chip, c), sibling) for j, chip in enumerate(chips)],
                from_sibling=[copy(0, sibling, me)] + [copy(4 + j, (*chip, 1 - c), me) for j, chip in enumerate(chips)]))
        return plans

    def start(self, in_refs, out_refs, sems):
        for p in self._copies(in_refs, out_refs, sems):
            p["mine"].start()
            for cp in p["first"]:
                cp.start()

    def finish(self, in_refs, out_refs, sems):
        plans = self._copies(in_refs, out_refs, sems)
        for p in plans:
            for arrived, onward in zip(p["over_ici"], p["passed"]):
                arrived.wait_recv()
                onward.start()
        for p in plans:
            for cp in p["from_sibling"]:
                cp.wait_recv()
            for cp in p["first"] + p["passed"]:
                cp.wait_send()
            p["mine"].wait()


class _Hosted:
    def __init__(self, plan, n_in, n_out, n_scratch):
        self.plan, self.n_in, self.n_out, self.n_scratch = plan, n_in, n_out, n_scratch
        self.operands = list(plan.operands) if plan else []
        self.out_shapes = list(plan.out_shapes) if plan else []
        self.scratch = list(plan.scratch) if plan else []
        self.in_specs = [pl.BlockSpec(memory_space=pl.ANY)] * len(self.operands)
        self.out_specs = [pl.BlockSpec(memory_space=pl.ANY)] * len(self.out_shapes)

    def semantics(self, sem):
        return sem if self.plan is None else ("arbitrary",) * len(sem)

    def _at(self, grid, last):
        hit = None
        for d, n in enumerate(grid):
            here = pl.program_id(d) == (n - 1 if last else 0)
            hit = here if hit is None else hit & here
        return hit

    def begin(self, refs, grid):
        if self.plan is None:
            return refs
        k_in, k_out = len(self.operands), len(self.out_shapes)
        ins, rest = refs[:self.n_in], refs[self.n_in:]
        c_in, rest = rest[:k_in], rest[k_in:]
        outs, rest = rest[:self.n_out], rest[self.n_out:]
        c_out, rest = rest[:k_out], rest[k_out:]
        scratch, sems = rest[:self.n_scratch], rest[self.n_scratch:]
        self._args = (c_in, c_out, sems)
        pl.when(self._at(grid, False))(lambda: self.plan.start(*self._args))
        return tuple(ins) + tuple(outs) + tuple(scratch)

    def end(self, grid):
        if self.plan is not None:
            pl.when(self._at(grid, True))(lambda: self.plan.finish(*self._args))


class _ExchangePlan:
    def __init__(self, packs):
        n = len(packs)
        self.operands = list(packs)
        self.out_shapes = [jax.ShapeDtypeStruct(p.shape, p.dtype) for p in packs]
        self.scratch = [pltpu.SemaphoreType.DMA((n, 7)), pltpu.SemaphoreType.DMA((n, 7)),
                        pltpu.SemaphoreType.DMA((n,))]

    def _copies(self, in_refs, out_refs, sems):
        send_sems, recv_sems, local_sems = sems
        x, y, c = _mesh_position()
        me = 4 * x + 2 * y + c
        copies = [pltpu.make_async_copy(i.at[me], o.at[me], local_sems.at[t])
                  for t, (i, o) in enumerate(zip(in_refs, out_refs))]
        for r, (rx, ry, rc) in enumerate(RELATIONS):
            px, py, pc = _flip(x, rx), _flip(y, ry), _flip(c, rc)
            peer = 4 * px + 2 * py + pc
            copies += [pltpu.make_async_remote_copy(
                src_ref=i.at[peer], dst_ref=o.at[me], send_sem=send_sems.at[t, r], recv_sem=recv_sems.at[t, r],
                device_id=(px, py, pc), device_id_type=pl.DeviceIdType.MESH)
                for t, (i, o) in enumerate(zip(in_refs, out_refs))]
        return copies

    def start(self, in_refs, out_refs, sems):
        for cp in self._copies(in_refs, out_refs, sems):
            cp.start()

    def finish(self, in_refs, out_refs, sems):
        for cp in self._copies(in_refs, out_refs, sems):
            cp.wait()


def _all_reduce_rows(v, *, name):
    R, C = v.shape

    def body(v_ref, out_ref, buf, send_sems, recv_sems):
        x, y, c = _mesh_position()
        me = 4 * x + 2 * y + c
        buf[me] = v_ref[...]
        copies = []
        for r, (rx, ry, rc) in enumerate(RELATIONS):
            cp = pltpu.make_async_remote_copy(
                src_ref=v_ref, dst_ref=buf.at[me], send_sem=send_sems.at[r], recv_sem=recv_sems.at[r],
                device_id=(_flip(x, rx), _flip(y, ry), _flip(c, rc)), device_id_type=pl.DeviceIdType.MESH)
            cp.start()
            copies.append(cp)
        for cp in copies:
            cp.wait()
        total = buf[0]
        for s in range(1, N_DEV):
            total = total + buf[s]
        out_ref[...] = total

    return pl.pallas_call(
        body, name=name,
        out_shape=jax.ShapeDtypeStruct((R, C), F32),
        in_specs=[pl.BlockSpec(memory_space=pltpu.VMEM)],
        out_specs=pl.BlockSpec(memory_space=pltpu.VMEM),
        scratch_shapes=[pltpu.VMEM((N_DEV, R, C), F32), pltpu.SemaphoreType.DMA((7,)), pltpu.SemaphoreType.DMA((7,))],
    )(v)


def _adamw(w, g, m, v, *, name):
    R, C = w.shape
    slots = g.ndim == 3
    tr = _pick(R, (256, 128, 64, 32, 16) if slots else (256, 128, 64, 32, 16, 8))

    def body(w_ref, g_ref, m_ref, v_ref, g_out, d_ref, nm_ref, nv_ref):
        if slots:
            g = g_ref[0].astype(F32)
            for s in range(1, N_DEV):
                g = g + g_ref[s].astype(F32)
        else:
            g = g_ref[...]
        g_out[...] = g
        m_new = ADAM_B1 * m_ref[...] + (1.0 - ADAM_B1) * g
        v_new = ADAM_B2 * v_ref[...] + (1.0 - ADAM_B2) * (g * g)
        m_hat = m_new / (1.0 - ADAM_B1 ** ADAM_STEP)
        v_hat = v_new / (1.0 - ADAM_B2 ** ADAM_STEP)
        d_ref[...] = -ADAM_LR * (m_hat / (jnp.sqrt(v_hat) + ADAM_EPS) + ADAM_WD * w_ref[...])
        nm_ref[...] = m_new
        nv_ref[...] = v_new

    spec = pl.BlockSpec((tr, C), lambda i: (i, 0))
    g_spec = pl.BlockSpec((N_DEV, tr, C), lambda i: (0, i, 0)) if slots else spec
    return pl.pallas_call(
        body, name=name, grid=(R // tr,),
        out_shape=[jax.ShapeDtypeStruct((R, C), F32)] * 4,
        in_specs=[spec, g_spec, spec, spec], out_specs=[spec] * 4,
        compiler_params=_cparams(("parallel",)),
    )(w, g, m, v)


WEIGHT_NAMES = ["ffn1_norm", "ffn1_w_gate", "ffn1_w_up", "ffn1_w_down", "mix_norm", "w_in", "sb_out_norm",
                "dil_out_norm", "w_out", "ffn2_norm", "ffn2_w_gate", "ffn2_w_up", "ffn2_w_down", "final_norm"]
GAIN_NAMES = ["ffn1_norm", "mix_norm", "sb_out_norm", "dil_out_norm", "ffn2_norm", "final_norm"]
COL_SHARDED = ["ffn1_w_gate", "ffn1_w_up", "ffn2_w_gate", "ffn2_w_up", "w_in"]
ROW_SHARDED = ["ffn1_w_down", "ffn2_w_down", "w_out"]
GROUPS = {"mixer": (["w_in"], ["w_out"]),
          "ffn2": (["ffn2_w_gate", "ffn2_w_up"], ["ffn2_w_down"])}
for _ffn in ("ffn1", "ffn2"):
    GROUPS.update({f"{_ffn}_w_gate": ([f"{_ffn}_w_gate"], []), f"{_ffn}_w_up": ([f"{_ffn}_w_up"], []),
                   f"{_ffn}_w_down": ([], [f"{_ffn}_w_down"])})


class _Exchanges:
    def __init__(self, params):
        self.params = params
        self.grads = {}

    def gather(self, group):
        if group is None:
            return None
        cols, rows = GROUPS[group]
        return _GatherPlan([self.params[n].astype(BF16) for n in cols + rows])

    def gathered(self, group, got, weights):
        if group is None:
            return
        cols, rows = GROUPS[group]
        for n, blocks in zip(cols + rows, got):
            if n in cols:
                weights[n] = jnp.transpose(blocks, (1, 0, 2)).reshape(blocks.shape[1], N_DEV * blocks.shape[2])
            else:
                weights[n] = blocks.reshape(N_DEV * blocks.shape[1], blocks.shape[2])

    def send(self, group, grads):
        if group is None:
            return None
        cols, rows = GROUPS[group]
        packs = [jnp.transpose(grads[n].reshape(grads[n].shape[0], N_DEV, self.params[n].shape[1]), (1, 0, 2))
                 for n in cols]
        packs += [grads[n].reshape(N_DEV, self.params[n].shape[0], grads[n].shape[1]) for n in rows]
        return _ExchangePlan([p.astype(GRAD_WIRE) for p in packs])

    def received(self, group, got):
        if group is None:
            return
        cols, rows = GROUPS[group]
        for n, slots in zip(cols + rows, got):
            self.grads[n] = slots


def _step(x, target, params, moments_m, moments_v):
    ex = _Exchanges(params)
    weights = {}
    gains = {n: params[n] for n in GAIN_NAMES}
    loss_row, grad_x, gain_grads, _ = _local_step(x, target, gains, weights, ex)
    grads = ex.grads

    rows = [gain_grads[n].reshape(-1, LANES) for n in GAIN_NAMES] + [loss_row]
    small = jnp.concatenate(rows, axis=0)
    pad = (-small.shape[0]) % 8
    small = jnp.pad(small, ((0, pad), (0, 0)))
    small = _all_reduce_rows(small, name="reduce_gains_loss")
    off = 0
    for n in GAIN_NAMES:
        r = gain_grads[n].shape[1] // LANES
        grads[n] = small[off:off + r].reshape(1, -1)
        off += r
    loss = small[off, 0]

    delta, new_m, new_v = {}, {}, {}
    for n in WEIGHT_NAMES:
        grads[n], delta[n], new_m[n], new_v[n] = _adamw(params[n], grads[n], moments_m[n], moments_v[n],
                                                        name=f"adamw_{n}")
    return loss, grad_x, grads, delta, new_m, new_v


def kernel(x, ffn1_norm, ffn1_w_gate, ffn1_w_up, ffn1_w_down, mix_norm, w_in, sb_out_norm, dil_out_norm, w_out, ffn2_norm, ffn2_w_gate, ffn2_w_up, ffn2_w_down, final_norm, loss_target, m_ffn1_norm, m_ffn1_w_gate, m_ffn1_w_up, m_ffn1_w_down, m_mix_norm, m_w_in, m_sb_out_norm, m_dil_out_norm, m_w_out, m_ffn2_norm, m_ffn2_w_gate, m_ffn2_w_up, m_ffn2_w_down, m_final_norm, v_ffn1_norm, v_ffn1_w_gate, v_ffn1_w_up, v_ffn1_w_down, v_mix_norm, v_w_in, v_sb_out_norm, v_dil_out_norm, v_w_out, v_ffn2_norm, v_ffn2_w_gate, v_ffn2_w_up, v_ffn2_w_down, v_final_norm):
    given = dict(locals())
    shapes = {n: given[n].shape for n in WEIGHT_NAMES}

    def as2d(a):
        return a.reshape(1, -1) if a.ndim == 1 else a.reshape(a.shape[-2], a.shape[-1])

    params = {n: as2d(given[n]) for n in WEIGHT_NAMES}
    moments_m = {n: as2d(given["m_" + n]) for n in WEIGHT_NAMES}
    moments_v = {n: as2d(given["v_" + n]) for n in WEIGHT_NAMES}
    loss, grad_x, grads, delta, new_m, new_v = _step(x[0], loss_target[0], params, moments_m, moments_v)
    back = lambda d: [d[n].reshape(shapes[n]) for n in WEIGHT_NAMES]
    return (loss, grad_x[None], *back(grads), *back(delta), *back(new_m), *back(new_v))
```

```python
import functools

import jax
import jax.numpy as jnp
from jax import lax
from jax.experimental import pallas as pl
from jax.experimental.pallas import tpu as pltpu

F32 = jnp.float32
BF16 = jnp.bfloat16
GRAD_WIRE = jnp.bfloat16

N_DEV = 8
HEAD_DIM = 64
LANES = 128
DILATED_PATTERNS = ((128, 1), (512, 4), (2048, 16))
DIL_BLOCK = 128
DIL_SUPER = 2048
DIL_UNROLL = 16
SB_TILE = 256
SB_LANES = 128
SB_UNROLL = 4
SB_DEAD = 90.0
SB_UNSEEN = -1e30
ROPE_THETA = 10000.0
RMS_EPS = 1e-6
HALF_STEP = 0.5
ADAM_LR = 0.001
ADAM_B1 = 0.9
ADAM_B2 = 0.999
ADAM_EPS = 1e-08
ADAM_WD = 0.01
ADAM_STEP = 10
NEG_BIG = -1e30
VMEM_CAP_MB = 60


def _pick(n, prefs):
    for p in prefs:
        if n % p == 0:
            return p
    return n


MM_MAX_TILE = 1536
MM_WHOLE = 3072


def _largest_tile(n, cap):
    if n <= cap:
        return n
    for t in range(cap - cap % LANES, 0, -LANES):
        if n % t == 0:
            return t
    return n


def _cparams(sem=None, vmem_mb=48):
    return pltpu.CompilerParams(dimension_semantics=sem, vmem_limit_bytes=min(vmem_mb, VMEM_CAP_MB) * 1024 * 1024)


def _nbytes(shape, dtype):
    n = 1
    for s in shape:
        n *= s
    return n * jnp.dtype(dtype).itemsize


def _mm(a, b, *, name, ta=False, tb=False, outs=(F32,), res=None, alpha=1.0, extras=(), epilogue=None,
        tm=None, tn=None, tk=None, comm=None, rows=(), lanes=(), row_sums=0, b_cols=None):
    if ta:
        K, M = a.shape
    else:
        M, K = a.shape
    if tb:
        N, Kb = b.shape
    else:
        Kb, N = b.shape
    col0 = 0
    if b_cols is not None:
        col0, N = b_cols
    assert K == Kb, (a.shape, b.shape, ta, tb)
    tn = tn or (N if (not ta and K <= MM_WHOLE and N <= MM_WHOLE) else _largest_tile(N, MM_MAX_TILE))
    tm = tm or (_largest_tile(M, MM_MAX_TILE) if ta else _pick(M, (512, 256, 128) if tn <= MM_MAX_TILE else (256, 128)))
    tk = tk or (K if K <= MM_WHOLE else _pick(K, (2048, 1024, 512, 256, 128)))
    nk = K // tk
    a_spec = pl.BlockSpec((tk, tm), lambda i, j, k: (k, i)) if ta else pl.BlockSpec((tm, tk), lambda i, j, k: (i, k))
    assert col0 % tn == 0
    b_spec = (pl.BlockSpec((tn, tk), lambda i, j, k: (j + col0 // tn, k)) if tb
              else pl.BlockSpec((tk, tn), lambda i, j, k: (k, j + col0 // tn)))
    mn_spec = pl.BlockSpec((tm, tn), lambda i, j, k: (i, j))
    dims = (((0 if ta else 1,), (1 if tb else 0,)), ((), ()))
    row_spec = pl.BlockSpec((1, tn), lambda i, j, k: (0, j))
    lane_spec = pl.BlockSpec((tm, LANES), lambda i, j, k: (i, 0))
    n_extra = len(extras) + (1 if res is not None else 0) + len(rows) + len(lanes)
    n_mn = len(outs)
    n_out = n_mn + row_sums
    assert row_sums == 0 or tn == N
    grid = (M // tm, N // tn, nk)
    hosted = _Hosted(comm, n_in=2 + n_extra, n_out=n_out, n_scratch=1 if nk > 1 else 0)

    def body(*refs):
        a_ref, b_ref = refs[0], refs[1]
        in_refs = refs[2:2 + n_extra]
        refs = hosted.begin(refs, grid)
        out_refs = refs[2 + n_extra:2 + n_extra + n_out]
        prod = lax.dot_general(a_ref[...].astype(BF16), b_ref[...].astype(BF16), dims, preferred_element_type=F32)

        def finish(acc):
            blocks = [r[...] for r in in_refs]
            if res is not None:
                r_blk, blocks = blocks[0], blocks[1:]
            else:
                r_blk = None
            if epilogue is None:
                val = acc * alpha
                if r_blk is not None:
                    val = val + r_blk
                vals = (val,)
            else:
                vals = epilogue(acc, r_blk, *blocks)
                vals = vals if isinstance(vals, (tuple, list)) else (vals,)
            for o_ref, v in zip(out_refs[:n_mn], vals[:n_mn]):
                o_ref[...] = v.astype(o_ref.dtype)
            first_rows = pl.program_id(0) == 0
            for o_ref, part in zip(out_refs[n_mn:], vals[n_mn:]):
                @pl.when(first_rows)
                def _(o_ref=o_ref, part=part):
                    o_ref[...] = part

                @pl.when(jnp.logical_not(first_rows))
                def _(o_ref=o_ref, part=part):
                    o_ref[...] += part

        if nk == 1:
            finish(prod)
        else:
            acc_ref = refs[2 + n_extra + n_out]
            k = pl.program_id(2)

            @pl.when(k == 0)
            def _():
                acc_ref[...] = prod

            @pl.when(k > 0)
            def _():
                acc_ref[...] += prod

            @pl.when(k == nk - 1)
            def _():
                finish(acc_ref[...])

        hosted.end(grid)

    mn_operands = ([res] if res is not None else []) + list(extras)
    operands = [a, b] + mn_operands + list(rows) + list(lanes)
    in_specs = [a_spec, b_spec] + [mn_spec] * len(mn_operands) + [row_spec] * len(rows) + [lane_spec] * len(lanes)
    est = 2 * (_nbytes((tm, tk), a.dtype) + _nbytes((tk, tn), b.dtype))
    est += 2 * sum(_nbytes((tm, tn), o.dtype) for o in mn_operands)
    est += 2 * sum(_nbytes((tm, tn), d) for d in outs) + 2 * _nbytes((tm, tn), F32)
    semantics = ("parallel", "parallel", "arbitrary") if row_sums == 0 else ("arbitrary",) * 3
    result = pl.pallas_call(
        body, name=name, grid=grid,
        out_shape=[jax.ShapeDtypeStruct((M, N), d) for d in outs]
        + [jax.ShapeDtypeStruct((1, N), F32)] * row_sums + hosted.out_shapes,
        in_specs=in_specs + hosted.in_specs,
        out_specs=[mn_spec] * n_mn + [row_spec] * row_sums + hosted.out_specs,
        scratch_shapes=([pltpu.VMEM((tm, tn), F32)] if nk > 1 else []) + hosted.scratch,
        compiler_params=_cparams(hosted.semantics(semantics), vmem_mb=max(32, 2 * est // (1024 * 1024))),
    )(*operands, *hosted.operands)
    own, got = result[:n_out], list(result[n_out:])
    own = own[0] if n_out == 1 else own
    return own if comm is None else (own, got)


def _rms_hat(x):
    r = lax.rsqrt(jnp.mean(x * x, axis=-1, keepdims=True) + RMS_EPS)
    return x * r, r


def _rms_fwd(xs, gains, *, name, comm=None):
    S = xs[0].shape[0]
    widths = [x.shape[1] for x in xs]
    tm = _pick(S, (512, 256, 128))
    n = len(xs)
    grid = (S // tm,)
    hosted = _Hosted(comm, n_in=2 * n, n_out=1, n_scratch=0)

    def body(*refs):
        refs = hosted.begin(refs, grid)
        o_ref = refs[2 * n]
        off = 0
        for i in range(n):
            xh, _ = _rms_hat(refs[i][...])
            o_ref[:, off:off + widths[i]] = (xh * refs[n + i][...]).astype(o_ref.dtype)
            off += widths[i]
        hosted.end(grid)

    out, *got = pl.pallas_call(
        body, name=name, grid=grid,
        out_shape=[jax.ShapeDtypeStruct((S, sum(widths)), BF16)] + hosted.out_shapes,
        in_specs=[pl.BlockSpec((tm, w), lambda i: (i, 0)) for w in widths]
        + [pl.BlockSpec((1, w), lambda i: (0, 0)) for w in widths] + hosted.in_specs,
        out_specs=[pl.BlockSpec((tm, sum(widths)), lambda i: (i, 0))] + hosted.out_specs,
        scratch_shapes=hosted.scratch,
        compiler_params=_cparams(hosted.semantics(("parallel",))),
    )(*xs, *gains, *hosted.operands)
    return out if comm is None else (out, got)


def _sigmoid(g):
    return 1.0 / (1.0 + jnp.exp(-g))


def _ride(result, plan):
    return result if plan is not None else (result, None)


def _residual_then_norm(alpha):
    def epilogue(acc, res, gain):
        y = res + alpha * acc
        return y, _rms_hat(y)[0] * gain
    return epilogue


def _ffn_fwd(x, gain, w, *, tag, ex, first_rider=None, riders=(None, None, None), head=None, h=None,
             next_gain=None):
    if h is None:
        plan = ex.gather(first_rider)
        h, got = _ride(_rms_fwd([x], [gain], name=f"{tag}_norm", comm=plan), plan)
        ex.gathered(first_rider, got, w)
    plan = ex.gather(riders[0])
    g, got = _ride(_mm(h, w[f"{tag}_w_gate"], outs=(BF16,), name=f"{tag}_gate", comm=plan), plan)
    ex.gathered(riders[0], got, w)

    def act(acc, _, g_blk):
        gf = g_blk.astype(F32)
        return acc, gf * _sigmoid(gf) * acc

    plan = ex.gather(riders[1])
    (u, a), got = _ride(_mm(h, w[f"{tag}_w_up"], outs=(BF16, BF16), extras=(g,), epilogue=act, name=f"{tag}_up_act",
                            comm=plan), plan)
    ex.gathered(riders[1], got, w)
    plan = ex.gather(riders[2])
    if head is None and next_gain is None:
        y, got = _ride(_mm(a, w[f"{tag}_w_down"], res=x, alpha=HALF_STEP, name=f"{tag}_down", comm=plan), plan)
    elif head is None:
        y, got = _ride(_mm(a, w[f"{tag}_w_down"], res=x, rows=(next_gain,), outs=(F32, BF16),
                           epilogue=_residual_then_norm(HALF_STEP), name=f"{tag}_down_norm", comm=plan), plan)
    else:
        final_gain, target = head
        y, got = _ride(_mm(a, w[f"{tag}_w_down"], res=x, extras=(target,), rows=(final_gain,), row_sums=2,
                           epilogue=_loss_head_epilogue, name=f"{tag}_down_loss", comm=plan), plan)
    ex.gathered(riders[2], got, w)
    return y, (h, g, u, a)


def _loss_head_epilogue(acc, x_in, target, gain):
    xh, r = _rms_hat(x_in + HALF_STEP * acc)
    err = xh * gain - target
    dy = err * (1.0 / acc.shape[1])
    dxh = dy * gain
    dx = r * (dxh - xh * jnp.mean(dxh * xh, axis=-1, keepdims=True))
    loss = 0.5 * jnp.sum(jnp.mean(err * err, axis=-1, keepdims=True), axis=0, keepdims=True)
    return dx, jnp.sum(dy * xh, axis=0, keepdims=True), jnp.zeros_like(gain) + loss


def _rms_bwd_epilogue(acc, dh_so_far, x, *dres_and_gain):
    gain = dres_and_gain[-1]
    dh = acc if dh_so_far is None else acc + dh_so_far
    xh, r = _rms_hat(x)
    dxh = dh * gain
    dx = r * (dxh - xh * jnp.mean(dxh * xh, axis=-1, keepdims=True))
    if len(dres_and_gain) == 2:
        dx = dx + dres_and_gain[0]
    return dx, jnp.sum(dh * xh, axis=0, keepdims=True)


def _ffn_bwd(dout, x, gain, w, saved, *, tag, ex, rider=(None, None), spread=False):
    h, g, u, a = saved
    wg, wu, wd = (w[f"{tag}_w_{n}"] for n in ("gate", "up", "down"))

    def act_bwd(acc, _, g_blk, u_blk):
        gf, uf = g_blk.astype(F32), u_blk.astype(F32)
        da = acc * HALF_STEP
        sig = _sigmoid(gf)
        silu = gf * sig
        return da * uf * (sig + silu * (1.0 - sig)), da * silu

    def carrying(group, grad, call):
        group = group if spread else None
        plan = ex.send(group, {group: grad})
        out, got = _ride(call(plan), plan)
        ex.received(group, got)
        return out

    plan = ex.send(*rider)
    (dg, du), got = _ride(_mm(dout, wd, tb=True, outs=(BF16, BF16), extras=(g, u), epilogue=act_bwd,
                              name=f"{tag}_bwd_act", comm=plan), plan)
    ex.received(rider[0], got)
    dwg = _mm(h, dg, ta=True, outs=(GRAD_WIRE,), name=f"{tag}_dwg")
    dwu = carrying(f"{tag}_w_gate", dwg, lambda plan: _mm(h, du, ta=True, outs=(GRAD_WIRE,), name=f"{tag}_dwu", comm=plan))
    dwd = carrying(f"{tag}_w_up", dwu,
                   lambda plan: _mm(a, dout, ta=True, outs=(GRAD_WIRE,), alpha=HALF_STEP, name=f"{tag}_dwd", comm=plan))
    dh = carrying(f"{tag}_w_down", dwd, lambda plan: _mm(dg, wg, tb=True, name=f"{tag}_dh_gate", comm=plan))
    dx, dgain = _mm(du, wu, tb=True, res=dh, extras=(x, dout), rows=(gain,), row_sums=1, epilogue=_rms_bwd_epilogue,
                    name=f"{tag}_dh_up_norm_bwd")
    return dx, dgain, dwg, dwu, dwd


def _rope_tables(S):
    half = HEAD_DIM // 2
    inv_freq = ROPE_THETA ** (-jnp.arange(half, dtype=F32) / half)
    ang = jnp.arange(S, dtype=F32)[:, None] * inv_freq[None, :]
    cos, sin = jnp.cos(ang), jnp.sin(ang)
    reps = LANES // HEAD_DIM
    cos_t = jnp.tile(jnp.concatenate([cos, cos], axis=1), (1, reps))
    sin_t = jnp.tile(jnp.concatenate([-sin, sin], axis=1), (1, reps))
    return cos_t, sin_t


def _rotate(v, cos, sin, sign):
    half = HEAD_DIM // 2
    groups = []
    for g in range(v.shape[1] // LANES):
        t = v[:, g * LANES:(g + 1) * LANES]
        lane = lax.broadcasted_iota(jnp.int32, t.shape, 1)
        swapped = jnp.where(lane % HEAD_DIM < half, pltpu.roll(t, LANES - half, axis=1), pltpu.roll(t, half, axis=1))
        groups.append(t * cos + swapped * (sin * sign))
    return groups[0] if len(groups) == 1 else jnp.concatenate(groups, axis=1)


def _join_d_proj(pieces, rotated, cos_t, sin_t, *, name):
    S = pieces[0].shape[0]
    widths = [p.shape[1] for p in pieces]
    tm = _pick(S, (256, 128))
    n = len(pieces)

    def body(*refs):
        c_ref, s_ref, o_ref = refs[n], refs[n + 1], refs[n + 2]
        off = 0
        for i in range(n):
            v = refs[i][...]
            if i in rotated:
                v = _rotate(v, c_ref[...], s_ref[...], -1.0)
            o_ref[:, off:off + widths[i]] = v.astype(o_ref.dtype)
            off += widths[i]

    return pl.pallas_call(
        body, name=name, grid=(S // tm,),
        out_shape=jax.ShapeDtypeStruct((S, sum(widths)), BF16),
        in_specs=[pl.BlockSpec((tm, w), lambda i: (i, 0)) for w in widths]
        + [pl.BlockSpec((tm, LANES), lambda i: (i, 0))] * 2,
        out_specs=pl.BlockSpec((tm, sum(widths)), lambda i: (i, 0)),
        compiler_params=_cparams(("parallel",)),
    )(*pieces, cos_t, sin_t)


def _head_masks(shape):
    lane = lax.broadcasted_iota(jnp.int32, shape, 1)
    return [(lane >= HEAD_DIM * h) & (lane < HEAD_DIM * (h + 1)) for h in range(shape[1] // HEAD_DIM)]


def _sb_scores(q2, k_j):
    z = lax.dot_general(q2, k_j, (((1,), (1,)), ((), ())), preferred_element_type=F32)
    sign_bit = jnp.int32(-2 ** 31)
    minus_abs = lax.bitcast_convert_type(lax.bitcast_convert_type(z, jnp.int32) | sign_bit, F32)
    softplus = jnp.maximum(z, 0.0) + jnp.log(1.0 + jnp.exp(minus_abs))
    return z - softplus, softplus


def _sb_stack_heads(t, scale=None):
    parts = [jnp.where(hm, t, jnp.zeros_like(t)) for hm in _head_masks(t.shape)]
    t2 = jnp.concatenate(parts, axis=0)
    if scale is not None:
        t2 = (t2.astype(F32) * scale).astype(t2.dtype)
    return t2


def _sb_unstack_heads(t2):
    n = t2.shape[1] // HEAD_DIM
    T = t2.shape[0] // n
    masks = _head_masks((T, t2.shape[1]))
    out = t2[:T]
    for h in range(1, n):
        out = jnp.where(masks[h], t2[h * T:(h + 1) * T], out)
    return out


def _sb_causal(T, n_heads):
    row = lax.broadcasted_iota(jnp.int32, (n_heads * T, T), 0)
    col = lax.broadcasted_iota(jnp.int32, (n_heads * T, T), 1)
    return col < row % T


def _sb_triangle(T, later):
    row = lax.broadcasted_iota(jnp.int32, (T, T), 0)
    col = lax.broadcasted_iota(jnp.int32, (T, T), 1)
    return ((row > col) if later else (row < col)).astype(BF16)


def _sb_fwd(p_sb, *, name, comm=None):
    S = p_sb.shape[0]
    W = p_sb.shape[1] // 3
    LW = min(SB_LANES, W)
    NH = LW // HEAD_DIM
    npair = W // LW
    T = SB_TILE
    n_tiles = S // T
    assert n_tiles <= HEAD_DIM
    scale = HEAD_DIM ** -0.5

    grid = (npair, n_tiles)
    hosted = _Hosted(comm, n_in=5, n_out=2, n_scratch=0)

    def body(*refs):
        q_ref, k_ref, v_ref, causal_ref, later_ref, o_ref, c_ref = hosted.begin(refs, grid)
        I = pl.program_id(1)
        lane = lax.broadcasted_iota(jnp.int32, (T, LW), 1)
        causal = causal_ref[...]
        later_than = later_ref[...]
        q2 = _sb_stack_heads(q_ref[...], scale)

        def scores(J, diag):
            off = pl.multiple_of(J * T, T)
            log_beta, stay = _sb_scores(q2, k_ref[pl.ds(off, T), :])
            if diag:
                stay = stay * causal
            local = jnp.dot(stay.astype(BF16), later_than, preferred_element_type=F32)
            return log_beta, local, jnp.sum(stay, axis=1, keepdims=True), v_ref[pl.ds(off, T), :]

        def weigh(J, sc, gone, acc, carr, diag):
            log_beta, local, _, v_j = sc
            w = jnp.exp((log_beta - gone) - local)
            if diag:
                w = w * causal
            acc = acc + jnp.dot(w.astype(BF16), v_j, preferred_element_type=F32)
            for h in range(NH):
                carr = jnp.where(lane == HEAD_DIM * h + J, -gone[h * T:(h + 1) * T], carr)
            return acc, carr

        def tiles(J, count, state, diag):
            gone, acc, carr, _ = state
            scs = [scores(J - u, diag and u == 0) for u in range(count)]
            for u, sc in enumerate(scs):
                acc, carr = weigh(J - u, sc, gone, acc, carr, diag and u == 0)
                gone = gone + sc[2]
            return gone, acc, carr, jnp.min(gone)

        U = SB_UNROLL
        alive = lambda st: st[3] < SB_DEAD
        state = (jnp.zeros((NH * T, 1), F32), jnp.zeros((NH * T, LW), F32),
                 jnp.full((T, LW), SB_UNSEEN, F32), jnp.zeros((), F32))
        state = lax.cond(I > 0, lambda st: tiles(I, 2, st, True), lambda st: tiles(I, 1, st, True), state)
        rest = jnp.maximum(I - 1, 0)
        singles = jnp.where(rest > 0, (rest - 1) % U + 1, 0)
        _, state = lax.while_loop(lambda c: (c[0] < singles) & alive(c[1]),
                                  lambda c: (c[0] + 1, tiles(I - 2 - c[0], 1, c[1], False)), (jnp.int32(0), state))
        blocks = (rest - singles) // U
        _, state = lax.while_loop(lambda c: (c[0] < blocks) & alive(c[1]),
                                  lambda c: (c[0] + 1, tiles(I - 2 - singles - U * c[0], U, c[1], False)),
                                  (jnp.int32(0), state))
        _, acc, carr, _ = state
        o_ref[...] = _sb_unstack_heads(acc)
        c_ref[...] = carr
        hosted.end(grid)

    blk = lambda I_off: pl.BlockSpec((T, LW), lambda p, I: (I, I_off + p))
    full = lambda off: pl.BlockSpec((S, LW), lambda p, I: (0, off + p))
    const = lambda rows: pl.BlockSpec((rows, T), lambda p, I: (0, 0))
    o, carries, *got = pl.pallas_call(
        body, name=name, grid=grid,
        out_shape=[jax.ShapeDtypeStruct((S, W), F32), jax.ShapeDtypeStruct((S, W), F32)] + hosted.out_shapes,
        in_specs=[blk(0), full(npair), full(2 * npair), const(NH * T), const(T)] + hosted.in_specs,
        out_specs=[blk(0), blk(0)] + hosted.out_specs,
        scratch_shapes=hosted.scratch,
        compiler_params=_cparams(hosted.semantics(("parallel", "arbitrary")), vmem_mb=56),
    )(p_sb, p_sb, p_sb, _sb_causal(T, NH).astype(F32), _sb_triangle(T, True), *hosted.operands)
    return (o, carries) if comm is None else (o, carries, got)


def _sb_bwd(p_sb, do, carries, *, name, comm=None):
    S = p_sb.shape[0]
    W = p_sb.shape[1] // 3
    LW = min(LANES, W)
    NH = LW // HEAD_DIM
    npair = W // LW
    T = SB_TILE
    n_tiles = S // T
    scale = HEAD_DIM ** -0.5

    grid = (npair, n_tiles)
    hosted = _Hosted(comm, n_in=8, n_out=3, n_scratch=0)

    def body(*refs):
        (q_ref, k_ref, v_ref, do_ref, c_ref, causal_ref, later_ref, earlier_ref,
         dq_ref, dk_ref, dv_ref) = hosted.begin(refs, grid)
        I = pl.program_id(1)

        @pl.when(I == 0)
        def _():
            dk_ref[...] = jnp.zeros_like(dk_ref)
            dv_ref[...] = jnp.zeros_like(dv_ref)

        lane = lax.broadcasted_iota(jnp.int32, (T, LW), 1)
        causal = causal_ref[...]
        later_than = later_ref[...]
        earlier_than = earlier_ref[...]
        q2 = _sb_stack_heads(q_ref[...], scale)
        do2 = _sb_stack_heads(do_ref[...].astype(BF16))
        carr = c_ref[...]
        tn_dims = (((0,), (0,)), ((), ()))

        def chain(J, diag):
            off = pl.multiple_of(J * T, T)
            k_j = k_ref[pl.ds(off, T), :]
            v_j = v_ref[pl.ds(off, T), :]
            log_beta, stay = _sb_scores(q2, k_j)
            if diag:
                stay = stay * causal
            lc = jnp.concatenate(
                [jnp.sum(jnp.where(lane == HEAD_DIM * h + J, carr, 0.0), axis=1, keepdims=True) for h in range(NH)],
                axis=0)
            w = jnp.exp((log_beta + lc) - jnp.dot(stay.astype(BF16), later_than, preferred_element_type=F32))
            if diag:
                w = w * causal
            dw = lax.dot_general(do2, v_j, (((1,), (1,)), ((), ())), preferred_element_type=F32)
            e = w * dw
            local = jnp.dot(e.astype(BF16), earlier_than, preferred_element_type=F32)
            return off, k_j, w, e, local, jnp.exp(log_beta), jnp.sum(e, axis=1, keepdims=True)

        def finish(ch, ec, dq_acc, diag):
            off, k_j, w, e, local, beta, _ = ch
            e_before = local + ec
            dz = e - beta * (e + e_before)
            if diag:
                dz = dz * causal
            dzb = dz.astype(BF16)
            dq_acc = dq_acc + jnp.dot(dzb, k_j, preferred_element_type=F32)
            dk_ref[pl.ds(off, T), :] += lax.dot_general(dzb, q2, tn_dims, preferred_element_type=F32)
            dv_ref[pl.ds(off, T), :] += lax.dot_general(w.astype(BF16), do2, tn_dims, preferred_element_type=F32)
            return dq_acc

        def tiles(J, count, state, diag):
            ec, dq_acc = state
            chains = [chain(J + u, diag and u == count - 1) for u in range(count)]
            for u, ch in enumerate(chains):
                dq_acc = finish(ch, ec, dq_acc, diag and u == count - 1)
                ec = ec + ch[6]
            return ec, dq_acc

        lane_row = lax.broadcasted_iota(jnp.int32, (1, LW), 1)
        reached = (jnp.max(carr, axis=0, keepdims=True) > 0.5 * SB_UNSEEN) & (lane_row < HEAD_DIM)
        first = jnp.min(jnp.where(reached, lane_row.astype(F32), float(n_tiles))).astype(jnp.int32)
        U = SB_UNROLL
        count = I - first
        rest = jnp.maximum(count - 1, 0)
        state = (jnp.zeros((NH * T, 1), F32), jnp.zeros((NH * T, LW), F32))
        state = lax.fori_loop(0, rest // U, lambda jj, st: tiles(first + U * jj, U, st, False), state)
        state = lax.fori_loop(0, rest % U, lambda r, st: tiles(I - 1 - rest % U + r, 1, st, False), state)
        _, dq_acc = lax.cond(count > 0, lambda st: tiles(I - 1, 2, st, True), lambda st: tiles(I, 1, st, True), state)
        dq_ref[...] = _sb_unstack_heads(dq_acc) * scale
        hosted.end(grid)

    blk = lambda src_off: pl.BlockSpec((T, LW), lambda p, I: (I, src_off + p))
    full = lambda off: pl.BlockSpec((S, LW), lambda p, I: (0, off + p))
    const = lambda rows: pl.BlockSpec((rows, T), lambda p, I: (0, 0))
    dq, dk, dv, *got = pl.pallas_call(
        body, name=name, grid=grid,
        out_shape=[jax.ShapeDtypeStruct((S, W), F32)] * 3 + hosted.out_shapes,
        in_specs=[blk(0), full(npair), full(2 * npair), blk(0), blk(0), const(NH * T), const(T), const(T)]
        + hosted.in_specs,
        out_specs=[blk(0), full(0), full(0)] + hosted.out_specs,
        scratch_shapes=hosted.scratch,
        compiler_params=_cparams(hosted.semantics(("parallel", "arbitrary")), vmem_mb=56),
    )(p_sb, p_sb, p_sb, do, carries, _sb_causal(T, NH).astype(F32), _sb_triangle(T, True), _sb_triangle(T, False),
      *hosted.operands)
    return (dq, dk, dv) if comm is None else (dq, dk, dv, got)


def _dil_blocks(b, body_fn):
    for pi, (window, dil) in enumerate(DILATED_PATTERNS):
        assert window // dil == DIL_BLOCK
        nblk = DIL_SUPER // (DIL_BLOCK * dil)
        assert (dil * nblk) % DIL_UNROLL == 0

        def group(g, _, pi=pi, dil=dil, nblk=nblk):
            for u in range(DIL_UNROLL):
                t = g * DIL_UNROLL + u
                n = t % nblk
                body_fn(pi, dil, t // nblk, n, b * nblk + n)
            return 0

        lax.fori_loop(0, dil * nblk // DIL_UNROLL, group, 0)


def _dil_rows(start, size, dil):
    if dil == 1:
        return pl.ds(pl.multiple_of(start, DIL_BLOCK), size)
    return pl.ds(start, size, stride=dil)


def _dil_fill_bias(bias_ref):
    row = lax.broadcasted_iota(jnp.int32, (2 * DIL_BLOCK, 2 * DIL_BLOCK), 0)
    kk = lax.broadcasted_iota(jnp.int32, (2 * DIL_BLOCK, 2 * DIL_BLOCK), 1)
    qi = jnp.where(row >= DIL_BLOCK, row - DIL_BLOCK, row)
    for s in range(2):
        dist = s * DIL_BLOCK + qi - kk
        bias_ref[s] = jnp.where((dist >= 0) & (dist <= DIL_BLOCK), 0.0, NEG_BIG)


def _dl_fwd(p_dl, *, name):
    S, W = p_dl.shape[0], p_dl.shape[1] // 3
    npair = W // LANES
    nsuper = S // DIL_SUPER
    assert S % DIL_SUPER == 0 and S // max(d for _, d in DILATED_PATTERNS) >= 2 * DIL_BLOCK
    scale = HEAD_DIM ** -0.5
    npat = len(DILATED_PATTERNS)

    def body(q_ref, k_ref, v_ref, o_ref, l_ref, bias_ref, *pattern_refs):
        op_refs, lp_refs = pattern_refs[:npat], pattern_refs[npat:]
        b = pl.program_id(1)
        masks = _head_masks((DIL_BLOCK, LANES))
        pl.when(b == 0)(lambda: _dil_fill_bias(bias_ref))

        def block(pi, dil, c, n, gn):
            ws = jnp.maximum(gn - 1, 0)
            qrows = n * (DIL_BLOCK * dil) + c
            krows = ws * (DIL_BLOCK * dil) + c
            q_idx = _dil_rows(qrows, DIL_BLOCK, dil)
            k_idx = _dil_rows(krows, 2 * DIL_BLOCK, dil)
            qb = q_ref[q_idx, :]
            kb = k_ref[k_idx, :].astype(BF16)
            vb = v_ref[k_idx, :].astype(BF16)
            q2 = _sb_stack_heads(qb.astype(BF16), scale)
            z = lax.dot_general(q2, kb, (((1,), (1,)), ((), ())), preferred_element_type=F32) + bias_ref[gn - ws]
            m = jnp.max(z, axis=1, keepdims=True)
            p = jnp.exp(z - m)
            den = jnp.sum(p, axis=1, keepdims=True)
            acc = jnp.dot(p.astype(BF16), vb, preferred_element_type=F32)
            lse = m + jnp.log(den)
            op_refs[pi][q_idx, :] = _sb_unstack_heads(acc / den)
            lp_refs[pi][q_idx, :] = jnp.where(masks[0], lse[:DIL_BLOCK], lse[DIL_BLOCK:])

        _dil_blocks(b, block)
        lses = [r[...] for r in lp_refs]
        top = functools.reduce(jnp.maximum, lses)
        ws_ = [jnp.exp(l - top) for l in lses]
        den = functools.reduce(jnp.add, ws_)
        num = functools.reduce(jnp.add, [w * r[...] for r, w in zip(op_refs, ws_)])
        o_ref[...] = num / den
        l_ref[...] = top + jnp.log(den)

    blk = pl.BlockSpec((DIL_SUPER, LANES), lambda p, b: (b, p))
    full = lambda off: pl.BlockSpec((S, LANES), lambda p, b: (0, off + p))
    return pl.pallas_call(
        body, name=name, grid=(npair, nsuper),
        out_shape=[jax.ShapeDtypeStruct((S, W), F32)] * 2,
        in_specs=[blk, full(npair), full(2 * npair)], out_specs=[blk, blk],
        scratch_shapes=[pltpu.VMEM((2, 2 * DIL_BLOCK, 2 * DIL_BLOCK), F32)]
        + [pltpu.VMEM((DIL_SUPER, LANES), F32)] * (2 * npat),
        compiler_params=_cparams(("arbitrary", "arbitrary")),
    )(p_dl, p_dl, p_dl)


def _dl_bwd(p_dl, o, lse, do, *, name):
    S, W = p_dl.shape[0], p_dl.shape[1] // 3
    npair = W // LANES
    nsuper = S // DIL_SUPER
    scale = HEAD_DIM ** -0.5

    def body(q_ref, k_ref, v_ref, o_ref, l_ref, do_ref, dq_ref, dk_ref, dv_ref, delta_ref, bias_ref):
        b = pl.program_id(1)

        @pl.when(b == 0)
        def _():
            dk_ref[...] = jnp.zeros_like(dk_ref)
            dv_ref[...] = jnp.zeros_like(dv_ref)
            _dil_fill_bias(bias_ref)

        dq_ref[...] = jnp.zeros_like(dq_ref)
        prod = do_ref[...] * o_ref[...]
        delta = jnp.zeros_like(prod)
        for hm in _head_masks(prod.shape):
            delta = jnp.where(hm, jnp.sum(jnp.where(hm, prod, 0.0), axis=1, keepdims=True), delta)
        delta_ref[...] = delta

        def block(pi, dil, c, n, gn):
            ws = jnp.maximum(gn - 1, 0)
            qrows = n * (DIL_BLOCK * dil) + c
            krows = ws * (DIL_BLOCK * dil) + c
            q_idx = _dil_rows(qrows, DIL_BLOCK, dil)
            k_idx = _dil_rows(krows, 2 * DIL_BLOCK, dil)
            qb = q_ref[q_idx, :]
            dob = do_ref[q_idx, :]
            lb = l_ref[q_idx, :]
            db = delta_ref[q_idx, :]
            kb = k_ref[k_idx, :].astype(BF16)
            vb = v_ref[k_idx, :].astype(BF16)
            q2 = _sb_stack_heads(qb.astype(BF16), scale)
            do2 = _sb_stack_heads(dob.astype(BF16))
            lse2 = jnp.concatenate([lb[:, HEAD_DIM * h:HEAD_DIM * h + 1] for h in range(2)], axis=0)
            delta2 = jnp.concatenate([db[:, HEAD_DIM * h:HEAD_DIM * h + 1] for h in range(2)], axis=0)
            z = lax.dot_general(q2, kb, (((1,), (1,)), ((), ())), preferred_element_type=F32)
            p = jnp.exp((z + bias_ref[gn - ws]) - lse2)
            dp = lax.dot_general(do2, vb, (((1,), (1,)), ((), ())), preferred_element_type=F32)
            dzb = (p * (dp - delta2)).astype(BF16)
            tn_dims = (((0,), (0,)), ((), ()))
            dq_blk = _sb_unstack_heads(jnp.dot(dzb, kb, preferred_element_type=F32)) * scale
            dk_blk = lax.dot_general(dzb, q2, tn_dims, preferred_element_type=F32)
            dv_blk = lax.dot_general(p.astype(BF16), do2, tn_dims, preferred_element_type=F32)
            dq_ref[q_idx, :] = dq_ref[q_idx, :] + dq_blk
            dk_ref[k_idx, :] = dk_ref[k_idx, :] + dk_blk
            dv_ref[k_idx, :] = dv_ref[k_idx, :] + dv_blk

        _dil_blocks(b, block)

    blk = pl.BlockSpec((DIL_SUPER, LANES), lambda p, b: (b, p))
    full = lambda off: pl.BlockSpec((S, LANES), lambda p, b: (0, off + p))
    return pl.pallas_call(
        body, name=name, grid=(npair, nsuper),
        out_shape=[jax.ShapeDtypeStruct((S, W), F32)] * 3,
        in_specs=[blk, full(npair), full(2 * npair), blk, blk, blk], out_specs=[blk, full(0), full(0)],
        scratch_shapes=[pltpu.VMEM((DIL_SUPER, LANES), F32), pltpu.VMEM((2, 2 * DIL_BLOCK, 2 * DIL_BLOCK), F32)],
        compiler_params=_cparams(("arbitrary", "arbitrary")),
    )(p_dl, p_dl, p_dl, o, lse, do)


class _NoExchange:
    def gather(self, family):
        return None

    def gathered(self, family, got, weights):
        pass

    def send(self, family, grads):
        return None

    def received(self, family, got):
        pass


def _local_step(x, target, gains, weights, exchanges=None):
    S, D = x.shape
    ex = exchanges or _NoExchange()
    weights = dict(weights)
    d_sb = gains["sb_out_norm"].shape[1]
    d_dl = gains["dil_out_norm"].shape[1]
    cos_t, sin_t = _rope_tables(S)

    riders = ("ffn1_w_up", "ffn1_w_down", "mixer") if exchanges else (None, None, None)
    (x1, h2), saved1 = _ffn_fwd(x, gains["ffn1_norm"], weights, tag="ffn1", ex=ex, riders=riders,
                                first_rider="ffn1_w_gate" if exchanges else None,
                                next_gain=gains["mix_norm"])
    w_in = weights["w_in"]
    w_out = weights["w_out"]
    p_sb = _mm(h2, w_in, b_cols=(0, 3 * d_sb), outs=(BF16,), name="proj_sb")

    def rope_qk(acc, _, cos, sin):
        return jnp.concatenate([_rotate(acc[:, :2 * d_dl], cos, sin, 1.0), acc[:, 2 * d_dl:]], axis=1)

    p_dl = _mm(h2, w_in, b_cols=(3 * d_sb, 3 * d_dl), lanes=(cos_t, sin_t), epilogue=rope_qk, name="proj_dl_rope")
    plan = ex.gather("ffn2" if exchanges else None)
    o_sb, carries, *got = _sb_fwd(p_sb, name="sb_fwd", comm=plan)
    ex.gathered("ffn2", got[0] if got else None, weights)
    o_dl, lse_dl = _dl_fwd(p_dl, name="dl_fwd")
    merged = _rms_fwd([o_sb, o_dl], [gains["sb_out_norm"], gains["dil_out_norm"]], name="out_norm")
    x2, h3 = _mm(merged, w_out, res=x1, rows=(gains["ffn2_norm"],), outs=(F32, BF16),
                 epilogue=_residual_then_norm(1.0), name="out_proj_norm")
    (dx3, d_final, loss_wide), saved2 = _ffn_fwd(x2, gains["ffn2_norm"], weights, tag="ffn2", ex=ex, h=h3,
                                                 head=(gains["final_norm"], target))
    loss_row = loss_wide[:, :LANES]

    dx2, d_ffn2_norm, dwg2, dwu2, dwd2 = _ffn_bwd(dx3, x2, gains["ffn2_norm"], weights, saved2, tag="ffn2", ex=ex)
    d_w_out = _mm(merged, dx2, ta=True, outs=(GRAD_WIRE,), name="d_w_out")
    do_sb, d_sb_norm = _mm(dx2, w_out, tb=True, b_cols=(0, d_sb), extras=(o_sb,), rows=(gains["sb_out_norm"],),
                           row_sums=1, epilogue=_rms_bwd_epilogue, name="d_merged_sb")
    do_dl, d_dl_norm = _mm(dx2, w_out, tb=True, b_cols=(d_sb, d_dl), extras=(o_dl,), rows=(gains["dil_out_norm"],),
                           row_sums=1, epilogue=_rms_bwd_epilogue, name="d_merged_dl")
    plan = ex.send("ffn2", dict(ffn2_w_gate=dwg2, ffn2_w_up=dwu2, ffn2_w_down=dwd2))
    dq_sb, dk_sb, dv_sb, *got = _sb_bwd(p_sb, do_sb, carries, name="sb_bwd", comm=plan)
    ex.received("ffn2", got[0] if got else None)
    dq_dl, dk_dl, dv_dl = _dl_bwd(p_dl, o_dl, lse_dl, do_dl, name="dl_bwd")
    d_proj = _join_d_proj([dq_sb, dk_sb, dv_sb, dq_dl, dk_dl, dv_dl], (3, 4), cos_t, sin_t, name="d_proj")
    d_w_in = _mm(h2, d_proj, ta=True, outs=(GRAD_WIRE,), name="d_w_in")
    dx1, d_mix_norm = _mm(d_proj, w_in, tb=True, extras=(x1, dx2), rows=(gains["mix_norm"],), row_sums=1,
                          epilogue=_rms_bwd_epilogue, name="dh_mix_norm_bwd")
    dx, d_ffn1_norm, dwg1, dwu1, dwd1 = _ffn_bwd(
        dx1, x, gains["ffn1_norm"], weights, saved1, tag="ffn1", ex=ex,
        rider=("mixer", dict(w_in=d_w_in, w_out=d_w_out)), spread=True)
    gain_grads = dict(ffn1_norm=d_ffn1_norm, mix_norm=d_mix_norm, sb_out_norm=d_sb_norm, dil_out_norm=d_dl_norm,
                      ffn2_norm=d_ffn2_norm, final_norm=d_final)
    weight_grads = dict(ffn1_w_gate=dwg1, ffn1_w_up=dwu1, ffn1_w_down=dwd1, w_in=d_w_in, w_out=d_w_out,
                        ffn2_w_gate=dwg2, ffn2_w_up=dwu2, ffn2_w_down=dwd2)
    return loss_row, dx, gain_grads, weight_grads


def _mesh_position():
    return lax.axis_index("x"), lax.axis_index("y"), lax.axis_index("c")


def _flip(coord, bit):
    return 1 - coord if bit else coord


RELATIONS = [(rx, ry, rc) for rx in (0, 1) for ry in (0, 1) for rc in (0, 1)][1:]


class _GatherPlan:
    def __init__(self, shards):
        n = len(shards)
        self.operands = list(shards)
        self.out_shapes = [jax.ShapeDtypeStruct((N_DEV,) + s.shape, s.dtype) for s in shards]
        self.scratch = [pltpu.SemaphoreType.DMA((n, 7)), pltpu.SemaphoreType.DMA((n, 7)),
                        pltpu.SemaphoreType.DMA((n,))]

    def _copies(self, in_refs, out_refs, sems):
        send_sems, recv_sems, local_sems = sems
        x, y, c = _mesh_position()
        me, sibling = (x, y, c), (x, y, 1 - c)
        chips = [(1 - x, y), (x, 1 - y), (1 - x, 1 - y)]
        plans = []
        for t, (x_ref, out_ref) in enumerate(zip(in_refs, out_refs)):
            def slot(px, py, pc, out_ref=out_ref):
                return out_ref.at[4 * px + 2 * py + pc]

            def copy(k, block, to, src=None, t=t, slot=slot):
                return pltpu.make_async_remote_copy(
                    src_ref=slot(*block) if src is None else src, dst_ref=slot(*block),
                    send_sem=send_sems.at[t, k], recv_sem=recv_sems.at[t, k],
                    device_id=to, device_id_type=pl.DeviceIdType.MESH)

            plans.append(dict(
                mine=pltpu.make_async_copy(x_ref, slot(*me), local_sems.at[t]),
                first=[copy(0, me, sibling, src=x_ref)]
                + [copy(1 + j, me, (*chip, c), src=x_ref) for j, chip in enumerate(chips)],
                over_ici=[copy(1 + j, (*chip, c), me) for j, chip in enumerate(chips)],
                passed=[copy(4 + j, (*chip, c), sibling) for j, chip in enumerate(chips)],
                from_sibling=[copy(0, sibling, me)] + [copy(4 + j, (*chip, 1 - c), me) for j, chip in enumerate(chips)]))
        return plans

    def start(self, in_refs, out_refs, sems):
        for p in self._copies(in_refs, out_refs, sems):
            p["mine"].start()
            for cp in p["first"]:
                cp.start()

    def finish(self, in_refs, out_refs, sems):
        plans = self._copies(in_refs, out_refs, sems)
        for p in plans:
            for arrived, onward in zip(p["over_ici"], p["passed"]):
                arrived.wait_recv()
                onward.start()
        for p in plans:
            for cp in p["from_sibling"]:
                cp.wait_recv()
            for cp in p["first"] + p["passed"]:
                cp.wait_send()
            p["mine"].wait()


class _Hosted:
    def __init__(self, plan, n_in, n_out, n_scratch):
        self.plan, self.n_in, self.n_out, self.n_scratch = plan, n_in, n_out, n_scratch
        self.operands = list(plan.operands) if plan else []
        self.out_shapes = list(plan.out_shapes) if plan else []
        self.scratch = list(plan.scratch) if plan else []
        self.in_specs = [pl.BlockSpec(memory_space=pl.ANY)] * len(self.operands)
        self.out_specs = [pl.BlockSpec(memory_space=pl.ANY)] * len(self.out_shapes)

    def semantics(self, sem):
        return sem if self.plan is None else ("arbitrary",) * len(sem)

    def _at(self, grid, last):
        hit = None
        for d, n in enumerate(grid):
            here = pl.program_id(d) == (n - 1 if last else 0)
            hit = here if hit is None else hit & here
        return hit

    def begin(self, refs, grid):
        if self.plan is None:
            return refs
        k_in, k_out = len(self.operands), len(self.out_shapes)
        ins, rest = refs[:self.n_in], refs[self.n_in:]
        c_in, rest = rest[:k_in], rest[k_in:]
        outs, rest = rest[:self.n_out], rest[self.n_out:]
        c_out, rest = rest[:k_out], rest[k_out:]
        scratch, sems = rest[:self.n_scratch], rest[self.n_scratch:]
        self._args = (c_in, c_out, sems)
        pl.when(self._at(grid, False))(lambda: self.plan.start(*self._args))
        return tuple(ins) + tuple(outs) + tuple(scratch)

    def end(self, grid):
        if self.plan is not None:
            pl.when(self._at(grid, True))(lambda: self.plan.finish(*self._args))


class _ExchangePlan:
    def __init__(self, packs):
        n = len(packs)
        self.operands = list(packs)
        self.out_shapes = [jax.ShapeDtypeStruct(p.shape, p.dtype) for p in packs]
        self.scratch = [pltpu.SemaphoreType.DMA((n, 7)), pltpu.SemaphoreType.DMA((n, 7)),
                        pltpu.SemaphoreType.DMA((n,))]

    def _copies(self, in_refs, out_refs, sems):
        send_sems, recv_sems, local_sems = sems
        x, y, c = _mesh_position()
        me = 4 * x + 2 * y + c
        copies = [pltpu.make_async_copy(i.at[me], o.at[me], local_sems.at[t])
                  for t, (i, o) in enumerate(zip(in_refs, out_refs))]
        for r, (rx, ry, rc) in enumerate(RELATIONS):
            px, py, pc = _flip(x, rx), _flip(y, ry), _flip(c, rc)
            peer = 4 * px + 2 * py + pc
            copies += [pltpu.make_async_remote_copy(
                src_ref=i.at[peer], dst_ref=o.at[me], send_sem=send_sems.at[t, r], recv_sem=recv_sems.at[t, r],
                device_id=(px, py, pc), device_id_type=pl.DeviceIdType.MESH)
                for t, (i, o) in enumerate(zip(in_refs, out_refs))]
        return copies

    def start(self, in_refs, out_refs, sems):
        for cp in self._copies(in_refs, out_refs, sems):
            cp.start()

    def finish(self, in_refs, out_refs, sems):
        for cp in self._copies(in_refs, out_refs, sems):
            cp.wait()


def _all_reduce_rows(v, *, name):
    R, C = v.shape

    def body(v_ref, out_ref, buf, send_sems, recv_sems):
        x, y, c = _mesh_position()
        me = 4 * x + 2 * y + c
        buf[me] = v_ref[...]
        copies = []
        for r, (rx, ry, rc) in enumerate(RELATIONS):
            cp = pltpu.make_async_remote_copy(
                src_ref=v_ref, dst_ref=buf.at[me], send_sem=send_sems.at[r], recv_sem=recv_sems.at[r],
                device_id=(_flip(x, rx), _flip(y, ry), _flip(c, rc)), device_id_type=pl.DeviceIdType.MESH)
            cp.start()
            copies.append(cp)
        for cp in copies:
            cp.wait()
        total = buf[0]
        for s in range(1, N_DEV):
            total = total + buf[s]
        out_ref[...] = total

    return pl.pallas_call(
        body, name=name,
        out_shape=jax.ShapeDtypeStruct((R, C), F32),
        in_specs=[pl.BlockSpec(memory_space=pltpu.VMEM)],
        out_specs=pl.BlockSpec(memory_space=pltpu.VMEM),
        scratch_shapes=[pltpu.VMEM((N_DEV, R, C), F32), pltpu.SemaphoreType.DMA((7,)), pltpu.SemaphoreType.DMA((7,))],
    )(v)


def _adamw(w, g, m, v, *, name):
    R, C = w.shape
    slots = g.ndim == 3
    tr = _pick(R, (256, 128, 64, 32, 16) if slots else (256, 128, 64, 32, 16, 8))

    def body(w_ref, g_ref, m_ref, v_ref, g_out, d_ref, nm_ref, nv_ref):
        if slots:
            g = g_ref[0].astype(F32)
            for s in range(1, N_DEV):
                g = g + g_ref[s].astype(F32)
        else:
            g = g_ref[...]
        g_out[...] = g
        m_new = ADAM_B1 * m_ref[...] + (1.0 - ADAM_B1) * g
        v_new = ADAM_B2 * v_ref[...] + (1.0 - ADAM_B2) * (g * g)
        m_hat = m_new / (1.0 - ADAM_B1 ** ADAM_STEP)
        v_hat = v_new / (1.0 - ADAM_B2 ** ADAM_STEP)
        d_ref[...] = -ADAM_LR * (m_hat / (jnp.sqrt(v_hat) + ADAM_EPS) + ADAM_WD * w_ref[...])
        nm_ref[...] = m_new
        nv_ref[...] = v_new

    spec = pl.BlockSpec((tr, C), lambda i: (i, 0))
    g_spec = pl.BlockSpec((N_DEV, tr, C), lambda i: (0, i, 0)) if slots else spec
    return pl.pallas_call(
        body, name=name, grid=(R // tr,),
        out_shape=[jax.ShapeDtypeStruct((R, C), F32)] * 4,
        in_specs=[spec, g_spec, spec, spec], out_specs=[spec] * 4,
        compiler_params=_cparams(("parallel",)),
    )(w, g, m, v)


WEIGHT_NAMES = ["ffn1_norm", "ffn1_w_gate", "ffn1_w_up", "ffn1_w_down", "mix_norm", "w_in", "sb_out_norm",
                "dil_out_norm", "w_out", "ffn2_norm", "ffn2_w_gate", "ffn2_w_up", "ffn2_w_down", "final_norm"]
GAIN_NAMES = ["ffn1_norm", "mix_norm", "sb_out_norm", "dil_out_norm", "ffn2_norm", "final_norm"]
COL_SHARDED = ["ffn1_w_gate", "ffn1_w_up", "ffn2_w_gate", "ffn2_w_up", "w_in"]
ROW_SHARDED = ["ffn1_w_down", "ffn2_w_down", "w_out"]
GROUPS = {"mixer": (["w_in"], ["w_out"]),
          "ffn2": (["ffn2_w_gate", "ffn2_w_up"], ["ffn2_w_down"])}
for _ffn in ("ffn1", "ffn2"):
    GROUPS.update({f"{_ffn}_w_gate": ([f"{_ffn}_w_gate"], []), f"{_ffn}_w_up": ([f"{_ffn}_w_up"], []),
                   f"{_ffn}_w_down": ([], [f"{_ffn}_w_down"])})


class _Exchanges:
    def __init__(self, params):
        self.params = params
        self.grads = {}

    def gather(self, group):
        if group is None:
            return None
        cols, rows = GROUPS[group]
        return _GatherPlan([self.params[n].astype(BF16) for n in cols + rows])

    def gathered(self, group, got, weights):
        if group is None:
            return
        cols, rows = GROUPS[group]
        for n, blocks in zip(cols + rows, got):
            if n in cols:
                weights[n] = jnp.transpose(blocks, (1, 0, 2)).reshape(blocks.shape[1], N_DEV * blocks.shape[2])
            else:
                weights[n] = blocks.reshape(N_DEV * blocks.shape[1], blocks.shape[2])

    def send(self, group, grads):
        if group is None:
            return None
        cols, rows = GROUPS[group]
        packs = [jnp.transpose(grads[n].reshape(grads[n].shape[0], N_DEV, self.params[n].shape[1]), (1, 0, 2))
                 for n in cols]
        packs += [grads[n].reshape(N_DEV, self.params[n].shape[0], grads[n].shape[1]) for n in rows]
        return _ExchangePlan([p.astype(GRAD_WIRE) for p in packs])

    def received(self, group, got):
        if group is None:
            return
        cols, rows = GROUPS[group]
        for n, slots in zip(cols + rows, got):
            self.grads[n] = slots


def _step(x, target, params, moments_m, moments_v):
    ex = _Exchanges(params)
    weights = {}
    gains = {n: params[n] for n in GAIN_NAMES}
    loss_row, grad_x, gain_grads, _ = _local_step(x, target, gains, weights, ex)
    grads = ex.grads

    rows = [gain_grads[n].reshape(-1, LANES) for n in GAIN_NAMES] + [loss_row]
    small = jnp.concatenate(rows, axis=0)
    pad = (-small.shape[0]) % 8
    small = jnp.pad(small, ((0, pad), (0, 0)))
    small = _all_reduce_rows(small, name="reduce_gains_loss")
    off = 0
    for n in GAIN_NAMES:
        r = gain_grads[n].shape[1] // LANES
        grads[n] = small[off:off + r].reshape(1, -1)
        off += r
    loss = small[off, 0]

    delta, new_m, new_v = {}, {}, {}
    for n in WEIGHT_NAMES:
        grads[n], delta[n], new_m[n], new_v[n] = _adamw(params[n], grads[n], moments_m[n], moments_v[n],
                                                        name=f"adamw_{n}")
    return loss, grad_x, grads, delta, new_m, new_v


def kernel(x, ffn1_norm, ffn1_w_gate, ffn1_w_up, ffn1_w_down, mix_norm, w_in, sb_out_norm, dil_out_norm, w_out, ffn2_norm, ffn2_w_gate, ffn2_w_up, ffn2_w_down, final_norm, loss_target, m_ffn1_norm, m_ffn1_w_gate, m_ffn1_w_up, m_ffn1_w_down, m_mix_norm, m_w_in, m_sb_out_norm, m_dil_out_norm, m_w_out, m_ffn2_norm, m_ffn2_w_gate, m_ffn2_w_up, m_ffn2_w_down, m_final_norm, v_ffn1_norm, v_ffn1_w_gate, v_ffn1_w_up, v_ffn1_w_down, v_mix_norm, v_w_in, v_sb_out_norm, v_dil_out_norm, v_w_out, v_ffn2_norm, v_ffn2_w_gate, v_ffn2_w_up, v_ffn2_w_down, v_final_norm):
    given = dict(locals())
    shapes = {n: given[n].shape for n in WEIGHT_NAMES}

    def as2d(a):
        return a.reshape(1, -1) if a.ndim == 1 else a.reshape(a.shape[-2], a.shape[-1])

    params = {n: as2d(given[n]) for n in WEIGHT_NAMES}
    moments_m = {n: as2d(given["m_" + n]) for n in WEIGHT_NAMES}
    moments_v = {n: as2d(given["v_" + n]) for n in WEIGHT_NAMES}
    loss, grad_x, grads, delta, new_m, new_v = _step(x[0], loss_target[0], params, moments_m, moments_v)
    back = lambda d: [d[n].reshape(shapes[n]) for n in WEIGHT_NAMES]
    return (loss, grad_x[None], *back(grads), *back(delta), *back(new_m), *back(new_v))
```

```python
import functools

import jax
import jax.numpy as jnp
from jax import lax
from jax.experimental import pallas as pl
from jax.experimental.pallas import tpu as pltpu

F32 = jnp.float32
BF16 = jnp.bfloat16
GRAD_WIRE = jnp.bfloat16

N_DEV = 8
HEAD_DIM = 64
LANES = 128
DILATED_PATTERNS = ((128, 1), (512, 4), (2048, 16))
DIL_BLOCK = 128
DIL_SUPER = 2048
DIL_UNROLL = 16
SB_TILE = 256
SB_LANES = 128
SB_UNROLL = 4
SB_STEP_TILES = 2
SB_DEAD = 90.0
SB_UNSEEN = -1e30
ROPE_THETA = 10000.0
RMS_EPS = 1e-6
HALF_STEP = 0.5
ADAM_LR = 0.001
ADAM_B1 = 0.9
ADAM_B2 = 0.999
ADAM_EPS = 1e-08
ADAM_WD = 0.01
ADAM_STEP = 10
NEG_BIG = -1e30
VMEM_CAP_MB = 60


def _pick(n, prefs):
    for p in prefs:
        if n % p == 0:
            return p
    return n


MM_MAX_TILE = 1536
MM_WHOLE = 3072


def _largest_tile(n, cap):
    if n <= cap:
        return n
    for t in range(cap - cap % LANES, 0, -LANES):
        if n % t == 0:
            return t
    return n


def _cparams(sem=None, vmem_mb=48):
    return pltpu.CompilerParams(dimension_semantics=sem, vmem_limit_bytes=min(vmem_mb, VMEM_CAP_MB) * 1024 * 1024)


def _nbytes(shape, dtype):
    n = 1
    for s in shape:
        n *= s
    return n * jnp.dtype(dtype).itemsize


def _mm(a, b, *, name, ta=False, tb=False, outs=(F32,), res=None, alpha=1.0, extras=(), epilogue=None,
        tm=None, tn=None, tk=None, comm=None, rows=(), lanes=(), row_sums=0, b_cols=None):
    if ta:
        K, M = a.shape
    else:
        M, K = a.shape
    if tb:
        N, Kb = b.shape
    else:
        Kb, N = b.shape
    col0 = 0
    if b_cols is not None:
        col0, N = b_cols
    assert K == Kb, (a.shape, b.shape, ta, tb)
    tn = tn or (N if (not ta and K <= MM_WHOLE and N <= MM_WHOLE) else _largest_tile(N, MM_MAX_TILE))
    wide = tn > MM_MAX_TILE and (len(extras) + len(outs) > 3 or a.dtype == F32)
    tm = tm or (_largest_tile(M, MM_MAX_TILE) if ta else _pick(M, (256, 128) if wide else (512, 256, 128)))
    tk = tk or (K if K <= MM_WHOLE else _pick(K, (2048, 1024, 512, 256, 128)))
    nk = K // tk
    a_spec = pl.BlockSpec((tk, tm), lambda i, j, k: (k, i)) if ta else pl.BlockSpec((tm, tk), lambda i, j, k: (i, k))
    assert col0 % tn == 0
    b_spec = (pl.BlockSpec((tn, tk), lambda i, j, k: (j + col0 // tn, k)) if tb
              else pl.BlockSpec((tk, tn), lambda i, j, k: (k, j + col0 // tn)))
    mn_spec = pl.BlockSpec((tm, tn), lambda i, j, k: (i, j))
    dims = (((0 if ta else 1,), (1 if tb else 0,)), ((), ()))
    row_spec = pl.BlockSpec((1, tn), lambda i, j, k: (0, j))
    lane_spec = pl.BlockSpec((tm, LANES), lambda i, j, k: (i, 0))
    n_extra = len(extras) + (1 if res is not None else 0) + len(rows) + len(lanes)
    n_mn = len(outs)
    n_out = n_mn + row_sums
    assert row_sums == 0 or tn == N
    grid = (M // tm, N // tn, nk)
    hosted = _Hosted(comm, n_in=2 + n_extra, n_out=n_out, n_scratch=1 if nk > 1 else 0)

    def body(*refs):
        a_ref, b_ref = refs[0], refs[1]
        in_refs = refs[2:2 + n_extra]
        refs = hosted.begin(refs, grid)
        out_refs = refs[2 + n_extra:2 + n_extra + n_out]
        prod = lax.dot_general(a_ref[...].astype(BF16), b_ref[...].astype(BF16), dims, preferred_element_type=F32)

        def finish(acc):
            blocks = [r[...] for r in in_refs]
            if res is not None:
                r_blk, blocks = blocks[0], blocks[1:]
            else:
                r_blk = None
            if epilogue is None:
                val = acc * alpha
                if r_blk is not None:
                    val = val + r_blk
                vals = (val,)
            else:
                vals = epilogue(acc, r_blk, *blocks)
                vals = vals if isinstance(vals, (tuple, list)) else (vals,)
            for o_ref, v in zip(out_refs[:n_mn], vals[:n_mn]):
                o_ref[...] = v.astype(o_ref.dtype)
            first_rows = pl.program_id(0) == 0
            for o_ref, part in zip(out_refs[n_mn:], vals[n_mn:]):
                @pl.when(first_rows)
                def _(o_ref=o_ref, part=part):
                    o_ref[...] = part

                @pl.when(jnp.logical_not(first_rows))
                def _(o_ref=o_ref, part=part):
                    o_ref[...] += part

        if nk == 1:
            finish(prod)
        else:
            acc_ref = refs[2 + n_extra + n_out]
            k = pl.program_id(2)

            @pl.when(k == 0)
            def _():
                acc_ref[...] = prod

            @pl.when(k > 0)
            def _():
                acc_ref[...] += prod

            @pl.when(k == nk - 1)
            def _():
                finish(acc_ref[...])

        hosted.end(grid)

    mn_operands = ([res] if res is not None else []) + list(extras)
    operands = [a, b] + mn_operands + list(rows) + list(lanes)
    in_specs = [a_spec, b_spec] + [mn_spec] * len(mn_operands) + [row_spec] * len(rows) + [lane_spec] * len(lanes)
    est = 2 * (_nbytes((tm, tk), a.dtype) + _nbytes((tk, tn), b.dtype))
    est += 2 * sum(_nbytes((tm, tn), o.dtype) for o in mn_operands)
    est += 2 * sum(_nbytes((tm, tn), d) for d in outs) + 2 * _nbytes((tm, tn), F32)
    semantics = ("parallel", "parallel", "arbitrary") if row_sums == 0 else ("arbitrary",) * 3
    result = pl.pallas_call(
        body, name=name, grid=grid,
        out_shape=[jax.ShapeDtypeStruct((M, N), d) for d in outs]
        + [jax.ShapeDtypeStruct((1, N), F32)] * row_sums + hosted.out_shapes,
        in_specs=in_specs + hosted.in_specs,
        out_specs=[mn_spec] * n_mn + [row_spec] * row_sums + hosted.out_specs,
        scratch_shapes=([pltpu.VMEM((tm, tn), F32)] if nk > 1 else []) + hosted.scratch,
        compiler_params=_cparams(hosted.semantics(semantics), vmem_mb=max(32, 2 * est // (1024 * 1024))),
    )(*operands, *hosted.operands)
    own, got = result[:n_out], list(result[n_out:])
    own = own[0] if n_out == 1 else own
    return own if comm is None else (own, got)


def _rms_hat(x):
    r = lax.rsqrt(jnp.mean(x * x, axis=-1, keepdims=True) + RMS_EPS)
    return x * r, r


def _rms_fwd(xs, gains, *, name, comm=None):
    S = xs[0].shape[0]
    widths = [x.shape[1] for x in xs]
    tm = _pick(S, (512, 256, 128))
    n = len(xs)
    grid = (S // tm,)
    hosted = _Hosted(comm, n_in=2 * n, n_out=1, n_scratch=0)

    def body(*refs):
        refs = hosted.begin(refs, grid)
        o_ref = refs[2 * n]
        off = 0
        for i in range(n):
            xh, _ = _rms_hat(refs[i][...])
            o_ref[:, off:off + widths[i]] = (xh * refs[n + i][...]).astype(o_ref.dtype)
            off += widths[i]
        hosted.end(grid)

    out, *got = pl.pallas_call(
        body, name=name, grid=grid,
        out_shape=[jax.ShapeDtypeStruct((S, sum(widths)), BF16)] + hosted.out_shapes,
        in_specs=[pl.BlockSpec((tm, w), lambda i: (i, 0)) for w in widths]
        + [pl.BlockSpec((1, w), lambda i: (0, 0)) for w in widths] + hosted.in_specs,
        out_specs=[pl.BlockSpec((tm, sum(widths)), lambda i: (i, 0))] + hosted.out_specs,
        scratch_shapes=hosted.scratch,
        compiler_params=_cparams(hosted.semantics(("parallel",))),
    )(*xs, *gains, *hosted.operands)
    return out if comm is None else (out, got)


def _sigmoid(g):
    return 1.0 / (1.0 + jnp.exp(-g))


def _ride(result, plan):
    return result if plan is not None else (result, None)


def _residual_then_norm(alpha):
    def epilogue(acc, res, gain):
        y = res + alpha * acc
        return y, _rms_hat(y)[0] * gain
    return epilogue


def _ffn_fwd(x, gain, w, *, tag, ex, first_rider=None, riders=(None, None, None), head=None, h=None,
             next_gain=None):
    if h is None:
        plan = ex.gather(first_rider)
        h, got = _ride(_rms_fwd([x], [gain], name=f"{tag}_norm", comm=plan), plan)
        ex.gathered(first_rider, got, w)
    plan = ex.gather(riders[0])
    g, got = _ride(_mm(h, w[f"{tag}_w_gate"], outs=(BF16,), name=f"{tag}_gate", comm=plan), plan)
    ex.gathered(riders[0], got, w)

    def act(acc, _, g_blk):
        gf = g_blk.astype(F32)
        return acc, gf * _sigmoid(gf) * acc

    plan = ex.gather(riders[1])
    (u, a), got = _ride(_mm(h, w[f"{tag}_w_up"], outs=(BF16, BF16), extras=(g,), epilogue=act, name=f"{tag}_up_act",
                            comm=plan), plan)
    ex.gathered(riders[1], got, w)
    plan = ex.gather(riders[2])
    if head is None and next_gain is None:
        y, got = _ride(_mm(a, w[f"{tag}_w_down"], res=x, alpha=HALF_STEP, name=f"{tag}_down", comm=plan), plan)
    elif head is None:
        y, got = _ride(_mm(a, w[f"{tag}_w_down"], res=x, rows=(next_gain,), outs=(F32, BF16),
                           epilogue=_residual_then_norm(HALF_STEP), name=f"{tag}_down_norm", comm=plan), plan)
    else:
        final_gain, target = head
        y, got = _ride(_mm(a, w[f"{tag}_w_down"], res=x, extras=(target,), rows=(final_gain,), row_sums=2,
                           epilogue=_loss_head_epilogue, name=f"{tag}_down_loss", comm=plan), plan)
    ex.gathered(riders[2], got, w)
    return y, (h, g, u, a)


def _loss_head_epilogue(acc, x_in, target, gain):
    xh, r = _rms_hat(x_in + HALF_STEP * acc)
    err = xh * gain - target
    dy = err * (1.0 / acc.shape[1])
    dxh = dy * gain
    dx = r * (dxh - xh * jnp.mean(dxh * xh, axis=-1, keepdims=True))
    loss = 0.5 * jnp.sum(jnp.mean(err * err, axis=-1, keepdims=True), axis=0, keepdims=True)
    return dx, jnp.sum(dy * xh, axis=0, keepdims=True), jnp.zeros_like(gain) + loss


def _rms_bwd_epilogue(acc, dh_so_far, x, *dres_and_gain):
    gain = dres_and_gain[-1]
    dh = acc if dh_so_far is None else acc + dh_so_far
    xh, r = _rms_hat(x)
    dxh = dh * gain
    dx = r * (dxh - xh * jnp.mean(dxh * xh, axis=-1, keepdims=True))
    if len(dres_and_gain) == 2:
        dx = dx + dres_and_gain[0]
    return dx, jnp.sum(dh * xh, axis=0, keepdims=True)


def _ffn_bwd(dout, x, gain, w, saved, *, tag, ex, rider=(None, None), spread=False):
    h, g, u, a = saved
    wg, wu, wd = (w[f"{tag}_w_{n}"] for n in ("gate", "up", "down"))

    def act_bwd(acc, _, g_blk, u_blk):
        gf, uf = g_blk.astype(F32), u_blk.astype(F32)
        da = acc * HALF_STEP
        sig = _sigmoid(gf)
        silu = gf * sig
        return da * uf * (sig + silu * (1.0 - sig)), da * silu

    def carrying(group, grad, call):
        group = group if spread else None
        plan = ex.send(group, {group: grad})
        out, got = _ride(call(plan), plan)
        ex.received(group, got)
        return out

    plan = ex.send(*rider)
    (dg, du), got = _ride(_mm(dout, wd, tb=True, outs=(BF16, BF16), extras=(g, u), epilogue=act_bwd,
                              name=f"{tag}_bwd_act", comm=plan), plan)
    ex.received(rider[0], got)
    dwg = _mm(h, dg, ta=True, outs=(GRAD_WIRE,), name=f"{tag}_dwg")
    dwu = carrying(f"{tag}_w_gate", dwg, lambda plan: _mm(h, du, ta=True, outs=(GRAD_WIRE,), name=f"{tag}_dwu", comm=plan))
    dwd = carrying(f"{tag}_w_up", dwu,
                   lambda plan: _mm(a, dout, ta=True, outs=(GRAD_WIRE,), alpha=HALF_STEP, name=f"{tag}_dwd", comm=plan))
    dh = carrying(f"{tag}_w_down", dwd, lambda plan: _mm(dg, wg, tb=True, name=f"{tag}_dh_gate", comm=plan))
    dx, dgain = _mm(du, wu, tb=True, res=dh, extras=(x, dout), rows=(gain,), row_sums=1, epilogue=_rms_bwd_epilogue,
                    name=f"{tag}_dh_up_norm_bwd")
    return dx, dgain, dwg, dwu, dwd


def _rope_tables(S):
    half = HEAD_DIM // 2
    inv_freq = ROPE_THETA ** (-jnp.arange(half, dtype=F32) / half)
    ang = jnp.arange(S, dtype=F32)[:, None] * inv_freq[None, :]
    cos, sin = jnp.cos(ang), jnp.sin(ang)
    reps = LANES // HEAD_DIM
    cos_t = jnp.tile(jnp.concatenate([cos, cos], axis=1), (1, reps))
    sin_t = jnp.tile(jnp.concatenate([-sin, sin], axis=1), (1, reps))
    return cos_t, sin_t


def _rotate(v, cos, sin, sign):
    half = HEAD_DIM // 2
    groups = []
    for g in range(v.shape[1] // LANES):
        t = v[:, g * LANES:(g + 1) * LANES]
        lane = lax.broadcasted_iota(jnp.int32, t.shape, 1)
        swapped = jnp.where(lane % HEAD_DIM < half, pltpu.roll(t, LANES - half, axis=1), pltpu.roll(t, half, axis=1))
        groups.append(t * cos + swapped * (sin * sign))
    return groups[0] if len(groups) == 1 else jnp.concatenate(groups, axis=1)


def _join_d_proj(pieces, rotated, cos_t, sin_t, *, name):
    S = pieces[0].shape[0]
    widths = [p.shape[1] for p in pieces]
    tm = _pick(S, (256, 128))
    n = len(pieces)

    def body(*refs):
        c_ref, s_ref, o_ref = refs[n], refs[n + 1], refs[n + 2]
        off = 0
        for i in range(n):
            v = refs[i][...]
            if i in rotated:
                v = _rotate(v, c_ref[...], s_ref[...], -1.0)
            o_ref[:, off:off + widths[i]] = v.astype(o_ref.dtype)
            off += widths[i]

    return pl.pallas_call(
        body, name=name, grid=(S // tm,),
        out_shape=jax.ShapeDtypeStruct((S, sum(widths)), BF16),
        in_specs=[pl.BlockSpec((tm, w), lambda i: (i, 0)) for w in widths]
        + [pl.BlockSpec((tm, LANES), lambda i: (i, 0))] * 2,
        out_specs=pl.BlockSpec((tm, sum(widths)), lambda i: (i, 0)),
        compiler_params=_cparams(("parallel",)),
    )(*pieces, cos_t, sin_t)


def _head_masks(shape):
    lane = lax.broadcasted_iota(jnp.int32, shape, 1)
    return [(lane >= HEAD_DIM * h) & (lane < HEAD_DIM * (h + 1)) for h in range(shape[1] // HEAD_DIM)]


def _sb_scores(q2, k_j):
    z = lax.dot_general(q2, k_j, (((1,), (1,)), ((), ())), preferred_element_type=F32)
    sign_bit = jnp.int32(-2 ** 31)
    minus_abs = lax.bitcast_convert_type(lax.bitcast_convert_type(z, jnp.int32) | sign_bit, F32)
    softplus = jnp.maximum(z, 0.0) + jnp.log(1.0 + jnp.exp(minus_abs))
    return z - softplus, softplus


def _sb_stack_heads(t, scale=None):
    parts = [jnp.where(hm, t, jnp.zeros_like(t)) for hm in _head_masks(t.shape)]
    t2 = jnp.concatenate(parts, axis=0)
    if scale is not None:
        t2 = (t2.astype(F32) * scale).astype(t2.dtype)
    return t2


def _sb_unstack_heads(t2):
    n = t2.shape[1] // HEAD_DIM
    T = t2.shape[0] // n
    masks = _head_masks((T, t2.shape[1]))
    out = t2[:T]
    for h in range(1, n):
        out = jnp.where(masks[h], t2[h * T:(h + 1) * T], out)
    return out


def _sb_causal(T, n_heads):
    row = lax.broadcasted_iota(jnp.int32, (n_heads * T, T), 0)
    col = lax.broadcasted_iota(jnp.int32, (n_heads * T, T), 1)
    return col < row % T


def _sb_triangle(T, later):
    row = lax.broadcasted_iota(jnp.int32, (T, T), 0)
    col = lax.broadcasted_iota(jnp.int32, (T, T), 1)
    return ((row > col) if later else (row < col)).astype(BF16)


def _sb_fwd(p_sb, *, name, comm=None):
    S = p_sb.shape[0]
    W = p_sb.shape[1] // 3
    LW = min(SB_LANES, W)
    NH = LW // HEAD_DIM
    npair = W // LW
    T = SB_TILE
    n_tiles = S // T
    assert n_tiles <= HEAD_DIM
    scale = HEAD_DIM ** -0.5

    R = SB_STEP_TILES
    grid = (npair, n_tiles // R)
    hosted = _Hosted(comm, n_in=5, n_out=2, n_scratch=0)

    def body(*refs):
        refs = hosted.begin(refs, grid)
        step = pl.program_id(1)
        lax.fori_loop(0, R, lambda sub, _: query_tile(step * R + sub, sub, *refs), 0)
        hosted.end(grid)

    def query_tile(I, sub, q_ref, k_ref, v_ref, causal_ref, later_ref, o_ref, c_ref):
        rows = pl.ds(pl.multiple_of(sub * T, T), T)
        lane = lax.broadcasted_iota(jnp.int32, (T, LW), 1)
        causal = causal_ref[...]
        later_than = later_ref[...]
        q2 = _sb_stack_heads(q_ref[rows, :], scale)

        def scores(J, diag):
            off = pl.multiple_of(J * T, T)
            log_beta, stay = _sb_scores(q2, k_ref[pl.ds(off, T), :])
            if diag:
                stay = stay * causal
            local = jnp.dot(stay.astype(BF16), later_than, preferred_element_type=F32)
            return log_beta, local, jnp.sum(stay, axis=1, keepdims=True), v_ref[pl.ds(off, T), :]

        def weigh(J, sc, gone, acc, carr, diag):
            log_beta, local, _, v_j = sc
            w = jnp.exp((log_beta - gone) - local)
            if diag:
                w = w * causal
            acc = acc + jnp.dot(w.astype(BF16), v_j, preferred_element_type=F32)
            for h in range(NH):
                carr = jnp.where(lane == HEAD_DIM * h + J, -gone[h * T:(h + 1) * T], carr)
            return acc, carr

        def tiles(J, count, state, diag):
            gone, acc, carr, _ = state
            scs = [scores(J - u, diag and u == 0) for u in range(count)]
            for u, sc in enumerate(scs):
                acc, carr = weigh(J - u, sc, gone, acc, carr, diag and u == 0)
                gone = gone + sc[2]
            return gone, acc, carr, jnp.min(gone)

        U = SB_UNROLL
        alive = lambda st: st[3] < SB_DEAD
        state = (jnp.zeros((NH * T, 1), F32), jnp.zeros((NH * T, LW), F32),
                 jnp.full((T, LW), SB_UNSEEN, F32), jnp.zeros((), F32))
        state = lax.cond(I > 0, lambda st: tiles(I, 2, st, True), lambda st: tiles(I, 1, st, True), state)
        rest = jnp.maximum(I - 1, 0)
        singles = jnp.where(rest > 0, (rest - 1) % U + 1, 0)
        _, state = lax.while_loop(lambda c: (c[0] < singles) & alive(c[1]),
                                  lambda c: (c[0] + 1, tiles(I - 2 - c[0], 1, c[1], False)), (jnp.int32(0), state))
        blocks = (rest - singles) // U
        _, state = lax.while_loop(lambda c: (c[0] < blocks) & alive(c[1]),
                                  lambda c: (c[0] + 1, tiles(I - 2 - singles - U * c[0], U, c[1], False)),
                                  (jnp.int32(0), state))
        _, acc, carr, _ = state
        o_ref[rows, :] = _sb_unstack_heads(acc)
        c_ref[rows, :] = carr
        return 0

    blk = lambda I_off: pl.BlockSpec((R * T, LW), lambda p, I: (I, I_off + p))
    full = lambda off: pl.BlockSpec((S, LW), lambda p, I: (0, off + p))
    const = lambda rows: pl.BlockSpec((rows, T), lambda p, I: (0, 0))
    o, carries, *got = pl.pallas_call(
        body, name=name, grid=grid,
        out_shape=[jax.ShapeDtypeStruct((S, W), F32), jax.ShapeDtypeStruct((S, W), F32)] + hosted.out_shapes,
        in_specs=[blk(0), full(npair), full(2 * npair), const(NH * T), const(T)] + hosted.in_specs,
        out_specs=[blk(0), blk(0)] + hosted.out_specs,
        scratch_shapes=hosted.scratch,
        compiler_params=_cparams(hosted.semantics(("parallel", "arbitrary")), vmem_mb=56),
    )(p_sb, p_sb, p_sb, _sb_causal(T, NH).astype(F32), _sb_triangle(T, True), *hosted.operands)
    return (o, carries) if comm is None else (o, carries, got)


def _sb_bwd(p_sb, do, carries, *, name, comm=None):
    S = p_sb.shape[0]
    W = p_sb.shape[1] // 3
    LW = min(LANES, W)
    NH = LW // HEAD_DIM
    npair = W // LW
    T = SB_TILE
    n_tiles = S // T
    scale = HEAD_DIM ** -0.5

    R = SB_STEP_TILES
    grid = (npair, n_tiles // R)
    hosted = _Hosted(comm, n_in=8, n_out=3, n_scratch=0)

    def body(*refs):
        refs = hosted.begin(refs, grid)
        dk_ref, dv_ref = refs[9], refs[10]
        step = pl.program_id(1)

        @pl.when(step == 0)
        def _():
            dk_ref[...] = jnp.zeros_like(dk_ref)
            dv_ref[...] = jnp.zeros_like(dv_ref)

        lax.fori_loop(0, R, lambda sub, _: query_tile(step * R + sub, sub, *refs), 0)
        hosted.end(grid)

    def query_tile(I, sub, q_ref, k_ref, v_ref, do_ref, c_ref, causal_ref, later_ref, earlier_ref,
                   dq_ref, dk_ref, dv_ref):
        rows = pl.ds(pl.multiple_of(sub * T, T), T)
        lane = lax.broadcasted_iota(jnp.int32, (T, LW), 1)
        causal = causal_ref[...]
        later_than = later_ref[...]
        earlier_than = earlier_ref[...]
        q2 = _sb_stack_heads(q_ref[rows, :], scale)
        do2 = _sb_stack_heads(do_ref[rows, :].astype(BF16))
        carr = c_ref[rows, :]
        tn_dims = (((0,), (0,)), ((), ()))

        def chain(J, diag):
            off = pl.multiple_of(J * T, T)
            k_j = k_ref[pl.ds(off, T), :]
            v_j = v_ref[pl.ds(off, T), :]
            log_beta, stay = _sb_scores(q2, k_j)
            if diag:
                stay = stay * causal
            lc = jnp.concatenate(
                [jnp.sum(jnp.where(lane == HEAD_DIM * h + J, carr, 0.0), axis=1, keepdims=True) for h in range(NH)],
                axis=0)
            w = jnp.exp((log_beta + lc) - jnp.dot(stay.astype(BF16), later_than, preferred_element_type=F32))
            if diag:
                w = w * causal
            dw = lax.dot_general(do2, v_j, (((1,), (1,)), ((), ())), preferred_element_type=F32)
            e = w * dw
            local = jnp.dot(e.astype(BF16), earlier_than, preferred_element_type=F32)
            return off, k_j, w, e, local, jnp.exp(log_beta), jnp.sum(e, axis=1, keepdims=True)

        def finish(ch, ec, dq_acc, diag):
            off, k_j, w, e, local, beta, _ = ch
            e_before = local + ec
            dz = e - beta * (e + e_before)
            if diag:
                dz = dz * causal
            dzb = dz.astype(BF16)
            dq_acc = dq_acc + jnp.dot(dzb, k_j, preferred_element_type=F32)
            dk_ref[pl.ds(off, T), :] += lax.dot_general(dzb, q2, tn_dims, preferred_element_type=F32)
            dv_ref[pl.ds(off, T), :] += lax.dot_general(w.astype(BF16), do2, tn_dims, preferred_element_type=F32)
            return dq_acc

        def tiles(J, count, state, diag):
            ec, dq_acc = state
            chains = [chain(J + u, diag and u == count - 1) for u in range(count)]
            for u, ch in enumerate(chains):
                dq_acc = finish(ch, ec, dq_acc, diag and u == count - 1)
                ec = ec + ch[6]
            return ec, dq_acc

        lane_row = lax.broadcasted_iota(jnp.int32, (1, LW), 1)
        reached = (jnp.max(carr, axis=0, keepdims=True) > 0.5 * SB_UNSEEN) & (lane_row < HEAD_DIM)
        first = jnp.min(jnp.where(reached, lane_row.astype(F32), float(n_tiles))).astype(jnp.int32)
        U = SB_UNROLL
        count = I - first
        rest = jnp.maximum(count - 1, 0)
        state = (jnp.zeros((NH * T, 1), F32), jnp.zeros((NH * T, LW), F32))
        state = lax.fori_loop(0, rest // U, lambda jj, st: tiles(first + U * jj, U, st, False), state)
        state = lax.fori_loop(0, rest % U, lambda r, st: tiles(I - 1 - rest % U + r, 1, st, False), state)
        _, dq_acc = lax.cond(count > 0, lambda st: tiles(I - 1, 2, st, True), lambda st: tiles(I, 1, st, True), state)
        dq_ref[rows, :] = _sb_unstack_heads(dq_acc) * scale
        return 0

    blk = lambda src_off: pl.BlockSpec((R * T, LW), lambda p, I: (I, src_off + p))
    full = lambda off: pl.BlockSpec((S, LW), lambda p, I: (0, off + p))
    const = lambda rows: pl.BlockSpec((rows, T), lambda p, I: (0, 0))
    dq, dk, dv, *got = pl.pallas_call(
        body, name=name, grid=grid,
        out_shape=[jax.ShapeDtypeStruct((S, W), F32)] * 3 + hosted.out_shapes,
        in_specs=[blk(0), full(npair), full(2 * npair), blk(0), blk(0), const(NH * T), const(T), const(T)]
        + hosted.in_specs,
        out_specs=[blk(0), full(0), full(0)] + hosted.out_specs,
        scratch_shapes=hosted.scratch,
        compiler_params=_cparams(hosted.semantics(("parallel", "arbitrary")), vmem_mb=56),
    )(p_sb, p_sb, p_sb, do, carries, _sb_causal(T, NH).astype(F32), _sb_triangle(T, True), _sb_triangle(T, False),
      *hosted.operands)
    return (dq, dk, dv) if comm is None else (dq, dk, dv, got)


def _dil_blocks(b, body_fn):
    for pi, (window, dil) in enumerate(DILATED_PATTERNS):
        assert window // dil == DIL_BLOCK
        nblk = DIL_SUPER // (DIL_BLOCK * dil)
        assert (dil * nblk) % DIL_UNROLL == 0

        def group(g, _, pi=pi, dil=dil, nblk=nblk):
            for u in range(DIL_UNROLL):
                t = g * DIL_UNROLL + u
                n = t % nblk
                body_fn(pi, dil, t // nblk, n, b * nblk + n)
            return 0

        lax.fori_loop(0, dil * nblk // DIL_UNROLL, group, 0)


def _dil_rows(start, size, dil):
    if dil == 1:
        return pl.ds(pl.multiple_of(start, DIL_BLOCK), size)
    return pl.ds(start, size, stride=dil)


def _dil_fill_bias(bias_ref):
    row = lax.broadcasted_iota(jnp.int32, (2 * DIL_BLOCK, 2 * DIL_BLOCK), 0)
    kk = lax.broadcasted_iota(jnp.int32, (2 * DIL_BLOCK, 2 * DIL_BLOCK), 1)
    qi = jnp.where(row >= DIL_BLOCK, row - DIL_BLOCK, row)
    for s in range(2):
        dist = s * DIL_BLOCK + qi - kk
        bias_ref[s] = jnp.where((dist >= 0) & (dist <= DIL_BLOCK), 0.0, NEG_BIG)


def _dl_fwd(p_dl, *, name):
    S, W = p_dl.shape[0], p_dl.shape[1] // 3
    npair = W // LANES
    nsuper = S // DIL_SUPER
    assert S % DIL_SUPER == 0 and S // max(d for _, d in DILATED_PATTERNS) >= 2 * DIL_BLOCK
    scale = HEAD_DIM ** -0.5
    npat = len(DILATED_PATTERNS)

    def body(q_ref, k_ref, v_ref, o_ref, l_ref, bias_ref, *pattern_refs):
        op_refs, lp_refs = pattern_refs[:npat], pattern_refs[npat:]
        b = pl.program_id(1)
        masks = _head_masks((DIL_BLOCK, LANES))
        pl.when(b == 0)(lambda: _dil_fill_bias(bias_ref))

        def block(pi, dil, c, n, gn):
            ws = jnp.maximum(gn - 1, 0)
            qrows = n * (DIL_BLOCK * dil) + c
            krows = ws * (DIL_BLOCK * dil) + c
            q_idx = _dil_rows(qrows, DIL_BLOCK, dil)
            k_idx = _dil_rows(krows, 2 * DIL_BLOCK, dil)
            qb = q_ref[q_idx, :]
            kb = k_ref[k_idx, :].astype(BF16)
            vb = v_ref[k_idx, :].astype(BF16)
            q2 = _sb_stack_heads(qb.astype(BF16), scale)
            z = lax.dot_general(q2, kb, (((1,), (1,)), ((), ())), preferred_element_type=F32) + bias_ref[gn - ws]
            m = jnp.max(z, axis=1, keepdims=True)
            p = jnp.exp(z - m)
            den = jnp.sum(p, axis=1, keepdims=True)
            acc = jnp.dot(p.astype(BF16), vb, preferred_element_type=F32)
            lse = m + jnp.log(den)
            op_refs[pi][q_idx, :] = _sb_unstack_heads(acc / den)
            lp_refs[pi][q_idx, :] = jnp.where(masks[0], lse[:DIL_BLOCK], lse[DIL_BLOCK:])

        _dil_blocks(b, block)
        lses = [r[...] for r in lp_refs]
        top = functools.reduce(jnp.maximum, lses)
        ws_ = [jnp.exp(l - top) for l in lses]
        den = functools.reduce(jnp.add, ws_)
        num = functools.reduce(jnp.add, [w * r[...] for r, w in zip(op_refs, ws_)])
        o_ref[...] = num / den
        l_ref[...] = top + jnp.log(den)

    blk = pl.BlockSpec((DIL_SUPER, LANES), lambda p, b: (b, p))
    full = lambda off: pl.BlockSpec((S, LANES), lambda p, b: (0, off + p))
    return pl.pallas_call(
        body, name=name, grid=(npair, nsuper),
        out_shape=[jax.ShapeDtypeStruct((S, W), F32)] * 2,
        in_specs=[blk, full(npair), full(2 * npair)], out_specs=[blk, blk],
        scratch_shapes=[pltpu.VMEM((2, 2 * DIL_BLOCK, 2 * DIL_BLOCK), F32)]
        + [pltpu.VMEM((DIL_SUPER, LANES), F32)] * (2 * npat),
        compiler_params=_cparams(("arbitrary", "arbitrary")),
    )(p_dl, p_dl, p_dl)


def _dl_bwd(p_dl, o, lse, do, *, name):
    S, W = p_dl.shape[0], p_dl.shape[1] // 3
    npair = W // LANES
    nsuper = S // DIL_SUPER
    scale = HEAD_DIM ** -0.5

    def body(q_ref, k_ref, v_ref, o_ref, l_ref, do_ref, dq_ref, dk_ref, dv_ref, delta_ref, bias_ref):
        b = pl.program_id(1)

        @pl.when(b == 0)
        def _():
            dk_ref[...] = jnp.zeros_like(dk_ref)
            dv_ref[...] = jnp.zeros_like(dv_ref)
            _dil_fill_bias(bias_ref)

        dq_ref[...] = jnp.zeros_like(dq_ref)
        prod = do_ref[...] * o_ref[...]
        delta = jnp.zeros_like(prod)
        for hm in _head_masks(prod.shape):
            delta = jnp.where(hm, jnp.sum(jnp.where(hm, prod, 0.0), axis=1, keepdims=True), delta)
        delta_ref[...] = delta

        def block(pi, dil, c, n, gn):
            ws = jnp.maximum(gn - 1, 0)
            qrows = n * (DIL_BLOCK * dil) + c
            krows = ws * (DIL_BLOCK * dil) + c
            q_idx = _dil_rows(qrows, DIL_BLOCK, dil)
            k_idx = _dil_rows(krows, 2 * DIL_BLOCK, dil)
            qb = q_ref[q_idx, :]
            dob = do_ref[q_idx, :]
            lb = l_ref[q_idx, :]
            db = delta_ref[q_idx, :]
            kb = k_ref[k_idx, :].astype(BF16)
            vb = v_ref[k_idx, :].astype(BF16)
            q2 = _sb_stack_heads(qb.astype(BF16), scale)
            do2 = _sb_stack_heads(dob.astype(BF16))
            lse2 = jnp.concatenate([lb[:, HEAD_DIM * h:HEAD_DIM * h + 1] for h in range(2)], axis=0)
            delta2 = jnp.concatenate([db[:, HEAD_DIM * h:HEAD_DIM * h + 1] for h in range(2)], axis=0)
            z = lax.dot_general(q2, kb, (((1,), (1,)), ((), ())), preferred_element_type=F32)
            p = jnp.exp((z + bias_ref[gn - ws]) - lse2)
            dp = lax.dot_general(do2, vb, (((1,), (1,)), ((), ())), preferred_element_type=F32)
            dzb = (p * (dp - delta2)).astype(BF16)
            tn_dims = (((0,), (0,)), ((), ()))
            dq_blk = _sb_unstack_heads(jnp.dot(dzb, kb, preferred_element_type=F32)) * scale
            dk_blk = lax.dot_general(dzb, q2, tn_dims, preferred_element_type=F32)
            dv_blk = lax.dot_general(p.astype(BF16), do2, tn_dims, preferred_element_type=F32)
            dq_ref[q_idx, :] = dq_ref[q_idx, :] + dq_blk
            dk_ref[k_idx, :] = dk_ref[k_idx, :] + dk_blk
            dv_ref[k_idx, :] = dv_ref[k_idx, :] + dv_blk

        _dil_blocks(b, block)

    blk = pl.BlockSpec((DIL_SUPER, LANES), lambda p, b: (b, p))
    full = lambda off: pl.BlockSpec((S, LANES), lambda p, b: (0, off + p))
    return pl.pallas_call(
        body, name=name, grid=(npair, nsuper),
        out_shape=[jax.ShapeDtypeStruct((S, W), F32)] * 3,
        in_specs=[blk, full(npair), full(2 * npair), blk, blk, blk], out_specs=[blk, full(0), full(0)],
        scratch_shapes=[pltpu.VMEM((DIL_SUPER, LANES), F32), pltpu.VMEM((2, 2 * DIL_BLOCK, 2 * DIL_BLOCK), F32)],
        compiler_params=_cparams(("arbitrary", "arbitrary")),
    )(p_dl, p_dl, p_dl, o, lse, do)


class _NoExchange:
    def gather(self, family):
        return None

    def gathered(self, family, got, weights):
        pass

    def send(self, family, grads):
        return None

    def received(self, family, got):
        pass


def _local_step(x, target, gains, weights, exchanges=None):
    S, D = x.shape
    ex = exchanges or _NoExchange()
    weights = dict(weights)
    d_sb = gains["sb_out_norm"].shape[1]
    d_dl = gains["dil_out_norm"].shape[1]
    cos_t, sin_t = _rope_tables(S)

    riders = ("ffn1_w_up", "ffn1_w_down", "mixer") if exchanges else (None, None, None)
    (x1, h2), saved1 = _ffn_fwd(x, gains["ffn1_norm"], weights, tag="ffn1", ex=ex, riders=riders,
                                first_rider="ffn1_w_gate" if exchanges else None,
                                next_gain=gains["mix_norm"])
    w_in = weights["w_in"]
    w_out = weights["w_out"]
    p_sb = _mm(h2, w_in, b_cols=(0, 3 * d_sb), outs=(BF16,), name="proj_sb")

    def rope_qk(acc, _, cos, sin):
        return jnp.concatenate([_rotate(acc[:, :2 * d_dl], cos, sin, 1.0), acc[:, 2 * d_dl:]], axis=1)

    p_dl = _mm(h2, w_in, b_cols=(3 * d_sb, 3 * d_dl), lanes=(cos_t, sin_t), epilogue=rope_qk, name="proj_dl_rope")
    plan = ex.gather("ffn2" if exchanges else None)
    o_sb, carries, *got = _sb_fwd(p_sb, name="sb_fwd", comm=plan)
    ex.gathered("ffn2", got[0] if got else None, weights)
    o_dl, lse_dl = _dl_fwd(p_dl, name="dl_fwd")
    merged = _rms_fwd([o_sb, o_dl], [gains["sb_out_norm"], gains["dil_out_norm"]], name="out_norm")
    x2, h3 = _mm(merged, w_out, res=x1, rows=(gains["ffn2_norm"],), outs=(F32, BF16),
                 epilogue=_residual_then_norm(1.0), name="out_proj_norm")
    (dx3, d_final, loss_wide), saved2 = _ffn_fwd(x2, gains["ffn2_norm"], weights, tag="ffn2", ex=ex, h=h3,
                                                 head=(gains["final_norm"], target))
    loss_row = loss_wide[:, :LANES]

    dx2, d_ffn2_norm, dwg2, dwu2, dwd2 = _ffn_bwd(dx3, x2, gains["ffn2_norm"], weights, saved2, tag="ffn2", ex=ex)
    d_w_out = _mm(merged, dx2, ta=True, outs=(GRAD_WIRE,), name="d_w_out")
    do_sb, d_sb_norm = _mm(dx2, w_out, tb=True, b_cols=(0, d_sb), extras=(o_sb,), rows=(gains["sb_out_norm"],),
                           row_sums=1, epilogue=_rms_bwd_epilogue, name="d_merged_sb")
    do_dl, d_dl_norm = _mm(dx2, w_out, tb=True, b_cols=(d_sb, d_dl), extras=(o_dl,), rows=(gains["dil_out_norm"],),
                           row_sums=1, epilogue=_rms_bwd_epilogue, name="d_merged_dl")
    plan = ex.send("ffn2", dict(ffn2_w_gate=dwg2, ffn2_w_up=dwu2, ffn2_w_down=dwd2))
    dq_sb, dk_sb, dv_sb, *got = _sb_bwd(p_sb, do_sb, carries, name="sb_bwd", comm=plan)
    ex.received("ffn2", got[0] if got else None)
    dq_dl, dk_dl, dv_dl = _dl_bwd(p_dl, o_dl, lse_dl, do_dl, name="dl_bwd")
    d_proj = _join_d_proj([dq_sb, dk_sb, dv_sb, dq_dl, dk_dl, dv_dl], (3, 4), cos_t, sin_t, name="d_proj")
    d_w_in = _mm(h2, d_proj, ta=True, outs=(GRAD_WIRE,), name="d_w_in")
    dx1, d_mix_norm = _mm(d_proj, w_in, tb=True, extras=(x1, dx2), rows=(gains["mix_norm"],), row_sums=1,
                          epilogue=_rms_bwd_epilogue, name="dh_mix_norm_bwd")
    dx, d_ffn1_norm, dwg1, dwu1, dwd1 = _ffn_bwd(
        dx1, x, gains["ffn1_norm"], weights, saved1, tag="ffn1", ex=ex,
        rider=("mixer", dict(w_in=d_w_in, w_out=d_w_out)), spread=True)
    gain_grads = dict(ffn1_norm=d_ffn1_norm, mix_norm=d_mix_norm, sb_out_norm=d_sb_norm, dil_out_norm=d_dl_norm,
                      ffn2_norm=d_ffn2_norm, final_norm=d_final)
    weight_grads = dict(ffn1_w_gate=dwg1, ffn1_w_up=dwu1, ffn1_w_down=dwd1, w_in=d_w_in, w_out=d_w_out,
                        ffn2_w_gate=dwg2, ffn2_w_up=dwu2, ffn2_w_down=dwd2)
    return loss_row, dx, gain_grads, weight_grads


def _mesh_position():
    return lax.axis_index("x"), lax.axis_index("y"), lax.axis_index("c")


def _flip(coord, bit):
    return 1 - coord if bit else coord


RELATIONS = [(rx, ry, rc) for rx in (0, 1) for ry in (0, 1) for rc in (0, 1)][1:]


class _GatherPlan:
    def __init__(self, shards):
        n = len(shards)
        self.operands = list(shards)
        self.out_shapes = [jax.ShapeDtypeStruct((N_DEV,) + s.shape, s.dtype) for s in shards]
        self.scratch = [pltpu.SemaphoreType.DMA((n, 7)), pltpu.SemaphoreType.DMA((n, 7)),
                        pltpu.SemaphoreType.DMA((n,))]

    def _copies(self, in_refs, out_refs, sems):
        send_sems, recv_sems, local_sems = sems
        x, y, c = _mesh_position()
        me, sibling = (x, y, c), (x, y, 1 - c)
        chips = [(1 - x, y), (x, 1 - y), (1 - x, 1 - y)]
        plans = []
        for t, (x_ref, out_ref) in enumerate(zip(in_refs, out_refs)):
            def slot(px, py, pc, out_ref=out_ref):
                return out_ref.at[4 * px + 2 * py + pc]

            def copy(k, block, to, src=None, t=t, slot=slot):
                return pltpu.make_async_remote_copy(
                    src_ref=slot(*block) if src is None else src, dst_ref=slot(*block),
                    send_sem=send_sems.at[t, k], recv_sem=recv_sems.at[t, k],
                    device_id=to, device_id_type=pl.DeviceIdType.MESH)

            plans.append(dict(
                mine=pltpu.make_async_copy(x_ref, slot(*me), local_sems.at[t]),
                first=[copy(0, me, sibling, src=x_ref)]
                + [copy(1 + j, me, (*chip, c), src=x_ref) for j, chip in enumerate(chips)],
                over_ici=[copy(1 + j, (*chip, c), me) for j, chip in enumerate(chips)],
                passed=[copy(4 + j, (*chip, c), sibling) for j, chip in enumerate(chips)],
                from_sibling=[copy(0, sibling, me)] + [copy(4 + j, (*chip, 1 - c), me) for j, chip in enumerate(chips)]))
        return plans

    def start(self, in_refs, out_refs, sems):
        for p in self._copies(in_refs, out_refs, sems):
            p["mine"].start()
            for cp in p["first"]:
                cp.start()

    def finish(self, in_refs, out_refs, sems):
        plans = self._copies(in_refs, out_refs, sems)
        for p in plans:
            for arrived, onward in zip(p["over_ici"], p["passed"]):
                arrived.wait_recv()
                onward.start()
        for p in plans:
            for cp in p["from_sibling"]:
                cp.wait_recv()
            for cp in p["first"] + p["passed"]:
                cp.wait_send()
            p["mine"].wait()


class _Hosted:
    def __init__(self, plan, n_in, n_out, n_scratch):
        self.plan, self.n_in, self.n_out, self.n_scratch = plan, n_in, n_out, n_scratch
        self.operands = list(plan.operands) if plan else []
        self.out_shapes = list(plan.out_shapes) if plan else []
        self.scratch = list(plan.scratch) if plan else []
        self.in_specs = [pl.BlockSpec(memory_space=pl.ANY)] * len(self.operands)
        self.out_specs = [pl.BlockSpec(memory_space=pl.ANY)] * len(self.out_shapes)

    def semantics(self, sem):
        return sem if self.plan is None else ("arbitrary",) * len(sem)

    def _at(self, grid, last):
        hit = None
        for d, n in enumerate(grid):
            here = pl.program_id(d) == (n - 1 if last else 0)
            hit = here if hit is None else hit & here
        return hit

    def begin(self, refs, grid):
        if self.plan is None:
            return refs
        k_in, k_out = len(self.operands), len(self.out_shapes)
        ins, rest = refs[:self.n_in], refs[self.n_in:]
        c_in, rest = rest[:k_in], rest[k_in:]
        outs, rest = rest[:self.n_out], rest[self.n_out:]
        c_out, rest = rest[:k_out], rest[k_out:]
        scratch, sems = rest[:self.n_scratch], rest[self.n_scratch:]
        self._args = (c_in, c_out, sems)
        pl.when(self._at(grid, False))(lambda: self.plan.start(*self._args))
        return tuple(ins) + tuple(outs) + tuple(scratch)

    def end(self, grid):
        if self.plan is not None:
            pl.when(self._at(grid, True))(lambda: self.plan.finish(*self._args))


class _ExchangePlan:
    def __init__(self, packs):
        n = len(packs)
        self.operands = list(packs)
        self.out_shapes = [jax.ShapeDtypeStruct(p.shape, p.dtype) for p in packs]
        self.scratch = [pltpu.SemaphoreType.DMA((n, 7)), pltpu.SemaphoreType.DMA((n, 7)),
                        pltpu.SemaphoreType.DMA((n,))]

    def _copies(self, in_refs, out_refs, sems):
        send_sems, recv_sems, local_sems = sems
        x, y, c = _mesh_position()
        me = 4 * x + 2 * y + c
        copies = [pltpu.make_async_copy(i.at[me], o.at[me], local_sems.at[t])
                  for t, (i, o) in enumerate(zip(in_refs, out_refs))]
        for r, (rx, ry, rc) in enumerate(RELATIONS):
            px, py, pc = _flip(x, rx), _flip(y, ry), _flip(c, rc)
            peer = 4 * px + 2 * py + pc
            copies += [pltpu.make_async_remote_copy(
                src_ref=i.at[peer], dst_ref=o.at[me], send_sem=send_sems.at[t, r], recv_sem=recv_sems.at[t, r],
                device_id=(px, py, pc), device_id_type=pl.DeviceIdType.MESH)
                for t, (i, o) in enumerate(zip(in_refs, out_refs))]
        return copies

    def start(self, in_refs, out_refs, sems):
        for cp in self._copies(in_refs, out_refs, sems):
            cp.start()

    def finish(self, in_refs, out_refs, sems):
        for cp in self._copies(in_refs, out_refs, sems):
            cp.wait()


def _all_reduce_rows(v, *, name):
    R, C = v.shape

    def body(v_ref, out_ref, buf, send_sems, recv_sems):
        x, y, c = _mesh_position()
        me = 4 * x + 2 * y + c
        buf[me] = v_ref[...]
        copies = []
        for r, (rx, ry, rc) in enumerate(RELATIONS):
            cp = pltpu.make_async_remote_copy(
                src_ref=v_ref, dst_ref=buf.at[me], send_sem=send_sems.at[r], recv_sem=recv_sems.at[r],
                device_id=(_flip(x, rx), _flip(y, ry), _flip(c, rc)), device_id_type=pl.DeviceIdType.MESH)
            cp.start()
            copies.append(cp)
        for cp in copies:
            cp.wait()
        total = buf[0]
        for s in range(1, N_DEV):
            total = total + buf[s]
        out_ref[...] = total

    return pl.pallas_call(
        body, name=name,
        out_shape=jax.ShapeDtypeStruct((R, C), F32),
        in_specs=[pl.BlockSpec(memory_space=pltpu.VMEM)],
        out_specs=pl.BlockSpec(memory_space=pltpu.VMEM),
        scratch_shapes=[pltpu.VMEM((N_DEV, R, C), F32), pltpu.SemaphoreType.DMA((7,)), pltpu.SemaphoreType.DMA((7,))],
    )(v)


def _adamw(w, g, m, v, *, name):
    R, C = w.shape
    slots = g.ndim == 3
    tr = _pick(R, (256, 128, 64, 32, 16) if slots else (256, 128, 64, 32, 16, 8))

    def body(w_ref, g_ref, m_ref, v_ref, g_out, d_ref, nm_ref, nv_ref):
        if slots:
            g = g_ref[0].astype(F32)
            for s in range(1, N_DEV):
                g = g + g_ref[s].astype(F32)
        else:
            g = g_ref[...]
        g_out[...] = g
        m_new = ADAM_B1 * m_ref[...] + (1.0 - ADAM_B1) * g
        v_new = ADAM_B2 * v_ref[...] + (1.0 - ADAM_B2) * (g * g)
        m_hat = m_new / (1.0 - ADAM_B1 ** ADAM_STEP)
        v_hat = v_new / (1.0 - ADAM_B2 ** ADAM_STEP)
        d_ref[...] = -ADAM_LR * (m_hat / (jnp.sqrt(v_hat) + ADAM_EPS) + ADAM_WD * w_ref[...])
        nm_ref[...] = m_new
        nv_ref[...] = v_new

    spec = pl.BlockSpec((tr, C), lambda i: (i, 0))
    g_spec = pl.BlockSpec((N_DEV, tr, C), lambda i: (0, i, 0)) if slots else spec
    return pl.pallas_call(
        body, name=name, grid=(R // tr,),
        out_shape=[jax.ShapeDtypeStruct((R, C), F32)] * 4,
        in_specs=[spec, g_spec, spec, spec], out_specs=[spec] * 4,
        compiler_params=_cparams(("parallel",)),
    )(w, g, m, v)


WEIGHT_NAMES = ["ffn1_norm", "ffn1_w_gate", "ffn1_w_up", "ffn1_w_down", "mix_norm", "w_in", "sb_out_norm",
                "dil_out_norm", "w_out", "ffn2_norm", "ffn2_w_gate", "ffn2_w_up", "ffn2_w_down", "final_norm"]
GAIN_NAMES = ["ffn1_norm", "mix_norm", "sb_out_norm", "dil_out_norm", "ffn2_norm", "final_norm"]
COL_SHARDED = ["ffn1_w_gate", "ffn1_w_up", "ffn2_w_gate", "ffn2_w_up", "w_in"]
ROW_SHARDED = ["ffn1_w_down", "ffn2_w_down", "w_out"]
GROUPS = {"mixer": (["w_in"], ["w_out"]),
          "ffn2": (["ffn2_w_gate", "ffn2_w_up"], ["ffn2_w_down"])}
for _ffn in ("ffn1", "ffn2"):
    GROUPS.update({f"{_ffn}_w_gate": ([f"{_ffn}_w_gate"], []), f"{_ffn}_w_up": ([f"{_ffn}_w_up"], []),
                   f"{_ffn}_w_down": ([], [f"{_ffn}_w_down"])})


class _Exchanges:
    def __init__(self, params):
        self.params = params
        self.grads = {}

    def gather(self, group):
        if group is None:
            return None
        cols, rows = GROUPS[group]
        return _GatherPlan([self.params[n].astype(BF16) for n in cols + rows])

    def gathered(self, group, got, weights):
        if group is None:
            return
        cols, rows = GROUPS[group]
        for n, blocks in zip(cols + rows, got):
            if n in cols:
                weights[n] = jnp.transpose(blocks, (1, 0, 2)).reshape(blocks.shape[1], N_DEV * blocks.shape[2])
            else:
                weights[n] = blocks.reshape(N_DEV * blocks.shape[1], blocks.shape[2])

    def send(self, group, grads):
        if group is None:
            return None
        cols, rows = GROUPS[group]
        packs = [jnp.transpose(grads[n].reshape(grads[n].shape[0], N_DEV, self.params[n].shape[1]), (1, 0, 2))
                 for n in cols]
        packs += [grads[n].reshape(N_DEV, self.params[n].shape[0], grads[n].shape[1]) for n in rows]
        return _ExchangePlan([p.astype(GRAD_WIRE) for p in packs])

    def received(self, group, got):
        if group is None:
            return
        cols, rows = GROUPS[group]
        for n, slots in zip(cols + rows, got):
            self.grads[n] = slots


def _step(x, target, params, moments_m, moments_v):
    ex = _Exchanges(params)
    weights = {}
    gains = {n: params[n] for n in GAIN_NAMES}
    loss_row, grad_x, gain_grads, _ = _local_step(x, target, gains, weights, ex)
    grads = ex.grads

    rows = [gain_grads[n].reshape(-1, LANES) for n in GAIN_NAMES] + [loss_row]
    small = jnp.concatenate(rows, axis=0)
    pad = (-small.shape[0]) % 8
    small = jnp.pad(small, ((0, pad), (0, 0)))
    small = _all_reduce_rows(small, name="reduce_gains_loss")
    off = 0
    for n in GAIN_NAMES:
        r = gain_grads[n].shape[1] // LANES
        grads[n] = small[off:off + r].reshape(1, -1)
        off += r
    loss = small[off, 0]

    delta, new_m, new_v = {}, {}, {}
    for n in WEIGHT_NAMES:
        grads[n], delta[n], new_m[n], new_v[n] = _adamw(params[n], grads[n], moments_m[n], moments_v[n],
                                                        name=f"adamw_{n}")
    return loss, grad_x, grads, delta, new_m, new_v


def kernel(x, ffn1_norm, ffn1_w_gate, ffn1_w_up, ffn1_w_down, mix_norm, w_in, sb_out_norm, dil_out_norm, w_out, ffn2_norm, ffn2_w_gate, ffn2_w_up, ffn2_w_down, final_norm, loss_target, m_ffn1_norm, m_ffn1_w_gate, m_ffn1_w_up, m_ffn1_w_down, m_mix_norm, m_w_in, m_sb_out_norm, m_dil_out_norm, m_w_out, m_ffn2_norm, m_ffn2_w_gate, m_ffn2_w_up, m_ffn2_w_down, m_final_norm, v_ffn1_norm, v_ffn1_w_gate, v_ffn1_w_up, v_ffn1_w_down, v_mix_norm, v_w_in, v_sb_out_norm, v_dil_out_norm, v_w_out, v_ffn2_norm, v_ffn2_w_gate, v_ffn2_w_up, v_ffn2_w_down, v_final_norm):
    given = dict(locals())
    shapes = {n: given[n].shape for n in WEIGHT_NAMES}

    def as2d(a):
        return a.reshape(1, -1) if a.ndim == 1 else a.reshape(a.shape[-2], a.shape[-1])

    params = {n: as2d(given[n]) for n in WEIGHT_NAMES}
    moments_m = {n: as2d(given["m_" + n]) for n in WEIGHT_NAMES}
    moments_v = {n: as2d(given["v_" + n]) for n in WEIGHT_NAMES}
    loss, grad_x, grads, delta, new_m, new_v = _step(x[0], loss_target[0], params, moments_m, moments_v)
    back = lambda d: [d[n].reshape(shapes[n]) for n in WEIGHT_NAMES]
    return (loss, grad_x[None], *back(grads), *back(delta), *back(new_m), *back(new_v))
```

```python
import functools

import jax
import jax.numpy as jnp
from jax import lax
from jax.experimental import pallas as pl
from jax.experimental.pallas import tpu as pltpu

F32 = jnp.float32
BF16 = jnp.bfloat16
GRAD_WIRE = jnp.bfloat16

N_DEV = 8
HEAD_DIM = 64
LANES = 128
DILATED_PATTERNS = ((128, 1), (512, 4), (2048, 16))
DIL_BLOCK = 128
DIL_SUPER = 2048
DIL_UNROLL = 16
SB_TILE = 256
SB_LANES = 128
SB_UNROLL = 4
SB_STEP_TILES = 4
SB_DEAD = 90.0
SB_UNSEEN = -1e30
ROPE_THETA = 10000.0
RMS_EPS = 1e-6
HALF_STEP = 0.5
ADAM_LR = 0.001
ADAM_B1 = 0.9
ADAM_B2 = 0.999
ADAM_EPS = 1e-08
ADAM_WD = 0.01
ADAM_STEP = 10
NEG_BIG = -1e30
VMEM_CAP_MB = 60


def _pick(n, prefs):
    for p in prefs:
        if n % p == 0:
            return p
    return n


MM_MAX_TILE = 1536
MM_WHOLE = 3072


def _largest_tile(n, cap):
    if n <= cap:
        return n
    for t in range(cap - cap % LANES, 0, -LANES):
        if n % t == 0:
            return t
    return n


def _cparams(sem=None, vmem_mb=48):
    return pltpu.CompilerParams(dimension_semantics=sem, vmem_limit_bytes=min(vmem_mb, VMEM_CAP_MB) * 1024 * 1024)


def _nbytes(shape, dtype):
    n = 1
    for s in shape:
        n *= s
    return n * jnp.dtype(dtype).itemsize


def _mm(a, b, *, name, ta=False, tb=False, outs=(F32,), res=None, alpha=1.0, extras=(), epilogue=None,
        tm=None, tn=None, tk=None, comm=None, rows=(), lanes=(), row_sums=0, b_cols=None):
    if ta:
        K, M = a.shape
    else:
        M, K = a.shape
    if tb:
        N, Kb = b.shape
    else:
        Kb, N = b.shape
    col0 = 0
    if b_cols is not None:
        col0, N = b_cols
    assert K == Kb, (a.shape, b.shape, ta, tb)
    tn = tn or (N if (not ta and K <= MM_WHOLE and N <= MM_WHOLE) else _largest_tile(N, MM_MAX_TILE))
    wide = tn > MM_MAX_TILE and (len(extras) + len(outs) > 3 or a.dtype == F32)
    tm = tm or (_largest_tile(M, MM_MAX_TILE) if ta else _pick(M, (256, 128) if wide else (512, 256, 128)))
    tk = tk or (K if K <= MM_WHOLE else _pick(K, (2048, 1024, 512, 256, 128)))
    nk = K // tk
    a_spec = pl.BlockSpec((tk, tm), lambda i, j, k: (k, i)) if ta else pl.BlockSpec((tm, tk), lambda i, j, k: (i, k))
    assert col0 % tn == 0
    b_spec = (pl.BlockSpec((tn, tk), lambda i, j, k: (j + col0 // tn, k)) if tb
              else pl.BlockSpec((tk, tn), lambda i, j, k: (k, j + col0 // tn)))
    mn_spec = pl.BlockSpec((tm, tn), lambda i, j, k: (i, j))
    dims = (((0 if ta else 1,), (1 if tb else 0,)), ((), ()))
    row_spec = pl.BlockSpec((1, tn), lambda i, j, k: (0, j))
    lane_spec = pl.BlockSpec((tm, LANES), lambda i, j, k: (i, 0))
    n_extra = len(extras) + (1 if res is not None else 0) + len(rows) + len(lanes)
    n_mn = len(outs)
    n_out = n_mn + row_sums
    assert row_sums == 0 or tn == N
    grid = (M // tm, N // tn, nk)
    hosted = _Hosted(comm, n_in=2 + n_extra, n_out=n_out, n_scratch=1 if nk > 1 else 0)

    def body(*refs):
        a_ref, b_ref = refs[0], refs[1]
        in_refs = refs[2:2 + n_extra]
        refs = hosted.begin(refs, grid)
        out_refs = refs[2 + n_extra:2 + n_extra + n_out]
        prod = lax.dot_general(a_ref[...].astype(BF16), b_ref[...].astype(BF16), dims, preferred_element_type=F32)

        def finish(acc):
            blocks = [r[...] for r in in_refs]
            if res is not None:
                r_blk, blocks = blocks[0], blocks[1:]
            else:
                r_blk = None
            if epilogue is None:
                val = acc * alpha
                if r_blk is not None:
                    val = val + r_blk
                vals = (val,)
            else:
                vals = epilogue(acc, r_blk, *blocks)
                vals = vals if isinstance(vals, (tuple, list)) else (vals,)
            for o_ref, v in zip(out_refs[:n_mn], vals[:n_mn]):
                o_ref[...] = v.astype(o_ref.dtype)
            first_rows = pl.program_id(0) == 0
            for o_ref, part in zip(out_refs[n_mn:], vals[n_mn:]):
                @pl.when(first_rows)
                def _(o_ref=o_ref, part=part):
                    o_ref[...] = part

                @pl.when(jnp.logical_not(first_rows))
                def _(o_ref=o_ref, part=part):
                    o_ref[...] += part

        if nk == 1:
            finish(prod)
        else:
            acc_ref = refs[2 + n_extra + n_out]
            k = pl.program_id(2)

            @pl.when(k == 0)
            def _():
                acc_ref[...] = prod

            @pl.when(k > 0)
            def _():
                acc_ref[...] += prod

            @pl.when(k == nk - 1)
            def _():
                finish(acc_ref[...])

        hosted.end(grid)

    mn_operands = ([res] if res is not None else []) + list(extras)
    operands = [a, b] + mn_operands + list(rows) + list(lanes)
    in_specs = [a_spec, b_spec] + [mn_spec] * len(mn_operands) + [row_spec] * len(rows) + [lane_spec] * len(lanes)
    est = 2 * (_nbytes((tm, tk), a.dtype) + _nbytes((tk, tn), b.dtype))
    est += 2 * sum(_nbytes((tm, tn), o.dtype) for o in mn_operands)
    est += 2 * sum(_nbytes((tm, tn), d) for d in outs) + 2 * _nbytes((tm, tn), F32)
    semantics = ("parallel", "parallel", "arbitrary") if row_sums == 0 else ("arbitrary",) * 3
    result = pl.pallas_call(
        body, name=name, grid=grid,
        out_shape=[jax.ShapeDtypeStruct((M, N), d) for d in outs]
        + [jax.ShapeDtypeStruct((1, N), F32)] * row_sums + hosted.out_shapes,
        in_specs=in_specs + hosted.in_specs,
        out_specs=[mn_spec] * n_mn + [row_spec] * row_sums + hosted.out_specs,
        scratch_shapes=([pltpu.VMEM((tm, tn), F32)] if nk > 1 else []) + hosted.scratch,
        compiler_params=_cparams(hosted.semantics(semantics), vmem_mb=max(32, 2 * est // (1024 * 1024))),
    )(*operands, *hosted.operands)
    own, got = result[:n_out], list(result[n_out:])
    own = own[0] if n_out == 1 else own
    return own if comm is None else (own, got)


def _rms_hat(x):
    r = lax.rsqrt(jnp.mean(x * x, axis=-1, keepdims=True) + RMS_EPS)
    return x * r, r


def _rms_fwd(xs, gains, *, name, comm=None):
    S = xs[0].shape[0]
    widths = [x.shape[1] for x in xs]
    tm = _pick(S, (512, 256, 128))
    n = len(xs)
    grid = (S // tm,)
    hosted = _Hosted(comm, n_in=2 * n, n_out=1, n_scratch=0)

    def body(*refs):
        refs = hosted.begin(refs, grid)
        o_ref = refs[2 * n]
        off = 0
        for i in range(n):
            xh, _ = _rms_hat(refs[i][...])
            o_ref[:, off:off + widths[i]] = (xh * refs[n + i][...]).astype(o_ref.dtype)
            off += widths[i]
        hosted.end(grid)

    out, *got = pl.pallas_call(
        body, name=name, grid=grid,
        out_shape=[jax.ShapeDtypeStruct((S, sum(widths)), BF16)] + hosted.out_shapes,
        in_specs=[pl.BlockSpec((tm, w), lambda i: (i, 0)) for w in widths]
        + [pl.BlockSpec((1, w), lambda i: (0, 0)) for w in widths] + hosted.in_specs,
        out_specs=[pl.BlockSpec((tm, sum(widths)), lambda i: (i, 0))] + hosted.out_specs,
        scratch_shapes=hosted.scratch,
        compiler_params=_cparams(hosted.semantics(("parallel",))),
    )(*xs, *gains, *hosted.operands)
    return out if comm is None else (out, got)


def _sigmoid(g):
    return 1.0 / (1.0 + jnp.exp(-g))


def _ride(result, plan):
    return result if plan is not None else (result, None)


def _residual_then_norm(alpha):
    def epilogue(acc, res, gain):
        y = res + alpha * acc
        return y, _rms_hat(y)[0] * gain
    return epilogue


def _ffn_fwd(x, gain, w, *, tag, ex, first_rider=None, riders=(None, None, None), head=None, h=None,
             next_gain=None):
    if h is None:
        plan = ex.gather(first_rider)
        h, got = _ride(_rms_fwd([x], [gain], name=f"{tag}_norm", comm=plan), plan)
        ex.gathered(first_rider, got, w)
    plan = ex.gather(riders[0])
    g, got = _ride(_mm(h, w[f"{tag}_w_gate"], outs=(BF16,), name=f"{tag}_gate", comm=plan), plan)
    ex.gathered(riders[0], got, w)

    def act(acc, _, g_blk):
        gf = g_blk.astype(F32)
        return acc, gf * _sigmoid(gf) * acc

    plan = ex.gather(riders[1])
    (u, a), got = _ride(_mm(h, w[f"{tag}_w_up"], outs=(BF16, BF16), extras=(g,), epilogue=act, name=f"{tag}_up_act",
                            comm=plan), plan)
    ex.gathered(riders[1], got, w)
    plan = ex.gather(riders[2])
    if head is None and next_gain is None:
        y, got = _ride(_mm(a, w[f"{tag}_w_down"], res=x, alpha=HALF_STEP, name=f"{tag}_down", comm=plan), plan)
    elif head is None:
        y, got = _ride(_mm(a, w[f"{tag}_w_down"], res=x, rows=(next_gain,), outs=(F32, BF16),
                           epilogue=_residual_then_norm(HALF_STEP), name=f"{tag}_down_norm", comm=plan), plan)
    else:
        final_gain, target = head
        y, got = _ride(_mm(a, w[f"{tag}_w_down"], res=x, extras=(target,), rows=(final_gain,), row_sums=2,
                           epilogue=_loss_head_epilogue, name=f"{tag}_down_loss", comm=plan), plan)
    ex.gathered(riders[2], got, w)
    return y, (h, g, u, a)


def _loss_head_epilogue(acc, x_in, target, gain):
    xh, r = _rms_hat(x_in + HALF_STEP * acc)
    err = xh * gain - target
    dy = err * (1.0 / acc.shape[1])
    dxh = dy * gain
    dx = r * (dxh - xh * jnp.mean(dxh * xh, axis=-1, keepdims=True))
    loss = 0.5 * jnp.sum(jnp.mean(err * err, axis=-1, keepdims=True), axis=0, keepdims=True)
    return dx, jnp.sum(dy * xh, axis=0, keepdims=True), jnp.zeros_like(gain) + loss


def _rms_bwd_epilogue(acc, dh_so_far, x, *dres_and_gain):
    gain = dres_and_gain[-1]
    dh = acc if dh_so_far is None else acc + dh_so_far
    xh, r = _rms_hat(x)
    dxh = dh * gain
    dx = r * (dxh - xh * jnp.mean(dxh * xh, axis=-1, keepdims=True))
    if len(dres_and_gain) == 2:
        dx = dx + dres_and_gain[0]
    return dx, jnp.sum(dh * xh, axis=0, keepdims=True)


def _ffn_bwd(dout, x, gain, w, saved, *, tag, ex, rider=(None, None), spread=False):
    h, g, u, a = saved
    wg, wu, wd = (w[f"{tag}_w_{n}"] for n in ("gate", "up", "down"))

    def act_bwd(acc, _, g_blk, u_blk):
        gf, uf = g_blk.astype(F32), u_blk.astype(F32)
        da = acc * HALF_STEP
        sig = _sigmoid(gf)
        silu = gf * sig
        return da * uf * (sig + silu * (1.0 - sig)), da * silu

    def carrying(group, grad, call):
        group = group if spread else None
        plan = ex.send(group, {group: grad})
        out, got = _ride(call(plan), plan)
        ex.received(group, got)
        return out

    plan = ex.send(*rider)
    (dg, du), got = _ride(_mm(dout, wd, tb=True, outs=(BF16, BF16), extras=(g, u), epilogue=act_bwd,
                              name=f"{tag}_bwd_act", comm=plan), plan)
    ex.received(rider[0], got)
    dwg = _mm(h, dg, ta=True, outs=(GRAD_WIRE,), name=f"{tag}_dwg")
    dwu = carrying(f"{tag}_w_gate", dwg, lambda plan: _mm(h, du, ta=True, outs=(GRAD_WIRE,), name=f"{tag}_dwu", comm=plan))
    dwd = carrying(f"{tag}_w_up", dwu,
                   lambda plan: _mm(a, dout, ta=True, outs=(GRAD_WIRE,), alpha=HALF_STEP, name=f"{tag}_dwd", comm=plan))
    dh = carrying(f"{tag}_w_down", dwd, lambda plan: _mm(dg, wg, tb=True, name=f"{tag}_dh_gate", comm=plan))
    dx, dgain = _mm(du, wu, tb=True, res=dh, extras=(x, dout), rows=(gain,), row_sums=1, epilogue=_rms_bwd_epilogue,
                    name=f"{tag}_dh_up_norm_bwd")
    return dx, dgain, dwg, dwu, dwd


def _rope_tables(S):
    half = HEAD_DIM // 2
    inv_freq = ROPE_THETA ** (-jnp.arange(half, dtype=F32) / half)
    ang = jnp.arange(S, dtype=F32)[:, None] * inv_freq[None, :]
    cos, sin = jnp.cos(ang), jnp.sin(ang)
    reps = LANES // HEAD_DIM
    cos_t = jnp.tile(jnp.concatenate([cos, cos], axis=1), (1, reps))
    sin_t = jnp.tile(jnp.concatenate([-sin, sin], axis=1), (1, reps))
    return cos_t, sin_t


def _rotate(v, cos, sin, sign):
    half = HEAD_DIM // 2
    groups = []
    for g in range(v.shape[1] // LANES):
        t = v[:, g * LANES:(g + 1) * LANES]
        lane = lax.broadcasted_iota(jnp.int32, t.shape, 1)
        swapped = jnp.where(lane % HEAD_DIM < half, pltpu.roll(t, LANES - half, axis=1), pltpu.roll(t, half, axis=1))
        groups.append(t * cos + swapped * (sin * sign))
    return groups[0] if len(groups) == 1 else jnp.concatenate(groups, axis=1)


def _join_d_proj(pieces, rotated, cos_t, sin_t, *, name):
    S = pieces[0].shape[0]
    widths = [p.shape[1] for p in pieces]
    tm = _pick(S, (256, 128))
    n = len(pieces)

    def body(*refs):
        c_ref, s_ref, o_ref = refs[n], refs[n + 1], refs[n + 2]
        off = 0
        for i in range(n):
            v = refs[i][...]
            if i in rotated:
                v = _rotate(v, c_ref[...], s_ref[...], -1.0)
            o_ref[:, off:off + widths[i]] = v.astype(o_ref.dtype)
            off += widths[i]

    return pl.pallas_call(
        body, name=name, grid=(S // tm,),
        out_shape=jax.ShapeDtypeStruct((S, sum(widths)), BF16),
        in_specs=[pl.BlockSpec((tm, w), lambda i: (i, 0)) for w in widths]
        + [pl.BlockSpec((tm, LANES), lambda i: (i, 0))] * 2,
        out_specs=pl.BlockSpec((tm, sum(widths)), lambda i: (i, 0)),
        compiler_params=_cparams(("parallel",)),
    )(*pieces, cos_t, sin_t)


def _head_masks(shape):
    lane = lax.broadcasted_iota(jnp.int32, shape, 1)
    return [(lane >= HEAD_DIM * h) & (lane < HEAD_DIM * (h + 1)) for h in range(shape[1] // HEAD_DIM)]


def _sb_scores(q2, k_j):
    z = lax.dot_general(q2, k_j, (((1,), (1,)), ((), ())), preferred_element_type=F32)
    sign_bit = jnp.int32(-2 ** 31)
    minus_abs = lax.bitcast_convert_type(lax.bitcast_convert_type(z, jnp.int32) | sign_bit, F32)
    softplus = jnp.maximum(z, 0.0) + jnp.log(1.0 + jnp.exp(minus_abs))
    return z - softplus, softplus


def _sb_stack_heads(t, scale=None):
    parts = [jnp.where(hm, t, jnp.zeros_like(t)) for hm in _head_masks(t.shape)]
    t2 = jnp.concatenate(parts, axis=0)
    if scale is not None:
        t2 = (t2.astype(F32) * scale).astype(t2.dtype)
    return t2


def _sb_unstack_heads(t2):
    n = t2.shape[1] // HEAD_DIM
    T = t2.shape[0] // n
    masks = _head_masks((T, t2.shape[1]))
    out = t2[:T]
    for h in range(1, n):
        out = jnp.where(masks[h], t2[h * T:(h + 1) * T], out)
    return out


def _sb_causal(T, n_heads):
    row = lax.broadcasted_iota(jnp.int32, (n_heads * T, T), 0)
    col = lax.broadcasted_iota(jnp.int32, (n_heads * T, T), 1)
    return col < row % T


def _sb_triangle(T, later):
    row = lax.broadcasted_iota(jnp.int32, (T, T), 0)
    col = lax.broadcasted_iota(jnp.int32, (T, T), 1)
    return ((row > col) if later else (row < col)).astype(BF16)


def _sb_fwd(p_sb, *, name, comm=None):
    S = p_sb.shape[0]
    W = p_sb.shape[1] // 3
    LW = min(SB_LANES, W)
    NH = LW // HEAD_DIM
    npair = W // LW
    T = SB_TILE
    n_tiles = S // T
    assert n_tiles <= HEAD_DIM
    scale = HEAD_DIM ** -0.5

    R = SB_STEP_TILES
    grid = (npair, n_tiles // R)
    hosted = _Hosted(comm, n_in=5, n_out=2, n_scratch=0)

    def body(*refs):
        refs = hosted.begin(refs, grid)
        step = pl.program_id(1)
        lax.fori_loop(0, R, lambda sub, _: query_tile(step * R + sub, sub, *refs), 0)
        hosted.end(grid)

    def query_tile(I, sub, q_ref, k_ref, v_ref, causal_ref, later_ref, o_ref, c_ref):
        rows = pl.ds(pl.multiple_of(sub * T, T), T)
        lane = lax.broadcasted_iota(jnp.int32, (T, LW), 1)
        causal = causal_ref[...]
        later_than = later_ref[...]
        q2 = _sb_stack_heads(q_ref[rows, :], scale)

        def scores(J, diag):
            off = pl.multiple_of(J * T, T)
            log_beta, stay = _sb_scores(q2, k_ref[pl.ds(off, T), :])
            if diag:
                stay = stay * causal
            local = jnp.dot(stay.astype(BF16), later_than, preferred_element_type=F32)
            return log_beta, local, jnp.sum(stay, axis=1, keepdims=True), v_ref[pl.ds(off, T), :]

        def weigh(J, sc, gone, acc, carr, diag):
            log_beta, local, _, v_j = sc
            w = jnp.exp((log_beta - gone) - local)
            if diag:
                w = w * causal
            acc = acc + jnp.dot(w.astype(BF16), v_j, preferred_element_type=F32)
            for h in range(NH):
                carr = jnp.where(lane == HEAD_DIM * h + J, -gone[h * T:(h + 1) * T], carr)
            return acc, carr

        def tiles(J, count, state, diag):
            gone, acc, carr, _ = state
            scs = [scores(J - u, diag and u == 0) for u in range(count)]
            for u, sc in enumerate(scs):
                acc, carr = weigh(J - u, sc, gone, acc, carr, diag and u == 0)
                gone = gone + sc[2]
            return gone, acc, carr, jnp.min(gone)

        U = SB_UNROLL
        alive = lambda st: st[3] < SB_DEAD
        state = (jnp.zeros((NH * T, 1), F32), jnp.zeros((NH * T, LW), F32),
                 jnp.full((T, LW), SB_UNSEEN, F32), jnp.zeros((), F32))
        state = lax.cond(I > 0, lambda st: tiles(I, 2, st, True), lambda st: tiles(I, 1, st, True), state)
        rest = jnp.maximum(I - 1, 0)
        singles = jnp.where(rest > 0, (rest - 1) % U + 1, 0)
        _, state = lax.while_loop(lambda c: (c[0] < singles) & alive(c[1]),
                                  lambda c: (c[0] + 1, tiles(I - 2 - c[0], 1, c[1], False)), (jnp.int32(0), state))
        blocks = (rest - singles) // U
        _, state = lax.while_loop(lambda c: (c[0] < blocks) & alive(c[1]),
                                  lambda c: (c[0] + 1, tiles(I - 2 - singles - U * c[0], U, c[1], False)),
                                  (jnp.int32(0), state))
        _, acc, carr, _ = state
        o_ref[rows, :] = _sb_unstack_heads(acc)
        c_ref[rows, :] = carr
        return 0

    blk = lambda I_off: pl.BlockSpec((R * T, LW), lambda p, I: (I, I_off + p))
    full = lambda off: pl.BlockSpec((S, LW), lambda p, I: (0, off + p))
    const = lambda rows: pl.BlockSpec((rows, T), lambda p, I: (0, 0))
    o, carries, *got = pl.pallas_call(
        body, name=name, grid=grid,
        out_shape=[jax.ShapeDtypeStruct((S, W), F32), jax.ShapeDtypeStruct((S, W), F32)] + hosted.out_shapes,
        in_specs=[blk(0), full(npair), full(2 * npair), const(NH * T), const(T)] + hosted.in_specs,
        out_specs=[blk(0), blk(0)] + hosted.out_specs,
        scratch_shapes=hosted.scratch,
        compiler_params=_cparams(hosted.semantics(("parallel", "arbitrary")), vmem_mb=56),
    )(p_sb, p_sb, p_sb, _sb_causal(T, NH).astype(F32), _sb_triangle(T, True), *hosted.operands)
    return (o, carries) if comm is None else (o, carries, got)


def _sb_bwd(p_sb, do, carries, *, name, comm=None):
    S = p_sb.shape[0]
    W = p_sb.shape[1] // 3
    LW = min(LANES, W)
    NH = LW // HEAD_DIM
    npair = W // LW
    T = SB_TILE
    n_tiles = S // T
    scale = HEAD_DIM ** -0.5

    R = SB_STEP_TILES
    grid = (npair, n_tiles // R)
    hosted = _Hosted(comm, n_in=8, n_out=3, n_scratch=0)

    def body(*refs):
        refs = hosted.begin(refs, grid)
        dk_ref, dv_ref = refs[9], refs[10]
        step = pl.program_id(1)

        @pl.when(step == 0)
        def _():
            dk_ref[...] = jnp.zeros_like(dk_ref)
            dv_ref[...] = jnp.zeros_like(dv_ref)

        lax.fori_loop(0, R, lambda sub, _: query_tile(step * R + sub, sub, *refs), 0)
        hosted.end(grid)

    def query_tile(I, sub, q_ref, k_ref, v_ref, do_ref, c_ref, causal_ref, later_ref, earlier_ref,
                   dq_ref, dk_ref, dv_ref):
        rows = pl.ds(pl.multiple_of(sub * T, T), T)
        lane = lax.broadcasted_iota(jnp.int32, (T, LW), 1)
        causal = causal_ref[...]
        later_than = later_ref[...]
        earlier_than = earlier_ref[...]
        q2 = _sb_stack_heads(q_ref[rows, :], scale)
        do2 = _sb_stack_heads(do_ref[rows, :].astype(BF16))
        carr = c_ref[rows, :]
        tn_dims = (((0,), (0,)), ((), ()))

        def chain(J, diag):
            off = pl.multiple_of(J * T, T)
            k_j = k_ref[pl.ds(off, T), :]
            v_j = v_ref[pl.ds(off, T), :]
            log_beta, stay = _sb_scores(q2, k_j)
            if diag:
                stay = stay * causal
            lc = jnp.concatenate(
                [jnp.sum(jnp.where(lane == HEAD_DIM * h + J, carr, 0.0), axis=1, keepdims=True) for h in range(NH)],
                axis=0)
            w = jnp.exp((log_beta + lc) - jnp.dot(stay.astype(BF16), later_than, preferred_element_type=F32))
            if diag:
                w = w * causal
            dw = lax.dot_general(do2, v_j, (((1,), (1,)), ((), ())), preferred_element_type=F32)
            e = w * dw
            local = jnp.dot(e.astype(BF16), earlier_than, preferred_element_type=F32)
            return off, k_j, w, e, local, jnp.exp(log_beta), jnp.sum(e, axis=1, keepdims=True)

        def finish(ch, ec, dq_acc, diag):
            off, k_j, w, e, local, beta, _ = ch
            e_before = local + ec
            dz = e - beta * (e + e_before)
            if diag:
                dz = dz * causal
            dzb = dz.astype(BF16)
            dq_acc = dq_acc + jnp.dot(dzb, k_j, preferred_element_type=F32)
            dk_ref[pl.ds(off, T), :] += lax.dot_general(dzb, q2, tn_dims, preferred_element_type=F32)
            dv_ref[pl.ds(off, T), :] += lax.dot_general(w.astype(BF16), do2, tn_dims, preferred_element_type=F32)
            return dq_acc

        def tiles(J, count, state, diag):
            ec, dq_acc = state
            chains = [chain(J + u, diag and u == count - 1) for u in range(count)]
            for u, ch in enumerate(chains):
                dq_acc = finish(ch, ec, dq_acc, diag and u == count - 1)
                ec = ec + ch[6]
            return ec, dq_acc

        lane_row = lax.broadcasted_iota(jnp.int32, (1, LW), 1)
        reached = (jnp.max(carr, axis=0, keepdims=True) > 0.5 * SB_UNSEEN) & (lane_row < HEAD_DIM)
        first = jnp.min(jnp.where(reached, lane_row.astype(F32), float(n_tiles))).astype(jnp.int32)
        U = SB_UNROLL
        count = I - first
        rest = jnp.maximum(count - 1, 0)
        state = (jnp.zeros((NH * T, 1), F32), jnp.zeros((NH * T, LW), F32))
        state = lax.fori_loop(0, rest // U, lambda jj, st: tiles(first + U * jj, U, st, False), state)
        state = lax.fori_loop(0, rest % U, lambda r, st: tiles(I - 1 - rest % U + r, 1, st, False), state)
        _, dq_acc = lax.cond(count > 0, lambda st: tiles(I - 1, 2, st, True), lambda st: tiles(I, 1, st, True), state)
        dq_ref[rows, :] = _sb_unstack_heads(dq_acc) * scale
        return 0

    blk = lambda src_off: pl.BlockSpec((R * T, LW), lambda p, I: (I, src_off + p))
    full = lambda off: pl.BlockSpec((S, LW), lambda p, I: (0, off + p))
    const = lambda rows: pl.BlockSpec((rows, T), lambda p, I: (0, 0))
    dq, dk, dv, *got = pl.pallas_call(
        body, name=name, grid=grid,
        out_shape=[jax.ShapeDtypeStruct((S, W), F32)] * 3 + hosted.out_shapes,
        in_specs=[blk(0), full(npair), full(2 * npair), blk(0), blk(0), const(NH * T), const(T), const(T)]
        + hosted.in_specs,
        out_specs=[blk(0), full(0), full(0)] + hosted.out_specs,
        scratch_shapes=hosted.scratch,
        compiler_params=_cparams(hosted.semantics(("parallel", "arbitrary")), vmem_mb=56),
    )(p_sb, p_sb, p_sb, do, carries, _sb_causal(T, NH).astype(F32), _sb_triangle(T, True), _sb_triangle(T, False),
      *hosted.operands)
    return (dq, dk, dv) if comm is None else (dq, dk, dv, got)


def _dil_blocks(b, body_fn):
    for pi, (window, dil) in enumerate(DILATED_PATTERNS):
        assert window // dil == DIL_BLOCK
        nblk = DIL_SUPER // (DIL_BLOCK * dil)
        assert (dil * nblk) % DIL_UNROLL == 0

        def group(g, _, pi=pi, dil=dil, nblk=nblk):
            for u in range(DIL_UNROLL):
                t = g * DIL_UNROLL + u
                n = t % nblk
                body_fn(pi, dil, t // nblk, n, b * nblk + n)
            return 0

        lax.fori_loop(0, dil * nblk // DIL_UNROLL, group, 0)


def _dil_rows(start, size, dil):
    if dil == 1:
        return pl.ds(pl.multiple_of(start, DIL_BLOCK), size)
    return pl.ds(start, size, stride=dil)


def _dil_fill_bias(bias_ref):
    row = lax.broadcasted_iota(jnp.int32, (2 * DIL_BLOCK, 2 * DIL_BLOCK), 0)
    kk = lax.broadcasted_iota(jnp.int32, (2 * DIL_BLOCK, 2 * DIL_BLOCK), 1)
    qi = jnp.where(row >= DIL_BLOCK, row - DIL_BLOCK, row)
    for s in range(2):
        dist = s * DIL_BLOCK + qi - kk
        bias_ref[s] = jnp.where((dist >= 0) & (dist <= DIL_BLOCK), 0.0, NEG_BIG)


def _dl_fwd(p_dl, *, name):
    S, W = p_dl.shape[0], p_dl.shape[1] // 3
    npair = W // LANES
    nsuper = S // DIL_SUPER
    assert S % DIL_SUPER == 0 and S // max(d for _, d in DILATED_PATTERNS) >= 2 * DIL_BLOCK
    scale = HEAD_DIM ** -0.5
    npat = len(DILATED_PATTERNS)

    def body(q_ref, k_ref, v_ref, o_ref, l_ref, bias_ref, *pattern_refs):
        op_refs, lp_refs = pattern_refs[:npat], pattern_refs[npat:]
        b = pl.program_id(1)
        masks = _head_masks((DIL_BLOCK, LANES))
        pl.when(b == 0)(lambda: _dil_fill_bias(bias_ref))

        def block(pi, dil, c, n, gn):
            ws = jnp.maximum(gn - 1, 0)
            qrows = n * (DIL_BLOCK * dil) + c
            krows = ws * (DIL_BLOCK * dil) + c
            q_idx = _dil_rows(qrows, DIL_BLOCK, dil)
            k_idx = _dil_rows(krows, 2 * DIL_BLOCK, dil)
            qb = q_ref[q_idx, :]
            kb = k_ref[k_idx, :].astype(BF16)
            vb = v_ref[k_idx, :].astype(BF16)
            q2 = _sb_stack_heads(qb.astype(BF16), scale)
            z = lax.dot_general(q2, kb, (((1,), (1,)), ((), ())), preferred_element_type=F32) + bias_ref[gn - ws]
            m = jnp.max(z, axis=1, keepdims=True)
            p = jnp.exp(z - m)
            den = jnp.sum(p, axis=1, keepdims=True)
            acc = jnp.dot(p.astype(BF16), vb, preferred_element_type=F32)
            lse = m + jnp.log(den)
            op_refs[pi][q_idx, :] = _sb_unstack_heads(acc / den)
            lp_refs[pi][q_idx, :] = jnp.where(masks[0], lse[:DIL_BLOCK], lse[DIL_BLOCK:])

        _dil_blocks(b, block)
        lses = [r[...] for r in lp_refs]
        top = functools.reduce(jnp.maximum, lses)
        ws_ = [jnp.exp(l - top) for l in lses]
        den = functools.reduce(jnp.add, ws_)
        num = functools.reduce(jnp.add, [w * r[...] for r, w in zip(op_refs, ws_)])
        o_ref[...] = num / den
        l_ref[...] = top + jnp.log(den)

    blk = pl.BlockSpec((DIL_SUPER, LANES), lambda p, b: (b, p))
    full = lambda off: pl.BlockSpec((S, LANES), lambda p, b: (0, off + p))
    return pl.pallas_call(
        body, name=name, grid=(npair, nsuper),
        out_shape=[jax.ShapeDtypeStruct((S, W), F32)] * 2,
        in_specs=[blk, full(npair), full(2 * npair)], out_specs=[blk, blk],
        scratch_shapes=[pltpu.VMEM((2, 2 * DIL_BLOCK, 2 * DIL_BLOCK), F32)]
        + [pltpu.VMEM((DIL_SUPER, LANES), F32)] * (2 * npat),
        compiler_params=_cparams(("arbitrary", "arbitrary")),
    )(p_dl, p_dl, p_dl)


def _dl_bwd(p_dl, o, lse, do, *, name):
    S, W = p_dl.shape[0], p_dl.shape[1] // 3
    npair = W // LANES
    nsuper = S // DIL_SUPER
    scale = HEAD_DIM ** -0.5

    def body(q_ref, k_ref, v_ref, o_ref, l_ref, do_ref, dq_ref, dk_ref, dv_ref, delta_ref, bias_ref):
        b = pl.program_id(1)

        @pl.when(b == 0)
        def _():
            dk_ref[...] = jnp.zeros_like(dk_ref)
            dv_ref[...] = jnp.zeros_like(dv_ref)
            _dil_fill_bias(bias_ref)

        dq_ref[...] = jnp.zeros_like(dq_ref)
        prod = do_ref[...] * o_ref[...]
        delta = jnp.zeros_like(prod)
        for hm in _head_masks(prod.shape):
            delta = jnp.where(hm, jnp.sum(jnp.where(hm, prod, 0.0), axis=1, keepdims=True), delta)
        delta_ref[...] = delta

        def block(pi, dil, c, n, gn):
            ws = jnp.maximum(gn - 1, 0)
            qrows = n * (DIL_BLOCK * dil) + c
            krows = ws * (DIL_BLOCK * dil) + c
            q_idx = _dil_rows(qrows, DIL_BLOCK, dil)
            k_idx = _dil_rows(krows, 2 * DIL_BLOCK, dil)
            qb = q_ref[q_idx, :]
            dob = do_ref[q_idx, :]
            lb = l_ref[q_idx, :]
            db = delta_ref[q_idx, :]
            kb = k_ref[k_idx, :].astype(BF16)
            vb = v_ref[k_idx, :].astype(BF16)
            q2 = _sb_stack_heads(qb.astype(BF16), scale)
            do2 = _sb_stack_heads(dob.astype(BF16))
            lse2 = jnp.concatenate([lb[:, HEAD_DIM * h:HEAD_DIM * h + 1] for h in range(2)], axis=0)
            delta2 = jnp.concatenate([db[:, HEAD_DIM * h:HEAD_DIM * h + 1] for h in range(2)], axis=0)
            z = lax.dot_general(q2, kb, (((1,), (1,)), ((), ())), preferred_element_type=F32)
            p = jnp.exp((z + bias_ref[gn - ws]) - lse2)
            dp = lax.dot_general(do2, vb, (((1,), (1,)), ((), ())), preferred_element_type=F32)
            dzb = (p * (dp - delta2)).astype(BF16)
            tn_dims = (((0,), (0,)), ((), ()))
            dq_blk = _sb_unstack_heads(jnp.dot(dzb, kb, preferred_element_type=F32)) * scale
            dk_blk = lax.dot_general(dzb, q2, tn_dims, preferred_element_type=F32)
            dv_blk = lax.dot_general(p.astype(BF16), do2, tn_dims, preferred_element_type=F32)
            dq_ref[q_idx, :] = dq_ref[q_idx, :] + dq_blk
            dk_ref[k_idx, :] = dk_ref[k_idx, :] + dk_blk
            dv_ref[k_idx, :] = dv_ref[k_idx, :] + dv_blk

        _dil_blocks(b, block)

    blk = pl.BlockSpec((DIL_SUPER, LANES), lambda p, b: (b, p))
    full = lambda off: pl.BlockSpec((S, LANES), lambda p, b: (0, off + p))
    return pl.pallas_call(
        body, name=name, grid=(npair, nsuper),
        out_shape=[jax.ShapeDtypeStruct((S, W), F32)] * 3,
        in_specs=[blk, full(npair), full(2 * npair), blk, blk, blk], out_specs=[blk, full(0), full(0)],
        scratch_shapes=[pltpu.VMEM((DIL_SUPER, LANES), F32), pltpu.VMEM((2, 2 * DIL_BLOCK, 2 * DIL_BLOCK), F32)],
        compiler_params=_cparams(("arbitrary", "arbitrary")),
    )(p_dl, p_dl, p_dl, o, lse, do)


class _NoExchange:
    def gather(self, family):
        return None

    def gathered(self, family, got, weights):
        pass

    def send(self, family, grads):
        return None

    def received(self, family, got):
        pass


def _local_step(x, target, gains, weights, exchanges=None):
    S, D = x.shape
    ex = exchanges or _NoExchange()
    weights = dict(weights)
    d_sb = gains["sb_out_norm"].shape[1]
    d_dl = gains["dil_out_norm"].shape[1]
    cos_t, sin_t = _rope_tables(S)

    riders = ("ffn1_w_up", "ffn1_w_down", "mixer") if exchanges else (None, None, None)
    (x1, h2), saved1 = _ffn_fwd(x, gains["ffn1_norm"], weights, tag="ffn1", ex=ex, riders=riders,
                                first_rider="ffn1_w_gate" if exchanges else None,
                                next_gain=gains["mix_norm"])
    w_in = weights["w_in"]
    w_out = weights["w_out"]
    p_sb = _mm(h2, w_in, b_cols=(0, 3 * d_sb), outs=(BF16,), name="proj_sb")

    def rope_qk(acc, _, cos, sin):
        return jnp.concatenate([_rotate(acc[:, :2 * d_dl], cos, sin, 1.0), acc[:, 2 * d_dl:]], axis=1)

    p_dl = _mm(h2, w_in, b_cols=(3 * d_sb, 3 * d_dl), lanes=(cos_t, sin_t), epilogue=rope_qk, name="proj_dl_rope")
    plan = ex.gather("ffn2" if exchanges else None)
    o_sb, carries, *got = _sb_fwd(p_sb, name="sb_fwd", comm=plan)
    ex.gathered("ffn2", got[0] if got else None, weights)
    o_dl, lse_dl = _dl_fwd(p_dl, name="dl_fwd")
    merged = _rms_fwd([o_sb, o_dl], [gains["sb_out_norm"], gains["dil_out_norm"]], name="out_norm")
    x2, h3 = _mm(merged, w_out, res=x1, rows=(gains["ffn2_norm"],), outs=(F32, BF16),
                 epilogue=_residual_then_norm(1.0), name="out_proj_norm")
    (dx3, d_final, loss_wide), saved2 = _ffn_fwd(x2, gains["ffn2_norm"], weights, tag="ffn2", ex=ex, h=h3,
                                                 head=(gains["final_norm"], target))
    loss_row = loss_wide[:, :LANES]

    dx2, d_ffn2_norm, dwg2, dwu2, dwd2 = _ffn_bwd(dx3, x2, gains["ffn2_norm"], weights, saved2, tag="ffn2", ex=ex)
    d_w_out = _mm(merged, dx2, ta=True, outs=(GRAD_WIRE,), name="d_w_out")
    do_sb, d_sb_norm = _mm(dx2, w_out, tb=True, b_cols=(0, d_sb), extras=(o_sb,), rows=(gains["sb_out_norm"],),
                           row_sums=1, epilogue=_rms_bwd_epilogue, name="d_merged_sb")
    do_dl, d_dl_norm = _mm(dx2, w_out, tb=True, b_cols=(d_sb, d_dl), extras=(o_dl,), rows=(gains["dil_out_norm"],),
                           row_sums=1, epilogue=_rms_bwd_epilogue, name="d_merged_dl")
    plan = ex.send("ffn2", dict(ffn2_w_gate=dwg2, ffn2_w_up=dwu2, ffn2_w_down=dwd2))
    dq_sb, dk_sb, dv_sb, *got = _sb_bwd(p_sb, do_sb, carries, name="sb_bwd", comm=plan)
    ex.received("ffn2", got[0] if got else None)
    dq_dl, dk_dl, dv_dl = _dl_bwd(p_dl, o_dl, lse_dl, do_dl, name="dl_bwd")
    d_proj = _join_d_proj([dq_sb, dk_sb, dv_sb, dq_dl, dk_dl, dv_dl], (3, 4), cos_t, sin_t, name="d_proj")
    d_w_in = _mm(h2, d_proj, ta=True, outs=(GRAD_WIRE,), name="d_w_in")
    dx1, d_mix_norm = _mm(d_proj, w_in, tb=True, extras=(x1, dx2), rows=(gains["mix_norm"],), row_sums=1,
                          epilogue=_rms_bwd_epilogue, name="dh_mix_norm_bwd")
    dx, d_ffn1_norm, dwg1, dwu1, dwd1 = _ffn_bwd(
        dx1, x, gains["ffn1_norm"], weights, saved1, tag="ffn1", ex=ex,
        rider=("mixer", dict(w_in=d_w_in, w_out=d_w_out)), spread=True)
    gain_grads = dict(ffn1_norm=d_ffn1_norm, mix_norm=d_mix_norm, sb_out_norm=d_sb_norm, dil_out_norm=d_dl_norm,
                      ffn2_norm=d_ffn2_norm, final_norm=d_final)
    weight_grads = dict(ffn1_w_gate=dwg1, ffn1_w_up=dwu1, ffn1_w_down=dwd1, w_in=d_w_in, w_out=d_w_out,
                        ffn2_w_gate=dwg2, ffn2_w_up=dwu2, ffn2_w_down=dwd2)
    return loss_row, dx, gain_grads, weight_grads


def _mesh_position():
    return lax.axis_index("x"), lax.axis_index("y"), lax.axis_index("c")


def _flip(coord, bit):
    return 1 - coord if bit else coord


RELATIONS = [(rx, ry, rc) for rx in (0, 1) for ry in (0, 1) for rc in (0, 1)][1:]


class _GatherPlan:
    def __init__(self, shards):
        n = len(shards)
        self.operands = list(shards)
        self.out_shapes = [jax.ShapeDtypeStruct((N_DEV,) + s.shape, s.dtype) for s in shards]
        self.scratch = [pltpu.SemaphoreType.DMA((n, 7)), pltpu.SemaphoreType.DMA((n, 7)),
                        pltpu.SemaphoreType.DMA((n,))]

    def _copies(self, in_refs, out_refs, sems):
        send_sems, recv_sems, local_sems = sems
        x, y, c = _mesh_position()
        me, sibling = (x, y, c), (x, y, 1 - c)
        chips = [(1 - x, y), (x, 1 - y), (1 - x, 1 - y)]
        plans = []
        for t, (x_ref, out_ref) in enumerate(zip(in_refs, out_refs)):
            def slot(px, py, pc, out_ref=out_ref):
                return out_ref.at[4 * px + 2 * py + pc]

            def copy(k, block, to, src=None, t=t, slot=slot):
                return pltpu.make_async_remote_copy(
                    src_ref=slot(*block) if src is None else src, dst_ref=slot(*block),
                    send_sem=send_sems.at[t, k], recv_sem=recv_sems.at[t, k],
                    device_id=to, device_id_type=pl.DeviceIdType.MESH)

            plans.append(dict(
                mine=pltpu.make_async_copy(x_ref, slot(*me), local_sems.at[t]),
                first=[copy(0, me, sibling, src=x_ref)]
                + [copy(1 + j, me, (*chip, c), src=x_ref) for j, chip in enumerate(chips)],
                over_ici=[copy(1 + j, (*chip, c), me) for j, chip in enumerate(chips)],
                passed=[copy(4 + j, (*chip, c), sibling) for j, chip in enumerate(chips)],
                from_sibling=[copy(0, sibling, me)] + [copy(4 + j, (*chip, 1 - c), me) for j, chip in enumerate(chips)]))
        return plans

    def start(self, in_refs, out_refs, sems):
        for p in self._copies(in_refs, out_refs, sems):
            p["mine"].start()
            for cp in p["first"]:
                cp.start()

    def finish(self, in_refs, out_refs, sems):
        plans = self._copies(in_refs, out_refs, sems)
        for p in plans:
            for arrived, onward in zip(p["over_ici"], p["passed"]):
                arrived.wait_recv()
                onward.start()
        for p in plans:
            for cp in p["from_sibling"]:
                cp.wait_recv()
            for cp in p["first"] + p["passed"]:
                cp.wait_send()
            p["mine"].wait()


class _Hosted:
    def __init__(self, plan, n_in, n_out, n_scratch):
        self.plan, self.n_in, self.n_out, self.n_scratch = plan, n_in, n_out, n_scratch
        self.operands = list(plan.operands) if plan else []
        self.out_shapes = list(plan.out_shapes) if plan else []
        self.scratch = list(plan.scratch) if plan else []
        self.in_specs = [pl.BlockSpec(memory_space=pl.ANY)] * len(self.operands)
        self.out_specs = [pl.BlockSpec(memory_space=pl.ANY)] * len(self.out_shapes)

    def semantics(self, sem):
        return sem if self.plan is None else ("arbitrary",) * len(sem)

    def _at(self, grid, last):
        hit = None
        for d, n in enumerate(grid):
            here = pl.program_id(d) == (n - 1 if last else 0)
            hit = here if hit is None else hit & here
        return hit

    def begin(self, refs, grid):
        if self.plan is None:
            return refs
        k_in, k_out = len(self.operands), len(self.out_shapes)
        ins, rest = refs[:self.n_in], refs[self.n_in:]
        c_in, rest = rest[:k_in], rest[k_in:]
        outs, rest = rest[:self.n_out], rest[self.n_out:]
        c_out, rest = rest[:k_out], rest[k_out:]
        scratch, sems = rest[:self.n_scratch], rest[self.n_scratch:]
        self._args = (c_in, c_out, sems)
        pl.when(self._at(grid, False))(lambda: self.plan.start(*self._args))
        return tuple(ins) + tuple(outs) + tuple(scratch)

    def end(self, grid):
        if self.plan is not None:
            pl.when(self._at(grid, True))(lambda: self.plan.finish(*self._args))


class _ExchangePlan:
    def __init__(self, packs):
        n = len(packs)
        self.operands = list(packs)
        self.out_shapes = [jax.ShapeDtypeStruct(p.shape, p.dtype) for p in packs]
        self.scratch = [pltpu.SemaphoreType.DMA((n, 7)), pltpu.SemaphoreType.DMA((n, 7)),
                        pltpu.SemaphoreType.DMA((n,))]

    def _copies(self, in_refs, out_refs, sems):
        send_sems, recv_sems, local_sems = sems
        x, y, c = _mesh_position()
        me = 4 * x + 2 * y + c
        copies = [pltpu.make_async_copy(i.at[me], o.at[me], local_sems.at[t])
                  for t, (i, o) in enumerate(zip(in_refs, out_refs))]
        for r, (rx, ry, rc) in enumerate(RELATIONS):
            px, py, pc = _flip(x, rx), _flip(y, ry), _flip(c, rc)
            peer = 4 * px + 2 * py + pc
            copies += [pltpu.make_async_remote_copy(
                src_ref=i.at[peer], dst_ref=o.at[me], send_sem=send_sems.at[t, r], recv_sem=recv_sems.at[t, r],
                device_id=(px, py, pc), device_id_type=pl.DeviceIdType.MESH)
                for t, (i, o) in enumerate(zip(in_refs, out_refs))]
        return copies

    def start(self, in_refs, out_refs, sems):
        for cp in self._copies(in_refs, out_refs, sems):
            cp.start()

    def finish(self, in_refs, out_refs, sems):
        for cp in self._copies(in_refs, out_refs, sems):
            cp.wait()


def _all_reduce_rows(v, *, name):
    R, C = v.shape

    def body(v_ref, out_ref, buf, send_sems, recv_sems):
        x, y, c = _mesh_position()
        me = 4 * x + 2 * y + c
        buf[me] = v_ref[...]
        copies = []
        for r, (rx, ry, rc) in enumerate(RELATIONS):
            cp = pltpu.make_async_remote_copy(
                src_ref=v_ref, dst_ref=buf.at[me], send_sem=send_sems.at[r], recv_sem=recv_sems.at[r],
                device_id=(_flip(x, rx), _flip(y, ry), _flip(c, rc)), device_id_type=pl.DeviceIdType.MESH)
            cp.start()
            copies.append(cp)
        for cp in copies:
            cp.wait()
        total = buf[0]
        for s in range(1, N_DEV):
            total = total + buf[s]
        out_ref[...] = total

    return pl.pallas_call(
        body, name=name,
        out_shape=jax.ShapeDtypeStruct((R, C), F32),
        in_specs=[pl.BlockSpec(memory_space=pltpu.VMEM)],
        out_specs=pl.BlockSpec(memory_space=pltpu.VMEM),
        scratch_shapes=[pltpu.VMEM((N_DEV, R, C), F32), pltpu.SemaphoreType.DMA((7,)), pltpu.SemaphoreType.DMA((7,))],
    )(v)


def _adamw(w, g, m, v, *, name):
    R, C = w.shape
    slots = g.ndim == 3
    tr = _pick(R, (256, 128, 64, 32, 16) if slots else (256, 128, 64, 32, 16, 8))

    def body(w_ref, g_ref, m_ref, v_ref, g_out, d_ref, nm_ref, nv_ref):
        if slots:
            g = g_ref[0].astype(F32)
            for s in range(1, N_DEV):
                g = g + g_ref[s].astype(F32)
        else:
            g = g_ref[...]
        g_out[...] = g
        m_new = ADAM_B1 * m_ref[...] + (1.0 - ADAM_B1) * g
        v_new = ADAM_B2 * v_ref[...] + (1.0 - ADAM_B2) * (g * g)
        m_hat = m_new / (1.0 - ADAM_B1 ** ADAM_STEP)
        v_hat = v_new / (1.0 - ADAM_B2 ** ADAM_STEP)
        d_ref[...] = -ADAM_LR * (m_hat / (jnp.sqrt(v_hat) + ADAM_EPS) + ADAM_WD * w_ref[...])
        nm_ref[...] = m_new
        nv_ref[...] = v_new

    spec = pl.BlockSpec((tr, C), lambda i: (i, 0))
    g_spec = pl.BlockSpec((N_DEV, tr, C), lambda i: (0, i, 0)) if slots else spec
    return pl.pallas_call(
        body, name=name, grid=(R // tr,),
        out_shape=[jax.ShapeDtypeStruct((R, C), F32)] * 4,
        in_specs=[spec, g_spec, spec, spec], out_specs=[spec] * 4,
        compiler_params=_cparams(("parallel",)),
    )(w, g, m, v)


WEIGHT_NAMES = ["ffn1_norm", "ffn1_w_gate", "ffn1_w_up", "ffn1_w_down", "mix_norm", "w_in", "sb_out_norm",
                "dil_out_norm", "w_out", "ffn2_norm", "ffn2_w_gate", "ffn2_w_up", "ffn2_w_down", "final_norm"]
GAIN_NAMES = ["ffn1_norm", "mix_norm", "sb_out_norm", "dil_out_norm", "ffn2_norm", "final_norm"]
COL_SHARDED = ["ffn1_w_gate", "ffn1_w_up", "ffn2_w_gate", "ffn2_w_up", "w_in"]
ROW_SHARDED = ["ffn1_w_down", "ffn2_w_down", "w_out"]
GROUPS = {"mixer": (["w_in"], ["w_out"]),
          "ffn2": (["ffn2_w_gate", "ffn2_w_up"], ["ffn2_w_down"])}
for _ffn in ("ffn1", "ffn2"):
    GROUPS.update({f"{_ffn}_w_gate": ([f"{_ffn}_w_gate"], []), f"{_ffn}_w_up": ([f"{_ffn}_w_up"], []),
                   f"{_ffn}_w_down": ([], [f"{_ffn}_w_down"])})


class _Exchanges:
    def __init__(self, params):
        self.params = params
        self.grads = {}

    def gather(self, group):
        if group is None:
            return None
        cols, rows = GROUPS[group]
        return _GatherPlan([self.params[n].astype(BF16) for n in cols + rows])

    def gathered(self, group, got, weights):
        if group is None:
            return
        cols, rows = GROUPS[group]
        for n, blocks in zip(cols + rows, got):
            if n in cols:
                weights[n] = jnp.transpose(blocks, (1, 0, 2)).reshape(blocks.shape[1], N_DEV * blocks.shape[2])
            else:
                weights[n] = blocks.reshape(N_DEV * blocks.shape[1], blocks.shape[2])

    def send(self, group, grads):
        if group is None:
            return None
        cols, rows = GROUPS[group]
        packs = [jnp.transpose(grads[n].reshape(grads[n].shape[0], N_DEV, self.params[n].shape[1]), (1, 0, 2))
                 for n in cols]
        packs += [grads[n].reshape(N_DEV, self.params[n].shape[0], grads[n].shape[1]) for n in rows]
        return _ExchangePlan([p.astype(GRAD_WIRE) for p in packs])

    def received(self, group, got):
        if group is None:
            return
        cols, rows = GROUPS[group]
        for n, slots in zip(cols + rows, got):
            self.grads[n] = slots


def _step(x, target, params, moments_m, moments_v):
    ex = _Exchanges(params)
    weights = {}
    gains = {n: params[n] for n in GAIN_NAMES}
    loss_row, grad_x, gain_grads, _ = _local_step(x, target, gains, weights, ex)
    grads = ex.grads

    rows = [gain_grads[n].reshape(-1, LANES) for n in GAIN_NAMES] + [loss_row]
    small = jnp.concatenate(rows, axis=0)
    pad = (-small.shape[0]) % 8
    small = jnp.pad(small, ((0, pad), (0, 0)))
    small = _all_reduce_rows(small, name="reduce_gains_loss")
    off = 0
    for n in GAIN_NAMES:
        r = gain_grads[n].shape[1] // LANES
        grads[n] = small[off:off + r].reshape(1, -1)
        off += r
    loss = small[off, 0]

    delta, new_m, new_v = {}, {}, {}
    for n in WEIGHT_NAMES:
        grads[n], delta[n], new_m[n], new_v[n] = _adamw(params[n], grads[n], moments_m[n], moments_v[n],
                                                        name=f"adamw_{n}")
    return loss, grad_x, grads, delta, new_m, new_v


def kernel(x, ffn1_norm, ffn1_w_gate, ffn1_w_up, ffn1_w_down, mix_norm, w_in, sb_out_norm, dil_out_norm, w_out, ffn2_norm, ffn2_w_gate, ffn2_w_up, ffn2_w_down, final_norm, loss_target, m_ffn1_norm, m_ffn1_w_gate, m_ffn1_w_up, m_ffn1_w_down, m_mix_norm, m_w_in, m_sb_out_norm, m_dil_out_norm, m_w_out, m_ffn2_norm, m_ffn2_w_gate, m_ffn2_w_up, m_ffn2_w_down, m_final_norm, v_ffn1_norm, v_ffn1_w_gate, v_ffn1_w_up, v_ffn1_w_down, v_mix_norm, v_w_in, v_sb_out_norm, v_dil_out_norm, v_w_out, v_ffn2_norm, v_ffn2_w_gate, v_ffn2_w_up, v_ffn2_w_down, v_final_norm):
    given = dict(locals())
    shapes = {n: given[n].shape for n in WEIGHT_NAMES}

    def as2d(a):
        return a.reshape(1, -1) if a.ndim == 1 else a.reshape(a.shape[-2], a.shape[-1])

    params = {n: as2d(given[n]) for n in WEIGHT_NAMES}
    moments_m = {n: as2d(given["m_" + n]) for n in WEIGHT_NAMES}
    moments_v = {n: as2d(given["v_" + n]) for n in WEIGHT_NAMES}
    loss, grad_x, grads, delta, new_m, new_v = _step(x[0], loss_target[0], params, moments_m, moments_v)
    back = lambda d: [d[n].reshape(shapes[n]) for n in WEIGHT_NAMES]
    return (loss, grad_x[None], *back(grads), *back(delta), *back(new_m), *back(new_v))
```

```python
import functools

import jax
import jax.numpy as jnp
from jax import lax
from jax.experimental import pallas as pl
from jax.experimental.pallas import tpu as pltpu

F32 = jnp.float32
BF16 = jnp.bfloat16
GRAD_WIRE = jnp.bfloat16

N_DEV = 8
HEAD_DIM = 64
LANES = 128
DILATED_PATTERNS = ((128, 1), (512, 4), (2048, 16))
DIL_BLOCK = 128
DIL_SUPER = 2048
DIL_UNROLL = 16
SB_TILE = 256
SB_LANES = 128
SB_UNROLL = 4
SB_STEP_TILES = 2
SB_DEAD = 90.0
SB_UNSEEN = -1e30
ROPE_THETA = 10000.0
RMS_EPS = 1e-6
HALF_STEP = 0.5
ADAM_LR = 0.001
ADAM_B1 = 0.9
ADAM_B2 = 0.999
ADAM_EPS = 1e-08
ADAM_WD = 0.01
ADAM_STEP = 10
NEG_BIG = -1e30
VMEM_CAP_MB = 60


def _pick(n, prefs):
    for p in prefs:
        if n % p == 0:
            return p
    return n


MM_MAX_TILE = 1536
MM_WHOLE = 3072


def _largest_tile(n, cap):
    if n <= cap:
        return n
    for t in range(cap - cap % LANES, 0, -LANES):
        if n % t == 0:
            return t
    return n


def _cparams(sem=None, vmem_mb=48):
    return pltpu.CompilerParams(dimension_semantics=sem, vmem_limit_bytes=min(vmem_mb, VMEM_CAP_MB) * 1024 * 1024)


def _nbytes(shape, dtype):
    n = 1
    for s in shape:
        n *= s
    return n * jnp.dtype(dtype).itemsize


def _mm(a, b, *, name, ta=False, tb=False, outs=(F32,), res=None, alpha=1.0, extras=(), epilogue=None,
        tm=None, tn=None, tk=None, comm=None, rows=(), lanes=(), row_sums=0, b_cols=None, second=None):
    if ta:
        K, M = a.shape
    else:
        M, K = a.shape
    if tb:
        N, Kb = b.shape
    else:
        Kb, N = b.shape
    col0 = 0
    if b_cols is not None:
        col0, N = b_cols
    assert K == Kb, (a.shape, b.shape, ta, tb)
    tn = tn or (N if (not ta and K <= MM_WHOLE and N <= MM_WHOLE) else _largest_tile(N, MM_MAX_TILE))
    wide = tn > MM_MAX_TILE and (len(extras) + len(outs) > 3 or a.dtype == F32)
    tm = tm or (_largest_tile(M, MM_MAX_TILE) if ta else _pick(M, (256, 128) if wide else (512, 256, 128)))
    tk = tk or (K if K <= MM_WHOLE else _pick(K, (2048, 1024, 512, 256, 128)))
    nk = K // tk
    a_spec = pl.BlockSpec((tk, tm), lambda i, j, k: (k, i)) if ta else pl.BlockSpec((tm, tk), lambda i, j, k: (i, k))
    assert col0 % tn == 0
    b_spec = (pl.BlockSpec((tn, tk), lambda i, j, k: (j + col0 // tn, k)) if tb
              else pl.BlockSpec((tk, tn), lambda i, j, k: (k, j + col0 // tn)))
    mn_spec = pl.BlockSpec((tm, tn), lambda i, j, k: (i, j))
    dims = (((0 if ta else 1,), (1 if tb else 0,)), ((), ()))
    row_spec = pl.BlockSpec((1, tn), lambda i, j, k: (0, j))
    lane_spec = pl.BlockSpec((tm, LANES), lambda i, j, k: (i, 0))
    n_extra = len(extras) + (1 if res is not None else 0) + len(rows) + len(lanes)
    n_mn = len(outs)
    n_out = n_mn + row_sums
    assert row_sums == 0 or tn == N
    grid = (M // tm, N // tn, nk)
    n_ab = 2 if second is None else 4
    hosted = _Hosted(comm, n_in=n_ab + n_extra, n_out=n_out, n_scratch=1 if nk > 1 else 0)

    def body(*refs):
        in_refs = refs[n_ab:n_ab + n_extra]
        ab_refs = refs[:n_ab]
        refs = hosted.begin(refs, grid)
        out_refs = refs[n_ab + n_extra:n_ab + n_extra + n_out]
        prod = None
        for a_ref, b_ref in zip(ab_refs[0::2], ab_refs[1::2]):
            part = lax.dot_general(a_ref[...].astype(BF16), b_ref[...].astype(BF16), dims, preferred_element_type=F32)
            prod = part if prod is None else prod + part

        def finish(acc):
            blocks = [r[...] for r in in_refs]
            if res is not None:
                r_blk, blocks = blocks[0], blocks[1:]
            else:
                r_blk = None
            if epilogue is None:
                val = acc * alpha
                if r_blk is not None:
                    val = val + r_blk
                vals = (val,)
            else:
                vals = epilogue(acc, r_blk, *blocks)
                vals = vals if isinstance(vals, (tuple, list)) else (vals,)
            for o_ref, v in zip(out_refs[:n_mn], vals[:n_mn]):
                o_ref[...] = v.astype(o_ref.dtype)
            first_rows = pl.program_id(0) == 0
            for o_ref, part in zip(out_refs[n_mn:], vals[n_mn:]):
                @pl.when(first_rows)
                def _(o_ref=o_ref, part=part):
                    o_ref[...] = part

                @pl.when(jnp.logical_not(first_rows))
                def _(o_ref=o_ref, part=part):
                    o_ref[...] += part

        if nk == 1:
            finish(prod)
        else:
            acc_ref = refs[n_ab + n_extra + n_out]
            k = pl.program_id(2)

            @pl.when(k == 0)
            def _():
                acc_ref[...] = prod

            @pl.when(k > 0)
            def _():
                acc_ref[...] += prod

            @pl.when(k == nk - 1)
            def _():
                finish(acc_ref[...])

        hosted.end(grid)

    mn_operands = ([res] if res is not None else []) + list(extras)
    ab = [a, b] + (list(second) if second is not None else [])
    operands = ab + mn_operands + list(rows) + list(lanes)
    in_specs = [a_spec, b_spec] * (n_ab // 2) + [mn_spec] * len(mn_operands) + [row_spec] * len(rows) + [lane_spec] * len(lanes)
    est = n_ab * (_nbytes((tm, tk), a.dtype) + _nbytes((tk, tn), b.dtype))
    est += 2 * sum(_nbytes((tm, tn), o.dtype) for o in mn_operands)
    est += 2 * sum(_nbytes((tm, tn), d) for d in outs) + 2 * _nbytes((tm, tn), F32)
    semantics = ("parallel", "parallel", "arbitrary") if row_sums == 0 else ("arbitrary",) * 3
    result = pl.pallas_call(
        body, name=name, grid=grid,
        out_shape=[jax.ShapeDtypeStruct((M, N), d) for d in outs]
        + [jax.ShapeDtypeStruct((1, N), F32)] * row_sums + hosted.out_shapes,
        in_specs=in_specs + hosted.in_specs,
        out_specs=[mn_spec] * n_mn + [row_spec] * row_sums + hosted.out_specs,
        scratch_shapes=([pltpu.VMEM((tm, tn), F32)] if nk > 1 else []) + hosted.scratch,
        compiler_params=_cparams(hosted.semantics(semantics), vmem_mb=max(32, 2 * est // (1024 * 1024))),
    )(*operands, *hosted.operands)
    own, got = result[:n_out], list(result[n_out:])
    own = own[0] if n_out == 1 else own
    return own if comm is None else (own, got)


def _rms_hat(x):
    r = lax.rsqrt(jnp.mean(x * x, axis=-1, keepdims=True) + RMS_EPS)
    return x * r, r


def _rms_fwd(xs, gains, *, name, comm=None):
    S = xs[0].shape[0]
    widths = [x.shape[1] for x in xs]
    tm = _pick(S, (512, 256, 128))
    n = len(xs)
    grid = (S // tm,)
    hosted = _Hosted(comm, n_in=2 * n, n_out=1, n_scratch=0)

    def body(*refs):
        refs = hosted.begin(refs, grid)
        o_ref = refs[2 * n]
        off = 0
        for i in range(n):
            xh, _ = _rms_hat(refs[i][...])
            o_ref[:, off:off + widths[i]] = (xh * refs[n + i][...]).astype(o_ref.dtype)
            off += widths[i]
        hosted.end(grid)

    out, *got = pl.pallas_call(
        body, name=name, grid=grid,
        out_shape=[jax.ShapeDtypeStruct((S, sum(widths)), BF16)] + hosted.out_shapes,
        in_specs=[pl.BlockSpec((tm, w), lambda i: (i, 0)) for w in widths]
        + [pl.BlockSpec((1, w), lambda i: (0, 0)) for w in widths] + hosted.in_specs,
        out_specs=[pl.BlockSpec((tm, sum(widths)), lambda i: (i, 0))] + hosted.out_specs,
        scratch_shapes=hosted.scratch,
        compiler_params=_cparams(hosted.semantics(("parallel",))),
    )(*xs, *gains, *hosted.operands)
    return out if comm is None else (out, got)


def _sigmoid(g):
    return 1.0 / (1.0 + jnp.exp(-g))


def _ride(result, plan):
    return result if plan is not None else (result, None)


def _residual_then_norm(alpha):
    def epilogue(acc, res, gain):
        y = res + alpha * acc
        return y, _rms_hat(y)[0] * gain
    return epilogue


def _ffn_fwd(x, gain, w, *, tag, ex, first_rider=None, riders=(None, None, None), head=None, h=None,
             next_gain=None):
    if h is None:
        plan = ex.gather(first_rider)
        h, got = _ride(_rms_fwd([x], [gain], name=f"{tag}_norm", comm=plan), plan)
        ex.gathered(first_rider, got, w)
    plan = ex.gather(riders[0])
    g, got = _ride(_mm(h, w[f"{tag}_w_gate"], outs=(BF16,), name=f"{tag}_gate", comm=plan), plan)
    ex.gathered(riders[0], got, w)

    def act(acc, _, g_blk):
        gf = g_blk.astype(F32)
        return acc, gf * _sigmoid(gf) * acc

    plan = ex.gather(riders[1])
    (u, a), got = _ride(_mm(h, w[f"{tag}_w_up"], outs=(BF16, BF16), extras=(g,), epilogue=act, name=f"{tag}_up_act",
                            comm=plan), plan)
    ex.gathered(riders[1], got, w)
    plan = ex.gather(riders[2])
    if head is None and next_gain is None:
        y, got = _ride(_mm(a, w[f"{tag}_w_down"], res=x, alpha=HALF_STEP, name=f"{tag}_down", comm=plan), plan)
    elif head is None:
        y, got = _ride(_mm(a, w[f"{tag}_w_down"], res=x, rows=(next_gain,), outs=(F32, BF16),
                           epilogue=_residual_then_norm(HALF_STEP), name=f"{tag}_down_norm", comm=plan), plan)
    else:
        final_gain, target = head
        y, got = _ride(_mm(a, w[f"{tag}_w_down"], res=x, extras=(target,), rows=(final_gain,), row_sums=2,
                           epilogue=_loss_head_epilogue, name=f"{tag}_down_loss", comm=plan), plan)
    ex.gathered(riders[2], got, w)
    return y, (h, g, u, a)


def _loss_head_epilogue(acc, x_in, target, gain):
    xh, r = _rms_hat(x_in + HALF_STEP * acc)
    err = xh * gain - target
    dy = err * (1.0 / acc.shape[1])
    dxh = dy * gain
    dx = r * (dxh - xh * jnp.mean(dxh * xh, axis=-1, keepdims=True))
    loss = 0.5 * jnp.sum(jnp.mean(err * err, axis=-1, keepdims=True), axis=0, keepdims=True)
    return dx, jnp.sum(dy * xh, axis=0, keepdims=True), jnp.zeros_like(gain) + loss


def _rms_bwd_epilogue(acc, dh_so_far, x, *dres_and_gain):
    gain = dres_and_gain[-1]
    dh = acc if dh_so_far is None else acc + dh_so_far
    xh, r = _rms_hat(x)
    dxh = dh * gain
    dx = r * (dxh - xh * jnp.mean(dxh * xh, axis=-1, keepdims=True))
    if len(dres_and_gain) == 2:
        dx = dx + dres_and_gain[0]
    return dx, jnp.sum(dh * xh, axis=0, keepdims=True)


def _ffn_bwd(dout, x, gain, w, saved, *, tag, ex, rider=(None, None), spread=False):
    h, g, u, a = saved
    wg, wu, wd = (w[f"{tag}_w_{n}"] for n in ("gate", "up", "down"))

    def act_bwd(acc, _, g_blk, u_blk):
        gf, uf = g_blk.astype(F32), u_blk.astype(F32)
        da = acc * HALF_STEP
        sig = _sigmoid(gf)
        silu = gf * sig
        return da * uf * (sig + silu * (1.0 - sig)), da * silu

    def carrying(group, grad, call):
        group = group if spread else None
        plan = ex.send(group, {group: grad})
        out, got = _ride(call(plan), plan)
        ex.received(group, got)
        return out

    plan = ex.send(*rider)
    (dg, du), got = _ride(_mm(dout, wd, tb=True, outs=(BF16, BF16), extras=(g, u), epilogue=act_bwd,
                              name=f"{tag}_bwd_act", comm=plan), plan)
    ex.received(rider[0], got)
    dwg = _mm(h, dg, ta=True, outs=(GRAD_WIRE,), name=f"{tag}_dwg")
    dwu = carrying(f"{tag}_w_gate", dwg, lambda plan: _mm(h, du, ta=True, outs=(GRAD_WIRE,), name=f"{tag}_dwu", comm=plan))
    dwd = carrying(f"{tag}_w_up", dwu,
                   lambda plan: _mm(a, dout, ta=True, outs=(GRAD_WIRE,), alpha=HALF_STEP, name=f"{tag}_dwd", comm=plan))
    dx, dgain = carrying(f"{tag}_w_down", dwd, lambda plan: _mm(
        dg, wg, tb=True, second=(du, wu), extras=(x, dout), rows=(gain,), row_sums=1, epilogue=_rms_bwd_epilogue,
        tm=256, name=f"{tag}_dh_norm_bwd", comm=plan))
    return dx, dgain, dwg, dwu, dwd


def _rope_tables(S):
    half = HEAD_DIM // 2
    inv_freq = ROPE_THETA ** (-jnp.arange(half, dtype=F32) / half)
    ang = jnp.arange(S, dtype=F32)[:, None] * inv_freq[None, :]
    cos, sin = jnp.cos(ang), jnp.sin(ang)
    reps = LANES // HEAD_DIM
    cos_t = jnp.tile(jnp.concatenate([cos, cos], axis=1), (1, reps))
    sin_t = jnp.tile(jnp.concatenate([-sin, sin], axis=1), (1, reps))
    return cos_t, sin_t


def _rotate(v, cos, sin, sign):
    half = HEAD_DIM // 2
    groups = []
    for g in range(v.shape[1] // LANES):
        t = v[:, g * LANES:(g + 1) * LANES]
        lane = lax.broadcasted_iota(jnp.int32, t.shape, 1)
        swapped = jnp.where(lane % HEAD_DIM < half, pltpu.roll(t, LANES - half, axis=1), pltpu.roll(t, half, axis=1))
        groups.append(t * cos + swapped * (sin * sign))
    return groups[0] if len(groups) == 1 else jnp.concatenate(groups, axis=1)


def _join_d_proj(pieces, rotated, cos_t, sin_t, *, name):
    S = pieces[0].shape[0]
    widths = [p.shape[1] for p in pieces]
    tm = _pick(S, (256, 128))
    n = len(pieces)

    def body(*refs):
        c_ref, s_ref, o_ref = refs[n], refs[n + 1], refs[n + 2]
        off = 0
        for i in range(n):
            v = refs[i][...]
            if i in rotated:
                v = _rotate(v, c_ref[...], s_ref[...], -1.0)
            o_ref[:, off:off + widths[i]] = v.astype(o_ref.dtype)
            off += widths[i]

    return pl.pallas_call(
        body, name=name, grid=(S // tm,),
        out_shape=jax.ShapeDtypeStruct((S, sum(widths)), BF16),
        in_specs=[pl.BlockSpec((tm, w), lambda i: (i, 0)) for w in widths]
        + [pl.BlockSpec((tm, LANES), lambda i: (i, 0))] * 2,
        out_specs=pl.BlockSpec((tm, sum(widths)), lambda i: (i, 0)),
        compiler_params=_cparams(("parallel",)),
    )(*pieces, cos_t, sin_t)


def _head_masks(shape):
    lane = lax.broadcasted_iota(jnp.int32, shape, 1)
    return [(lane >= HEAD_DIM * h) & (lane < HEAD_DIM * (h + 1)) for h in range(shape[1] // HEAD_DIM)]


def _sb_scores(q2, k_j):
    z = lax.dot_general(q2, k_j, (((1,), (1,)), ((), ())), preferred_element_type=F32)
    sign_bit = jnp.int32(-2 ** 31)
    minus_abs = lax.bitcast_convert_type(lax.bitcast_convert_type(z, jnp.int32) | sign_bit, F32)
    softplus = jnp.maximum(z, 0.0) + jnp.log(1.0 + jnp.exp(minus_abs))
    return z - softplus, softplus


def _sb_stack_heads(t, scale=None):
    parts = [jnp.where(hm, t, jnp.zeros_like(t)) for hm in _head_masks(t.shape)]
    t2 = jnp.concatenate(parts, axis=0)
    if scale is not None:
        t2 = (t2.astype(F32) * scale).astype(t2.dtype)
    return t2


def _sb_unstack_heads(t2):
    n = t2.shape[1] // HEAD_DIM
    T = t2.shape[0] // n
    masks = _head_masks((T, t2.shape[1]))
    out = t2[:T]
    for h in range(1, n):
        out = jnp.where(masks[h], t2[h * T:(h + 1) * T], out)
    return out


def _sb_causal(T, n_heads):
    row = lax.broadcasted_iota(jnp.int32, (n_heads * T, T), 0)
    col = lax.broadcasted_iota(jnp.int32, (n_heads * T, T), 1)
    return col < row % T


def _sb_triangle(T, later):
    row = lax.broadcasted_iota(jnp.int32, (T, T), 0)
    col = lax.broadcasted_iota(jnp.int32, (T, T), 1)
    return ((row > col) if later else (row < col)).astype(BF16)


def _sb_fwd(p_sb, *, name, comm=None):
    S = p_sb.shape[0]
    W = p_sb.shape[1] // 3
    LW = min(SB_LANES, W)
    NH = LW // HEAD_DIM
    npair = W // LW
    T = SB_TILE
    n_tiles = S // T
    assert n_tiles <= HEAD_DIM
    scale = HEAD_DIM ** -0.5

    R = SB_STEP_TILES
    grid = (npair, n_tiles // R)
    hosted = _Hosted(comm, n_in=5, n_out=2, n_scratch=0)

    def body(*refs):
        refs = hosted.begin(refs, grid)
        step = pl.program_id(1)
        lax.fori_loop(0, R, lambda sub, _: query_tile(step * R + sub, sub, *refs), 0)
        hosted.end(grid)

    def query_tile(I, sub, q_ref, k_ref, v_ref, causal_ref, later_ref, o_ref, c_ref):
        rows = pl.ds(pl.multiple_of(sub * T, T), T)
        lane = lax.broadcasted_iota(jnp.int32, (T, LW), 1)
        causal = causal_ref[...]
        later_than = later_ref[...]
        q2 = _sb_stack_heads(q_ref[rows, :], scale)

        def scores(J, diag):
            off = pl.multiple_of(J * T, T)
            log_beta, stay = _sb_scores(q2, k_ref[pl.ds(off, T), :])
            if diag:
                stay = stay * causal
            local = jnp.dot(stay.astype(BF16), later_than, preferred_element_type=F32)
            return log_beta, local, jnp.sum(stay, axis=1, keepdims=True), v_ref[pl.ds(off, T), :]

        def weigh(J, sc, gone, acc, carr, diag):
            log_beta, local, _, v_j = sc
            w = jnp.exp((log_beta - gone) - local)
            if diag:
                w = w * causal
            acc = acc + jnp.dot(w.astype(BF16), v_j, preferred_element_type=F32)
            for h in range(NH):
                carr = jnp.where(lane == HEAD_DIM * h + J, -gone[h * T:(h + 1) * T], carr)
            return acc, carr

        def tiles(J, count, state, diag):
            gone, acc, carr, _ = state
            scs = [scores(J - u, diag and u == 0) for u in range(count)]
            for u, sc in enumerate(scs):
                acc, carr = weigh(J - u, sc, gone, acc, carr, diag and u == 0)
                gone = gone + sc[2]
            return gone, acc, carr, jnp.min(gone)

        U = SB_UNROLL
        alive = lambda st: st[3] < SB_DEAD
        state = (jnp.zeros((NH * T, 1), F32), jnp.zeros((NH * T, LW), F32),
                 jnp.full((T, LW), SB_UNSEEN, F32), jnp.zeros((), F32))
        state = lax.cond(I > 0, lambda st: tiles(I, 2, st, True), lambda st: tiles(I, 1, st, True), state)
        rest = jnp.maximum(I - 1, 0)
        singles = jnp.where(rest > 0, (rest - 1) % U + 1, 0)
        _, state = lax.while_loop(lambda c: (c[0] < singles) & alive(c[1]),
                                  lambda c: (c[0] + 1, tiles(I - 2 - c[0], 1, c[1], False)), (jnp.int32(0), state))
        blocks = (rest - singles) // U
        _, state = lax.while_loop(lambda c: (c[0] < blocks) & alive(c[1]),
                                  lambda c: (c[0] + 1, tiles(I - 2 - singles - U * c[0], U, c[1], False)),
                                  (jnp.int32(0), state))
        _, acc, carr, _ = state
        o_ref[rows, :] = _sb_unstack_heads(acc)
        c_ref[rows, :] = carr
        return 0

    blk = lambda I_off: pl.BlockSpec((R * T, LW), lambda p, I: (I, I_off + p))
    full = lambda off: pl.BlockSpec((S, LW), lambda p, I: (0, off + p))
    const = lambda rows: pl.BlockSpec((rows, T), lambda p, I: (0, 0))
    o, carries, *got = pl.pallas_call(
        body, name=name, grid=grid,
        out_shape=[jax.ShapeDtypeStruct((S, W), F32), jax.ShapeDtypeStruct((S, W), F32)] + hosted.out_shapes,
        in_specs=[blk(0), full(npair), full(2 * npair), const(NH * T), const(T)] + hosted.in_specs,
        out_specs=[blk(0), blk(0)] + hosted.out_specs,
        scratch_shapes=hosted.scratch,
        compiler_params=_cparams(hosted.semantics(("parallel", "arbitrary")), vmem_mb=56),
    )(p_sb, p_sb, p_sb, _sb_causal(T, NH).astype(F32), _sb_triangle(T, True), *hosted.operands)
    return (o, carries) if comm is None else (o, carries, got)


def _sb_bwd(p_sb, do, carries, *, name, comm=None):
    S = p_sb.shape[0]
    W = p_sb.shape[1] // 3
    LW = min(LANES, W)
    NH = LW // HEAD_DIM
    npair = W // LW
    T = SB_TILE
    n_tiles = S // T
    scale = HEAD_DIM ** -0.5

    R = SB_STEP_TILES
    grid = (npair, n_tiles // R)
    hosted = _Hosted(comm, n_in=8, n_out=3, n_scratch=0)

    def body(*refs):
        refs = hosted.begin(refs, grid)
        dk_ref, dv_ref = refs[9], refs[10]
        step = pl.program_id(1)

        @pl.when(step == 0)
        def _():
            dk_ref[...] = jnp.zeros_like(dk_ref)
            dv_ref[...] = jnp.zeros_like(dv_ref)

        lax.fori_loop(0, R, lambda sub, _: query_tile(step * R + sub, sub, *refs), 0)
        hosted.end(grid)

    def query_tile(I, sub, q_ref, k_ref, v_ref, do_ref, c_ref, causal_ref, later_ref, earlier_ref,
                   dq_ref, dk_ref, dv_ref):
        rows = pl.ds(pl.multiple_of(sub * T, T), T)
        lane = lax.broadcasted_iota(jnp.int32, (T, LW), 1)
        causal = causal_ref[...]
        later_than = later_ref[...]
        earlier_than = earlier_ref[...]
        q2 = _sb_stack_heads(q_ref[rows, :], scale)
        do2 = _sb_stack_heads(do_ref[rows, :].astype(BF16))
        carr = c_ref[rows, :]
        tn_dims = (((0,), (0,)), ((), ()))

        def chain(J, diag):
            off = pl.multiple_of(J * T, T)
            k_j = k_ref[pl.ds(off, T), :]
            v_j = v_ref[pl.ds(off, T), :]
            log_beta, stay = _sb_scores(q2, k_j)
            if diag:
                stay = stay * causal
            lc = jnp.concatenate(
                [jnp.sum(jnp.where(lane == HEAD_DIM * h + J, carr, 0.0), axis=1, keepdims=True) for h in range(NH)],
                axis=0)
            w = jnp.exp((log_beta + lc) - jnp.dot(stay.astype(BF16), later_than, preferred_element_type=F32))
            if diag:
                w = w * causal
            dw = lax.dot_general(do2, v_j, (((1,), (1,)), ((), ())), preferred_element_type=F32)
            e = w * dw
            local = jnp.dot(e.astype(BF16), earlier_than, preferred_element_type=F32)
            return off, k_j, w, e, local, jnp.exp(log_beta), jnp.sum(e, axis=1, keepdims=True)

        def finish(ch, ec, dq_acc, diag):
            off, k_j, w, e, local, beta, _ = ch
            e_before = local + ec
            dz = e - beta * (e + e_before)
            if diag:
                dz = dz * causal
            dzb = dz.astype(BF16)
            dq_acc = dq_acc + jnp.dot(dzb, k_j, preferred_element_type=F32)
            dk_ref[pl.ds(off, T), :] += lax.dot_general(dzb, q2, tn_dims, preferred_element_type=F32)
            dv_ref[pl.ds(off, T), :] += lax.dot_general(w.astype(BF16), do2, tn_dims, preferred_element_type=F32)
            return dq_acc

        def tiles(J, count, state, diag):
            ec, dq_acc = state
            chains = [chain(J + u, diag and u == count - 1) for u in range(count)]
            for u, ch in enumerate(chains):
                dq_acc = finish(ch, ec, dq_acc, diag and u == count - 1)
                ec = ec + ch[6]
            return ec, dq_acc

        lane_row = lax.broadcasted_iota(jnp.int32, (1, LW), 1)
        reached = (jnp.max(carr, axis=0, keepdims=True) > 0.5 * SB_UNSEEN) & (lane_row < HEAD_DIM)
        first = jnp.min(jnp.where(reached, lane_row.astype(F32), float(n_tiles))).astype(jnp.int32)
        U = SB_UNROLL
        count = I - first
        rest = jnp.maximum(count - 1, 0)
        state = (jnp.zeros((NH * T, 1), F32), jnp.zeros((NH * T, LW), F32))
        state = lax.fori_loop(0, rest // U, lambda jj, st: tiles(first + U * jj, U, st, False), state)
        state = lax.fori_loop(0, rest % U, lambda r, st: tiles(I - 1 - rest % U + r, 1, st, False), state)
        _, dq_acc = lax.cond(count > 0, lambda st: tiles(I - 1, 2, st, True), lambda st: tiles(I, 1, st, True), state)
        dq_ref[rows, :] = _sb_unstack_heads(dq_acc) * scale
        return 0

    blk = lambda src_off: pl.BlockSpec((R * T, LW), lambda p, I: (I, src_off + p))
    full = lambda off: pl.BlockSpec((S, LW), lambda p, I: (0, off + p))
    const = lambda rows: pl.BlockSpec((rows, T), lambda p, I: (0, 0))
    dq, dk, dv, *got = pl.pallas_call(
        body, name=name, grid=grid,
        out_shape=[jax.ShapeDtypeStruct((S, W), F32)] * 3 + hosted.out_shapes,
        in_specs=[blk(0), full(npair), full(2 * npair), blk(0), blk(0), const(NH * T), const(T), const(T)]
        + hosted.in_specs,
        out_specs=[blk(0), full(0), full(0)] + hosted.out_specs,
        scratch_shapes=hosted.scratch,
        compiler_params=_cparams(hosted.semantics(("parallel", "arbitrary")), vmem_mb=56),
    )(p_sb, p_sb, p_sb, do, carries, _sb_causal(T, NH).astype(F32), _sb_triangle(T, True), _sb_triangle(T, False),
      *hosted.operands)
    return (dq, dk, dv) if comm is None else (dq, dk, dv, got)


def _dil_blocks(b, body_fn):
    for pi, (window, dil) in enumerate(DILATED_PATTERNS):
        assert window // dil == DIL_BLOCK
        nblk = DIL_SUPER // (DIL_BLOCK * dil)
        assert (dil * nblk) % DIL_UNROLL == 0

        def group(g, _, pi=pi, dil=dil, nblk=nblk):
            for u in range(DIL_UNROLL):
                t = g * DIL_UNROLL + u
                n = t % nblk
                body_fn(pi, dil, t // nblk, n, b * nblk + n)
            return 0

        lax.fori_loop(0, dil * nblk // DIL_UNROLL, group, 0)


def _dil_rows(start, size, dil):
    if dil == 1:
        return pl.ds(pl.multiple_of(start, DIL_BLOCK), size)
    return pl.ds(start, size, stride=dil)


def _dil_fill_bias(bias_ref):
    row = lax.broadcasted_iota(jnp.int32, (2 * DIL_BLOCK, 2 * DIL_BLOCK), 0)
    kk = lax.broadcasted_iota(jnp.int32, (2 * DIL_BLOCK, 2 * DIL_BLOCK), 1)
    qi = jnp.where(row >= DIL_BLOCK, row - DIL_BLOCK, row)
    for s in range(2):
        dist = s * DIL_BLOCK + qi - kk
        bias_ref[s] = jnp.where((dist >= 0) & (dist <= DIL_BLOCK), 0.0, NEG_BIG)


def _dl_fwd(p_dl, *, name):
    S, W = p_dl.shape[0], p_dl.shape[1] // 3
    npair = W // LANES
    nsuper = S // DIL_SUPER
    assert S % DIL_SUPER == 0 and S // max(d for _, d in DILATED_PATTERNS) >= 2 * DIL_BLOCK
    scale = HEAD_DIM ** -0.5
    npat = len(DILATED_PATTERNS)

    def body(q_ref, k_ref, v_ref, o_ref, l_ref, bias_ref, *pattern_refs):
        op_refs, lp_refs = pattern_refs[:npat], pattern_refs[npat:]
        b = pl.program_id(1)
        masks = _head_masks((DIL_BLOCK, LANES))
        pl.when(b == 0)(lambda: _dil_fill_bias(bias_ref))

        def block(pi, dil, c, n, gn):
            ws = jnp.maximum(gn - 1, 0)
            qrows = n * (DIL_BLOCK * dil) + c
            krows = ws * (DIL_BLOCK * dil) + c
            q_idx = _dil_rows(qrows, DIL_BLOCK, dil)
            k_idx = _dil_rows(krows, 2 * DIL_BLOCK, dil)
            qb = q_ref[q_idx, :]
            kb = k_ref[k_idx, :].astype(BF16)
            vb = v_ref[k_idx, :].astype(BF16)
            q2 = _sb_stack_heads(qb.astype(BF16), scale)
            z = lax.dot_general(q2, kb, (((1,), (1,)), ((), ())), preferred_element_type=F32) + bias_ref[gn - ws]
            m = jnp.max(z, axis=1, keepdims=True)
            p = jnp.exp(z - m)
            den = jnp.sum(p, axis=1, keepdims=True)
            acc = jnp.dot(p.astype(BF16), vb, preferred_element_type=F32)
            lse = m + jnp.log(den)
            op_refs[pi][q_idx, :] = _sb_unstack_heads(acc / den)
            lp_refs[pi][q_idx, :] = jnp.where(masks[0], lse[:DIL_BLOCK], lse[DIL_BLOCK:])

        _dil_blocks(b, block)
        lses = [r[...] for r in lp_refs]
        top = functools.reduce(jnp.maximum, lses)
        ws_ = [jnp.exp(l - top) for l in lses]
        den = functools.reduce(jnp.add, ws_)
        num = functools.reduce(jnp.add, [w * r[...] for r, w in zip(op_refs, ws_)])
        o_ref[...] = num / den
        l_ref[...] = top + jnp.log(den)

    blk = pl.BlockSpec((DIL_SUPER, LANES), lambda p, b: (b, p))
    full = lambda off: pl.BlockSpec((S, LANES), lambda p, b: (0, off + p))
    return pl.pallas_call(
        body, name=name, grid=(npair, nsuper),
        out_shape=[jax.ShapeDtypeStruct((S, W), F32)] * 2,
        in_specs=[blk, full(npair), full(2 * npair)], out_specs=[blk, blk],
        scratch_shapes=[pltpu.VMEM((2, 2 * DIL_BLOCK, 2 * DIL_BLOCK), F32)]
        + [pltpu.VMEM((DIL_SUPER, LANES), F32)] * (2 * npat),
        compiler_params=_cparams(("arbitrary", "arbitrary")),
    )(p_dl, p_dl, p_dl)


def _dl_bwd(p_dl, o, lse, do, *, name):
    S, W = p_dl.shape[0], p_dl.shape[1] // 3
    npair = W // LANES
    nsuper = S // DIL_SUPER
    scale = HEAD_DIM ** -0.5

    def body(q_ref, k_ref, v_ref, o_ref, l_ref, do_ref, dq_ref, dk_ref, dv_ref, delta_ref, bias_ref):
        b = pl.program_id(1)

        @pl.when(b == 0)
        def _():
            dk_ref[...] = jnp.zeros_like(dk_ref)
            dv_ref[...] = jnp.zeros_like(dv_ref)
            _dil_fill_bias(bias_ref)

        dq_ref[...] = jnp.zeros_like(dq_ref)
        prod = do_ref[...] * o_ref[...]
        delta = jnp.zeros_like(prod)
        for hm in _head_masks(prod.shape):
            delta = jnp.where(hm, jnp.sum(jnp.where(hm, prod, 0.0), axis=1, keepdims=True), delta)
        delta_ref[...] = delta

        def block(pi, dil, c, n, gn):
            ws = jnp.maximum(gn - 1, 0)
            qrows = n * (DIL_BLOCK * dil) + c
            krows = ws * (DIL_BLOCK * dil) + c
            q_idx = _dil_rows(qrows, DIL_BLOCK, dil)
            k_idx = _dil_rows(krows, 2 * DIL_BLOCK, dil)
            qb = q_ref[q_idx, :]
            dob = do_ref[q_idx, :]
            lb = l_ref[q_idx, :]
            db = delta_ref[q_idx, :]
            kb = k_ref[k_idx, :].astype(BF16)
            vb = v_ref[k_idx, :].astype(BF16)
            q2 = _sb_stack_heads(qb.astype(BF16), scale)
            do2 = _sb_stack_heads(dob.astype(BF16))
            lse2 = jnp.concatenate([lb[:, HEAD_DIM * h:HEAD_DIM * h + 1] for h in range(2)], axis=0)
            delta2 = jnp.concatenate([db[:, HEAD_DIM * h:HEAD_DIM * h + 1] for h in range(2)], axis=0)
            z = lax.dot_general(q2, kb, (((1,), (1,)), ((), ())), preferred_element_type=F32)
            p = jnp.exp((z + bias_ref[gn - ws]) - lse2)
            dp = lax.dot_general(do2, vb, (((1,), (1,)), ((), ())), preferred_element_type=F32)
            dzb = (p * (dp - delta2)).astype(BF16)
            tn_dims = (((0,), (0,)), ((), ()))
            dq_blk = _sb_unstack_heads(jnp.dot(dzb, kb, preferred_element_type=F32)) * scale
            dk_blk = lax.dot_general(dzb, q2, tn_dims, preferred_element_type=F32)
            dv_blk = lax.dot_general(p.astype(BF16), do2, tn_dims, preferred_element_type=F32)
            dq_ref[q_idx, :] = dq_ref[q_idx, :] + dq_blk
            dk_ref[k_idx, :] = dk_ref[k_idx, :] + dk_blk
            dv_ref[k_idx, :] = dv_ref[k_idx, :] + dv_blk

        _dil_blocks(b, block)

    blk = pl.BlockSpec((DIL_SUPER, LANES), lambda p, b: (b, p))
    full = lambda off: pl.BlockSpec((S, LANES), lambda p, b: (0, off + p))
    return pl.pallas_call(
        body, name=name, grid=(npair, nsuper),
        out_shape=[jax.ShapeDtypeStruct((S, W), F32)] * 3,
        in_specs=[blk, full(npair), full(2 * npair), blk, blk, blk], out_specs=[blk, full(0), full(0)],
        scratch_shapes=[pltpu.VMEM((DIL_SUPER, LANES), F32), pltpu.VMEM((2, 2 * DIL_BLOCK, 2 * DIL_BLOCK), F32)],
        compiler_params=_cparams(("arbitrary", "arbitrary")),
    )(p_dl, p_dl, p_dl, o, lse, do)


class _NoExchange:
    def gather(self, family):
        return None

    def gathered(self, family, got, weights):
        pass

    def send(self, family, grads):
        return None

    def received(self, family, got):
        pass


def _local_step(x, target, gains, weights, exchanges=None):
    S, D = x.shape
    ex = exchanges or _NoExchange()
    weights = dict(weights)
    d_sb = gains["sb_out_norm"].shape[1]
    d_dl = gains["dil_out_norm"].shape[1]
    cos_t, sin_t = _rope_tables(S)

    riders = ("ffn1_w_up", "ffn1_w_down", "mixer") if exchanges else (None, None, None)
    (x1, h2), saved1 = _ffn_fwd(x, gains["ffn1_norm"], weights, tag="ffn1", ex=ex, riders=riders,
                                first_rider="ffn1_w_gate" if exchanges else None,
                                next_gain=gains["mix_norm"])
    w_in = weights["w_in"]
    w_out = weights["w_out"]
    p_sb = _mm(h2, w_in, b_cols=(0, 3 * d_sb), outs=(BF16,), name="proj_sb")

    def rope_qk(acc, _, cos, sin):
        return jnp.concatenate([_rotate(acc[:, :2 * d_dl], cos, sin, 1.0), acc[:, 2 * d_dl:]], axis=1)

    p_dl = _mm(h2, w_in, b_cols=(3 * d_sb, 3 * d_dl), lanes=(cos_t, sin_t), epilogue=rope_qk, name="proj_dl_rope")
    plan = ex.gather("ffn2" if exchanges else None)
    o_sb, carries, *got = _sb_fwd(p_sb, name="sb_fwd", comm=plan)
    ex.gathered("ffn2", got[0] if got else None, weights)
    o_dl, lse_dl = _dl_fwd(p_dl, name="dl_fwd")
    merged = _rms_fwd([o_sb, o_dl], [gains["sb_out_norm"], gains["dil_out_norm"]], name="out_norm")
    x2, h3 = _mm(merged, w_out, res=x1, rows=(gains["ffn2_norm"],), outs=(F32, BF16),
                 epilogue=_residual_then_norm(1.0), name="out_proj_norm")
    (dx3, d_final, loss_wide), saved2 = _ffn_fwd(x2, gains["ffn2_norm"], weights, tag="ffn2", ex=ex, h=h3,
                                                 head=(gains["final_norm"], target))
    loss_row = loss_wide[:, :LANES]

    dx2, d_ffn2_norm, dwg2, dwu2, dwd2 = _ffn_bwd(dx3, x2, gains["ffn2_norm"], weights, saved2, tag="ffn2", ex=ex)
    d_w_out = _mm(merged, dx2, ta=True, outs=(GRAD_WIRE,), name="d_w_out")
    do_sb, d_sb_norm = _mm(dx2, w_out, tb=True, b_cols=(0, d_sb), extras=(o_sb,), rows=(gains["sb_out_norm"],),
                           row_sums=1, epilogue=_rms_bwd_epilogue, name="d_merged_sb")
    do_dl, d_dl_norm = _mm(dx2, w_out, tb=True, b_cols=(d_sb, d_dl), extras=(o_dl,), rows=(gains["dil_out_norm"],),
                           row_sums=1, epilogue=_rms_bwd_epilogue, name="d_merged_dl")
    plan = ex.send("ffn2", dict(ffn2_w_gate=dwg2, ffn2_w_up=dwu2, ffn2_w_down=dwd2))
    dq_sb, dk_sb, dv_sb, *got = _sb_bwd(p_sb, do_sb, carries, name="sb_bwd", comm=plan)
    ex.received("ffn2", got[0] if got else None)
    dq_dl, dk_dl, dv_dl = _dl_bwd(p_dl, o_dl, lse_dl, do_dl, name="dl_bwd")
    d_proj = _join_d_proj([dq_sb, dk_sb, dv_sb, dq_dl, dk_dl, dv_dl], (3, 4), cos_t, sin_t, name="d_proj")
    d_w_in = _mm(h2, d_proj, ta=True, outs=(GRAD_WIRE,), name="d_w_in")
    dx1, d_mix_norm = _mm(d_proj, w_in, tb=True, extras=(x1, dx2), rows=(gains["mix_norm"],), row_sums=1,
                          epilogue=_rms_bwd_epilogue, name="dh_mix_norm_bwd")
    dx, d_ffn1_norm, dwg1, dwu1, dwd1 = _ffn_bwd(
        dx1, x, gains["ffn1_norm"], weights, saved1, tag="ffn1", ex=ex,
        rider=("mixer", dict(w_in=d_w_in, w_out=d_w_out)), spread=True)
    gain_grads = dict(ffn1_norm=d_ffn1_norm, mix_norm=d_mix_norm, sb_out_norm=d_sb_norm, dil_out_norm=d_dl_norm,
                      ffn2_norm=d_ffn2_norm, final_norm=d_final)
    weight_grads = dict(ffn1_w_gate=dwg1, ffn1_w_up=dwu1, ffn1_w_down=dwd1, w_in=d_w_in, w_out=d_w_out,
                        ffn2_w_gate=dwg2, ffn2_w_up=dwu2, ffn2_w_down=dwd2)
    return loss_row, dx, gain_grads, weight_grads


def _mesh_position():
    return lax.axis_index("x"), lax.axis_index("y"), lax.axis_index("c")


def _flip(coord, bit):
    return 1 - coord if bit else coord


RELATIONS = [(rx, ry, rc) for rx in (0, 1) for ry in (0, 1) for rc in (0, 1)][1:]


class _GatherPlan:
    def __init__(self, shards):
        n = len(shards)
        self.operands = list(shards)
        self.out_shapes = [jax.ShapeDtypeStruct((N_DEV,) + s.shape, s.dtype) for s in shards]
        self.scratch = [pltpu.SemaphoreType.DMA((n, 7)), pltpu.SemaphoreType.DMA((n, 7)),
                        pltpu.SemaphoreType.DMA((n,))]

    def _copies(self, in_refs, out_refs, sems):
        send_sems, recv_sems, local_sems = sems
        x, y, c = _mesh_position()
        me, sibling = (x, y, c), (x, y, 1 - c)
        chips = [(1 - x, y), (x, 1 - y), (1 - x, 1 - y)]
        plans = []
        for t, (x_ref, out_ref) in enumerate(zip(in_refs, out_refs)):
            def slot(px, py, pc, out_ref=out_ref):
                return out_ref.at[4 * px + 2 * py + pc]

            def copy(k, block, to, src=None, t=t, slot=slot):
                return pltpu.make_async_remote_copy(
                    src_ref=slot(*block) if src is None else src, dst_ref=slot(*block),
                    send_sem=send_sems.at[t, k], recv_sem=recv_sems.at[t, k],
                    device_id=to, device_id_type=pl.DeviceIdType.MESH)

            plans.append(dict(
                mine=pltpu.make_async_copy(x_ref, slot(*me), local_sems.at[t]),
                first=[copy(0, me, sibling, src=x_ref)]
                + [copy(1 + j, me, (*chip, c), src=x_ref) for j, chip in enumerate(chips)],
                over_ici=[copy(1 + j, (*chip, c), me) for j, chip in enumerate(chips)],
                passed=[copy(4 + j, (*chip, c), sibling) for j, chip in enumerate(chips)],
                from_sibling=[copy(0, sibling, me)] + [copy(4 + j, (*chip, 1 - c), me) for j, chip in enumerate(chips)]))
        return plans

    def start(self, in_refs, out_refs, sems):
        for p in self._copies(in_refs, out_refs, sems):
            p["mine"].start()
            for cp in p["first"]:
                cp.start()

    def finish(self, in_refs, out_refs, sems):
        plans = self._copies(in_refs, out_refs, sems)
        for p in plans:
            for arrived, onward in zip(p["over_ici"], p["passed"]):
                arrived.wait_recv()
                onward.start()
        for p in plans:
            for cp in p["from_sibling"]:
                cp.wait_recv()
            for cp in p["first"] + p["passed"]:
                cp.wait_send()
            p["mine"].wait()


class _Hosted:
    def __init__(self, plan, n_in, n_out, n_scratch):
        self.plan, self.n_in, self.n_out, self.n_scratch = plan, n_in, n_out, n_scratch
        self.operands = list(plan.operands) if plan else []
        self.out_shapes = list(plan.out_shapes) if plan else []
        self.scratch = list(plan.scratch) if plan else []
        self.in_specs = [pl.BlockSpec(memory_space=pl.ANY)] * len(self.operands)
        self.out_specs = [pl.BlockSpec(memory_space=pl.ANY)] * len(self.out_shapes)

    def semantics(self, sem):
        return sem if self.plan is None else ("arbitrary",) * len(sem)

    def _at(self, grid, last):
        hit = None
        for d, n in enumerate(grid):
            here = pl.program_id(d) == (n - 1 if last else 0)
            hit = here if hit is None else hit & here
        return hit

    def begin(self, refs, grid):
        if self.plan is None:
            return refs
        k_in, k_out = len(self.operands), len(self.out_shapes)
        ins, rest = refs[:self.n_in], refs[self.n_in:]
        c_in, rest = rest[:k_in], rest[k_in:]
        outs, rest = rest[:self.n_out], rest[self.n_out:]
        c_out, rest = rest[:k_out], rest[k_out:]
        scratch, sems = rest[:self.n_scratch], rest[self.n_scratch:]
        self._args = (c_in, c_out, sems)
        pl.when(self._at(grid, False))(lambda: self.plan.start(*self._args))
        return tuple(ins) + tuple(outs) + tuple(scratch)

    def end(self, grid):
        if self.plan is not None:
            pl.when(self._at(grid, True))(lambda: self.plan.finish(*self._args))


class _ExchangePlan:
    def __init__(self, packs):
        n = len(packs)
        self.operands = list(packs)
        self.out_shapes = [jax.ShapeDtypeStruct(p.shape, p.dtype) for p in packs]
        self.scratch = [pltpu.SemaphoreType.DMA((n, 7)), pltpu.SemaphoreType.DMA((n, 7)),
                        pltpu.SemaphoreType.DMA((n,))]

    def _copies(self, in_refs, out_refs, sems):
        send_sems, recv_sems, local_sems = sems
        x, y, c = _mesh_position()
        me = 4 * x + 2 * y + c
        copies = [pltpu.make_async_copy(i.at[me], o.at[me], local_sems.at[t])
                  for t, (i, o) in enumerate(zip(in_refs, out_refs))]
        for r, (rx, ry, rc) in enumerate(RELATIONS):
            px, py, pc = _flip(x, rx), _flip(y, ry), _flip(c, rc)
            peer = 4 * px + 2 * py + pc
            copies += [pltpu.make_async_remote_copy(
                src_ref=i.at[peer], dst_ref=o.at[me], send_sem=send_sems.at[t, r], recv_sem=recv_sems.at[t, r],
                device_id=(px, py, pc), device_id_type=pl.DeviceIdType.MESH)
                for t, (i, o) in enumerate(zip(in_refs, out_refs))]
        return copies

    def start(self, in_refs, out_refs, sems):
        for cp in self._copies(in_refs, out_refs, sems):
            cp.start()

    def finish(self, in_refs, out_refs, sems):
        for cp in self._copies(in_refs, out_refs, sems):
            cp.wait()


def _all_reduce_rows(v, *, name):
    R, C = v.shape

    def body(v_ref, out_ref, buf, send_sems, recv_sems):
        x, y, c = _mesh_position()
        me = 4 * x + 2 * y + c
        buf[me] = v_ref[...]
        copies = []
        for r, (rx, ry, rc) in enumerate(RELATIONS):
            cp = pltpu.make_async_remote_copy(
                src_ref=v_ref, dst_ref=buf.at[me], send_sem=send_sems.at[r], recv_sem=recv_sems.at[r],
                device_id=(_flip(x, rx), _flip(y, ry), _flip(c, rc)), device_id_type=pl.DeviceIdType.MESH)
            cp.start()
            copies.append(cp)
        for cp in copies:
            cp.wait()
        total = buf[0]
        for s in range(1, N_DEV):
            total = total + buf[s]
        out_ref[...] = total

    return pl.pallas_call(
        body, name=name,
        out_shape=jax.ShapeDtypeStruct((R, C), F32),
        in_specs=[pl.BlockSpec(memory_space=pltpu.VMEM)],
        out_specs=pl.BlockSpec(memory_space=pltpu.VMEM),
        scratch_shapes=[pltpu.VMEM((N_DEV, R, C), F32), pltpu.SemaphoreType.DMA((7,)), pltpu.SemaphoreType.DMA((7,))],
    )(v)


def _adamw(w, g, m, v, *, name):
    R, C = w.shape
    slots = g.ndim == 3
    tr = _pick(R, (256, 128, 64, 32, 16) if slots else (256, 128, 64, 32, 16, 8))

    def body(w_ref, g_ref, m_ref, v_ref, g_out, d_ref, nm_ref, nv_ref):
        if slots:
            g = g_ref[0].astype(F32)
            for s in range(1, N_DEV):
                g = g + g_ref[s].astype(F32)
        else:
            g = g_ref[...]
        g_out[...] = g
        m_new = ADAM_B1 * m_ref[...] + (1.0 - ADAM_B1) * g
        v_new = ADAM_B2 * v_ref[...] + (1.0 - ADAM_B2) * (g * g)
        m_hat = m_new / (1.0 - ADAM_B1 ** ADAM_STEP)
        v_hat = v_new / (1.0 - ADAM_B2 ** ADAM_STEP)
        d_ref[...] = -ADAM_LR * (m_hat / (jnp.sqrt(v_hat) + ADAM_EPS) + ADAM_WD * w_ref[...])
        nm_ref[...] = m_new
        nv_ref[...] = v_new

    spec = pl.BlockSpec((tr, C), lambda i: (i, 0))
    g_spec = pl.BlockSpec((N_DEV, tr, C), lambda i: (0, i, 0)) if slots else spec
    return pl.pallas_call(
        body, name=name, grid=(R // tr,),
        out_shape=[jax.ShapeDtypeStruct((R, C), F32)] * 4,
        in_specs=[spec, g_spec, spec, spec], out_specs=[spec] * 4,
        compiler_params=_cparams(("parallel",)),
    )(w, g, m, v)


WEIGHT_NAMES = ["ffn1_norm", "ffn1_w_gate", "ffn1_w_up", "ffn1_w_down", "mix_norm", "w_in", "sb_out_norm",
                "dil_out_norm", "w_out", "ffn2_norm", "ffn2_w_gate", "ffn2_w_up", "ffn2_w_down", "final_norm"]
GAIN_NAMES = ["ffn1_norm", "mix_norm", "sb_out_norm", "dil_out_norm", "ffn2_norm", "final_norm"]
COL_SHARDED = ["ffn1_w_gate", "ffn1_w_up", "ffn2_w_gate", "ffn2_w_up", "w_in"]
ROW_SHARDED = ["ffn1_w_down", "ffn2_w_down", "w_out"]
GROUPS = {"mixer": (["w_in"], ["w_out"]),
          "ffn2": (["ffn2_w_gate", "ffn2_w_up"], ["ffn2_w_down"])}
for _ffn in ("ffn1", "ffn2"):
    GROUPS.update({f"{_ffn}_w_gate": ([f"{_ffn}_w_gate"], []), f"{_ffn}_w_up": ([f"{_ffn}_w_up"], []),
                   f"{_ffn}_w_down": ([], [f"{_ffn}_w_down"])})


class _Exchanges:
    def __init__(self, params):
        self.params = params
        self.grads = {}

    def gather(self, group):
        if group is None:
            return None
        cols, rows = GROUPS[group]
        return _GatherPlan([self.params[n].astype(BF16) for n in cols + rows])

    def gathered(self, group, got, weights):
        if group is None:
            return
        cols, rows = GROUPS[group]
        for n, blocks in zip(cols + rows, got):
            if n in cols:
                weights[n] = jnp.transpose(blocks, (1, 0, 2)).reshape(blocks.shape[1], N_DEV * blocks.shape[2])
            else:
                weights[n] = blocks.reshape(N_DEV * blocks.shape[1], blocks.shape[2])

    def send(self, group, grads):
        if group is None:
            return None
        cols, rows = GROUPS[group]
        packs = [jnp.transpose(grads[n].reshape(grads[n].shape[0], N_DEV, self.params[n].shape[1]), (1, 0, 2))
                 for n in cols]
        packs += [grads[n].reshape(N_DEV, self.params[n].shape[0], grads[n].shape[1]) for n in rows]
        return _ExchangePlan([p.astype(GRAD_WIRE) for p in packs])

    def received(self, group, got):
        if group is None:
            return
        cols, rows = GROUPS[group]
        for n, slots in zip(cols + rows, got):
            self.grads[n] = slots


def _step(x, target, params, moments_m, moments_v):
    ex = _Exchanges(params)
    weights = {}
    gains = {n: params[n] for n in GAIN_NAMES}
    loss_row, grad_x, gain_grads, _ = _local_step(x, target, gains, weights, ex)
    grads = ex.grads

    rows = [gain_grads[n].reshape(-1, LANES) for n in GAIN_NAMES] + [loss_row]
    small = jnp.concatenate(rows, axis=0)
    pad = (-small.shape[0]) % 8
    small = jnp.pad(small, ((0, pad), (0, 0)))
    small = _all_reduce_rows(small, name="reduce_gains_loss")
    off = 0
    for n in GAIN_NAMES:
        r = gain_grads[n].shape[1] // LANES
        grads[n] = small[off:off + r].reshape(1, -1)
        off += r
    loss = small[off, 0]

    delta, new_m, new_v = {}, {}, {}
    for n in WEIGHT_NAMES:
        grads[n], delta[n], new_m[n], new_v[n] = _adamw(params[n], grads[n], moments_m[n], moments_v[n],
                                                        name=f"adamw_{n}")
    return loss, grad_x, grads, delta, new_m, new_v


def kernel(x, ffn1_norm, ffn1_w_gate, ffn1_w_up, ffn1_w_down, mix_norm, w_in, sb_out_norm, dil_out_norm, w_out, ffn2_norm, ffn2_w_gate, ffn2_w_up, ffn2_w_down, final_norm, loss_target, m_ffn1_norm, m_ffn1_w_gate, m_ffn1_w_up, m_ffn1_w_down, m_mix_norm, m_w_in, m_sb_out_norm, m_dil_out_norm, m_w_out, m_ffn2_norm, m_ffn2_w_gate, m_ffn2_w_up, m_ffn2_w_down, m_final_norm, v_ffn1_norm, v_ffn1_w_gate, v_ffn1_w_up, v_ffn1_w_down, v_mix_norm, v_w_in, v_sb_out_norm, v_dil_out_norm, v_w_out, v_ffn2_norm, v_ffn2_w_gate, v_ffn2_w_up, v_ffn2_w_down, v_final_norm):
    given = dict(locals())
    shapes = {n: given[n].shape for n in WEIGHT_NAMES}

    def as2d(a):
        return a.reshape(1, -1) if a.ndim == 1 else a.reshape(a.shape[-2], a.shape[-1])

    params = {n: as2d(given[n]) for n in WEIGHT_NAMES}
    moments_m = {n: as2d(given["m_" + n]) for n in WEIGHT_NAMES}
    moments_v = {n: as2d(given["v_" + n]) for n in WEIGHT_NAMES}
    loss, grad_x, grads, delta, new_m, new_v = _step(x[0], loss_target[0], params, moments_m, moments_v)
    back = lambda d: [d[n].reshape(shapes[n]) for n in WEIGHT_NAMES]
    return (loss, grad_x[None], *back(grads), *back(delta), *back(new_m), *back(new_v))
```

```python
import functools

import jax
import jax.numpy as jnp
from jax import lax
from jax.experimental import pallas as pl
from jax.experimental.pallas import tpu as pltpu

F32 = jnp.float32
BF16 = jnp.bfloat16
GRAD_WIRE = jnp.bfloat16

N_DEV = 8
HEAD_DIM = 64
LANES = 128
DILATED_PATTERNS = ((128, 1), (512, 4), (2048, 16))
DIL_BLOCK = 128
DIL_SUPER = 2048
DIL_UNROLL = 16
SB_TILE = 256
SB_LANES = 128
SB_UNROLL = 4
SB_STEP_TILES = 2
SB_DEAD = 90.0
SB_UNSEEN = -1e30
ROPE_THETA = 10000.0
RMS_EPS = 1e-6
HALF_STEP = 0.5
ADAM_LR = 0.001
ADAM_B1 = 0.9
ADAM_B2 = 0.999
ADAM_EPS = 1e-08
ADAM_WD = 0.01
ADAM_STEP = 10
NEG_BIG = -1e30
VMEM_CAP_MB = 60


def _pick(n, prefs):
    for p in prefs:
        if n % p == 0:
            return p
    return n


MM_MAX_TILE = 1536
MM_WHOLE = 3072


def _largest_tile(n, cap):
    if n <= cap:
        return n
    for t in range(cap - cap % LANES, 0, -LANES):
        if n % t == 0:
            return t
    return n


def _cparams(sem=None, vmem_mb=48):
    return pltpu.CompilerParams(dimension_semantics=sem, vmem_limit_bytes=min(vmem_mb, VMEM_CAP_MB) * 1024 * 1024)


def _nbytes(shape, dtype):
    n = 1
    for s in shape:
        n *= s
    return n * jnp.dtype(dtype).itemsize


def _mm(a, b, *, name, ta=False, tb=False, outs=(F32,), res=None, alpha=1.0, extras=(), epilogue=None,
        tm=None, tn=None, tk=None, comm=None, rows=(), lanes=(), row_sums=0, b_cols=None, second=None):
    if ta:
        K, M = a.shape
    else:
        M, K = a.shape
    if tb:
        N, Kb = b.shape
    else:
        Kb, N = b.shape
    col0 = 0
    if b_cols is not None:
        col0, N = b_cols
    assert K == Kb, (a.shape, b.shape, ta, tb)
    tn = tn or (N if (not ta and K <= MM_WHOLE and N <= MM_WHOLE) else _largest_tile(N, MM_MAX_TILE))
    wide = tn > MM_MAX_TILE and (len(extras) + len(outs) > 3 or a.dtype == F32)
    tm = tm or (_largest_tile(M, MM_MAX_TILE) if ta else _pick(M, (256, 128) if wide else (512, 256, 128)))
    tk = tk or (K if K <= MM_WHOLE else _pick(K, (2048, 1024, 512, 256, 128)))
    nk = K // tk
    a_spec = pl.BlockSpec((tk, tm), lambda i, j, k: (k, i)) if ta else pl.BlockSpec((tm, tk), lambda i, j, k: (i, k))
    assert col0 % tn == 0
    b_spec = (pl.BlockSpec((tn, tk), lambda i, j, k: (j + col0 // tn, k)) if tb
              else pl.BlockSpec((tk, tn), lambda i, j, k: (k, j + col0 // tn)))
    mn_spec = pl.BlockSpec((tm, tn), lambda i, j, k: (i, j))
    dims = (((0 if ta else 1,), (1 if tb else 0,)), ((), ()))
    row_spec = pl.BlockSpec((1, tn), lambda i, j, k: (0, j))
    lane_spec = pl.BlockSpec((tm, LANES), lambda i, j, k: (i, 0))
    n_extra = len(extras) + (1 if res is not None else 0) + len(rows) + len(lanes)
    n_mn = len(outs)
    n_out = n_mn + row_sums
    assert row_sums == 0 or tn == N
    grid = (M // tm, N // tn, nk)
    n_ab = 2 if second is None else 4
    hosted = _Hosted(comm, n_in=n_ab + n_extra, n_out=n_out, n_scratch=1 if nk > 1 else 0)

    def body(*refs):
        in_refs = refs[n_ab:n_ab + n_extra]
        ab_refs = refs[:n_ab]
        refs = hosted.begin(refs, grid)
        out_refs = refs[n_ab + n_extra:n_ab + n_extra + n_out]
        prod = None
        for a_ref, b_ref in zip(ab_refs[0::2], ab_refs[1::2]):
            part = lax.dot_general(a_ref[...].astype(BF16), b_ref[...].astype(BF16), dims, preferred_element_type=F32)
            prod = part if prod is None else prod + part

        def finish(acc):
            blocks = [r[...] for r in in_refs]
            if res is not None:
                r_blk, blocks = blocks[0], blocks[1:]
            else:
                r_blk = None
            if epilogue is None:
                val = acc * alpha
                if r_blk is not None:
                    val = val + r_blk
                vals = (val,)
            else:
                vals = epilogue(acc, r_blk, *blocks)
                vals = vals if isinstance(vals, (tuple, list)) else (vals,)
            for o_ref, v in zip(out_refs[:n_mn], vals[:n_mn]):
                o_ref[...] = v.astype(o_ref.dtype)
            first_rows = pl.program_id(0) == 0
            for o_ref, part in zip(out_refs[n_mn:], vals[n_mn:]):
                @pl.when(first_rows)
                def _(o_ref=o_ref, part=part):
                    o_ref[...] = part

                @pl.when(jnp.logical_not(first_rows))
                def _(o_ref=o_ref, part=part):
                    o_ref[...] += part

        if nk == 1:
            finish(prod)
        else:
            acc_ref = refs[n_ab + n_extra + n_out]
            k = pl.program_id(2)

            @pl.when(k == 0)
            def _():
                acc_ref[...] = prod

            @pl.when(k > 0)
            def _():
                acc_ref[...] += prod

            @pl.when(k == nk - 1)
            def _():
                finish(acc_ref[...])

        hosted.end(grid)

    mn_operands = ([res] if res is not None else []) + list(extras)
    ab = [a, b] + (list(second) if second is not None else [])
    operands = ab + mn_operands + list(rows) + list(lanes)
    in_specs = [a_spec, b_spec] * (n_ab // 2) + [mn_spec] * len(mn_operands) + [row_spec] * len(rows) + [lane_spec] * len(lanes)
    est = n_ab * (_nbytes((tm, tk), a.dtype) + _nbytes((tk, tn), b.dtype))
    est += 2 * sum(_nbytes((tm, tn), o.dtype) for o in mn_operands)
    est += 2 * sum(_nbytes((tm, tn), d) for d in outs) + 2 * _nbytes((tm, tn), F32)
    semantics = ("parallel", "parallel", "arbitrary") if row_sums == 0 else ("arbitrary",) * 3
    result = pl.pallas_call(
        body, name=name, grid=grid,
        out_shape=[jax.ShapeDtypeStruct((M, N), d) for d in outs]
        + [jax.ShapeDtypeStruct((1, N), F32)] * row_sums + hosted.out_shapes,
        in_specs=in_specs + hosted.in_specs,
        out_specs=[mn_spec] * n_mn + [row_spec] * row_sums + hosted.out_specs,
        scratch_shapes=([pltpu.VMEM((tm, tn), F32)] if nk > 1 else []) + hosted.scratch,
        compiler_params=_cparams(hosted.semantics(semantics), vmem_mb=max(32, 2 * est // (1024 * 1024))),
    )(*operands, *hosted.operands)
    own, got = result[:n_out], list(result[n_out:])
    own = own[0] if n_out == 1 else own
    return own if comm is None else (own, got)


def _rms_hat(x):
    r = lax.rsqrt(jnp.mean(x * x, axis=-1, keepdims=True) + RMS_EPS)
    return x * r, r


def _rms_fwd(xs, gains, *, name, comm=None):
    S = xs[0].shape[0]
    widths = [x.shape[1] for x in xs]
    tm = _pick(S, (512, 256, 128))
    n = len(xs)
    grid = (S // tm,)
    hosted = _Hosted(comm, n_in=2 * n, n_out=1, n_scratch=0)

    def body(*refs):
        refs = hosted.begin(refs, grid)
        o_ref = refs[2 * n]
        off = 0
        for i in range(n):
            xh, _ = _rms_hat(refs[i][...])
            o_ref[:, off:off + widths[i]] = (xh * refs[n + i][...]).astype(o_ref.dtype)
            off += widths[i]
        hosted.end(grid)

    out, *got = pl.pallas_call(
        body, name=name, grid=grid,
        out_shape=[jax.ShapeDtypeStruct((S, sum(widths)), BF16)] + hosted.out_shapes,
        in_specs=[pl.BlockSpec((tm, w), lambda i: (i, 0)) for w in widths]
        + [pl.BlockSpec((1, w), lambda i: (0, 0)) for w in widths] + hosted.in_specs,
        out_specs=[pl.BlockSpec((tm, sum(widths)), lambda i: (i, 0))] + hosted.out_specs,
        scratch_shapes=hosted.scratch,
        compiler_params=_cparams(hosted.semantics(("parallel",))),
    )(*xs, *gains, *hosted.operands)
    return out if comm is None else (out, got)


def _sigmoid(g):
    return 1.0 / (1.0 + jnp.exp(-g))


def _ride(result, plan):
    return result if plan is not None else (result, None)


def _residual_then_norm(alpha):
    def epilogue(acc, res, gain):
        y = res + alpha * acc
        return y, _rms_hat(y)[0] * gain
    return epilogue


def _ffn_fwd(x, gain, w, *, tag, ex, first_rider=None, riders=(None, None, None), head=None, h=None,
             next_gain=None):
    if h is None:
        plan = ex.gather(first_rider)
        h, got = _ride(_rms_fwd([x], [gain], name=f"{tag}_norm", comm=plan), plan)
        ex.gathered(first_rider, got, w)
    plan = ex.gather(riders[0])
    g, got = _ride(_mm(h, w[f"{tag}_w_gate"], tb=True, outs=(BF16,), name=f"{tag}_gate", comm=plan), plan)
    ex.gathered(riders[0], got, w)

    def act(acc, _, g_blk):
        gf = g_blk.astype(F32)
        return acc, gf * _sigmoid(gf) * acc

    plan = ex.gather(riders[1])
    (u, a), got = _ride(_mm(h, w[f"{tag}_w_up"], tb=True, outs=(BF16, BF16), extras=(g,), epilogue=act, name=f"{tag}_up_act",
                            comm=plan), plan)
    ex.gathered(riders[1], got, w)
    plan = ex.gather(riders[2])
    if head is None and next_gain is None:
        y, got = _ride(_mm(a, w[f"{tag}_w_down"], res=x, alpha=HALF_STEP, name=f"{tag}_down", comm=plan), plan)
    elif head is None:
        y, got = _ride(_mm(a, w[f"{tag}_w_down"], res=x, rows=(next_gain,), outs=(F32, BF16),
                           epilogue=_residual_then_norm(HALF_STEP), name=f"{tag}_down_norm", comm=plan), plan)
    else:
        final_gain, target = head
        y, got = _ride(_mm(a, w[f"{tag}_w_down"], res=x, extras=(target,), rows=(final_gain,), row_sums=2,
                           epilogue=_loss_head_epilogue, name=f"{tag}_down_loss", comm=plan), plan)
    ex.gathered(riders[2], got, w)
    return y, (h, g, u, a)


def _loss_head_epilogue(acc, x_in, target, gain):
    xh, r = _rms_hat(x_in + HALF_STEP * acc)
    err = xh * gain - target
    dy = err * (1.0 / acc.shape[1])
    dxh = dy * gain
    dx = r * (dxh - xh * jnp.mean(dxh * xh, axis=-1, keepdims=True))
    loss = 0.5 * jnp.sum(jnp.mean(err * err, axis=-1, keepdims=True), axis=0, keepdims=True)
    return dx, jnp.sum(dy * xh, axis=0, keepdims=True), jnp.zeros_like(gain) + loss


def _rms_bwd_epilogue(acc, dh_so_far, x, *dres_and_gain):
    gain = dres_and_gain[-1]
    dh = acc if dh_so_far is None else acc + dh_so_far
    xh, r = _rms_hat(x)
    dxh = dh * gain
    dx = r * (dxh - xh * jnp.mean(dxh * xh, axis=-1, keepdims=True))
    if len(dres_and_gain) == 2:
        dx = dx + dres_and_gain[0]
    return dx, jnp.sum(dh * xh, axis=0, keepdims=True)


def _ffn_bwd(dout, x, gain, w, saved, *, tag, ex, rider=(None, None), spread=False):
    h, g, u, a = saved
    wg, wu, wd = (w[f"{tag}_w_{n}"] for n in ("gate", "up", "down"))

    def act_bwd(acc, _, g_blk, u_blk):
        gf, uf = g_blk.astype(F32), u_blk.astype(F32)
        da = acc * HALF_STEP
        sig = _sigmoid(gf)
        silu = gf * sig
        return da * uf * (sig + silu * (1.0 - sig)), da * silu

    def carrying(group, grad, call):
        group = group if spread else None
        plan = ex.send(group, {group: grad})
        out, got = _ride(call(plan), plan)
        ex.received(group, got)
        return out

    plan = ex.send(*rider)
    (dg, du), got = _ride(_mm(dout, wd, tb=True, outs=(BF16, BF16), extras=(g, u), epilogue=act_bwd,
                              name=f"{tag}_bwd_act", comm=plan), plan)
    ex.received(rider[0], got)
    dwg = _mm(h, dg, ta=True, outs=(GRAD_WIRE,), name=f"{tag}_dwg")
    dwu = carrying(f"{tag}_w_gate", dwg, lambda plan: _mm(h, du, ta=True, outs=(GRAD_WIRE,), name=f"{tag}_dwu", comm=plan))
    dwd = carrying(f"{tag}_w_up", dwu,
                   lambda plan: _mm(a, dout, ta=True, outs=(GRAD_WIRE,), alpha=HALF_STEP, name=f"{tag}_dwd", comm=plan))
    dx, dgain = carrying(f"{tag}_w_down", dwd, lambda plan: _mm(
        dg, wg, second=(du, wu), extras=(x, dout), rows=(gain,), row_sums=1, epilogue=_rms_bwd_epilogue,
        tm=256, name=f"{tag}_dh_norm_bwd", comm=plan))
    return dx, dgain, dwg, dwu, dwd


def _rope_tables(S):
    half = HEAD_DIM // 2
    inv_freq = ROPE_THETA ** (-jnp.arange(half, dtype=F32) / half)
    ang = jnp.arange(S, dtype=F32)[:, None] * inv_freq[None, :]
    cos, sin = jnp.cos(ang), jnp.sin(ang)
    reps = LANES // HEAD_DIM
    cos_t = jnp.tile(jnp.concatenate([cos, cos], axis=1), (1, reps))
    sin_t = jnp.tile(jnp.concatenate([-sin, sin], axis=1), (1, reps))
    return cos_t, sin_t


def _rotate(v, cos, sin, sign):
    half = HEAD_DIM // 2
    groups = []
    for g in range(v.shape[1] // LANES):
        t = v[:, g * LANES:(g + 1) * LANES]
        lane = lax.broadcasted_iota(jnp.int32, t.shape, 1)
        swapped = jnp.where(lane % HEAD_DIM < half, pltpu.roll(t, LANES - half, axis=1), pltpu.roll(t, half, axis=1))
        groups.append(t * cos + swapped * (sin * sign))
    return groups[0] if len(groups) == 1 else jnp.concatenate(groups, axis=1)


def _join_d_proj(pieces, rotated, cos_t, sin_t, *, name):
    S = pieces[0].shape[0]
    widths = [p.shape[1] for p in pieces]
    tm = _pick(S, (256, 128))
    n = len(pieces)

    def body(*refs):
        c_ref, s_ref, o_ref = refs[n], refs[n + 1], refs[n + 2]
        off = 0
        for i in range(n):
            v = refs[i][...]
            if i in rotated:
                v = _rotate(v, c_ref[...], s_ref[...], -1.0)
            o_ref[:, off:off + widths[i]] = v.astype(o_ref.dtype)
            off += widths[i]

    return pl.pallas_call(
        body, name=name, grid=(S // tm,),
        out_shape=jax.ShapeDtypeStruct((S, sum(widths)), BF16),
        in_specs=[pl.BlockSpec((tm, w), lambda i: (i, 0)) for w in widths]
        + [pl.BlockSpec((tm, LANES), lambda i: (i, 0))] * 2,
        out_specs=pl.BlockSpec((tm, sum(widths)), lambda i: (i, 0)),
        compiler_params=_cparams(("parallel",)),
    )(*pieces, cos_t, sin_t)


def _head_masks(shape):
    lane = lax.broadcasted_iota(jnp.int32, shape, 1)
    return [(lane >= HEAD_DIM * h) & (lane < HEAD_DIM * (h + 1)) for h in range(shape[1] // HEAD_DIM)]


def _sb_scores(q2, k_j):
    z = lax.dot_general(q2, k_j, (((1,), (1,)), ((), ())), preferred_element_type=F32)
    sign_bit = jnp.int32(-2 ** 31)
    minus_abs = lax.bitcast_convert_type(lax.bitcast_convert_type(z, jnp.int32) | sign_bit, F32)
    softplus = jnp.maximum(z, 0.0) + jnp.log(1.0 + jnp.exp(minus_abs))
    return z - softplus, softplus


def _sb_stack_heads(t, scale=None):
    parts = [jnp.where(hm, t, jnp.zeros_like(t)) for hm in _head_masks(t.shape)]
    t2 = jnp.concatenate(parts, axis=0)
    if scale is not None:
        t2 = (t2.astype(F32) * scale).astype(t2.dtype)
    return t2


def _sb_unstack_heads(t2):
    n = t2.shape[1] // HEAD_DIM
    T = t2.shape[0] // n
    masks = _head_masks((T, t2.shape[1]))
    out = t2[:T]
    for h in range(1, n):
        out = jnp.where(masks[h], t2[h * T:(h + 1) * T], out)
    return out


def _sb_causal(T, n_heads):
    row = lax.broadcasted_iota(jnp.int32, (n_heads * T, T), 0)
    col = lax.broadcasted_iota(jnp.int32, (n_heads * T, T), 1)
    return col < row % T


def _sb_triangle(T, later):
    row = lax.broadcasted_iota(jnp.int32, (T, T), 0)
    col = lax.broadcasted_iota(jnp.int32, (T, T), 1)
    return ((row > col) if later else (row < col)).astype(BF16)


def _sb_fwd(p_sb, *, name, comm=None):
    S = p_sb.shape[0]
    W = p_sb.shape[1] // 3
    LW = min(SB_LANES, W)
    NH = LW // HEAD_DIM
    npair = W // LW
    T = SB_TILE
    n_tiles = S // T
    assert n_tiles <= HEAD_DIM
    scale = HEAD_DIM ** -0.5

    R = SB_STEP_TILES
    grid = (npair, n_tiles // R)
    hosted = _Hosted(comm, n_in=5, n_out=2, n_scratch=0)

    def body(*refs):
        refs = hosted.begin(refs, grid)
        step = pl.program_id(1)
        lax.fori_loop(0, R, lambda sub, _: query_tile(step * R + sub, sub, *refs), 0)
        hosted.end(grid)

    def query_tile(I, sub, q_ref, k_ref, v_ref, causal_ref, later_ref, o_ref, c_ref):
        rows = pl.ds(pl.multiple_of(sub * T, T), T)
        lane = lax.broadcasted_iota(jnp.int32, (T, LW), 1)
        causal = causal_ref[...]
        later_than = later_ref[...]
        q2 = _sb_stack_heads(q_ref[rows, :], scale)

        def scores(J, diag):
            off = pl.multiple_of(J * T, T)
            log_beta, stay = _sb_scores(q2, k_ref[pl.ds(off, T), :])
            if diag:
                stay = stay * causal
            local = jnp.dot(stay.astype(BF16), later_than, preferred_element_type=F32)
            return log_beta, local, jnp.sum(stay, axis=1, keepdims=True), v_ref[pl.ds(off, T), :]

        def weigh(J, sc, gone, acc, carr, diag):
            log_beta, local, _, v_j = sc
            w = jnp.exp((log_beta - gone) - local)
            if diag:
                w = w * causal
            acc = acc + jnp.dot(w.astype(BF16), v_j, preferred_element_type=F32)
            for h in range(NH):
                carr = jnp.where(lane == HEAD_DIM * h + J, -gone[h * T:(h + 1) * T], carr)
            return acc, carr

        def tiles(J, count, state, diag):
            gone, acc, carr, _ = state
            scs = [scores(J - u, diag and u == 0) for u in range(count)]
            for u, sc in enumerate(scs):
                acc, carr = weigh(J - u, sc, gone, acc, carr, diag and u == 0)
                gone = gone + sc[2]
            return gone, acc, carr, jnp.min(gone)

        U = SB_UNROLL
        alive = lambda st: st[3] < SB_DEAD
        state = (jnp.zeros((NH * T, 1), F32), jnp.zeros((NH * T, LW), F32),
                 jnp.full((T, LW), SB_UNSEEN, F32), jnp.zeros((), F32))
        state = lax.cond(I > 0, lambda st: tiles(I, 2, st, True), lambda st: tiles(I, 1, st, True), state)
        rest = jnp.maximum(I - 1, 0)
        singles = jnp.where(rest > 0, (rest - 1) % U + 1, 0)
        _, state = lax.while_loop(lambda c: (c[0] < singles) & alive(c[1]),
                                  lambda c: (c[0] + 1, tiles(I - 2 - c[0], 1, c[1], False)), (jnp.int32(0), state))
        blocks = (rest - singles) // U
        _, state = lax.while_loop(lambda c: (c[0] < blocks) & alive(c[1]),
                                  lambda c: (c[0] + 1, tiles(I - 2 - singles - U * c[0], U, c[1], False)),
                                  (jnp.int32(0), state))
        _, acc, carr, _ = state
        o_ref[rows, :] = _sb_unstack_heads(acc)
        c_ref[rows, :] = carr
        return 0

    blk = lambda I_off: pl.BlockSpec((R * T, LW), lambda p, I: (I, I_off + p))
    full = lambda off: pl.BlockSpec((S, LW), lambda p, I: (0, off + p))
    const = lambda rows: pl.BlockSpec((rows, T), lambda p, I: (0, 0))
    o, carries, *got = pl.pallas_call(
        body, name=name, grid=grid,
        out_shape=[jax.ShapeDtypeStruct((S, W), F32), jax.ShapeDtypeStruct((S, W), F32)] + hosted.out_shapes,
        in_specs=[blk(0), full(npair), full(2 * npair), const(NH * T), const(T)] + hosted.in_specs,
        out_specs=[blk(0), blk(0)] + hosted.out_specs,
        scratch_shapes=hosted.scratch,
        compiler_params=_cparams(hosted.semantics(("parallel", "arbitrary")), vmem_mb=56),
    )(p_sb, p_sb, p_sb, _sb_causal(T, NH).astype(F32), _sb_triangle(T, True), *hosted.operands)
    return (o, carries) if comm is None else (o, carries, got)


def _sb_bwd(p_sb, do, carries, *, name, comm=None):
    S = p_sb.shape[0]
    W = p_sb.shape[1] // 3
    LW = min(LANES, W)
    NH = LW // HEAD_DIM
    npair = W // LW
    T = SB_TILE
    n_tiles = S // T
    scale = HEAD_DIM ** -0.5

    R = SB_STEP_TILES
    grid = (npair, n_tiles // R)
    hosted = _Hosted(comm, n_in=8, n_out=3, n_scratch=0)

    def body(*refs):
        refs = hosted.begin(refs, grid)
        dk_ref, dv_ref = refs[9], refs[10]
        step = pl.program_id(1)

        @pl.when(step == 0)
        def _():
            dk_ref[...] = jnp.zeros_like(dk_ref)
            dv_ref[...] = jnp.zeros_like(dv_ref)

        lax.fori_loop(0, R, lambda sub, _: query_tile(step * R + sub, sub, *refs), 0)
        hosted.end(grid)

    def query_tile(I, sub, q_ref, k_ref, v_ref, do_ref, c_ref, causal_ref, later_ref, earlier_ref,
                   dq_ref, dk_ref, dv_ref):
        rows = pl.ds(pl.multiple_of(sub * T, T), T)
        lane = lax.broadcasted_iota(jnp.int32, (T, LW), 1)
        causal = causal_ref[...]
        later_than = later_ref[...]
        earlier_than = earlier_ref[...]
        q2 = _sb_stack_heads(q_ref[rows, :], scale)
        do2 = _sb_stack_heads(do_ref[rows, :].astype(BF16))
        carr = c_ref[rows, :]
        tn_dims = (((0,), (0,)), ((), ()))

        def chain(J, diag):
            off = pl.multiple_of(J * T, T)
            k_j = k_ref[pl.ds(off, T), :]
            v_j = v_ref[pl.ds(off, T), :]
            log_beta, stay = _sb_scores(q2, k_j)
            if diag:
                stay = stay * causal
            lc = jnp.concatenate(
                [jnp.sum(jnp.where(lane == HEAD_DIM * h + J, carr, 0.0), axis=1, keepdims=True) for h in range(NH)],
                axis=0)
            w = jnp.exp((log_beta + lc) - jnp.dot(stay.astype(BF16), later_than, preferred_element_type=F32))
            if diag:
                w = w * causal
            dw = lax.dot_general(do2, v_j, (((1,), (1,)), ((), ())), preferred_element_type=F32)
            e = w * dw
            local = jnp.dot(e.astype(BF16), earlier_than, preferred_element_type=F32)
            return off, k_j, w, e, local, jnp.exp(log_beta), jnp.sum(e, axis=1, keepdims=True)

        def finish(ch, ec, dq_acc, diag):
            off, k_j, w, e, local, beta, _ = ch
            e_before = local + ec
            dz = e - beta * (e + e_before)
            if diag:
                dz = dz * causal
            dzb = dz.astype(BF16)
            dq_acc = dq_acc + jnp.dot(dzb, k_j, preferred_element_type=F32)
            dk_ref[pl.ds(off, T), :] += lax.dot_general(dzb, q2, tn_dims, preferred_element_type=F32)
            dv_ref[pl.ds(off, T), :] += lax.dot_general(w.astype(BF16), do2, tn_dims, preferred_element_type=F32)
            return dq_acc

        def tiles(J, count, state, diag):
            ec, dq_acc = state
            chains = [chain(J + u, diag and u == count - 1) for u in range(count)]
            for u, ch in enumerate(chains):
                dq_acc = finish(ch, ec, dq_acc, diag and u == count - 1)
                ec = ec + ch[6]
            return ec, dq_acc

        lane_row = lax.broadcasted_iota(jnp.int32, (1, LW), 1)
        reached = (jnp.max(carr, axis=0, keepdims=True) > 0.5 * SB_UNSEEN) & (lane_row < HEAD_DIM)
        first = jnp.min(jnp.where(reached, lane_row.astype(F32), float(n_tiles))).astype(jnp.int32)
        U = SB_UNROLL
        count = I - first
        rest = jnp.maximum(count - 1, 0)
        state = (jnp.zeros((NH * T, 1), F32), jnp.zeros((NH * T, LW), F32))
        state = lax.fori_loop(0, rest // U, lambda jj, st: tiles(first + U * jj, U, st, False), state)
        state = lax.fori_loop(0, rest % U, lambda r, st: tiles(I - 1 - rest % U + r, 1, st, False), state)
        _, dq_acc = lax.cond(count > 0, lambda st: tiles(I - 1, 2, st, True), lambda st: tiles(I, 1, st, True), state)
        dq_ref[rows, :] = _sb_unstack_heads(dq_acc) * scale
        return 0

    blk = lambda src_off: pl.BlockSpec((R * T, LW), lambda p, I: (I, src_off + p))
    full = lambda off: pl.BlockSpec((S, LW), lambda p, I: (0, off + p))
    const = lambda rows: pl.BlockSpec((rows, T), lambda p, I: (0, 0))
    dq, dk, dv, *got = pl.pallas_call(
        body, name=name, grid=grid,
        out_shape=[jax.ShapeDtypeStruct((S, W), F32)] * 3 + hosted.out_shapes,
        in_specs=[blk(0), full(npair), full(2 * npair), blk(0), blk(0), const(NH * T), const(T), const(T)]
        + hosted.in_specs,
        out_specs=[blk(0), full(0), full(0)] + hosted.out_specs,
        scratch_shapes=hosted.scratch,
        compiler_params=_cparams(hosted.semantics(("parallel", "arbitrary")), vmem_mb=56),
    )(p_sb, p_sb, p_sb, do, carries, _sb_causal(T, NH).astype(F32), _sb_triangle(T, True), _sb_triangle(T, False),
      *hosted.operands)
    return (dq, dk, dv) if comm is None else (dq, dk, dv, got)


def _dil_blocks(b, body_fn):
    for pi, (window, dil) in enumerate(DILATED_PATTERNS):
        assert window // dil == DIL_BLOCK
        nblk = DIL_SUPER // (DIL_BLOCK * dil)
        assert (dil * nblk) % DIL_UNROLL == 0

        def group(g, _, pi=pi, dil=dil, nblk=nblk):
            for u in range(DIL_UNROLL):
                t = g * DIL_UNROLL + u
                n = t % nblk
                body_fn(pi, dil, t // nblk, n, b * nblk + n)
            return 0

        lax.fori_loop(0, dil * nblk // DIL_UNROLL, group, 0)


def _dil_rows(start, size, dil):
    if dil == 1:
        return pl.ds(pl.multiple_of(start, DIL_BLOCK), size)
    return pl.ds(start, size, stride=dil)


def _dil_fill_bias(bias_ref):
    row = lax.broadcasted_iota(jnp.int32, (2 * DIL_BLOCK, 2 * DIL_BLOCK), 0)
    kk = lax.broadcasted_iota(jnp.int32, (2 * DIL_BLOCK, 2 * DIL_BLOCK), 1)
    qi = jnp.where(row >= DIL_BLOCK, row - DIL_BLOCK, row)
    for s in range(2):
        dist = s * DIL_BLOCK + qi - kk
        bias_ref[s] = jnp.where((dist >= 0) & (dist <= DIL_BLOCK), 0.0, NEG_BIG)


def _dl_fwd(p_dl, *, name):
    S, W = p_dl.shape[0], p_dl.shape[1] // 3
    npair = W // LANES
    nsuper = S // DIL_SUPER
    assert S % DIL_SUPER == 0 and S // max(d for _, d in DILATED_PATTERNS) >= 2 * DIL_BLOCK
    scale = HEAD_DIM ** -0.5
    npat = len(DILATED_PATTERNS)

    def body(q_ref, k_ref, v_ref, o_ref, l_ref, bias_ref, *pattern_refs):
        op_refs, lp_refs = pattern_refs[:npat], pattern_refs[npat:]
        b = pl.program_id(1)
        masks = _head_masks((DIL_BLOCK, LANES))
        pl.when(b == 0)(lambda: _dil_fill_bias(bias_ref))

        def block(pi, dil, c, n, gn):
            ws = jnp.maximum(gn - 1, 0)
            qrows = n * (DIL_BLOCK * dil) + c
            krows = ws * (DIL_BLOCK * dil) + c
            q_idx = _dil_rows(qrows, DIL_BLOCK, dil)
            k_idx = _dil_rows(krows, 2 * DIL_BLOCK, dil)
            qb = q_ref[q_idx, :]
            kb = k_ref[k_idx, :].astype(BF16)
            vb = v_ref[k_idx, :].astype(BF16)
            q2 = _sb_stack_heads(qb.astype(BF16), scale)
            z = lax.dot_general(q2, kb, (((1,), (1,)), ((), ())), preferred_element_type=F32) + bias_ref[gn - ws]
            m = jnp.max(z, axis=1, keepdims=True)
            p = jnp.exp(z - m)
            den = jnp.sum(p, axis=1, keepdims=True)
            acc = jnp.dot(p.astype(BF16), vb, preferred_element_type=F32)
            lse = m + jnp.log(den)
            op_refs[pi][q_idx, :] = _sb_unstack_heads(acc / den)
            lp_refs[pi][q_idx, :] = jnp.where(masks[0], lse[:DIL_BLOCK], lse[DIL_BLOCK:])

        _dil_blocks(b, block)
        lses = [r[...] for r in lp_refs]
        top = functools.reduce(jnp.maximum, lses)
        ws_ = [jnp.exp(l - top) for l in lses]
        den = functools.reduce(jnp.add, ws_)
        num = functools.reduce(jnp.add, [w * r[...] for r, w in zip(op_refs, ws_)])
        o_ref[...] = num / den
        l_ref[...] = top + jnp.log(den)

    blk = pl.BlockSpec((DIL_SUPER, LANES), lambda p, b: (b, p))
    full = lambda off: pl.BlockSpec((S, LANES), lambda p, b: (0, off + p))
    return pl.pallas_call(
        body, name=name, grid=(npair, nsuper),
        out_shape=[jax.ShapeDtypeStruct((S, W), F32)] * 2,
        in_specs=[blk, full(npair), full(2 * npair)], out_specs=[blk, blk],
        scratch_shapes=[pltpu.VMEM((2, 2 * DIL_BLOCK, 2 * DIL_BLOCK), F32)]
        + [pltpu.VMEM((DIL_SUPER, LANES), F32)] * (2 * npat),
        compiler_params=_cparams(("arbitrary", "arbitrary")),
    )(p_dl, p_dl, p_dl)


def _dl_bwd(p_dl, o, lse, do, *, name):
    S, W = p_dl.shape[0], p_dl.shape[1] // 3
    npair = W // LANES
    nsuper = S // DIL_SUPER
    scale = HEAD_DIM ** -0.5

    def body(q_ref, k_ref, v_ref, o_ref, l_ref, do_ref, dq_ref, dk_ref, dv_ref, delta_ref, bias_ref):
        b = pl.program_id(1)

        @pl.when(b == 0)
        def _():
            dk_ref[...] = jnp.zeros_like(dk_ref)
            dv_ref[...] = jnp.zeros_like(dv_ref)
            _dil_fill_bias(bias_ref)

        dq_ref[...] = jnp.zeros_like(dq_ref)
        prod = do_ref[...] * o_ref[...]
        delta = jnp.zeros_like(prod)
        for hm in _head_masks(prod.shape):
            delta = jnp.where(hm, jnp.sum(jnp.where(hm, prod, 0.0), axis=1, keepdims=True), delta)
        delta_ref[...] = delta

        def block(pi, dil, c, n, gn):
            ws = jnp.maximum(gn - 1, 0)
            qrows = n * (DIL_BLOCK * dil) + c
            krows = ws * (DIL_BLOCK * dil) + c
            q_idx = _dil_rows(qrows, DIL_BLOCK, dil)
            k_idx = _dil_rows(krows, 2 * DIL_BLOCK, dil)
            qb = q_ref[q_idx, :]
            dob = do_ref[q_idx, :]
            lb = l_ref[q_idx, :]
            db = delta_ref[q_idx, :]
            kb = k_ref[k_idx, :].astype(BF16)
            vb = v_ref[k_idx, :].astype(BF16)
            q2 = _sb_stack_heads(qb.astype(BF16), scale)
            do2 = _sb_stack_heads(dob.astype(BF16))
            lse2 = jnp.concatenate([lb[:, HEAD_DIM * h:HEAD_DIM * h + 1] for h in range(2)], axis=0)
            delta2 = jnp.concatenate([db[:, HEAD_DIM * h:HEAD_DIM * h + 1] for h in range(2)], axis=0)
            z = lax.dot_general(q2, kb, (((1,), (1,)), ((), ())), preferred_element_type=F32)
            p = jnp.exp((z + bias_ref[gn - ws]) - lse2)
            dp = lax.dot_general(do2, vb, (((1,), (1,)), ((), ())), preferred_element_type=F32)
            dzb = (p * (dp - delta2)).astype(BF16)
            tn_dims = (((0,), (0,)), ((), ()))
            dq_blk = _sb_unstack_heads(jnp.dot(dzb, kb, preferred_element_type=F32)) * scale
            dk_blk = lax.dot_general(dzb, q2, tn_dims, preferred_element_type=F32)
            dv_blk = lax.dot_general(p.astype(BF16), do2, tn_dims, preferred_element_type=F32)
            dq_ref[q_idx, :] = dq_ref[q_idx, :] + dq_blk
            dk_ref[k_idx, :] = dk_ref[k_idx, :] + dk_blk
            dv_ref[k_idx, :] = dv_ref[k_idx, :] + dv_blk

        _dil_blocks(b, block)

    blk = pl.BlockSpec((DIL_SUPER, LANES), lambda p, b: (b, p))
    full = lambda off: pl.BlockSpec((S, LANES), lambda p, b: (0, off + p))
    return pl.pallas_call(
        body, name=name, grid=(npair, nsuper),
        out_shape=[jax.ShapeDtypeStruct((S, W), F32)] * 3,
        in_specs=[blk, full(npair), full(2 * npair), blk, blk, blk], out_specs=[blk, full(0), full(0)],
        scratch_shapes=[pltpu.VMEM((DIL_SUPER, LANES), F32), pltpu.VMEM((2, 2 * DIL_BLOCK, 2 * DIL_BLOCK), F32)],
        compiler_params=_cparams(("arbitrary", "arbitrary")),
    )(p_dl, p_dl, p_dl, o, lse, do)


class _NoExchange:
    def gather(self, family):
        return None

    def gathered(self, family, got, weights):
        pass

    def send(self, family, grads):
        return None

    def received(self, family, got):
        pass


def _local_step(x, target, gains, weights, exchanges=None):
    S, D = x.shape
    ex = exchanges or _NoExchange()
    weights = dict(weights)
    d_sb = gains["sb_out_norm"].shape[1]
    d_dl = gains["dil_out_norm"].shape[1]
    cos_t, sin_t = _rope_tables(S)

    riders = ("ffn1_w_up", "ffn1_w_down", "mixer") if exchanges else (None, None, None)
    (x1, h2), saved1 = _ffn_fwd(x, gains["ffn1_norm"], weights, tag="ffn1", ex=ex, riders=riders,
                                first_rider="ffn1_w_gate" if exchanges else None,
                                next_gain=gains["mix_norm"])
    w_in = weights["w_in"]
    w_out = weights["w_out"]
    p_sb = _mm(h2, w_in, tb=True, b_cols=(0, 3 * d_sb), outs=(BF16,), name="proj_sb")

    def rope_qk(acc, _, cos, sin):
        return jnp.concatenate([_rotate(acc[:, :2 * d_dl], cos, sin, 1.0), acc[:, 2 * d_dl:]], axis=1)

    p_dl = _mm(h2, w_in, tb=True, b_cols=(3 * d_sb, 3 * d_dl), lanes=(cos_t, sin_t), epilogue=rope_qk,
               name="proj_dl_rope")
    plan = ex.gather("ffn2" if exchanges else None)
    o_sb, carries, *got = _sb_fwd(p_sb, name="sb_fwd", comm=plan)
    ex.gathered("ffn2", got[0] if got else None, weights)
    o_dl, lse_dl = _dl_fwd(p_dl, name="dl_fwd")
    merged = _rms_fwd([o_sb, o_dl], [gains["sb_out_norm"], gains["dil_out_norm"]], name="out_norm")
    x2, h3 = _mm(merged, w_out, res=x1, rows=(gains["ffn2_norm"],), outs=(F32, BF16),
                 epilogue=_residual_then_norm(1.0), name="out_proj_norm")
    (dx3, d_final, loss_wide), saved2 = _ffn_fwd(x2, gains["ffn2_norm"], weights, tag="ffn2", ex=ex, h=h3,
                                                 head=(gains["final_norm"], target))
    loss_row = loss_wide[:, :LANES]

    dx2, d_ffn2_norm, dwg2, dwu2, dwd2 = _ffn_bwd(dx3, x2, gains["ffn2_norm"], weights, saved2, tag="ffn2", ex=ex)
    d_w_out = _mm(merged, dx2, ta=True, outs=(GRAD_WIRE,), name="d_w_out")
    do_sb, d_sb_norm = _mm(dx2, w_out, tb=True, b_cols=(0, d_sb), extras=(o_sb,), rows=(gains["sb_out_norm"],),
                           row_sums=1, epilogue=_rms_bwd_epilogue, name="d_merged_sb")
    do_dl, d_dl_norm = _mm(dx2, w_out, tb=True, b_cols=(d_sb, d_dl), extras=(o_dl,), rows=(gains["dil_out_norm"],),
                           row_sums=1, epilogue=_rms_bwd_epilogue, name="d_merged_dl")
    plan = ex.send("ffn2", dict(ffn2_w_gate=dwg2, ffn2_w_up=dwu2, ffn2_w_down=dwd2))
    dq_sb, dk_sb, dv_sb, *got = _sb_bwd(p_sb, do_sb, carries, name="sb_bwd", comm=plan)
    ex.received("ffn2", got[0] if got else None)
    dq_dl, dk_dl, dv_dl = _dl_bwd(p_dl, o_dl, lse_dl, do_dl, name="dl_bwd")
    d_proj = _join_d_proj([dq_sb, dk_sb, dv_sb, dq_dl, dk_dl, dv_dl], (3, 4), cos_t, sin_t, name="d_proj")
    d_w_in = _mm(h2, d_proj, ta=True, outs=(GRAD_WIRE,), name="d_w_in")
    dx1, d_mix_norm = _mm(d_proj, w_in, extras=(x1, dx2), rows=(gains["mix_norm"],), row_sums=1,
                          epilogue=_rms_bwd_epilogue, name="dh_mix_norm_bwd")
    dx, d_ffn1_norm, dwg1, dwu1, dwd1 = _ffn_bwd(
        dx1, x, gains["ffn1_norm"], weights, saved1, tag="ffn1", ex=ex,
        rider=("mixer", dict(w_in=d_w_in, w_out=d_w_out)), spread=True)
    gain_grads = dict(ffn1_norm=d_ffn1_norm, mix_norm=d_mix_norm, sb_out_norm=d_sb_norm, dil_out_norm=d_dl_norm,
                      ffn2_norm=d_ffn2_norm, final_norm=d_final)
    weight_grads = dict(ffn1_w_gate=dwg1, ffn1_w_up=dwu1, ffn1_w_down=dwd1, w_in=d_w_in, w_out=d_w_out,
                        ffn2_w_gate=dwg2, ffn2_w_up=dwu2, ffn2_w_down=dwd2)
    return loss_row, dx, gain_grads, weight_grads


def _mesh_position():
    return lax.axis_index("x"), lax.axis_index("y"), lax.axis_index("c")


def _flip(coord, bit):
    return 1 - coord if bit else coord


RELATIONS = [(rx, ry, rc) for rx in (0, 1) for ry in (0, 1) for rc in (0, 1)][1:]


class _GatherPlan:
    def __init__(self, shards):
        n = len(shards)
        self.operands = list(shards)
        self.out_shapes = [jax.ShapeDtypeStruct((N_DEV,) + s.shape, s.dtype) for s in shards]
        self.scratch = [pltpu.SemaphoreType.DMA((n, 7)), pltpu.SemaphoreType.DMA((n, 7)),
                        pltpu.SemaphoreType.DMA((n,))]

    def _copies(self, in_refs, out_refs, sems):
        send_sems, recv_sems, local_sems = sems
        x, y, c = _mesh_position()
        me, sibling = (x, y, c), (x, y, 1 - c)
        chips = [(1 - x, y), (x, 1 - y), (1 - x, 1 - y)]
        plans = []
        for t, (x_ref, out_ref) in enumerate(zip(in_refs, out_refs)):
            def slot(px, py, pc, out_ref=out_ref):
                return out_ref.at[4 * px + 2 * py + pc]

            def copy(k, block, to, src=None, t=t, slot=slot):
                return pltpu.make_async_remote_copy(
                    src_ref=slot(*block) if src is None else src, dst_ref=slot(*block),
                    send_sem=send_sems.at[t, k], recv_sem=recv_sems.at[t, k],
                    device_id=to, device_id_type=pl.DeviceIdType.MESH)

            plans.append(dict(
                mine=pltpu.make_async_copy(x_ref, slot(*me), local_sems.at[t]),
                first=[copy(0, me, sibling, src=x_ref)]
                + [copy(1 + j, me, (*chip, c), src=x_ref) for j, chip in enumerate(chips)],
                over_ici=[copy(1 + j, (*chip, c), me) for j, chip in enumerate(chips)],
                passed=[copy(4 + j, (*chip, c), sibling) for j, chip in enumerate(chips)],
                from_sibling=[copy(0, sibling, me)] + [copy(4 + j, (*chip, 1 - c), me) for j, chip in enumerate(chips)]))
        return plans

    def start(self, in_refs, out_refs, sems):
        for p in self._copies(in_refs, out_refs, sems):
            p["mine"].start()
            for cp in p["first"]:
                cp.start()

    def finish(self, in_refs, out_refs, sems):
        plans = self._copies(in_refs, out_refs, sems)
        for p in plans:
            for arrived, onward in zip(p["over_ici"], p["passed"]):
                arrived.wait_recv()
                onward.start()
        for p in plans:
            for cp in p["from_sibling"]:
                cp.wait_recv()
            for cp in p["first"] + p["passed"]:
                cp.wait_send()
            p["mine"].wait()


class _Hosted:
    def __init__(self, plan, n_in, n_out, n_scratch):
        self.plan, self.n_in, self.n_out, self.n_scratch = plan, n_in, n_out, n_scratch
        self.operands = list(plan.operands) if plan else []
        self.out_shapes = list(plan.out_shapes) if plan else []
        self.scratch = list(plan.scratch) if plan else []
        self.in_specs = [pl.BlockSpec(memory_space=pl.ANY)] * len(self.operands)
        self.out_specs = [pl.BlockSpec(memory_space=pl.ANY)] * len(self.out_shapes)

    def semantics(self, sem):
        return sem if self.plan is None else ("arbitrary",) * len(sem)

    def _at(self, grid, last):
        hit = None
        for d, n in enumerate(grid):
            here = pl.program_id(d) == (n - 1 if last else 0)
            hit = here if hit is None else hit & here
        return hit

    def begin(self, refs, grid):
        if self.plan is None:
            return refs
        k_in, k_out = len(self.operands), len(self.out_shapes)
        ins, rest = refs[:self.n_in], refs[self.n_in:]
        c_in, rest = rest[:k_in], rest[k_in:]
        outs, rest = rest[:self.n_out], rest[self.n_out:]
        c_out, rest = rest[:k_out], rest[k_out:]
        scratch, sems = rest[:self.n_scratch], rest[self.n_scratch:]
        self._args = (c_in, c_out, sems)
        pl.when(self._at(grid, False))(lambda: self.plan.start(*self._args))
        return tuple(ins) + tuple(outs) + tuple(scratch)

    def end(self, grid):
        if self.plan is not None:
            pl.when(self._at(grid, True))(lambda: self.plan.finish(*self._args))


class _ExchangePlan:
    def __init__(self, packs):
        n = len(packs)
        self.operands = list(packs)
        self.out_shapes = [jax.ShapeDtypeStruct(p.shape, p.dtype) for p in packs]
        self.scratch = [pltpu.SemaphoreType.DMA((n, 7)), pltpu.SemaphoreType.DMA((n, 7)),
                        pltpu.SemaphoreType.DMA((n,))]

    def _copies(self, in_refs, out_refs, sems):
        send_sems, recv_sems, local_sems = sems
        x, y, c = _mesh_position()
        me = 4 * x + 2 * y + c
        copies = [pltpu.make_async_copy(i.at[me], o.at[me], local_sems.at[t])
                  for t, (i, o) in enumerate(zip(in_refs, out_refs))]
        for r, (rx, ry, rc) in enumerate(RELATIONS):
            px, py, pc = _flip(x, rx), _flip(y, ry), _flip(c, rc)
            peer = 4 * px + 2 * py + pc
            copies += [pltpu.make_async_remote_copy(
                src_ref=i.at[peer], dst_ref=o.at[me], send_sem=send_sems.at[t, r], recv_sem=recv_sems.at[t, r],
                device_id=(px, py, pc), device_id_type=pl.DeviceIdType.MESH)
                for t, (i, o) in enumerate(zip(in_refs, out_refs))]
        return copies

    def start(self, in_refs, out_refs, sems):
        for cp in self._copies(in_refs, out_refs, sems):
            cp.start()

    def finish(self, in_refs, out_refs, sems):
        for cp in self._copies(in_refs, out_refs, sems):
            cp.wait()


def _all_reduce_rows(v, *, name):
    R, C = v.shape

    def body(v_ref, out_ref, buf, send_sems, recv_sems):
        x, y, c = _mesh_position()
        me = 4 * x + 2 * y + c
        buf[me] = v_ref[...]
        copies = []
        for r, (rx, ry, rc) in enumerate(RELATIONS):
            cp = pltpu.make_async_remote_copy(
                src_ref=v_ref, dst_ref=buf.at[me], send_sem=send_sems.at[r], recv_sem=recv_sems.at[r],
                device_id=(_flip(x, rx), _flip(y, ry), _flip(c, rc)), device_id_type=pl.DeviceIdType.MESH)
            cp.start()
            copies.append(cp)
        for cp in copies:
            cp.wait()
        total = buf[0]
        for s in range(1, N_DEV):
            total = total + buf[s]
        out_ref[...] = total

    return pl.pallas_call(
        body, name=name,
        out_shape=jax.ShapeDtypeStruct((R, C), F32),
        in_specs=[pl.BlockSpec(memory_space=pltpu.VMEM)],
        out_specs=pl.BlockSpec(memory_space=pltpu.VMEM),
        scratch_shapes=[pltpu.VMEM((N_DEV, R, C), F32), pltpu.SemaphoreType.DMA((7,)), pltpu.SemaphoreType.DMA((7,))],
    )(v)


def _adamw(w, g, m, v, *, name):
    R, C = w.shape
    slots = g.ndim == 3
    tr = _pick(R, (256, 128, 64, 32, 16) if slots else (256, 128, 64, 32, 16, 8))

    def body(w_ref, g_ref, m_ref, v_ref, g_out, d_ref, nm_ref, nv_ref):
        if slots:
            g = g_ref[0].astype(F32)
            for s in range(1, N_DEV):
                g = g + g_ref[s].astype(F32)
        else:
            g = g_ref[...]
        g_out[...] = g
        m_new = ADAM_B1 * m_ref[...] + (1.0 - ADAM_B1) * g
        v_new = ADAM_B2 * v_ref[...] + (1.0 - ADAM_B2) * (g * g)
        m_hat = m_new / (1.0 - ADAM_B1 ** ADAM_STEP)
        v_hat = v_new / (1.0 - ADAM_B2 ** ADAM_STEP)
        d_ref[...] = -ADAM_LR * (m_hat / (jnp.sqrt(v_hat) + ADAM_EPS) + ADAM_WD * w_ref[...])
        nm_ref[...] = m_new
        nv_ref[...] = v_new

    spec = pl.BlockSpec((tr, C), lambda i: (i, 0))
    g_spec = pl.BlockSpec((N_DEV, tr, C), lambda i: (0, i, 0)) if slots else spec
    return pl.pallas_call(
        body, name=name, grid=(R // tr,),
        out_shape=[jax.ShapeDtypeStruct((R, C), F32)] * 4,
        in_specs=[spec, g_spec, spec, spec], out_specs=[spec] * 4,
        compiler_params=_cparams(("parallel",)),
    )(w, g, m, v)


WEIGHT_NAMES = ["ffn1_norm", "ffn1_w_gate", "ffn1_w_up", "ffn1_w_down", "mix_norm", "w_in", "sb_out_norm",
                "dil_out_norm", "w_out", "ffn2_norm", "ffn2_w_gate", "ffn2_w_up", "ffn2_w_down", "final_norm"]
GAIN_NAMES = ["ffn1_norm", "mix_norm", "sb_out_norm", "dil_out_norm", "ffn2_norm", "final_norm"]
COL_SHARDED = ["ffn1_w_gate", "ffn1_w_up", "ffn2_w_gate", "ffn2_w_up", "w_in"]
ROW_SHARDED = ["ffn1_w_down", "ffn2_w_down", "w_out"]
GROUPS = {"mixer": (["w_in"], ["w_out"]),
          "ffn2": (["ffn2_w_gate", "ffn2_w_up"], ["ffn2_w_down"])}
for _ffn in ("ffn1", "ffn2"):
    GROUPS.update({f"{_ffn}_w_gate": ([f"{_ffn}_w_gate"], []), f"{_ffn}_w_up": ([f"{_ffn}_w_up"], []),
                   f"{_ffn}_w_down": ([], [f"{_ffn}_w_down"])})


class _Exchanges:
    def __init__(self, params):
        self.params = params
        self.grads = {}

    def gather(self, group):
        if group is None:
            return None
        cols, rows = GROUPS[group]
        return _GatherPlan([self.params[n].T.astype(BF16) for n in cols] + [self.params[n].astype(BF16) for n in rows])

    def gathered(self, group, got, weights):
        if group is None:
            return
        cols, rows = GROUPS[group]
        for n, blocks in zip(cols + rows, got):
            weights[n] = blocks.reshape(N_DEV * blocks.shape[1], blocks.shape[2])

    def send(self, group, grads):
        if group is None:
            return None
        cols, rows = GROUPS[group]
        packs = [jnp.transpose(grads[n].reshape(grads[n].shape[0], N_DEV, self.params[n].shape[1]), (1, 0, 2))
                 for n in cols]
        packs += [grads[n].reshape(N_DEV, self.params[n].shape[0], grads[n].shape[1]) for n in rows]
        return _ExchangePlan([p.astype(GRAD_WIRE) for p in packs])

    def received(self, group, got):
        if group is None:
            return
        cols, rows = GROUPS[group]
        for n, slots in zip(cols + rows, got):
            self.grads[n] = slots


def _step(x, target, params, moments_m, moments_v):
    ex = _Exchanges(params)
    weights = {}
    gains = {n: params[n] for n in GAIN_NAMES}
    loss_row, grad_x, gain_grads, _ = _local_step(x, target, gains, weights, ex)
    grads = ex.grads

    rows = [gain_grads[n].reshape(-1, LANES) for n in GAIN_NAMES] + [loss_row]
    small = jnp.concatenate(rows, axis=0)
    pad = (-small.shape[0]) % 8
    small = jnp.pad(small, ((0, pad), (0, 0)))
    small = _all_reduce_rows(small, name="reduce_gains_loss")
    off = 0
    for n in GAIN_NAMES:
        r = gain_grads[n].shape[1] // LANES
        grads[n] = small[off:off + r].reshape(1, -1)
        off += r
    loss = small[off, 0]

    delta, new_m, new_v = {}, {}, {}
    for n in WEIGHT_NAMES:
        grads[n], delta[n], new_m[n], new_v[n] = _adamw(params[n], grads[n], moments_m[n], moments_v[n],
                                                        name=f"adamw_{n}")
    return loss, grad_x, grads, delta, new_m, new_v


def kernel(x, ffn1_norm, ffn1_w_gate, ffn1_w_up, ffn1_w_down, mix_norm, w_in, sb_out_norm, dil_out_norm, w_out, ffn2_norm, ffn2_w_gate, ffn2_w_up, ffn2_w_down, final_norm, loss_target, m_ffn1_norm, m_ffn1_w_gate, m_ffn1_w_up, m_ffn1_w_down, m_mix_norm, m_w_in, m_sb_out_norm, m_dil_out_norm, m_w_out, m_ffn2_norm, m_ffn2_w_gate, m_ffn2_w_up, m_ffn2_w_down, m_final_norm, v_ffn1_norm, v_ffn1_w_gate, v_ffn1_w_up, v_ffn1_w_down, v_mix_norm, v_w_in, v_sb_out_norm, v_dil_out_norm, v_w_out, v_ffn2_norm, v_ffn2_w_gate, v_ffn2_w_up, v_ffn2_w_down, v_final_norm):
    given = dict(locals())
    shapes = {n: given[n].shape for n in WEIGHT_NAMES}

    def as2d(a):
        return a.reshape(1, -1) if a.ndim == 1 else a.reshape(a.shape[-2], a.shape[-1])

    params = {n: as2d(given[n]) for n in WEIGHT_NAMES}
    moments_m = {n: as2d(given["m_" + n]) for n in WEIGHT_NAMES}
    moments_v = {n: as2d(given["v_" + n]) for n in WEIGHT_NAMES}
    loss, grad_x, grads, delta, new_m, new_v = _step(x[0], loss_target[0], params, moments_m, moments_v)
    back = lambda d: [d[n].reshape(shapes[n]) for n in WEIGHT_NAMES]
    return (loss, grad_x[None], *back(grads), *back(delta), *back(new_m), *back(new_v))
```

```python
import functools

import jax
import jax.numpy as jnp
from jax import lax
from jax.experimental import pallas as pl
from jax.experimental.pallas import tpu as pltpu

F32 = jnp.float32
BF16 = jnp.bfloat16
GRAD_WIRE = jnp.bfloat16

N_DEV = 8
HEAD_DIM = 64
LANES = 128
DILATED_PATTERNS = ((128, 1), (512, 4), (2048, 16))
DIL_BLOCK = 128
DIL_SUPER = 2048
DIL_UNROLL = 16
SB_TILE = 256
SB_LANES = 128
SB_UNROLL = 4
SB_STEP_TILES = 2
SB_DEAD = 90.0
SB_UNSEEN = -1e30
ROPE_THETA = 10000.0
RMS_EPS = 1e-6
HALF_STEP = 0.5
ADAM_LR = 0.001
ADAM_B1 = 0.9
ADAM_B2 = 0.999
ADAM_EPS = 1e-08
ADAM_WD = 0.01
ADAM_STEP = 10
NEG_BIG = -1e30
VMEM_CAP_MB = 60


def _pick(n, prefs):
    for p in prefs:
        if n % p == 0:
            return p
    return n


MM_MAX_TILE = 1536
MM_WHOLE = 3072


def _largest_tile(n, cap):
    if n <= cap:
        return n
    for t in range(cap - cap % LANES, 0, -LANES):
        if n % t == 0:
            return t
    return n


def _cparams(sem=None, vmem_mb=48):
    return pltpu.CompilerParams(dimension_semantics=sem, vmem_limit_bytes=min(vmem_mb, VMEM_CAP_MB) * 1024 * 1024)


def _nbytes(shape, dtype):
    n = 1
    for s in shape:
        n *= s
    return n * jnp.dtype(dtype).itemsize


def _mm(a, b, *, name, ta=False, tb=False, outs=(F32,), res=None, alpha=1.0, extras=(), epilogue=None,
        tm=None, tn=None, tk=None, comm=None, rows=(), lanes=(), row_sums=0, b_cols=None, second=None,
        twin_b=None):
    if ta:
        K, M = a.shape
    else:
        M, K = a.shape
    if tb:
        N, Kb = b.shape
    else:
        Kb, N = b.shape
    col0 = 0
    if b_cols is not None:
        col0, N = b_cols
    assert K == Kb, (a.shape, b.shape, ta, tb)
    tn = tn or (N if (not ta and K <= MM_WHOLE and N <= MM_WHOLE) else _largest_tile(N, MM_MAX_TILE))
    wide = tn > MM_MAX_TILE and (len(extras) + len(outs) > 3 or a.dtype == F32)
    tm = tm or (_largest_tile(M, MM_MAX_TILE) if ta else _pick(M, (256, 128) if wide else (512, 256, 128)))
    tk = tk or (K if K <= MM_WHOLE else _pick(K, (2048, 1024, 512, 256, 128)))
    nk = K // tk
    a_spec = pl.BlockSpec((tk, tm), lambda i, j, k: (k, i)) if ta else pl.BlockSpec((tm, tk), lambda i, j, k: (i, k))
    assert col0 % tn == 0
    b_spec = (pl.BlockSpec((tn, tk), lambda i, j, k: (j + col0 // tn, k)) if tb
              else pl.BlockSpec((tk, tn), lambda i, j, k: (k, j + col0 // tn)))
    mn_spec = pl.BlockSpec((tm, tn), lambda i, j, k: (i, j))
    dims = (((0 if ta else 1,), (1 if tb else 0,)), ((), ()))
    row_spec = pl.BlockSpec((1, tn), lambda i, j, k: (0, j))
    lane_spec = pl.BlockSpec((tm, LANES), lambda i, j, k: (i, 0))
    n_extra = len(extras) + (1 if res is not None else 0) + len(rows) + len(lanes)
    n_mn = len(outs)
    n_out = n_mn + row_sums
    assert row_sums == 0 or tn == N
    grid = (M // tm, N // tn, nk)
    n_ab = 2 + (2 if second is not None else 0) + (1 if twin_b is not None else 0)
    assert twin_b is None or (nk == 1 and epilogue is not None)
    hosted = _Hosted(comm, n_in=n_ab + n_extra, n_out=n_out, n_scratch=1 if nk > 1 else 0)

    def body(*refs):
        in_refs = refs[n_ab:n_ab + n_extra]
        ab_refs = refs[:n_ab]
        refs = hosted.begin(refs, grid)
        out_refs = refs[n_ab + n_extra:n_ab + n_extra + n_out]
        def product(a_ref, b_ref):
            return lax.dot_general(a_ref[...].astype(BF16), b_ref[...].astype(BF16), dims, preferred_element_type=F32)

        prod = product(ab_refs[0], ab_refs[1])
        if second is not None:
            prod = prod + product(ab_refs[2], ab_refs[3])
        if twin_b is not None:
            prod = (prod, product(ab_refs[0], ab_refs[-1]))

        def finish(acc):
            blocks = [r[...] for r in in_refs]
            if res is not None:
                r_blk, blocks = blocks[0], blocks[1:]
            else:
                r_blk = None
            if epilogue is None:
                val = acc * alpha
                if r_blk is not None:
                    val = val + r_blk
                vals = (val,)
            else:
                vals = epilogue(acc, r_blk, *blocks)
                vals = vals if isinstance(vals, (tuple, list)) else (vals,)
            for o_ref, v in zip(out_refs[:n_mn], vals[:n_mn]):
                o_ref[...] = v.astype(o_ref.dtype)
            first_rows = pl.program_id(0) == 0
            for o_ref, part in zip(out_refs[n_mn:], vals[n_mn:]):
                @pl.when(first_rows)
                def _(o_ref=o_ref, part=part):
                    o_ref[...] = part

                @pl.when(jnp.logical_not(first_rows))
                def _(o_ref=o_ref, part=part):
                    o_ref[...] += part

        if nk == 1:
            finish(prod)
        else:
            acc_ref = refs[n_ab + n_extra + n_out]
            k = pl.program_id(2)

            @pl.when(k == 0)
            def _():
                acc_ref[...] = prod

            @pl.when(k > 0)
            def _():
                acc_ref[...] += prod

            @pl.when(k == nk - 1)
            def _():
                finish(acc_ref[...])

        hosted.end(grid)

    mn_operands = ([res] if res is not None else []) + list(extras)
    ab = [a, b] + (list(second) if second is not None else []) + ([twin_b] if twin_b is not None else [])
    ab_specs = [a_spec, b_spec] * (1 if second is None else 2) + ([b_spec] if twin_b is not None else [])
    operands = ab + mn_operands + list(rows) + list(lanes)
    in_specs = ab_specs + [mn_spec] * len(mn_operands) + [row_spec] * len(rows) + [lane_spec] * len(lanes)
    est = n_ab * (_nbytes((tm, tk), a.dtype) + _nbytes((tk, tn), b.dtype))
    est += 2 * sum(_nbytes((tm, tn), o.dtype) for o in mn_operands)
    est += 2 * sum(_nbytes((tm, tn), d) for d in outs) + 2 * _nbytes((tm, tn), F32)
    semantics = ("parallel", "parallel", "arbitrary") if row_sums == 0 else ("arbitrary",) * 3
    result = pl.pallas_call(
        body, name=name, grid=grid,
        out_shape=[jax.ShapeDtypeStruct((M, N), d) for d in outs]
        + [jax.ShapeDtypeStruct((1, N), F32)] * row_sums + hosted.out_shapes,
        in_specs=in_specs + hosted.in_specs,
        out_specs=[mn_spec] * n_mn + [row_spec] * row_sums + hosted.out_specs,
        scratch_shapes=([pltpu.VMEM((tm, tn), F32)] if nk > 1 else []) + hosted.scratch,
        compiler_params=_cparams(hosted.semantics(semantics), vmem_mb=max(32, 2 * est // (1024 * 1024))),
    )(*operands, *hosted.operands)
    own, got = result[:n_out], list(result[n_out:])
    own = own[0] if n_out == 1 else own
    return own if comm is None else (own, got)


def _rms_hat(x):
    r = lax.rsqrt(jnp.mean(x * x, axis=-1, keepdims=True) + RMS_EPS)
    return x * r, r


def _rms_fwd(xs, gains, *, name, comm=None):
    S = xs[0].shape[0]
    widths = [x.shape[1] for x in xs]
    tm = _pick(S, (512, 256, 128))
    n = len(xs)
    grid = (S // tm,)
    hosted = _Hosted(comm, n_in=2 * n, n_out=1, n_scratch=0)

    def body(*refs):
        refs = hosted.begin(refs, grid)
        o_ref = refs[2 * n]
        off = 0
        for i in range(n):
            xh, _ = _rms_hat(refs[i][...])
            o_ref[:, off:off + widths[i]] = (xh * refs[n + i][...]).astype(o_ref.dtype)
            off += widths[i]
        hosted.end(grid)

    out, *got = pl.pallas_call(
        body, name=name, grid=grid,
        out_shape=[jax.ShapeDtypeStruct((S, sum(widths)), BF16)] + hosted.out_shapes,
        in_specs=[pl.BlockSpec((tm, w), lambda i: (i, 0)) for w in widths]
        + [pl.BlockSpec((1, w), lambda i: (0, 0)) for w in widths] + hosted.in_specs,
        out_specs=[pl.BlockSpec((tm, sum(widths)), lambda i: (i, 0))] + hosted.out_specs,
        scratch_shapes=hosted.scratch,
        compiler_params=_cparams(hosted.semantics(("parallel",))),
    )(*xs, *gains, *hosted.operands)
    return out if comm is None else (out, got)


def _sigmoid(g):
    return 1.0 / (1.0 + jnp.exp(-g))


def _ride(result, plan):
    return result if plan is not None else (result, None)


def _residual_then_norm(alpha):
    def epilogue(acc, res, gain):
        y = res + alpha * acc
        return y, _rms_hat(y)[0] * gain
    return epilogue


def _ffn_fwd(x, gain, w, *, tag, ex, first_rider=None, riders=(None, None, None), head=None, h=None,
             next_gain=None):
    if h is None:
        plan = ex.gather(first_rider)
        h, got = _ride(_rms_fwd([x], [gain], name=f"{tag}_norm", comm=plan), plan)
        ex.gathered(first_rider, got, w)
    def act(acc, _, g_blk):
        gf = g_blk.astype(F32)
        return acc, gf * _sigmoid(gf) * acc

    if riders[0] is None and riders[1] is None and f"{tag}_w_up" in w:
        def gate_up_act(accs, _):
            g_blk = accs[0].astype(BF16)
            return (g_blk,) + act(accs[1], None, g_blk)

        g, u, a = _mm(h, w[f"{tag}_w_gate"], tb=True, twin_b=w[f"{tag}_w_up"], outs=(BF16, BF16, BF16),
                      epilogue=gate_up_act, tm=256, name=f"{tag}_gate_up_act")
    else:
        plan = ex.gather(riders[0])
        g, got = _ride(_mm(h, w[f"{tag}_w_gate"], tb=True, outs=(BF16,), name=f"{tag}_gate", comm=plan), plan)
        ex.gathered(riders[0], got, w)
        plan = ex.gather(riders[1])
        (u, a), got = _ride(_mm(h, w[f"{tag}_w_up"], tb=True, outs=(BF16, BF16), extras=(g,), epilogue=act,
                                name=f"{tag}_up_act", comm=plan), plan)
        ex.gathered(riders[1], got, w)
    plan = ex.gather(riders[2])
    if head is None and next_gain is None:
        y, got = _ride(_mm(a, w[f"{tag}_w_down"], res=x, alpha=HALF_STEP, name=f"{tag}_down", comm=plan), plan)
    elif head is None:
        y, got = _ride(_mm(a, w[f"{tag}_w_down"], res=x, rows=(next_gain,), outs=(F32, BF16),
                           epilogue=_residual_then_norm(HALF_STEP), name=f"{tag}_down_norm", comm=plan), plan)
    else:
        final_gain, target = head
        y, got = _ride(_mm(a, w[f"{tag}_w_down"], res=x, extras=(target,), rows=(final_gain,), row_sums=2,
                           epilogue=_loss_head_epilogue, name=f"{tag}_down_loss", comm=plan), plan)
    ex.gathered(riders[2], got, w)
    return y, (h, g, u, a)


def _loss_head_epilogue(acc, x_in, target, gain):
    xh, r = _rms_hat(x_in + HALF_STEP * acc)
    err = xh * gain - target
    dy = err * (1.0 / acc.shape[1])
    dxh = dy * gain
    dx = r * (dxh - xh * jnp.mean(dxh * xh, axis=-1, keepdims=True))
    loss = 0.5 * jnp.sum(jnp.mean(err * err, axis=-1, keepdims=True), axis=0, keepdims=True)
    return dx, jnp.sum(dy * xh, axis=0, keepdims=True), jnp.zeros_like(gain) + loss


def _rms_bwd_epilogue(acc, dh_so_far, x, *dres_and_gain):
    gain = dres_and_gain[-1]
    dh = acc if dh_so_far is None else acc + dh_so_far
    xh, r = _rms_hat(x)
    dxh = dh * gain
    dx = r * (dxh - xh * jnp.mean(dxh * xh, axis=-1, keepdims=True))
    if len(dres_and_gain) == 2:
        dx = dx + dres_and_gain[0]
    return dx, jnp.sum(dh * xh, axis=0, keepdims=True)


def _ffn_bwd(dout, x, gain, w, saved, *, tag, ex, rider=(None, None), spread=False):
    h, g, u, a = saved
    wg, wu, wd = (w[f"{tag}_w_{n}"] for n in ("gate", "up", "down"))

    def act_bwd(acc, _, g_blk, u_blk):
        gf, uf = g_blk.astype(F32), u_blk.astype(F32)
        da = acc * HALF_STEP
        sig = _sigmoid(gf)
        silu = gf * sig
        return da * uf * (sig + silu * (1.0 - sig)), da * silu

    def carrying(group, grad, call):
        group = group if spread else None
        plan = ex.send(group, {group: grad})
        out, got = _ride(call(plan), plan)
        ex.received(group, got)
        return out

    plan = ex.send(*rider)
    (dg, du), got = _ride(_mm(dout, wd, tb=True, outs=(BF16, BF16), extras=(g, u), epilogue=act_bwd,
                              name=f"{tag}_bwd_act", comm=plan), plan)
    ex.received(rider[0], got)
    dwg = _mm(h, dg, ta=True, outs=(GRAD_WIRE,), name=f"{tag}_dwg")
    dwu = carrying(f"{tag}_w_gate", dwg, lambda plan: _mm(h, du, ta=True, outs=(GRAD_WIRE,), name=f"{tag}_dwu", comm=plan))
    dwd = carrying(f"{tag}_w_up", dwu,
                   lambda plan: _mm(a, dout, ta=True, outs=(GRAD_WIRE,), alpha=HALF_STEP, name=f"{tag}_dwd", comm=plan))
    dx, dgain = carrying(f"{tag}_w_down", dwd, lambda plan: _mm(
        dg, wg, second=(du, wu), extras=(x, dout), rows=(gain,), row_sums=1, epilogue=_rms_bwd_epilogue,
        tm=256, name=f"{tag}_dh_norm_bwd", comm=plan))
    return dx, dgain, dwg, dwu, dwd


def _rope_tables(S):
    half = HEAD_DIM // 2
    inv_freq = ROPE_THETA ** (-jnp.arange(half, dtype=F32) / half)
    ang = jnp.arange(S, dtype=F32)[:, None] * inv_freq[None, :]
    cos, sin = jnp.cos(ang), jnp.sin(ang)
    reps = LANES // HEAD_DIM
    cos_t = jnp.tile(jnp.concatenate([cos, cos], axis=1), (1, reps))
    sin_t = jnp.tile(jnp.concatenate([-sin, sin], axis=1), (1, reps))
    return cos_t, sin_t


def _rotate(v, cos, sin, sign):
    half = HEAD_DIM // 2
    groups = []
    for g in range(v.shape[1] // LANES):
        t = v[:, g * LANES:(g + 1) * LANES]
        lane = lax.broadcasted_iota(jnp.int32, t.shape, 1)
        swapped = jnp.where(lane % HEAD_DIM < half, pltpu.roll(t, LANES - half, axis=1), pltpu.roll(t, half, axis=1))
        groups.append(t * cos + swapped * (sin * sign))
    return groups[0] if len(groups) == 1 else jnp.concatenate(groups, axis=1)


def _join_d_proj(pieces, rotated, cos_t, sin_t, *, name):
    S = pieces[0].shape[0]
    widths = [p.shape[1] for p in pieces]
    tm = _pick(S, (256, 128))
    n = len(pieces)

    def body(*refs):
        c_ref, s_ref, o_ref = refs[n], refs[n + 1], refs[n + 2]
        off = 0
        for i in range(n):
            v = refs[i][...]
            if i in rotated:
                v = _rotate(v, c_ref[...], s_ref[...], -1.0)
            o_ref[:, off:off + widths[i]] = v.astype(o_ref.dtype)
            off += widths[i]

    return pl.pallas_call(
        body, name=name, grid=(S // tm,),
        out_shape=jax.ShapeDtypeStruct((S, sum(widths)), BF16),
        in_specs=[pl.BlockSpec((tm, w), lambda i: (i, 0)) for w in widths]
        + [pl.BlockSpec((tm, LANES), lambda i: (i, 0))] * 2,
        out_specs=pl.BlockSpec((tm, sum(widths)), lambda i: (i, 0)),
        compiler_params=_cparams(("parallel",)),
    )(*pieces, cos_t, sin_t)


def _head_masks(shape):
    lane = lax.broadcasted_iota(jnp.int32, shape, 1)
    return [(lane >= HEAD_DIM * h) & (lane < HEAD_DIM * (h + 1)) for h in range(shape[1] // HEAD_DIM)]


def _sb_scores(q2, k_j):
    z = lax.dot_general(q2, k_j, (((1,), (1,)), ((), ())), preferred_element_type=F32)
    sign_bit = jnp.int32(-2 ** 31)
    minus_abs = lax.bitcast_convert_type(lax.bitcast_convert_type(z, jnp.int32) | sign_bit, F32)
    softplus = jnp.maximum(z, 0.0) + jnp.log(1.0 + jnp.exp(minus_abs))
    return z - softplus, softplus


def _sb_stack_heads(t, scale=None):
    parts = [jnp.where(hm, t, jnp.zeros_like(t)) for hm in _head_masks(t.shape)]
    t2 = jnp.concatenate(parts, axis=0)
    if scale is not None:
        t2 = (t2.astype(F32) * scale).astype(t2.dtype)
    return t2


def _sb_unstack_heads(t2):
    n = t2.shape[1] // HEAD_DIM
    T = t2.shape[0] // n
    masks = _head_masks((T, t2.shape[1]))
    out = t2[:T]
    for h in range(1, n):
        out = jnp.where(masks[h], t2[h * T:(h + 1) * T], out)
    return out


def _sb_causal(T, n_heads):
    row = lax.broadcasted_iota(jnp.int32, (n_heads * T, T), 0)
    col = lax.broadcasted_iota(jnp.int32, (n_heads * T, T), 1)
    return col < row % T


def _sb_triangle(T, later):
    row = lax.broadcasted_iota(jnp.int32, (T, T), 0)
    col = lax.broadcasted_iota(jnp.int32, (T, T), 1)
    return ((row > col) if later else (row < col)).astype(BF16)


def _sb_fwd(p_sb, *, name, comm=None):
    S = p_sb.shape[0]
    W = p_sb.shape[1] // 3
    LW = min(SB_LANES, W)
    NH = LW // HEAD_DIM
    npair = W // LW
    T = SB_TILE
    n_tiles = S // T
    assert n_tiles <= HEAD_DIM
    scale = HEAD_DIM ** -0.5

    R = SB_STEP_TILES
    grid = (npair, n_tiles // R)
    hosted = _Hosted(comm, n_in=5, n_out=2, n_scratch=0)

    def body(*refs):
        refs = hosted.begin(refs, grid)
        step = pl.program_id(1)
        lax.fori_loop(0, R, lambda sub, _: query_tile(step * R + sub, sub, *refs), 0)
        hosted.end(grid)

    def query_tile(I, sub, q_ref, k_ref, v_ref, causal_ref, later_ref, o_ref, c_ref):
        rows = pl.ds(pl.multiple_of(sub * T, T), T)
        lane = lax.broadcasted_iota(jnp.int32, (T, LW), 1)
        causal = causal_ref[...]
        later_than = later_ref[...]
        q2 = _sb_stack_heads(q_ref[rows, :], scale)

        def scores(J, diag):
            off = pl.multiple_of(J * T, T)
            log_beta, stay = _sb_scores(q2, k_ref[pl.ds(off, T), :])
            if diag:
                stay = stay * causal
            local = jnp.dot(stay.astype(BF16), later_than, preferred_element_type=F32)
            return log_beta, local, jnp.sum(stay, axis=1, keepdims=True), v_ref[pl.ds(off, T), :]

        def weigh(J, sc, gone, acc, carr, diag):
            log_beta, local, _, v_j = sc
            w = jnp.exp((log_beta - gone) - local)
            if diag:
                w = w * causal
            acc = acc + jnp.dot(w.astype(BF16), v_j, preferred_element_type=F32)
            for h in range(NH):
                carr = jnp.where(lane == HEAD_DIM * h + J, -gone[h * T:(h + 1) * T], carr)
            return acc, carr

        def tiles(J, count, state, diag):
            gone, acc, carr, _ = state
            scs = [scores(J - u, diag and u == 0) for u in range(count)]
            for u, sc in enumerate(scs):
                acc, carr = weigh(J - u, sc, gone, acc, carr, diag and u == 0)
                gone = gone + sc[2]
            return gone, acc, carr, jnp.min(gone)

        U = SB_UNROLL
        alive = lambda st: st[3] < SB_DEAD
        state = (jnp.zeros((NH * T, 1), F32), jnp.zeros((NH * T, LW), F32),
                 jnp.full((T, LW), SB_UNSEEN, F32), jnp.zeros((), F32))
        state = lax.cond(I > 0, lambda st: tiles(I, 2, st, True), lambda st: tiles(I, 1, st, True), state)
        rest = jnp.maximum(I - 1, 0)
        singles = jnp.where(rest > 0, (rest - 1) % U + 1, 0)
        _, state = lax.while_loop(lambda c: (c[0] < singles) & alive(c[1]),
                                  lambda c: (c[0] + 1, tiles(I - 2 - c[0], 1, c[1], False)), (jnp.int32(0), state))
        blocks = (rest - singles) // U
        _, state = lax.while_loop(lambda c: (c[0] < blocks) & alive(c[1]),
                                  lambda c: (c[0] + 1, tiles(I - 2 - singles - U * c[0], U, c[1], False)),
                                  (jnp.int32(0), state))
        _, acc, carr, _ = state
        o_ref[rows, :] = _sb_unstack_heads(acc)
        c_ref[rows, :] = carr
        return 0

    blk = lambda I_off: pl.BlockSpec((R * T, LW), lambda p, I: (I, I_off + p))
    full = lambda off: pl.BlockSpec((S, LW), lambda p, I: (0, off + p))
    const = lambda rows: pl.BlockSpec((rows, T), lambda p, I: (0, 0))
    o, carries, *got = pl.pallas_call(
        body, name=name, grid=grid,
        out_shape=[jax.ShapeDtypeStruct((S, W), F32), jax.ShapeDtypeStruct((S, W), F32)] + hosted.out_shapes,
        in_specs=[blk(0), full(npair), full(2 * npair), const(NH * T), const(T)] + hosted.in_specs,
        out_specs=[blk(0), blk(0)] + hosted.out_specs,
        scratch_shapes=hosted.scratch,
        compiler_params=_cparams(hosted.semantics(("parallel", "arbitrary")), vmem_mb=56),
    )(p_sb, p_sb, p_sb, _sb_causal(T, NH).astype(F32), _sb_triangle(T, True), *hosted.operands)
    return (o, carries) if comm is None else (o, carries, got)


def _sb_bwd(p_sb, do, carries, *, name, comm=None):
    S = p_sb.shape[0]
    W = p_sb.shape[1] // 3
    LW = min(LANES, W)
    NH = LW // HEAD_DIM
    npair = W // LW
    T = SB_TILE
    n_tiles = S // T
    scale = HEAD_DIM ** -0.5

    R = SB_STEP_TILES
    grid = (npair, n_tiles // R)
    hosted = _Hosted(comm, n_in=8, n_out=3, n_scratch=0)

    def body(*refs):
        refs = hosted.begin(refs, grid)
        dk_ref, dv_ref = refs[9], refs[10]
        step = pl.program_id(1)

        @pl.when(step == 0)
        def _():
            dk_ref[...] = jnp.zeros_like(dk_ref)
            dv_ref[...] = jnp.zeros_like(dv_ref)

        lax.fori_loop(0, R, lambda sub, _: query_tile(step * R + sub, sub, *refs), 0)
        hosted.end(grid)

    def query_tile(I, sub, q_ref, k_ref, v_ref, do_ref, c_ref, causal_ref, later_ref, earlier_ref,
                   dq_ref, dk_ref, dv_ref):
        rows = pl.ds(pl.multiple_of(sub * T, T), T)
        lane = lax.broadcasted_iota(jnp.int32, (T, LW), 1)
        causal = causal_ref[...]
        later_than = later_ref[...]
        earlier_than = earlier_ref[...]
        q2 = _sb_stack_heads(q_ref[rows, :], scale)
        do2 = _sb_stack_heads(do_ref[rows, :].astype(BF16))
        carr = c_ref[rows, :]
        tn_dims = (((0,), (0,)), ((), ()))

        def chain(J, diag):
            off = pl.multiple_of(J * T, T)
            k_j = k_ref[pl.ds(off, T), :]
            v_j = v_ref[pl.ds(off, T), :]
            log_beta, stay = _sb_scores(q2, k_j)
            if diag:
                stay = stay * causal
            lc = jnp.concatenate(
                [jnp.sum(jnp.where(lane == HEAD_DIM * h + J, carr, 0.0), axis=1, keepdims=True) for h in range(NH)],
                axis=0)
            w = jnp.exp((log_beta + lc) - jnp.dot(stay.astype(BF16), later_than, preferred_element_type=F32))
            if diag:
                w = w * causal
            dw = lax.dot_general(do2, v_j, (((1,), (1,)), ((), ())), preferred_element_type=F32)
            e = w * dw
            local = jnp.dot(e.astype(BF16), earlier_than, preferred_element_type=F32)
            return off, k_j, w, e, local, jnp.exp(log_beta), jnp.sum(e, axis=1, keepdims=True)

        def finish(ch, ec, dq_acc, diag):
            off, k_j, w, e, local, beta, _ = ch
            e_before = local + ec
            dz = e - beta * (e + e_before)
            if diag:
                dz = dz * causal
            dzb = dz.astype(BF16)
            dq_acc = dq_acc + jnp.dot(dzb, k_j, preferred_element_type=F32)
            dk_ref[pl.ds(off, T), :] += lax.dot_general(dzb, q2, tn_dims, preferred_element_type=F32)
            dv_ref[pl.ds(off, T), :] += lax.dot_general(w.astype(BF16), do2, tn_dims, preferred_element_type=F32)
            return dq_acc

        def tiles(J, count, state, diag):
            ec, dq_acc = state
            chains = [chain(J + u, diag and u == count - 1) for u in range(count)]
            for u, ch in enumerate(chains):
                dq_acc = finish(ch, ec, dq_acc, diag and u == count - 1)
                ec = ec + ch[6]
            return ec, dq_acc

        lane_row = lax.broadcasted_iota(jnp.int32, (1, LW), 1)
        reached = (jnp.max(carr, axis=0, keepdims=True) > 0.5 * SB_UNSEEN) & (lane_row < HEAD_DIM)
        first = jnp.min(jnp.where(reached, lane_row.astype(F32), float(n_tiles))).astype(jnp.int32)
        U = SB_UNROLL
        count = I - first
        rest = jnp.maximum(count - 1, 0)
        state = (jnp.zeros((NH * T, 1), F32), jnp.zeros((NH * T, LW), F32))
        state = lax.fori_loop(0, rest // U, lambda jj, st: tiles(first + U * jj, U, st, False), state)
        state = lax.fori_loop(0, rest % U, lambda r, st: tiles(I - 1 - rest % U + r, 1, st, False), state)
        _, dq_acc = lax.cond(count > 0, lambda st: tiles(I - 1, 2, st, True), lambda st: tiles(I, 1, st, True), state)
        dq_ref[rows, :] = _sb_unstack_heads(dq_acc) * scale
        return 0

    blk = lambda src_off: pl.BlockSpec((R * T, LW), lambda p, I: (I, src_off + p))
    full = lambda off: pl.BlockSpec((S, LW), lambda p, I: (0, off + p))
    const = lambda rows: pl.BlockSpec((rows, T), lambda p, I: (0, 0))
    dq, dk, dv, *got = pl.pallas_call(
        body, name=name, grid=grid,
        out_shape=[jax.ShapeDtypeStruct((S, W), F32)] * 3 + hosted.out_shapes,
        in_specs=[blk(0), full(npair), full(2 * npair), blk(0), blk(0), const(NH * T), const(T), const(T)]
        + hosted.in_specs,
        out_specs=[blk(0), full(0), full(0)] + hosted.out_specs,
        scratch_shapes=hosted.scratch,
        compiler_params=_cparams(hosted.semantics(("parallel", "arbitrary")), vmem_mb=56),
    )(p_sb, p_sb, p_sb, do, carries, _sb_causal(T, NH).astype(F32), _sb_triangle(T, True), _sb_triangle(T, False),
      *hosted.operands)
    return (dq, dk, dv) if comm is None else (dq, dk, dv, got)


def _dil_blocks(b, body_fn):
    for pi, (window, dil) in enumerate(DILATED_PATTERNS):
        assert window // dil == DIL_BLOCK
        nblk = DIL_SUPER // (DIL_BLOCK * dil)
        assert (dil * nblk) % DIL_UNROLL == 0

        def group(g, _, pi=pi, dil=dil, nblk=nblk):
            for u in range(DIL_UNROLL):
                t = g * DIL_UNROLL + u
                n = t % nblk
                body_fn(pi, dil, t // nblk, n, b * nblk + n)
            return 0

        lax.fori_loop(0, dil * nblk // DIL_UNROLL, group, 0)


def _dil_rows(start, size, dil):
    if dil == 1:
        return pl.ds(pl.multiple_of(start, DIL_BLOCK), size)
    return pl.ds(start, size, stride=dil)


def _dil_fill_bias(bias_ref):
    row = lax.broadcasted_iota(jnp.int32, (2 * DIL_BLOCK, 2 * DIL_BLOCK), 0)
    kk = lax.broadcasted_iota(jnp.int32, (2 * DIL_BLOCK, 2 * DIL_BLOCK), 1)
    qi = jnp.where(row >= DIL_BLOCK, row - DIL_BLOCK, row)
    for s in range(2):
        dist = s * DIL_BLOCK + qi - kk
        bias_ref[s] = jnp.where((dist >= 0) & (dist <= DIL_BLOCK), 0.0, NEG_BIG)


def _dl_fwd(p_dl, *, name):
    S, W = p_dl.shape[0], p_dl.shape[1] // 3
    npair = W // LANES
    nsuper = S // DIL_SUPER
    assert S % DIL_SUPER == 0 and S // max(d for _, d in DILATED_PATTERNS) >= 2 * DIL_BLOCK
    scale = HEAD_DIM ** -0.5
    npat = len(DILATED_PATTERNS)

    def body(q_ref, k_ref, v_ref, o_ref, l_ref, bias_ref, *pattern_refs):
        op_refs, lp_refs = pattern_refs[:npat], pattern_refs[npat:]
        b = pl.program_id(1)
        masks = _head_masks((DIL_BLOCK, LANES))
        pl.when(b == 0)(lambda: _dil_fill_bias(bias_ref))

        def block(pi, dil, c, n, gn):
            ws = jnp.maximum(gn - 1, 0)
            qrows = n * (DIL_BLOCK * dil) + c
            krows = ws * (DIL_BLOCK * dil) + c
            q_idx = _dil_rows(qrows, DIL_BLOCK, dil)
            k_idx = _dil_rows(krows, 2 * DIL_BLOCK, dil)
            qb = q_ref[q_idx, :]
            kb = k_ref[k_idx, :].astype(BF16)
            vb = v_ref[k_idx, :].astype(BF16)
            q2 = _sb_stack_heads(qb.astype(BF16), scale)
            z = lax.dot_general(q2, kb, (((1,), (1,)), ((), ())), preferred_element_type=F32) + bias_ref[gn - ws]
            m = jnp.max(z, axis=1, keepdims=True)
            p = jnp.exp(z - m)
            den = jnp.sum(p, axis=1, keepdims=True)
            acc = jnp.dot(p.astype(BF16), vb, preferred_element_type=F32)
            lse = m + jnp.log(den)
            op_refs[pi][q_idx, :] = _sb_unstack_heads(acc / den)
            lp_refs[pi][q_idx, :] = jnp.where(masks[0], lse[:DIL_BLOCK], lse[DIL_BLOCK:])

        _dil_blocks(b, block)
        lses = [r[...] for r in lp_refs]
        top = functools.reduce(jnp.maximum, lses)
        ws_ = [jnp.exp(l - top) for l in lses]
        den = functools.reduce(jnp.add, ws_)
        num = functools.reduce(jnp.add, [w * r[...] for r, w in zip(op_refs, ws_)])
        o_ref[...] = num / den
        l_ref[...] = top + jnp.log(den)

    blk = pl.BlockSpec((DIL_SUPER, LANES), lambda p, b: (b, p))
    full = lambda off: pl.BlockSpec((S, LANES), lambda p, b: (0, off + p))
    return pl.pallas_call(
        body, name=name, grid=(npair, nsuper),
        out_shape=[jax.ShapeDtypeStruct((S, W), F32)] * 2,
        in_specs=[blk, full(npair), full(2 * npair)], out_specs=[blk, blk],
        scratch_shapes=[pltpu.VMEM((2, 2 * DIL_BLOCK, 2 * DIL_BLOCK), F32)]
        + [pltpu.VMEM((DIL_SUPER, LANES), F32)] * (2 * npat),
        compiler_params=_cparams(("arbitrary", "arbitrary")),
    )(p_dl, p_dl, p_dl)


def _dl_bwd(p_dl, o, lse, do, *, name):
    S, W = p_dl.shape[0], p_dl.shape[1] // 3
    npair = W // LANES
    nsuper = S // DIL_SUPER
    scale = HEAD_DIM ** -0.5

    def body(q_ref, k_ref, v_ref, o_ref, l_ref, do_ref, dq_ref, dk_ref, dv_ref, delta_ref, bias_ref):
        b = pl.program_id(1)

        @pl.when(b == 0)
        def _():
            dk_ref[...] = jnp.zeros_like(dk_ref)
            dv_ref[...] = jnp.zeros_like(dv_ref)
            _dil_fill_bias(bias_ref)

        dq_ref[...] = jnp.zeros_like(dq_ref)
        prod = do_ref[...] * o_ref[...]
        delta = jnp.zeros_like(prod)
        for hm in _head_masks(prod.shape):
            delta = jnp.where(hm, jnp.sum(jnp.where(hm, prod, 0.0), axis=1, keepdims=True), delta)
        delta_ref[...] = delta

        def block(pi, dil, c, n, gn):
            ws = jnp.maximum(gn - 1, 0)
            qrows = n * (DIL_BLOCK * dil) + c
            krows = ws * (DIL_BLOCK * dil) + c
            q_idx = _dil_rows(qrows, DIL_BLOCK, dil)
            k_idx = _dil_rows(krows, 2 * DIL_BLOCK, dil)
            qb = q_ref[q_idx, :]
            dob = do_ref[q_idx, :]
            lb = l_ref[q_idx, :]
            db = delta_ref[q_idx, :]
            kb = k_ref[k_idx, :].astype(BF16)
            vb = v_ref[k_idx, :].astype(BF16)
            q2 = _sb_stack_heads(qb.astype(BF16), scale)
            do2 = _sb_stack_heads(dob.astype(BF16))
            lse2 = jnp.concatenate([lb[:, HEAD_DIM * h:HEAD_DIM * h + 1] for h in range(2)], axis=0)
            delta2 = jnp.concatenate([db[:, HEAD_DIM * h:HEAD_DIM * h + 1] for h in range(2)], axis=0)
            z = lax.dot_general(q2, kb, (((1,), (1,)), ((), ())), preferred_element_type=F32)
            p = jnp.exp((z + bias_ref[gn - ws]) - lse2)
            dp = lax.dot_general(do2, vb, (((1,), (1,)), ((), ())), preferred_element_type=F32)
            dzb = (p * (dp - delta2)).astype(BF16)
            tn_dims = (((0,), (0,)), ((), ()))
            dq_blk = _sb_unstack_heads(jnp.dot(dzb, kb, preferred_element_type=F32)) * scale
            dk_blk = lax.dot_general(dzb, q2, tn_dims, preferred_element_type=F32)
            dv_blk = lax.dot_general(p.astype(BF16), do2, tn_dims, preferred_element_type=F32)
            dq_ref[q_idx, :] = dq_ref[q_idx, :] + dq_blk
            dk_ref[k_idx, :] = dk_ref[k_idx, :] + dk_blk
            dv_ref[k_idx, :] = dv_ref[k_idx, :] + dv_blk

        _dil_blocks(b, block)

    blk = pl.BlockSpec((DIL_SUPER, LANES), lambda p, b: (b, p))
    full = lambda off: pl.BlockSpec((S, LANES), lambda p, b: (0, off + p))
    return pl.pallas_call(
        body, name=name, grid=(npair, nsuper),
        out_shape=[jax.ShapeDtypeStruct((S, W), F32)] * 3,
        in_specs=[blk, full(npair), full(2 * npair), blk, blk, blk], out_specs=[blk, full(0), full(0)],
        scratch_shapes=[pltpu.VMEM((DIL_SUPER, LANES), F32), pltpu.VMEM((2, 2 * DIL_BLOCK, 2 * DIL_BLOCK), F32)],
        compiler_params=_cparams(("arbitrary", "arbitrary")),
    )(p_dl, p_dl, p_dl, o, lse, do)


class _NoExchange:
    def gather(self, family):
        return None

    def gathered(self, family, got, weights):
        pass

    def send(self, family, grads):
        return None

    def received(self, family, got):
        pass


def _local_step(x, target, gains, weights, exchanges=None):
    S, D = x.shape
    ex = exchanges or _NoExchange()
    weights = dict(weights)
    d_sb = gains["sb_out_norm"].shape[1]
    d_dl = gains["dil_out_norm"].shape[1]
    cos_t, sin_t = _rope_tables(S)

    riders = ("ffn1_w_up", "ffn1_w_down", "mixer") if exchanges else (None, None, None)
    (x1, h2), saved1 = _ffn_fwd(x, gains["ffn1_norm"], weights, tag="ffn1", ex=ex, riders=riders,
                                first_rider="ffn1_w_gate" if exchanges else None,
                                next_gain=gains["mix_norm"])
    w_in = weights["w_in"]
    w_out = weights["w_out"]
    p_sb = _mm(h2, w_in, tb=True, b_cols=(0, 3 * d_sb), outs=(BF16,), name="proj_sb")

    def rope_qk(acc, _, cos, sin):
        return jnp.concatenate([_rotate(acc[:, :2 * d_dl], cos, sin, 1.0), acc[:, 2 * d_dl:]], axis=1)

    p_dl = _mm(h2, w_in, tb=True, b_cols=(3 * d_sb, 3 * d_dl), lanes=(cos_t, sin_t), epilogue=rope_qk,
               name="proj_dl_rope")
    plan = ex.gather("ffn2" if exchanges else None)
    o_sb, carries, *got = _sb_fwd(p_sb, name="sb_fwd", comm=plan)
    ex.gathered("ffn2", got[0] if got else None, weights)
    o_dl, lse_dl = _dl_fwd(p_dl, name="dl_fwd")
    merged = _rms_fwd([o_sb, o_dl], [gains["sb_out_norm"], gains["dil_out_norm"]], name="out_norm")
    x2, h3 = _mm(merged, w_out, res=x1, rows=(gains["ffn2_norm"],), outs=(F32, BF16),
                 epilogue=_residual_then_norm(1.0), name="out_proj_norm")
    (dx3, d_final, loss_wide), saved2 = _ffn_fwd(x2, gains["ffn2_norm"], weights, tag="ffn2", ex=ex, h=h3,
                                                 head=(gains["final_norm"], target))
    loss_row = loss_wide[:, :LANES]

    dx2, d_ffn2_norm, dwg2, dwu2, dwd2 = _ffn_bwd(dx3, x2, gains["ffn2_norm"], weights, saved2, tag="ffn2", ex=ex)
    d_w_out = _mm(merged, dx2, ta=True, outs=(GRAD_WIRE,), name="d_w_out")
    do_sb, d_sb_norm = _mm(dx2, w_out, tb=True, b_cols=(0, d_sb), extras=(o_sb,), rows=(gains["sb_out_norm"],),
                           row_sums=1, epilogue=_rms_bwd_epilogue, name="d_merged_sb")
    do_dl, d_dl_norm = _mm(dx2, w_out, tb=True, b_cols=(d_sb, d_dl), extras=(o_dl,), rows=(gains["dil_out_norm"],),
                           row_sums=1, epilogue=_rms_bwd_epilogue, name="d_merged_dl")
    plan = ex.send("ffn2", dict(ffn2_w_gate=dwg2, ffn2_w_up=dwu2, ffn2_w_down=dwd2))
    dq_sb, dk_sb, dv_sb, *got = _sb_bwd(p_sb, do_sb, carries, name="sb_bwd", comm=plan)
    ex.received("ffn2", got[0] if got else None)
    dq_dl, dk_dl, dv_dl = _dl_bwd(p_dl, o_dl, lse_dl, do_dl, name="dl_bwd")
    d_proj = _join_d_proj([dq_sb, dk_sb, dv_sb, dq_dl, dk_dl, dv_dl], (3, 4), cos_t, sin_t, name="d_proj")
    d_w_in = _mm(h2, d_proj, ta=True, outs=(GRAD_WIRE,), name="d_w_in")
    dx1, d_mix_norm = _mm(d_proj, w_in, extras=(x1, dx2), rows=(gains["mix_norm"],), row_sums=1,
                          epilogue=_rms_bwd_epilogue, name="dh_mix_norm_bwd")
    dx, d_ffn1_norm, dwg1, dwu1, dwd1 = _ffn_bwd(
        dx1, x, gains["ffn1_norm"], weights, saved1, tag="ffn1", ex=ex,
        rider=("mixer", dict(w_in=d_w_in, w_out=d_w_out)), spread=True)
    gain_grads = dict(ffn1_norm=d_ffn1_norm, mix_norm=d_mix_norm, sb_out_norm=d_sb_norm, dil_out_norm=d_dl_norm,
                      ffn2_norm=d_ffn2_norm, final_norm=d_final)
    weight_grads = dict(ffn1_w_gate=dwg1, ffn1_w_up=dwu1, ffn1_w_down=dwd1, w_in=d_w_in, w_out=d_w_out,
                        ffn2_w_gate=dwg2, ffn2_w_up=dwu2, ffn2_w_down=dwd2)
    return loss_row, dx, gain_grads, weight_grads


def _mesh_position():
    return lax.axis_index("x"), lax.axis_index("y"), lax.axis_index("c")


def _flip(coord, bit):
    return 1 - coord if bit else coord


RELATIONS = [(rx, ry, rc) for rx in (0, 1) for ry in (0, 1) for rc in (0, 1)][1:]


class _GatherPlan:
    def __init__(self, shards):
        n = len(shards)
        self.operands = list(shards)
        self.out_shapes = [jax.ShapeDtypeStruct((N_DEV,) + s.shape, s.dtype) for s in shards]
        self.scratch = [pltpu.SemaphoreType.DMA((n, 7)), pltpu.SemaphoreType.DMA((n, 7)),
                        pltpu.SemaphoreType.DMA((n,))]

    def _copies(self, in_refs, out_refs, sems):
        send_sems, recv_sems, local_sems = sems
        x, y, c = _mesh_position()
        me, sibling = (x, y, c), (x, y, 1 - c)
        chips = [(1 - x, y), (x, 1 - y), (1 - x, 1 - y)]
        plans = []
        for t, (x_ref, out_ref) in enumerate(zip(in_refs, out_refs)):
            def slot(px, py, pc, out_ref=out_ref):
                return out_ref.at[4 * px + 2 * py + pc]

            def copy(k, block, to, src=None, t=t, slot=slot):
                return pltpu.make_async_remote_copy(
                    src_ref=slot(*block) if src is None else src, dst_ref=slot(*block),
                    send_sem=send_sems.at[t, k], recv_sem=recv_sems.at[t, k],
                    device_id=to, device_id_type=pl.DeviceIdType.MESH)

            plans.append(dict(
                mine=pltpu.make_async_copy(x_ref, slot(*me), local_sems.at[t]),
                first=[copy(0, me, sibling, src=x_ref)]
                + [copy(1 + j, me, (*chip, c), src=x_ref) for j, chip in enumerate(chips)],
                over_ici=[copy(1 + j, (*chip, c), me) for j, chip in enumerate(chips)],
                passed=[copy(4 + j, (*chip, c), sibling) for j, chip in enumerate(chips)],
                from_sibling=[copy(0, sibling, me)] + [copy(4 + j, (*chip, 1 - c), me) for j, chip in enumerate(chips)]))
        return plans

    def start(self, in_refs, out_refs, sems):
        for p in self._copies(in_refs, out_refs, sems):
            p["mine"].start()
            for cp in p["first"]:
                cp.start()

    def finish(self, in_refs, out_refs, sems):
        plans = self._copies(in_refs, out_refs, sems)
        for p in plans:
            for arrived, onward in zip(p["over_ici"], p["passed"]):
                arrived.wait_recv()
                onward.start()
        for p in plans:
            for cp in p["from_sibling"]:
                cp.wait_recv()
            for cp in p["first"] + p["passed"]:
                cp.wait_send()
            p["mine"].wait()


class _Hosted:
    def __init__(self, plan, n_in, n_out, n_scratch):
        self.plan, self.n_in, self.n_out, self.n_scratch = plan, n_in, n_out, n_scratch
        self.operands = list(plan.operands) if plan else []
        self.out_shapes = list(plan.out_shapes) if plan else []
        self.scratch = list(plan.scratch) if plan else []
        self.in_specs = [pl.BlockSpec(memory_space=pl.ANY)] * len(self.operands)
        self.out_specs = [pl.BlockSpec(memory_space=pl.ANY)] * len(self.out_shapes)

    def semantics(self, sem):
        return sem if self.plan is None else ("arbitrary",) * len(sem)

    def _at(self, grid, last):
        hit = None
        for d, n in enumerate(grid):
            here = pl.program_id(d) == (n - 1 if last else 0)
            hit = here if hit is None else hit & here
        return hit

    def begin(self, refs, grid):
        if self.plan is None:
            return refs
        k_in, k_out = len(self.operands), len(self.out_shapes)
        ins, rest = refs[:self.n_in], refs[self.n_in:]
        c_in, rest = rest[:k_in], rest[k_in:]
        outs, rest = rest[:self.n_out], rest[self.n_out:]
        c_out, rest = rest[:k_out], rest[k_out:]
        scratch, sems = rest[:self.n_scratch], rest[self.n_scratch:]
        self._args = (c_in, c_out, sems)
        pl.when(self._at(grid, False))(lambda: self.plan.start(*self._args))
        return tuple(ins) + tuple(outs) + tuple(scratch)

    def end(self, grid):
        if self.plan is not None:
            pl.when(self._at(grid, True))(lambda: self.plan.finish(*self._args))


class _ExchangePlan:
    def __init__(self, packs):
        n = len(packs)
        self.operands = list(packs)
        self.out_shapes = [jax.ShapeDtypeStruct(p.shape, p.dtype) for p in packs]
        self.scratch = [pltpu.SemaphoreType.DMA((n, 7)), pltpu.SemaphoreType.DMA((n, 7)),
                        pltpu.SemaphoreType.DMA((n,))]

    def _copies(self, in_refs, out_refs, sems):
        send_sems, recv_sems, local_sems = sems
        x, y, c = _mesh_position()
        me = 4 * x + 2 * y + c
        copies = [pltpu.make_async_copy(i.at[me], o.at[me], local_sems.at[t])
                  for t, (i, o) in enumerate(zip(in_refs, out_refs))]
        for r, (rx, ry, rc) in enumerate(RELATIONS):
            px, py, pc = _flip(x, rx), _flip(y, ry), _flip(c, rc)
            peer = 4 * px + 2 * py + pc
            copies += [pltpu.make_async_remote_copy(
                src_ref=i.at[peer], dst_ref=o.at[me], send_sem=send_sems.at[t, r], recv_sem=recv_sems.at[t, r],
                device_id=(px, py, pc), device_id_type=pl.DeviceIdType.MESH)
                for t, (i, o) in enumerate(zip(in_refs, out_refs))]
        return copies

    def start(self, in_refs, out_refs, sems):
        for cp in self._copies(in_refs, out_refs, sems):
            cp.start()

    def finish(self, in_refs, out_refs, sems):
        for cp in self._copies(in_refs, out_refs, sems):
            cp.wait()


def _all_reduce_rows(v, *, name):
    R, C = v.shape

    def body(v_ref, out_ref, buf, send_sems, recv_sems):
        x, y, c = _mesh_position()
        me = 4 * x + 2 * y + c
        buf[me] = v_ref[...]
        copies = []
        for r, (rx, ry, rc) in enumerate(RELATIONS):
            cp = pltpu.make_async_remote_copy(
                src_ref=v_ref, dst_ref=buf.at[me], send_sem=send_sems.at[r], recv_sem=recv_sems.at[r],
                device_id=(_flip(x, rx), _flip(y, ry), _flip(c, rc)), device_id_type=pl.DeviceIdType.MESH)
            cp.start()
            copies.append(cp)
        for cp in copies:
            cp.wait()
        total = buf[0]
        for s in range(1, N_DEV):
            total = total + buf[s]
        out_ref[...] = total

    return pl.pallas_call(
        body, name=name,
        out_shape=jax.ShapeDtypeStruct((R, C), F32),
        in_specs=[pl.BlockSpec(memory_space=pltpu.VMEM)],
        out_specs=pl.BlockSpec(memory_space=pltpu.VMEM),
        scratch_shapes=[pltpu.VMEM((N_DEV, R, C), F32), pltpu.SemaphoreType.DMA((7,)), pltpu.SemaphoreType.DMA((7,))],
    )(v)


def _adamw(w, g, m, v, *, name):
    R, C = w.shape
    slots = g.ndim == 3
    tr = _pick(R, (256, 128, 64, 32, 16) if slots else (256, 128, 64, 32, 16, 8))

    def body(w_ref, g_ref, m_ref, v_ref, g_out, d_ref, nm_ref, nv_ref):
        if slots:
            g = g_ref[0].astype(F32)
            for s in range(1, N_DEV):
                g = g + g_ref[s].astype(F32)
        else:
            g = g_ref[...]
        g_out[...] = g
        m_new = ADAM_B1 * m_ref[...] + (1.0 - ADAM_B1) * g
        v_new = ADAM_B2 * v_ref[...] + (1.0 - ADAM_B2) * (g * g)
        m_hat = m_new / (1.0 - ADAM_B1 ** ADAM_STEP)
        v_hat = v_new / (1.0 - ADAM_B2 ** ADAM_STEP)
        d_ref[...] = -ADAM_LR * (m_hat / (jnp.sqrt(v_hat) + ADAM_EPS) + ADAM_WD * w_ref[...])
        nm_ref[...] = m_new
        nv_ref[...] = v_new

    spec = pl.BlockSpec((tr, C), lambda i: (i, 0))
    g_spec = pl.BlockSpec((N_DEV, tr, C), lambda i: (0, i, 0)) if slots else spec
    return pl.pallas_call(
        body, name=name, grid=(R // tr,),
        out_shape=[jax.ShapeDtypeStruct((R, C), F32)] * 4,
        in_specs=[spec, g_spec, spec, spec], out_specs=[spec] * 4,
        compiler_params=_cparams(("parallel",)),
    )(w, g, m, v)


WEIGHT_NAMES = ["ffn1_norm", "ffn1_w_gate", "ffn1_w_up", "ffn1_w_down", "mix_norm", "w_in", "sb_out_norm",
                "dil_out_norm", "w_out", "ffn2_norm", "ffn2_w_gate", "ffn2_w_up", "ffn2_w_down", "final_norm"]
GAIN_NAMES = ["ffn1_norm", "mix_norm", "sb_out_norm", "dil_out_norm", "ffn2_norm", "final_norm"]
COL_SHARDED = ["ffn1_w_gate", "ffn1_w_up", "ffn2_w_gate", "ffn2_w_up", "w_in"]
ROW_SHARDED = ["ffn1_w_down", "ffn2_w_down", "w_out"]
GROUPS = {"mixer": (["w_in"], ["w_out"]),
          "ffn2": (["ffn2_w_gate", "ffn2_w_up"], ["ffn2_w_down"])}
for _ffn in ("ffn1", "ffn2"):
    GROUPS.update({f"{_ffn}_w_gate": ([f"{_ffn}_w_gate"], []), f"{_ffn}_w_up": ([f"{_ffn}_w_up"], []),
                   f"{_ffn}_w_down": ([], [f"{_ffn}_w_down"])})


class _Exchanges:
    def __init__(self, params):
        self.params = params
        self.grads = {}

    def gather(self, group):
        if group is None:
            return None
        cols, rows = GROUPS[group]
        return _GatherPlan([self.params[n].T.astype(BF16) for n in cols] + [self.params[n].astype(BF16) for n in rows])

    def gathered(self, group, got, weights):
        if group is None:
            return
        cols, rows = GROUPS[group]
        for n, blocks in zip(cols + rows, got):
            weights[n] = blocks.reshape(N_DEV * blocks.shape[1], blocks.shape[2])

    def send(self, group, grads):
        if group is None:
            return None
        cols, rows = GROUPS[group]
        packs = [jnp.transpose(grads[n].reshape(grads[n].shape[0], N_DEV, self.params[n].shape[1]), (1, 0, 2))
                 for n in cols]
        packs += [grads[n].reshape(N_DEV, self.params[n].shape[0], grads[n].shape[1]) for n in rows]
        return _ExchangePlan([p.astype(GRAD_WIRE) for p in packs])

    def received(self, group, got):
        if group is None:
            return
        cols, rows = GROUPS[group]
        for n, slots in zip(cols + rows, got):
            self.grads[n] = slots


def _step(x, target, params, moments_m, moments_v):
    ex = _Exchanges(params)
    weights = {}
    gains = {n: params[n] for n in GAIN_NAMES}
    loss_row, grad_x, gain_grads, _ = _local_step(x, target, gains, weights, ex)
    grads = ex.grads

    rows = [gain_grads[n].reshape(-1, LANES) for n in GAIN_NAMES] + [loss_row]
    small = jnp.concatenate(rows, axis=0)
    pad = (-small.shape[0]) % 8
    small = jnp.pad(small, ((0, pad), (0, 0)))
    small = _all_reduce_rows(small, name="reduce_gains_loss")
    off = 0
    for n in GAIN_NAMES:
        r = gain_grads[n].shape[1] // LANES
        grads[n] = small[off:off + r].reshape(1, -1)
        off += r
    loss = small[off, 0]

    delta, new_m, new_v = {}, {}, {}
    for n in WEIGHT_NAMES:
        grads[n], delta[n], new_m[n], new_v[n] = _adamw(params[n], grads[n], moments_m[n], moments_v[n],
                                                        name=f"adamw_{n}")
    return loss, grad_x, grads, delta, new_m, new_v


def kernel(x, ffn1_norm, ffn1_w_gate, ffn1_w_up, ffn1_w_down, mix_norm, w_in, sb_out_norm, dil_out_norm, w_out, ffn2_norm, ffn2_w_gate, ffn2_w_up, ffn2_w_down, final_norm, loss_target, m_ffn1_norm, m_ffn1_w_gate, m_ffn1_w_up, m_ffn1_w_down, m_mix_norm, m_w_in, m_sb_out_norm, m_dil_out_norm, m_w_out, m_ffn2_norm, m_ffn2_w_gate, m_ffn2_w_up, m_ffn2_w_down, m_final_norm, v_ffn1_norm, v_ffn1_w_gate, v_ffn1_w_up, v_ffn1_w_down, v_mix_norm, v_w_in, v_sb_out_norm, v_dil_out_norm, v_w_out, v_ffn2_norm, v_ffn2_w_gate, v_ffn2_w_up, v_ffn2_w_down, v_final_norm):
    given = dict(locals())
    shapes = {n: given[n].shape for n in WEIGHT_NAMES}

    def as2d(a):
        return a.reshape(1, -1) if a.ndim == 1 else a.reshape(a.shape[-2], a.shape[-1])

    params = {n: as2d(given[n]) for n in WEIGHT_NAMES}
    moments_m = {n: as2d(given["m_" + n]) for n in WEIGHT_NAMES}
    moments_v = {n: as2d(given["v_" + n]) for n in WEIGHT_NAMES}
    loss, grad_x, grads, delta, new_m, new_v = _step(x[0], loss_target[0], params, moments_m, moments_v)
    back = lambda d: [d[n].reshape(shapes[n]) for n in WEIGHT_NAMES]
    return (loss, grad_x[None], *back(grads), *back(delta), *back(new_m), *back(new_v))
```

```python
import functools

import jax
import jax.numpy as jnp
from jax import lax
from jax.experimental import pallas as pl
from jax.experimental.pallas import tpu as pltpu

F32 = jnp.float32
BF16 = jnp.bfloat16
GRAD_WIRE = jnp.bfloat16

N_DEV = 8
HEAD_DIM = 64
LANES = 128
DILATED_PATTERNS = ((128, 1), (512, 4), (2048, 16))
DIL_BLOCK = 128
DIL_SUPER = 2048
DIL_UNROLL = 16
SB_TILE = 256
SB_LANES = 128
SB_UNROLL = 4
SB_STEP_TILES = 2
SB_DEAD = 90.0
SB_UNSEEN = -1e30
ROPE_THETA = 10000.0
RMS_EPS = 1e-6
HALF_STEP = 0.5
ADAM_LR = 0.001
ADAM_B1 = 0.9
ADAM_B2 = 0.999
ADAM_EPS = 1e-08
ADAM_WD = 0.01
ADAM_STEP = 10
NEG_BIG = -1e30
VMEM_CAP_MB = 60


def _pick(n, prefs):
    for p in prefs:
        if n % p == 0:
            return p
    return n


MM_MAX_TILE = 1536
MM_WHOLE = 3072


def _largest_tile(n, cap):
    if n <= cap:
        return n
    for t in range(cap - cap % LANES, 0, -LANES):
        if n % t == 0:
            return t
    return n


def _cparams(sem=None, vmem_mb=48):
    return pltpu.CompilerParams(dimension_semantics=sem, vmem_limit_bytes=min(vmem_mb, VMEM_CAP_MB) * 1024 * 1024)


def _nbytes(shape, dtype):
    n = 1
    for s in shape:
        n *= s
    return n * jnp.dtype(dtype).itemsize


def _mm(a, b, *, name, ta=False, tb=False, outs=(F32,), res=None, alpha=1.0, extras=(), epilogue=None,
        tm=None, tn=None, tk=None, comm=None, rows=(), lanes=(), row_sums=0, b_cols=None, second=None,
        twin_b=None):
    if ta:
        K, M = a.shape
    else:
        M, K = a.shape
    if tb:
        N, Kb = b.shape
    else:
        Kb, N = b.shape
    col0 = 0
    if b_cols is not None:
        col0, N = b_cols
    assert K == Kb, (a.shape, b.shape, ta, tb)
    tn = tn or (N if (not ta and K <= MM_WHOLE and N <= MM_WHOLE) else _largest_tile(N, MM_MAX_TILE))
    wide = tn > MM_MAX_TILE and (len(extras) + len(outs) > 3 or a.dtype == F32)
    tm = tm or (_largest_tile(M, MM_MAX_TILE) if ta else _pick(M, (256, 128) if wide else (512, 256, 128)))
    tk = tk or (K if K <= MM_WHOLE else _pick(K, (2048, 1024, 512, 256, 128)))
    nk = K // tk
    a_spec = pl.BlockSpec((tk, tm), lambda i, j, k: (k, i)) if ta else pl.BlockSpec((tm, tk), lambda i, j, k: (i, k))
    assert col0 % tn == 0
    b_spec = (pl.BlockSpec((tn, tk), lambda i, j, k: (j + col0 // tn, k)) if tb
              else pl.BlockSpec((tk, tn), lambda i, j, k: (k, j + col0 // tn)))
    mn_spec = pl.BlockSpec((tm, tn), lambda i, j, k: (i, j))
    dims = (((0 if ta else 1,), (1 if tb else 0,)), ((), ()))
    row_spec = pl.BlockSpec((1, tn), lambda i, j, k: (0, j))
    lane_spec = pl.BlockSpec((tm, LANES), lambda i, j, k: (i, 0))
    n_extra = len(extras) + (1 if res is not None else 0) + len(rows) + len(lanes)
    n_mn = len(outs)
    n_out = n_mn + row_sums
    assert row_sums == 0 or tn == N
    grid = (M // tm, N // tn, nk)
    n_ab = 2 + (2 if second is not None else 0) + (1 if twin_b is not None else 0)
    assert twin_b is None or (nk == 1 and epilogue is not None)
    hosted = _Hosted(comm, n_in=n_ab + n_extra, n_out=n_out, n_scratch=1 if nk > 1 else 0)

    def body(*refs):
        in_refs = refs[n_ab:n_ab + n_extra]
        ab_refs = refs[:n_ab]
        refs = hosted.begin(refs, grid)
        out_refs = refs[n_ab + n_extra:n_ab + n_extra + n_out]
        def product(a_ref, b_ref):
            return lax.dot_general(a_ref[...].astype(BF16), b_ref[...].astype(BF16), dims, preferred_element_type=F32)

        prod = product(ab_refs[0], ab_refs[1])
        if second is not None:
            prod = prod + product(ab_refs[2], ab_refs[3])
        if twin_b is not None:
            prod = (prod, product(ab_refs[0], ab_refs[-1]))

        def finish(acc):
            blocks = [r[...] for r in in_refs]
            if res is not None:
                r_blk, blocks = blocks[0], blocks[1:]
            else:
                r_blk = None
            if epilogue is None:
                val = acc * alpha
                if r_blk is not None:
                    val = val + r_blk
                vals = (val,)
            else:
                vals = epilogue(acc, r_blk, *blocks)
                vals = vals if isinstance(vals, (tuple, list)) else (vals,)
            for o_ref, v in zip(out_refs[:n_mn], vals[:n_mn]):
                o_ref[...] = v.astype(o_ref.dtype)
            first_rows = pl.program_id(0) == 0
            for o_ref, part in zip(out_refs[n_mn:], vals[n_mn:]):
                @pl.when(first_rows)
                def _(o_ref=o_ref, part=part):
                    o_ref[...] = part

                @pl.when(jnp.logical_not(first_rows))
                def _(o_ref=o_ref, part=part):
                    o_ref[...] += part

        if nk == 1:
            finish(prod)
        else:
            acc_ref = refs[n_ab + n_extra + n_out]
            k = pl.program_id(2)

            @pl.when(k == 0)
            def _():
                acc_ref[...] = prod

            @pl.when(k > 0)
            def _():
                acc_ref[...] += prod

            @pl.when(k == nk - 1)
            def _():
                finish(acc_ref[...])

        hosted.end(grid)

    mn_operands = ([res] if res is not None else []) + list(extras)
    ab = [a, b] + (list(second) if second is not None else []) + ([twin_b] if twin_b is not None else [])
    ab_specs = [a_spec, b_spec] * (1 if second is None else 2) + ([b_spec] if twin_b is not None else [])
    operands = ab + mn_operands + list(rows) + list(lanes)
    in_specs = ab_specs + [mn_spec] * len(mn_operands) + [row_spec] * len(rows) + [lane_spec] * len(lanes)
    est = n_ab * (_nbytes((tm, tk), a.dtype) + _nbytes((tk, tn), b.dtype))
    est += 2 * sum(_nbytes((tm, tn), o.dtype) for o in mn_operands)
    est += 2 * sum(_nbytes((tm, tn), d) for d in outs) + 2 * _nbytes((tm, tn), F32)
    semantics = ("parallel", "parallel", "arbitrary") if row_sums == 0 else ("arbitrary",) * 3
    result = pl.pallas_call(
        body, name=name, grid=grid,
        out_shape=[jax.ShapeDtypeStruct((M, N), d) for d in outs]
        + [jax.ShapeDtypeStruct((1, N), F32)] * row_sums + hosted.out_shapes,
        in_specs=in_specs + hosted.in_specs,
        out_specs=[mn_spec] * n_mn + [row_spec] * row_sums + hosted.out_specs,
        scratch_shapes=([pltpu.VMEM((tm, tn), F32)] if nk > 1 else []) + hosted.scratch,
        compiler_params=_cparams(hosted.semantics(semantics), vmem_mb=max(32, 2 * est // (1024 * 1024))),
    )(*operands, *hosted.operands)
    own, got = result[:n_out], list(result[n_out:])
    own = own[0] if n_out == 1 else own
    return own if comm is None else (own, got)


def _rms_hat(x):
    r = lax.rsqrt(jnp.mean(x * x, axis=-1, keepdims=True) + RMS_EPS)
    return x * r, r


def _rms_fwd(xs, gains, *, name, comm=None):
    S = xs[0].shape[0]
    widths = [x.shape[1] for x in xs]
    tm = _pick(S, (512, 256, 128))
    n = len(xs)
    grid = (S // tm,)
    hosted = _Hosted(comm, n_in=2 * n, n_out=1, n_scratch=0)

    def body(*refs):
        refs = hosted.begin(refs, grid)
        o_ref = refs[2 * n]
        off = 0
        for i in range(n):
            xh, _ = _rms_hat(refs[i][...])
            o_ref[:, off:off + widths[i]] = (xh * refs[n + i][...]).astype(o_ref.dtype)
            off += widths[i]
        hosted.end(grid)

    out, *got = pl.pallas_call(
        body, name=name, grid=grid,
        out_shape=[jax.ShapeDtypeStruct((S, sum(widths)), BF16)] + hosted.out_shapes,
        in_specs=[pl.BlockSpec((tm, w), lambda i: (i, 0)) for w in widths]
        + [pl.BlockSpec((1, w), lambda i: (0, 0)) for w in widths] + hosted.in_specs,
        out_specs=[pl.BlockSpec((tm, sum(widths)), lambda i: (i, 0))] + hosted.out_specs,
        scratch_shapes=hosted.scratch,
        compiler_params=_cparams(hosted.semantics(("parallel",))),
    )(*xs, *gains, *hosted.operands)
    return out if comm is None else (out, got)


def _sigmoid(g):
    return 1.0 / (1.0 + jnp.exp(-g))


def _ride(result, plan):
    return result if plan is not None else (result, None)


def _residual_then_norm(alpha):
    def epilogue(acc, res, gain):
        y = res + alpha * acc
        return y, _rms_hat(y)[0] * gain
    return epilogue


def _ffn_fwd(x, gain, w, *, tag, ex, first_rider=None, riders=(None, None, None), head=None, h=None,
             next_gain=None):
    if h is None:
        plan = ex.gather(first_rider)
        h, got = _ride(_rms_fwd([x], [gain], name=f"{tag}_norm", comm=plan), plan)
        ex.gathered(first_rider, got, w)
    def act(acc, _, g_blk):
        gf = g_blk.astype(F32)
        return acc, gf * _sigmoid(gf) * acc

    if riders[0] is None and riders[1] is None and f"{tag}_w_up" in w:
        def gate_up_act(accs, _):
            g_blk = accs[0].astype(BF16)
            return (g_blk,) + act(accs[1], None, g_blk)

        g, u, a = _mm(h, w[f"{tag}_w_gate"], tb=True, twin_b=w[f"{tag}_w_up"], outs=(BF16, BF16, BF16),
                      epilogue=gate_up_act, tm=256, name=f"{tag}_gate_up_act")
    else:
        plan = ex.gather(riders[0])
        g, got = _ride(_mm(h, w[f"{tag}_w_gate"], tb=True, outs=(BF16,), name=f"{tag}_gate", comm=plan), plan)
        ex.gathered(riders[0], got, w)
        plan = ex.gather(riders[1])
        (u, a), got = _ride(_mm(h, w[f"{tag}_w_up"], tb=True, outs=(BF16, BF16), extras=(g,), epilogue=act,
                                name=f"{tag}_up_act", comm=plan), plan)
        ex.gathered(riders[1], got, w)
    plan = ex.gather(riders[2])
    if head is None and next_gain is None:
        y, got = _ride(_mm(a, w[f"{tag}_w_down"], res=x, alpha=HALF_STEP, name=f"{tag}_down", comm=plan), plan)
    elif head is None:
        y, got = _ride(_mm(a, w[f"{tag}_w_down"], res=x, rows=(next_gain,), outs=(F32, BF16),
                           epilogue=_residual_then_norm(HALF_STEP), name=f"{tag}_down_norm", comm=plan), plan)
    else:
        final_gain, target = head
        y, got = _ride(_mm(a, w[f"{tag}_w_down"], res=x, extras=(target,), rows=(final_gain,), row_sums=2,
                           epilogue=_loss_head_epilogue, name=f"{tag}_down_loss", comm=plan), plan)
    ex.gathered(riders[2], got, w)
    return y, (h, g, u, a)


def _loss_head_epilogue(acc, x_in, target, gain):
    xh, r = _rms_hat(x_in + HALF_STEP * acc)
    err = xh * gain - target
    dy = err * (1.0 / acc.shape[1])
    dxh = dy * gain
    dx = r * (dxh - xh * jnp.mean(dxh * xh, axis=-1, keepdims=True))
    loss = 0.5 * jnp.sum(jnp.mean(err * err, axis=-1, keepdims=True), axis=0, keepdims=True)
    return dx, jnp.sum(dy * xh, axis=0, keepdims=True), jnp.zeros_like(gain) + loss


def _rms_bwd_epilogue(acc, dh_so_far, x, *dres_and_gain):
    gain = dres_and_gain[-1]
    dh = acc if dh_so_far is None else acc + dh_so_far
    xh, r = _rms_hat(x)
    dxh = dh * gain
    dx = r * (dxh - xh * jnp.mean(dxh * xh, axis=-1, keepdims=True))
    if len(dres_and_gain) == 2:
        dx = dx + dres_and_gain[0]
    return dx, jnp.sum(dh * xh, axis=0, keepdims=True)


def _ffn_bwd(dout, x, gain, w, saved, *, tag, ex, rider=(None, None), spread=False):
    h, g, u, a = saved
    wg, wu, wd = (w[f"{tag}_w_{n}"] for n in ("gate", "up", "down"))

    def act_bwd(acc, _, g_blk, u_blk):
        gf, uf = g_blk.astype(F32), u_blk.astype(F32)
        da = acc * HALF_STEP
        sig = _sigmoid(gf)
        silu = gf * sig
        return da * uf * (sig + silu * (1.0 - sig)), da * silu

    def carrying(group, grad, call):
        group = group if spread else None
        plan = ex.send(group, {group: grad})
        out, got = _ride(call(plan), plan)
        ex.received(group, got)
        return out

    plan = ex.send(*rider)
    (dg, du), got = _ride(_mm(dout, wd, tb=True, outs=(BF16, BF16), extras=(g, u), epilogue=act_bwd,
                              name=f"{tag}_bwd_act", comm=plan), plan)
    ex.received(rider[0], got)
    dwg = _mm(h, dg, ta=True, outs=(GRAD_WIRE,), name=f"{tag}_dwg")
    dwu = carrying(f"{tag}_w_gate", dwg, lambda plan: _mm(h, du, ta=True, outs=(GRAD_WIRE,), name=f"{tag}_dwu", comm=plan))
    dwd = carrying(f"{tag}_w_up", dwu,
                   lambda plan: _mm(a, dout, ta=True, outs=(GRAD_WIRE,), alpha=HALF_STEP, name=f"{tag}_dwd", comm=plan))
    dx, dgain = carrying(f"{tag}_w_down", dwd, lambda plan: _mm(
        dg, wg, second=(du, wu), extras=(x, dout), rows=(gain,), row_sums=1, epilogue=_rms_bwd_epilogue,
        tm=512, name=f"{tag}_dh_norm_bwd", comm=plan))
    return dx, dgain, dwg, dwu, dwd


def _rope_tables(S):
    half = HEAD_DIM // 2
    inv_freq = ROPE_THETA ** (-jnp.arange(half, dtype=F32) / half)
    ang = jnp.arange(S, dtype=F32)[:, None] * inv_freq[None, :]
    cos, sin = jnp.cos(ang), jnp.sin(ang)
    reps = LANES // HEAD_DIM
    cos_t = jnp.tile(jnp.concatenate([cos, cos], axis=1), (1, reps))
    sin_t = jnp.tile(jnp.concatenate([-sin, sin], axis=1), (1, reps))
    return cos_t, sin_t


def _rotate(v, cos, sin, sign):
    half = HEAD_DIM // 2
    groups = []
    for g in range(v.shape[1] // LANES):
        t = v[:, g * LANES:(g + 1) * LANES]
        lane = lax.broadcasted_iota(jnp.int32, t.shape, 1)
        swapped = jnp.where(lane % HEAD_DIM < half, pltpu.roll(t, LANES - half, axis=1), pltpu.roll(t, half, axis=1))
        groups.append(t * cos + swapped * (sin * sign))
    return groups[0] if len(groups) == 1 else jnp.concatenate(groups, axis=1)


def _join_d_proj(pieces, rotated, cos_t, sin_t, *, name):
    S = pieces[0].shape[0]
    widths = [p.shape[1] for p in pieces]
    tm = _pick(S, (256, 128))
    n = len(pieces)

    def body(*refs):
        c_ref, s_ref, o_ref = refs[n], refs[n + 1], refs[n + 2]
        off = 0
        for i in range(n):
            v = refs[i][...]
            if i in rotated:
                v = _rotate(v, c_ref[...], s_ref[...], -1.0)
            o_ref[:, off:off + widths[i]] = v.astype(o_ref.dtype)
            off += widths[i]

    return pl.pallas_call(
        body, name=name, grid=(S // tm,),
        out_shape=jax.ShapeDtypeStruct((S, sum(widths)), BF16),
        in_specs=[pl.BlockSpec((tm, w), lambda i: (i, 0)) for w in widths]
        + [pl.BlockSpec((tm, LANES), lambda i: (i, 0))] * 2,
        out_specs=pl.BlockSpec((tm, sum(widths)), lambda i: (i, 0)),
        compiler_params=_cparams(("parallel",)),
    )(*pieces, cos_t, sin_t)


def _head_masks(shape):
    lane = lax.broadcasted_iota(jnp.int32, shape, 1)
    return [(lane >= HEAD_DIM * h) & (lane < HEAD_DIM * (h + 1)) for h in range(shape[1] // HEAD_DIM)]


def _sb_scores(q2, k_j):
    z = lax.dot_general(q2, k_j, (((1,), (1,)), ((), ())), preferred_element_type=F32)
    sign_bit = jnp.int32(-2 ** 31)
    minus_abs = lax.bitcast_convert_type(lax.bitcast_convert_type(z, jnp.int32) | sign_bit, F32)
    softplus = jnp.maximum(z, 0.0) + jnp.log(1.0 + jnp.exp(minus_abs))
    return z - softplus, softplus


def _sb_stack_heads(t, scale=None):
    parts = [jnp.where(hm, t, jnp.zeros_like(t)) for hm in _head_masks(t.shape)]
    t2 = jnp.concatenate(parts, axis=0)
    if scale is not None:
        t2 = (t2.astype(F32) * scale).astype(t2.dtype)
    return t2


def _sb_unstack_heads(t2):
    n = t2.shape[1] // HEAD_DIM
    T = t2.shape[0] // n
    masks = _head_masks((T, t2.shape[1]))
    out = t2[:T]
    for h in range(1, n):
        out = jnp.where(masks[h], t2[h * T:(h + 1) * T], out)
    return out


def _sb_causal(T, n_heads):
    row = lax.broadcasted_iota(jnp.int32, (n_heads * T, T), 0)
    col = lax.broadcasted_iota(jnp.int32, (n_heads * T, T), 1)
    return col < row % T


def _sb_triangle(T, later):
    row = lax.broadcasted_iota(jnp.int32, (T, T), 0)
    col = lax.broadcasted_iota(jnp.int32, (T, T), 1)
    return ((row > col) if later else (row < col)).astype(BF16)


def _sb_fwd(p_sb, *, name, comm=None):
    S = p_sb.shape[0]
    W = p_sb.shape[1] // 3
    LW = min(SB_LANES, W)
    NH = LW // HEAD_DIM
    npair = W // LW
    T = SB_TILE
    n_tiles = S // T
    assert n_tiles <= HEAD_DIM
    scale = HEAD_DIM ** -0.5

    R = SB_STEP_TILES
    grid = (npair, n_tiles // R)
    hosted = _Hosted(comm, n_in=5, n_out=2, n_scratch=0)

    def body(*refs):
        refs = hosted.begin(refs, grid)
        step = pl.program_id(1)
        lax.fori_loop(0, R, lambda sub, _: query_tile(step * R + sub, sub, *refs), 0)
        hosted.end(grid)

    def query_tile(I, sub, q_ref, k_ref, v_ref, causal_ref, later_ref, o_ref, c_ref):
        rows = pl.ds(pl.multiple_of(sub * T, T), T)
        lane = lax.broadcasted_iota(jnp.int32, (T, LW), 1)
        causal = causal_ref[...]
        later_than = later_ref[...]
        q2 = _sb_stack_heads(q_ref[rows, :], scale)

        def scores(J, diag):
            off = pl.multiple_of(J * T, T)
            log_beta, stay = _sb_scores(q2, k_ref[pl.ds(off, T), :])
            if diag:
                stay = stay * causal
            local = jnp.dot(stay.astype(BF16), later_than, preferred_element_type=F32)
            return log_beta, local, jnp.sum(stay, axis=1, keepdims=True), v_ref[pl.ds(off, T), :]

        def weigh(J, sc, gone, acc, carr, diag):
            log_beta, local, _, v_j = sc
            w = jnp.exp((log_beta - gone) - local)
            if diag:
                w = w * causal
            acc = acc + jnp.dot(w.astype(BF16), v_j, preferred_element_type=F32)
            for h in range(NH):
                carr = jnp.where(lane == HEAD_DIM * h + J, -gone[h * T:(h + 1) * T], carr)
            return acc, carr

        def tiles(J, count, state, diag):
            gone, acc, carr, _ = state
            scs = [scores(J - u, diag and u == 0) for u in range(count)]
            for u, sc in enumerate(scs):
                acc, carr = weigh(J - u, sc, gone, acc, carr, diag and u == 0)
                gone = gone + sc[2]
            return gone, acc, carr, jnp.min(gone)

        U = SB_UNROLL
        alive = lambda st: st[3] < SB_DEAD
        state = (jnp.zeros((NH * T, 1), F32), jnp.zeros((NH * T, LW), F32),
                 jnp.full((T, LW), SB_UNSEEN, F32), jnp.zeros((), F32))
        state = lax.cond(I > 0, lambda st: tiles(I, 2, st, True), lambda st: tiles(I, 1, st, True), state)
        rest = jnp.maximum(I - 1, 0)
        singles = jnp.where(rest > 0, (rest - 1) % U + 1, 0)
        _, state = lax.while_loop(lambda c: (c[0] < singles) & alive(c[1]),
                                  lambda c: (c[0] + 1, tiles(I - 2 - c[0], 1, c[1], False)), (jnp.int32(0), state))
        blocks = (rest - singles) // U
        _, state = lax.while_loop(lambda c: (c[0] < blocks) & alive(c[1]),
                                  lambda c: (c[0] + 1, tiles(I - 2 - singles - U * c[0], U, c[1], False)),
                                  (jnp.int32(0), state))
        _, acc, carr, _ = state
        o_ref[rows, :] = _sb_unstack_heads(acc)
        c_ref[rows, :] = carr
        return 0

    blk = lambda I_off: pl.BlockSpec((R * T, LW), lambda p, I: (I, I_off + p))
    full = lambda off: pl.BlockSpec((S, LW), lambda p, I: (0, off + p))
    const = lambda rows: pl.BlockSpec((rows, T), lambda p, I: (0, 0))
    o, carries, *got = pl.pallas_call(
        body, name=name, grid=grid,
        out_shape=[jax.ShapeDtypeStruct((S, W), F32), jax.ShapeDtypeStruct((S, W), F32)] + hosted.out_shapes,
        in_specs=[blk(0), full(npair), full(2 * npair), const(NH * T), const(T)] + hosted.in_specs,
        out_specs=[blk(0), blk(0)] + hosted.out_specs,
        scratch_shapes=hosted.scratch,
        compiler_params=_cparams(hosted.semantics(("parallel", "arbitrary")), vmem_mb=56),
    )(p_sb, p_sb, p_sb, _sb_causal(T, NH).astype(F32), _sb_triangle(T, True), *hosted.operands)
    return (o, carries) if comm is None else (o, carries, got)


def _sb_bwd(p_sb, do, carries, *, name, comm=None):
    S = p_sb.shape[0]
    W = p_sb.shape[1] // 3
    LW = min(LANES, W)
    NH = LW // HEAD_DIM
    npair = W // LW
    T = SB_TILE
    n_tiles = S // T
    scale = HEAD_DIM ** -0.5

    R = SB_STEP_TILES
    grid = (npair, n_tiles // R)
    hosted = _Hosted(comm, n_in=8, n_out=3, n_scratch=0)

    def body(*refs):
        refs = hosted.begin(refs, grid)
        dk_ref, dv_ref = refs[9], refs[10]
        step = pl.program_id(1)

        @pl.when(step == 0)
        def _():
            dk_ref[...] = jnp.zeros_like(dk_ref)
            dv_ref[...] = jnp.zeros_like(dv_ref)

        lax.fori_loop(0, R, lambda sub, _: query_tile(step * R + sub, sub, *refs), 0)
        hosted.end(grid)

    def query_tile(I, sub, q_ref, k_ref, v_ref, do_ref, c_ref, causal_ref, later_ref, earlier_ref,
                   dq_ref, dk_ref, dv_ref):
        rows = pl.ds(pl.multiple_of(sub * T, T), T)
        lane = lax.broadcasted_iota(jnp.int32, (T, LW), 1)
        causal = causal_ref[...]
        later_than = later_ref[...]
        earlier_than = earlier_ref[...]
        q2 = _sb_stack_heads(q_ref[rows, :], scale)
        do2 = _sb_stack_heads(do_ref[rows, :].astype(BF16))
        carr = c_ref[rows, :]
        tn_dims = (((0,), (0,)), ((), ()))

        def chain(J, diag):
            off = pl.multiple_of(J * T, T)
            k_j = k_ref[pl.ds(off, T), :]
            v_j = v_ref[pl.ds(off, T), :]
            log_beta, stay = _sb_scores(q2, k_j)
            if diag:
                stay = stay * causal
            lc = jnp.concatenate(
                [jnp.sum(jnp.where(lane == HEAD_DIM * h + J, carr, 0.0), axis=1, keepdims=True) for h in range(NH)],
                axis=0)
            w = jnp.exp((log_beta + lc) - jnp.dot(stay.astype(BF16), later_than, preferred_element_type=F32))
            if diag:
                w = w * causal
            dw = lax.dot_general(do2, v_j, (((1,), (1,)), ((), ())), preferred_element_type=F32)
            e = w * dw
            local = jnp.dot(e.astype(BF16), earlier_than, preferred_element_type=F32)
            return off, k_j, w, e, local, jnp.exp(log_beta), jnp.sum(e, axis=1, keepdims=True)

        def finish(ch, ec, dq_acc, diag):
            off, k_j, w, e, local, beta, _ = ch
            e_before = local + ec
            dz = e - beta * (e + e_before)
            if diag:
                dz = dz * causal
            dzb = dz.astype(BF16)
            dq_acc = dq_acc + jnp.dot(dzb, k_j, preferred_element_type=F32)
            dk_ref[pl.ds(off, T), :] += lax.dot_general(dzb, q2, tn_dims, preferred_element_type=F32)
            dv_ref[pl.ds(off, T), :] += lax.dot_general(w.astype(BF16), do2, tn_dims, preferred_element_type=F32)
            return dq_acc

        def tiles(J, count, state, diag):
            ec, dq_acc = state
            chains = [chain(J + u, diag and u == count - 1) for u in range(count)]
            for u, ch in enumerate(chains):
                dq_acc = finish(ch, ec, dq_acc, diag and u == count - 1)
                ec = ec + ch[6]
            return ec, dq_acc

        lane_row = lax.broadcasted_iota(jnp.int32, (1, LW), 1)
        reached = (jnp.max(carr, axis=0, keepdims=True) > 0.5 * SB_UNSEEN) & (lane_row < HEAD_DIM)
        first = jnp.min(jnp.where(reached, lane_row.astype(F32), float(n_tiles))).astype(jnp.int32)
        U = SB_UNROLL
        count = I - first
        rest = jnp.maximum(count - 1, 0)
        state = (jnp.zeros((NH * T, 1), F32), jnp.zeros((NH * T, LW), F32))
        state = lax.fori_loop(0, rest // U, lambda jj, st: tiles(first + U * jj, U, st, False), state)
        state = lax.fori_loop(0, rest % U, lambda r, st: tiles(I - 1 - rest % U + r, 1, st, False), state)
        _, dq_acc = lax.cond(count > 0, lambda st: tiles(I - 1, 2, st, True), lambda st: tiles(I, 1, st, True), state)
        dq_ref[rows, :] = _sb_unstack_heads(dq_acc) * scale
        return 0

    blk = lambda src_off: pl.BlockSpec((R * T, LW), lambda p, I: (I, src_off + p))
    full = lambda off: pl.BlockSpec((S, LW), lambda p, I: (0, off + p))
    const = lambda rows: pl.BlockSpec((rows, T), lambda p, I: (0, 0))
    dq, dk, dv, *got = pl.pallas_call(
        body, name=name, grid=grid,
        out_shape=[jax.ShapeDtypeStruct((S, W), F32)] * 3 + hosted.out_shapes,
        in_specs=[blk(0), full(npair), full(2 * npair), blk(0), blk(0), const(NH * T), const(T), const(T)]
        + hosted.in_specs,
        out_specs=[blk(0), full(0), full(0)] + hosted.out_specs,
        scratch_shapes=hosted.scratch,
        compiler_params=_cparams(hosted.semantics(("parallel", "arbitrary")), vmem_mb=56),
    )(p_sb, p_sb, p_sb, do, carries, _sb_causal(T, NH).astype(F32), _sb_triangle(T, True), _sb_triangle(T, False),
      *hosted.operands)
    return (dq, dk, dv) if comm is None else (dq, dk, dv, got)


def _dil_blocks(b, body_fn):
    for pi, (window, dil) in enumerate(DILATED_PATTERNS):
        assert window // dil == DIL_BLOCK
        nblk = DIL_SUPER // (DIL_BLOCK * dil)
        assert (dil * nblk) % DIL_UNROLL == 0

        def group(g, _, pi=pi, dil=dil, nblk=nblk):
            for u in range(DIL_UNROLL):
                t = g * DIL_UNROLL + u
                n = t % nblk
                body_fn(pi, dil, t // nblk, n, b * nblk + n)
            return 0

        lax.fori_loop(0, dil * nblk // DIL_UNROLL, group, 0)


def _dil_rows(start, size, dil):
    if dil == 1:
        return pl.ds(pl.multiple_of(start, DIL_BLOCK), size)
    return pl.ds(start, size, stride=dil)


def _dil_fill_bias(bias_ref):
    row = lax.broadcasted_iota(jnp.int32, (2 * DIL_BLOCK, 2 * DIL_BLOCK), 0)
    kk = lax.broadcasted_iota(jnp.int32, (2 * DIL_BLOCK, 2 * DIL_BLOCK), 1)
    qi = jnp.where(row >= DIL_BLOCK, row - DIL_BLOCK, row)
    for s in range(2):
        dist = s * DIL_BLOCK + qi - kk
        bias_ref[s] = jnp.where((dist >= 0) & (dist <= DIL_BLOCK), 0.0, NEG_BIG)


def _dl_fwd(p_dl, *, name):
    S, W = p_dl.shape[0], p_dl.shape[1] // 3
    npair = W // LANES
    nsuper = S // DIL_SUPER
    assert S % DIL_SUPER == 0 and S // max(d for _, d in DILATED_PATTERNS) >= 2 * DIL_BLOCK
    scale = HEAD_DIM ** -0.5
    npat = len(DILATED_PATTERNS)

    def body(q_ref, k_ref, v_ref, o_ref, l_ref, bias_ref, *pattern_refs):
        op_refs, lp_refs = pattern_refs[:npat], pattern_refs[npat:]
        b = pl.program_id(1)
        masks = _head_masks((DIL_BLOCK, LANES))
        pl.when(b == 0)(lambda: _dil_fill_bias(bias_ref))

        def block(pi, dil, c, n, gn):
            ws = jnp.maximum(gn - 1, 0)
            qrows = n * (DIL_BLOCK * dil) + c
            krows = ws * (DIL_BLOCK * dil) + c
            q_idx = _dil_rows(qrows, DIL_BLOCK, dil)
            k_idx = _dil_rows(krows, 2 * DIL_BLOCK, dil)
            qb = q_ref[q_idx, :]
            kb = k_ref[k_idx, :].astype(BF16)
            vb = v_ref[k_idx, :].astype(BF16)
            q2 = _sb_stack_heads(qb.astype(BF16), scale)
            z = lax.dot_general(q2, kb, (((1,), (1,)), ((), ())), preferred_element_type=F32) + bias_ref[gn - ws]
            m = jnp.max(z, axis=1, keepdims=True)
            p = jnp.exp(z - m)
            den = jnp.sum(p, axis=1, keepdims=True)
            acc = jnp.dot(p.astype(BF16), vb, preferred_element_type=F32)
            lse = m + jnp.log(den)
            op_refs[pi][q_idx, :] = _sb_unstack_heads(acc / den)
            lp_refs[pi][q_idx, :] = jnp.where(masks[0], lse[:DIL_BLOCK], lse[DIL_BLOCK:])

        _dil_blocks(b, block)
        lses = [r[...] for r in lp_refs]
        top = functools.reduce(jnp.maximum, lses)
        ws_ = [jnp.exp(l - top) for l in lses]
        den = functools.reduce(jnp.add, ws_)
        num = functools.reduce(jnp.add, [w * r[...] for r, w in zip(op_refs, ws_)])
        o_ref[...] = num / den
        l_ref[...] = top + jnp.log(den)

    blk = pl.BlockSpec((DIL_SUPER, LANES), lambda p, b: (b, p))
    full = lambda off: pl.BlockSpec((S, LANES), lambda p, b: (0, off + p))
    return pl.pallas_call(
        body, name=name, grid=(npair, nsuper),
        out_shape=[jax.ShapeDtypeStruct((S, W), F32)] * 2,
        in_specs=[blk, full(npair), full(2 * npair)], out_specs=[blk, blk],
        scratch_shapes=[pltpu.VMEM((2, 2 * DIL_BLOCK, 2 * DIL_BLOCK), F32)]
        + [pltpu.VMEM((DIL_SUPER, LANES), F32)] * (2 * npat),
        compiler_params=_cparams(("arbitrary", "arbitrary")),
    )(p_dl, p_dl, p_dl)


def _dl_bwd(p_dl, o, lse, do, *, name):
    S, W = p_dl.shape[0], p_dl.shape[1] // 3
    npair = W // LANES
    nsuper = S // DIL_SUPER
    scale = HEAD_DIM ** -0.5

    def body(q_ref, k_ref, v_ref, o_ref, l_ref, do_ref, dq_ref, dk_ref, dv_ref, delta_ref, bias_ref):
        b = pl.program_id(1)

        @pl.when(b == 0)
        def _():
            dk_ref[...] = jnp.zeros_like(dk_ref)
            dv_ref[...] = jnp.zeros_like(dv_ref)
            _dil_fill_bias(bias_ref)

        dq_ref[...] = jnp.zeros_like(dq_ref)
        prod = do_ref[...] * o_ref[...]
        delta = jnp.zeros_like(prod)
        for hm in _head_masks(prod.shape):
            delta = jnp.where(hm, jnp.sum(jnp.where(hm, prod, 0.0), axis=1, keepdims=True), delta)
        delta_ref[...] = delta

        def block(pi, dil, c, n, gn):
            ws = jnp.maximum(gn - 1, 0)
            qrows = n * (DIL_BLOCK * dil) + c
            krows = ws * (DIL_BLOCK * dil) + c
            q_idx = _dil_rows(qrows, DIL_BLOCK, dil)
            k_idx = _dil_rows(krows, 2 * DIL_BLOCK, dil)
            qb = q_ref[q_idx, :]
            dob = do_ref[q_idx, :]
            lb = l_ref[q_idx, :]
            db = delta_ref[q_idx, :]
            kb = k_ref[k_idx, :].astype(BF16)
            vb = v_ref[k_idx, :].astype(BF16)
            q2 = _sb_stack_heads(qb.astype(BF16), scale)
            do2 = _sb_stack_heads(dob.astype(BF16))
            lse2 = jnp.concatenate([lb[:, HEAD_DIM * h:HEAD_DIM * h + 1] for h in range(2)], axis=0)
            delta2 = jnp.concatenate([db[:, HEAD_DIM * h:HEAD_DIM * h + 1] for h in range(2)], axis=0)
            z = lax.dot_general(q2, kb, (((1,), (1,)), ((), ())), preferred_element_type=F32)
            p = jnp.exp((z + bias_ref[gn - ws]) - lse2)
            dp = lax.dot_general(do2, vb, (((1,), (1,)), ((), ())), preferred_element_type=F32)
            dzb = (p * (dp - delta2)).astype(BF16)
            tn_dims = (((0,), (0,)), ((), ()))
            dq_blk = _sb_unstack_heads(jnp.dot(dzb, kb, preferred_element_type=F32)) * scale
            dk_blk = lax.dot_general(dzb, q2, tn_dims, preferred_element_type=F32)
            dv_blk = lax.dot_general(p.astype(BF16), do2, tn_dims, preferred_element_type=F32)
            dq_ref[q_idx, :] = dq_ref[q_idx, :] + dq_blk
            dk_ref[k_idx, :] = dk_ref[k_idx, :] + dk_blk
            dv_ref[k_idx, :] = dv_ref[k_idx, :] + dv_blk

        _dil_blocks(b, block)

    blk = pl.BlockSpec((DIL_SUPER, LANES), lambda p, b: (b, p))
    full = lambda off: pl.BlockSpec((S, LANES), lambda p, b: (0, off + p))
    return pl.pallas_call(
        body, name=name, grid=(npair, nsuper),
        out_shape=[jax.ShapeDtypeStruct((S, W), F32)] * 3,
        in_specs=[blk, full(npair), full(2 * npair), blk, blk, blk], out_specs=[blk, full(0), full(0)],
        scratch_shapes=[pltpu.VMEM((DIL_SUPER, LANES), F32), pltpu.VMEM((2, 2 * DIL_BLOCK, 2 * DIL_BLOCK), F32)],
        compiler_params=_cparams(("arbitrary", "arbitrary")),
    )(p_dl, p_dl, p_dl, o, lse, do)


class _NoExchange:
    def gather(self, family):
        return None

    def gathered(self, family, got, weights):
        pass

    def send(self, family, grads):
        return None

    def received(self, family, got):
        pass


def _local_step(x, target, gains, weights, exchanges=None):
    S, D = x.shape
    ex = exchanges or _NoExchange()
    weights = dict(weights)
    d_sb = gains["sb_out_norm"].shape[1]
    d_dl = gains["dil_out_norm"].shape[1]
    cos_t, sin_t = _rope_tables(S)

    riders = ("ffn1_w_up", "ffn1_w_down", "mixer") if exchanges else (None, None, None)
    (x1, h2), saved1 = _ffn_fwd(x, gains["ffn1_norm"], weights, tag="ffn1", ex=ex, riders=riders,
                                first_rider="ffn1_w_gate" if exchanges else None,
                                next_gain=gains["mix_norm"])
    w_in = weights["w_in"]
    w_out = weights["w_out"]
    p_sb = _mm(h2, w_in, tb=True, b_cols=(0, 3 * d_sb), outs=(BF16,), name="proj_sb")

    def rope_qk(acc, _, cos, sin):
        return jnp.concatenate([_rotate(acc[:, :2 * d_dl], cos, sin, 1.0), acc[:, 2 * d_dl:]], axis=1)

    p_dl = _mm(h2, w_in, tb=True, b_cols=(3 * d_sb, 3 * d_dl), lanes=(cos_t, sin_t), epilogue=rope_qk,
               name="proj_dl_rope")
    plan = ex.gather("ffn2" if exchanges else None)
    o_sb, carries, *got = _sb_fwd(p_sb, name="sb_fwd", comm=plan)
    ex.gathered("ffn2", got[0] if got else None, weights)
    o_dl, lse_dl = _dl_fwd(p_dl, name="dl_fwd")
    merged = _rms_fwd([o_sb, o_dl], [gains["sb_out_norm"], gains["dil_out_norm"]], name="out_norm")
    x2, h3 = _mm(merged, w_out, res=x1, rows=(gains["ffn2_norm"],), outs=(F32, BF16),
                 epilogue=_residual_then_norm(1.0), name="out_proj_norm")
    (dx3, d_final, loss_wide), saved2 = _ffn_fwd(x2, gains["ffn2_norm"], weights, tag="ffn2", ex=ex, h=h3,
                                                 head=(gains["final_norm"], target))
    loss_row = loss_wide[:, :LANES]

    dx2, d_ffn2_norm, dwg2, dwu2, dwd2 = _ffn_bwd(dx3, x2, gains["ffn2_norm"], weights, saved2, tag="ffn2", ex=ex)
    d_w_out = _mm(merged, dx2, ta=True, outs=(GRAD_WIRE,), name="d_w_out")
    do_sb, d_sb_norm = _mm(dx2, w_out, tb=True, b_cols=(0, d_sb), extras=(o_sb,), rows=(gains["sb_out_norm"],),
                           row_sums=1, epilogue=_rms_bwd_epilogue, name="d_merged_sb")
    do_dl, d_dl_norm = _mm(dx2, w_out, tb=True, b_cols=(d_sb, d_dl), extras=(o_dl,), rows=(gains["dil_out_norm"],),
                           row_sums=1, epilogue=_rms_bwd_epilogue, name="d_merged_dl")
    plan = ex.send("ffn2", dict(ffn2_w_gate=dwg2, ffn2_w_up=dwu2, ffn2_w_down=dwd2))
    dq_sb, dk_sb, dv_sb, *got = _sb_bwd(p_sb, do_sb, carries, name="sb_bwd", comm=plan)
    ex.received("ffn2", got[0] if got else None)
    dq_dl, dk_dl, dv_dl = _dl_bwd(p_dl, o_dl, lse_dl, do_dl, name="dl_bwd")
    d_proj = _join_d_proj([dq_sb, dk_sb, dv_sb, dq_dl, dk_dl, dv_dl], (3, 4), cos_t, sin_t, name="d_proj")
    d_w_in = _mm(h2, d_proj, ta=True, outs=(GRAD_WIRE,), name="d_w_in")
    dx1, d_mix_norm = _mm(d_proj, w_in, extras=(x1, dx2), rows=(gains["mix_norm"],), row_sums=1,
                          epilogue=_rms_bwd_epilogue, name="dh_mix_norm_bwd")
    dx, d_ffn1_norm, dwg1, dwu1, dwd1 = _ffn_bwd(
        dx1, x, gains["ffn1_norm"], weights, saved1, tag="ffn1", ex=ex,
        rider=("mixer", dict(w_in=d_w_in, w_out=d_w_out)), spread=True)
    gain_grads = dict(ffn1_norm=d_ffn1_norm, mix_norm=d_mix_norm, sb_out_norm=d_sb_norm, dil_out_norm=d_dl_norm,
                      ffn2_norm=d_ffn2_norm, final_norm=d_final)
    weight_grads = dict(ffn1_w_gate=dwg1, ffn1_w_up=dwu1, ffn1_w_down=dwd1, w_in=d_w_in, w_out=d_w_out,
                        ffn2_w_gate=dwg2, ffn2_w_up=dwu2, ffn2_w_down=dwd2)
    return loss_row, dx, gain_grads, weight_grads


def _mesh_position():
    return lax.axis_index("x"), lax.axis_index("y"), lax.axis_index("c")


def _flip(coord, bit):
    return 1 - coord if bit else coord


RELATIONS = [(rx, ry, rc) for rx in (0, 1) for ry in (0, 1) for rc in (0, 1)][1:]


class _GatherPlan:
    def __init__(self, shards):
        n = len(shards)
        self.operands = list(shards)
        self.out_shapes = [jax.ShapeDtypeStruct((N_DEV,) + s.shape, s.dtype) for s in shards]
        self.scratch = [pltpu.SemaphoreType.DMA((n, 7)), pltpu.SemaphoreType.DMA((n, 7)),
                        pltpu.SemaphoreType.DMA((n,))]

    def _copies(self, in_refs, out_refs, sems):
        send_sems, recv_sems, local_sems = sems
        x, y, c = _mesh_position()
        me, sibling = (x, y, c), (x, y, 1 - c)
        chips = [(1 - x, y), (x, 1 - y), (1 - x, 1 - y)]
        plans = []
        for t, (x_ref, out_ref) in enumerate(zip(in_refs, out_refs)):
            def slot(px, py, pc, out_ref=out_ref):
                return out_ref.at[4 * px + 2 * py + pc]

            def copy(k, block, to, src=None, t=t, slot=slot):
                return pltpu.make_async_remote_copy(
                    src_ref=slot(*block) if src is None else src, dst_ref=slot(*block),
                    send_sem=send_sems.at[t, k], recv_sem=recv_sems.at[t, k],
                    device_id=to, device_id_type=pl.DeviceIdType.MESH)

            plans.append(dict(
                mine=pltpu.make_async_copy(x_ref, slot(*me), local_sems.at[t]),
                first=[copy(0, me, sibling, src=x_ref)]
                + [copy(1 + j, me, (*chip, c), src=x_ref) for j, chip in enumerate(chips)],
                over_ici=[copy(1 + j, (*chip, c), me) for j, chip in enumerate(chips)],
                passed=[copy(4 + j, (*chip, c), sibling) for j, chip in enumerate(chips)],
                from_sibling=[copy(0, sibling, me)] + [copy(4 + j, (*chip, 1 - c), me) for j, chip in enumerate(chips)]))
        return plans

    def start(self, in_refs, out_refs, sems):
        for p in self._copies(in_refs, out_refs, sems):
            p["mine"].start()
            for cp in p["first"]:
                cp.start()

    def finish(self, in_refs, out_refs, sems):
        plans = self._copies(in_refs, out_refs, sems)
        for p in plans:
            for arrived, onward in zip(p["over_ici"], p["passed"]):
                arrived.wait_recv()
                onward.start()
        for p in plans:
            for cp in p["from_sibling"]:
                cp.wait_recv()
            for cp in p["first"] + p["passed"]:
                cp.wait_send()
            p["mine"].wait()


class _Hosted:
    def __init__(self, plan, n_in, n_out, n_scratch):
        self.plan, self.n_in, self.n_out, self.n_scratch = plan, n_in, n_out, n_scratch
        self.operands = list(plan.operands) if plan else []
        self.out_shapes = list(plan.out_shapes) if plan else []
        self.scratch = list(plan.scratch) if plan else []
        self.in_specs = [pl.BlockSpec(memory_space=pl.ANY)] * len(self.operands)
        self.out_specs = [pl.BlockSpec(memory_space=pl.ANY)] * len(self.out_shapes)

    def semantics(self, sem):
        return sem if self.plan is None else ("arbitrary",) * len(sem)

    def _at(self, grid, last):
        hit = None
        for d, n in enumerate(grid):
            here = pl.program_id(d) == (n - 1 if last else 0)
            hit = here if hit is None else hit & here
        return hit

    def begin(self, refs, grid):
        if self.plan is None:
            return refs
        k_in, k_out = len(self.operands), len(self.out_shapes)
        ins, rest = refs[:self.n_in], refs[self.n_in:]
        c_in, rest = rest[:k_in], rest[k_in:]
        outs, rest = rest[:self.n_out], rest[self.n_out:]
        c_out, rest = rest[:k_out], rest[k_out:]
        scratch, sems = rest[:self.n_scratch], rest[self.n_scratch:]
        self._args = (c_in, c_out, sems)
        pl.when(self._at(grid, False))(lambda: self.plan.start(*self._args))
        return tuple(ins) + tuple(outs) + tuple(scratch)

    def end(self, grid):
        if self.plan is not None:
            pl.when(self._at(grid, True))(lambda: self.plan.finish(*self._args))


class _ExchangePlan:
    def __init__(self, packs):
        n = len(packs)
        self.operands = list(packs)
        self.out_shapes = [jax.ShapeDtypeStruct(p.shape, p.dtype) for p in packs]
        self.scratch = [pltpu.SemaphoreType.DMA((n, 7)), pltpu.SemaphoreType.DMA((n, 7)),
                        pltpu.SemaphoreType.DMA((n,))]

    def _copies(self, in_refs, out_refs, sems):
        send_sems, recv_sems, local_sems = sems
        x, y, c = _mesh_position()
        me = 4 * x + 2 * y + c
        copies = [pltpu.make_async_copy(i.at[me], o.at[me], local_sems.at[t])
                  for t, (i, o) in enumerate(zip(in_refs, out_refs))]
        for r, (rx, ry, rc) in enumerate(RELATIONS):
            px, py, pc = _flip(x, rx), _flip(y, ry), _flip(c, rc)
            peer = 4 * px + 2 * py + pc
            copies += [pltpu.make_async_remote_copy(
                src_ref=i.at[peer], dst_ref=o.at[me], send_sem=send_sems.at[t, r], recv_sem=recv_sems.at[t, r],
                device_id=(px, py, pc), device_id_type=pl.DeviceIdType.MESH)
                for t, (i, o) in enumerate(zip(in_refs, out_refs))]
        return copies

    def start(self, in_refs, out_refs, sems):
        for cp in self._copies(in_refs, out_refs, sems):
            cp.start()

    def finish(self, in_refs, out_refs, sems):
        for cp in self._copies(in_refs, out_refs, sems):
            cp.wait()


def _all_reduce_rows(v, *, name):
    R, C = v.shape

    def body(v_ref, out_ref, buf, send_sems, recv_sems):
        x, y, c = _mesh_position()
        me = 4 * x + 2 * y + c
        buf[me] = v_ref[...]
        copies = []
        for r, (rx, ry, rc) in enumerate(RELATIONS):
            cp = pltpu.make_async_remote_copy(
                src_ref=v_ref, dst_ref=buf.at[me], send_sem=send_sems.at[r], recv_sem=recv_sems.at[r],
                device_id=(_flip(x, rx), _flip(y, ry), _flip(c, rc)), device_id_type=pl.DeviceIdType.MESH)
            cp.start()
            copies.append(cp)
        for cp in copies:
            cp.wait()
        total = buf[0]
        for s in range(1, N_DEV):
            total = total + buf[s]
        out_ref[...] = total

    return pl.pallas_call(
        body, name=name,
        out_shape=jax.ShapeDtypeStruct((R, C), F32),
        in_specs=[pl.BlockSpec(memory_space=pltpu.VMEM)],
        out_specs=pl.BlockSpec(memory_space=pltpu.VMEM),
        scratch_shapes=[pltpu.VMEM((N_DEV, R, C), F32), pltpu.SemaphoreType.DMA((7,)), pltpu.SemaphoreType.DMA((7,))],
    )(v)


def _adamw(w, g, m, v, *, name):
    R, C = w.shape
    slots = g.ndim == 3
    tr = _pick(R, (256, 128, 64, 32, 16) if slots else (256, 128, 64, 32, 16, 8))

    def body(w_ref, g_ref, m_ref, v_ref, g_out, d_ref, nm_ref, nv_ref):
        if slots:
            g = g_ref[0].astype(F32)
            for s in range(1, N_DEV):
                g = g + g_ref[s].astype(F32)
        else:
            g = g_ref[...]
        g_out[...] = g
        m_new = ADAM_B1 * m_ref[...] + (1.0 - ADAM_B1) * g
        v_new = ADAM_B2 * v_ref[...] + (1.0 - ADAM_B2) * (g * g)
        m_hat = m_new / (1.0 - ADAM_B1 ** ADAM_STEP)
        v_hat = v_new / (1.0 - ADAM_B2 ** ADAM_STEP)
        d_ref[...] = -ADAM_LR * (m_hat / (jnp.sqrt(v_hat) + ADAM_EPS) + ADAM_WD * w_ref[...])
        nm_ref[...] = m_new
        nv_ref[...] = v_new

    spec = pl.BlockSpec((tr, C), lambda i: (i, 0))
    g_spec = pl.BlockSpec((N_DEV, tr, C), lambda i: (0, i, 0)) if slots else spec
    return pl.pallas_call(
        body, name=name, grid=(R // tr,),
        out_shape=[jax.ShapeDtypeStruct((R, C), F32)] * 4,
        in_specs=[spec, g_spec, spec, spec], out_specs=[spec] * 4,
        compiler_params=_cparams(("parallel",)),
    )(w, g, m, v)


WEIGHT_NAMES = ["ffn1_norm", "ffn1_w_gate", "ffn1_w_up", "ffn1_w_down", "mix_norm", "w_in", "sb_out_norm",
                "dil_out_norm", "w_out", "ffn2_norm", "ffn2_w_gate", "ffn2_w_up", "ffn2_w_down", "final_norm"]
GAIN_NAMES = ["ffn1_norm", "mix_norm", "sb_out_norm", "dil_out_norm", "ffn2_norm", "final_norm"]
COL_SHARDED = ["ffn1_w_gate", "ffn1_w_up", "ffn2_w_gate", "ffn2_w_up", "w_in"]
ROW_SHARDED = ["ffn1_w_down", "ffn2_w_down", "w_out"]
GROUPS = {"mixer": (["w_in"], ["w_out"]),
          "ffn2": (["ffn2_w_gate", "ffn2_w_up"], ["ffn2_w_down"])}
for _ffn in ("ffn1", "ffn2"):
    GROUPS.update({f"{_ffn}_w_gate": ([f"{_ffn}_w_gate"], []), f"{_ffn}_w_up": ([f"{_ffn}_w_up"], []),
                   f"{_ffn}_w_down": ([], [f"{_ffn}_w_down"])})


class _Exchanges:
    def __init__(self, params):
        self.params = params
        self.grads = {}

    def gather(self, group):
        if group is None:
            return None
        cols, rows = GROUPS[group]
        return _GatherPlan([self.params[n].T.astype(BF16) for n in cols] + [self.params[n].astype(BF16) for n in rows])

    def gathered(self, group, got, weights):
        if group is None:
            return
        cols, rows = GROUPS[group]
        for n, blocks in zip(cols + rows, got):
            weights[n] = blocks.reshape(N_DEV * blocks.shape[1], blocks.shape[2])

    def send(self, group, grads):
        if group is None:
            return None
        cols, rows = GROUPS[group]
        packs = [jnp.transpose(grads[n].reshape(grads[n].shape[0], N_DEV, self.params[n].shape[1]), (1, 0, 2))
                 for n in cols]
        packs += [grads[n].reshape(N_DEV, self.params[n].shape[0], grads[n].shape[1]) for n in rows]
        return _ExchangePlan([p.astype(GRAD_WIRE) for p in packs])

    def received(self, group, got):
        if group is None:
            return
        cols, rows = GROUPS[group]
        for n, slots in zip(cols + rows, got):
            self.grads[n] = slots


def _step(x, target, params, moments_m, moments_v):
    ex = _Exchanges(params)
    weights = {}
    gains = {n: params[n] for n in GAIN_NAMES}
    loss_row, grad_x, gain_grads, _ = _local_step(x, target, gains, weights, ex)
    grads = ex.grads

    rows = [gain_grads[n].reshape(-1, LANES) for n in GAIN_NAMES] + [loss_row]
    small = jnp.concatenate(rows, axis=0)
    pad = (-small.shape[0]) % 8
    small = jnp.pad(small, ((0, pad), (0, 0)))
    small = _all_reduce_rows(small, name="reduce_gains_loss")
    off = 0
    for n in GAIN_NAMES:
        r = gain_grads[n].shape[1] // LANES
        grads[n] = small[off:off + r].reshape(1, -1)
        off += r
    loss = small[off, 0]

    delta, new_m, new_v = {}, {}, {}
    for n in WEIGHT_NAMES:
        grads[n], delta[n], new_m[n], new_v[n] = _adamw(params[n], grads[n], moments_m[n], moments_v[n],
                                                        name=f"adamw_{n}")
    return loss, grad_x, grads, delta, new_m, new_v


def kernel(x, ffn1_norm, ffn1_w_gate, ffn1_w_up, ffn1_w_down, mix_norm, w_in, sb_out_norm, dil_out_norm, w_out, ffn2_norm, ffn2_w_gate, ffn2_w_up, ffn2_w_down, final_norm, loss_target, m_ffn1_norm, m_ffn1_w_gate, m_ffn1_w_up, m_ffn1_w_down, m_mix_norm, m_w_in, m_sb_out_norm, m_dil_out_norm, m_w_out, m_ffn2_norm, m_ffn2_w_gate, m_ffn2_w_up, m_ffn2_w_down, m_final_norm, v_ffn1_norm, v_ffn1_w_gate, v_ffn1_w_up, v_ffn1_w_down, v_mix_norm, v_w_in, v_sb_out_norm, v_dil_out_norm, v_w_out, v_ffn2_norm, v_ffn2_w_gate, v_ffn2_w_up, v_ffn2_w_down, v_final_norm):
    given = dict(locals())
    shapes = {n: given[n].shape for n in WEIGHT_NAMES}

    def as2d(a):
        return a.reshape(1, -1) if a.ndim == 1 else a.reshape(a.shape[-2], a.shape[-1])

    params = {n: as2d(given[n]) for n in WEIGHT_NAMES}
    moments_m = {n: as2d(given["m_" + n]) for n in WEIGHT_NAMES}
    moments_v = {n: as2d(given["v_" + n]) for n in WEIGHT_NAMES}
    loss, grad_x, grads, delta, new_m, new_v = _step(x[0], loss_target[0], params, moments_m, moments_v)
    back = lambda d: [d[n].reshape(shapes[n]) for n in WEIGHT_NAMES]
    return (loss, grad_x[None], *back(grads), *back(delta), *back(new_m), *back(new_v))
```

```python
import functools

import jax
import jax.numpy as jnp
from jax import lax
from jax.experimental import pallas as pl
from jax.experimental.pallas import tpu as pltpu

F32 = jnp.float32
BF16 = jnp.bfloat16
GRAD_WIRE = jnp.bfloat16

N_DEV = 8
HEAD_DIM = 64
LANES = 128
DILATED_PATTERNS = ((128, 1), (512, 4), (2048, 16))
DIL_BLOCK = 128
DIL_SUPER = 2048
DIL_UNROLL = 16
SB_TILE = 256
SB_LANES = 128
SB_UNROLL = 4
SB_STEP_TILES = 2
SB_DEAD = 90.0
SB_UNSEEN = -1e30
ROPE_THETA = 10000.0
RMS_EPS = 1e-6
HALF_STEP = 0.5
ADAM_LR = 0.001
ADAM_B1 = 0.9
ADAM_B2 = 0.999
ADAM_EPS = 1e-08
ADAM_WD = 0.01
ADAM_STEP = 10
NEG_BIG = -1e30
VMEM_CAP_MB = 60


def _pick(n, prefs):
    for p in prefs:
        if n % p == 0:
            return p
    return n


MM_MAX_TILE = 1536
MM_WHOLE = 3072
RING_SLOTS = 3


def _largest_tile(n, cap):
    if n <= cap:
        return n
    for t in range(cap - cap % LANES, 0, -LANES):
        if n % t == 0:
            return t
    return n


def _cparams(sem=None, vmem_mb=48):
    return pltpu.CompilerParams(dimension_semantics=sem, vmem_limit_bytes=min(vmem_mb, VMEM_CAP_MB) * 1024 * 1024)


def _nbytes(shape, dtype):
    n = 1
    for s in shape:
        n *= s
    return n * jnp.dtype(dtype).itemsize


def _mm(a, b, *, name, ta=False, tb=False, outs=(F32,), res=None, alpha=1.0, extras=(), epilogue=None,
        tm=None, tn=None, tk=None, comm=None, rows=(), lanes=(), row_sums=0, b_cols=None, second=None,
        twin_b=None, ring=False):
    if ta:
        K, M = a.shape
    else:
        M, K = a.shape
    if tb:
        N, Kb = b.shape
    else:
        Kb, N = b.shape
    col0 = 0
    if b_cols is not None:
        col0, N = b_cols
    assert K == Kb, (a.shape, b.shape, ta, tb)
    tn = tn or (N if (not ta and K <= MM_WHOLE and N <= MM_WHOLE) else _largest_tile(N, MM_MAX_TILE))
    wide = tn > MM_MAX_TILE and (len(extras) + len(outs) > 3 or a.dtype == F32)
    tm = tm or (_largest_tile(M, MM_MAX_TILE) if ta else _pick(M, (256, 128) if wide else (512, 256, 128)))
    tk = tk or (K if K <= MM_WHOLE else _pick(K, (2048, 1024, 512, 256, 128)))
    nk = K // tk
    a_spec = pl.BlockSpec((tk, tm), lambda i, j, k: (k, i)) if ta else pl.BlockSpec((tm, tk), lambda i, j, k: (i, k))
    assert col0 % tn == 0
    b_spec = (pl.BlockSpec((tn, tk), lambda i, j, k: (j + col0 // tn, k)) if tb
              else pl.BlockSpec((tk, tn), lambda i, j, k: (k, j + col0 // tn)))
    mn_spec = pl.BlockSpec((tm, tn), lambda i, j, k: (i, j))
    dims = (((0 if ta else 1,), (1 if tb else 0,)), ((), ()))
    row_spec = pl.BlockSpec((1, tn), lambda i, j, k: (0, j))
    lane_spec = pl.BlockSpec((tm, LANES), lambda i, j, k: (i, 0))
    n_extra = len(extras) + (1 if res is not None else 0) + len(rows) + len(lanes)
    n_mn = len(outs)
    n_out = n_mn + row_sums
    assert row_sums == 0 or tn == N
    grid = (M // tm, N // tn, nk)
    n_ab = 2 + (2 if second is not None else 0) + (1 if twin_b is not None else 0)
    assert twin_b is None or (nk == 1 and epilogue is not None)
    n_ring = len(extras) if ring else 0
    assert not ring or (grid[1] == 1 and nk == 1 and res is None)
    hosted = _Hosted(comm, n_in=n_ab + n_extra, n_out=n_out, n_scratch=(1 if nk > 1 else 0) + 2 * n_ring)

    def body(*refs):
        in_refs = list(refs[n_ab:n_ab + n_extra])
        ab_refs = refs[:n_ab]
        refs = hosted.begin(refs, grid)
        out_refs = refs[n_ab + n_extra:n_ab + n_extra + n_out]
        ring_blocks = []
        if n_ring:
            scratch = refs[n_ab + n_extra + n_out:]
            bufs, sems = scratch[:n_ring], scratch[n_ring:2 * n_ring]
            step, steps = pl.program_id(0), grid[0]

            def fetch(s):
                return [pltpu.make_async_copy(src.at[pl.ds(pl.multiple_of(s * tm, tm), tm), :], buf.at[s % RING_SLOTS],
                                              sem.at[s % RING_SLOTS])
                        for src, buf, sem in zip(in_refs[:n_ring], bufs, sems)]

            @pl.when(step == 0)
            def _():
                for s in range(min(RING_SLOTS - 1, steps)):
                    for cp in fetch(s):
                        cp.start()

            @pl.when(step + RING_SLOTS - 1 < steps)
            def _():
                for cp in fetch(step + RING_SLOTS - 1):
                    cp.start()

            for cp in fetch(step):
                cp.wait()
            ring_blocks = [buf[step % RING_SLOTS] for buf in bufs]
        def product(a_ref, b_ref):
            return lax.dot_general(a_ref[...].astype(BF16), b_ref[...].astype(BF16), dims, preferred_element_type=F32)

        prod = product(ab_refs[0], ab_refs[1])
        if second is not None:
            prod = prod + product(ab_refs[2], ab_refs[3])
        if twin_b is not None:
            prod = (prod, product(ab_refs[0], ab_refs[-1]))

        def finish(acc):
            blocks = ring_blocks + [r[...] for r in in_refs[n_ring:]]
            if res is not None:
                r_blk, blocks = blocks[0], blocks[1:]
            else:
                r_blk = None
            if epilogue is None:
                val = acc * alpha
                if r_blk is not None:
                    val = val + r_blk
                vals = (val,)
            else:
                vals = epilogue(acc, r_blk, *blocks)
                vals = vals if isinstance(vals, (tuple, list)) else (vals,)
            for o_ref, v in zip(out_refs[:n_mn], vals[:n_mn]):
                o_ref[...] = v.astype(o_ref.dtype)
            first_rows = pl.program_id(0) == 0
            for o_ref, part in zip(out_refs[n_mn:], vals[n_mn:]):
                @pl.when(first_rows)
                def _(o_ref=o_ref, part=part):
                    o_ref[...] = part

                @pl.when(jnp.logical_not(first_rows))
                def _(o_ref=o_ref, part=part):
                    o_ref[...] += part

        if nk == 1:
            finish(prod)
        else:
            acc_ref = refs[n_ab + n_extra + n_out]
            k = pl.program_id(2)

            @pl.when(k == 0)
            def _():
                acc_ref[...] = prod

            @pl.when(k > 0)
            def _():
                acc_ref[...] += prod

            @pl.when(k == nk - 1)
            def _():
                finish(acc_ref[...])

        hosted.end(grid)

    mn_operands = ([res] if res is not None else []) + list(extras)
    ab = [a, b] + (list(second) if second is not None else []) + ([twin_b] if twin_b is not None else [])
    ab_specs = [a_spec, b_spec] * (1 if second is None else 2) + ([b_spec] if twin_b is not None else [])
    operands = ab + mn_operands + list(rows) + list(lanes)
    any_spec = pl.BlockSpec(memory_space=pl.ANY)
    mn_specs = [mn_spec] * (len(mn_operands) - n_ring) + [any_spec] * n_ring
    in_specs = ab_specs + mn_specs + [row_spec] * len(rows) + [lane_spec] * len(lanes)
    est = n_ab * (_nbytes((tm, tk), a.dtype) + _nbytes((tk, tn), b.dtype))
    est += RING_SLOTS * sum(_nbytes((tm, tn), o.dtype) for o in mn_operands)
    est += 2 * sum(_nbytes((tm, tn), d) for d in outs) + 2 * _nbytes((tm, tn), F32)
    semantics = ("parallel", "parallel", "arbitrary") if row_sums == 0 and not ring else ("arbitrary",) * 3
    ring_scratch = [pltpu.VMEM((RING_SLOTS, tm, tn), e.dtype) for e in extras[:n_ring]]
    ring_scratch += [pltpu.SemaphoreType.DMA((RING_SLOTS,))] * n_ring
    result = pl.pallas_call(
        body, name=name, grid=grid,
        out_shape=[jax.ShapeDtypeStruct((M, N), d) for d in outs]
        + [jax.ShapeDtypeStruct((1, N), F32)] * row_sums + hosted.out_shapes,
        in_specs=in_specs + hosted.in_specs,
        out_specs=[mn_spec] * n_mn + [row_spec] * row_sums + hosted.out_specs,
        scratch_shapes=([pltpu.VMEM((tm, tn), F32)] if nk > 1 else []) + ring_scratch + hosted.scratch,
        compiler_params=_cparams(hosted.semantics(semantics), vmem_mb=max(32, 2 * est // (1024 * 1024))),
    )(*operands, *hosted.operands)
    own, got = result[:n_out], list(result[n_out:])
    own = own[0] if n_out == 1 else own
    return own if comm is None else (own, got)


def _rms_hat(x):
    r = lax.rsqrt(jnp.mean(x * x, axis=-1, keepdims=True) + RMS_EPS)
    return x * r, r


def _rms_fwd(xs, gains, *, name, comm=None):
    S = xs[0].shape[0]
    widths = [x.shape[1] for x in xs]
    tm = _pick(S, (512, 256, 128))
    n = len(xs)
    grid = (S // tm,)
    hosted = _Hosted(comm, n_in=2 * n, n_out=1, n_scratch=0)

    def body(*refs):
        refs = hosted.begin(refs, grid)
        o_ref = refs[2 * n]
        off = 0
        for i in range(n):
            xh, _ = _rms_hat(refs[i][...])
            o_ref[:, off:off + widths[i]] = (xh * refs[n + i][...]).astype(o_ref.dtype)
            off += widths[i]
        hosted.end(grid)

    out, *got = pl.pallas_call(
        body, name=name, grid=grid,
        out_shape=[jax.ShapeDtypeStruct((S, sum(widths)), BF16)] + hosted.out_shapes,
        in_specs=[pl.BlockSpec((tm, w), lambda i: (i, 0)) for w in widths]
        + [pl.BlockSpec((1, w), lambda i: (0, 0)) for w in widths] + hosted.in_specs,
        out_specs=[pl.BlockSpec((tm, sum(widths)), lambda i: (i, 0))] + hosted.out_specs,
        scratch_shapes=hosted.scratch,
        compiler_params=_cparams(hosted.semantics(("parallel",))),
    )(*xs, *gains, *hosted.operands)
    return out if comm is None else (out, got)


def _sigmoid(g):
    return 1.0 / (1.0 + jnp.exp(-g))


def _ride(result, plan):
    return result if plan is not None else (result, None)


def _residual_then_norm(alpha):
    def epilogue(acc, res, gain):
        y = res + alpha * acc
        return y, _rms_hat(y)[0] * gain
    return epilogue


def _ffn_fwd(x, gain, w, *, tag, ex, first_rider=None, riders=(None, None, None), head=None, h=None,
             next_gain=None):
    if h is None:
        plan = ex.gather(first_rider)
        h, got = _ride(_rms_fwd([x], [gain], name=f"{tag}_norm", comm=plan), plan)
        ex.gathered(first_rider, got, w)
    def act(acc, _, g_blk):
        gf = g_blk.astype(F32)
        return acc, gf * _sigmoid(gf) * acc

    if riders[0] is None and riders[1] is None and f"{tag}_w_up" in w:
        def gate_up_act(accs, _):
            g_blk = accs[0].astype(BF16)
            return (g_blk,) + act(accs[1], None, g_blk)

        g, u, a = _mm(h, w[f"{tag}_w_gate"], tb=True, twin_b=w[f"{tag}_w_up"], outs=(BF16, BF16, BF16),
                      epilogue=gate_up_act, tm=256, name=f"{tag}_gate_up_act")
    else:
        plan = ex.gather(riders[0])
        g, got = _ride(_mm(h, w[f"{tag}_w_gate"], tb=True, outs=(BF16,), name=f"{tag}_gate", comm=plan), plan)
        ex.gathered(riders[0], got, w)
        plan = ex.gather(riders[1])
        (u, a), got = _ride(_mm(h, w[f"{tag}_w_up"], tb=True, outs=(BF16, BF16), extras=(g,), epilogue=act,
                                name=f"{tag}_up_act", comm=plan), plan)
        ex.gathered(riders[1], got, w)
    plan = ex.gather(riders[2])
    if head is None and next_gain is None:
        y, got = _ride(_mm(a, w[f"{tag}_w_down"], res=x, alpha=HALF_STEP, name=f"{tag}_down", comm=plan), plan)
    elif head is None:
        y, got = _ride(_mm(a, w[f"{tag}_w_down"], res=x, rows=(next_gain,), outs=(F32, BF16),
                           epilogue=_residual_then_norm(HALF_STEP), name=f"{tag}_down_norm", comm=plan), plan)
    else:
        final_gain, target = head
        y, got = _ride(_mm(a, w[f"{tag}_w_down"], res=x, extras=(target,), rows=(final_gain,), row_sums=2,
                           epilogue=_loss_head_epilogue, name=f"{tag}_down_loss", comm=plan), plan)
    ex.gathered(riders[2], got, w)
    return y, (h, g, u, a)


def _loss_head_epilogue(acc, x_in, target, gain):
    xh, r = _rms_hat(x_in + HALF_STEP * acc)
    err = xh * gain - target
    dy = err * (1.0 / acc.shape[1])
    dxh = dy * gain
    dx = r * (dxh - xh * jnp.mean(dxh * xh, axis=-1, keepdims=True))
    loss = 0.5 * jnp.sum(jnp.mean(err * err, axis=-1, keepdims=True), axis=0, keepdims=True)
    return dx, jnp.sum(dy * xh, axis=0, keepdims=True), jnp.zeros_like(gain) + loss


def _rms_bwd_epilogue(acc, dh_so_far, x, *dres_and_gain):
    gain = dres_and_gain[-1]
    dh = acc if dh_so_far is None else acc + dh_so_far
    xh, r = _rms_hat(x)
    dxh = dh * gain
    dx = r * (dxh - xh * jnp.mean(dxh * xh, axis=-1, keepdims=True))
    if len(dres_and_gain) == 2:
        dx = dx + dres_and_gain[0]
    return dx, jnp.sum(dh * xh, axis=0, keepdims=True)


def _ffn_bwd(dout, x, gain, w, saved, *, tag, ex, rider=(None, None), spread=False):
    h, g, u, a = saved
    wg, wu, wd = (w[f"{tag}_w_{n}"] for n in ("gate", "up", "down"))

    def act_bwd(acc, _, g_blk, u_blk):
        gf, uf = g_blk.astype(F32), u_blk.astype(F32)
        da = acc * HALF_STEP
        sig = _sigmoid(gf)
        silu = gf * sig
        return da * uf * (sig + silu * (1.0 - sig)), da * silu

    def carrying(group, grad, call):
        group = group if spread else None
        plan = ex.send(group, {group: grad})
        out, got = _ride(call(plan), plan)
        ex.received(group, got)
        return out

    plan = ex.send(*rider)
    (dg, du), got = _ride(_mm(dout, wd, tb=True, outs=(BF16, BF16), extras=(g, u), epilogue=act_bwd,
                              ring=True, name=f"{tag}_bwd_act", comm=plan), plan)
    ex.received(rider[0], got)
    dwg = _mm(h, dg, ta=True, outs=(GRAD_WIRE,), name=f"{tag}_dwg")
    dwu = carrying(f"{tag}_w_gate", dwg, lambda plan: _mm(h, du, ta=True, outs=(GRAD_WIRE,), name=f"{tag}_dwu", comm=plan))
    dwd = carrying(f"{tag}_w_up", dwu,
                   lambda plan: _mm(a, dout, ta=True, outs=(GRAD_WIRE,), alpha=HALF_STEP, name=f"{tag}_dwd", comm=plan))
    dx, dgain = carrying(f"{tag}_w_down", dwd, lambda plan: _mm(
        dg, wg, second=(du, wu), extras=(x, dout), rows=(gain,), row_sums=1, epilogue=_rms_bwd_epilogue,
        tm=512, name=f"{tag}_dh_norm_bwd", comm=plan))
    return dx, dgain, dwg, dwu, dwd


def _rope_tables(S):
    half = HEAD_DIM // 2
    inv_freq = ROPE_THETA ** (-jnp.arange(half, dtype=F32) / half)
    ang = jnp.arange(S, dtype=F32)[:, None] * inv_freq[None, :]
    cos, sin = jnp.cos(ang), jnp.sin(ang)
    reps = LANES // HEAD_DIM
    cos_t = jnp.tile(jnp.concatenate([cos, cos], axis=1), (1, reps))
    sin_t = jnp.tile(jnp.concatenate([-sin, sin], axis=1), (1, reps))
    return cos_t, sin_t


def _rotate(v, cos, sin, sign):
    half = HEAD_DIM // 2
    groups = []
    for g in range(v.shape[1] // LANES):
        t = v[:, g * LANES:(g + 1) * LANES]
        lane = lax.broadcasted_iota(jnp.int32, t.shape, 1)
        swapped = jnp.where(lane % HEAD_DIM < half, pltpu.roll(t, LANES - half, axis=1), pltpu.roll(t, half, axis=1))
        groups.append(t * cos + swapped * (sin * sign))
    return groups[0] if len(groups) == 1 else jnp.concatenate(groups, axis=1)


def _join_d_proj(pieces, rotated, cos_t, sin_t, *, name):
    S = pieces[0].shape[0]
    widths = [p.shape[1] for p in pieces]
    tm = _pick(S, (256, 128))
    n = len(pieces)

    def body(*refs):
        c_ref, s_ref, o_ref = refs[n], refs[n + 1], refs[n + 2]
        off = 0
        for i in range(n):
            v = refs[i][...]
            if i in rotated:
                v = _rotate(v, c_ref[...], s_ref[...], -1.0)
            o_ref[:, off:off + widths[i]] = v.astype(o_ref.dtype)
            off += widths[i]

    return pl.pallas_call(
        body, name=name, grid=(S // tm,),
        out_shape=jax.ShapeDtypeStruct((S, sum(widths)), BF16),
        in_specs=[pl.BlockSpec((tm, w), lambda i: (i, 0)) for w in widths]
        + [pl.BlockSpec((tm, LANES), lambda i: (i, 0))] * 2,
        out_specs=pl.BlockSpec((tm, sum(widths)), lambda i: (i, 0)),
        compiler_params=_cparams(("parallel",)),
    )(*pieces, cos_t, sin_t)


def _head_masks(shape):
    lane = lax.broadcasted_iota(jnp.int32, shape, 1)
    return [(lane >= HEAD_DIM * h) & (lane < HEAD_DIM * (h + 1)) for h in range(shape[1] // HEAD_DIM)]


def _sb_scores(q2, k_j):
    z = lax.dot_general(q2, k_j, (((1,), (1,)), ((), ())), preferred_element_type=F32)
    sign_bit = jnp.int32(-2 ** 31)
    minus_abs = lax.bitcast_convert_type(lax.bitcast_convert_type(z, jnp.int32) | sign_bit, F32)
    softplus = jnp.maximum(z, 0.0) + jnp.log(1.0 + jnp.exp(minus_abs))
    return z - softplus, softplus


def _sb_stack_heads(t, scale=None):
    parts = [jnp.where(hm, t, jnp.zeros_like(t)) for hm in _head_masks(t.shape)]
    t2 = jnp.concatenate(parts, axis=0)
    if scale is not None:
        t2 = (t2.astype(F32) * scale).astype(t2.dtype)
    return t2


def _sb_unstack_heads(t2):
    n = t2.shape[1] // HEAD_DIM
    T = t2.shape[0] // n
    masks = _head_masks((T, t2.shape[1]))
    out = t2[:T]
    for h in range(1, n):
        out = jnp.where(masks[h], t2[h * T:(h + 1) * T], out)
    return out


def _sb_causal(T, n_heads):
    row = lax.broadcasted_iota(jnp.int32, (n_heads * T, T), 0)
    col = lax.broadcasted_iota(jnp.int32, (n_heads * T, T), 1)
    return col < row % T


def _sb_triangle(T, later):
    row = lax.broadcasted_iota(jnp.int32, (T, T), 0)
    col = lax.broadcasted_iota(jnp.int32, (T, T), 1)
    return ((row > col) if later else (row < col)).astype(BF16)


def _sb_fwd(p_sb, *, name, comm=None):
    S = p_sb.shape[0]
    W = p_sb.shape[1] // 3
    LW = min(SB_LANES, W)
    NH = LW // HEAD_DIM
    npair = W // LW
    T = SB_TILE
    n_tiles = S // T
    assert n_tiles <= HEAD_DIM
    scale = HEAD_DIM ** -0.5

    R = SB_STEP_TILES
    grid = (npair, n_tiles // R)
    hosted = _Hosted(comm, n_in=5, n_out=2, n_scratch=0)

    def body(*refs):
        refs = hosted.begin(refs, grid)
        step = pl.program_id(1)
        lax.fori_loop(0, R, lambda sub, _: query_tile(step * R + sub, sub, *refs), 0)
        hosted.end(grid)

    def query_tile(I, sub, q_ref, k_ref, v_ref, causal_ref, later_ref, o_ref, c_ref):
        rows = pl.ds(pl.multiple_of(sub * T, T), T)
        lane = lax.broadcasted_iota(jnp.int32, (T, LW), 1)
        causal = causal_ref[...]
        later_than = later_ref[...]
        q2 = _sb_stack_heads(q_ref[rows, :], scale)

        def scores(J, diag):
            off = pl.multiple_of(J * T, T)
            log_beta, stay = _sb_scores(q2, k_ref[pl.ds(off, T), :])
            if diag:
                stay = stay * causal
            local = jnp.dot(stay.astype(BF16), later_than, preferred_element_type=F32)
            return log_beta, local, jnp.sum(stay, axis=1, keepdims=True), v_ref[pl.ds(off, T), :]

        def weigh(J, sc, gone, acc, carr, diag):
            log_beta, local, _, v_j = sc
            w = jnp.exp((log_beta - gone) - local)
            if diag:
                w = w * causal
            acc = acc + jnp.dot(w.astype(BF16), v_j, preferred_element_type=F32)
            for h in range(NH):
                carr = jnp.where(lane == HEAD_DIM * h + J, -gone[h * T:(h + 1) * T], carr)
            return acc, carr

        def tiles(J, count, state, diag):
            gone, acc, carr, _ = state
            scs = [scores(J - u, diag and u == 0) for u in range(count)]
            for u, sc in enumerate(scs):
                acc, carr = weigh(J - u, sc, gone, acc, carr, diag and u == 0)
                gone = gone + sc[2]
            return gone, acc, carr, jnp.min(gone)

        U = SB_UNROLL
        alive = lambda st: st[3] < SB_DEAD
        state = (jnp.zeros((NH * T, 1), F32), jnp.zeros((NH * T, LW), F32),
                 jnp.full((T, LW), SB_UNSEEN, F32), jnp.zeros((), F32))
        state = lax.cond(I > 0, lambda st: tiles(I, 2, st, True), lambda st: tiles(I, 1, st, True), state)
        rest = jnp.maximum(I - 1, 0)
        singles = jnp.where(rest > 0, (rest - 1) % U + 1, 0)
        _, state = lax.while_loop(lambda c: (c[0] < singles) & alive(c[1]),
                                  lambda c: (c[0] + 1, tiles(I - 2 - c[0], 1, c[1], False)), (jnp.int32(0), state))
        blocks = (rest - singles) // U
        _, state = lax.while_loop(lambda c: (c[0] < blocks) & alive(c[1]),
                                  lambda c: (c[0] + 1, tiles(I - 2 - singles - U * c[0], U, c[1], False)),
                                  (jnp.int32(0), state))
        _, acc, carr, _ = state
        o_ref[rows, :] = _sb_unstack_heads(acc)
        c_ref[rows, :] = carr
        return 0

    blk = lambda I_off: pl.BlockSpec((R * T, LW), lambda p, I: (I, I_off + p))
    full = lambda off: pl.BlockSpec((S, LW), lambda p, I: (0, off + p))
    const = lambda rows: pl.BlockSpec((rows, T), lambda p, I: (0, 0))
    o, carries, *got = pl.pallas_call(
        body, name=name, grid=grid,
        out_shape=[jax.ShapeDtypeStruct((S, W), F32), jax.ShapeDtypeStruct((S, W), F32)] + hosted.out_shapes,
        in_specs=[blk(0), full(npair), full(2 * npair), const(NH * T), const(T)] + hosted.in_specs,
        out_specs=[blk(0), blk(0)] + hosted.out_specs,
        scratch_shapes=hosted.scratch,
        compiler_params=_cparams(hosted.semantics(("parallel", "arbitrary")), vmem_mb=56),
    )(p_sb, p_sb, p_sb, _sb_causal(T, NH).astype(F32), _sb_triangle(T, True), *hosted.operands)
    return (o, carries) if comm is None else (o, carries, got)


def _sb_bwd(p_sb, do, carries, *, name, comm=None):
    S = p_sb.shape[0]
    W = p_sb.shape[1] // 3
    LW = min(LANES, W)
    NH = LW // HEAD_DIM
    npair = W // LW
    T = SB_TILE
    n_tiles = S // T
    scale = HEAD_DIM ** -0.5

    R = SB_STEP_TILES
    grid = (npair, n_tiles // R)
    hosted = _Hosted(comm, n_in=8, n_out=3, n_scratch=0)

    def body(*refs):
        refs = hosted.begin(refs, grid)
        dk_ref, dv_ref = refs[9], refs[10]
        step = pl.program_id(1)

        @pl.when(step == 0)
        def _():
            dk_ref[...] = jnp.zeros_like(dk_ref)
            dv_ref[...] = jnp.zeros_like(dv_ref)

        lax.fori_loop(0, R, lambda sub, _: query_tile(step * R + sub, sub, *refs), 0)
        hosted.end(grid)

    def query_tile(I, sub, q_ref, k_ref, v_ref, do_ref, c_ref, causal_ref, later_ref, earlier_ref,
                   dq_ref, dk_ref, dv_ref):
        rows = pl.ds(pl.multiple_of(sub * T, T), T)
        lane = lax.broadcasted_iota(jnp.int32, (T, LW), 1)
        causal = causal_ref[...]
        later_than = later_ref[...]
        earlier_than = earlier_ref[...]
        q2 = _sb_stack_heads(q_ref[rows, :], scale)
        do2 = _sb_stack_heads(do_ref[rows, :].astype(BF16))
        carr = c_ref[rows, :]
        tn_dims = (((0,), (0,)), ((), ()))

        def chain(J, diag):
            off = pl.multiple_of(J * T, T)
            k_j = k_ref[pl.ds(off, T), :]
            v_j = v_ref[pl.ds(off, T), :]
            log_beta, stay = _sb_scores(q2, k_j)
            if diag:
                stay = stay * causal
            lc = jnp.concatenate(
                [jnp.sum(jnp.where(lane == HEAD_DIM * h + J, carr, 0.0), axis=1, keepdims=True) for h in range(NH)],
                axis=0)
            w = jnp.exp((log_beta + lc) - jnp.dot(stay.astype(BF16), later_than, preferred_element_type=F32))
            if diag:
                w = w * causal
            dw = lax.dot_general(do2, v_j, (((1,), (1,)), ((), ())), preferred_element_type=F32)
            e = w * dw
            local = jnp.dot(e.astype(BF16), earlier_than, preferred_element_type=F32)
            return off, k_j, w, e, local, jnp.exp(log_beta), jnp.sum(e, axis=1, keepdims=True)

        def finish(ch, ec, dq_acc, diag):
            off, k_j, w, e, local, beta, _ = ch
            e_before = local + ec
            dz = e - beta * (e + e_before)
            if diag:
                dz = dz * causal
            dzb = dz.astype(BF16)
            dq_acc = dq_acc + jnp.dot(dzb, k_j, preferred_element_type=F32)
            dk_ref[pl.ds(off, T), :] += lax.dot_general(dzb, q2, tn_dims, preferred_element_type=F32)
            dv_ref[pl.ds(off, T), :] += lax.dot_general(w.astype(BF16), do2, tn_dims, preferred_element_type=F32)
            return dq_acc

        def tiles(J, count, state, diag):
            ec, dq_acc = state
            chains = [chain(J + u, diag and u == count - 1) for u in range(count)]
            for u, ch in enumerate(chains):
                dq_acc = finish(ch, ec, dq_acc, diag and u == count - 1)
                ec = ec + ch[6]
            return ec, dq_acc

        lane_row = lax.broadcasted_iota(jnp.int32, (1, LW), 1)
        reached = (jnp.max(carr, axis=0, keepdims=True) > 0.5 * SB_UNSEEN) & (lane_row < HEAD_DIM)
        first = jnp.min(jnp.where(reached, lane_row.astype(F32), float(n_tiles))).astype(jnp.int32)
        U = SB_UNROLL
        count = I - first
        rest = jnp.maximum(count - 1, 0)
        state = (jnp.zeros((NH * T, 1), F32), jnp.zeros((NH * T, LW), F32))
        state = lax.fori_loop(0, rest // U, lambda jj, st: tiles(first + U * jj, U, st, False), state)
        state = lax.fori_loop(0, rest % U, lambda r, st: tiles(I - 1 - rest % U + r, 1, st, False), state)
        _, dq_acc = lax.cond(count > 0, lambda st: tiles(I - 1, 2, st, True), lambda st: tiles(I, 1, st, True), state)
        dq_ref[rows, :] = _sb_unstack_heads(dq_acc) * scale
        return 0

    blk = lambda src_off: pl.BlockSpec((R * T, LW), lambda p, I: (I, src_off + p))
    full = lambda off: pl.BlockSpec((S, LW), lambda p, I: (0, off + p))
    const = lambda rows: pl.BlockSpec((rows, T), lambda p, I: (0, 0))
    dq, dk, dv, *got = pl.pallas_call(
        body, name=name, grid=grid,
        out_shape=[jax.ShapeDtypeStruct((S, W), F32)] * 3 + hosted.out_shapes,
        in_specs=[blk(0), full(npair), full(2 * npair), blk(0), blk(0), const(NH * T), const(T), const(T)]
        + hosted.in_specs,
        out_specs=[blk(0), full(0), full(0)] + hosted.out_specs,
        scratch_shapes=hosted.scratch,
        compiler_params=_cparams(hosted.semantics(("parallel", "arbitrary")), vmem_mb=56),
    )(p_sb, p_sb, p_sb, do, carries, _sb_causal(T, NH).astype(F32), _sb_triangle(T, True), _sb_triangle(T, False),
      *hosted.operands)
    return (dq, dk, dv) if comm is None else (dq, dk, dv, got)


def _dil_blocks(b, body_fn):
    for pi, (window, dil) in enumerate(DILATED_PATTERNS):
        assert window // dil == DIL_BLOCK
        nblk = DIL_SUPER // (DIL_BLOCK * dil)
        assert (dil * nblk) % DIL_UNROLL == 0

        def group(g, _, pi=pi, dil=dil, nblk=nblk):
            for u in range(DIL_UNROLL):
                t = g * DIL_UNROLL + u
                n = t % nblk
                body_fn(pi, dil, t // nblk, n, b * nblk + n)
            return 0

        lax.fori_loop(0, dil * nblk // DIL_UNROLL, group, 0)


def _dil_rows(start, size, dil):
    if dil == 1:
        return pl.ds(pl.multiple_of(start, DIL_BLOCK), size)
    return pl.ds(start, size, stride=dil)


def _dil_fill_bias(bias_ref):
    row = lax.broadcasted_iota(jnp.int32, (2 * DIL_BLOCK, 2 * DIL_BLOCK), 0)
    kk = lax.broadcasted_iota(jnp.int32, (2 * DIL_BLOCK, 2 * DIL_BLOCK), 1)
    qi = jnp.where(row >= DIL_BLOCK, row - DIL_BLOCK, row)
    for s in range(2):
        dist = s * DIL_BLOCK + qi - kk
        bias_ref[s] = jnp.where((dist >= 0) & (dist <= DIL_BLOCK), 0.0, NEG_BIG)


def _dl_fwd(p_dl, *, name):
    S, W = p_dl.shape[0], p_dl.shape[1] // 3
    npair = W // LANES
    nsuper = S // DIL_SUPER
    assert S % DIL_SUPER == 0 and S // max(d for _, d in DILATED_PATTERNS) >= 2 * DIL_BLOCK
    scale = HEAD_DIM ** -0.5
    npat = len(DILATED_PATTERNS)

    def body(q_ref, k_ref, v_ref, o_ref, l_ref, bias_ref, *pattern_refs):
        op_refs, lp_refs = pattern_refs[:npat], pattern_refs[npat:]
        b = pl.program_id(1)
        masks = _head_masks((DIL_BLOCK, LANES))
        pl.when(b == 0)(lambda: _dil_fill_bias(bias_ref))

        def block(pi, dil, c, n, gn):
            ws = jnp.maximum(gn - 1, 0)
            qrows = n * (DIL_BLOCK * dil) + c
            krows = ws * (DIL_BLOCK * dil) + c
            q_idx = _dil_rows(qrows, DIL_BLOCK, dil)
            k_idx = _dil_rows(krows, 2 * DIL_BLOCK, dil)
            qb = q_ref[q_idx, :]
            kb = k_ref[k_idx, :].astype(BF16)
            vb = v_ref[k_idx, :].astype(BF16)
            q2 = _sb_stack_heads(qb.astype(BF16), scale)
            z = lax.dot_general(q2, kb, (((1,), (1,)), ((), ())), preferred_element_type=F32) + bias_ref[gn - ws]
            m = jnp.max(z, axis=1, keepdims=True)
            p = jnp.exp(z - m)
            den = jnp.sum(p, axis=1, keepdims=True)
            acc = jnp.dot(p.astype(BF16), vb, preferred_element_type=F32)
            lse = m + jnp.log(den)
            op_refs[pi][q_idx, :] = _sb_unstack_heads(acc / den)
            lp_refs[pi][q_idx, :] = jnp.where(masks[0], lse[:DIL_BLOCK], lse[DIL_BLOCK:])

        _dil_blocks(b, block)
        lses = [r[...] for r in lp_refs]
        top = functools.reduce(jnp.maximum, lses)
        ws_ = [jnp.exp(l - top) for l in lses]
        den = functools.reduce(jnp.add, ws_)
        num = functools.reduce(jnp.add, [w * r[...] for r, w in zip(op_refs, ws_)])
        o_ref[...] = num / den
        l_ref[...] = top + jnp.log(den)

    blk = pl.BlockSpec((DIL_SUPER, LANES), lambda p, b: (b, p))
    full = lambda off: pl.BlockSpec((S, LANES), lambda p, b: (0, off + p))
    return pl.pallas_call(
        body, name=name, grid=(npair, nsuper),
        out_shape=[jax.ShapeDtypeStruct((S, W), F32)] * 2,
        in_specs=[blk, full(npair), full(2 * npair)], out_specs=[blk, blk],
        scratch_shapes=[pltpu.VMEM((2, 2 * DIL_BLOCK, 2 * DIL_BLOCK), F32)]
        + [pltpu.VMEM((DIL_SUPER, LANES), F32)] * (2 * npat),
        compiler_params=_cparams(("arbitrary", "arbitrary")),
    )(p_dl, p_dl, p_dl)


def _dl_bwd(p_dl, o, lse, do, *, name):
    S, W = p_dl.shape[0], p_dl.shape[1] // 3
    npair = W // LANES
    nsuper = S // DIL_SUPER
    scale = HEAD_DIM ** -0.5

    def body(q_ref, k_ref, v_ref, o_ref, l_ref, do_ref, dq_ref, dk_ref, dv_ref, delta_ref, bias_ref):
        b = pl.program_id(1)

        @pl.when(b == 0)
        def _():
            dk_ref[...] = jnp.zeros_like(dk_ref)
            dv_ref[...] = jnp.zeros_like(dv_ref)
            _dil_fill_bias(bias_ref)

        dq_ref[...] = jnp.zeros_like(dq_ref)
        prod = do_ref[...] * o_ref[...]
        delta = jnp.zeros_like(prod)
        for hm in _head_masks(prod.shape):
            delta = jnp.where(hm, jnp.sum(jnp.where(hm, prod, 0.0), axis=1, keepdims=True), delta)
        delta_ref[...] = delta

        def block(pi, dil, c, n, gn):
            ws = jnp.maximum(gn - 1, 0)
            qrows = n * (DIL_BLOCK * dil) + c
            krows = ws * (DIL_BLOCK * dil) + c
            q_idx = _dil_rows(qrows, DIL_BLOCK, dil)
            k_idx = _dil_rows(krows, 2 * DIL_BLOCK, dil)
            qb = q_ref[q_idx, :]
            dob = do_ref[q_idx, :]
            lb = l_ref[q_idx, :]
            db = delta_ref[q_idx, :]
            kb = k_ref[k_idx, :].astype(BF16)
            vb = v_ref[k_idx, :].astype(BF16)
            q2 = _sb_stack_heads(qb.astype(BF16), scale)
            do2 = _sb_stack_heads(dob.astype(BF16))
            lse2 = jnp.concatenate([lb[:, HEAD_DIM * h:HEAD_DIM * h + 1] for h in range(2)], axis=0)
            delta2 = jnp.concatenate([db[:, HEAD_DIM * h:HEAD_DIM * h + 1] for h in range(2)], axis=0)
            z = lax.dot_general(q2, kb, (((1,), (1,)), ((), ())), preferred_element_type=F32)
            p = jnp.exp((z + bias_ref[gn - ws]) - lse2)
            dp = lax.dot_general(do2, vb, (((1,), (1,)), ((), ())), preferred_element_type=F32)
            dzb = (p * (dp - delta2)).astype(BF16)
            tn_dims = (((0,), (0,)), ((), ()))
            dq_blk = _sb_unstack_heads(jnp.dot(dzb, kb, preferred_element_type=F32)) * scale
            dk_blk = lax.dot_general(dzb, q2, tn_dims, preferred_element_type=F32)
            dv_blk = lax.dot_general(p.astype(BF16), do2, tn_dims, preferred_element_type=F32)
            dq_ref[q_idx, :] = dq_ref[q_idx, :] + dq_blk
            dk_ref[k_idx, :] = dk_ref[k_idx, :] + dk_blk
            dv_ref[k_idx, :] = dv_ref[k_idx, :] + dv_blk

        _dil_blocks(b, block)

    blk = pl.BlockSpec((DIL_SUPER, LANES), lambda p, b: (b, p))
    full = lambda off: pl.BlockSpec((S, LANES), lambda p, b: (0, off + p))
    return pl.pallas_call(
        body, name=name, grid=(npair, nsuper),
        out_shape=[jax.ShapeDtypeStruct((S, W), F32)] * 3,
        in_specs=[blk, full(npair), full(2 * npair), blk, blk, blk], out_specs=[blk, full(0), full(0)],
        scratch_shapes=[pltpu.VMEM((DIL_SUPER, LANES), F32), pltpu.VMEM((2, 2 * DIL_BLOCK, 2 * DIL_BLOCK), F32)],
        compiler_params=_cparams(("arbitrary", "arbitrary")),
    )(p_dl, p_dl, p_dl, o, lse, do)


class _NoExchange:
    def gather(self, family):
        return None

    def gathered(self, family, got, weights):
        pass

    def send(self, family, grads):
        return None

    def received(self, family, got):
        pass


def _local_step(x, target, gains, weights, exchanges=None):
    S, D = x.shape
    ex = exchanges or _NoExchange()
    weights = dict(weights)
    d_sb = gains["sb_out_norm"].shape[1]
    d_dl = gains["dil_out_norm"].shape[1]
    cos_t, sin_t = _rope_tables(S)

    riders = ("ffn1_w_up", "ffn1_w_down", "mixer") if exchanges else (None, None, None)
    (x1, h2), saved1 = _ffn_fwd(x, gains["ffn1_norm"], weights, tag="ffn1", ex=ex, riders=riders,
                                first_rider="ffn1_w_gate" if exchanges else None,
                                next_gain=gains["mix_norm"])
    w_in = weights["w_in"]
    w_out = weights["w_out"]
    p_sb = _mm(h2, w_in, tb=True, b_cols=(0, 3 * d_sb), outs=(BF16,), name="proj_sb")

    def rope_qk(acc, _, cos, sin):
        return jnp.concatenate([_rotate(acc[:, :2 * d_dl], cos, sin, 1.0), acc[:, 2 * d_dl:]], axis=1)

    p_dl = _mm(h2, w_in, tb=True, b_cols=(3 * d_sb, 3 * d_dl), lanes=(cos_t, sin_t), epilogue=rope_qk,
               name="proj_dl_rope")
    plan = ex.gather("ffn2" if exchanges else None)
    o_sb, carries, *got = _sb_fwd(p_sb, name="sb_fwd", comm=plan)
    ex.gathered("ffn2", got[0] if got else None, weights)
    o_dl, lse_dl = _dl_fwd(p_dl, name="dl_fwd")
    merged = _rms_fwd([o_sb, o_dl], [gains["sb_out_norm"], gains["dil_out_norm"]], name="out_norm")
    x2, h3 = _mm(merged, w_out, res=x1, rows=(gains["ffn2_norm"],), outs=(F32, BF16),
                 epilogue=_residual_then_norm(1.0), name="out_proj_norm")
    (dx3, d_final, loss_wide), saved2 = _ffn_fwd(x2, gains["ffn2_norm"], weights, tag="ffn2", ex=ex, h=h3,
                                                 head=(gains["final_norm"], target))
    loss_row = loss_wide[:, :LANES]

    dx2, d_ffn2_norm, dwg2, dwu2, dwd2 = _ffn_bwd(dx3, x2, gains["ffn2_norm"], weights, saved2, tag="ffn2", ex=ex)
    d_w_out = _mm(merged, dx2, ta=True, outs=(GRAD_WIRE,), name="d_w_out")
    do_sb, d_sb_norm = _mm(dx2, w_out, tb=True, b_cols=(0, d_sb), extras=(o_sb,), rows=(gains["sb_out_norm"],),
                           row_sums=1, epilogue=_rms_bwd_epilogue, name="d_merged_sb")
    do_dl, d_dl_norm = _mm(dx2, w_out, tb=True, b_cols=(d_sb, d_dl), extras=(o_dl,), rows=(gains["dil_out_norm"],),
                           row_sums=1, epilogue=_rms_bwd_epilogue, name="d_merged_dl")
    plan = ex.send("ffn2", dict(ffn2_w_gate=dwg2, ffn2_w_up=dwu2, ffn2_w_down=dwd2))
    dq_sb, dk_sb, dv_sb, *got = _sb_bwd(p_sb, do_sb, carries, name="sb_bwd", comm=plan)
    ex.received("ffn2", got[0] if got else None)
    dq_dl, dk_dl, dv_dl = _dl_bwd(p_dl, o_dl, lse_dl, do_dl, name="dl_bwd")
    d_proj = _join_d_proj([dq_sb, dk_sb, dv_sb, dq_dl, dk_dl, dv_dl], (3, 4), cos_t, sin_t, name="d_proj")
    d_w_in = _mm(h2, d_proj, ta=True, outs=(GRAD_WIRE,), name="d_w_in")
    dx1, d_mix_norm = _mm(d_proj, w_in, extras=(x1, dx2), rows=(gains["mix_norm"],), row_sums=1,
                          epilogue=_rms_bwd_epilogue, name="dh_mix_norm_bwd")
    dx, d_ffn1_norm, dwg1, dwu1, dwd1 = _ffn_bwd(
        dx1, x, gains["ffn1_norm"], weights, saved1, tag="ffn1", ex=ex,
        rider=("mixer", dict(w_in=d_w_in, w_out=d_w_out)), spread=True)
    gain_grads = dict(ffn1_norm=d_ffn1_norm, mix_norm=d_mix_norm, sb_out_norm=d_sb_norm, dil_out_norm=d_dl_norm,
                      ffn2_norm=d_ffn2_norm, final_norm=d_final)
    weight_grads = dict(ffn1_w_gate=dwg1, ffn1_w_up=dwu1, ffn1_w_down=dwd1, w_in=d_w_in, w_out=d_w_out,
                        ffn2_w_gate=dwg2, ffn2_w_up=dwu2, ffn2_w_down=dwd2)
    return loss_row, dx, gain_grads, weight_grads


def _mesh_position():
    return lax.axis_index("x"), lax.axis_index("y"), lax.axis_index("c")


def _flip(coord, bit):
    return 1 - coord if bit else coord


RELATIONS = [(rx, ry, rc) for rx in (0, 1) for ry in (0, 1) for rc in (0, 1)][1:]


class _GatherPlan:
    def __init__(self, shards):
        n = len(shards)
        self.operands = list(shards)
        self.out_shapes = [jax.ShapeDtypeStruct((N_DEV,) + s.shape, s.dtype) for s in shards]
        self.scratch = [pltpu.SemaphoreType.DMA((n, 7)), pltpu.SemaphoreType.DMA((n, 7)),
                        pltpu.SemaphoreType.DMA((n,))]

    def _copies(self, in_refs, out_refs, sems):
        send_sems, recv_sems, local_sems = sems
        x, y, c = _mesh_position()
        me, sibling = (x, y, c), (x, y, 1 - c)
        chips = [(1 - x, y), (x, 1 - y), (1 - x, 1 - y)]
        plans = []
        for t, (x_ref, out_ref) in enumerate(zip(in_refs, out_refs)):
            def slot(px, py, pc, out_ref=out_ref):
                return out_ref.at[4 * px + 2 * py + pc]

            def copy(k, block, to, src=None, t=t, slot=slot):
                return pltpu.make_async_remote_copy(
                    src_ref=slot(*block) if src is None else src, dst_ref=slot(*block),
                    send_sem=send_sems.at[t, k], recv_sem=recv_sems.at[t, k],
                    device_id=to, device_id_type=pl.DeviceIdType.MESH)

            plans.append(dict(
                mine=pltpu.make_async_copy(x_ref, slot(*me), local_sems.at[t]),
                first=[copy(0, me, sibling, src=x_ref)]
                + [copy(1 + j, me, (*chip, c), src=x_ref) for j, chip in enumerate(chips)],
                over_ici=[copy(1 + j, (*chip, c), me) for j, chip in enumerate(chips)],
                passed=[copy(4 + j, (*chip, c), sibling) for j, chip in enumerate(chips)],
                from_sibling=[copy(0, sibling, me)] + [copy(4 + j, (*chip, 1 - c), me) for j, chip in enumerate(chips)]))
        return plans

    def start(self, in_refs, out_refs, sems):
        for p in self._copies(in_refs, out_refs, sems):
            p["mine"].start()
            for cp in p["first"]:
                cp.start()

    def finish(self, in_refs, out_refs, sems):
        plans = self._copies(in_refs, out_refs, sems)
        for p in plans:
            for arrived, onward in zip(p["over_ici"], p["passed"]):
                arrived.wait_recv()
                onward.start()
        for p in plans:
            for cp in p["from_sibling"]:
                cp.wait_recv()
            for cp in p["first"] + p["passed"]:
                cp.wait_send()
            p["mine"].wait()


class _Hosted:
    def __init__(self, plan, n_in, n_out, n_scratch):
        self.plan, self.n_in, self.n_out, self.n_scratch = plan, n_in, n_out, n_scratch
        self.operands = list(plan.operands) if plan else []
        self.out_shapes = list(plan.out_shapes) if plan else []
        self.scratch = list(plan.scratch) if plan else []
        self.in_specs = [pl.BlockSpec(memory_space=pl.ANY)] * len(self.operands)
        self.out_specs = [pl.BlockSpec(memory_space=pl.ANY)] * len(self.out_shapes)

    def semantics(self, sem):
        return sem if self.plan is None else ("arbitrary",) * len(sem)

    def _at(self, grid, last):
        hit = None
        for d, n in enumerate(grid):
            here = pl.program_id(d) == (n - 1 if last else 0)
            hit = here if hit is None else hit & here
        return hit

    def begin(self, refs, grid):
        if self.plan is None:
            return refs
        k_in, k_out = len(self.operands), len(self.out_shapes)
        ins, rest = refs[:self.n_in], refs[self.n_in:]
        c_in, rest = rest[:k_in], rest[k_in:]
        outs, rest = rest[:self.n_out], rest[self.n_out:]
        c_out, rest = rest[:k_out], rest[k_out:]
        scratch, sems = rest[:self.n_scratch], rest[self.n_scratch:]
        self._args = (c_in, c_out, sems)
        pl.when(self._at(grid, False))(lambda: self.plan.start(*self._args))
        return tuple(ins) + tuple(outs) + tuple(scratch)

    def end(self, grid):
        if self.plan is not None:
            pl.when(self._at(grid, True))(lambda: self.plan.finish(*self._args))


class _ExchangePlan:
    def __init__(self, packs):
        n = len(packs)
        self.operands = list(packs)
        self.out_shapes = [jax.ShapeDtypeStruct(p.shape, p.dtype) for p in packs]
        self.scratch = [pltpu.SemaphoreType.DMA((n, 7)), pltpu.SemaphoreType.DMA((n, 7)),
                        pltpu.SemaphoreType.DMA((n,))]

    def _copies(self, in_refs, out_refs, sems):
        send_sems, recv_sems, local_sems = sems
        x, y, c = _mesh_position()
        me = 4 * x + 2 * y + c
        copies = [pltpu.make_async_copy(i.at[me], o.at[me], local_sems.at[t])
                  for t, (i, o) in enumerate(zip(in_refs, out_refs))]
        for r, (rx, ry, rc) in enumerate(RELATIONS):
            px, py, pc = _flip(x, rx), _flip(y, ry), _flip(c, rc)
            peer = 4 * px + 2 * py + pc
            copies += [pltpu.make_async_remote_copy(
                src_ref=i.at[peer], dst_ref=o.at[me], send_sem=send_sems.at[t, r], recv_sem=recv_sems.at[t, r],
                device_id=(px, py, pc), device_id_type=pl.DeviceIdType.MESH)
                for t, (i, o) in enumerate(zip(in_refs, out_refs))]
        return copies

    def start(self, in_refs, out_refs, sems):
        for cp in self._copies(in_refs, out_refs, sems):
            cp.start()

    def finish(self, in_refs, out_refs, sems):
        for cp in self._copies(in_refs, out_refs, sems):
            cp.wait()


def _all_reduce_rows(v, *, name):
    R, C = v.shape

    def body(v_ref, out_ref, buf, send_sems, recv_sems):
        x, y, c = _mesh_position()
        me = 4 * x + 2 * y + c
        buf[me] = v_ref[...]
        copies = []
        for r, (rx, ry, rc) in enumerate(RELATIONS):
            cp = pltpu.make_async_remote_copy(
                src_ref=v_ref, dst_ref=buf.at[me], send_sem=send_sems.at[r], recv_sem=recv_sems.at[r],
                device_id=(_flip(x, rx), _flip(y, ry), _flip(c, rc)), device_id_type=pl.DeviceIdType.MESH)
            cp.start()
            copies.append(cp)
        for cp in copies:
            cp.wait()
        total = buf[0]
        for s in range(1, N_DEV):
            total = total + buf[s]
        out_ref[...] = total

    return pl.pallas_call(
        body, name=name,
        out_shape=jax.ShapeDtypeStruct((R, C), F32),
        in_specs=[pl.BlockSpec(memory_space=pltpu.VMEM)],
        out_specs=pl.BlockSpec(memory_space=pltpu.VMEM),
        scratch_shapes=[pltpu.VMEM((N_DEV, R, C), F32), pltpu.SemaphoreType.DMA((7,)), pltpu.SemaphoreType.DMA((7,))],
    )(v)


def _adamw(w, g, m, v, *, name):
    R, C = w.shape
    slots = g.ndim == 3
    tr = _pick(R, (256, 128, 64, 32, 16) if slots else (256, 128, 64, 32, 16, 8))

    def body(w_ref, g_ref, m_ref, v_ref, g_out, d_ref, nm_ref, nv_ref):
        if slots:
            g = g_ref[0].astype(F32)
            for s in range(1, N_DEV):
                g = g + g_ref[s].astype(F32)
        else:
            g = g_ref[...]
        g_out[...] = g
        m_new = ADAM_B1 * m_ref[...] + (1.0 - ADAM_B1) * g
        v_new = ADAM_B2 * v_ref[...] + (1.0 - ADAM_B2) * (g * g)
        m_hat = m_new / (1.0 - ADAM_B1 ** ADAM_STEP)
        v_hat = v_new / (1.0 - ADAM_B2 ** ADAM_STEP)
        d_ref[...] = -ADAM_LR * (m_hat / (jnp.sqrt(v_hat) + ADAM_EPS) + ADAM_WD * w_ref[...])
        nm_ref[...] = m_new
        nv_ref[...] = v_new

    spec = pl.BlockSpec((tr, C), lambda i: (i, 0))
    g_spec = pl.BlockSpec((N_DEV, tr, C), lambda i: (0, i, 0)) if slots else spec
    return pl.pallas_call(
        body, name=name, grid=(R // tr,),
        out_shape=[jax.ShapeDtypeStruct((R, C), F32)] * 4,
        in_specs=[spec, g_spec, spec, spec], out_specs=[spec] * 4,
        compiler_params=_cparams(("parallel",)),
    )(w, g, m, v)


WEIGHT_NAMES = ["ffn1_norm", "ffn1_w_gate", "ffn1_w_up", "ffn1_w_down", "mix_norm", "w_in", "sb_out_norm",
                "dil_out_norm", "w_out", "ffn2_norm", "ffn2_w_gate", "ffn2_w_up", "ffn2_w_down", "final_norm"]
GAIN_NAMES = ["ffn1_norm", "mix_norm", "sb_out_norm", "dil_out_norm", "ffn2_norm", "final_norm"]
COL_SHARDED = ["ffn1_w_gate", "ffn1_w_up", "ffn2_w_gate", "ffn2_w_up", "w_in"]
ROW_SHARDED = ["ffn1_w_down", "ffn2_w_down", "w_out"]
GROUPS = {"mixer": (["w_in"], ["w_out"]),
          "ffn2": (["ffn2_w_gate", "ffn2_w_up"], ["ffn2_w_down"])}
for _ffn in ("ffn1", "ffn2"):
    GROUPS.update({f"{_ffn}_w_gate": ([f"{_ffn}_w_gate"], []), f"{_ffn}_w_up": ([f"{_ffn}_w_up"], []),
                   f"{_ffn}_w_down": ([], [f"{_ffn}_w_down"])})


class _Exchanges:
    def __init__(self, params):
        self.params = params
        self.grads = {}

    def gather(self, group):
        if group is None:
            return None
        cols, rows = GROUPS[group]
        return _GatherPlan([self.params[n].T.astype(BF16) for n in cols] + [self.params[n].astype(BF16) for n in rows])

    def gathered(self, group, got, weights):
        if group is None:
            return
        cols, rows = GROUPS[group]
        for n, blocks in zip(cols + rows, got):
            weights[n] = blocks.reshape(N_DEV * blocks.shape[1], blocks.shape[2])

    def send(self, group, grads):
        if group is None:
            return None
        cols, rows = GROUPS[group]
        packs = [jnp.transpose(grads[n].reshape(grads[n].shape[0], N_DEV, self.params[n].shape[1]), (1, 0, 2))
                 for n in cols]
        packs += [grads[n].reshape(N_DEV, self.params[n].shape[0], grads[n].shape[1]) for n in rows]
        return _ExchangePlan([p.astype(GRAD_WIRE) for p in packs])

    def received(self, group, got):
        if group is None:
            return
        cols, rows = GROUPS[group]
        for n, slots in zip(cols + rows, got):
            self.grads[n] = slots


def _step(x, target, params, moments_m, moments_v):
    ex = _Exchanges(params)
    weights = {}
    gains = {n: params[n] for n in GAIN_NAMES}
    loss_row, grad_x, gain_grads, _ = _local_step(x, target, gains, weights, ex)
    grads = ex.grads

    rows = [gain_grads[n].reshape(-1, LANES) for n in GAIN_NAMES] + [loss_row]
    small = jnp.concatenate(rows, axis=0)
    pad = (-small.shape[0]) % 8
    small = jnp.pad(small, ((0, pad), (0, 0)))
    small = _all_reduce_rows(small, name="reduce_gains_loss")
    off = 0
    for n in GAIN_NAMES:
        r = gain_grads[n].shape[1] // LANES
        grads[n] = small[off:off + r].reshape(1, -1)
        off += r
    loss = small[off, 0]

    delta, new_m, new_v = {}, {}, {}
    for n in WEIGHT_NAMES:
        grads[n], delta[n], new_m[n], new_v[n] = _adamw(params[n], grads[n], moments_m[n], moments_v[n],
                                                        name=f"adamw_{n}")
    return loss, grad_x, grads, delta, new_m, new_v


def kernel(x, ffn1_norm, ffn1_w_gate, ffn1_w_up, ffn1_w_down, mix_norm, w_in, sb_out_norm, dil_out_norm, w_out, ffn2_norm, ffn2_w_gate, ffn2_w_up, ffn2_w_down, final_norm, loss_target, m_ffn1_norm, m_ffn1_w_gate, m_ffn1_w_up, m_ffn1_w_down, m_mix_norm, m_w_in, m_sb_out_norm, m_dil_out_norm, m_w_out, m_ffn2_norm, m_ffn2_w_gate, m_ffn2_w_up, m_ffn2_w_down, m_final_norm, v_ffn1_norm, v_ffn1_w_gate, v_ffn1_w_up, v_ffn1_w_down, v_mix_norm, v_w_in, v_sb_out_norm, v_dil_out_norm, v_w_out, v_ffn2_norm, v_ffn2_w_gate, v_ffn2_w_up, v_ffn2_w_down, v_final_norm):
    given = dict(locals())
    shapes = {n: given[n].shape for n in WEIGHT_NAMES}

    def as2d(a):
        return a.reshape(1, -1) if a.ndim == 1 else a.reshape(a.shape[-2], a.shape[-1])

    params = {n: as2d(given[n]) for n in WEIGHT_NAMES}
    moments_m = {n: as2d(given["m_" + n]) for n in WEIGHT_NAMES}
    moments_v = {n: as2d(given["v_" + n]) for n in WEIGHT_NAMES}
    loss, grad_x, grads, delta, new_m, new_v = _step(x[0], loss_target[0], params, moments_m, moments_v)
    back = lambda d: [d[n].reshape(shapes[n]) for n in WEIGHT_NAMES]
    return (loss, grad_x[None], *back(grads), *back(delta), *back(new_m), *back(new_v))
```

```python
import functools

import jax
import jax.numpy as jnp
from jax import lax
from jax.experimental import pallas as pl
from jax.experimental.pallas import tpu as pltpu

F32 = jnp.float32
BF16 = jnp.bfloat16
GRAD_WIRE = jnp.bfloat16

N_DEV = 8
HEAD_DIM = 64
LANES = 128
DILATED_PATTERNS = ((128, 1), (512, 4), (2048, 16))
DIL_BLOCK = 128
DIL_SUPER = 2048
DIL_UNROLL = 16
SB_TILE = 256
SB_LANES = 128
SB_UNROLL = 4
SB_STEP_TILES = 2
SB_DEAD = 90.0
SB_UNSEEN = -1e30
ROPE_THETA = 10000.0
RMS_EPS = 1e-6
HALF_STEP = 0.5
ADAM_LR = 0.001
ADAM_B1 = 0.9
ADAM_B2 = 0.999
ADAM_EPS = 1e-08
ADAM_WD = 0.01
ADAM_STEP = 10
NEG_BIG = -1e30
VMEM_CAP_MB = 60


def _pick(n, prefs):
    for p in prefs:
        if n % p == 0:
            return p
    return n


MM_MAX_TILE = 1536
MM_WHOLE = 3072
RING_SLOTS = 3


def _largest_tile(n, cap):
    if n <= cap:
        return n
    for t in range(cap - cap % LANES, 0, -LANES):
        if n % t == 0:
            return t
    return n


def _cparams(sem=None, vmem_mb=48):
    return pltpu.CompilerParams(dimension_semantics=sem, vmem_limit_bytes=min(vmem_mb, VMEM_CAP_MB) * 1024 * 1024)


def _nbytes(shape, dtype):
    n = 1
    for s in shape:
        n *= s
    return n * jnp.dtype(dtype).itemsize


def _mm(a, b, *, name, ta=False, tb=False, outs=(F32,), res=None, alpha=1.0, extras=(), epilogue=None,
        tm=None, tn=None, tk=None, comm=None, rows=(), lanes=(), row_sums=0, b_cols=None, second=None,
        twin_b=None, ring=False):
    if ta:
        K, M = a.shape
    else:
        M, K = a.shape
    if tb:
        N, Kb = b.shape
    else:
        Kb, N = b.shape
    col0 = 0
    if b_cols is not None:
        col0, N = b_cols
    assert K == Kb, (a.shape, b.shape, ta, tb)
    tn = tn or (N if (not ta and K <= MM_WHOLE and N <= MM_WHOLE) else _largest_tile(N, MM_MAX_TILE))
    wide = tn > MM_MAX_TILE and (len(extras) + len(outs) > 3 or a.dtype == F32)
    tm = tm or (_largest_tile(M, MM_MAX_TILE) if ta else _pick(M, (256, 128) if wide else (512, 256, 128)))
    tk = tk or (K if K <= MM_WHOLE else _pick(K, (2048, 1024, 512, 256, 128)))
    nk = K // tk
    a_spec = pl.BlockSpec((tk, tm), lambda i, j, k: (k, i)) if ta else pl.BlockSpec((tm, tk), lambda i, j, k: (i, k))
    assert col0 % tn == 0
    b_spec = (pl.BlockSpec((tn, tk), lambda i, j, k: (j + col0 // tn, k)) if tb
              else pl.BlockSpec((tk, tn), lambda i, j, k: (k, j + col0 // tn)))
    mn_spec = pl.BlockSpec((tm, tn), lambda i, j, k: (i, j))
    dims = (((0 if ta else 1,), (1 if tb else 0,)), ((), ()))
    row_spec = pl.BlockSpec((1, tn), lambda i, j, k: (0, j))
    lane_spec = pl.BlockSpec((tm, LANES), lambda i, j, k: (i, 0))
    n_extra = len(extras) + (1 if res is not None else 0) + len(rows) + len(lanes)
    n_mn = len(outs)
    n_out = n_mn + row_sums
    assert row_sums == 0 or tn == N
    grid = (M // tm, N // tn, nk)
    n_ab = 2 + (2 if second is not None else 0) + (1 if twin_b is not None else 0)
    assert twin_b is None or (nk == 1 and epilogue is not None)
    n_ring = len(extras) if ring else 0
    assert not ring or (grid[1] == 1 and nk == 1 and res is None)
    hosted = _Hosted(comm, n_in=n_ab + n_extra, n_out=n_out, n_scratch=(1 if nk > 1 else 0) + 2 * n_ring)

    def body(*refs):
        in_refs = list(refs[n_ab:n_ab + n_extra])
        ab_refs = refs[:n_ab]
        refs = hosted.begin(refs, grid)
        out_refs = refs[n_ab + n_extra:n_ab + n_extra + n_out]
        ring_blocks = []
        if n_ring:
            scratch = refs[n_ab + n_extra + n_out:]
            bufs, sems = scratch[:n_ring], scratch[n_ring:2 * n_ring]
            step, steps = pl.program_id(0), grid[0]

            def fetch(s):
                return [pltpu.make_async_copy(src.at[pl.ds(pl.multiple_of(s * tm, tm), tm), :], buf.at[s % RING_SLOTS],
                                              sem.at[s % RING_SLOTS])
                        for src, buf, sem in zip(in_refs[:n_ring], bufs, sems)]

            @pl.when(step == 0)
            def _():
                for s in range(min(RING_SLOTS - 1, steps)):
                    for cp in fetch(s):
                        cp.start()

            @pl.when(step + RING_SLOTS - 1 < steps)
            def _():
                for cp in fetch(step + RING_SLOTS - 1):
                    cp.start()

            for cp in fetch(step):
                cp.wait()
            ring_blocks = [buf[step % RING_SLOTS] for buf in bufs]
        def product(a_ref, b_ref):
            return lax.dot_general(a_ref[...].astype(BF16), b_ref[...].astype(BF16), dims, preferred_element_type=F32)

        prod = product(ab_refs[0], ab_refs[1])
        if second is not None:
            prod = prod + product(ab_refs[2], ab_refs[3])
        if twin_b is not None:
            prod = (prod, product(ab_refs[0], ab_refs[-1]))

        def finish(acc):
            blocks = ring_blocks + [r[...] for r in in_refs[n_ring:]]
            if res is not None:
                r_blk, blocks = blocks[0], blocks[1:]
            else:
                r_blk = None
            if epilogue is None:
                val = acc * alpha
                if r_blk is not None:
                    val = val + r_blk
                vals = (val,)
            else:
                vals = epilogue(acc, r_blk, *blocks)
                vals = vals if isinstance(vals, (tuple, list)) else (vals,)
            for o_ref, v in zip(out_refs[:n_mn], vals[:n_mn]):
                o_ref[...] = v.astype(o_ref.dtype)
            first_rows = pl.program_id(0) == 0
            for o_ref, part in zip(out_refs[n_mn:], vals[n_mn:]):
                @pl.when(first_rows)
                def _(o_ref=o_ref, part=part):
                    o_ref[...] = part

                @pl.when(jnp.logical_not(first_rows))
                def _(o_ref=o_ref, part=part):
                    o_ref[...] += part

        if nk == 1:
            finish(prod)
        else:
            acc_ref = refs[n_ab + n_extra + n_out]
            k = pl.program_id(2)

            @pl.when(k == 0)
            def _():
                acc_ref[...] = prod

            @pl.when(k > 0)
            def _():
                acc_ref[...] += prod

            @pl.when(k == nk - 1)
            def _():
                finish(acc_ref[...])

        hosted.end(grid)

    mn_operands = ([res] if res is not None else []) + list(extras)
    ab = [a, b] + (list(second) if second is not None else []) + ([twin_b] if twin_b is not None else [])
    ab_specs = [a_spec, b_spec] * (1 if second is None else 2) + ([b_spec] if twin_b is not None else [])
    operands = ab + mn_operands + list(rows) + list(lanes)
    any_spec = pl.BlockSpec(memory_space=pl.ANY)
    mn_specs = [mn_spec] * (len(mn_operands) - n_ring) + [any_spec] * n_ring
    in_specs = ab_specs + mn_specs + [row_spec] * len(rows) + [lane_spec] * len(lanes)
    est = n_ab * (_nbytes((tm, tk), a.dtype) + _nbytes((tk, tn), b.dtype))
    est += RING_SLOTS * sum(_nbytes((tm, tn), o.dtype) for o in mn_operands)
    est += 2 * sum(_nbytes((tm, tn), d) for d in outs) + 2 * _nbytes((tm, tn), F32)
    semantics = ("parallel", "parallel", "arbitrary") if row_sums == 0 and not ring else ("arbitrary",) * 3
    ring_scratch = [pltpu.VMEM((RING_SLOTS, tm, tn), e.dtype) for e in extras[:n_ring]]
    ring_scratch += [pltpu.SemaphoreType.DMA((RING_SLOTS,))] * n_ring
    result = pl.pallas_call(
        body, name=name, grid=grid,
        out_shape=[jax.ShapeDtypeStruct((M, N), d) for d in outs]
        + [jax.ShapeDtypeStruct((1, N), F32)] * row_sums + hosted.out_shapes,
        in_specs=in_specs + hosted.in_specs,
        out_specs=[mn_spec] * n_mn + [row_spec] * row_sums + hosted.out_specs,
        scratch_shapes=([pltpu.VMEM((tm, tn), F32)] if nk > 1 else []) + ring_scratch + hosted.scratch,
        compiler_params=_cparams(hosted.semantics(semantics), vmem_mb=max(32, 2 * est // (1024 * 1024))),
    )(*operands, *hosted.operands)
    own, got = result[:n_out], list(result[n_out:])
    own = own[0] if n_out == 1 else own
    return own if comm is None else (own, got)


def _rms_hat(x):
    r = lax.rsqrt(jnp.mean(x * x, axis=-1, keepdims=True) + RMS_EPS)
    return x * r, r


def _rms_fwd(xs, gains, *, name, comm=None):
    S = xs[0].shape[0]
    widths = [x.shape[1] for x in xs]
    tm = _pick(S, (512, 256, 128))
    n = len(xs)
    grid = (S // tm,)
    hosted = _Hosted(comm, n_in=2 * n, n_out=1, n_scratch=0)

    def body(*refs):
        refs = hosted.begin(refs, grid)
        o_ref = refs[2 * n]
        off = 0
        for i in range(n):
            xh, _ = _rms_hat(refs[i][...])
            o_ref[:, off:off + widths[i]] = (xh * refs[n + i][...]).astype(o_ref.dtype)
            off += widths[i]
        hosted.end(grid)

    out, *got = pl.pallas_call(
        body, name=name, grid=grid,
        out_shape=[jax.ShapeDtypeStruct((S, sum(widths)), BF16)] + hosted.out_shapes,
        in_specs=[pl.BlockSpec((tm, w), lambda i: (i, 0)) for w in widths]
        + [pl.BlockSpec((1, w), lambda i: (0, 0)) for w in widths] + hosted.in_specs,
        out_specs=[pl.BlockSpec((tm, sum(widths)), lambda i: (i, 0))] + hosted.out_specs,
        scratch_shapes=hosted.scratch,
        compiler_params=_cparams(hosted.semantics(("parallel",))),
    )(*xs, *gains, *hosted.operands)
    return out if comm is None else (out, got)


def _sigmoid(g):
    return 1.0 / (1.0 + jnp.exp(-g))


def _ride(result, plan):
    return result if plan is not None else (result, None)


def _residual_then_norm(alpha):
    def epilogue(acc, res, gain):
        y = res + alpha * acc
        return y, _rms_hat(y)[0] * gain
    return epilogue


def _ffn_fwd(x, gain, w, *, tag, ex, first_rider=None, riders=(None, None, None), head=None, h=None,
             next_gain=None):
    if h is None:
        plan = ex.gather(first_rider)
        h, got = _ride(_rms_fwd([x], [gain], name=f"{tag}_norm", comm=plan), plan)
        ex.gathered(first_rider, got, w)
    def act(acc, _, g_blk):
        gf = g_blk.astype(F32)
        return acc, gf * _sigmoid(gf) * acc

    if riders[0] is None and riders[1] is None and f"{tag}_w_up" in w:
        def gate_up_act(accs, _):
            g_blk = accs[0].astype(BF16)
            return (g_blk,) + act(accs[1], None, g_blk)

        g, u, a = _mm(h, w[f"{tag}_w_gate"], tb=True, twin_b=w[f"{tag}_w_up"], outs=(BF16, BF16, BF16),
                      epilogue=gate_up_act, tm=256, name=f"{tag}_gate_up_act")
    else:
        plan = ex.gather(riders[0])
        g, got = _ride(_mm(h, w[f"{tag}_w_gate"], tb=True, outs=(BF16,), name=f"{tag}_gate", comm=plan), plan)
        ex.gathered(riders[0], got, w)
        plan = ex.gather(riders[1])
        (u, a), got = _ride(_mm(h, w[f"{tag}_w_up"], tb=True, outs=(BF16, BF16), extras=(g,), epilogue=act,
                                ring=True, name=f"{tag}_up_act", comm=plan), plan)
        ex.gathered(riders[1], got, w)
    plan = ex.gather(riders[2])
    if head is None and next_gain is None:
        y, got = _ride(_mm(a, w[f"{tag}_w_down"], res=x, alpha=HALF_STEP, name=f"{tag}_down", comm=plan), plan)
    elif head is None:
        y, got = _ride(_mm(a, w[f"{tag}_w_down"], res=x, rows=(next_gain,), outs=(F32, BF16),
                           epilogue=_residual_then_norm(HALF_STEP), name=f"{tag}_down_norm", comm=plan), plan)
    else:
        final_gain, target = head
        y, got = _ride(_mm(a, w[f"{tag}_w_down"], res=x, extras=(target,), rows=(final_gain,), row_sums=2,
                           epilogue=_loss_head_epilogue, name=f"{tag}_down_loss", comm=plan), plan)
    ex.gathered(riders[2], got, w)
    return y, (h, g, u, a)


def _loss_head_epilogue(acc, x_in, target, gain):
    xh, r = _rms_hat(x_in + HALF_STEP * acc)
    err = xh * gain - target
    dy = err * (1.0 / acc.shape[1])
    dxh = dy * gain
    dx = r * (dxh - xh * jnp.mean(dxh * xh, axis=-1, keepdims=True))
    loss = 0.5 * jnp.sum(jnp.mean(err * err, axis=-1, keepdims=True), axis=0, keepdims=True)
    return dx, jnp.sum(dy * xh, axis=0, keepdims=True), jnp.zeros_like(gain) + loss


def _rms_bwd_epilogue(acc, dh_so_far, x, *dres_and_gain):
    gain = dres_and_gain[-1]
    dh = acc if dh_so_far is None else acc + dh_so_far
    xh, r = _rms_hat(x)
    dxh = dh * gain
    dx = r * (dxh - xh * jnp.mean(dxh * xh, axis=-1, keepdims=True))
    if len(dres_and_gain) == 2:
        dx = dx + dres_and_gain[0]
    return dx, jnp.sum(dh * xh, axis=0, keepdims=True)


def _ffn_bwd(dout, x, gain, w, saved, *, tag, ex, rider=(None, None), spread=False):
    h, g, u, a = saved
    wg, wu, wd = (w[f"{tag}_w_{n}"] for n in ("gate", "up", "down"))

    def act_bwd(acc, _, g_blk, u_blk):
        gf, uf = g_blk.astype(F32), u_blk.astype(F32)
        da = acc * HALF_STEP
        sig = _sigmoid(gf)
        silu = gf * sig
        return da * uf * (sig + silu * (1.0 - sig)), da * silu

    def carrying(group, grad, call):
        group = group if spread else None
        plan = ex.send(group, {group: grad})
        out, got = _ride(call(plan), plan)
        ex.received(group, got)
        return out

    plan = ex.send(*rider)
    (dg, du), got = _ride(_mm(dout, wd, tb=True, outs=(BF16, BF16), extras=(g, u), epilogue=act_bwd,
                              ring=True, name=f"{tag}_bwd_act", comm=plan), plan)
    ex.received(rider[0], got)
    dwg = _mm(h, dg, ta=True, outs=(GRAD_WIRE,), name=f"{tag}_dwg")
    dwu = carrying(f"{tag}_w_gate", dwg, lambda plan: _mm(h, du, ta=True, outs=(GRAD_WIRE,), name=f"{tag}_dwu", comm=plan))
    dwd = carrying(f"{tag}_w_up", dwu,
                   lambda plan: _mm(a, dout, ta=True, outs=(GRAD_WIRE,), alpha=HALF_STEP, name=f"{tag}_dwd", comm=plan))
    dx, dgain = carrying(f"{tag}_w_down", dwd, lambda plan: _mm(
        dg, wg, second=(du, wu), extras=(x, dout), rows=(gain,), row_sums=1, epilogue=_rms_bwd_epilogue,
        tm=512, ring=True, name=f"{tag}_dh_norm_bwd", comm=plan))
    return dx, dgain, dwg, dwu, dwd


def _rope_tables(S):
    half = HEAD_DIM // 2
    inv_freq = ROPE_THETA ** (-jnp.arange(half, dtype=F32) / half)
    ang = jnp.arange(S, dtype=F32)[:, None] * inv_freq[None, :]
    cos, sin = jnp.cos(ang), jnp.sin(ang)
    reps = LANES // HEAD_DIM
    cos_t = jnp.tile(jnp.concatenate([cos, cos], axis=1), (1, reps))
    sin_t = jnp.tile(jnp.concatenate([-sin, sin], axis=1), (1, reps))
    return cos_t, sin_t


def _rotate(v, cos, sin, sign):
    half = HEAD_DIM // 2
    groups = []
    for g in range(v.shape[1] // LANES):
        t = v[:, g * LANES:(g + 1) * LANES]
        lane = lax.broadcasted_iota(jnp.int32, t.shape, 1)
        swapped = jnp.where(lane % HEAD_DIM < half, pltpu.roll(t, LANES - half, axis=1), pltpu.roll(t, half, axis=1))
        groups.append(t * cos + swapped * (sin * sign))
    return groups[0] if len(groups) == 1 else jnp.concatenate(groups, axis=1)


def _join_d_proj(pieces, rotated, cos_t, sin_t, *, name):
    S = pieces[0].shape[0]
    widths = [p.shape[1] for p in pieces]
    tm = _pick(S, (256, 128))
    n = len(pieces)

    def body(*refs):
        c_ref, s_ref, o_ref = refs[n], refs[n + 1], refs[n + 2]
        off = 0
        for i in range(n):
            v = refs[i][...]
            if i in rotated:
                v = _rotate(v, c_ref[...], s_ref[...], -1.0)
            o_ref[:, off:off + widths[i]] = v.astype(o_ref.dtype)
            off += widths[i]

    return pl.pallas_call(
        body, name=name, grid=(S // tm,),
        out_shape=jax.ShapeDtypeStruct((S, sum(widths)), BF16),
        in_specs=[pl.BlockSpec((tm, w), lambda i: (i, 0)) for w in widths]
        + [pl.BlockSpec((tm, LANES), lambda i: (i, 0))] * 2,
        out_specs=pl.BlockSpec((tm, sum(widths)), lambda i: (i, 0)),
        compiler_params=_cparams(("parallel",)),
    )(*pieces, cos_t, sin_t)


def _head_masks(shape):
    lane = lax.broadcasted_iota(jnp.int32, shape, 1)
    return [(lane >= HEAD_DIM * h) & (lane < HEAD_DIM * (h + 1)) for h in range(shape[1] // HEAD_DIM)]


def _sb_scores(q2, k_j):
    z = lax.dot_general(q2, k_j, (((1,), (1,)), ((), ())), preferred_element_type=F32)
    sign_bit = jnp.int32(-2 ** 31)
    minus_abs = lax.bitcast_convert_type(lax.bitcast_convert_type(z, jnp.int32) | sign_bit, F32)
    softplus = jnp.maximum(z, 0.0) + jnp.log(1.0 + jnp.exp(minus_abs))
    return z - softplus, softplus


def _sb_stack_heads(t, scale=None):
    parts = [jnp.where(hm, t, jnp.zeros_like(t)) for hm in _head_masks(t.shape)]
    t2 = jnp.concatenate(parts, axis=0)
    if scale is not None:
        t2 = (t2.astype(F32) * scale).astype(t2.dtype)
    return t2


def _sb_unstack_heads(t2):
    n = t2.shape[1] // HEAD_DIM
    T = t2.shape[0] // n
    masks = _head_masks((T, t2.shape[1]))
    out = t2[:T]
    for h in range(1, n):
        out = jnp.where(masks[h], t2[h * T:(h + 1) * T], out)
    return out


def _sb_causal(T, n_heads):
    row = lax.broadcasted_iota(jnp.int32, (n_heads * T, T), 0)
    col = lax.broadcasted_iota(jnp.int32, (n_heads * T, T), 1)
    return col < row % T


def _sb_triangle(T, later):
    row = lax.broadcasted_iota(jnp.int32, (T, T), 0)
    col = lax.broadcasted_iota(jnp.int32, (T, T), 1)
    return ((row > col) if later else (row < col)).astype(BF16)


def _sb_fwd(p_sb, *, name, comm=None):
    S = p_sb.shape[0]
    W = p_sb.shape[1] // 3
    LW = min(SB_LANES, W)
    NH = LW // HEAD_DIM
    npair = W // LW
    T = SB_TILE
    n_tiles = S // T
    assert n_tiles <= HEAD_DIM
    scale = HEAD_DIM ** -0.5

    R = SB_STEP_TILES
    grid = (npair, n_tiles // R)
    hosted = _Hosted(comm, n_in=5, n_out=2, n_scratch=0)

    def body(*refs):
        refs = hosted.begin(refs, grid)
        step = pl.program_id(1)
        lax.fori_loop(0, R, lambda sub, _: query_tile(step * R + sub, sub, *refs), 0)
        hosted.end(grid)

    def query_tile(I, sub, q_ref, k_ref, v_ref, causal_ref, later_ref, o_ref, c_ref):
        rows = pl.ds(pl.multiple_of(sub * T, T), T)
        lane = lax.broadcasted_iota(jnp.int32, (T, LW), 1)
        causal = causal_ref[...]
        later_than = later_ref[...]
        q2 = _sb_stack_heads(q_ref[rows, :], scale)

        def scores(J, diag):
            off = pl.multiple_of(J * T, T)
            log_beta, stay = _sb_scores(q2, k_ref[pl.ds(off, T), :])
            if diag:
                stay = stay * causal
            local = jnp.dot(stay.astype(BF16), later_than, preferred_element_type=F32)
            return log_beta, local, jnp.sum(stay, axis=1, keepdims=True), v_ref[pl.ds(off, T), :]

        def weigh(J, sc, gone, acc, carr, diag):
            log_beta, local, _, v_j = sc
            w = jnp.exp((log_beta - gone) - local)
            if diag:
                w = w * causal
            acc = acc + jnp.dot(w.astype(BF16), v_j, preferred_element_type=F32)
            for h in range(NH):
                carr = jnp.where(lane == HEAD_DIM * h + J, -gone[h * T:(h + 1) * T], carr)
            return acc, carr

        def tiles(J, count, state, diag):
            gone, acc, carr, _ = state
            scs = [scores(J - u, diag and u == 0) for u in range(count)]
            for u, sc in enumerate(scs):
                acc, carr = weigh(J - u, sc, gone, acc, carr, diag and u == 0)
                gone = gone + sc[2]
            return gone, acc, carr, jnp.min(gone)

        U = SB_UNROLL
        alive = lambda st: st[3] < SB_DEAD
        state = (jnp.zeros((NH * T, 1), F32), jnp.zeros((NH * T, LW), F32),
                 jnp.full((T, LW), SB_UNSEEN, F32), jnp.zeros((), F32))
        state = lax.cond(I > 0, lambda st: tiles(I, 2, st, True), lambda st: tiles(I, 1, st, True), state)
        rest = jnp.maximum(I - 1, 0)
        singles = jnp.where(rest > 0, (rest - 1) % U + 1, 0)
        _, state = lax.while_loop(lambda c: (c[0] < singles) & alive(c[1]),
                                  lambda c: (c[0] + 1, tiles(I - 2 - c[0], 1, c[1], False)), (jnp.int32(0), state))
        blocks = (rest - singles) // U
        _, state = lax.while_loop(lambda c: (c[0] < blocks) & alive(c[1]),
                                  lambda c: (c[0] + 1, tiles(I - 2 - singles - U * c[0], U, c[1], False)),
                                  (jnp.int32(0), state))
        _, acc, carr, _ = state
        o_ref[rows, :] = _sb_unstack_heads(acc)
        c_ref[rows, :] = carr
        return 0

    blk = lambda I_off: pl.BlockSpec((R * T, LW), lambda p, I: (I, I_off + p))
    full = lambda off: pl.BlockSpec((S, LW), lambda p, I: (0, off + p))
    const = lambda rows: pl.BlockSpec((rows, T), lambda p, I: (0, 0))
    o, carries, *got = pl.pallas_call(
        body, name=name, grid=grid,
        out_shape=[jax.ShapeDtypeStruct((S, W), F32), jax.ShapeDtypeStruct((S, W), F32)] + hosted.out_shapes,
        in_specs=[blk(0), full(npair), full(2 * npair), const(NH * T), const(T)] + hosted.in_specs,
        out_specs=[blk(0), blk(0)] + hosted.out_specs,
        scratch_shapes=hosted.scratch,
        compiler_params=_cparams(hosted.semantics(("parallel", "arbitrary")), vmem_mb=56),
    )(p_sb, p_sb, p_sb, _sb_causal(T, NH).astype(F32), _sb_triangle(T, True), *hosted.operands)
    return (o, carries) if comm is None else (o, carries, got)


def _sb_bwd(p_sb, do, carries, *, name, comm=None):
    S = p_sb.shape[0]
    W = p_sb.shape[1] // 3
    LW = min(LANES, W)
    NH = LW // HEAD_DIM
    npair = W // LW
    T = SB_TILE
    n_tiles = S // T
    scale = HEAD_DIM ** -0.5

    R = SB_STEP_TILES
    grid = (npair, n_tiles // R)
    hosted = _Hosted(comm, n_in=8, n_out=3, n_scratch=0)

    def body(*refs):
        refs = hosted.begin(refs, grid)
        dk_ref, dv_ref = refs[9], refs[10]
        step = pl.program_id(1)

        @pl.when(step == 0)
        def _():
            dk_ref[...] = jnp.zeros_like(dk_ref)
            dv_ref[...] = jnp.zeros_like(dv_ref)

        lax.fori_loop(0, R, lambda sub, _: query_tile(step * R + sub, sub, *refs), 0)
        hosted.end(grid)

    def query_tile(I, sub, q_ref, k_ref, v_ref, do_ref, c_ref, causal_ref, later_ref, earlier_ref,
                   dq_ref, dk_ref, dv_ref):
        rows = pl.ds(pl.multiple_of(sub * T, T), T)
        lane = lax.broadcasted_iota(jnp.int32, (T, LW), 1)
        causal = causal_ref[...]
        later_than = later_ref[...]
        earlier_than = earlier_ref[...]
        q2 = _sb_stack_heads(q_ref[rows, :], scale)
        do2 = _sb_stack_heads(do_ref[rows, :].astype(BF16))
        carr = c_ref[rows, :]
        tn_dims = (((0,), (0,)), ((), ()))

        def chain(J, diag):
            off = pl.multiple_of(J * T, T)
            k_j = k_ref[pl.ds(off, T), :]
            v_j = v_ref[pl.ds(off, T), :]
            log_beta, stay = _sb_scores(q2, k_j)
            if diag:
                stay = stay * causal
            lc = jnp.concatenate(
                [jnp.sum(jnp.where(lane == HEAD_DIM * h + J, carr, 0.0), axis=1, keepdims=True) for h in range(NH)],
                axis=0)
            w = jnp.exp((log_beta + lc) - jnp.dot(stay.astype(BF16), later_than, preferred_element_type=F32))
            if diag:
                w = w * causal
            dw = lax.dot_general(do2, v_j, (((1,), (1,)), ((), ())), preferred_element_type=F32)
            e = w * dw
            local = jnp.dot(e.astype(BF16), earlier_than, preferred_element_type=F32)
            return off, k_j, w, e, local, jnp.exp(log_beta), jnp.sum(e, axis=1, keepdims=True)

        def finish(ch, ec, dq_acc, diag):
            off, k_j, w, e, local, beta, _ = ch
            e_before = local + ec
            dz = e - beta * (e + e_before)
            if diag:
                dz = dz * causal
            dzb = dz.astype(BF16)
            dq_acc = dq_acc + jnp.dot(dzb, k_j, preferred_element_type=F32)
            dk_ref[pl.ds(off, T), :] += lax.dot_general(dzb, q2, tn_dims, preferred_element_type=F32)
            dv_ref[pl.ds(off, T), :] += lax.dot_general(w.astype(BF16), do2, tn_dims, preferred_element_type=F32)
            return dq_acc

        def tiles(J, count, state, diag):
            ec, dq_acc = state
            chains = [chain(J + u, diag and u == count - 1) for u in range(count)]
            for u, ch in enumerate(chains):
                dq_acc = finish(ch, ec, dq_acc, diag and u == count - 1)
                ec = ec + ch[6]
            return ec, dq_acc

        lane_row = lax.broadcasted_iota(jnp.int32, (1, LW), 1)
        reached = (jnp.max(carr, axis=0, keepdims=True) > 0.5 * SB_UNSEEN) & (lane_row < HEAD_DIM)
        first = jnp.min(jnp.where(reached, lane_row.astype(F32), float(n_tiles))).astype(jnp.int32)
        U = SB_UNROLL
        count = I - first
        rest = jnp.maximum(count - 1, 0)
        state = (jnp.zeros((NH * T, 1), F32), jnp.zeros((NH * T, LW), F32))
        state = lax.fori_loop(0, rest // U, lambda jj, st: tiles(first + U * jj, U, st, False), state)
        state = lax.fori_loop(0, rest % U, lambda r, st: tiles(I - 1 - rest % U + r, 1, st, False), state)
        _, dq_acc = lax.cond(count > 0, lambda st: tiles(I - 1, 2, st, True), lambda st: tiles(I, 1, st, True), state)
        dq_ref[rows, :] = _sb_unstack_heads(dq_acc) * scale
        return 0

    blk = lambda src_off: pl.BlockSpec((R * T, LW), lambda p, I: (I, src_off + p))
    full = lambda off: pl.BlockSpec((S, LW), lambda p, I: (0, off + p))
    const = lambda rows: pl.BlockSpec((rows, T), lambda p, I: (0, 0))
    dq, dk, dv, *got = pl.pallas_call(
        body, name=name, grid=grid,
        out_shape=[jax.ShapeDtypeStruct((S, W), F32)] * 3 + hosted.out_shapes,
        in_specs=[blk(0), full(npair), full(2 * npair), blk(0), blk(0), const(NH * T), const(T), const(T)]
        + hosted.in_specs,
        out_specs=[blk(0), full(0), full(0)] + hosted.out_specs,
        scratch_shapes=hosted.scratch,
        compiler_params=_cparams(hosted.semantics(("parallel", "arbitrary")), vmem_mb=56),
    )(p_sb, p_sb, p_sb, do, carries, _sb_causal(T, NH).astype(F32), _sb_triangle(T, True), _sb_triangle(T, False),
      *hosted.operands)
    return (dq, dk, dv) if comm is None else (dq, dk, dv, got)


def _dil_blocks(b, body_fn):
    for pi, (window, dil) in enumerate(DILATED_PATTERNS):
        assert window // dil == DIL_BLOCK
        nblk = DIL_SUPER // (DIL_BLOCK * dil)
        assert (dil * nblk) % DIL_UNROLL == 0

        def group(g, _, pi=pi, dil=dil, nblk=nblk):
            for u in range(DIL_UNROLL):
                t = g * DIL_UNROLL + u
                n = t % nblk
                body_fn(pi, dil, t // nblk, n, b * nblk + n)
            return 0

        lax.fori_loop(0, dil * nblk // DIL_UNROLL, group, 0)


def _dil_rows(start, size, dil):
    if dil == 1:
        return pl.ds(pl.multiple_of(start, DIL_BLOCK), size)
    return pl.ds(start, size, stride=dil)


def _dil_fill_bias(bias_ref):
    row = lax.broadcasted_iota(jnp.int32, (2 * DIL_BLOCK, 2 * DIL_BLOCK), 0)
    kk = lax.broadcasted_iota(jnp.int32, (2 * DIL_BLOCK, 2 * DIL_BLOCK), 1)
    qi = jnp.where(row >= DIL_BLOCK, row - DIL_BLOCK, row)
    for s in range(2):
        dist = s * DIL_BLOCK + qi - kk
        bias_ref[s] = jnp.where((dist >= 0) & (dist <= DIL_BLOCK), 0.0, NEG_BIG)


def _dl_fwd(p_dl, *, name):
    S, W = p_dl.shape[0], p_dl.shape[1] // 3
    npair = W // LANES
    nsuper = S // DIL_SUPER
    assert S % DIL_SUPER == 0 and S // max(d for _, d in DILATED_PATTERNS) >= 2 * DIL_BLOCK
    scale = HEAD_DIM ** -0.5
    npat = len(DILATED_PATTERNS)

    def body(q_ref, k_ref, v_ref, o_ref, l_ref, bias_ref, *pattern_refs):
        op_refs, lp_refs = pattern_refs[:npat], pattern_refs[npat:]
        b = pl.program_id(1)
        masks = _head_masks((DIL_BLOCK, LANES))
        pl.when(b == 0)(lambda: _dil_fill_bias(bias_ref))

        def block(pi, dil, c, n, gn):
            ws = jnp.maximum(gn - 1, 0)
            qrows = n * (DIL_BLOCK * dil) + c
            krows = ws * (DIL_BLOCK * dil) + c
            q_idx = _dil_rows(qrows, DIL_BLOCK, dil)
            k_idx = _dil_rows(krows, 2 * DIL_BLOCK, dil)
            qb = q_ref[q_idx, :]
            kb = k_ref[k_idx, :].astype(BF16)
            vb = v_ref[k_idx, :].astype(BF16)
            q2 = _sb_stack_heads(qb.astype(BF16), scale)
            z = lax.dot_general(q2, kb, (((1,), (1,)), ((), ())), preferred_element_type=F32) + bias_ref[gn - ws]
            m = jnp.max(z, axis=1, keepdims=True)
            p = jnp.exp(z - m)
            den = jnp.sum(p, axis=1, keepdims=True)
            acc = jnp.dot(p.astype(BF16), vb, preferred_element_type=F32)
            lse = m + jnp.log(den)
            op_refs[pi][q_idx, :] = _sb_unstack_heads(acc / den)
            lp_refs[pi][q_idx, :] = jnp.where(masks[0], lse[:DIL_BLOCK], lse[DIL_BLOCK:])

        _dil_blocks(b, block)
        lses = [r[...] for r in lp_refs]
        top = functools.reduce(jnp.maximum, lses)
        ws_ = [jnp.exp(l - top) for l in lses]
        den = functools.reduce(jnp.add, ws_)
        num = functools.reduce(jnp.add, [w * r[...] for r, w in zip(op_refs, ws_)])
        o_ref[...] = num / den
        l_ref[...] = top + jnp.log(den)

    blk = pl.BlockSpec((DIL_SUPER, LANES), lambda p, b: (b, p))
    full = lambda off: pl.BlockSpec((S, LANES), lambda p, b: (0, off + p))
    return pl.pallas_call(
        body, name=name, grid=(npair, nsuper),
        out_shape=[jax.ShapeDtypeStruct((S, W), F32)] * 2,
        in_specs=[blk, full(npair), full(2 * npair)], out_specs=[blk, blk],
        scratch_shapes=[pltpu.VMEM((2, 2 * DIL_BLOCK, 2 * DIL_BLOCK), F32)]
        + [pltpu.VMEM((DIL_SUPER, LANES), F32)] * (2 * npat),
        compiler_params=_cparams(("arbitrary", "arbitrary")),
    )(p_dl, p_dl, p_dl)


def _dl_bwd(p_dl, o, lse, do, *, name):
    S, W = p_dl.shape[0], p_dl.shape[1] // 3
    npair = W // LANES
    nsuper = S // DIL_SUPER
    scale = HEAD_DIM ** -0.5

    def body(q_ref, k_ref, v_ref, o_ref, l_ref, do_ref, dq_ref, dk_ref, dv_ref, delta_ref, bias_ref):
        b = pl.program_id(1)

        @pl.when(b == 0)
        def _():
            dk_ref[...] = jnp.zeros_like(dk_ref)
            dv_ref[...] = jnp.zeros_like(dv_ref)
            _dil_fill_bias(bias_ref)

        dq_ref[...] = jnp.zeros_like(dq_ref)
        prod = do_ref[...] * o_ref[...]
        delta = jnp.zeros_like(prod)
        for hm in _head_masks(prod.shape):
            delta = jnp.where(hm, jnp.sum(jnp.where(hm, prod, 0.0), axis=1, keepdims=True), delta)
        delta_ref[...] = delta

        def block(pi, dil, c, n, gn):
            ws = jnp.maximum(gn - 1, 0)
            qrows = n * (DIL_BLOCK * dil) + c
            krows = ws * (DIL_BLOCK * dil) + c
            q_idx = _dil_rows(qrows, DIL_BLOCK, dil)
            k_idx = _dil_rows(krows, 2 * DIL_BLOCK, dil)
            qb = q_ref[q_idx, :]
            dob = do_ref[q_idx, :]
            lb = l_ref[q_idx, :]
            db = delta_ref[q_idx, :]
            kb = k_ref[k_idx, :].astype(BF16)
            vb = v_ref[k_idx, :].astype(BF16)
            q2 = _sb_stack_heads(qb.astype(BF16), scale)
            do2 = _sb_stack_heads(dob.astype(BF16))
            lse2 = jnp.concatenate([lb[:, HEAD_DIM * h:HEAD_DIM * h + 1] for h in range(2)], axis=0)
            delta2 = jnp.concatenate([db[:, HEAD_DIM * h:HEAD_DIM * h + 1] for h in range(2)], axis=0)
            z = lax.dot_general(q2, kb, (((1,), (1,)), ((), ())), preferred_element_type=F32)
            p = jnp.exp((z + bias_ref[gn - ws]) - lse2)
            dp = lax.dot_general(do2, vb, (((1,), (1,)), ((), ())), preferred_element_type=F32)
            dzb = (p * (dp - delta2)).astype(BF16)
            tn_dims = (((0,), (0,)), ((), ()))
            dq_blk = _sb_unstack_heads(jnp.dot(dzb, kb, preferred_element_type=F32)) * scale
            dk_blk = lax.dot_general(dzb, q2, tn_dims, preferred_element_type=F32)
            dv_blk = lax.dot_general(p.astype(BF16), do2, tn_dims, preferred_element_type=F32)
            dq_ref[q_idx, :] = dq_ref[q_idx, :] + dq_blk
            dk_ref[k_idx, :] = dk_ref[k_idx, :] + dk_blk
            dv_ref[k_idx, :] = dv_ref[k_idx, :] + dv_blk

        _dil_blocks(b, block)

    blk = pl.BlockSpec((DIL_SUPER, LANES), lambda p, b: (b, p))
    full = lambda off: pl.BlockSpec((S, LANES), lambda p, b: (0, off + p))
    return pl.pallas_call(
        body, name=name, grid=(npair, nsuper),
        out_shape=[jax.ShapeDtypeStruct((S, W), F32)] * 3,
        in_specs=[blk, full(npair), full(2 * npair), blk, blk, blk], out_specs=[blk, full(0), full(0)],
        scratch_shapes=[pltpu.VMEM((DIL_SUPER, LANES), F32), pltpu.VMEM((2, 2 * DIL_BLOCK, 2 * DIL_BLOCK), F32)],
        compiler_params=_cparams(("arbitrary", "arbitrary")),
    )(p_dl, p_dl, p_dl, o, lse, do)


class _NoExchange:
    def gather(self, family):
        return None

    def gathered(self, family, got, weights):
        pass

    def send(self, family, grads):
        return None

    def received(self, family, got):
        pass


def _local_step(x, target, gains, weights, exchanges=None):
    S, D = x.shape
    ex = exchanges or _NoExchange()
    weights = dict(weights)
    d_sb = gains["sb_out_norm"].shape[1]
    d_dl = gains["dil_out_norm"].shape[1]
    cos_t, sin_t = _rope_tables(S)

    riders = ("ffn1_w_up", "ffn1_w_down", "mixer") if exchanges else (None, None, None)
    (x1, h2), saved1 = _ffn_fwd(x, gains["ffn1_norm"], weights, tag="ffn1", ex=ex, riders=riders,
                                first_rider="ffn1_w_gate" if exchanges else None,
                                next_gain=gains["mix_norm"])
    w_in = weights["w_in"]
    w_out = weights["w_out"]
    p_sb = _mm(h2, w_in, tb=True, b_cols=(0, 3 * d_sb), outs=(BF16,), name="proj_sb")

    def rope_qk(acc, _, cos, sin):
        return jnp.concatenate([_rotate(acc[:, :2 * d_dl], cos, sin, 1.0), acc[:, 2 * d_dl:]], axis=1)

    p_dl = _mm(h2, w_in, tb=True, b_cols=(3 * d_sb, 3 * d_dl), lanes=(cos_t, sin_t), epilogue=rope_qk,
               name="proj_dl_rope")
    plan = ex.gather("ffn2" if exchanges else None)
    o_sb, carries, *got = _sb_fwd(p_sb, name="sb_fwd", comm=plan)
    ex.gathered("ffn2", got[0] if got else None, weights)
    o_dl, lse_dl = _dl_fwd(p_dl, name="dl_fwd")
    merged = _rms_fwd([o_sb, o_dl], [gains["sb_out_norm"], gains["dil_out_norm"]], name="out_norm")
    x2, h3 = _mm(merged, w_out, res=x1, rows=(gains["ffn2_norm"],), outs=(F32, BF16),
                 epilogue=_residual_then_norm(1.0), name="out_proj_norm")
    (dx3, d_final, loss_wide), saved2 = _ffn_fwd(x2, gains["ffn2_norm"], weights, tag="ffn2", ex=ex, h=h3,
                                                 head=(gains["final_norm"], target))
    loss_row = loss_wide[:, :LANES]

    dx2, d_ffn2_norm, dwg2, dwu2, dwd2 = _ffn_bwd(dx3, x2, gains["ffn2_norm"], weights, saved2, tag="ffn2", ex=ex)
    d_w_out = _mm(merged, dx2, ta=True, outs=(GRAD_WIRE,), name="d_w_out")
    do_sb, d_sb_norm = _mm(dx2, w_out, tb=True, b_cols=(0, d_sb), extras=(o_sb,), rows=(gains["sb_out_norm"],),
                           row_sums=1, epilogue=_rms_bwd_epilogue, name="d_merged_sb")
    do_dl, d_dl_norm = _mm(dx2, w_out, tb=True, b_cols=(d_sb, d_dl), extras=(o_dl,), rows=(gains["dil_out_norm"],),
                           row_sums=1, epilogue=_rms_bwd_epilogue, name="d_merged_dl")
    plan = ex.send("ffn2", dict(ffn2_w_gate=dwg2, ffn2_w_up=dwu2, ffn2_w_down=dwd2))
    dq_sb, dk_sb, dv_sb, *got = _sb_bwd(p_sb, do_sb, carries, name="sb_bwd", comm=plan)
    ex.received("ffn2", got[0] if got else None)
    dq_dl, dk_dl, dv_dl = _dl_bwd(p_dl, o_dl, lse_dl, do_dl, name="dl_bwd")
    d_proj = _join_d_proj([dq_sb, dk_sb, dv_sb, dq_dl, dk_dl, dv_dl], (3, 4), cos_t, sin_t, name="d_proj")
    d_w_in = _mm(h2, d_proj, ta=True, outs=(GRAD_WIRE,), name="d_w_in")
    dx1, d_mix_norm = _mm(d_proj, w_in, extras=(x1, dx2), rows=(gains["mix_norm"],), row_sums=1,
                          epilogue=_rms_bwd_epilogue, ring=True, name="dh_mix_norm_bwd")
    dx, d_ffn1_norm, dwg1, dwu1, dwd1 = _ffn_bwd(
        dx1, x, gains["ffn1_norm"], weights, saved1, tag="ffn1", ex=ex,
        rider=("mixer", dict(w_in=d_w_in, w_out=d_w_out)), spread=True)
    gain_grads = dict(ffn1_norm=d_ffn1_norm, mix_norm=d_mix_norm, sb_out_norm=d_sb_norm, dil_out_norm=d_dl_norm,
                      ffn2_norm=d_ffn2_norm, final_norm=d_final)
    weight_grads = dict(ffn1_w_gate=dwg1, ffn1_w_up=dwu1, ffn1_w_down=dwd1, w_in=d_w_in, w_out=d_w_out,
                        ffn2_w_gate=dwg2, ffn2_w_up=dwu2, ffn2_w_down=dwd2)
    return loss_row, dx, gain_grads, weight_grads


def _mesh_position():
    return lax.axis_index("x"), lax.axis_index("y"), lax.axis_index("c")


def _flip(coord, bit):
    return 1 - coord if bit else coord


RELATIONS = [(rx, ry, rc) for rx in (0, 1) for ry in (0, 1) for rc in (0, 1)][1:]


class _GatherPlan:
    def __init__(self, shards):
        n = len(shards)
        self.operands = list(shards)
        self.out_shapes = [jax.ShapeDtypeStruct((N_DEV,) + s.shape, s.dtype) for s in shards]
        self.scratch = [pltpu.SemaphoreType.DMA((n, 7)), pltpu.SemaphoreType.DMA((n, 7)),
                        pltpu.SemaphoreType.DMA((n,))]

    def _copies(self, in_refs, out_refs, sems):
        send_sems, recv_sems, local_sems = sems
        x, y, c = _mesh_position()
        me, sibling = (x, y, c), (x, y, 1 - c)
        chips = [(1 - x, y), (x, 1 - y), (1 - x, 1 - y)]
        plans = []
        for t, (x_ref, out_ref) in enumerate(zip(in_refs, out_refs)):
            def slot(px, py, pc, out_ref=out_ref):
                return out_ref.at[4 * px + 2 * py + pc]

            def copy(k, block, to, src=None, t=t, slot=slot):
                return pltpu.make_async_remote_copy(
                    src_ref=slot(*block) if src is None else src, dst_ref=slot(*block),
                    send_sem=send_sems.at[t, k], recv_sem=recv_sems.at[t, k],
                    device_id=to, device_id_type=pl.DeviceIdType.MESH)

            plans.append(dict(
                mine=pltpu.make_async_copy(x_ref, slot(*me), local_sems.at[t]),
                first=[copy(0, me, sibling, src=x_ref)]
                + [copy(1 + j, me, (*chip, c), src=x_ref) for j, chip in enumerate(chips)],
                over_ici=[copy(1 + j, (*chip, c), me) for j, chip in enumerate(chips)],
                passed=[copy(4 + j, (*chip, c), sibling) for j, chip in enumerate(chips)],
                from_sibling=[copy(0, sibling, me)] + [copy(4 + j, (*chip, 1 - c), me) for j, chip in enumerate(chips)]))
        return plans

    def start(self, in_refs, out_refs, sems):
        for p in self._copies(in_refs, out_refs, sems):
            p["mine"].start()
            for cp in p["first"]:
                cp.start()

    def finish(self, in_refs, out_refs, sems):
        plans = self._copies(in_refs, out_refs, sems)
        for p in plans:
            for arrived, onward in zip(p["over_ici"], p["passed"]):
                arrived.wait_recv()
                onward.start()
        for p in plans:
            for cp in p["from_sibling"]:
                cp.wait_recv()
            for cp in p["first"] + p["passed"]:
                cp.wait_send()
            p["mine"].wait()


class _Hosted:
    def __init__(self, plan, n_in, n_out, n_scratch):
        self.plan, self.n_in, self.n_out, self.n_scratch = plan, n_in, n_out, n_scratch
        self.operands = list(plan.operands) if plan else []
        self.out_shapes = list(plan.out_shapes) if plan else []
        self.scratch = list(plan.scratch) if plan else []
        self.in_specs = [pl.BlockSpec(memory_space=pl.ANY)] * len(self.operands)
        self.out_specs = [pl.BlockSpec(memory_space=pl.ANY)] * len(self.out_shapes)

    def semantics(self, sem):
        return sem if self.plan is None else ("arbitrary",) * len(sem)

    def _at(self, grid, last):
        hit = None
        for d, n in enumerate(grid):
            here = pl.program_id(d) == (n - 1 if last else 0)
            hit = here if hit is None else hit & here
        return hit

    def begin(self, refs, grid):
        if self.plan is None:
            return refs
        k_in, k_out = len(self.operands), len(self.out_shapes)
        ins, rest = refs[:self.n_in], refs[self.n_in:]
        c_in, rest = rest[:k_in], rest[k_in:]
        outs, rest = rest[:self.n_out], rest[self.n_out:]
        c_out, rest = rest[:k_out], rest[k_out:]
        scratch, sems = rest[:self.n_scratch], rest[self.n_scratch:]
        self._args = (c_in, c_out, sems)
        pl.when(self._at(grid, False))(lambda: self.plan.start(*self._args))
        return tuple(ins) + tuple(outs) + tuple(scratch)

    def end(self, grid):
        if self.plan is not None:
            pl.when(self._at(grid, True))(lambda: self.plan.finish(*self._args))


class _ExchangePlan:
    def __init__(self, packs):
        n = len(packs)
        self.operands = list(packs)
        self.out_shapes = [jax.ShapeDtypeStruct(p.shape, p.dtype) for p in packs]
        self.scratch = [pltpu.SemaphoreType.DMA((n, 7)), pltpu.SemaphoreType.DMA((n, 7)),
                        pltpu.SemaphoreType.DMA((n,))]

    def _copies(self, in_refs, out_refs, sems):
        send_sems, recv_sems, local_sems = sems
        x, y, c = _mesh_position()
        me = 4 * x + 2 * y + c
        copies = [pltpu.make_async_copy(i.at[me], o.at[me], local_sems.at[t])
                  for t, (i, o) in enumerate(zip(in_refs, out_refs))]
        for r, (rx, ry, rc) in enumerate(RELATIONS):
            px, py, pc = _flip(x, rx), _flip(y, ry), _flip(c, rc)
            peer = 4 * px + 2 * py + pc
            copies += [pltpu.make_async_remote_copy(
                src_ref=i.at[peer], dst_ref=o.at[me], send_sem=send_sems.at[t, r], recv_sem=recv_sems.at[t, r],
                device_id=(px, py, pc), device_id_type=pl.DeviceIdType.MESH)
                for t, (i, o) in enumerate(zip(in_refs, out_refs))]
        return copies

    def start(self, in_refs, out_refs, sems):
        for cp in self._copies(in_refs, out_refs, sems):
            cp.start()

    def finish(self, in_refs, out_refs, sems):
        for cp in self._copies(in_refs, out_refs, sems):
            cp.wait()


def _all_reduce_rows(v, *, name):
    R, C = v.shape

    def body(v_ref, out_ref, buf, send_sems, recv_sems):
        x, y, c = _mesh_position()
        me = 4 * x + 2 * y + c
        buf[me] = v_ref[...]
        copies = []
        for r, (rx, ry, rc) in enumerate(RELATIONS):
            cp = pltpu.make_async_remote_copy(
                src_ref=v_ref, dst_ref=buf.at[me], send_sem=send_sems.at[r], recv_sem=recv_sems.at[r],
                device_id=(_flip(x, rx), _flip(y, ry), _flip(c, rc)), device_id_type=pl.DeviceIdType.MESH)
            cp.start()
            copies.append(cp)
        for cp in copies:
            cp.wait()
        total = buf[0]
        for s in range(1, N_DEV):
            total = total + buf[s]
        out_ref[...] = total

    return pl.pallas_call(
        body, name=name,
        out_shape=jax.ShapeDtypeStruct((R, C), F32),
        in_specs=[pl.BlockSpec(memory_space=pltpu.VMEM)],
        out_specs=pl.BlockSpec(memory_space=pltpu.VMEM),
        scratch_shapes=[pltpu.VMEM((N_DEV, R, C), F32), pltpu.SemaphoreType.DMA((7,)), pltpu.SemaphoreType.DMA((7,))],
    )(v)


def _adamw(w, g, m, v, *, name):
    R, C = w.shape
    slots = g.ndim == 3
    tr = _pick(R, (256, 128, 64, 32, 16) if slots else (256, 128, 64, 32, 16, 8))

    def body(w_ref, g_ref, m_ref, v_ref, g_out, d_ref, nm_ref, nv_ref):
        if slots:
            g = g_ref[0].astype(F32)
            for s in range(1, N_DEV):
                g = g + g_ref[s].astype(F32)
        else:
            g = g_ref[...]
        g_out[...] = g
        m_new = ADAM_B1 * m_ref[...] + (1.0 - ADAM_B1) * g
        v_new = ADAM_B2 * v_ref[...] + (1.0 - ADAM_B2) * (g * g)
        m_hat = m_new / (1.0 - ADAM_B1 ** ADAM_STEP)
        v_hat = v_new / (1.0 - ADAM_B2 ** ADAM_STEP)
        d_ref[...] = -ADAM_LR * (m_hat / (jnp.sqrt(v_hat) + ADAM_EPS) + ADAM_WD * w_ref[...])
        nm_ref[...] = m_new
        nv_ref[...] = v_new

    spec = pl.BlockSpec((tr, C), lambda i: (i, 0))
    g_spec = pl.BlockSpec((N_DEV, tr, C), lambda i: (0, i, 0)) if slots else spec
    return pl.pallas_call(
        body, name=name, grid=(R // tr,),
        out_shape=[jax.ShapeDtypeStruct((R, C), F32)] * 4,
        in_specs=[spec, g_spec, spec, spec], out_specs=[spec] * 4,
        compiler_params=_cparams(("parallel",)),
    )(w, g, m, v)


WEIGHT_NAMES = ["ffn1_norm", "ffn1_w_gate", "ffn1_w_up", "ffn1_w_down", "mix_norm", "w_in", "sb_out_norm",
                "dil_out_norm", "w_out", "ffn2_norm", "ffn2_w_gate", "ffn2_w_up", "ffn2_w_down", "final_norm"]
GAIN_NAMES = ["ffn1_norm", "mix_norm", "sb_out_norm", "dil_out_norm", "ffn2_norm", "final_norm"]
COL_SHARDED = ["ffn1_w_gate", "ffn1_w_up", "ffn2_w_gate", "ffn2_w_up", "w_in"]
ROW_SHARDED = ["ffn1_w_down", "ffn2_w_down", "w_out"]
GROUPS = {"mixer": (["w_in"], ["w_out"]),
          "ffn2": (["ffn2_w_gate", "ffn2_w_up"], ["ffn2_w_down"])}
for _ffn in ("ffn1", "ffn2"):
    GROUPS.update({f"{_ffn}_w_gate": ([f"{_ffn}_w_gate"], []), f"{_ffn}_w_up": ([f"{_ffn}_w_up"], []),
                   f"{_ffn}_w_down": ([], [f"{_ffn}_w_down"])})


class _Exchanges:
    def __init__(self, params):
        self.params = params
        self.grads = {}

    def gather(self, group):
        if group is None:
            return None
        cols, rows = GROUPS[group]
        return _GatherPlan([self.params[n].T.astype(BF16) for n in cols] + [self.params[n].astype(BF16) for n in rows])

    def gathered(self, group, got, weights):
        if group is None:
            return
        cols, rows = GROUPS[group]
        for n, blocks in zip(cols + rows, got):
            weights[n] = blocks.reshape(N_DEV * blocks.shape[1], blocks.shape[2])

    def send(self, group, grads):
        if group is None:
            return None
        cols, rows = GROUPS[group]
        packs = [jnp.transpose(grads[n].reshape(grads[n].shape[0], N_DEV, self.params[n].shape[1]), (1, 0, 2))
                 for n in cols]
        packs += [grads[n].reshape(N_DEV, self.params[n].shape[0], grads[n].shape[1]) for n in rows]
        return _ExchangePlan([p.astype(GRAD_WIRE) for p in packs])

    def received(self, group, got):
        if group is None:
            return
        cols, rows = GROUPS[group]
        for n, slots in zip(cols + rows, got):
            self.grads[n] = slots


def _step(x, target, params, moments_m, moments_v):
    ex = _Exchanges(params)
    weights = {}
    gains = {n: params[n] for n in GAIN_NAMES}
    loss_row, grad_x, gain_grads, _ = _local_step(x, target, gains, weights, ex)
    grads = ex.grads

    rows = [gain_grads[n].reshape(-1, LANES) for n in GAIN_NAMES] + [loss_row]
    small = jnp.concatenate(rows, axis=0)
    pad = (-small.shape[0]) % 8
    small = jnp.pad(small, ((0, pad), (0, 0)))
    small = _all_reduce_rows(small, name="reduce_gains_loss")
    off = 0
    for n in GAIN_NAMES:
        r = gain_grads[n].shape[1] // LANES
        grads[n] = small[off:off + r].reshape(1, -1)
        off += r
    loss = small[off, 0]

    delta, new_m, new_v = {}, {}, {}
    for n in WEIGHT_NAMES:
        grads[n], delta[n], new_m[n], new_v[n] = _adamw(params[n], grads[n], moments_m[n], moments_v[n],
                                                        name=f"adamw_{n}")
    return loss, grad_x, grads, delta, new_m, new_v


def kernel(x, ffn1_norm, ffn1_w_gate, ffn1_w_up, ffn1_w_down, mix_norm, w_in, sb_out_norm, dil_out_norm, w_out, ffn2_norm, ffn2_w_gate, ffn2_w_up, ffn2_w_down, final_norm, loss_target, m_ffn1_norm, m_ffn1_w_gate, m_ffn1_w_up, m_ffn1_w_down, m_mix_norm, m_w_in, m_sb_out_norm, m_dil_out_norm, m_w_out, m_ffn2_norm, m_ffn2_w_gate, m_ffn2_w_up, m_ffn2_w_down, m_final_norm, v_ffn1_norm, v_ffn1_w_gate, v_ffn1_w_up, v_ffn1_w_down, v_mix_norm, v_w_in, v_sb_out_norm, v_dil_out_norm, v_w_out, v_ffn2_norm, v_ffn2_w_gate, v_ffn2_w_up, v_ffn2_w_down, v_final_norm):
    given = dict(locals())
    shapes = {n: given[n].shape for n in WEIGHT_NAMES}

    def as2d(a):
        return a.reshape(1, -1) if a.ndim == 1 else a.reshape(a.shape[-2], a.shape[-1])

    params = {n: as2d(given[n]) for n in WEIGHT_NAMES}
    moments_m = {n: as2d(given["m_" + n]) for n in WEIGHT_NAMES}
    moments_v = {n: as2d(given["v_" + n]) for n in WEIGHT_NAMES}
    loss, grad_x, grads, delta, new_m, new_v = _step(x[0], loss_target[0], params, moments_m, moments_v)
    back = lambda d: [d[n].reshape(shapes[n]) for n in WEIGHT_NAMES]
    return (loss, grad_x[None], *back(grads), *back(delta), *back(new_m), *back(new_v))
```

```python
import functools

import jax
import jax.numpy as jnp
from jax import lax
from jax.experimental import pallas as pl
from jax.experimental.pallas import tpu as pltpu

F32 = jnp.float32
BF16 = jnp.bfloat16
GRAD_WIRE = jnp.bfloat16

N_DEV = 8
HEAD_DIM = 64
LANES = 128
DILATED_PATTERNS = ((128, 1), (512, 4), (2048, 16))
DIL_BLOCK = 128
DIL_SUPER = 2048
DIL_UNROLL = 16
SB_TILE = 256
SB_LANES = 128
SB_UNROLL = 4
SB_STEP_TILES = 2
SB_DEAD = 90.0
SB_UNSEEN = -1e30
ROPE_THETA = 10000.0
RMS_EPS = 1e-6
HALF_STEP = 0.5
ADAM_LR = 0.001
ADAM_B1 = 0.9
ADAM_B2 = 0.999
ADAM_EPS = 1e-08
ADAM_WD = 0.01
ADAM_STEP = 10
NEG_BIG = -1e30
VMEM_CAP_MB = 60


def _pick(n, prefs):
    for p in prefs:
        if n % p == 0:
            return p
    return n


MM_MAX_TILE = 1536
MM_WHOLE = 3072
RING_SLOTS = 3


def _largest_tile(n, cap):
    if n <= cap:
        return n
    for t in range(cap - cap % LANES, 0, -LANES):
        if n % t == 0:
            return t
    return n


def _cparams(sem=None, vmem_mb=48):
    return pltpu.CompilerParams(dimension_semantics=sem, vmem_limit_bytes=min(vmem_mb, VMEM_CAP_MB) * 1024 * 1024)


def _nbytes(shape, dtype):
    n = 1
    for s in shape:
        n *= s
    return n * jnp.dtype(dtype).itemsize


def _mm(a, b, *, name, ta=False, tb=False, outs=(F32,), res=None, alpha=1.0, extras=(), epilogue=None,
        tm=None, tn=None, tk=None, comm=None, rows=(), lanes=(), row_sums=0, b_cols=None, second=None,
        twin_b=None, ring=False):
    if ta:
        K, M = a.shape
    else:
        M, K = a.shape
    if tb:
        N, Kb = b.shape
    else:
        Kb, N = b.shape
    col0 = 0
    if b_cols is not None:
        col0, N = b_cols
    assert K == Kb, (a.shape, b.shape, ta, tb)
    tn = tn or (N if (not ta and K <= MM_WHOLE and N <= MM_WHOLE) else _largest_tile(N, MM_MAX_TILE))
    wide = tn > MM_MAX_TILE and (len(extras) + len(outs) > 3 or a.dtype == F32)
    tm = tm or (_largest_tile(M, MM_MAX_TILE) if ta else _pick(M, (256, 128) if wide else (512, 256, 128)))
    tk = tk or (K if K <= MM_WHOLE else _pick(K, (2048, 1024, 512, 256, 128)))
    nk = K // tk
    a_spec = pl.BlockSpec((tk, tm), lambda i, j, k: (k, i)) if ta else pl.BlockSpec((tm, tk), lambda i, j, k: (i, k))
    assert col0 % tn == 0
    b_spec = (pl.BlockSpec((tn, tk), lambda i, j, k: (j + col0 // tn, k)) if tb
              else pl.BlockSpec((tk, tn), lambda i, j, k: (k, j + col0 // tn)))
    mn_spec = pl.BlockSpec((tm, tn), lambda i, j, k: (i, j))
    dims = (((0 if ta else 1,), (1 if tb else 0,)), ((), ()))
    row_spec = pl.BlockSpec((1, tn), lambda i, j, k: (0, j))
    lane_spec = pl.BlockSpec((tm, LANES), lambda i, j, k: (i, 0))
    n_extra = len(extras) + (1 if res is not None else 0) + len(rows) + len(lanes)
    n_mn = len(outs)
    n_out = n_mn + row_sums
    assert row_sums == 0 or tn == N
    grid = (M // tm, N // tn, nk)
    n_ab = 2 + (2 if second is not None else 0) + (1 if twin_b is not None else 0)
    assert twin_b is None or (nk == 1 and epilogue is not None)
    n_ring = len(extras) if ring else 0
    assert not ring or (grid[1] == 1 and nk == 1 and res is None)
    hosted = _Hosted(comm, n_in=n_ab + n_extra, n_out=n_out, n_scratch=(1 if nk > 1 else 0) + 2 * n_ring)

    def body(*refs):
        in_refs = list(refs[n_ab:n_ab + n_extra])
        ab_refs = refs[:n_ab]
        refs = hosted.begin(refs, grid)
        out_refs = refs[n_ab + n_extra:n_ab + n_extra + n_out]
        ring_blocks = []
        if n_ring:
            scratch = refs[n_ab + n_extra + n_out:]
            bufs, sems = scratch[:n_ring], scratch[n_ring:2 * n_ring]
            step, steps = pl.program_id(0), grid[0]

            def fetch(s):
                return [pltpu.make_async_copy(src.at[pl.ds(pl.multiple_of(s * tm, tm), tm), :], buf.at[s % RING_SLOTS],
                                              sem.at[s % RING_SLOTS])
                        for src, buf, sem in zip(in_refs[:n_ring], bufs, sems)]

            @pl.when(step == 0)
            def _():
                for s in range(min(RING_SLOTS - 1, steps)):
                    for cp in fetch(s):
                        cp.start()

            @pl.when(step + RING_SLOTS - 1 < steps)
            def _():
                for cp in fetch(step + RING_SLOTS - 1):
                    cp.start()

            for cp in fetch(step):
                cp.wait()
            ring_blocks = [buf[step % RING_SLOTS] for buf in bufs]
        def product(a_ref, b_ref):
            return lax.dot_general(a_ref[...].astype(BF16), b_ref[...].astype(BF16), dims, preferred_element_type=F32)

        prod = product(ab_refs[0], ab_refs[1])
        if second is not None:
            prod = prod + product(ab_refs[2], ab_refs[3])
        if twin_b is not None:
            prod = (prod, product(ab_refs[0], ab_refs[-1]))

        def finish(acc):
            blocks = ring_blocks + [r[...] for r in in_refs[n_ring:]]
            if res is not None:
                r_blk, blocks = blocks[0], blocks[1:]
            else:
                r_blk = None
            if epilogue is None:
                val = acc * alpha
                if r_blk is not None:
                    val = val + r_blk
                vals = (val,)
            else:
                vals = epilogue(acc, r_blk, *blocks)
                vals = vals if isinstance(vals, (tuple, list)) else (vals,)
            for o_ref, v in zip(out_refs[:n_mn], vals[:n_mn]):
                o_ref[...] = v.astype(o_ref.dtype)
            first_rows = pl.program_id(0) == 0
            for o_ref, part in zip(out_refs[n_mn:], vals[n_mn:]):
                @pl.when(first_rows)
                def _(o_ref=o_ref, part=part):
                    o_ref[...] = part

                @pl.when(jnp.logical_not(first_rows))
                def _(o_ref=o_ref, part=part):
                    o_ref[...] += part

        if nk == 1:
            finish(prod)
        else:
            acc_ref = refs[n_ab + n_extra + n_out]
            k = pl.program_id(2)

            @pl.when(k == 0)
            def _():
                acc_ref[...] = prod

            @pl.when(k > 0)
            def _():
                acc_ref[...] += prod

            @pl.when(k == nk - 1)
            def _():
                finish(acc_ref[...])

        hosted.end(grid)

    mn_operands = ([res] if res is not None else []) + list(extras)
    ab = [a, b] + (list(second) if second is not None else []) + ([twin_b] if twin_b is not None else [])
    ab_specs = [a_spec, b_spec] * (1 if second is None else 2) + ([b_spec] if twin_b is not None else [])
    operands = ab + mn_operands + list(rows) + list(lanes)
    any_spec = pl.BlockSpec(memory_space=pl.ANY)
    mn_specs = [mn_spec] * (len(mn_operands) - n_ring) + [any_spec] * n_ring
    in_specs = ab_specs + mn_specs + [row_spec] * len(rows) + [lane_spec] * len(lanes)
    est = n_ab * (_nbytes((tm, tk), a.dtype) + _nbytes((tk, tn), b.dtype))
    est += RING_SLOTS * sum(_nbytes((tm, tn), o.dtype) for o in mn_operands)
    est += 2 * sum(_nbytes((tm, tn), d) for d in outs) + 2 * _nbytes((tm, tn), F32)
    semantics = ("parallel", "parallel", "arbitrary") if row_sums == 0 and not ring else ("arbitrary",) * 3
    ring_scratch = [pltpu.VMEM((RING_SLOTS, tm, tn), e.dtype) for e in extras[:n_ring]]
    ring_scratch += [pltpu.SemaphoreType.DMA((RING_SLOTS,))] * n_ring
    result = pl.pallas_call(
        body, name=name, grid=grid,
        out_shape=[jax.ShapeDtypeStruct((M, N), d) for d in outs]
        + [jax.ShapeDtypeStruct((1, N), F32)] * row_sums + hosted.out_shapes,
        in_specs=in_specs + hosted.in_specs,
        out_specs=[mn_spec] * n_mn + [row_spec] * row_sums + hosted.out_specs,
        scratch_shapes=([pltpu.VMEM((tm, tn), F32)] if nk > 1 else []) + ring_scratch + hosted.scratch,
        compiler_params=_cparams(hosted.semantics(semantics), vmem_mb=max(32, 2 * est // (1024 * 1024))),
    )(*operands, *hosted.operands)
    own, got = result[:n_out], list(result[n_out:])
    own = own[0] if n_out == 1 else own
    return own if comm is None else (own, got)


def _rms_hat(x):
    r = lax.rsqrt(jnp.mean(x * x, axis=-1, keepdims=True) + RMS_EPS)
    return x * r, r


def _rms_fwd(xs, gains, *, name, comm=None):
    S = xs[0].shape[0]
    widths = [x.shape[1] for x in xs]
    tm = _pick(S, (512, 256, 128))
    n = len(xs)
    grid = (S // tm,)
    hosted = _Hosted(comm, n_in=2 * n, n_out=1, n_scratch=0)

    def body(*refs):
        refs = hosted.begin(refs, grid)
        o_ref = refs[2 * n]
        off = 0
        for i in range(n):
            xh, _ = _rms_hat(refs[i][...])
            o_ref[:, off:off + widths[i]] = (xh * refs[n + i][...]).astype(o_ref.dtype)
            off += widths[i]
        hosted.end(grid)

    out, *got = pl.pallas_call(
        body, name=name, grid=grid,
        out_shape=[jax.ShapeDtypeStruct((S, sum(widths)), BF16)] + hosted.out_shapes,
        in_specs=[pl.BlockSpec((tm, w), lambda i: (i, 0)) for w in widths]
        + [pl.BlockSpec((1, w), lambda i: (0, 0)) for w in widths] + hosted.in_specs,
        out_specs=[pl.BlockSpec((tm, sum(widths)), lambda i: (i, 0))] + hosted.out_specs,
        scratch_shapes=hosted.scratch,
        compiler_params=_cparams(hosted.semantics(("parallel",))),
    )(*xs, *gains, *hosted.operands)
    return out if comm is None else (out, got)


def _sigmoid(g):
    return 1.0 / (1.0 + jnp.exp(-g))


def _ride(result, plan):
    return result if plan is not None else (result, None)


def _residual_then_norm(alpha):
    def epilogue(acc, res, gain):
        y = res + alpha * acc
        return y, _rms_hat(y)[0] * gain
    return epilogue


def _ffn_fwd(x, gain, w, *, tag, ex, first_rider=None, riders=(None, None, None), head=None, h=None,
             next_gain=None):
    if h is None:
        plan = ex.gather(first_rider)
        h, got = _ride(_rms_fwd([x], [gain], name=f"{tag}_norm", comm=plan), plan)
        ex.gathered(first_rider, got, w)
    def act(acc, _, g_blk):
        gf = g_blk.astype(F32)
        return acc, gf * _sigmoid(gf) * acc

    if riders[0] is None and riders[1] is None and f"{tag}_w_up" in w:
        def gate_up_act(accs, _):
            g_blk = accs[0].astype(BF16)
            return (g_blk,) + act(accs[1], None, g_blk)

        g, u, a = _mm(h, w[f"{tag}_w_gate"], tb=True, twin_b=w[f"{tag}_w_up"], outs=(BF16, BF16, BF16),
                      epilogue=gate_up_act, tm=256, name=f"{tag}_gate_up_act")
    else:
        plan = ex.gather(riders[0])
        g, got = _ride(_mm(h, w[f"{tag}_w_gate"], tb=True, outs=(BF16,), name=f"{tag}_gate", comm=plan), plan)
        ex.gathered(riders[0], got, w)
        plan = ex.gather(riders[1])
        (u, a), got = _ride(_mm(h, w[f"{tag}_w_up"], tb=True, outs=(BF16, BF16), extras=(g,), epilogue=act,
                                name=f"{tag}_up_act", comm=plan), plan)
        ex.gathered(riders[1], got, w)
    plan = ex.gather(riders[2])
    if head is None and next_gain is None:
        y, got = _ride(_mm(a, w[f"{tag}_w_down"], res=x, alpha=HALF_STEP, name=f"{tag}_down", comm=plan), plan)
    elif head is None:
        y, got = _ride(_mm(a, w[f"{tag}_w_down"], res=x, rows=(next_gain,), outs=(F32, BF16),
                           epilogue=_residual_then_norm(HALF_STEP), name=f"{tag}_down_norm", comm=plan), plan)
    else:
        final_gain, target = head
        y, got = _ride(_mm(a, w[f"{tag}_w_down"], res=x, extras=(target,), rows=(final_gain,), row_sums=2,
                           outs=(F32, BF16), epilogue=_loss_head_epilogue, name=f"{tag}_down_loss", comm=plan), plan)
    ex.gathered(riders[2], got, w)
    return y, (h, g, u, a)


def _loss_head_epilogue(acc, x_in, target, gain):
    xh, r = _rms_hat(x_in + HALF_STEP * acc)
    err = xh * gain - target
    dy = err * (1.0 / acc.shape[1])
    dxh = dy * gain
    dx = r * (dxh - xh * jnp.mean(dxh * xh, axis=-1, keepdims=True))
    loss = 0.5 * jnp.sum(jnp.mean(err * err, axis=-1, keepdims=True), axis=0, keepdims=True)
    return dx, dx.astype(BF16), jnp.sum(dy * xh, axis=0, keepdims=True), jnp.zeros_like(gain) + loss


def _rms_bwd_epilogue(acc, dh_so_far, x, *dres_and_gain):
    gain = dres_and_gain[-1]
    dh = acc if dh_so_far is None else acc + dh_so_far
    xh, r = _rms_hat(x)
    dxh = dh * gain
    dx = r * (dxh - xh * jnp.mean(dxh * xh, axis=-1, keepdims=True))
    if len(dres_and_gain) == 2:
        dx = dx + dres_and_gain[0]
    return dx, jnp.sum(dh * xh, axis=0, keepdims=True)


def _rms_bwd_epilogue_twice(*args):
    dx, dgain = _rms_bwd_epilogue(*args)
    return dx, dx.astype(BF16), dgain


def _ffn_bwd(dout, x, gain, w, saved, *, tag, ex, rider=(None, None), spread=False, dout_op=None, twice=False):
    h, g, u, a = saved
    wg, wu, wd = (w[f"{tag}_w_{n}"] for n in ("gate", "up", "down"))

    def act_bwd(acc, _, g_blk, u_blk):
        gf, uf = g_blk.astype(F32), u_blk.astype(F32)
        da = acc * HALF_STEP
        sig = _sigmoid(gf)
        silu = gf * sig
        return da * uf * (sig + silu * (1.0 - sig)), da * silu

    def carrying(group, grad, call):
        group = group if spread else None
        plan = ex.send(group, {group: grad})
        out, got = _ride(call(plan), plan)
        ex.received(group, got)
        return out

    plan = ex.send(*rider)
    dout_op = dout if dout_op is None else dout_op
    (dg, du), got = _ride(_mm(dout_op, wd, tb=True, outs=(BF16, BF16), extras=(g, u), epilogue=act_bwd,
                              ring=True, name=f"{tag}_bwd_act", comm=plan), plan)
    ex.received(rider[0], got)
    dwg = _mm(h, dg, ta=True, outs=(GRAD_WIRE,), name=f"{tag}_dwg")
    dwu = carrying(f"{tag}_w_gate", dwg, lambda plan: _mm(h, du, ta=True, outs=(GRAD_WIRE,), name=f"{tag}_dwu", comm=plan))
    dwd = carrying(f"{tag}_w_up", dwu,
                   lambda plan: _mm(a, dout_op, ta=True, outs=(GRAD_WIRE,), alpha=HALF_STEP, name=f"{tag}_dwd", comm=plan))
    dx, *dx_op, dgain = carrying(f"{tag}_w_down", dwd, lambda plan: _mm(
        dg, wg, second=(du, wu), extras=(x, dout), rows=(gain,), row_sums=1,
        outs=(F32, BF16) if twice else (F32,), epilogue=_rms_bwd_epilogue_twice if twice else _rms_bwd_epilogue,
        tm=512, name=f"{tag}_dh_norm_bwd", comm=plan))
    return ((dx, dx_op[0]) if twice else dx), dgain, dwg, dwu, dwd


def _rope_tables(S):
    half = HEAD_DIM // 2
    inv_freq = ROPE_THETA ** (-jnp.arange(half, dtype=F32) / half)
    ang = jnp.arange(S, dtype=F32)[:, None] * inv_freq[None, :]
    cos, sin = jnp.cos(ang), jnp.sin(ang)
    reps = LANES // HEAD_DIM
    cos_t = jnp.tile(jnp.concatenate([cos, cos], axis=1), (1, reps))
    sin_t = jnp.tile(jnp.concatenate([-sin, sin], axis=1), (1, reps))
    return cos_t, sin_t


def _rotate(v, cos, sin, sign):
    half = HEAD_DIM // 2
    groups = []
    for g in range(v.shape[1] // LANES):
        t = v[:, g * LANES:(g + 1) * LANES]
        lane = lax.broadcasted_iota(jnp.int32, t.shape, 1)
        swapped = jnp.where(lane % HEAD_DIM < half, pltpu.roll(t, LANES - half, axis=1), pltpu.roll(t, half, axis=1))
        groups.append(t * cos + swapped * (sin * sign))
    return groups[0] if len(groups) == 1 else jnp.concatenate(groups, axis=1)


def _join_d_proj(pieces, rotated, cos_t, sin_t, *, name):
    S = pieces[0].shape[0]
    widths = [p.shape[1] for p in pieces]
    tm = _pick(S, (256, 128))
    n = len(pieces)

    def body(*refs):
        c_ref, s_ref, o_ref = refs[n], refs[n + 1], refs[n + 2]
        off = 0
        for i in range(n):
            v = refs[i][...]
            if i in rotated:
                v = _rotate(v, c_ref[...], s_ref[...], -1.0)
            o_ref[:, off:off + widths[i]] = v.astype(o_ref.dtype)
            off += widths[i]

    return pl.pallas_call(
        body, name=name, grid=(S // tm,),
        out_shape=jax.ShapeDtypeStruct((S, sum(widths)), BF16),
        in_specs=[pl.BlockSpec((tm, w), lambda i: (i, 0)) for w in widths]
        + [pl.BlockSpec((tm, LANES), lambda i: (i, 0))] * 2,
        out_specs=pl.BlockSpec((tm, sum(widths)), lambda i: (i, 0)),
        compiler_params=_cparams(("parallel",)),
    )(*pieces, cos_t, sin_t)


def _head_masks(shape):
    lane = lax.broadcasted_iota(jnp.int32, shape, 1)
    return [(lane >= HEAD_DIM * h) & (lane < HEAD_DIM * (h + 1)) for h in range(shape[1] // HEAD_DIM)]


def _sb_scores(q2, k_j):
    z = lax.dot_general(q2, k_j, (((1,), (1,)), ((), ())), preferred_element_type=F32)
    sign_bit = jnp.int32(-2 ** 31)
    minus_abs = lax.bitcast_convert_type(lax.bitcast_convert_type(z, jnp.int32) | sign_bit, F32)
    softplus = jnp.maximum(z, 0.0) + jnp.log(1.0 + jnp.exp(minus_abs))
    return z - softplus, softplus


def _sb_stack_heads(t, scale=None):
    parts = [jnp.where(hm, t, jnp.zeros_like(t)) for hm in _head_masks(t.shape)]
    t2 = jnp.concatenate(parts, axis=0)
    if scale is not None:
        t2 = (t2.astype(F32) * scale).astype(t2.dtype)
    return t2


def _sb_unstack_heads(t2):
    n = t2.shape[1] // HEAD_DIM
    T = t2.shape[0] // n
    masks = _head_masks((T, t2.shape[1]))
    out = t2[:T]
    for h in range(1, n):
        out = jnp.where(masks[h], t2[h * T:(h + 1) * T], out)
    return out


def _sb_causal(T, n_heads):
    row = lax.broadcasted_iota(jnp.int32, (n_heads * T, T), 0)
    col = lax.broadcasted_iota(jnp.int32, (n_heads * T, T), 1)
    return col < row % T


def _sb_triangle(T, later):
    row = lax.broadcasted_iota(jnp.int32, (T, T), 0)
    col = lax.broadcasted_iota(jnp.int32, (T, T), 1)
    return ((row > col) if later else (row < col)).astype(BF16)


def _sb_fwd(p_sb, *, name, comm=None):
    S = p_sb.shape[0]
    W = p_sb.shape[1] // 3
    LW = min(SB_LANES, W)
    NH = LW // HEAD_DIM
    npair = W // LW
    T = SB_TILE
    n_tiles = S // T
    assert n_tiles <= HEAD_DIM
    scale = HEAD_DIM ** -0.5

    R = SB_STEP_TILES
    grid = (npair, n_tiles // R)
    hosted = _Hosted(comm, n_in=5, n_out=2, n_scratch=0)

    def body(*refs):
        refs = hosted.begin(refs, grid)
        step = pl.program_id(1)
        lax.fori_loop(0, R, lambda sub, _: query_tile(step * R + sub, sub, *refs), 0)
        hosted.end(grid)

    def query_tile(I, sub, q_ref, k_ref, v_ref, causal_ref, later_ref, o_ref, c_ref):
        rows = pl.ds(pl.multiple_of(sub * T, T), T)
        lane = lax.broadcasted_iota(jnp.int32, (T, LW), 1)
        causal = causal_ref[...]
        later_than = later_ref[...]
        q2 = _sb_stack_heads(q_ref[rows, :], scale)

        def scores(J, diag):
            off = pl.multiple_of(J * T, T)
            log_beta, stay = _sb_scores(q2, k_ref[pl.ds(off, T), :])
            if diag:
                stay = stay * causal
            local = jnp.dot(stay.astype(BF16), later_than, preferred_element_type=F32)
            return log_beta, local, jnp.sum(stay, axis=1, keepdims=True), v_ref[pl.ds(off, T), :]

        def weigh(J, sc, gone, acc, carr, diag):
            log_beta, local, _, v_j = sc
            w = jnp.exp((log_beta - gone) - local)
            if diag:
                w = w * causal
            acc = acc + jnp.dot(w.astype(BF16), v_j, preferred_element_type=F32)
            for h in range(NH):
                carr = jnp.where(lane == HEAD_DIM * h + J, -gone[h * T:(h + 1) * T], carr)
            return acc, carr

        def tiles(J, count, state, diag):
            gone, acc, carr, _ = state
            scs = [scores(J - u, diag and u == 0) for u in range(count)]
            for u, sc in enumerate(scs):
                acc, carr = weigh(J - u, sc, gone, acc, carr, diag and u == 0)
                gone = gone + sc[2]
            return gone, acc, carr, jnp.min(gone)

        U = SB_UNROLL
        alive = lambda st: st[3] < SB_DEAD
        state = (jnp.zeros((NH * T, 1), F32), jnp.zeros((NH * T, LW), F32),
                 jnp.full((T, LW), SB_UNSEEN, F32), jnp.zeros((), F32))
        state = lax.cond(I > 0, lambda st: tiles(I, 2, st, True), lambda st: tiles(I, 1, st, True), state)
        rest = jnp.maximum(I - 1, 0)
        singles = jnp.where(rest > 0, (rest - 1) % U + 1, 0)
        _, state = lax.while_loop(lambda c: (c[0] < singles) & alive(c[1]),
                                  lambda c: (c[0] + 1, tiles(I - 2 - c[0], 1, c[1], False)), (jnp.int32(0), state))
        blocks = (rest - singles) // U
        _, state = lax.while_loop(lambda c: (c[0] < blocks) & alive(c[1]),
                                  lambda c: (c[0] + 1, tiles(I - 2 - singles - U * c[0], U, c[1], False)),
                                  (jnp.int32(0), state))
        _, acc, carr, _ = state
        o_ref[rows, :] = _sb_unstack_heads(acc)
        c_ref[rows, :] = carr
        return 0

    blk = lambda I_off: pl.BlockSpec((R * T, LW), lambda p, I: (I, I_off + p))
    full = lambda off: pl.BlockSpec((S, LW), lambda p, I: (0, off + p))
    const = lambda rows: pl.BlockSpec((rows, T), lambda p, I: (0, 0))
    o, carries, *got = pl.pallas_call(
        body, name=name, grid=grid,
        out_shape=[jax.ShapeDtypeStruct((S, W), F32), jax.ShapeDtypeStruct((S, W), F32)] + hosted.out_shapes,
        in_specs=[blk(0), full(npair), full(2 * npair), const(NH * T), const(T)] + hosted.in_specs,
        out_specs=[blk(0), blk(0)] + hosted.out_specs,
        scratch_shapes=hosted.scratch,
        compiler_params=_cparams(hosted.semantics(("parallel", "arbitrary")), vmem_mb=56),
    )(p_sb, p_sb, p_sb, _sb_causal(T, NH).astype(F32), _sb_triangle(T, True), *hosted.operands)
    return (o, carries) if comm is None else (o, carries, got)


def _sb_bwd(p_sb, do, carries, *, name, comm=None):
    S = p_sb.shape[0]
    W = p_sb.shape[1] // 3
    LW = min(LANES, W)
    NH = LW // HEAD_DIM
    npair = W // LW
    T = SB_TILE
    n_tiles = S // T
    scale = HEAD_DIM ** -0.5

    R = SB_STEP_TILES
    grid = (npair, n_tiles // R)
    hosted = _Hosted(comm, n_in=8, n_out=3, n_scratch=0)

    def body(*refs):
        refs = hosted.begin(refs, grid)
        dk_ref, dv_ref = refs[9], refs[10]
        step = pl.program_id(1)

        @pl.when(step == 0)
        def _():
            dk_ref[...] = jnp.zeros_like(dk_ref)
            dv_ref[...] = jnp.zeros_like(dv_ref)

        lax.fori_loop(0, R, lambda sub, _: query_tile(step * R + sub, sub, *refs), 0)
        hosted.end(grid)

    def query_tile(I, sub, q_ref, k_ref, v_ref, do_ref, c_ref, causal_ref, later_ref, earlier_ref,
                   dq_ref, dk_ref, dv_ref):
        rows = pl.ds(pl.multiple_of(sub * T, T), T)
        lane = lax.broadcasted_iota(jnp.int32, (T, LW), 1)
        causal = causal_ref[...]
        later_than = later_ref[...]
        earlier_than = earlier_ref[...]
        q2 = _sb_stack_heads(q_ref[rows, :], scale)
        do2 = _sb_stack_heads(do_ref[rows, :].astype(BF16))
        carr = c_ref[rows, :]
        tn_dims = (((0,), (0,)), ((), ()))

        def chain(J, diag):
            off = pl.multiple_of(J * T, T)
            k_j = k_ref[pl.ds(off, T), :]
            v_j = v_ref[pl.ds(off, T), :]
            log_beta, stay = _sb_scores(q2, k_j)
            if diag:
                stay = stay * causal
            lc = jnp.concatenate(
                [jnp.sum(jnp.where(lane == HEAD_DIM * h + J, carr, 0.0), axis=1, keepdims=True) for h in range(NH)],
                axis=0)
            w = jnp.exp((log_beta + lc) - jnp.dot(stay.astype(BF16), later_than, preferred_element_type=F32))
            if diag:
                w = w * causal
            dw = lax.dot_general(do2, v_j, (((1,), (1,)), ((), ())), preferred_element_type=F32)
            e = w * dw
            local = jnp.dot(e.astype(BF16), earlier_than, preferred_element_type=F32)
            return off, k_j, w, e, local, jnp.exp(log_beta), jnp.sum(e, axis=1, keepdims=True)

        def finish(ch, ec, dq_acc, diag):
            off, k_j, w, e, local, beta, _ = ch
            e_before = local + ec
            dz = e - beta * (e + e_before)
            if diag:
                dz = dz * causal
            dzb = dz.astype(BF16)
            dq_acc = dq_acc + jnp.dot(dzb, k_j, preferred_element_type=F32)
            dk_ref[pl.ds(off, T), :] += lax.dot_general(dzb, q2, tn_dims, preferred_element_type=F32)
            dv_ref[pl.ds(off, T), :] += lax.dot_general(w.astype(BF16), do2, tn_dims, preferred_element_type=F32)
            return dq_acc

        def tiles(J, count, state, diag):
            ec, dq_acc = state
            chains = [chain(J + u, diag and u == count - 1) for u in range(count)]
            for u, ch in enumerate(chains):
                dq_acc = finish(ch, ec, dq_acc, diag and u == count - 1)
                ec = ec + ch[6]
            return ec, dq_acc

        lane_row = lax.broadcasted_iota(jnp.int32, (1, LW), 1)
        reached = (jnp.max(carr, axis=0, keepdims=True) > 0.5 * SB_UNSEEN) & (lane_row < HEAD_DIM)
        first = jnp.min(jnp.where(reached, lane_row.astype(F32), float(n_tiles))).astype(jnp.int32)
        U = SB_UNROLL
        count = I - first
        rest = jnp.maximum(count - 1, 0)
        state = (jnp.zeros((NH * T, 1), F32), jnp.zeros((NH * T, LW), F32))
        state = lax.fori_loop(0, rest // U, lambda jj, st: tiles(first + U * jj, U, st, False), state)
        state = lax.fori_loop(0, rest % U, lambda r, st: tiles(I - 1 - rest % U + r, 1, st, False), state)
        _, dq_acc = lax.cond(count > 0, lambda st: tiles(I - 1, 2, st, True), lambda st: tiles(I, 1, st, True), state)
        dq_ref[rows, :] = _sb_unstack_heads(dq_acc) * scale
        return 0

    blk = lambda src_off: pl.BlockSpec((R * T, LW), lambda p, I: (I, src_off + p))
    full = lambda off: pl.BlockSpec((S, LW), lambda p, I: (0, off + p))
    const = lambda rows: pl.BlockSpec((rows, T), lambda p, I: (0, 0))
    dq, dk, dv, *got = pl.pallas_call(
        body, name=name, grid=grid,
        out_shape=[jax.ShapeDtypeStruct((S, W), F32)] * 3 + hosted.out_shapes,
        in_specs=[blk(0), full(npair), full(2 * npair), blk(0), blk(0), const(NH * T), const(T), const(T)]
        + hosted.in_specs,
        out_specs=[blk(0), full(0), full(0)] + hosted.out_specs,
        scratch_shapes=hosted.scratch,
        compiler_params=_cparams(hosted.semantics(("parallel", "arbitrary")), vmem_mb=56),
    )(p_sb, p_sb, p_sb, do, carries, _sb_causal(T, NH).astype(F32), _sb_triangle(T, True), _sb_triangle(T, False),
      *hosted.operands)
    return (dq, dk, dv) if comm is None else (dq, dk, dv, got)


def _dil_blocks(b, body_fn):
    for pi, (window, dil) in enumerate(DILATED_PATTERNS):
        assert window // dil == DIL_BLOCK
        nblk = DIL_SUPER // (DIL_BLOCK * dil)
        assert (dil * nblk) % DIL_UNROLL == 0

        def group(g, _, pi=pi, dil=dil, nblk=nblk):
            for u in range(DIL_UNROLL):
                t = g * DIL_UNROLL + u
                n = t % nblk
                body_fn(pi, dil, t // nblk, n, b * nblk + n)
            return 0

        lax.fori_loop(0, dil * nblk // DIL_UNROLL, group, 0)


def _dil_rows(start, size, dil):
    if dil == 1:
        return pl.ds(pl.multiple_of(start, DIL_BLOCK), size)
    return pl.ds(start, size, stride=dil)


def _dil_fill_bias(bias_ref):
    row = lax.broadcasted_iota(jnp.int32, (2 * DIL_BLOCK, 2 * DIL_BLOCK), 0)
    kk = lax.broadcasted_iota(jnp.int32, (2 * DIL_BLOCK, 2 * DIL_BLOCK), 1)
    qi = jnp.where(row >= DIL_BLOCK, row - DIL_BLOCK, row)
    for s in range(2):
        dist = s * DIL_BLOCK + qi - kk
        bias_ref[s] = jnp.where((dist >= 0) & (dist <= DIL_BLOCK), 0.0, NEG_BIG)


def _dl_fwd(p_dl, *, name):
    S, W = p_dl.shape[0], p_dl.shape[1] // 3
    npair = W // LANES
    nsuper = S // DIL_SUPER
    assert S % DIL_SUPER == 0 and S // max(d for _, d in DILATED_PATTERNS) >= 2 * DIL_BLOCK
    scale = HEAD_DIM ** -0.5
    npat = len(DILATED_PATTERNS)

    def body(q_ref, k_ref, v_ref, o_ref, l_ref, bias_ref, *pattern_refs):
        op_refs, lp_refs = pattern_refs[:npat], pattern_refs[npat:]
        b = pl.program_id(1)
        masks = _head_masks((DIL_BLOCK, LANES))
        pl.when(b == 0)(lambda: _dil_fill_bias(bias_ref))

        def block(pi, dil, c, n, gn):
            ws = jnp.maximum(gn - 1, 0)
            qrows = n * (DIL_BLOCK * dil) + c
            krows = ws * (DIL_BLOCK * dil) + c
            q_idx = _dil_rows(qrows, DIL_BLOCK, dil)
            k_idx = _dil_rows(krows, 2 * DIL_BLOCK, dil)
            qb = q_ref[q_idx, :]
            kb = k_ref[k_idx, :].astype(BF16)
            vb = v_ref[k_idx, :].astype(BF16)
            q2 = _sb_stack_heads(qb.astype(BF16), scale)
            z = lax.dot_general(q2, kb, (((1,), (1,)), ((), ())), preferred_element_type=F32) + bias_ref[gn - ws]
            m = jnp.max(z, axis=1, keepdims=True)
            p = jnp.exp(z - m)
            den = jnp.sum(p, axis=1, keepdims=True)
            acc = jnp.dot(p.astype(BF16), vb, preferred_element_type=F32)
            lse = m + jnp.log(den)
            op_refs[pi][q_idx, :] = _sb_unstack_heads(acc / den)
            lp_refs[pi][q_idx, :] = jnp.where(masks[0], lse[:DIL_BLOCK], lse[DIL_BLOCK:])

        _dil_blocks(b, block)
        lses = [r[...] for r in lp_refs]
        top = functools.reduce(jnp.maximum, lses)
        ws_ = [jnp.exp(l - top) for l in lses]
        den = functools.reduce(jnp.add, ws_)
        num = functools.reduce(jnp.add, [w * r[...] for r, w in zip(op_refs, ws_)])
        o_ref[...] = num / den
        l_ref[...] = top + jnp.log(den)

    blk = pl.BlockSpec((DIL_SUPER, LANES), lambda p, b: (b, p))
    full = lambda off: pl.BlockSpec((S, LANES), lambda p, b: (0, off + p))
    return pl.pallas_call(
        body, name=name, grid=(npair, nsuper),
        out_shape=[jax.ShapeDtypeStruct((S, W), F32)] * 2,
        in_specs=[blk, full(npair), full(2 * npair)], out_specs=[blk, blk],
        scratch_shapes=[pltpu.VMEM((2, 2 * DIL_BLOCK, 2 * DIL_BLOCK), F32)]
        + [pltpu.VMEM((DIL_SUPER, LANES), F32)] * (2 * npat),
        compiler_params=_cparams(("arbitrary", "arbitrary")),
    )(p_dl, p_dl, p_dl)


def _dl_bwd(p_dl, o, lse, do, *, name):
    S, W = p_dl.shape[0], p_dl.shape[1] // 3
    npair = W // LANES
    nsuper = S // DIL_SUPER
    scale = HEAD_DIM ** -0.5

    def body(q_ref, k_ref, v_ref, o_ref, l_ref, do_ref, dq_ref, dk_ref, dv_ref, delta_ref, bias_ref):
        b = pl.program_id(1)

        @pl.when(b == 0)
        def _():
            dk_ref[...] = jnp.zeros_like(dk_ref)
            dv_ref[...] = jnp.zeros_like(dv_ref)
            _dil_fill_bias(bias_ref)

        dq_ref[...] = jnp.zeros_like(dq_ref)
        prod = do_ref[...] * o_ref[...]
        delta = jnp.zeros_like(prod)
        for hm in _head_masks(prod.shape):
            delta = jnp.where(hm, jnp.sum(jnp.where(hm, prod, 0.0), axis=1, keepdims=True), delta)
        delta_ref[...] = delta

        def block(pi, dil, c, n, gn):
            ws = jnp.maximum(gn - 1, 0)
            qrows = n * (DIL_BLOCK * dil) + c
            krows = ws * (DIL_BLOCK * dil) + c
            q_idx = _dil_rows(qrows, DIL_BLOCK, dil)
            k_idx = _dil_rows(krows, 2 * DIL_BLOCK, dil)
            qb = q_ref[q_idx, :]
            dob = do_ref[q_idx, :]
            lb = l_ref[q_idx, :]
            db = delta_ref[q_idx, :]
            kb = k_ref[k_idx, :].astype(BF16)
            vb = v_ref[k_idx, :].astype(BF16)
            q2 = _sb_stack_heads(qb.astype(BF16), scale)
            do2 = _sb_stack_heads(dob.astype(BF16))
            lse2 = jnp.concatenate([lb[:, HEAD_DIM * h:HEAD_DIM * h + 1] for h in range(2)], axis=0)
            delta2 = jnp.concatenate([db[:, HEAD_DIM * h:HEAD_DIM * h + 1] for h in range(2)], axis=0)
            z = lax.dot_general(q2, kb, (((1,), (1,)), ((), ())), preferred_element_type=F32)
            p = jnp.exp((z + bias_ref[gn - ws]) - lse2)
            dp = lax.dot_general(do2, vb, (((1,), (1,)), ((), ())), preferred_element_type=F32)
            dzb = (p * (dp - delta2)).astype(BF16)
            tn_dims = (((0,), (0,)), ((), ()))
            dq_blk = _sb_unstack_heads(jnp.dot(dzb, kb, preferred_element_type=F32)) * scale
            dk_blk = lax.dot_general(dzb, q2, tn_dims, preferred_element_type=F32)
            dv_blk = lax.dot_general(p.astype(BF16), do2, tn_dims, preferred_element_type=F32)
            dq_ref[q_idx, :] = dq_ref[q_idx, :] + dq_blk
            dk_ref[k_idx, :] = dk_ref[k_idx, :] + dk_blk
            dv_ref[k_idx, :] = dv_ref[k_idx, :] + dv_blk

        _dil_blocks(b, block)

    blk = pl.BlockSpec((DIL_SUPER, LANES), lambda p, b: (b, p))
    full = lambda off: pl.BlockSpec((S, LANES), lambda p, b: (0, off + p))
    return pl.pallas_call(
        body, name=name, grid=(npair, nsuper),
        out_shape=[jax.ShapeDtypeStruct((S, W), F32)] * 3,
        in_specs=[blk, full(npair), full(2 * npair), blk, blk, blk], out_specs=[blk, full(0), full(0)],
        scratch_shapes=[pltpu.VMEM((DIL_SUPER, LANES), F32), pltpu.VMEM((2, 2 * DIL_BLOCK, 2 * DIL_BLOCK), F32)],
        compiler_params=_cparams(("arbitrary", "arbitrary")),
    )(p_dl, p_dl, p_dl, o, lse, do)


class _NoExchange:
    def gather(self, family):
        return None

    def gathered(self, family, got, weights):
        pass

    def send(self, family, grads):
        return None

    def received(self, family, got):
        pass


def _local_step(x, target, gains, weights, exchanges=None):
    S, D = x.shape
    ex = exchanges or _NoExchange()
    weights = dict(weights)
    d_sb = gains["sb_out_norm"].shape[1]
    d_dl = gains["dil_out_norm"].shape[1]
    cos_t, sin_t = _rope_tables(S)

    riders = ("ffn1_w_up", "ffn1_w_down", "mixer") if exchanges else (None, None, None)
    (x1, h2), saved1 = _ffn_fwd(x, gains["ffn1_norm"], weights, tag="ffn1", ex=ex, riders=riders,
                                first_rider="ffn1_w_gate" if exchanges else None,
                                next_gain=gains["mix_norm"])
    w_in = weights["w_in"]
    w_out = weights["w_out"]
    p_sb = _mm(h2, w_in, tb=True, b_cols=(0, 3 * d_sb), outs=(BF16,), name="proj_sb")

    def rope_qk(acc, _, cos, sin):
        return jnp.concatenate([_rotate(acc[:, :2 * d_dl], cos, sin, 1.0), acc[:, 2 * d_dl:]], axis=1)

    p_dl = _mm(h2, w_in, tb=True, b_cols=(3 * d_sb, 3 * d_dl), lanes=(cos_t, sin_t), epilogue=rope_qk,
               name="proj_dl_rope")
    plan = ex.gather("ffn2" if exchanges else None)
    o_sb, carries, *got = _sb_fwd(p_sb, name="sb_fwd", comm=plan)
    ex.gathered("ffn2", got[0] if got else None, weights)
    o_dl, lse_dl = _dl_fwd(p_dl, name="dl_fwd")
    merged = _rms_fwd([o_sb, o_dl], [gains["sb_out_norm"], gains["dil_out_norm"]], name="out_norm")
    x2, h3 = _mm(merged, w_out, res=x1, rows=(gains["ffn2_norm"],), outs=(F32, BF16),
                 epilogue=_residual_then_norm(1.0), name="out_proj_norm")
    (dx3, dx3_op, d_final, loss_wide), saved2 = _ffn_fwd(x2, gains["ffn2_norm"], weights, tag="ffn2", ex=ex, h=h3,
                                                         head=(gains["final_norm"], target))
    loss_row = loss_wide[:, :LANES]

    (dx2, dx2_op), d_ffn2_norm, dwg2, dwu2, dwd2 = _ffn_bwd(dx3, x2, gains["ffn2_norm"], weights, saved2, tag="ffn2",
                                                             ex=ex, dout_op=dx3_op, twice=True)
    d_w_out = _mm(merged, dx2_op, ta=True, outs=(GRAD_WIRE,), name="d_w_out")
    do_sb, d_sb_norm = _mm(dx2_op, w_out, tb=True, b_cols=(0, d_sb), extras=(o_sb,), rows=(gains["sb_out_norm"],),
                           row_sums=1, epilogue=_rms_bwd_epilogue, name="d_merged_sb")
    do_dl, d_dl_norm = _mm(dx2_op, w_out, tb=True, b_cols=(d_sb, d_dl), extras=(o_dl,), rows=(gains["dil_out_norm"],),
                           row_sums=1, epilogue=_rms_bwd_epilogue, name="d_merged_dl")
    plan = ex.send("ffn2", dict(ffn2_w_gate=dwg2, ffn2_w_up=dwu2, ffn2_w_down=dwd2))
    dq_sb, dk_sb, dv_sb, *got = _sb_bwd(p_sb, do_sb, carries, name="sb_bwd", comm=plan)
    ex.received("ffn2", got[0] if got else None)
    dq_dl, dk_dl, dv_dl = _dl_bwd(p_dl, o_dl, lse_dl, do_dl, name="dl_bwd")
    d_proj = _join_d_proj([dq_sb, dk_sb, dv_sb, dq_dl, dk_dl, dv_dl], (3, 4), cos_t, sin_t, name="d_proj")
    d_w_in = _mm(h2, d_proj, ta=True, outs=(GRAD_WIRE,), name="d_w_in")
    dx1, dx1_op, d_mix_norm = _mm(d_proj, w_in, extras=(x1, dx2), rows=(gains["mix_norm"],), row_sums=1,
                                  outs=(F32, BF16), epilogue=_rms_bwd_epilogue_twice, name="dh_mix_norm_bwd")
    dx, d_ffn1_norm, dwg1, dwu1, dwd1 = _ffn_bwd(
        dx1, x, gains["ffn1_norm"], weights, saved1, tag="ffn1", ex=ex, dout_op=dx1_op,
        rider=("mixer", dict(w_in=d_w_in, w_out=d_w_out)), spread=True)
    gain_grads = dict(ffn1_norm=d_ffn1_norm, mix_norm=d_mix_norm, sb_out_norm=d_sb_norm, dil_out_norm=d_dl_norm,
                      ffn2_norm=d_ffn2_norm, final_norm=d_final)
    weight_grads = dict(ffn1_w_gate=dwg1, ffn1_w_up=dwu1, ffn1_w_down=dwd1, w_in=d_w_in, w_out=d_w_out,
                        ffn2_w_gate=dwg2, ffn2_w_up=dwu2, ffn2_w_down=dwd2)
    return loss_row, dx, gain_grads, weight_grads


def _mesh_position():
    return lax.axis_index("x"), lax.axis_index("y"), lax.axis_index("c")


def _flip(coord, bit):
    return 1 - coord if bit else coord


RELATIONS = [(rx, ry, rc) for rx in (0, 1) for ry in (0, 1) for rc in (0, 1)][1:]


class _GatherPlan:
    def __init__(self, shards):
        n = len(shards)
        self.operands = list(shards)
        self.out_shapes = [jax.ShapeDtypeStruct((N_DEV,) + s.shape, s.dtype) for s in shards]
        self.scratch = [pltpu.SemaphoreType.DMA((n, 7)), pltpu.SemaphoreType.DMA((n, 7)),
                        pltpu.SemaphoreType.DMA((n,))]

    def _copies(self, in_refs, out_refs, sems):
        send_sems, recv_sems, local_sems = sems
        x, y, c = _mesh_position()
        me, sibling = (x, y, c), (x, y, 1 - c)
        chips = [(1 - x, y), (x, 1 - y), (1 - x, 1 - y)]
        plans = []
        for t, (x_ref, out_ref) in enumerate(zip(in_refs, out_refs)):
            def slot(px, py, pc, out_ref=out_ref):
                return out_ref.at[4 * px + 2 * py + pc]

            def copy(k, block, to, src=None, t=t, slot=slot):
                return pltpu.make_async_remote_copy(
                    src_ref=slot(*block) if src is None else src, dst_ref=slot(*block),
                    send_sem=send_sems.at[t, k], recv_sem=recv_sems.at[t, k],
                    device_id=to, device_id_type=pl.DeviceIdType.MESH)

            plans.append(dict(
                mine=pltpu.make_async_copy(x_ref, slot(*me), local_sems.at[t]),
                first=[copy(0, me, sibling, src=x_ref)]
                + [copy(1 + j, me, (*chip, c), src=x_ref) for j, chip in enumerate(chips)],
                over_ici=[copy(1 + j, (*chip, c), me) for j, chip in enumerate(chips)],
                passed=[copy(4 + j, (*chip, c), sibling) for j, chip in enumerate(chips)],
                from_sibling=[copy(0, sibling, me)] + [copy(4 + j, (*chip, 1 - c), me) for j, chip in enumerate(chips)]))
        return plans

    def start(self, in_refs, out_refs, sems):
        for p in self._copies(in_refs, out_refs, sems):
            p["mine"].start()
            for cp in p["first"]:
                cp.start()

    def finish(self, in_refs, out_refs, sems):
        plans = self._copies(in_refs, out_refs, sems)
        for p in plans:
            for arrived, onward in zip(p["over_ici"], p["passed"]):
                arrived.wait_recv()
                onward.start()
        for p in plans:
            for cp in p["from_sibling"]:
                cp.wait_recv()
            for cp in p["first"] + p["passed"]:
                cp.wait_send()
            p["mine"].wait()


class _Hosted:
    def __init__(self, plan, n_in, n_out, n_scratch):
        self.plan, self.n_in, self.n_out, self.n_scratch = plan, n_in, n_out, n_scratch
        self.operands = list(plan.operands) if plan else []
        self.out_shapes = list(plan.out_shapes) if plan else []
        self.scratch = list(plan.scratch) if plan else []
        self.in_specs = [pl.BlockSpec(memory_space=pl.ANY)] * len(self.operands)
        self.out_specs = [pl.BlockSpec(memory_space=pl.ANY)] * len(self.out_shapes)

    def semantics(self, sem):
        return sem if self.plan is None else ("arbitrary",) * len(sem)

    def _at(self, grid, last):
        hit = None
        for d, n in enumerate(grid):
            here = pl.program_id(d) == (n - 1 if last else 0)
            hit = here if hit is None else hit & here
        return hit

    def begin(self, refs, grid):
        if self.plan is None:
            return refs
        k_in, k_out = len(self.operands), len(self.out_shapes)
        ins, rest = refs[:self.n_in], refs[self.n_in:]
        c_in, rest = rest[:k_in], rest[k_in:]
        outs, rest = rest[:self.n_out], rest[self.n_out:]
        c_out, rest = rest[:k_out], rest[k_out:]
        scratch, sems = rest[:self.n_scratch], rest[self.n_scratch:]
        self._args = (c_in, c_out, sems)
        pl.when(self._at(grid, False))(lambda: self.plan.start(*self._args))
        return tuple(ins) + tuple(outs) + tuple(scratch)

    def end(self, grid):
        if self.plan is not None:
            pl.when(self._at(grid, True))(lambda: self.plan.finish(*self._args))


class _ExchangePlan:
    def __init__(self, packs):
        n = len(packs)
        self.operands = list(packs)
        self.out_shapes = [jax.ShapeDtypeStruct(p.shape, p.dtype) for p in packs]
        self.scratch = [pltpu.SemaphoreType.DMA((n, 7)), pltpu.SemaphoreType.DMA((n, 7)),
                        pltpu.SemaphoreType.DMA((n,))]

    def _copies(self, in_refs, out_refs, sems):
        send_sems, recv_sems, local_sems = sems
        x, y, c = _mesh_position()
        me = 4 * x + 2 * y + c
        copies = [pltpu.make_async_copy(i.at[me], o.at[me], local_sems.at[t])
                  for t, (i, o) in enumerate(zip(in_refs, out_refs))]
        for r, (rx, ry, rc) in enumerate(RELATIONS):
            px, py, pc = _flip(x, rx), _flip(y, ry), _flip(c, rc)
            peer = 4 * px + 2 * py + pc
            copies += [pltpu.make_async_remote_copy(
                src_ref=i.at[peer], dst_ref=o.at[me], send_sem=send_sems.at[t, r], recv_sem=recv_sems.at[t, r],
                device_id=(px, py, pc), device_id_type=pl.DeviceIdType.MESH)
                for t, (i, o) in enumerate(zip(in_refs, out_refs))]
        return copies

    def start(self, in_refs, out_refs, sems):
        for cp in self._copies(in_refs, out_refs, sems):
            cp.start()

    def finish(self, in_refs, out_refs, sems):
        for cp in self._copies(in_refs, out_refs, sems):
            cp.wait()


def _all_reduce_rows(v, *, name):
    R, C = v.shape

    def body(v_ref, out_ref, buf, send_sems, recv_sems):
        x, y, c = _mesh_position()
        me = 4 * x + 2 * y + c
        buf[me] = v_ref[...]
        copies = []
        for r, (rx, ry, rc) in enumerate(RELATIONS):
            cp = pltpu.make_async_remote_copy(
                src_ref=v_ref, dst_ref=buf.at[me], send_sem=send_sems.at[r], recv_sem=recv_sems.at[r],
                device_id=(_flip(x, rx), _flip(y, ry), _flip(c, rc)), device_id_type=pl.DeviceIdType.MESH)
            cp.start()
            copies.append(cp)
        for cp in copies:
            cp.wait()
        total = buf[0]
        for s in range(1, N_DEV):
            total = total + buf[s]
        out_ref[...] = total

    return pl.pallas_call(
        body, name=name,
        out_shape=jax.ShapeDtypeStruct((R, C), F32),
        in_specs=[pl.BlockSpec(memory_space=pltpu.VMEM)],
        out_specs=pl.BlockSpec(memory_space=pltpu.VMEM),
        scratch_shapes=[pltpu.VMEM((N_DEV, R, C), F32), pltpu.SemaphoreType.DMA((7,)), pltpu.SemaphoreType.DMA((7,))],
    )(v)


def _adamw(w, g, m, v, *, name):
    R, C = w.shape
    slots = g.ndim == 3
    tr = _pick(R, (256, 128, 64, 32, 16) if slots else (256, 128, 64, 32, 16, 8))

    def body(w_ref, g_ref, m_ref, v_ref, g_out, d_ref, nm_ref, nv_ref):
        if slots:
            g = g_ref[0].astype(F32)
            for s in range(1, N_DEV):
                g = g + g_ref[s].astype(F32)
        else:
            g = g_ref[...]
        g_out[...] = g
        m_new = ADAM_B1 * m_ref[...] + (1.0 - ADAM_B1) * g
        v_new = ADAM_B2 * v_ref[...] + (1.0 - ADAM_B2) * (g * g)
        m_hat = m_new / (1.0 - ADAM_B1 ** ADAM_STEP)
        v_hat = v_new / (1.0 - ADAM_B2 ** ADAM_STEP)
        d_ref[...] = -ADAM_LR * (m_hat / (jnp.sqrt(v_hat) + ADAM_EPS) + ADAM_WD * w_ref[...])
        nm_ref[...] = m_new
        nv_ref[...] = v_new

    spec = pl.BlockSpec((tr, C), lambda i: (i, 0))
    g_spec = pl.BlockSpec((N_DEV, tr, C), lambda i: (0, i, 0)) if slots else spec
    return pl.pallas_call(
        body, name=name, grid=(R // tr,),
        out_shape=[jax.ShapeDtypeStruct((R, C), F32)] * 4,
        in_specs=[spec, g_spec, spec, spec], out_specs=[spec] * 4,
        compiler_params=_cparams(("parallel",)),
    )(w, g, m, v)


WEIGHT_NAMES = ["ffn1_norm", "ffn1_w_gate", "ffn1_w_up", "ffn1_w_down", "mix_norm", "w_in", "sb_out_norm",
                "dil_out_norm", "w_out", "ffn2_norm", "ffn2_w_gate", "ffn2_w_up", "ffn2_w_down", "final_norm"]
GAIN_NAMES = ["ffn1_norm", "mix_norm", "sb_out_norm", "dil_out_norm", "ffn2_norm", "final_norm"]
COL_SHARDED = ["ffn1_w_gate", "ffn1_w_up", "ffn2_w_gate", "ffn2_w_up", "w_in"]
ROW_SHARDED = ["ffn1_w_down", "ffn2_w_down", "w_out"]
GROUPS = {"mixer": (["w_in"], ["w_out"]),
          "ffn2": (["ffn2_w_gate", "ffn2_w_up"], ["ffn2_w_down"])}
for _ffn in ("ffn1", "ffn2"):
    GROUPS.update({f"{_ffn}_w_gate": ([f"{_ffn}_w_gate"], []), f"{_ffn}_w_up": ([f"{_ffn}_w_up"], []),
                   f"{_ffn}_w_down": ([], [f"{_ffn}_w_down"])})


class _Exchanges:
    def __init__(self, params):
        self.params = params
        self.grads = {}

    def gather(self, group):
        if group is None:
            return None
        cols, rows = GROUPS[group]
        return _GatherPlan([self.params[n].T.astype(BF16) for n in cols] + [self.params[n].astype(BF16) for n in rows])

    def gathered(self, group, got, weights):
        if group is None:
            return
        cols, rows = GROUPS[group]
        for n, blocks in zip(cols + rows, got):
            weights[n] = blocks.reshape(N_DEV * blocks.shape[1], blocks.shape[2])

    def send(self, group, grads):
        if group is None:
            return None
        cols, rows = GROUPS[group]
        packs = [jnp.transpose(grads[n].reshape(grads[n].shape[0], N_DEV, self.params[n].shape[1]), (1, 0, 2))
                 for n in cols]
        packs += [grads[n].reshape(N_DEV, self.params[n].shape[0], grads[n].shape[1]) for n in rows]
        return _ExchangePlan([p.astype(GRAD_WIRE) for p in packs])

    def received(self, group, got):
        if group is None:
            return
        cols, rows = GROUPS[group]
        for n, slots in zip(cols + rows, got):
            self.grads[n] = slots


def _step(x, target, params, moments_m, moments_v):
    ex = _Exchanges(params)
    weights = {}
    gains = {n: params[n] for n in GAIN_NAMES}
    loss_row, grad_x, gain_grads, _ = _local_step(x, target, gains, weights, ex)
    grads = ex.grads

    rows = [gain_grads[n].reshape(-1, LANES) for n in GAIN_NAMES] + [loss_row]
    small = jnp.concatenate(rows, axis=0)
    pad = (-small.shape[0]) % 8
    small = jnp.pad(small, ((0, pad), (0, 0)))
    small = _all_reduce_rows(small, name="reduce_gains_loss")
    off = 0
    for n in GAIN_NAMES:
        r = gain_grads[n].shape[1] // LANES
        grads[n] = small[off:off + r].reshape(1, -1)
        off += r
    loss = small[off, 0]

    delta, new_m, new_v = {}, {}, {}
    for n in WEIGHT_NAMES:
        grads[n], delta[n], new_m[n], new_v[n] = _adamw(params[n], grads[n], moments_m[n], moments_v[n],
                                                        name=f"adamw_{n}")
    return loss, grad_x, grads, delta, new_m, new_v


def kernel(x, ffn1_norm, ffn1_w_gate, ffn1_w_up, ffn1_w_down, mix_norm, w_in, sb_out_norm, dil_out_norm, w_out, ffn2_norm, ffn2_w_gate, ffn2_w_up, ffn2_w_down, final_norm, loss_target, m_ffn1_norm, m_ffn1_w_gate, m_ffn1_w_up, m_ffn1_w_down, m_mix_norm, m_w_in, m_sb_out_norm, m_dil_out_norm, m_w_out, m_ffn2_norm, m_ffn2_w_gate, m_ffn2_w_up, m_ffn2_w_down, m_final_norm, v_ffn1_norm, v_ffn1_w_gate, v_ffn1_w_up, v_ffn1_w_down, v_mix_norm, v_w_in, v_sb_out_norm, v_dil_out_norm, v_w_out, v_ffn2_norm, v_ffn2_w_gate, v_ffn2_w_up, v_ffn2_w_down, v_final_norm):
    given = dict(locals())
    shapes = {n: given[n].shape for n in WEIGHT_NAMES}

    def as2d(a):
        return a.reshape(1, -1) if a.ndim == 1 else a.reshape(a.shape[-2], a.shape[-1])

    params = {n: as2d(given[n]) for n in WEIGHT_NAMES}
    moments_m = {n: as2d(given["m_" + n]) for n in WEIGHT_NAMES}
    moments_v = {n: as2d(given["v_" + n]) for n in WEIGHT_NAMES}
    loss, grad_x, grads, delta, new_m, new_v = _step(x[0], loss_target[0], params, moments_m, moments_v)
    back = lambda d: [d[n].reshape(shapes[n]) for n in WEIGHT_NAMES]
    return (loss, grad_x[None], *back(grads), *back(delta), *back(new_m), *back(new_v))
```
